```python
import math
import jax
import jax.numpy as jnp
from jax import lax
import numpy as np

D_MODEL = 2048
BATCH = 8
SEQ = 8192
DEPTH = 2

GRID_W = 64
N_MEM = 256
EPS = 1e-6

SSM_GROUP = 16
SSM_STATE = 64
SSM_GROUPS = 48
SSM_WIDTH = SSM_GROUPS * SSM_GROUP

DN_HEADS = 6
DN_HEAD_DIM = 128
DN_WIDTH = DN_HEADS * DN_HEAD_DIM
DN_CONV = 5
DN_CHUNK = 64

ATT_HEADS = 8
ATT_KV_HEADS = 2
ATT_HEAD_DIM = 128
ATT_WIDTH = ATT_HEADS * ATT_HEAD_DIM
ATT_KV_WIDTH = ATT_KV_HEADS * ATT_HEAD_DIM
ATT_BLOCK = 128
ROPE_THETA = 10000.0

MEM_HEADS = 4
MEM_HEAD_DIM = 128
MEM_WIDTH = MEM_HEADS * MEM_HEAD_DIM

N_BRANCH = 4
BRANCH_WIDTHS = (SSM_WIDTH, DN_WIDTH, ATT_WIDTH, MEM_WIDTH)
BRANCH_OFFSETS = (0, SSM_WIDTH, SSM_WIDTH + DN_WIDTH, SSM_WIDTH + DN_WIDTH + ATT_WIDTH)
BRANCH_TOTAL = SSM_WIDTH + DN_WIDTH + ATT_WIDTH + MEM_WIDTH

IN_SPLITS = (
    SSM_WIDTH, SSM_WIDTH,
    DN_WIDTH, DN_WIDTH, DN_WIDTH, 2 * DN_HEADS, 2 * DN_HEADS, DN_WIDTH,
    ATT_WIDTH, ATT_KV_WIDTH, ATT_KV_WIDTH, ATT_WIDTH,
    MEM_WIDTH, MEM_WIDTH,
    N_BRANCH * D_MODEL,
)
IN_WIDTH = (2 * SSM_WIDTH + 4 * DN_WIDTH + 4 * DN_HEADS + 2 * ATT_WIDTH
            + 2 * ATT_KV_WIDTH + 2 * MEM_WIDTH + N_BRANCH * D_MODEL)

kernel_name = "hybrid_gated_s5_deltanet_gridattn_encoder"


def rmsnorm(x, g):
    xf = x.astype(jnp.float32)
    y = xf * lax.rsqrt(jnp.mean(xf * xf, axis=-1, keepdims=True) + EPS)
    return (y * g.astype(jnp.float32)).astype(x.dtype)


def l2norm(x):
    return x * lax.rsqrt(jnp.sum(x * x, axis=-1, keepdims=True) + EPS)


def _cmul(ar, ai, br, bi):
    return ar * br - ai * bi, ar * bi + ai * br


def _ssm_combine(e1, e2):
    a1r, a1i, b1r, b1i = e1
    a2r, a2i, b2r, b2i = e2
    ar, ai = _cmul(a2r, a2i, a1r, a1i)
    br, bi = _cmul(a2r, a2i, b1r, b1i)
    return ar, ai, br + b2r, bi + b2i


def s5_direction(u, a_re, a_im, log_step, b_re, b_im, c_re, c_im, reverse):
    step = jnp.exp(log_step)[:, None]
    mag = jnp.exp(a_re * step)
    lam_re = mag * jnp.cos(a_im * step)
    lam_im = mag * jnp.sin(a_im * step)
    den = a_re * a_re + a_im * a_im
    nr = lam_re - 1.0
    ni = lam_im
    coef_re = (nr * a_re + ni * a_im) / den
    coef_im = (ni * a_re - nr * a_im) / den
    bb_re = coef_re[..., None] * b_re - coef_im[..., None] * b_im
    bb_im = coef_re[..., None] * b_im + coef_im[..., None] * b_re
    bu_re = jnp.einsum("blgp,gnp->blgn", u, bb_re)
    bu_im = jnp.einsum("blgp,gnp->blgn", u, bb_im)
    lr = jnp.broadcast_to(lam_re, bu_re.shape)
    li = jnp.broadcast_to(lam_im, bu_re.shape)
    _, _, s_re, s_im = lax.associative_scan(
        _ssm_combine, (lr, li, bu_re, bu_im), reverse=reverse, axis=1)
    return (jnp.einsum("blgn,gpn->blgp", s_re, c_re)
            - jnp.einsum("blgn,gpn->blgp", s_im, c_im))


def s5_mixer(u, a_re, a_im, log_step, b_re, b_im, c_re, c_im, d, w_glu, b_glu):
    dtype = u.dtype
    bsz, seq, _ = u.shape
    f = lambda t: t.astype(jnp.float32)
    ug = f(u).reshape(bsz, seq, SSM_GROUPS, SSM_GROUP)
    y = s5_direction(ug, f(a_re[0]), f(a_im[0]), f(log_step[0]), f(b_re[0]), f(b_im[0]),
                     f(c_re[0]), f(c_im[0]), reverse=False)
    y = y + s5_direction(ug, f(a_re[1]), f(a_im[1]), f(log_step[1]), f(b_re[1]), f(b_im[1]),
                         f(c_re[1]), f(c_im[1]), reverse=True)
    y = y + f(d).reshape(SSM_GROUPS, SSM_GROUP) * ug
    y = jax.nn.gelu(y.reshape(bsz, seq, SSM_WIDTH))
    y = y * jax.nn.sigmoid(y @ f(w_glu) + f(b_glu))
    return y.astype(dtype)


def short_conv(x, w):
    ch = x.shape[-1]
    rhs = jnp.transpose(w)[:, None, :].astype(x.dtype)
    return lax.conv_general_dilated(
        x, rhs, window_strides=(1,), padding=[(DN_CONV // 2, DN_CONV // 2)],
        dimension_numbers=("NWC", "WIO", "NWC"), feature_group_count=ch)


def gated_delta_rule(q, k, v, beta, g):
    b, h, l, dk = q.shape
    dv = v.shape[-1]
    c = DN_CHUNK
    n = l // c
    q = q.reshape(b, h, n, c, dk)
    k = k.reshape(b, h, n, c, dk)
    v = v.reshape(b, h, n, c, dv)
    beta = beta.reshape(b, h, n, c)
    g = jnp.cumsum(g.reshape(b, h, n, c), axis=-1)
    idx = jnp.arange(c)
    incl = idx[:, None] >= idx[None, :]
    strict = idx[:, None] > idx[None, :]
    diff = g[..., :, None] - g[..., None, :]
    decay = jnp.where(incl, jnp.exp(jnp.where(incl, diff, 0.0)), 0.0)
    k_beta = k * beta[..., None]
    lower = jnp.where(strict, jnp.einsum("bhncd,bhnsd->bhncs", k_beta, k) * decay, 0.0)
    eye = jnp.eye(c, dtype=q.dtype)
    rhs = jnp.concatenate([v * beta[..., None], k_beta * jnp.exp(g)[..., None]], axis=-1)
    sol = lax.linalg.triangular_solve(eye + lower, rhs, left_side=True, lower=True,
                                      unit_diagonal=True)
    u_c = sol[..., :dv]
    w_c = sol[..., dv:]
    intra = jnp.einsum("bhncd,bhnsd->bhncs", q, k) * decay
    q_dec = q * jnp.exp(g)[..., None]
    k_dec = k * jnp.exp(g[..., -1:] - g)[..., None]
    g_last = jnp.exp(g[..., -1])
    xs = (jnp.moveaxis(u_c, 2, 0), jnp.moveaxis(w_c, 2, 0), jnp.moveaxis(q_dec, 2, 0),
          jnp.moveaxis(k_dec, 2, 0), jnp.moveaxis(intra, 2, 0), jnp.moveaxis(g_last, 2, 0))

    def step(state, inp):
        u_i, w_i, qd_i, kd_i, a_i, gl_i = inp
        v_new = u_i - jnp.einsum("bhck,bhkv->bhcv", w_i, state)
        o = (jnp.einsum("bhck,bhkv->bhcv", qd_i, state)
             + jnp.einsum("bhcs,bhsv->bhcv", a_i, v_new))
        state = state * gl_i[..., None, None] + jnp.einsum("bhck,bhcv->bhkv", kd_i, v_new)
        return state, o

    s0 = jnp.zeros((b, h, dk, dv), q.dtype)
    _, o = lax.scan(step, s0, xs)
    return jnp.moveaxis(o, 0, 2).reshape(b, h, l, dv)


def deltanet_mixer(q, k, v, a_logit, b_logit, conv_w, a_log, dt_bias, norm_g):
    dtype = q.dtype
    bsz, seq, _ = q.shape
    qkv = jax.nn.silu(short_conv(jnp.concatenate([q, k, v], axis=-1), conv_w))
    qkv = qkv.astype(jnp.float32).reshape(bsz, seq, 3, DN_HEADS, DN_HEAD_DIM)
    qh = jnp.transpose(l2norm(qkv[:, :, 0]) * (DN_HEAD_DIM ** -0.5), (0, 2, 1, 3))
    kh = jnp.transpose(l2norm(qkv[:, :, 1]), (0, 2, 1, 3))
    vh = jnp.transpose(qkv[:, :, 2], (0, 2, 1, 3))
    a4 = a_logit.astype(jnp.float32).reshape(bsz, seq, 2, DN_HEADS)
    b4 = b_logit.astype(jnp.float32).reshape(bsz, seq, 2, DN_HEADS)
    beta = jax.nn.sigmoid(b4)
    g = -jnp.exp(a_log.astype(jnp.float32)) * jax.nn.softplus(a4 + dt_bias.astype(jnp.float32))
    beta_f = jnp.transpose(beta[:, :, 0], (0, 2, 1))
    beta_b = jnp.transpose(beta[:, :, 1], (0, 2, 1))
    g_f = jnp.transpose(g[:, :, 0], (0, 2, 1))
    g_b = jnp.transpose(g[:, :, 1], (0, 2, 1))
    o_f = gated_delta_rule(qh, kh, vh, beta_f, g_f)
    o_b = jnp.flip(gated_delta_rule(jnp.flip(qh, 2), jnp.flip(kh, 2), jnp.flip(vh, 2),
                                    jnp.flip(beta_b, 2), jnp.flip(g_b, 2)), 2)
    o = jnp.transpose(o_f + o_b, (0, 2, 1, 3))
    o = rmsnorm(o, norm_g)
    return o.reshape(bsz, seq, DN_WIDTH).astype(dtype)


def axial_rope(rows):
    row = jnp.repeat(jnp.arange(rows), GRID_W).astype(jnp.float32)
    col = jnp.tile(jnp.arange(GRID_W), rows).astype(jnp.float32)
    axis_dim = ATT_HEAD_DIM // 2
    freqs = ROPE_THETA ** (-jnp.arange(0, axis_dim, 2, dtype=jnp.float32) / axis_dim)
    ang = jnp.concatenate([row[:, None] * freqs, col[:, None] * freqs], axis=-1)
    return jnp.cos(ang), jnp.sin(ang)


def apply_rope(x, cos, sin):
    xp = x.reshape(x.shape[:-1] + (x.shape[-1] // 2, 2))
    x0, x1 = xp[..., 0], xp[..., 1]
    c = cos[None, :, None, :]
    s = sin[None, :, None, :]
    return jnp.stack([x0 * c - x1 * s, x0 * s + x1 * c], axis=-1).reshape(x.shape)


def grid_attention(q, k, v, qn_g, kn_g, cos, sin):
    dtype = q.dtype
    bsz, seq, _ = q.shape
    grp = ATT_HEADS // ATT_KV_HEADS
    qh = rmsnorm(q.reshape(bsz, seq, ATT_HEADS, ATT_HEAD_DIM), qn_g).astype(jnp.float32)
    kh = rmsnorm(k.reshape(bsz, seq, ATT_KV_HEADS, ATT_HEAD_DIM), kn_g).astype(jnp.float32)
    vh = v.reshape(bsz, seq, ATT_KV_HEADS, ATT_HEAD_DIM).astype(jnp.float32)
    qh = apply_rope(qh, cos, sin) * (ATT_HEAD_DIM ** -0.5)
    kh = apply_rope(kh, cos, sin)
    nblk = seq // ATT_BLOCK
    qb = qh.reshape(bsz, nblk, ATT_BLOCK, ATT_KV_HEADS, grp, ATT_HEAD_DIM)
    qb = jnp.transpose(qb, (1, 0, 2, 3, 4, 5))

    def block(qi):
        s = jnp.einsum("bqhgd,bkhd->bhgqk", qi, kh)
        p = jax.nn.softmax(s, axis=-1)
        return jnp.einsum("bhgqk,bkhd->bqhgd", p, vh)

    o = lax.map(block, qb)
    o = jnp.transpose(o, (1, 0, 2, 3, 4, 5)).reshape(bsz, seq, ATT_WIDTH)
    return o.astype(dtype)


def memory_attention(q, mem_n, w_kv):
    dtype = q.dtype
    bsz, seq, _ = q.shape
    kv = mem_n @ w_kv
    km = kv[..., :MEM_WIDTH].reshape(bsz, -1, MEM_HEADS, MEM_HEAD_DIM).astype(jnp.float32)
    vm = kv[..., MEM_WIDTH:].reshape(bsz, -1, MEM_HEADS, MEM_HEAD_DIM).astype(jnp.float32)
    qh = q.reshape(bsz, seq, MEM_HEADS, MEM_HEAD_DIM).astype(jnp.float32)
    s = jnp.einsum("bqhd,bkhd->bhqk", qh, km) * (MEM_HEAD_DIM ** -0.5)
    p = jax.nn.softmax(s, axis=-1)
    o = jnp.einsum("bhqk,bkhd->bqhd", p, vm).reshape(bsz, seq, MEM_WIDTH)
    return o.astype(dtype)


def _fwd_setup_inputs(seed: int = 0) -> dict:
    key = jax.random.key(seed)
    ks = jax.random.split(key, 32)
    f32 = jnp.float32

    def nrm(k, shape, scale):
        return jax.random.normal(k, shape, f32) * scale

    x = nrm(ks[0], (BATCH, SEQ, D_MODEL), 1.0)
    mem = nrm(ks[1], (BATCH, N_MEM, D_MODEL), 1.0)
    norm_g = 1.0 + nrm(ks[2], (DEPTH, D_MODEL), 0.02)
    w_in = nrm(ks[3], (DEPTH, D_MODEL, IN_WIDTH), D_MODEL ** -0.5)
    ssm_shape = (DEPTH, 2, SSM_GROUPS, SSM_STATE)
    ssm_a_re = -0.5 + nrm(ks[4], ssm_shape, 0.01)
    ssm_a_im = jnp.pi * jnp.arange(SSM_STATE, dtype=f32) + nrm(ks[5], ssm_shape, 0.01)
    ssm_log_step = jax.random.uniform(ks[6], (DEPTH, 2, SSM_GROUPS), f32,
                                      math.log(1e-3), math.log(1e-1))
    ssm_b_re = nrm(ks[7], (DEPTH, 2, SSM_GROUPS, SSM_STATE, SSM_GROUP), (2 * SSM_GROUP) ** -0.5)
    ssm_b_im = nrm(ks[8], (DEPTH, 2, SSM_GROUPS, SSM_STATE, SSM_GROUP), (2 * SSM_GROUP) ** -0.5)
    ssm_c_re = nrm(ks[9], (DEPTH, 2, SSM_GROUPS, SSM_GROUP, SSM_STATE), SSM_STATE ** -0.5)
    ssm_c_im = nrm(ks[10], (DEPTH, 2, SSM_GROUPS, SSM_GROUP, SSM_STATE), SSM_STATE ** -0.5)
    ssm_d = nrm(ks[11], (DEPTH, SSM_WIDTH), 1.0)
    ssm_w_glu = nrm(ks[12], (DEPTH, SSM_WIDTH, SSM_WIDTH), SSM_WIDTH ** -0.5)
    ssm_b_glu = nrm(ks[13], (DEPTH, SSM_WIDTH), 0.02)
    dn_conv = nrm(ks[14], (DEPTH, 3 * DN_WIDTH, DN_CONV), DN_CONV ** -0.5)
    dn_a_log = jnp.log(jax.random.uniform(ks[15], (DEPTH, 2, DN_HEADS), f32, 1.0, 16.0))
    dt = jnp.exp(jax.random.uniform(ks[16], (DEPTH, 2, DN_HEADS), f32,
                                    math.log(1e-3), math.log(1e-1)))
    dn_dt_bias = dt + jnp.log(-jnp.expm1(-dt))
    dn_norm_g = 1.0 + nrm(ks[17], (DEPTH, DN_HEAD_DIM), 0.02)
    attn_q_norm = 1.0 + nrm(ks[18], (DEPTH, ATT_HEAD_DIM), 0.02)
    attn_k_norm = 1.0 + nrm(ks[19], (DEPTH, ATT_HEAD_DIM), 0.02)
    mem_norm_g = 1.0 + nrm(ks[20], (DEPTH, D_MODEL), 0.02)
    w_mem_kv = nrm(ks[21], (DEPTH, D_MODEL, 2 * MEM_WIDTH), D_MODEL ** -0.5)
    bks = jax.random.split(ks[22], N_BRANCH)
    w_branch = jnp.concatenate(
        [nrm(bks[i], (DEPTH, BRANCH_WIDTHS[i], D_MODEL), BRANCH_WIDTHS[i] ** -0.5)
         for i in range(N_BRANCH)], axis=1)
    w_out = nrm(ks[23], (DEPTH, D_MODEL, D_MODEL), D_MODEL ** -0.5)
    final_norm_g = 1.0 + nrm(ks[24], (D_MODEL,), 0.02)
    return {
        "x": x, "mem": mem, "norm_g": norm_g, "w_in": w_in,
        "ssm_a_re": ssm_a_re, "ssm_a_im": ssm_a_im, "ssm_log_step": ssm_log_step,
        "ssm_b_re": ssm_b_re, "ssm_b_im": ssm_b_im, "ssm_c_re": ssm_c_re,
        "ssm_c_im": ssm_c_im, "ssm_d": ssm_d, "ssm_w_glu": ssm_w_glu,
        "ssm_b_glu": ssm_b_glu, "dn_conv": dn_conv, "dn_a_log": dn_a_log,
        "dn_dt_bias": dn_dt_bias, "dn_norm_g": dn_norm_g, "attn_q_norm": attn_q_norm,
        "attn_k_norm": attn_k_norm, "mem_norm_g": mem_norm_g, "w_mem_kv": w_mem_kv,
        "w_branch": w_branch, "w_out": w_out, "final_norm_g": final_norm_g,
    }


def _fwd_reference(x, mem, norm_g, w_in, ssm_a_re, ssm_a_im, ssm_log_step, ssm_b_re, ssm_b_im,
              ssm_c_re, ssm_c_im, ssm_d, ssm_w_glu, ssm_b_glu, dn_conv, dn_a_log,
              dn_dt_bias, dn_norm_g, attn_q_norm, attn_k_norm, mem_norm_g, w_mem_kv,
              w_branch, w_out, final_norm_g):
    bsz, seq, _ = x.shape
    rows = seq // GRID_W
    cos, sin = axial_rope(rows)
    split_at = [int(i) for i in np.cumsum(IN_SPLITS)[:-1]]
    for layer in range(DEPTH):
        xn = rmsnorm(x, norm_g[layer])
        h = xn @ w_in[layer]
        (u_a, z_a, dq, dk, dv, da, db, z_b, aq, ak, av, z_c, mq, z_m,
         gate_logits) = jnp.split(h, split_at, axis=-1)

        y_a = s5_mixer(u_a, ssm_a_re[layer], ssm_a_im[layer], ssm_log_step[layer],
                       ssm_b_re[layer], ssm_b_im[layer], ssm_c_re[layer], ssm_c_im[layer],
                       ssm_d[layer], ssm_w_glu[layer], ssm_b_glu[layer]) * jax.nn.silu(z_a)
        y_b = deltanet_mixer(dq, dk, dv, da, db, dn_conv[layer], dn_a_log[layer],
                             dn_dt_bias[layer], dn_norm_g[layer]) * jax.nn.silu(z_b)
        y_c = grid_attention(aq, ak, av, attn_q_norm[layer], attn_k_norm[layer],
                             cos, sin) * jax.nn.silu(z_c)
        y_m = memory_attention(mq, rmsnorm(mem, mem_norm_g[layer]),
                               w_mem_kv[layer]) * jax.nn.silu(z_m)

        gates = jax.nn.sigmoid(gate_logits.reshape(bsz, seq, N_BRANCH, D_MODEL))
        merged = jnp.zeros_like(x)
        for bi, y_br in enumerate((y_a, y_b, y_c, y_m)):
            off = BRANCH_OFFSETS[bi]
            w_b = w_branch[layer, off:off + BRANCH_WIDTHS[bi]]
            merged = merged + gates[:, :, bi] * (y_br @ w_b)
        x = x + merged @ w_out[layer]
    return rmsnorm(x, final_norm_g)


import jax as _jax
import jax.numpy as _jnp

TWIN_FORMAT = 'train_step'
FWD_PARAMS = ['x', 'mem', 'norm_g', 'w_in', 'ssm_a_re', 'ssm_a_im', 'ssm_log_step', 'ssm_b_re', 'ssm_b_im', 'ssm_c_re', 'ssm_c_im', 'ssm_d', 'ssm_w_glu', 'ssm_b_glu', 'dn_conv', 'dn_a_log', 'dn_dt_bias', 'dn_norm_g', 'attn_q_norm', 'attn_k_norm', 'mem_norm_g', 'w_mem_kv', 'w_branch', 'w_out', 'final_norm_g']
TWIN_WEIGHTS = ['norm_g', 'w_in', 'ssm_a_re', 'ssm_a_im', 'ssm_log_step', 'ssm_b_re', 'ssm_b_im', 'ssm_c_re', 'ssm_c_im', 'ssm_d', 'ssm_w_glu', 'ssm_b_glu', 'dn_conv', 'dn_a_log', 'dn_dt_bias', 'dn_norm_g', 'attn_q_norm', 'attn_k_norm', 'mem_norm_g', 'w_mem_kv', 'w_branch', 'w_out', 'final_norm_g']
TWIN_DIFF_INPUT = 'x'
TWIN_INPUTS = ['x', 'mem', 'norm_g', 'w_in', 'ssm_a_re', 'ssm_a_im', 'ssm_log_step', 'ssm_b_re', 'ssm_b_im', 'ssm_c_re', 'ssm_c_im', 'ssm_d', 'ssm_w_glu', 'ssm_b_glu', 'dn_conv', 'dn_a_log', 'dn_dt_bias', 'dn_norm_g', 'attn_q_norm', 'attn_k_norm', 'mem_norm_g', 'w_mem_kv', 'w_branch', 'w_out', 'final_norm_g', 'loss_target', 'm_norm_g', 'm_w_in', 'm_ssm_a_re', 'm_ssm_a_im', 'm_ssm_log_step', 'm_ssm_b_re', 'm_ssm_b_im', 'm_ssm_c_re', 'm_ssm_c_im', 'm_ssm_d', 'm_ssm_w_glu', 'm_ssm_b_glu', 'm_dn_conv', 'm_dn_a_log', 'm_dn_dt_bias', 'm_dn_norm_g', 'm_attn_q_norm', 'm_attn_k_norm', 'm_mem_norm_g', 'm_w_mem_kv', 'm_w_branch', 'm_w_out', 'm_final_norm_g', 'v_norm_g', 'v_w_in', 'v_ssm_a_re', 'v_ssm_a_im', 'v_ssm_log_step', 'v_ssm_b_re', 'v_ssm_b_im', 'v_ssm_c_re', 'v_ssm_c_im', 'v_ssm_d', 'v_ssm_w_glu', 'v_ssm_b_glu', 'v_dn_conv', 'v_dn_a_log', 'v_dn_dt_bias', 'v_dn_norm_g', 'v_attn_q_norm', 'v_attn_k_norm', 'v_mem_norm_g', 'v_w_mem_kv', 'v_w_branch', 'v_w_out', 'v_final_norm_g']
TWIN_OUTPUTS = ['loss', 'grad_x', 'grad_norm_g', 'grad_w_in', 'grad_ssm_a_re', 'grad_ssm_a_im', 'grad_ssm_log_step', 'grad_ssm_b_re', 'grad_ssm_b_im', 'grad_ssm_c_re', 'grad_ssm_c_im', 'grad_ssm_d', 'grad_ssm_w_glu', 'grad_ssm_b_glu', 'grad_dn_conv', 'grad_dn_a_log', 'grad_dn_dt_bias', 'grad_dn_norm_g', 'grad_attn_q_norm', 'grad_attn_k_norm', 'grad_mem_norm_g', 'grad_w_mem_kv', 'grad_w_branch', 'grad_w_out', 'grad_final_norm_g', 'delta_norm_g', 'delta_w_in', 'delta_ssm_a_re', 'delta_ssm_a_im', 'delta_ssm_log_step', 'delta_ssm_b_re', 'delta_ssm_b_im', 'delta_ssm_c_re', 'delta_ssm_c_im', 'delta_ssm_d', 'delta_ssm_w_glu', 'delta_ssm_b_glu', 'delta_dn_conv', 'delta_dn_a_log', 'delta_dn_dt_bias', 'delta_dn_norm_g', 'delta_attn_q_norm', 'delta_attn_k_norm', 'delta_mem_norm_g', 'delta_w_mem_kv', 'delta_w_branch', 'delta_w_out', 'delta_final_norm_g', 'new_m_norm_g', 'new_m_w_in', 'new_m_ssm_a_re', 'new_m_ssm_a_im', 'new_m_ssm_log_step', 'new_m_ssm_b_re', 'new_m_ssm_b_im', 'new_m_ssm_c_re', 'new_m_ssm_c_im', 'new_m_ssm_d', 'new_m_ssm_w_glu', 'new_m_ssm_b_glu', 'new_m_dn_conv', 'new_m_dn_a_log', 'new_m_dn_dt_bias', 'new_m_dn_norm_g', 'new_m_attn_q_norm', 'new_m_attn_k_norm', 'new_m_mem_norm_g', 'new_m_w_mem_kv', 'new_m_w_branch', 'new_m_w_out', 'new_m_final_norm_g', 'new_v_norm_g', 'new_v_w_in', 'new_v_ssm_a_re', 'new_v_ssm_a_im', 'new_v_ssm_log_step', 'new_v_ssm_b_re', 'new_v_ssm_b_im', 'new_v_ssm_c_re', 'new_v_ssm_c_im', 'new_v_ssm_d', 'new_v_ssm_w_glu', 'new_v_ssm_b_glu', 'new_v_dn_conv', 'new_v_dn_a_log', 'new_v_dn_dt_bias', 'new_v_dn_norm_g', 'new_v_attn_q_norm', 'new_v_attn_k_norm', 'new_v_mem_norm_g', 'new_v_w_mem_kv', 'new_v_w_branch', 'new_v_w_out', 'new_v_final_norm_g']
TWIN_LEAF_KINDS = {'loss': 'loss', 'grad_x': 'grad_x', 'grad_norm_g': 'grad_w', 'grad_w_in': 'grad_w', 'grad_ssm_a_re': 'grad_w', 'grad_ssm_a_im': 'grad_w', 'grad_ssm_log_step': 'grad_w', 'grad_ssm_b_re': 'grad_w', 'grad_ssm_b_im': 'grad_w', 'grad_ssm_c_re': 'grad_w', 'grad_ssm_c_im': 'grad_w', 'grad_ssm_d': 'grad_w', 'grad_ssm_w_glu': 'grad_w', 'grad_ssm_b_glu': 'grad_w', 'grad_dn_conv': 'grad_w', 'grad_dn_a_log': 'grad_w', 'grad_dn_dt_bias': 'grad_w', 'grad_dn_norm_g': 'grad_w', 'grad_attn_q_norm': 'grad_w', 'grad_attn_k_norm': 'grad_w', 'grad_mem_norm_g': 'grad_w', 'grad_w_mem_kv': 'grad_w', 'grad_w_branch': 'grad_w', 'grad_w_out': 'grad_w', 'grad_final_norm_g': 'grad_w', 'delta_norm_g': 'delta_w', 'delta_w_in': 'delta_w', 'delta_ssm_a_re': 'delta_w', 'delta_ssm_a_im': 'delta_w', 'delta_ssm_log_step': 'delta_w', 'delta_ssm_b_re': 'delta_w', 'delta_ssm_b_im': 'delta_w', 'delta_ssm_c_re': 'delta_w', 'delta_ssm_c_im': 'delta_w', 'delta_ssm_d': 'delta_w', 'delta_ssm_w_glu': 'delta_w', 'delta_ssm_b_glu': 'delta_w', 'delta_dn_conv': 'delta_w', 'delta_dn_a_log': 'delta_w', 'delta_dn_dt_bias': 'delta_w', 'delta_dn_norm_g': 'delta_w', 'delta_attn_q_norm': 'delta_w', 'delta_attn_k_norm': 'delta_w', 'delta_mem_norm_g': 'delta_w', 'delta_w_mem_kv': 'delta_w', 'delta_w_branch': 'delta_w', 'delta_w_out': 'delta_w', 'delta_final_norm_g': 'delta_w', 'new_m_norm_g': 'new_m', 'new_m_w_in': 'new_m', 'new_m_ssm_a_re': 'new_m', 'new_m_ssm_a_im': 'new_m', 'new_m_ssm_log_step': 'new_m', 'new_m_ssm_b_re': 'new_m', 'new_m_ssm_b_im': 'new_m', 'new_m_ssm_c_re': 'new_m', 'new_m_ssm_c_im': 'new_m', 'new_m_ssm_d': 'new_m', 'new_m_ssm_w_glu': 'new_m', 'new_m_ssm_b_glu': 'new_m', 'new_m_dn_conv': 'new_m', 'new_m_dn_a_log': 'new_m', 'new_m_dn_dt_bias': 'new_m', 'new_m_dn_norm_g': 'new_m', 'new_m_attn_q_norm': 'new_m', 'new_m_attn_k_norm': 'new_m', 'new_m_mem_norm_g': 'new_m', 'new_m_w_mem_kv': 'new_m', 'new_m_w_branch': 'new_m', 'new_m_w_out': 'new_m', 'new_m_final_norm_g': 'new_m', 'new_v_norm_g': 'new_v', 'new_v_w_in': 'new_v', 'new_v_ssm_a_re': 'new_v', 'new_v_ssm_a_im': 'new_v', 'new_v_ssm_log_step': 'new_v', 'new_v_ssm_b_re': 'new_v', 'new_v_ssm_b_im': 'new_v', 'new_v_ssm_c_re': 'new_v', 'new_v_ssm_c_im': 'new_v', 'new_v_ssm_d': 'new_v', 'new_v_ssm_w_glu': 'new_v', 'new_v_ssm_b_glu': 'new_v', 'new_v_dn_conv': 'new_v', 'new_v_dn_a_log': 'new_v', 'new_v_dn_dt_bias': 'new_v', 'new_v_dn_norm_g': 'new_v', 'new_v_attn_q_norm': 'new_v', 'new_v_attn_k_norm': 'new_v', 'new_v_mem_norm_g': 'new_v', 'new_v_w_mem_kv': 'new_v', 'new_v_w_branch': 'new_v', 'new_v_w_out': 'new_v', 'new_v_final_norm_g': 'new_v'}


def _forward(args):
    return _fwd_reference(*[args[k] for k in FWD_PARAMS])


def _output_shape():
    def fwd():
        inp = _fwd_setup_inputs(0)
        return _fwd_reference(*[inp[k] for k in FWD_PARAMS])
    out = _jax.eval_shape(fwd)
    return out.shape, out.dtype

N_MICROBATCH = 1
ADAM_LR = 0.001
ADAM_B1 = 0.9
ADAM_B2 = 0.999
ADAM_EPS = 1e-08
ADAM_WD = 0.01
ADAM_STEP = 10
PER_EXAMPLE_BATCH_AXIS = {'x': 0, 'mem': 0, 'loss_target': 0}
SHARED_INPUTS = []
_WEIGHT_DTYPES = {'norm_g': _jnp.float32, 'w_in': _jnp.float32, 'ssm_a_re': _jnp.float32, 'ssm_a_im': _jnp.float32, 'ssm_log_step': _jnp.float32, 'ssm_b_re': _jnp.float32, 'ssm_b_im': _jnp.float32, 'ssm_c_re': _jnp.float32, 'ssm_c_im': _jnp.float32, 'ssm_d': _jnp.float32, 'ssm_w_glu': _jnp.float32, 'ssm_b_glu': _jnp.float32, 'dn_conv': _jnp.float32, 'dn_a_log': _jnp.float32, 'dn_dt_bias': _jnp.float32, 'dn_norm_g': _jnp.float32, 'attn_q_norm': _jnp.float32, 'attn_k_norm': _jnp.float32, 'mem_norm_g': _jnp.float32, 'w_mem_kv': _jnp.float32, 'w_branch': _jnp.float32, 'w_out': _jnp.float32, 'final_norm_g': _jnp.float32}
MOMENT_SCALE = {'norm_g': 7.132965e-02, 'w_in': 2.408294e-02, 'ssm_a_re': 2.014254e-03, 'ssm_a_im': 2.027094e-03, 'ssm_log_step': 1.963797e+00, 'ssm_b_re': 1.153280e-03, 'ssm_b_im': 1.158665e-03, 'ssm_c_re': 1.652072e-03, 'ssm_c_im': 1.629421e-03, 'ssm_d': 2.430100e-02, 'ssm_w_glu': 6.530240e-03, 'ssm_b_glu': 9.540068e-03, 'dn_conv': 4.890497e-02, 'dn_a_log': 1.415193e-01, 'dn_dt_bias': 1.366402e-01, 'dn_norm_g': 1.550863e-01, 'attn_q_norm': 1.535014e-02, 'attn_k_norm': 1.508424e-02, 'mem_norm_g': 5.983480e-03, 'w_mem_kv': 8.108411e-03, 'w_branch': 2.088526e-02, 'w_out': 4.173895e-02, 'final_norm_g': 3.198425e+01}


def _to_microbatches(a, axis):
    t = _jnp.moveaxis(a, axis, 0)
    t = t.reshape((N_MICROBATCH, t.shape[0] // N_MICROBATCH) + t.shape[1:])
    return _jnp.moveaxis(t, 1, axis + 1)


def setup_inputs(seed: int = 0) -> dict:
    inp = _fwd_setup_inputs(seed)
    key = _jax.random.fold_in(_jax.random.key(seed), 7919)
    shape, _ = _output_shape()
    out = dict(inp)
    out["loss_target"] = _jax.random.normal(_jax.random.fold_in(key, 0), shape, _jnp.float32)
    for i, name in enumerate(TWIN_WEIGHTS):
        w = inp[name].astype(_jnp.float32)
        if MOMENT_SCALE is None:
            s = _jnp.sqrt(_jnp.mean(_jnp.square(w)) + 1e-30)
        else:
            s = MOMENT_SCALE[name]
        km, kv = _jax.random.split(_jax.random.fold_in(key, i + 1))
        out[name] = w
        out["m_" + name] = s * _jax.random.normal(km, w.shape, _jnp.float32)
        out["v_" + name] = (s * s) * _jax.random.uniform(kv, w.shape, _jnp.float32, 0.5, 1.5)
    if N_MICROBATCH > 1:
        for name, axis in PER_EXAMPLE_BATCH_AXIS.items():
            out[name] = _to_microbatches(out[name], axis)
    return {'x': out['x'], 'mem': out['mem'], 'norm_g': out['norm_g'], 'w_in': out['w_in'], 'ssm_a_re': out['ssm_a_re'], 'ssm_a_im': out['ssm_a_im'], 'ssm_log_step': out['ssm_log_step'], 'ssm_b_re': out['ssm_b_re'], 'ssm_b_im': out['ssm_b_im'], 'ssm_c_re': out['ssm_c_re'], 'ssm_c_im': out['ssm_c_im'], 'ssm_d': out['ssm_d'], 'ssm_w_glu': out['ssm_w_glu'], 'ssm_b_glu': out['ssm_b_glu'], 'dn_conv': out['dn_conv'], 'dn_a_log': out['dn_a_log'], 'dn_dt_bias': out['dn_dt_bias'], 'dn_norm_g': out['dn_norm_g'], 'attn_q_norm': out['attn_q_norm'], 'attn_k_norm': out['attn_k_norm'], 'mem_norm_g': out['mem_norm_g'], 'w_mem_kv': out['w_mem_kv'], 'w_branch': out['w_branch'], 'w_out': out['w_out'], 'final_norm_g': out['final_norm_g'], 'loss_target': out['loss_target'], 'm_norm_g': out['m_norm_g'], 'm_w_in': out['m_w_in'], 'm_ssm_a_re': out['m_ssm_a_re'], 'm_ssm_a_im': out['m_ssm_a_im'], 'm_ssm_log_step': out['m_ssm_log_step'], 'm_ssm_b_re': out['m_ssm_b_re'], 'm_ssm_b_im': out['m_ssm_b_im'], 'm_ssm_c_re': out['m_ssm_c_re'], 'm_ssm_c_im': out['m_ssm_c_im'], 'm_ssm_d': out['m_ssm_d'], 'm_ssm_w_glu': out['m_ssm_w_glu'], 'm_ssm_b_glu': out['m_ssm_b_glu'], 'm_dn_conv': out['m_dn_conv'], 'm_dn_a_log': out['m_dn_a_log'], 'm_dn_dt_bias': out['m_dn_dt_bias'], 'm_dn_norm_g': out['m_dn_norm_g'], 'm_attn_q_norm': out['m_attn_q_norm'], 'm_attn_k_norm': out['m_attn_k_norm'], 'm_mem_norm_g': out['m_mem_norm_g'], 'm_w_mem_kv': out['m_w_mem_kv'], 'm_w_branch': out['m_w_branch'], 'm_w_out': out['m_w_out'], 'm_final_norm_g': out['m_final_norm_g'], 'v_norm_g': out['v_norm_g'], 'v_w_in': out['v_w_in'], 'v_ssm_a_re': out['v_ssm_a_re'], 'v_ssm_a_im': out['v_ssm_a_im'], 'v_ssm_log_step': out['v_ssm_log_step'], 'v_ssm_b_re': out['v_ssm_b_re'], 'v_ssm_b_im': out['v_ssm_b_im'], 'v_ssm_c_re': out['v_ssm_c_re'], 'v_ssm_c_im': out['v_ssm_c_im'], 'v_ssm_d': out['v_ssm_d'], 'v_ssm_w_glu': out['v_ssm_w_glu'], 'v_ssm_b_glu': out['v_ssm_b_glu'], 'v_dn_conv': out['v_dn_conv'], 'v_dn_a_log': out['v_dn_a_log'], 'v_dn_dt_bias': out['v_dn_dt_bias'], 'v_dn_norm_g': out['v_dn_norm_g'], 'v_attn_q_norm': out['v_attn_q_norm'], 'v_attn_k_norm': out['v_attn_k_norm'], 'v_mem_norm_g': out['v_mem_norm_g'], 'v_w_mem_kv': out['v_w_mem_kv'], 'v_w_branch': out['v_w_branch'], 'v_w_out': out['v_w_out'], 'v_final_norm_g': out['v_final_norm_g']}


def _loss(weights, diff, rest, loss_target):
    with _jax.named_scope("forward"):
        args = {**rest, TWIN_DIFF_INPUT: diff, **{k: w.astype(_WEIGHT_DTYPES[k]) for k, w in weights.items()}}
        y = _forward(args)
    with _jax.named_scope("loss_head"):
        err = _jnp.square(y.astype(_jnp.float32) - loss_target)
        return 0.5 * _jnp.sum(_jnp.mean(err, axis=-1)) if err.ndim else 0.5 * err


def _adamw(w, g, m, v):
    m = ADAM_B1 * m + (1.0 - ADAM_B1) * g
    v = ADAM_B2 * v + (1.0 - ADAM_B2) * _jnp.square(g)
    m_hat = m / (1.0 - ADAM_B1 ** ADAM_STEP)
    v_hat = v / (1.0 - ADAM_B2 ** ADAM_STEP)
    delta = -ADAM_LR * (m_hat / (_jnp.sqrt(v_hat) + ADAM_EPS) + ADAM_WD * w)
    return delta, m, v


def reference(x, mem, norm_g, w_in, ssm_a_re, ssm_a_im, ssm_log_step, ssm_b_re, ssm_b_im, ssm_c_re, ssm_c_im, ssm_d, ssm_w_glu, ssm_b_glu, dn_conv, dn_a_log, dn_dt_bias, dn_norm_g, attn_q_norm, attn_k_norm, mem_norm_g, w_mem_kv, w_branch, w_out, final_norm_g, loss_target, m_norm_g, m_w_in, m_ssm_a_re, m_ssm_a_im, m_ssm_log_step, m_ssm_b_re, m_ssm_b_im, m_ssm_c_re, m_ssm_c_im, m_ssm_d, m_ssm_w_glu, m_ssm_b_glu, m_dn_conv, m_dn_a_log, m_dn_dt_bias, m_dn_norm_g, m_attn_q_norm, m_attn_k_norm, m_mem_norm_g, m_w_mem_kv, m_w_branch, m_w_out, m_final_norm_g, v_norm_g, v_w_in, v_ssm_a_re, v_ssm_a_im, v_ssm_log_step, v_ssm_b_re, v_ssm_b_im, v_ssm_c_re, v_ssm_c_im, v_ssm_d, v_ssm_w_glu, v_ssm_b_glu, v_dn_conv, v_dn_a_log, v_dn_dt_bias, v_dn_norm_g, v_attn_q_norm, v_attn_k_norm, v_mem_norm_g, v_w_mem_kv, v_w_branch, v_w_out, v_final_norm_g):
    given = dict(x=x, mem=mem, norm_g=norm_g, w_in=w_in, ssm_a_re=ssm_a_re, ssm_a_im=ssm_a_im, ssm_log_step=ssm_log_step, ssm_b_re=ssm_b_re, ssm_b_im=ssm_b_im, ssm_c_re=ssm_c_re, ssm_c_im=ssm_c_im, ssm_d=ssm_d, ssm_w_glu=ssm_w_glu, ssm_b_glu=ssm_b_glu, dn_conv=dn_conv, dn_a_log=dn_a_log, dn_dt_bias=dn_dt_bias, dn_norm_g=dn_norm_g, attn_q_norm=attn_q_norm, attn_k_norm=attn_k_norm, mem_norm_g=mem_norm_g, w_mem_kv=w_mem_kv, w_branch=w_branch, w_out=w_out, final_norm_g=final_norm_g, loss_target=loss_target, m_norm_g=m_norm_g, m_w_in=m_w_in, m_ssm_a_re=m_ssm_a_re, m_ssm_a_im=m_ssm_a_im, m_ssm_log_step=m_ssm_log_step, m_ssm_b_re=m_ssm_b_re, m_ssm_b_im=m_ssm_b_im, m_ssm_c_re=m_ssm_c_re, m_ssm_c_im=m_ssm_c_im, m_ssm_d=m_ssm_d, m_ssm_w_glu=m_ssm_w_glu, m_ssm_b_glu=m_ssm_b_glu, m_dn_conv=m_dn_conv, m_dn_a_log=m_dn_a_log, m_dn_dt_bias=m_dn_dt_bias, m_dn_norm_g=m_dn_norm_g, m_attn_q_norm=m_attn_q_norm, m_attn_k_norm=m_attn_k_norm, m_mem_norm_g=m_mem_norm_g, m_w_mem_kv=m_w_mem_kv, m_w_branch=m_w_branch, m_w_out=m_w_out, m_final_norm_g=m_final_norm_g, v_norm_g=v_norm_g, v_w_in=v_w_in, v_ssm_a_re=v_ssm_a_re, v_ssm_a_im=v_ssm_a_im, v_ssm_log_step=v_ssm_log_step, v_ssm_b_re=v_ssm_b_re, v_ssm_b_im=v_ssm_b_im, v_ssm_c_re=v_ssm_c_re, v_ssm_c_im=v_ssm_c_im, v_ssm_d=v_ssm_d, v_ssm_w_glu=v_ssm_w_glu, v_ssm_b_glu=v_ssm_b_glu, v_dn_conv=v_dn_conv, v_dn_a_log=v_dn_a_log, v_dn_dt_bias=v_dn_dt_bias, v_dn_norm_g=v_dn_norm_g, v_attn_q_norm=v_attn_q_norm, v_attn_k_norm=v_attn_k_norm, v_mem_norm_g=v_mem_norm_g, v_w_mem_kv=v_w_mem_kv, v_w_branch=v_w_branch, v_w_out=v_w_out, v_final_norm_g=v_final_norm_g)
    weights = {n: given[n] for n in TWIN_WEIGHTS}
    shared = {n: given[n] for n in SHARED_INPUTS}
    per_example = {n: given[n] for n in ['x', 'mem']}
    grad_fn = _jax.value_and_grad(_loss, argnums=(0, 1))

    def one_microbatch(ex, loss_target):
        ex = dict(ex)
        diff = ex.pop(TWIN_DIFF_INPUT)
        return grad_fn(weights, diff, {**shared, **ex}, loss_target)

    if N_MICROBATCH == 1:
        loss, (grad_w, grad_x) = one_microbatch(per_example, given["loss_target"])
    else:
        def body(carry, xs):
            loss_sum, grad_sum = carry
            l_k, (gw_k, gx_k) = one_microbatch(xs[0], xs[1])
            with _jax.named_scope("update"):
                return (loss_sum + l_k, _jax.tree.map(_jnp.add, grad_sum, gw_k)), gx_k

        init = (_jnp.zeros((), _jnp.float32), _jax.tree.map(_jnp.zeros_like, weights))
        (loss, grad_w), grad_x = _jax.lax.scan(body, init, (per_example, given["loss_target"]))
    with _jax.named_scope("update"):
        delta_w, new_m, new_v = {}, {}, {}
        for n in TWIN_WEIGHTS:
            delta_w[n], new_m[n], new_v[n] = _adamw(weights[n], grad_w[n], given["m_" + n], given["v_" + n])
    return (loss, grad_x, *[grad_w[n] for n in TWIN_WEIGHTS], *[delta_w[n] for n in TWIN_WEIGHTS],
            *[new_m[n] for n in TWIN_WEIGHTS], *[new_v[n] for n in TWIN_WEIGHTS])
```

```python
import functools
import math

import numpy as np
import jax
import jax.numpy as jnp
from jax import lax
from jax.experimental import pallas as pl
from jax.experimental.pallas import tpu as pltpu

F32 = jnp.float32
BF16 = jnp.bfloat16
HI = lax.Precision.HIGHEST
EPS = 1e-6
LANE = 128
SUBLANE = 8
VMEM_LIMIT = 56 * 1024 * 1024
N_DEV = 8
PACK_W = 1024

ADAM_LR, ADAM_B1, ADAM_B2, ADAM_EPS, ADAM_WD, ADAM_STEP = 0.001, 0.9, 0.999, 1e-08, 0.01, 10

CFG = dict(D=2048, L=8192, GRID_W=64, NMEM=256, DEPTH=2,
           SG=48, SP=16, SN=64,
           DNH=6, DNK=128, CONV=5, CHUNK=64,
           AH=8, AKV=2, AD=128, ROPE_THETA=10000.0,
           MH=4, MD=128)

WEIGHTS = ['norm_g', 'w_in', 'ssm_a_re', 'ssm_a_im', 'ssm_log_step', 'ssm_b_re', 'ssm_b_im', 'ssm_c_re',
           'ssm_c_im', 'ssm_d', 'ssm_w_glu', 'ssm_b_glu', 'dn_conv', 'dn_a_log', 'dn_dt_bias', 'dn_norm_g',
           'attn_q_norm', 'attn_k_norm', 'mem_norm_g', 'w_mem_kv', 'w_branch', 'w_out', 'final_norm_g']
SHARDED = ['w_in', 'w_branch', 'w_out', 'w_mem_kv', 'ssm_w_glu', 'dn_conv']
SMALL = [w for w in WEIGHTS if w not in SHARDED]


def _dims(c):
    d = dict(c)
    d['SW'] = c['SG'] * c['SP']
    d['NB'] = d['SW'] // LANE
    d['GPB'] = LANE // c['SP']
    d['BS'] = d['GPB'] * c['SN']
    d['DW'] = c['DNH'] * c['DNK']
    d['AW'] = c['AH'] * c['AD']
    d['AKW'] = c['AKV'] * c['AD']
    d['MW'] = c['MH'] * c['MD']
    d['BT'] = d['SW'] + d['DW'] + d['AW'] + d['MW']
    return d


def _round_up(a, b):
    return (a + b - 1) // b * b


def _layout(c):
    d = _dims(c)
    D, SW, DW, AW, AKW, MW = d['D'], d['SW'], d['DW'], d['AW'], d['AKW'], d['MW']
    order = [('u_a', SW, SW), ('z_a', SW, SW), ('dq', DW, DW), ('dk', DW, DW), ('dv', DW, DW), ('z_b', DW, DW),
             ('ak', AKW, AKW), ('av', AKW, AKW), ('aq', AW, AW), ('z_c', AW, AW), ('mq', MW, MW), ('z_m', MW, MW),
             ('gates', 4 * D, D), ('dadb', LANE, LANE)]
    off, seg = 0, {}
    for name, w, al in order:
        off = _round_up(off, al)
        seg[name] = (off, w)
        off += w
    seg['_total'] = _round_up(off, 512)
    ref_order = [('u_a', SW), ('z_a', SW), ('dq', DW), ('dk', DW), ('dv', DW), ('da', 2 * d['DNH']),
                 ('db', 2 * d['DNH']), ('z_b', DW), ('aq', AW), ('ak', AKW), ('av', AKW), ('z_c', AW),
                 ('mq', MW), ('z_m', MW), ('gates', 4 * D)]
    roff, rseg = 0, {}
    for name, w in ref_order:
        rseg[name] = (roff, w)
        roff += w
    rseg['_total'] = roff
    return seg, rseg


def _cparams(sem):
    return pltpu.CompilerParams(dimension_semantics=sem, vmem_limit_bytes=VMEM_LIMIT)


def _pick(t, n):
    t = min(t, n)
    while n % t:
        t //= 2
    return t


def _mm(a, b, *, name, ta=False, tb=False, add=None, out_dtype=F32, tm=512, tn=512, tk=512):
    M, K = (a.shape[1], a.shape[0]) if ta else a.shape
    N = b.shape[0] if tb else b.shape[1]
    assert (b.shape[1] if tb else b.shape[0]) == K
    tm, tn, tk = _pick(tm, M), _pick(tn, N), _pick(tk, K)
    nk = K // tk
    dn = (((0 if ta else 1,), (1 if tb else 0,)), ((), ()))
    has_add = add is not None

    def body(*refs):
        if has_add:
            a_ref, b_ref, add_ref, o_ref, acc = refs
        else:
            a_ref, b_ref, o_ref, acc = refs
        k = pl.program_id(2)

        @pl.when(k == 0)
        def _():
            acc[...] = jnp.zeros_like(acc)

        acc[...] += lax.dot_general(a_ref[...].astype(BF16), b_ref[...].astype(BF16), dn,
                                    preferred_element_type=F32)

        @pl.when(k == nk - 1)
        def _():
            r = acc[...]
            if has_add:
                r = r + add_ref[...]
            o_ref[...] = r.astype(o_ref.dtype)

    a_spec = pl.BlockSpec((tk, tm), lambda i, j, k: (k, i)) if ta else pl.BlockSpec((tm, tk), lambda i, j, k: (i, k))
    b_spec = pl.BlockSpec((tn, tk), lambda i, j, k: (j, k)) if tb else pl.BlockSpec((tk, tn), lambda i, j, k: (k, j))
    in_specs = [a_spec, b_spec]
    args = [a, b]
    if has_add:
        in_specs.append(pl.BlockSpec((tm, tn), lambda i, j, k: (i, j)))
        args.append(add)
    return pl.pallas_call(
        body, name=name, grid=(M // tm, N // tn, nk),
        in_specs=in_specs, out_specs=pl.BlockSpec((tm, tn), lambda i, j, k: (i, j)),
        out_shape=jax.ShapeDtypeStruct((M, N), out_dtype),
        scratch_shapes=[pltpu.VMEM((tm, tn), F32)],
        compiler_params=_cparams(("parallel", "parallel", "arbitrary")),
    )(*args)


def _row_spec(tm, w, cb):
    return pl.BlockSpec((tm, w), lambda i, cb=cb: (i, cb))


def _rowwise(fn, rows, params, outs, *, tm, name):
    L = rows[0][0].shape[0]
    tm = _pick(tm, L)
    nr, npar = len(rows), len(params)

    def body(*refs):
        vals = [r[...] for r in refs[:nr + npar]]
        res = fn(*vals)
        for o_ref, v in zip(refs[nr + npar:], res):
            o_ref[...] = v.astype(o_ref.dtype)

    in_specs = [_row_spec(tm, w, cb) for (_, w, cb) in rows]
    in_specs += [pl.BlockSpec(p.shape, lambda i: (0, 0)) for p in params]
    res = pl.pallas_call(
        body, name=name, grid=(L // tm,), in_specs=in_specs,
        out_specs=[pl.BlockSpec((tm, w), lambda i: (i, 0)) for (w, _) in outs],
        out_shape=[jax.ShapeDtypeStruct((L, w), dt) for (w, dt) in outs],
        compiler_params=_cparams(("parallel",)),
    )(*[r[0] for r in rows], *params)
    return list(res)


def _rowwise_bwd(fn, rows, params, cts, drows, dparams, *, tm, name, accs=None):
    L = rows[0][0].shape[0]
    tm = _pick(tm, L)
    nr, npar = len(rows), len(params)
    accs = accs or {}
    ct_flat = [c for grp in cts for c in grp]
    ct_sizes = [len(grp) for grp in cts]
    acc_keys = sorted(accs)
    d_r = [i for i in range(nr) if drows[i]]
    d_p = [i for i in range(npar) if dparams[i]]
    n_in = nr + npar + len(ct_flat) + len(acc_keys)

    def body(*refs):
        vals = [r[...] for r in refs[:nr + npar]]
        ct_refs = refs[nr + npar:nr + npar + len(ct_flat)]
        acc_refs = refs[nr + npar + len(ct_flat):n_in]
        o_refs = refs[n_in:]
        ct_vals, pos = [], 0
        for n in ct_sizes:
            v = ct_refs[pos][...].astype(F32)
            for r in ct_refs[pos + 1:pos + n]:
                v = v + r[...].astype(F32)
            ct_vals.append(v)
            pos += n
        diff_idx = d_r + [nr + i for i in d_p]

        def g(*dv):
            full = list(vals)
            for i, v in zip(diff_idx, dv):
                full[i] = v
            return tuple(o.astype(F32) for o in fn(*full))

        _, vjp = jax.vjp(g, *[vals[i] for i in diff_idx])
        grads = vjp(tuple(ct_vals))
        for n, i in enumerate(d_r):
            gv = grads[n].astype(F32)
            if i in accs:
                gv = gv + acc_refs[acc_keys.index(i)][...]
            o_refs[n][...] = gv
        step = pl.program_id(0)
        for n, i in enumerate(d_p):
            o_ref = o_refs[len(d_r) + n]

            @pl.when(step == 0)
            def _(o_ref=o_ref):
                o_ref[...] = jnp.zeros_like(o_ref)

            o_ref[...] += grads[len(d_r) + n].astype(F32)

    in_specs = [_row_spec(tm, w, cb) for (_, w, cb) in rows]
    in_specs += [pl.BlockSpec(p.shape, lambda i: (0, 0)) for p in params]
    in_specs += [_row_spec(tm, w, cb) for (_, w, cb) in ct_flat]
    in_specs += [_row_spec(tm, accs[k][1], accs[k][2]) for k in acc_keys]
    out_specs = [pl.BlockSpec((tm, rows[i][1]), lambda i_: (i_, 0)) for i in d_r]
    out_specs += [pl.BlockSpec(params[i].shape, lambda i_: (0, 0)) for i in d_p]
    out_shape = [jax.ShapeDtypeStruct((L, rows[i][1]), F32) for i in d_r]
    out_shape += [jax.ShapeDtypeStruct(params[i].shape, F32) for i in d_p]
    res = pl.pallas_call(
        body, name=name, grid=(L // tm,), in_specs=in_specs, out_specs=out_specs, out_shape=out_shape,
        compiler_params=_cparams(("arbitrary",)),
    )(*[r[0] for r in rows], *params, *[c[0] for c in ct_flat], *[accs[k][0] for k in acc_keys])
    res = list(res)
    return res[:len(d_r)], res[len(d_r):]


def _silu(x):
    return x * jax.nn.sigmoid(x)


def _rms(x, g):
    return x * lax.rsqrt(jnp.mean(x * x, axis=-1, keepdims=True) + EPS) * g


def _softplus(x):
    return jnp.maximum(x, 0.0) + jnp.log1p(jnp.exp(-jnp.abs(x)))


def _heads(x, hd):
    return [x[:, i * hd:(i + 1) * hd] for i in range(x.shape[1] // hd)]


def _f_norm(x, g):
    return (_rms(x, g),)


def _f_s5tail(ys, u, z, d, wglu, bglu):
    y = jax.nn.gelu(ys + d * u)
    gate = jax.nn.sigmoid(jnp.dot(y.astype(BF16), wglu.astype(BF16), preferred_element_type=F32) + bglu)
    return (y * gate * _silu(z),)


def _make_f_dnpre(nh, hd):
    def f(qc, kc, dadb, alog, dtb):
        qn = [q * lax.rsqrt(jnp.sum(q * q, axis=-1, keepdims=True) + EPS) * (hd ** -0.5) for q in _heads(qc, hd)]
        kn = [k * lax.rsqrt(jnp.sum(k * k, axis=-1, keepdims=True) + EPS) for k in _heads(kc, hd)]
        g = -jnp.exp(alog) * _softplus(dadb + dtb)
        beta = jax.nn.sigmoid(dadb)
        lane = lax.broadcasted_iota(jnp.int32, dadb.shape, 1)
        gb = jnp.where(lane < 2 * nh, g, jnp.where(lane < 4 * nh, beta, 0.0))
        return jnp.concatenate(qn, axis=1), jnp.concatenate(kn, axis=1), gb
    return f


def _make_f_dnpost(hd):
    def f(o, z, ng):
        y = [_rms(oh, ng) for oh in _heads(o, hd)]
        return (jnp.concatenate(y, axis=1) * _silu(z),)
    return f


def _make_f_attpre(hd, with_v):
    def rope(x, g, cosf, sins, perm, scale):
        xn = _rms(x, g)
        xs = jnp.dot(xn, perm, precision=HI, preferred_element_type=F32)
        return (xn * cosf + xs * sins) * scale

    def f(aq, ak, *rest):
        if with_v:
            av, cosf, sins, perm, qg, kg = rest
        else:
            cosf, sins, perm, qg, kg = rest
        qh = jnp.concatenate([rope(x, qg, cosf, sins, perm, hd ** -0.5) for x in _heads(aq, hd)], axis=1)
        kh = jnp.concatenate([rope(x, kg, cosf, sins, perm, 1.0) for x in _heads(ak, hd)], axis=1)
        return (qh, kh, av) if with_v else (qh, kh)
    return f


def _f_gate(o, z):
    return (o * _silu(z),)


def _make_f_mem(nh, hd):
    def f(mq, z, kv):
        mw = nh * hd
        outs = []
        for h, q in enumerate(_heads(mq, hd)):
            k = kv[:, h * hd:(h + 1) * hd]
            v = kv[:, mw + h * hd:mw + (h + 1) * hd]
            s = lax.dot_general(q.astype(BF16), k.astype(BF16), (((1,), (1,)), ((), ())),
                                preferred_element_type=F32) * (hd ** -0.5)
            s = s - jnp.max(s, axis=-1, keepdims=True)
            p = jnp.exp(s)
            p = p / jnp.sum(p, axis=-1, keepdims=True)
            outs.append(jnp.dot(p.astype(BF16), v.astype(BF16), preferred_element_type=F32))
        return (jnp.concatenate(outs, axis=1) * _silu(z),)
    return f


def _f_merge(p0, p1, p2, p3, g0, g1, g2, g3):
    return (jax.nn.sigmoid(g0) * p0 + jax.nn.sigmoid(g1) * p1 + jax.nn.sigmoid(g2) * p2 + jax.nn.sigmoid(g3) * p3,)


def _make_f_delta(hd):
    def f(do, o):
        out = [jnp.broadcast_to(jnp.sum(a * b, axis=-1, keepdims=True), a.shape)
               for a, b in zip(_heads(do, hd), _heads(o, hd))]
        return (jnp.concatenate(out, axis=1),)
    return f


def _s5_prep(a_re, a_im, log_step, b_re, b_im, c_re, c_im, d):
    nb, gpb, sn, sp = d['NB'], d['GPB'], d['SN'], d['SP']
    step = jnp.exp(log_step)[:, None]
    mag = jnp.exp(a_re * step)
    lam_re = mag * jnp.cos(a_im * step)
    lam_im = mag * jnp.sin(a_im * step)
    den = a_re * a_re + a_im * a_im
    nr, ni = lam_re - 1.0, lam_im
    coef_re = (nr * a_re + ni * a_im) / den
    coef_im = (ni * a_re - nr * a_im) / den
    bb_re = coef_re[..., None] * b_re - coef_im[..., None] * b_im
    bb_im = coef_re[..., None] * b_im + coef_im[..., None] * b_re
    eye = jnp.eye(gpb, dtype=F32)

    def blk_in(bb):
        t = bb.reshape(nb, gpb, sn, sp)
        return jnp.einsum("jgnp,gh->jgphn", t, eye).reshape(nb, gpb * sp, gpb * sn)

    def blk_out(cc):
        t = cc.reshape(nb, gpb, sp, sn)
        return jnp.einsum("jgpn,gh->jgnhp", t, eye).reshape(nb, gpb * sn, gpb * sp)

    wb = jnp.concatenate([blk_in(bb_re), blk_in(bb_im)], axis=2)
    wc = jnp.concatenate([blk_out(c_re), blk_out(-c_im)], axis=1)
    return wb, wc, lam_re.reshape(nb, gpb * sn), lam_im.reshape(nb, gpb * sn)


def _s5_tables(lam_re, lam_im, rev, conj):
    lr, li = lam_re, (-lam_im if conj else lam_im)

    def cmul(a, b):
        return a[0] * b[0] - a[1] * b[1], a[0] * b[1] + a[1] * b[0]

    pw = [(lr, li)]
    for _ in range(7):
        pw.append(cmul(pw[-1], (lr, li)))
    bc = lambda t: jnp.broadcast_to(t[:, None, :], (t.shape[0], 8, t.shape[1]))
    order = list(range(8))[::-1] if rev else list(range(8))
    pwr = jnp.stack([pw[i][0] for i in order], axis=1)
    pwi = jnp.stack([pw[i][1] for i in order], axis=1)
    tabs = [bc(pw[0][0]), bc(pw[0][1]), bc(pw[1][0]), bc(pw[1][1]), bc(pw[3][0]), bc(pw[3][1]), pwr, pwi]
    return jnp.stack(tabs, axis=1)


def _scan_group(xr, xi, lt_ref, j, cr, ci, rev):
    row = lax.broadcasted_iota(jnp.int32, xr.shape, 0)
    for lvl, k in enumerate((1, 2, 4)):
        l_r, l_i = lt_ref[j, 2 * lvl], lt_ref[j, 2 * lvl + 1]
        sh = (8 - k) if rev else k
        keep = (row < 8 - k) if rev else (row >= k)
        sr = jnp.where(keep, pltpu.roll(xr, sh, 0), 0.0)
        si = jnp.where(keep, pltpu.roll(xi, sh, 0), 0.0)
        xr, xi = xr + l_r * sr - l_i * si, xi + l_r * si + l_i * sr
    p_r, p_i = lt_ref[j, 6], lt_ref[j, 7]
    return xr + p_r * cr - p_i * ci, xi + p_r * ci + p_i * cr


def _last_row(x, rev):
    row = lax.broadcasted_iota(jnp.int32, x.shape, 0)
    v = jnp.sum(jnp.where(row == (0 if rev else 7), x, 0.0), axis=0, keepdims=True)
    return jnp.broadcast_to(v, x.shape)


def _s5_fwd(hsrc, ucb, wb, wc, lt, *, rev, acc, tb, name, d):
    L, SW, NB, BS = hsrc.shape[0], d['SW'], d['NB'], d['BS']
    tb = _pick(tb, L)
    nblk, ngr = L // tb, tb // 8
    tix = (lambda b: nblk - 1 - b) if rev else (lambda b: b)
    has_acc = acc is not None

    def body(*refs):
        if has_acc:
            u_ref, wb_ref, wc_ref, lt_ref, acc_ref, y_ref, cin_ref, bu_s, car = refs
        else:
            u_ref, wb_ref, wc_ref, lt_ref, y_ref, cin_ref, bu_s, car = refs

        @pl.when(pl.program_id(0) == 0)
        def _():
            car[...] = jnp.zeros_like(car)

        cin_ref[...] = car[...]
        for j in range(NB):
            bu_s[:, j * 2 * BS:(j + 1) * 2 * BS] = jnp.dot(
                u_ref[:, j * LANE:(j + 1) * LANE].astype(BF16), wb_ref[j], preferred_element_type=F32)

        def grp(r, _):
            base = pl.multiple_of((ngr - 1 - r if rev else r) * 8, 8)
            for j in range(NB):
                c0 = j * 2 * BS
                xr, xi = _scan_group(bu_s[pl.ds(base, 8), c0:c0 + BS], bu_s[pl.ds(base, 8), c0 + BS:c0 + 2 * BS],
                                     lt_ref, j, car[:, c0:c0 + BS], car[:, c0 + BS:c0 + 2 * BS], rev)
                bu_s[pl.ds(base, 8), c0:c0 + BS] = xr
                bu_s[pl.ds(base, 8), c0 + BS:c0 + 2 * BS] = xi
                car[:, c0:c0 + BS] = _last_row(xr, rev)
                car[:, c0 + BS:c0 + 2 * BS] = _last_row(xi, rev)
            return 0

        lax.fori_loop(0, ngr, grp, 0)
        for j in range(NB):
            y = jnp.dot(bu_s[:, j * 2 * BS:(j + 1) * 2 * BS].astype(BF16), wc_ref[j], preferred_element_type=F32)
            if has_acc:
                y = y + acc_ref[:, j * LANE:(j + 1) * LANE]
            y_ref[:, j * LANE:(j + 1) * LANE] = y

    in_specs = [pl.BlockSpec((tb, SW), lambda b: (tix(b), ucb)),
                pl.BlockSpec(wb.shape, lambda b: (0, 0, 0)), pl.BlockSpec(wc.shape, lambda b: (0, 0, 0)),
                pl.BlockSpec(lt.shape, lambda b: (0, 0, 0, 0))]
    args = [hsrc, wb, wc, lt]
    if has_acc:
        in_specs.append(pl.BlockSpec((tb, SW), lambda b: (tix(b), 0)))
        args.append(acc)
    y, cin = pl.pallas_call(
        body, name=name, grid=(nblk,), in_specs=in_specs,
        out_specs=[pl.BlockSpec((tb, SW), lambda b: (tix(b), 0)),
                   pl.BlockSpec((8, NB * 2 * BS), lambda b: (tix(b), 0))],
        out_shape=[jax.ShapeDtypeStruct((L, SW), F32), jax.ShapeDtypeStruct((nblk * 8, NB * 2 * BS), F32)],
        scratch_shapes=[pltpu.VMEM((tb, NB * 2 * BS), F32), pltpu.VMEM((8, NB * 2 * BS), F32)],
        compiler_params=_cparams(("arbitrary",)),
    )(*args)
    return y, cin


def _s5_bwd(hsrc, ucb, dy, cin, wb, wc, lt, lt_adj, *, rev, acc, tb, name, d):
    L, SW, NB, BS = hsrc.shape[0], d['SW'], d['NB'], d['BS']
    tb = _pick(tb, L)
    nblk, ngr = L // tb, tb // 8
    arev = not rev
    tix = (lambda b: nblk - 1 - b) if arev else (lambda b: b)
    has_acc = acc is not None
    NT = (((1,), (1,)), ((), ()))
    TN = (((0,), (0,)), ((), ()))

    def body(*refs):
        if has_acc:
            (u_ref, dy_ref, cin_ref, wb_ref, wc_ref, lt_ref, la_ref, acc_ref,
             du_ref, dwb_ref, dwc_ref, dlam_ref, s_s, g_s, car, acar) = refs
        else:
            (u_ref, dy_ref, cin_ref, wb_ref, wc_ref, lt_ref, la_ref,
             du_ref, dwb_ref, dwc_ref, dlam_ref, s_s, g_s, car, acar) = refs

        @pl.when(pl.program_id(0) == 0)
        def _():
            acar[...] = jnp.zeros_like(acar)
            dwb_ref[...] = jnp.zeros_like(dwb_ref)
            dwc_ref[...] = jnp.zeros_like(dwc_ref)
            dlam_ref[...] = jnp.zeros_like(dlam_ref)

        car[...] = cin_ref[...]
        for j in range(NB):
            s_s[:, j * 2 * BS:(j + 1) * 2 * BS] = jnp.dot(
                u_ref[:, j * LANE:(j + 1) * LANE].astype(BF16), wb_ref[j], preferred_element_type=F32)
            g_s[:, j * 2 * BS:(j + 1) * 2 * BS] = lax.dot_general(
                dy_ref[:, j * LANE:(j + 1) * LANE].astype(BF16), wc_ref[j], NT, preferred_element_type=F32)

        def fgrp(r, _):
            base = pl.multiple_of((ngr - 1 - r if rev else r) * 8, 8)
            for j in range(NB):
                c0 = j * 2 * BS
                xr, xi = _scan_group(s_s[pl.ds(base, 8), c0:c0 + BS], s_s[pl.ds(base, 8), c0 + BS:c0 + 2 * BS],
                                     lt_ref, j, car[:, c0:c0 + BS], car[:, c0 + BS:c0 + 2 * BS], rev)
                s_s[pl.ds(base, 8), c0:c0 + BS] = xr
                s_s[pl.ds(base, 8), c0 + BS:c0 + 2 * BS] = xi
                car[:, c0:c0 + BS] = _last_row(xr, rev)
                car[:, c0 + BS:c0 + 2 * BS] = _last_row(xi, rev)
            return 0

        lax.fori_loop(0, ngr, fgrp, 0)

        row = lax.broadcasted_iota(jnp.int32, (8, BS), 0)

        def agrp(r, _):
            gi = ngr - 1 - r if arev else r
            base = pl.multiple_of(gi * 8, 8)
            pgi = gi + 1 if rev else gi - 1
            inside = jnp.logical_and(pgi >= 0, pgi < ngr)
            pbase = pl.multiple_of(jnp.clip(pgi, 0, ngr - 1) * 8, 8)
            for j in range(NB):
                c0 = j * 2 * BS
                ar, ai = _scan_group(g_s[pl.ds(base, 8), c0:c0 + BS], g_s[pl.ds(base, 8), c0 + BS:c0 + 2 * BS],
                                     la_ref, j, acar[:, c0:c0 + BS], acar[:, c0 + BS:c0 + 2 * BS], arev)
                g_s[pl.ds(base, 8), c0:c0 + BS] = ar
                g_s[pl.ds(base, 8), c0 + BS:c0 + 2 * BS] = ai
                acar[:, c0:c0 + BS] = _last_row(ar, arev)
                acar[:, c0 + BS:c0 + 2 * BS] = _last_row(ai, arev)
                sr, si = s_s[pl.ds(base, 8), c0:c0 + BS], s_s[pl.ds(base, 8), c0 + BS:c0 + 2 * BS]
                edge_r = jnp.where(inside, _last_row(s_s[pl.ds(pbase, 8), c0:c0 + BS], rev), cin_ref[:, c0:c0 + BS])
                edge_i = jnp.where(inside, _last_row(s_s[pl.ds(pbase, 8), c0 + BS:c0 + 2 * BS], rev),
                                   cin_ref[:, c0 + BS:c0 + 2 * BS])
                sh = 7 if rev else 1
                first = 7 if rev else 0
                pr = jnp.where(row == first, edge_r, pltpu.roll(sr, sh, 0))
                pi = jnp.where(row == first, edge_i, pltpu.roll(si, sh, 0))
                dlam_ref[:, c0:c0 + BS] += ar * pr + ai * pi
                dlam_ref[:, c0 + BS:c0 + 2 * BS] += ai * pr - ar * pi
            return 0

        lax.fori_loop(0, ngr, agrp, 0)
        for j in range(NB):
            a_j = g_s[:, j * 2 * BS:(j + 1) * 2 * BS].astype(BF16)
            u_j = u_ref[:, j * LANE:(j + 1) * LANE].astype(BF16)
            du = lax.dot_general(a_j, wb_ref[j], NT, preferred_element_type=F32)
            if has_acc:
                du = du + acc_ref[:, j * LANE:(j + 1) * LANE]
            du_ref[:, j * LANE:(j + 1) * LANE] = du
            dwb_ref[j] += lax.dot_general(u_j, a_j, TN, preferred_element_type=F32)
            dwc_ref[j] += lax.dot_general(s_s[:, j * 2 * BS:(j + 1) * 2 * BS].astype(BF16),
                                          dy_ref[:, j * LANE:(j + 1) * LANE].astype(BF16), TN,
                                          preferred_element_type=F32)

    W2 = NB * 2 * BS
    in_specs = [pl.BlockSpec((tb, SW), lambda b: (tix(b), ucb)), pl.BlockSpec((tb, SW), lambda b: (tix(b), 0)),
                pl.BlockSpec((8, W2), lambda b: (tix(b), 0)),
                pl.BlockSpec(wb.shape, lambda b: (0, 0, 0)), pl.BlockSpec(wc.shape, lambda b: (0, 0, 0)),
                pl.BlockSpec(lt.shape, lambda b: (0, 0, 0, 0)), pl.BlockSpec(lt_adj.shape, lambda b: (0, 0, 0, 0))]
    args = [hsrc, dy, cin, wb, wc, lt, lt_adj]
    if has_acc:
        in_specs.append(pl.BlockSpec((tb, SW), lambda b: (tix(b), 0)))
        args.append(acc)
    return pl.pallas_call(
        body, name=name, grid=(nblk,), in_specs=in_specs,
        out_specs=[pl.BlockSpec((tb, SW), lambda b: (tix(b), 0)),
                   pl.BlockSpec(wb.shape, lambda b: (0, 0, 0)), pl.BlockSpec(wc.shape, lambda b: (0, 0, 0)),
                   pl.BlockSpec((8, W2), lambda b: (0, 0))],
        out_shape=[jax.ShapeDtypeStruct((L, SW), F32), jax.ShapeDtypeStruct(wb.shape, F32),
                   jax.ShapeDtypeStruct(wc.shape, F32), jax.ShapeDtypeStruct((8, W2), F32)],
        scratch_shapes=[pltpu.VMEM((tb, W2), F32), pltpu.VMEM((tb, W2), F32),
                        pltpu.VMEM((8, W2), F32), pltpu.VMEM((8, W2), F32)],
        compiler_params=_cparams(("arbitrary",)),
    )(*args)


def _dotf(a, b, dn=(((1,), (0,)), ((), ()))):
    return lax.dot_general(a, b, dn, precision=HI, preferred_element_type=F32)


def _dotb(a, b, dn=(((1,), (0,)), ((), ()))):
    return lax.dot_general(a.astype(BF16), b.astype(BF16), dn, preferred_element_type=F32)


_NT = (((1,), (1,)), ((), ()))
_TN = (((0,), (0,)), ((), ()))


def _delta_chunk(q, k, v, e, s_in, ig, ib, rev):
    c, hd = q.shape
    ii = lax.broadcasted_iota(jnp.int32, (c, c), 0)
    jj = lax.broadcasted_iota(jnp.int32, (c, c), 1)
    incl = (ii <= jj) if rev else (ii >= jj)
    strict = (ii < jj) if rev else (ii > jj)
    tri = incl.astype(F32)
    sel_g_cc = (lax.broadcasted_iota(jnp.int32, (LANE, c), 0) == ig).astype(F32)
    sel_g_hd = (lax.broadcasted_iota(jnp.int32, (LANE, hd), 0) == ig).astype(F32)
    sel_b_hd = (lax.broadcasted_iota(jnp.int32, (LANE, hd), 0) == ib).astype(F32)
    sel_g_row = (lax.broadcasted_iota(jnp.int32, (c, LANE), 1) == ig).astype(F32)
    g_cc = _dotf(e, sel_g_cc)
    g_hd = _dotf(e, sel_g_hd)
    beta = _dotf(e, sel_b_hd)
    g_row = _dotf(sel_g_row, e, _NT)
    gc_i = _dotf(tri, g_cc)
    gc_j = _dotf(g_row, tri, _NT)
    gc_hd = _dotf(tri, g_hd)
    gtot_hd = _dotf(jnp.ones((c, c), F32), g_hd)
    gtot_s = _dotf(jnp.ones((hd, c), F32), g_hd)
    decay = jnp.where(incl, jnp.exp(jnp.where(incl, gc_i - gc_j, 0.0)), 0.0)
    kb = k * beta
    a = jnp.where(strict, _dotf(kb, k, _NT) * decay, 0.0)
    eye = (ii == jj).astype(F32)
    tinv = eye - a
    p = a
    n = 2
    while n < c:
        p = _dotf(p, p)
        tinv = _dotf(tinv, eye + p)
        n *= 2
    eg = jnp.exp(gc_hd)
    u = _dotf(tinv, v * beta)
    w = _dotf(tinv, kb * eg)
    intra = _dotb(q, k, _NT) * decay
    qd = q * eg
    kd = k * jnp.exp(gtot_hd - gc_hd)
    v_new = u - _dotb(w, s_in)
    o = _dotb(qd, s_in) + _dotb(intra, v_new)
    s_out = s_in * jnp.exp(gtot_s) + _dotb(kd, v_new, _TN)
    return o, s_out


def _delta_fwd(q, k, v, gb, *, voff, dirn, rev, acc, name, d):
    L, H, hd, C = q.shape[0], d['DNH'], d['DNK'], d['CHUNK']
    nc = L // C
    cix = (lambda i: nc - 1 - i) if rev else (lambda i: i)
    has_acc = acc is not None

    def body(*refs):
        if has_acc:
            q_ref, k_ref, v_ref, e_ref, acc_ref, o_ref, ss_ref, st = refs
        else:
            q_ref, k_ref, v_ref, e_ref, o_ref, ss_ref, st = refs
        h = pl.program_id(1)

        @pl.when(pl.program_id(0) == 0)
        def _():
            st[h] = jnp.zeros((hd, hd), F32)

        s_in = st[h]
        ss_ref[0, 0] = s_in
        o, s_out = _delta_chunk(q_ref[...], k_ref[...], v_ref[...], e_ref[...], s_in,
                                dirn * H + h, 2 * H + dirn * H + h, rev)
        if has_acc:
            o = o + acc_ref[...]
        o_ref[...] = o
        st[h] = s_out

    blk = pl.BlockSpec((C, hd), lambda i, h: (cix(i), h))
    vblk = pl.BlockSpec((C, hd), lambda i, h: (cix(i), voff + h))
    in_specs = [blk, blk, vblk, pl.BlockSpec((C, LANE), lambda i, h: (cix(i), 0))]
    args = [q, k, v, gb]
    if has_acc:
        in_specs.append(blk)
        args.append(acc)
    return pl.pallas_call(
        body, name=name, grid=(nc, H), in_specs=in_specs,
        out_specs=[blk, pl.BlockSpec((1, 1, hd, hd), lambda i, h: (cix(i), h, 0, 0))],
        out_shape=[jax.ShapeDtypeStruct((L, H * hd), F32), jax.ShapeDtypeStruct((nc, H, hd, hd), F32)],
        scratch_shapes=[pltpu.VMEM((H, hd, hd), F32)],
        compiler_params=_cparams(("arbitrary", "arbitrary")),
    )(*args)


def _delta_bwd(q, k, v, gb, ssave, do, *, voff, dirn, rev, accs, name, d):
    L, H, hd, C = q.shape[0], d['DNH'], d['DNK'], d['CHUNK']
    nc = L // C
    cix = (lambda i: i) if rev else (lambda i: nc - 1 - i)
    has_acc = accs is not None

    def body(*refs):
        if has_acc:
            (q_ref, k_ref, v_ref, e_ref, ss_ref, do_ref, aq_ref, ak_ref, av_ref, ae_ref,
             dq_ref, dk_ref, dv_ref, de_ref, dst) = refs
        else:
            q_ref, k_ref, v_ref, e_ref, ss_ref, do_ref, dq_ref, dk_ref, dv_ref, de_ref, dst = refs
        h = pl.program_id(1)

        @pl.when(pl.program_id(0) == 0)
        def _():
            dst[h] = jnp.zeros((hd, hd), F32)

        fn = functools.partial(_delta_chunk, ig=dirn * H + h, ib=2 * H + dirn * H + h, rev=rev)
        _, vjp = jax.vjp(fn, q_ref[...], k_ref[...], v_ref[...], e_ref[...], ss_ref[0, 0])
        dq, dk, dv, de, ds = vjp((do_ref[...], dst[h]))
        dst[h] = ds
        if has_acc:
            dq, dk, dv = dq + aq_ref[...], dk + ak_ref[...], dv + av_ref[...]
        dq_ref[...] = dq
        dk_ref[...] = dk
        dv_ref[...] = dv

        @pl.when(h == 0)
        def _():
            de_ref[...] = ae_ref[...] if has_acc else jnp.zeros_like(de_ref)

        de_ref[...] += de

    blk = pl.BlockSpec((C, hd), lambda i, h: (cix(i), h))
    eblk = pl.BlockSpec((C, LANE), lambda i, h: (cix(i), 0))
    vblk = pl.BlockSpec((C, hd), lambda i, h: (cix(i), voff + h))
    in_specs = [blk, blk, vblk, eblk, pl.BlockSpec((1, 1, hd, hd), lambda i, h: (cix(i), h, 0, 0)), blk]
    args = [q, k, v, gb, ssave, do]
    if has_acc:
        in_specs += [blk, blk, blk, eblk]
        args += list(accs)
    return pl.pallas_call(
        body, name=name, grid=(nc, H), in_specs=in_specs, out_specs=[blk, blk, blk, eblk],
        out_shape=[jax.ShapeDtypeStruct((L, H * hd), F32)] * 3 + [jax.ShapeDtypeStruct((L, LANE), F32)],
        scratch_shapes=[pltpu.VMEM((H, hd, hd), F32)],
        compiler_params=_cparams(("arbitrary", "arbitrary")),
    )(*args)


def _conv_specs(tm, w, cb0, nrb, L):
    hb = tm // 8
    last8 = L // 8 - 1
    cur = pl.BlockSpec((tm, w), lambda s, i: (i, cb0 + s))
    prev = pl.BlockSpec((8, w), lambda s, i: (jnp.maximum(i * hb - 1, 0), cb0 + s))
    nxt = pl.BlockSpec((8, w), lambda s, i: (jnp.minimum((i + 1) * hb, last8), cb0 + s))
    return [prev, cur, nxt]


def _fill_halo(dst, prev_ref, cur_ref, next_ref, i, nrb, tm):
    dst[pl.ds(0, 8), :] = jnp.where(i > 0, prev_ref[...], 0.0)
    dst[pl.ds(8, tm), :] = cur_ref[...]
    dst[pl.ds(8 + tm, 8), :] = jnp.where(i < nrb - 1, next_ref[...], 0.0)


def _conv_fwd(hsrc, cb0, wt, *, tm, name, d):
    L, w, K = hsrc.shape[0], d['DW'], d['CONV']
    tm = _pick(tm, L)
    nrb = L // tm

    def body(prev_ref, cur_ref, next_ref, w_ref, o_ref, xs):
        i = pl.program_id(1)
        _fill_halo(xs, prev_ref, cur_ref, next_ref, i, nrb, tm)
        y = jnp.zeros((tm, w), F32)
        for kk in range(K):
            y = y + w_ref[0, pl.ds(kk, 1), :] * xs[pl.ds(8 - K // 2 + kk, tm), :]
        o_ref[...] = _silu(y)

    return pl.pallas_call(
        body, name=name, grid=(3, nrb),
        in_specs=_conv_specs(tm, w, cb0, nrb, L) + [pl.BlockSpec((1, 8, w), lambda s, i: (s, 0, 0))],
        out_specs=pl.BlockSpec((tm, w), lambda s, i: (i, s)),
        out_shape=jax.ShapeDtypeStruct((L, 3 * w), F32),
        scratch_shapes=[pltpu.VMEM((tm + 16, w), F32)],
        compiler_params=_cparams(("parallel", "parallel")),
    )(hsrc, hsrc, hsrc, wt)


def _conv_bwd(hsrc, cb0, wt, dact, *, tm, name, d):
    L, w, K = hsrc.shape[0], d['DW'], d['CONV']
    tm = _pick(tm, L)
    nrb = L // tm
    half = K // 2

    def body(xp_ref, xc_ref, xn_ref, gp_ref, gc_ref, gn_ref, w_ref, dx_ref, dw_ref, xs, gs, dys):
        i = pl.program_id(1)
        _fill_halo(xs, xp_ref, xc_ref, xn_ref, i, nrb, tm)
        _fill_halo(gs, gp_ref, gc_ref, gn_ref, i, nrb, tm)
        y = jnp.zeros((tm + 8, w), F32)
        for kk in range(K):
            y = y + w_ref[0, pl.ds(kk, 1), :] * xs[pl.ds(4 - half + kk, tm + 8), :]
        sg = jax.nn.sigmoid(y)
        dys[...] = gs[pl.ds(4, tm + 8), :] * (sg * (1.0 + y * (1.0 - sg)))
        dx = jnp.zeros((tm, w), F32)
        for kk in range(K):
            dx = dx + w_ref[0, pl.ds(kk, 1), :] * dys[pl.ds(4 + half - kk, tm), :]
        dx_ref[...] = dx

        @pl.when(i == 0)
        def _():
            dw_ref[...] = jnp.zeros_like(dw_ref)

        dy = dys[pl.ds(4, tm), :]
        for kk in range(K):
            dw_ref[0, pl.ds(kk, 1), :] += jnp.sum(dy * xs[pl.ds(8 - half + kk, tm), :], axis=0, keepdims=True)

    gspecs = _conv_specs(tm, w, 0, nrb, L)
    return pl.pallas_call(
        body, name=name, grid=(3, nrb),
        in_specs=_conv_specs(tm, w, cb0, nrb, L) + gspecs + [pl.BlockSpec((1, 8, w), lambda s, i: (s, 0, 0))],
        out_specs=[pl.BlockSpec((tm, w), lambda s, i: (i, s)), pl.BlockSpec((1, 8, w), lambda s, i: (s, 0, 0))],
        out_shape=[jax.ShapeDtypeStruct((L, 3 * w), F32), jax.ShapeDtypeStruct((3, 8, w), F32)],
        scratch_shapes=[pltpu.VMEM((tm + 16, w), F32), pltpu.VMEM((tm + 16, w), F32), pltpu.VMEM((tm + 8, w), F32)],
        compiler_params=_cparams(("parallel", "arbitrary")),
    )(hsrc, hsrc, hsrc, dact, dact, dact, wt)


def _wide(v, n):
    return v if n == LANE else jnp.tile(v, (1, n // LANE))


def _attn_fwd(qh, kh, vh, *, tq, tk, name, d):
    L, H, KVH, hd = qh.shape[0], d['AH'], d['AKV'], d['AD']
    grp = H // KVH
    tq, tk = _pick(tq, L), _pick(tk, L)
    nk = L // tk

    def body(q_ref, k_ref, v_ref, o_ref, lse_ref, m_s, l_s, acc):
        j = pl.program_id(2)

        @pl.when(j == 0)
        def _():
            m_s[...] = jnp.full_like(m_s, -1e30)
            l_s[...] = jnp.zeros_like(l_s)
            acc[...] = jnp.zeros_like(acc)

        s = lax.dot_general(q_ref[...], k_ref[...], _NT, preferred_element_type=F32)
        m_old = m_s[...]
        m_new = jnp.maximum(m_old, jnp.max(s, axis=-1, keepdims=True))
        alpha = jnp.exp(m_old - m_new)
        p = jnp.exp(s - _wide(m_new, tk))
        l_s[...] = alpha * l_s[...] + jnp.sum(p, axis=-1, keepdims=True)
        acc[...] = alpha * acc[...] + jnp.dot(p.astype(BF16), v_ref[...], preferred_element_type=F32)
        m_s[...] = m_new

        @pl.when(j == nk - 1)
        def _():
            o_ref[...] = acc[...] / l_s[...]
            lse_ref[...] = m_s[...] + jnp.log(l_s[...])

    qspec = pl.BlockSpec((tq, hd), lambda h, i, j: (i, h))
    kspec = pl.BlockSpec((tk, hd), lambda h, i, j: (j, h // grp))
    return pl.pallas_call(
        body, name=name, grid=(H, L // tq, nk), in_specs=[qspec, kspec, kspec], out_specs=[qspec, qspec],
        out_shape=[jax.ShapeDtypeStruct((L, H * hd), F32), jax.ShapeDtypeStruct((L, H * hd), F32)],
        scratch_shapes=[pltpu.VMEM((tq, hd), F32), pltpu.VMEM((tq, hd), F32), pltpu.VMEM((tq, hd), F32)],
        compiler_params=_cparams(("parallel", "parallel", "arbitrary")),
    )(qh, kh, vh)


def _attn_dkv(qh, kh, vh, do, lse, delta, *, tq, tk, name, d):
    L, H, KVH, hd = qh.shape[0], d['AH'], d['AKV'], d['AD']
    grp = H // KVH
    tq, tk = _pick(tq, L), _pick(tk, L)
    nq = L // tq

    def body(q_ref, k_ref, v_ref, do_ref, lse_ref, dl_ref, dk_ref, dv_ref, dk_s, dv_s):
        g, i = pl.program_id(2), pl.program_id(3)

        @pl.when(jnp.logical_and(g == 0, i == 0))
        def _():
            dk_s[...] = jnp.zeros_like(dk_s)
            dv_s[...] = jnp.zeros_like(dv_s)

        q, do_ = q_ref[...], do_ref[...].astype(BF16)
        s = lax.dot_general(q, k_ref[...], _NT, preferred_element_type=F32)
        p = jnp.exp(s - _wide(lse_ref[...], tk))
        dv_s[...] += lax.dot_general(p.astype(BF16), do_, _TN, preferred_element_type=F32)
        dp = lax.dot_general(do_, v_ref[...], _NT, preferred_element_type=F32)
        ds = (p * (dp - _wide(dl_ref[...], tk))).astype(BF16)
        dk_s[...] += lax.dot_general(ds, q, _TN, preferred_element_type=F32)

        @pl.when(jnp.logical_and(g == grp - 1, i == nq - 1))
        def _():
            dk_ref[...] = dk_s[...]
            dv_ref[...] = dv_s[...]

    qspec = pl.BlockSpec((tq, hd), lambda kv, j, g, i: (i, kv * grp + g))
    kspec = pl.BlockSpec((tk, hd), lambda kv, j, g, i: (j, kv))
    return pl.pallas_call(
        body, name=name, grid=(KVH, L // tk, grp, nq),
        in_specs=[qspec, kspec, kspec, qspec, qspec, qspec], out_specs=[kspec, kspec],
        out_shape=[jax.ShapeDtypeStruct((L, KVH * hd), F32)] * 2,
        scratch_shapes=[pltpu.VMEM((tk, hd), F32), pltpu.VMEM((tk, hd), F32)],
        compiler_params=_cparams(("parallel", "parallel", "arbitrary", "arbitrary")),
    )(qh, kh, vh, do, lse, delta)


def _attn_dq(qh, kh, vh, do, lse, delta, *, tq, tk, name, d):
    L, H, KVH, hd = qh.shape[0], d['AH'], d['AKV'], d['AD']
    grp = H // KVH
    tq, tk = _pick(tq, L), _pick(tk, L)
    nk = L // tk

    def body(q_ref, k_ref, v_ref, do_ref, lse_ref, dl_ref, dq_ref, dq_s):
        j = pl.program_id(2)

        @pl.when(j == 0)
        def _():
            dq_s[...] = jnp.zeros_like(dq_s)

        k = k_ref[...]
        s = lax.dot_general(q_ref[...], k, _NT, preferred_element_type=F32)
        p = jnp.exp(s - _wide(lse_ref[...], tk))
        dp = lax.dot_general(do_ref[...].astype(BF16), v_ref[...], _NT, preferred_element_type=F32)
        ds = (p * (dp - _wide(dl_ref[...], tk))).astype(BF16)
        dq_s[...] += jnp.dot(ds, k, preferred_element_type=F32)

        @pl.when(j == nk - 1)
        def _():
            dq_ref[...] = dq_s[...]

    qspec = pl.BlockSpec((tq, hd), lambda h, i, j: (i, h))
    kspec = pl.BlockSpec((tk, hd), lambda h, i, j: (j, h // grp))
    return pl.pallas_call(
        body, name=name, grid=(H, L // tq, nk),
        in_specs=[qspec, kspec, kspec, qspec, qspec, qspec], out_specs=qspec,
        out_shape=jax.ShapeDtypeStruct((L, H * hd), F32),
        scratch_shapes=[pltpu.VMEM((tq, hd), F32)],
        compiler_params=_cparams(("parallel", "parallel", "arbitrary")),
    )(qh, kh, vh, do, lse, delta)


def _loss_grad(x, g, tgt, *, tm, name):
    L, D = x.shape
    tm = _pick(tm, L)

    def body(x_ref, g_ref, t_ref, loss_ref, dx_ref, dg_ref):
        def f(xv, gv):
            err = _rms(xv, gv) - t_ref[...]
            return 0.5 * jnp.sum(jnp.mean(err * err, axis=-1, keepdims=True))

        val, vjp = jax.vjp(f, x_ref[...], g_ref[...])
        dx, dg = vjp(jnp.ones((), F32))
        dx_ref[...] = dx

        @pl.when(pl.program_id(0) == 0)
        def _():
            loss_ref[...] = jnp.zeros_like(loss_ref)
            dg_ref[...] = jnp.zeros_like(dg_ref)

        loss_ref[...] += val
        dg_ref[...] += dg

    return pl.pallas_call(
        body, name=name, grid=(L // tm,),
        in_specs=[pl.BlockSpec((tm, D), lambda i: (i, 0)), pl.BlockSpec((1, D), lambda i: (0, 0)),
                  pl.BlockSpec((tm, D), lambda i: (i, 0))],
        out_specs=[pl.BlockSpec((8, LANE), lambda i: (0, 0)), pl.BlockSpec((tm, D), lambda i: (i, 0)),
                   pl.BlockSpec((1, D), lambda i: (0, 0))],
        out_shape=[jax.ShapeDtypeStruct((8, LANE), F32), jax.ShapeDtypeStruct((L, D), F32),
                   jax.ShapeDtypeStruct((1, D), F32)],
        compiler_params=_cparams(("arbitrary",)),
    )(x, g, tgt)


def _sum_slots(recv, *, tr, name):
    n, R, W = recv.shape
    tr = _pick(tr, R)

    def body(r_ref, o_ref):
        s = r_ref[0]
        for i in range(1, n):
            s = s + r_ref[i]
        o_ref[...] = s

    return pl.pallas_call(
        body, name=name, grid=(R // tr,),
        in_specs=[pl.BlockSpec((n, tr, W), lambda i: (0, i, 0))], out_specs=pl.BlockSpec((tr, W), lambda i: (i, 0)),
        out_shape=jax.ShapeDtypeStruct((R, W), F32), compiler_params=_cparams(("parallel",)),
    )(recv)


def _adamw(w, g, m, v, *, tr, name):
    R, W = w.shape
    tr = _pick(tr, R)
    c1 = 1.0 - ADAM_B1 ** ADAM_STEP
    c2 = 1.0 - ADAM_B2 ** ADAM_STEP

    def body(w_ref, g_ref, m_ref, v_ref, d_ref, nm_ref, nv_ref):
        gv = g_ref[...]
        nm = ADAM_B1 * m_ref[...] + (1.0 - ADAM_B1) * gv
        nv = ADAM_B2 * v_ref[...] + (1.0 - ADAM_B2) * (gv * gv)
        d_ref[...] = -ADAM_LR * ((nm / c1) / (jnp.sqrt(nv / c2) + ADAM_EPS) + ADAM_WD * w_ref[...])
        nm_ref[...] = nm
        nv_ref[...] = nv

    spec = pl.BlockSpec((tr, W), lambda i: (i, 0))
    return pl.pallas_call(
        body, name=name, grid=(R // tr,), in_specs=[spec] * 4, out_specs=[spec] * 3,
        out_shape=[jax.ShapeDtypeStruct((R, W), F32)] * 3, compiler_params=_cparams(("parallel",)),
    )(w, g, m, v)


_MESH = pl.DeviceIdType.MESH


def _all_gather(xs):
    R, W = xs.shape

    def body(x_ref, out_ref, send_sems, recv_sems, local_sem):
        x, y, c = lax.axis_index("x"), lax.axis_index("y"), lax.axis_index("c")
        me, sibling = (x, y, c), (x, y, 1 - c)
        chips = [(1 - x, y), (x, 1 - y), (1 - x, 1 - y)]

        def slot(px, py, pc):
            return out_ref.at[4 * px + 2 * py + pc]

        def copy(k, block, to, src=None):
            return pltpu.make_async_remote_copy(
                src_ref=slot(*block) if src is None else src, dst_ref=slot(*block),
                send_sem=send_sems.at[k], recv_sem=recv_sems.at[k], device_id=to, device_id_type=_MESH)

        mine = pltpu.make_async_copy(x_ref, slot(*me), local_sem)
        mine.start()
        first = [copy(0, me, sibling, src=x_ref)]
        first += [copy(1 + j, me, (*chip, c), src=x_ref) for j, chip in enumerate(chips)]
        for cp in first:
            cp.start()
        passed = [copy(4 + j, (*chip, c), sibling) for j, chip in enumerate(chips)]
        for j, chip in enumerate(chips):
            copy(1 + j, (*chip, c), me).wait_recv()
            passed[j].start()
        copy(0, sibling, me).wait_recv()
        for j, chip in enumerate(chips):
            copy(4 + j, (*chip, 1 - c), me).wait_recv()
        for cp in first + passed:
            cp.wait_send()
        mine.wait()

    return pl.pallas_call(
        body, name="weights_all_gather",
        out_shape=jax.ShapeDtypeStruct((N_DEV, R, W), xs.dtype),
        in_specs=[pl.BlockSpec(memory_space=pl.ANY)], out_specs=pl.BlockSpec(memory_space=pl.ANY),
        scratch_shapes=[pltpu.SemaphoreType.DMA((7,)), pltpu.SemaphoreType.DMA((7,)), pltpu.SemaphoreType.DMA],
    )(xs)


def _all_to_all(gs):
    n, R, W = gs.shape

    def body(g_ref, out_ref, send_sems, recv_sems, local_sem):
        x, y, c = lax.axis_index("x"), lax.axis_index("y"), lax.axis_index("c")
        me = 4 * x + 2 * y + c

        def peer(mask):
            return (x ^ (mask >> 2), y ^ ((mask >> 1) & 1), c ^ (mask & 1))

        def copy(mask):
            px, py, pc = peer(mask)
            return pltpu.make_async_remote_copy(
                src_ref=g_ref.at[4 * px + 2 * py + pc], dst_ref=out_ref.at[me],
                send_sem=send_sems.at[mask - 1], recv_sem=recv_sems.at[mask - 1],
                device_id=(px, py, pc), device_id_type=_MESH)

        def arrival(mask):
            px, py, pc = peer(mask)
            return pltpu.make_async_remote_copy(
                src_ref=g_ref.at[me], dst_ref=out_ref.at[4 * px + 2 * py + pc],
                send_sem=send_sems.at[mask - 1], recv_sem=recv_sems.at[mask - 1],
                device_id=(px, py, pc), device_id_type=_MESH)

        mine = pltpu.make_async_copy(g_ref.at[me], out_ref.at[me], local_sem)
        mine.start()
        sends = [copy(mask) for mask in range(1, n)]
        for cp in sends:
            cp.start()
        for mask in range(1, n):
            arrival(mask).wait_recv()
        for cp in sends:
            cp.wait_send()
        mine.wait()

    return pl.pallas_call(
        body, name="grads_all_to_all",
        out_shape=jax.ShapeDtypeStruct((n, R, W), gs.dtype),
        in_specs=[pl.BlockSpec(memory_space=pl.ANY)], out_specs=pl.BlockSpec(memory_space=pl.ANY),
        scratch_shapes=[pltpu.SemaphoreType.DMA((n - 1,)), pltpu.SemaphoreType.DMA((n - 1,)),
                        pltpu.SemaphoreType.DMA],
    )(gs)


def _rows_of(shape):
    return -(-int(np.prod(shape)) // PACK_W)


def _pack(arrs, dtype, lead=0, total_rows=None):
    pieces = []
    for a in arrs:
        f = a.reshape(a.shape[:lead] + (-1,)).astype(dtype)
        pad = (-f.shape[-1]) % PACK_W
        if pad:
            f = jnp.pad(f, [(0, 0)] * lead + [(0, pad)])
        pieces.append(f.reshape(a.shape[:lead] + (-1, PACK_W)))
    buf = jnp.concatenate(pieces, axis=lead)
    if total_rows is not None and buf.shape[lead] < total_rows:
        buf = jnp.pad(buf, [(0, 0)] * lead + [(0, total_rows - buf.shape[lead]), (0, 0)])
    return buf


def _unpack(buf, shapes, lead=0):
    out, r = [], 0
    for shp in shapes:
        n, rows = int(np.prod(shp)), _rows_of(shp)
        piece = buf[(slice(None),) * lead + (slice(r, r + rows),)]
        piece = piece.reshape(buf.shape[:lead] + (-1,))[..., :n]
        out.append(piece.reshape(buf.shape[:lead] + tuple(shp)))
        r += rows
    return out


def _to_full(name, parts):
    if name == 'w_in':
        dep, D, w = parts.shape[1:]
        return jnp.transpose(parts, (1, 2, 0, 3)).reshape(dep, D, N_DEV * w)
    dep, r = parts.shape[1:3]
    return jnp.transpose(parts, (1, 0) + tuple(range(2, parts.ndim))).reshape((dep, N_DEV * r) + parts.shape[3:])


def _to_slabs(name, full):
    if name == 'w_in':
        dep, D, w = full.shape
        return jnp.transpose(full.reshape(dep, D, N_DEV, w // N_DEV), (2, 0, 1, 3))
    dep, r = full.shape[:2]
    t = full.reshape((dep, N_DEV, r // N_DEV) + full.shape[2:])
    return jnp.transpose(t, (1, 0) + tuple(range(2, t.ndim)))


def _w_in_to_layout(w, seg, rseg, nh2):
    D = w.shape[0]
    cols, off = [], 0
    names = sorted([k for k in seg if not k.startswith('_')], key=lambda k: seg[k][0])
    for nm in names:
        o, wd = seg[nm]
        if o > off:
            cols.append(jnp.zeros((D, o - off), w.dtype))
        if nm == 'dadb':
            ra, rb = rseg['da'][0], rseg['db'][0]
            cols += [w[:, ra:ra + nh2], w[:, rb:rb + nh2], jnp.zeros((D, wd - 2 * nh2), w.dtype)]
        else:
            ro = rseg[nm][0]
            cols.append(w[:, ro:ro + wd])
        off = o + wd
    if seg['_total'] > off:
        cols.append(jnp.zeros((D, seg['_total'] - off), w.dtype))
    return jnp.concatenate(cols, axis=1)


def _w_in_from_layout(dw, seg, rseg, nh2):
    cols = []
    for nm in sorted(rseg, key=lambda k: rseg[k][0] if not k.startswith('_') else 1 << 60):
        if nm.startswith('_'):
            continue
        if nm == 'da':
            o = seg['dadb'][0]
            cols.append(dw[:, o:o + nh2])
        elif nm == 'db':
            o = seg['dadb'][0] + nh2
            cols.append(dw[:, o:o + nh2])
        else:
            o, wd = seg[nm]
            cols.append(dw[:, o:o + wd])
    return jnp.concatenate(cols, axis=1)


def _assemble_dh(pieces, seg, L):
    cols, off = [], 0
    for nm in sorted(pieces, key=lambda k: seg[k][0]):
        o = seg[nm][0]
        if o > off:
            cols.append(jnp.zeros((L, o - off), F32))
        cols.append(pieces[nm])
        off = o + pieces[nm].shape[1]
    if seg['_total'] > off:
        cols.append(jnp.zeros((L, seg['_total'] - off), F32))
    return jnp.concatenate(cols, axis=1)


def _lane_pad(v):
    v = v.reshape(1, -1)
    return jnp.pad(v, ((0, 0), (0, LANE - v.shape[1])))


def _rope_tables(L, c):
    rows = L // c['GRID_W']
    row = jnp.repeat(jnp.arange(rows), c['GRID_W']).astype(F32)
    col = jnp.tile(jnp.arange(c['GRID_W']), rows).astype(F32)
    axis_dim = c['AD'] // 2
    freqs = c['ROPE_THETA'] ** (-jnp.arange(0, axis_dim, 2, dtype=F32) / axis_dim)
    ang = jnp.concatenate([row[:, None] * freqs, col[:, None] * freqs], axis=-1)
    cosf = jnp.repeat(jnp.cos(ang), 2, axis=1)
    sn = jnp.sin(ang)
    sins = jnp.stack([-sn, sn], axis=-1).reshape(L, c['AD'])
    idx = np.arange(c['AD'])
    perm = np.zeros((c['AD'], c['AD']), np.float32)
    perm[idx, idx ^ 1] = 1.0
    return cosf, sins, jnp.asarray(perm)


def _s5_dir_params(a, l, dr):
    return (a['ssm_a_re'][l, dr], a['ssm_a_im'][l, dr], a['ssm_log_step'][l, dr], a['ssm_b_re'][l, dr],
            a['ssm_b_im'][l, dr], a['ssm_c_re'][l, dr], a['ssm_c_im'][l, dr])


def _layer_fwd(x, mem, l, wt, a, rope, c, d, seg):
    L, D = x.shape
    SW, DW, AW, AKW, MW, H = d['SW'], d['DW'], d['AW'], d['AKW'], d['MW'], d['DNH']
    cb = lambda nm: seg[nm][0] // seg[nm][1]
    sv = {'x': x}
    p = f"l{l}_"
    sv['g_norm'] = a['norm_g'][l][None, :]
    xn, = _rowwise(_f_norm, [(x, D, 0)], [sv['g_norm']], [(D, BF16)], tm=256, name=p + "norm")
    h = _mm(xn, wt['wp'], name=p + "in_proj")
    sv['xn'], sv['h'] = xn, h

    ysum, sv['s5'] = None, []
    for dr in range(2):
        wb, wc, lr, li = _s5_prep(*_s5_dir_params(a, l, dr), d)
        wb16, wc16 = wb.astype(BF16), wc.astype(BF16)
        lt = _s5_tables(lr, li, bool(dr), False)
        ysum, cin = _s5_fwd(h, cb('u_a'), wb16, wc16, lt, rev=bool(dr), acc=ysum, tb=256, name=p + f"s5_fwd{dr}", d=d)
        sv['s5'].append((wb16, wc16, lt, _s5_tables(lr, li, not bool(dr), True), cin))
    sv['ysum'] = ysum
    sv['s5_par'] = [a['ssm_d'][l][None, :], wt['w_glu'], a['ssm_b_glu'][l][None, :]]
    sv['s5_rows'] = [(ysum, SW, 0), (h, SW, cb('u_a')), (h, SW, cb('z_a'))]
    y_a, = _rowwise(_f_s5tail, sv['s5_rows'], sv['s5_par'], [(SW, F32)], tm=256, name=p + "s5_tail")

    act = _conv_fwd(h, cb('dq'), wt['conv'], tm=256, name=p + "dn_conv", d=d)
    sv['act'] = act
    sv['dn_par'] = [_lane_pad(a['dn_a_log'][l]), _lane_pad(a['dn_dt_bias'][l])]
    sv['dn_rows'] = [(act, DW, 0), (act, DW, 1), (h, LANE, seg['dadb'][0] // LANE)]
    qn, kn, gb = _rowwise(_make_f_dnpre(H, d['DNK']), sv['dn_rows'], sv['dn_par'],
                          [(DW, F32), (DW, F32), (LANE, F32)], tm=256, name=p + "dn_pre")
    sv['qn'], sv['kn'], sv['gb'] = qn, kn, gb
    o_dn, sv['dn_state'] = None, []
    for dr in range(2):
        o_dn, ss = _delta_fwd(qn, kn, act, gb, voff=2 * H, dirn=dr, rev=bool(dr), acc=o_dn,
                              name=p + f"dn_fwd{dr}", d=d)
        sv['dn_state'].append(ss)
    sv['dnpost_rows'] = [(o_dn, DW, 0), (h, DW, cb('z_b'))]
    sv['dnpost_par'] = [a['dn_norm_g'][l][None, :]]
    y_b, = _rowwise(_make_f_dnpost(d['DNK']), sv['dnpost_rows'], sv['dnpost_par'], [(DW, F32)], tm=256,
                    name=p + "dn_post")

    cosf, sins, perm = rope
    sv['att_par'] = [perm, a['attn_q_norm'][l][None, :], a['attn_k_norm'][l][None, :]]
    qh, kh, vh = _rowwise(_make_f_attpre(d['AD'], True),
                          [(h, AW, cb('aq')), (h, AKW, cb('ak')), (h, AKW, cb('av')), (cosf, d['AD'], 0),
                           (sins, d['AD'], 0)], sv['att_par'], [(AW, BF16), (AKW, BF16), (AKW, BF16)],
                          tm=256, name=p + "att_pre")
    o_att, lse = _attn_fwd(qh, kh, vh, tq=512, tk=512, name=p + "att_fwd", d=d)
    sv['qh'], sv['kh'], sv['vh'], sv['o_att'], sv['lse'] = qh, kh, vh, o_att, lse
    y_c, = _rowwise(_f_gate, [(o_att, AW, 0), (h, AW, cb('z_c'))], [], [(AW, F32)], tm=256, name=p + "att_post")

    sv['g_mem'] = a['mem_norm_g'][l][None, :]
    memn, = _rowwise(_f_norm, [(mem, D, 0)], [sv['g_mem']], [(D, BF16)], tm=256, name=p + "mem_norm")
    kv = _mm(memn, wt['w_mem_kv'], name=p + "mem_kv")
    sv['memn'], sv['kv'] = memn, kv
    y_m, = _rowwise(_make_f_mem(d['MH'], d['MD']), [(h, MW, cb('mq')), (h, MW, cb('z_m'))], [kv], [(MW, F32)],
                    tm=256, name=p + "mem_attn")

    ys = [y_a, y_b, y_c, y_m]
    ps = [_mm(y, wb_, name=p + f"branch_proj{i}") for i, (y, wb_) in enumerate(zip(ys, wt['w_branch']))]
    gcb = seg['gates'][0] // D
    sv['merge_rows'] = [(pp, D, 0) for pp in ps] + [(h, D, gcb + i) for i in range(4)]
    merged, = _rowwise(_f_merge, sv['merge_rows'], [], [(D, BF16)], tm=128, name=p + "merge")
    sv['ys'], sv['merged'] = ys, merged
    return _mm(merged, wt['w_out'], add=x, name=p + "out_proj"), sv


def _layer_bwd(dx, mem, l, wt, a, rope, sv, c, d, seg):
    L, D = dx.shape
    SW, DW, AW, AKW, MW, H = d['SW'], d['DW'], d['AW'], d['AKW'], d['MW'], d['DNH']
    cb = lambda nm: seg[nm][0] // seg[nm][1]
    p = f"l{l}_"
    h = sv['h']
    gr = {}
    dmerged = _mm(dx, wt['w_out'], tb=True, name=p + "d_merged")
    gr['w_out'] = _mm(sv['merged'], dx, ta=True, name=p + "dw_out")
    dmr, _ = _rowwise_bwd(_f_merge, sv['merge_rows'], [], [[(dmerged, D, 0)]], [True] * 8, [], tm=128,
                          name=p + "merge_bwd")
    dps, dgates = dmr[:4], dmr[4:]
    dys = [_mm(dp, wb_, tb=True, name=p + f"d_branch{i}") for i, (dp, wb_) in enumerate(zip(dps, wt['w_branch']))]
    gr['w_branch'] = jnp.concatenate(
        [_mm(y, dp, ta=True, name=p + f"dw_branch{i}") for i, (y, dp) in enumerate(zip(sv['ys'], dps))], axis=0)

    (dmq, dzm), (dkv,) = _rowwise_bwd(_make_f_mem(d['MH'], d['MD']), [(h, MW, cb('mq')), (h, MW, cb('z_m'))],
                                      [sv['kv']], [[(dys[3], MW, 0)]], [True, True], [True], tm=256,
                                      name=p + "mem_attn_bwd")
    gr['w_mem_kv'] = _mm(sv['memn'], dkv, ta=True, name=p + "dw_mem_kv")
    dmemn = _mm(dkv, wt['w_mem_kv'], tb=True, name=p + "d_memn")
    _, (dg_mem,) = _rowwise_bwd(_f_norm, [(mem, D, 0)], [sv['g_mem']], [[(dmemn, D, 0)]], [False], [True], tm=256,
                                name=p + "mem_norm_bwd")
    gr['mem_norm_g'] = dg_mem[0]

    (do_att, dzc), _ = _rowwise_bwd(_f_gate, [(sv['o_att'], AW, 0), (h, AW, cb('z_c'))], [], [[(dys[2], AW, 0)]],
                                    [True, True], [], tm=256, name=p + "att_post_bwd")
    delta, = _rowwise(_make_f_delta(d['AD']), [(do_att, AW, 0), (sv['o_att'], AW, 0)], [], [(AW, F32)], tm=256,
                      name=p + "att_delta")
    att_in = (sv['qh'], sv['kh'], sv['vh'], do_att, sv['lse'], delta)
    dkh, dvh = _attn_dkv(*att_in, tq=512, tk=512, name=p + "att_dkv", d=d)
    dqh = _attn_dq(*att_in, tq=512, tk=512, name=p + "att_dq", d=d)
    cosf, sins, _ = rope
    (daq, dak), (dqg, dkg) = _rowwise_bwd(
        _make_f_attpre(d['AD'], False),
        [(h, AW, cb('aq')), (h, AKW, cb('ak')), (cosf, d['AD'], 0), (sins, d['AD'], 0)], sv['att_par'],
        [[(dqh, AW, 0)], [(dkh, AKW, 0)]], [True, True, False, False], [False, True, True], tm=256,
        name=p + "att_pre_bwd")
    gr['attn_q_norm'], gr['attn_k_norm'] = dqg[0], dkg[0]

    (do_dn, dzb), (dng,) = _rowwise_bwd(_make_f_dnpost(d['DNK']), sv['dnpost_rows'], sv['dnpost_par'],
                                        [[(dys[1], DW, 0)]], [True, True], [True], tm=256, name=p + "dn_post_bwd")
    gr['dn_norm_g'] = dng[0]
    accs = None
    for dr in range(2):
        accs = _delta_bwd(sv['qn'], sv['kn'], sv['act'], sv['gb'], sv['dn_state'][dr], do_dn, voff=2 * H, dirn=dr,
                          rev=bool(dr), accs=accs, name=p + f"dn_bwd{dr}", d=d)
    dqn, dkn, dvc, dgb = accs
    (dqc, dkc, ddadb), (dalog, ddtb) = _rowwise_bwd(
        _make_f_dnpre(H, d['DNK']), sv['dn_rows'], sv['dn_par'],
        [[(dqn, DW, 0)], [(dkn, DW, 0)], [(dgb, LANE, 0)]], [True] * 3, [True, True], tm=256, name=p + "dn_pre_bwd")
    gr['dn_a_log'] = dalog[0, :2 * H].reshape(2, H)
    gr['dn_dt_bias'] = ddtb[0, :2 * H].reshape(2, H)
    dconv_x, dconv_w = _conv_bwd(h, cb('dq'), wt['conv'], jnp.concatenate([dqc, dkc, dvc], axis=1), tm=256,
                                 name=p + "dn_conv_bwd", d=d)
    gr['dn_conv'] = jnp.transpose(dconv_w[:, :c['CONV'], :], (0, 2, 1)).reshape(3 * DW, c['CONV'])

    (dysum, du, dza), (dd, dwglu, dbglu) = _rowwise_bwd(_f_s5tail, sv['s5_rows'], sv['s5_par'], [[(dys[0], SW, 0)]],
                                                        [True] * 3, [True] * 3, tm=256, name=p + "s5_tail_bwd")
    gr['ssm_d'], gr['ssm_w_glu'], gr['ssm_b_glu'] = dd[0], dwglu, dbglu[0]
    s5g = []
    for dr in range(2):
        wb16, wc16, lt, lt_adj, cin = sv['s5'][dr]
        du, dwb, dwc, dlam = _s5_bwd(h, cb('u_a'), dysum, cin, wb16, wc16, lt, lt_adj, rev=bool(dr), acc=du, tb=256,
                                     name=p + f"s5_bwd{dr}", d=d)
        dl = jnp.sum(dlam, axis=0).reshape(d['NB'], 2, d['BS'])
        _, prep_vjp = jax.vjp(lambda *pp: _s5_prep(*pp, d), *_s5_dir_params(a, l, dr))
        s5g.append(prep_vjp((dwb, dwc, dl[:, 0], dl[:, 1])))
    for i, nm in enumerate(['ssm_a_re', 'ssm_a_im', 'ssm_log_step', 'ssm_b_re', 'ssm_b_im', 'ssm_c_re', 'ssm_c_im']):
        gr[nm] = jnp.stack([s5g[0][i], s5g[1][i]], axis=0)

    dh = _assemble_dh({'u_a': du, 'z_a': dza, 'dq': dconv_x, 'z_b': dzb, 'ak': dak, 'av': dvh, 'aq': daq,
                       'z_c': dzc, 'mq': dmq, 'z_m': dzm, 'gates': jnp.concatenate(dgates, axis=1),
                       'dadb': ddadb}, seg, L)
    gr['wp'] = _mm(sv['xn'], dh, ta=True, name=p + "dw_in")
    dxn = _mm(dh, wt['wp'], tb=True, name=p + "d_xn")
    (dx_in,), (dg_norm,) = _rowwise_bwd(_f_norm, [(sv['x'], D, 0)], [sv['g_norm']], [[(dxn, D, 0)]], [True], [True],
                                        tm=256, name=p + "norm_bwd", accs={0: (dx, D, 0)})
    gr['norm_g'] = dg_norm[0]
    return dx_in, gr


_ARG_NAMES = (['x', 'mem'] + WEIGHTS + ['loss_target'] + ['m_' + w for w in WEIGHTS] + ['v_' + w for w in WEIGHTS])


def kernel(x, mem, norm_g, w_in, ssm_a_re, ssm_a_im, ssm_log_step, ssm_b_re, ssm_b_im, ssm_c_re, ssm_c_im,
           ssm_d, ssm_w_glu, ssm_b_glu, dn_conv, dn_a_log, dn_dt_bias, dn_norm_g, attn_q_norm, attn_k_norm,
           mem_norm_g, w_mem_kv, w_branch, w_out, final_norm_g, loss_target, m_norm_g, m_w_in, m_ssm_a_re,
           m_ssm_a_im, m_ssm_log_step, m_ssm_b_re, m_ssm_b_im, m_ssm_c_re, m_ssm_c_im, m_ssm_d, m_ssm_w_glu,
           m_ssm_b_glu, m_dn_conv, m_dn_a_log, m_dn_dt_bias, m_dn_norm_g, m_attn_q_norm, m_attn_k_norm,
           m_mem_norm_g, m_w_mem_kv, m_w_branch, m_w_out, m_final_norm_g, v_norm_g, v_w_in, v_ssm_a_re,
           v_ssm_a_im, v_ssm_log_step, v_ssm_b_re, v_ssm_b_im, v_ssm_c_re, v_ssm_c_im, v_ssm_d, v_ssm_w_glu,
           v_ssm_b_glu, v_dn_conv, v_dn_a_log, v_dn_dt_bias, v_dn_norm_g, v_attn_q_norm, v_attn_k_norm,
           v_mem_norm_g, v_w_mem_kv, v_w_branch, v_w_out, v_final_norm_g):
    given = locals()
    return _train_step({n: given[n] for n in _ARG_NAMES})


def _train_step(a):
    c = CFG
    d = _dims(c)
    seg, rseg = _layout(c)
    depth, nh2 = c['DEPTH'], 2 * c['DNH']
    x, mem, tgt = a['x'][0], a['mem'][0], a['loss_target'][0]
    L, D = x.shape

    shard_shapes = [a[n].shape for n in SHARDED]
    rw = _round_up(sum(_rows_of(s) for s in shard_shapes), 16)
    gathered = _all_gather(_pack([a[n] for n in SHARDED], BF16, total_rows=rw))
    full = {n: _to_full(n, p_) for n, p_ in zip(SHARDED, _unpack(gathered, shard_shapes, lead=1))}
    offs = np.cumsum([0, d['SW'], d['DW'], d['AW'], d['MW']])
    wts = []
    for l in range(depth):
        conv = jnp.transpose(full['dn_conv'][l].astype(F32).reshape(3, d['DW'], c['CONV']), (0, 2, 1))
        wts.append(dict(
            wp=_w_in_to_layout(full['w_in'][l], seg, rseg, nh2),
            w_branch=[full['w_branch'][l, offs[i]:offs[i + 1]] for i in range(4)],
            w_out=full['w_out'][l], w_mem_kv=full['w_mem_kv'][l], w_glu=full['ssm_w_glu'][l].astype(F32),
            conv=jnp.pad(conv, ((0, 0), (0, 8 - c['CONV']), (0, 0)))))
    rope = _rope_tables(L, c)

    saved = []
    for l in range(depth):
        x, sv = _layer_fwd(x, mem, l, wts[l], a, rope, c, d, seg)
        saved.append(sv)
    loss_part, dx, dg_final = _loss_grad(x, a['final_norm_g'][None, :], tgt, tm=256, name="final_norm_loss")
    grads = [None] * depth
    for l in reversed(range(depth)):
        dx, grads[l] = _layer_bwd(dx, mem, l, wts[l], a, rope, saved[l], c, d, seg)

    gfull = {n: jnp.stack([grads[l][n] for l in range(depth)], axis=0) for n in WEIGHTS
             if n not in ('w_in', 'final_norm_g')}
    gfull['w_in'] = jnp.stack([_w_in_from_layout(grads[l]['wp'], seg, rseg, nh2) for l in range(depth)], axis=0)
    gfull['final_norm_g'] = dg_final[0]

    small_shapes = [a[n].shape for n in SMALL] + [(1,)]
    rs = _round_up(sum(_rows_of(s) for s in small_shapes), 16)
    g_shard = _pack([_to_slabs(n, gfull[n]) for n in SHARDED], F32, lead=1, total_rows=rw)
    g_small = _pack([gfull[n] for n in SMALL] + [loss_part[0, :1]], F32, total_rows=rs)
    slabs = jnp.concatenate([g_shard, jnp.broadcast_to(g_small[None], (N_DEV,) + g_small.shape)], axis=1)
    gsum = _sum_slots(_all_to_all(slabs), tr=256, name="grad_sum")

    def local_pack(prefix):
        zero = jnp.zeros((1,), F32)
        return jnp.concatenate([_pack([a[prefix + n] for n in SHARDED], F32, total_rows=rw),
                                _pack([a[prefix + n] for n in SMALL] + [zero], F32, total_rows=rs)], axis=0)

    delta, new_m, new_v = _adamw(local_pack(''), gsum, local_pack('m_'), local_pack('v_'), tr=256, name="adamw")

    def split(buf):
        vals = dict(zip(SHARDED, _unpack(buf[:rw], shard_shapes)))
        small = _unpack(buf[rw:], small_shapes)
        vals.update(zip(SMALL, small[:-1]))
        return vals, small[-1]

    g_out, loss = split(gsum)
    outs = [loss.reshape(()), dx[None]]
    for buf in (gsum, delta, new_m, new_v):
        vals, _ = split(buf)
        outs += [vals[n] for n in WEIGHTS]
    return tuple(outs)
```

```python
import functools
import math

import numpy as np
import jax
import jax.numpy as jnp
from jax import lax
from jax.experimental import pallas as pl
from jax.experimental.pallas import tpu as pltpu

F32 = jnp.float32
BF16 = jnp.bfloat16
HI = lax.Precision.HIGHEST
EPS = 1e-6
LANE = 128
SUBLANE = 8
VMEM_LIMIT = 56 * 1024 * 1024
N_DEV = 8
PACK_W = 1024

ADAM_LR, ADAM_B1, ADAM_B2, ADAM_EPS, ADAM_WD, ADAM_STEP = 0.001, 0.9, 0.999, 1e-08, 0.01, 10

CFG = dict(D=2048, L=8192, GRID_W=64, NMEM=256, DEPTH=2,
           SG=48, SP=16, SN=64,
           DNH=6, DNK=128, CONV=5, CHUNK=64,
           AH=8, AKV=2, AD=128, ROPE_THETA=10000.0,
           MH=4, MD=128)

TILES = dict(att_q=512, att_k=512)

WEIGHTS = ['norm_g', 'w_in', 'ssm_a_re', 'ssm_a_im', 'ssm_log_step', 'ssm_b_re', 'ssm_b_im', 'ssm_c_re',
           'ssm_c_im', 'ssm_d', 'ssm_w_glu', 'ssm_b_glu', 'dn_conv', 'dn_a_log', 'dn_dt_bias', 'dn_norm_g',
           'attn_q_norm', 'attn_k_norm', 'mem_norm_g', 'w_mem_kv', 'w_branch', 'w_out', 'final_norm_g']
SHARDED = ['w_in', 'w_branch', 'w_out', 'w_mem_kv', 'ssm_w_glu', 'dn_conv']
SMALL = [w for w in WEIGHTS if w not in SHARDED]


def _dims(c):
    d = dict(c)
    d['SW'] = c['SG'] * c['SP']
    d['NB'] = d['SW'] // LANE
    d['GPB'] = LANE // c['SP']
    d['BS'] = d['GPB'] * c['SN']
    d['DW'] = c['DNH'] * c['DNK']
    d['AW'] = c['AH'] * c['AD']
    d['AKW'] = c['AKV'] * c['AD']
    d['MW'] = c['MH'] * c['MD']
    d['BT'] = d['SW'] + d['DW'] + d['AW'] + d['MW']
    return d


def _round_up(a, b):
    return (a + b - 1) // b * b


def _layout(c):
    d = _dims(c)
    D, SW, DW, AW, AKW, MW = d['D'], d['SW'], d['DW'], d['AW'], d['AKW'], d['MW']
    order = [('u_a', SW, SW), ('z_a', SW, SW), ('dq', DW, DW), ('dk', DW, DW), ('dv', DW, DW), ('z_b', DW, DW),
             ('ak', AKW, AKW), ('av', AKW, AKW), ('aq', AW, AW), ('z_c', AW, AW), ('mq', MW, MW), ('z_m', MW, MW),
             ('gates', 4 * D, D), ('dadb', LANE, LANE)]
    off, seg = 0, {}
    for name, w, al in order:
        off = _round_up(off, al)
        seg[name] = (off, w)
        off += w
    seg['_total'] = _round_up(off, 512)
    ref_order = [('u_a', SW), ('z_a', SW), ('dq', DW), ('dk', DW), ('dv', DW), ('da', 2 * d['DNH']),
                 ('db', 2 * d['DNH']), ('z_b', DW), ('aq', AW), ('ak', AKW), ('av', AKW), ('z_c', AW),
                 ('mq', MW), ('z_m', MW), ('gates', 4 * D)]
    roff, rseg = 0, {}
    for name, w in ref_order:
        rseg[name] = (roff, w)
        roff += w
    rseg['_total'] = roff
    return seg, rseg


def _cparams(sem):
    return pltpu.CompilerParams(dimension_semantics=sem, vmem_limit_bytes=VMEM_LIMIT)


def _pick(t, n):
    if n <= t:
        return n
    for align in (LANE, 2 * SUBLANE):
        for cand in range(t - t % align, 0, -align):
            if n % cand == 0:
                return cand
    return n


def _mm(a, b, *, name, ta=False, tb=False, add=None, out_dtype=F32, tm=1024, tn=1024, tk=512):
    M, K = (a.shape[1], a.shape[0]) if ta else a.shape
    N = b.shape[0] if tb else b.shape[1]
    assert (b.shape[1] if tb else b.shape[0]) == K
    tm, tn, tk = _pick(tm, M), _pick(tn, N), _pick(tk, K)
    nk = K // tk
    dn = (((0 if ta else 1,), (1 if tb else 0,)), ((), ()))
    has_add = add is not None

    def body(*refs):
        if has_add:
            a_ref, b_ref, add_ref, o_ref, acc = refs
        else:
            a_ref, b_ref, o_ref, acc = refs
        k = pl.program_id(2)

        @pl.when(k == 0)
        def _():
            acc[...] = jnp.zeros_like(acc)

        acc[...] += lax.dot_general(a_ref[...].astype(BF16), b_ref[...].astype(BF16), dn,
                                    preferred_element_type=F32)

        @pl.when(k == nk - 1)
        def _():
            r = acc[...]
            if has_add:
                r = r + add_ref[...]
            o_ref[...] = r.astype(o_ref.dtype)

    a_spec = pl.BlockSpec((tk, tm), lambda i, j, k: (k, i)) if ta else pl.BlockSpec((tm, tk), lambda i, j, k: (i, k))
    b_spec = pl.BlockSpec((tn, tk), lambda i, j, k: (j, k)) if tb else pl.BlockSpec((tk, tn), lambda i, j, k: (k, j))
    in_specs = [a_spec, b_spec]
    args = [a, b]
    if has_add:
        in_specs.append(pl.BlockSpec((tm, tn), lambda i, j, k: (i, j)))
        args.append(add)
    return pl.pallas_call(
        body, name=name, grid=(M // tm, N // tn, nk),
        in_specs=in_specs, out_specs=pl.BlockSpec((tm, tn), lambda i, j, k: (i, j)),
        out_shape=jax.ShapeDtypeStruct((M, N), out_dtype),
        scratch_shapes=[pltpu.VMEM((tm, tn), F32)],
        compiler_params=_cparams(("parallel", "parallel", "arbitrary")),
    )(*args)


def _row_spec(tm, w, cb):
    return pl.BlockSpec((tm, w), lambda i, cb=cb: (i, cb))


def _rowwise(fn, rows, params, outs, *, tm, name):
    L = rows[0][0].shape[0]
    tm = _pick(tm, L)
    nr, npar = len(rows), len(params)

    def body(*refs):
        vals = [r[...] for r in refs[:nr + npar]]
        res = fn(*vals)
        for o_ref, v in zip(refs[nr + npar:], res):
            o_ref[...] = v.astype(o_ref.dtype)

    in_specs = [_row_spec(tm, w, cb) for (_, w, cb) in rows]
    in_specs += [pl.BlockSpec(p.shape, lambda i: (0, 0)) for p in params]
    res = pl.pallas_call(
        body, name=name, grid=(L // tm,), in_specs=in_specs,
        out_specs=[pl.BlockSpec((tm, w), lambda i: (i, 0)) for (w, _) in outs],
        out_shape=[jax.ShapeDtypeStruct((L, w), dt) for (w, dt) in outs],
        compiler_params=_cparams(("parallel",)),
    )(*[r[0] for r in rows], *params)
    return list(res)


def _rowwise_bwd(fn, rows, params, cts, drows, dparams, *, tm, name, accs=None):
    L = rows[0][0].shape[0]
    tm = _pick(tm, L)
    nr, npar = len(rows), len(params)
    accs = accs or {}
    ct_flat = [c for grp in cts for c in grp]
    ct_sizes = [len(grp) for grp in cts]
    acc_keys = sorted(accs)
    d_r = [i for i in range(nr) if drows[i]]
    d_p = [i for i in range(npar) if dparams[i]]
    n_in = nr + npar + len(ct_flat) + len(acc_keys)

    def body(*refs):
        vals = [r[...] for r in refs[:nr + npar]]
        ct_refs = refs[nr + npar:nr + npar + len(ct_flat)]
        acc_refs = refs[nr + npar + len(ct_flat):n_in]
        o_refs = refs[n_in:]
        ct_vals, pos = [], 0
        for n in ct_sizes:
            v = ct_refs[pos][...].astype(F32)
            for r in ct_refs[pos + 1:pos + n]:
                v = v + r[...].astype(F32)
            ct_vals.append(v)
            pos += n
        diff_idx = d_r + [nr + i for i in d_p]

        def g(*dv):
            full = list(vals)
            for i, v in zip(diff_idx, dv):
                full[i] = v
            return tuple(o.astype(F32) for o in fn(*full))

        _, vjp = jax.vjp(g, *[vals[i] for i in diff_idx])
        grads = vjp(tuple(ct_vals))
        for n, i in enumerate(d_r):
            gv = grads[n].astype(F32)
            if i in accs:
                gv = gv + acc_refs[acc_keys.index(i)][...]
            o_refs[n][...] = gv
        step = pl.program_id(0)
        for n, i in enumerate(d_p):
            o_ref = o_refs[len(d_r) + n]

            @pl.when(step == 0)
            def _(o_ref=o_ref):
                o_ref[...] = jnp.zeros_like(o_ref)

            o_ref[...] += grads[len(d_r) + n].astype(F32)

    in_specs = [_row_spec(tm, w, cb) for (_, w, cb) in rows]
    in_specs += [pl.BlockSpec(p.shape, lambda i: (0, 0)) for p in params]
    in_specs += [_row_spec(tm, w, cb) for (_, w, cb) in ct_flat]
    in_specs += [_row_spec(tm, accs[k][1], accs[k][2]) for k in acc_keys]
    out_specs = [pl.BlockSpec((tm, rows[i][1]), lambda i_: (i_, 0)) for i in d_r]
    out_specs += [pl.BlockSpec(params[i].shape, lambda i_: (0, 0)) for i in d_p]
    out_shape = [jax.ShapeDtypeStruct((L, rows[i][1]), F32) for i in d_r]
    out_shape += [jax.ShapeDtypeStruct(params[i].shape, F32) for i in d_p]
    res = pl.pallas_call(
        body, name=name, grid=(L // tm,), in_specs=in_specs, out_specs=out_specs, out_shape=out_shape,
        compiler_params=_cparams(("arbitrary",)),
    )(*[r[0] for r in rows], *params, *[c[0] for c in ct_flat], *[accs[k][0] for k in acc_keys])
    res = list(res)
    return res[:len(d_r)], res[len(d_r):]


def _silu(x):
    return x * jax.nn.sigmoid(x)


def _rms(x, g):
    return x * lax.rsqrt(jnp.mean(x * x, axis=-1, keepdims=True) + EPS) * g


def _softplus(x):
    return jnp.maximum(x, 0.0) + jnp.log1p(jnp.exp(-jnp.abs(x)))


def _heads(x, hd):
    return [x[:, i * hd:(i + 1) * hd] for i in range(x.shape[1] // hd)]


def _f_norm(x, g):
    return (_rms(x, g),)


def _f_s5tail(ys, u, z, d, wglu, bglu):
    y = jax.nn.gelu(ys + d * u)
    gate = jax.nn.sigmoid(jnp.dot(y.astype(BF16), wglu.astype(BF16), preferred_element_type=F32) + bglu)
    return (y * gate * _silu(z),)


def _make_f_dnpre(nh, hd, chunk):
    def f(qc, kc, dadb, alog, dtb):
        tm = qc.shape[0]
        qn = [q * lax.rsqrt(jnp.sum(q * q, axis=-1, keepdims=True) + EPS) * (hd ** -0.5) for q in _heads(qc, hd)]
        kn = [k * lax.rsqrt(jnp.sum(k * k, axis=-1, keepdims=True) + EPS) for k in _heads(kc, hd)]
        g = -jnp.exp(alog) * _softplus(dadb + dtb)
        beta = jax.nn.sigmoid(dadb)
        ii = lax.broadcasted_iota(jnp.int32, (tm, tm), 0)
        jj = lax.broadcasted_iota(jnp.int32, (tm, tm), 1)
        same = (ii // chunk) == (jj // chunk)
        outs = [jnp.concatenate(qn, axis=1), jnp.concatenate(kn, axis=1)]
        gt = jnp.dot(same.astype(F32), g, precision=HI, preferred_element_type=F32)
        for dr in range(2):
            tri = jnp.logical_and(same, (ii <= jj) if dr else (ii >= jj)).astype(F32)
            gc = jnp.dot(tri, g, precision=HI, preferred_element_type=F32)

            def spread(t, lane0):
                return jnp.concatenate([jnp.broadcast_to(t[:, lane0 + h:lane0 + h + 1], (tm, hd))
                                        for h in range(nh)], axis=1)

            outs += [spread(beta, 2 * nh + dr * nh), spread(gc, dr * nh), spread(gt, dr * nh)]
        return tuple(outs)
    return f


def _make_f_dnpost(hd):
    def f(o, z, ng):
        y = [_rms(oh, ng) for oh in _heads(o, hd)]
        return (jnp.concatenate(y, axis=1) * _silu(z),)
    return f


def _make_f_attpre(hd, with_v):
    def rope(x, g, cosf, sins, perm, scale):
        xn = _rms(x, g)
        xs = jnp.dot(xn, perm, precision=HI, preferred_element_type=F32)
        return (xn * cosf + xs * sins) * scale

    def f(aq, ak, *rest):
        if with_v:
            av, cosf, sins, perm, qg, kg = rest
        else:
            cosf, sins, perm, qg, kg = rest
        qh = jnp.concatenate([rope(x, qg, cosf, sins, perm, hd ** -0.5) for x in _heads(aq, hd)], axis=1)
        kh = jnp.concatenate([rope(x, kg, cosf, sins, perm, 1.0) for x in _heads(ak, hd)], axis=1)
        return (qh, kh, av) if with_v else (qh, kh)
    return f


def _f_gate(o, z):
    return (o * _silu(z),)


def _make_f_mem(nh, hd):
    def f(mq, z, kv):
        mw = nh * hd
        outs = []
        for h, q in enumerate(_heads(mq, hd)):
            k = kv[:, h * hd:(h + 1) * hd]
            v = kv[:, mw + h * hd:mw + (h + 1) * hd]
            s = lax.dot_general(q.astype(BF16), k.astype(BF16), (((1,), (1,)), ((), ())),
                                preferred_element_type=F32) * (hd ** -0.5)
            s = s - jnp.max(s, axis=-1, keepdims=True)
            p = jnp.exp(s)
            p = p / jnp.sum(p, axis=-1, keepdims=True)
            outs.append(jnp.dot(p.astype(BF16), v.astype(BF16), preferred_element_type=F32))
        return (jnp.concatenate(outs, axis=1) * _silu(z),)
    return f


def _f_merge(p0, p1, p2, p3, g0, g1, g2, g3):
    return (jax.nn.sigmoid(g0) * p0 + jax.nn.sigmoid(g1) * p1 + jax.nn.sigmoid(g2) * p2 + jax.nn.sigmoid(g3) * p3,)


def _make_f_delta(hd):
    def f(do, o):
        out = [jnp.broadcast_to(jnp.sum(a * b, axis=-1, keepdims=True), a.shape)
               for a, b in zip(_heads(do, hd), _heads(o, hd))]
        return (jnp.concatenate(out, axis=1),)
    return f


def _s5_prep(a_re, a_im, log_step, b_re, b_im, c_re, c_im, d):
    nb, gpb, sn, sp = d['NB'], d['GPB'], d['SN'], d['SP']
    step = jnp.exp(log_step)[:, None]
    mag = jnp.exp(a_re * step)
    lam_re = mag * jnp.cos(a_im * step)
    lam_im = mag * jnp.sin(a_im * step)
    den = a_re * a_re + a_im * a_im
    nr, ni = lam_re - 1.0, lam_im
    coef_re = (nr * a_re + ni * a_im) / den
    coef_im = (ni * a_re - nr * a_im) / den
    bb_re = coef_re[..., None] * b_re - coef_im[..., None] * b_im
    bb_im = coef_re[..., None] * b_im + coef_im[..., None] * b_re
    eye = jnp.eye(gpb, dtype=F32)

    def blk_in(bb):
        t = bb.reshape(nb, gpb, sn, sp)
        return jnp.einsum("jgnp,gh->jgphn", t, eye).reshape(nb, gpb * sp, gpb * sn)

    def blk_out(cc):
        t = cc.reshape(nb, gpb, sp, sn)
        return jnp.einsum("jgpn,gh->jgnhp", t, eye).reshape(nb, gpb * sn, gpb * sp)

    wb = jnp.concatenate([blk_in(bb_re), blk_in(bb_im)], axis=2)
    wc = jnp.concatenate([blk_out(c_re), blk_out(-c_im)], axis=1)
    return wb, wc, lam_re.reshape(nb, gpb * sn), lam_im.reshape(nb, gpb * sn)


def _s5_tables(lam_re, lam_im, rev, conj):
    lr, li = lam_re, (-lam_im if conj else lam_im)

    def cmul(a, b):
        return a[0] * b[0] - a[1] * b[1], a[0] * b[1] + a[1] * b[0]

    pw = [(lr, li)]
    for _ in range(7):
        pw.append(cmul(pw[-1], (lr, li)))
    bc = lambda t: jnp.broadcast_to(t[:, None, :], (t.shape[0], 8, t.shape[1]))
    order = list(range(8))[::-1] if rev else list(range(8))
    pwr = jnp.stack([pw[i][0] for i in order], axis=1)
    pwi = jnp.stack([pw[i][1] for i in order], axis=1)
    tabs = [bc(pw[0][0]), bc(pw[0][1]), bc(pw[1][0]), bc(pw[1][1]), bc(pw[3][0]), bc(pw[3][1]), pwr, pwi]
    return jnp.stack(tabs, axis=1)


def _scan_group(xr, xi, lt_ref, j, cr, ci, rev):
    row = lax.broadcasted_iota(jnp.int32, xr.shape, 0)
    for lvl, k in enumerate((1, 2, 4)):
        l_r, l_i = lt_ref[j, 2 * lvl], lt_ref[j, 2 * lvl + 1]
        sh = (8 - k) if rev else k
        keep = (row < 8 - k) if rev else (row >= k)
        sr = jnp.where(keep, pltpu.roll(xr, sh, 0), 0.0)
        si = jnp.where(keep, pltpu.roll(xi, sh, 0), 0.0)
        xr, xi = xr + l_r * sr - l_i * si, xi + l_r * si + l_i * sr
    p_r, p_i = lt_ref[j, 6], lt_ref[j, 7]
    return xr + p_r * cr - p_i * ci, xi + p_r * ci + p_i * cr


def _last_row(x, rev):
    row = lax.broadcasted_iota(jnp.int32, x.shape, 0)
    v = jnp.sum(jnp.where(row == (0 if rev else 7), x, 0.0), axis=0, keepdims=True)
    return jnp.broadcast_to(v, x.shape)


def _s5_fwd(hsrc, ucb, wb, wc, lt, *, rev, acc, tb, name, d):
    L, SW, NB, BS = hsrc.shape[0], d['SW'], d['NB'], d['BS']
    tb = _pick(tb, L)
    nblk, ngr = L // tb, tb // 8
    tix = (lambda b: nblk - 1 - b) if rev else (lambda b: b)
    has_acc = acc is not None

    def body(*refs):
        if has_acc:
            u_ref, wb_ref, wc_ref, lt_ref, acc_ref, y_ref, cin_ref, bu_s, car = refs
        else:
            u_ref, wb_ref, wc_ref, lt_ref, y_ref, cin_ref, bu_s, car = refs

        @pl.when(pl.program_id(0) == 0)
        def _():
            car[...] = jnp.zeros_like(car)

        cin_ref[...] = car[...]
        for j in range(NB):
            bu_s[:, j * 2 * BS:(j + 1) * 2 * BS] = jnp.dot(
                u_ref[:, j * LANE:(j + 1) * LANE].astype(BF16), wb_ref[j], preferred_element_type=F32)

        def grp(r, _):
            base = pl.multiple_of((ngr - 1 - r if rev else r) * 8, 8)
            for j in range(NB):
                c0 = j * 2 * BS
                xr, xi = _scan_group(bu_s[pl.ds(base, 8), c0:c0 + BS], bu_s[pl.ds(base, 8), c0 + BS:c0 + 2 * BS],
                                     lt_ref, j, car[:, c0:c0 + BS], car[:, c0 + BS:c0 + 2 * BS], rev)
                bu_s[pl.ds(base, 8), c0:c0 + BS] = xr
                bu_s[pl.ds(base, 8), c0 + BS:c0 + 2 * BS] = xi
                car[:, c0:c0 + BS] = _last_row(xr, rev)
                car[:, c0 + BS:c0 + 2 * BS] = _last_row(xi, rev)
            return 0

        lax.fori_loop(0, ngr, grp, 0)
        for j in range(NB):
            y = jnp.dot(bu_s[:, j * 2 * BS:(j + 1) * 2 * BS].astype(BF16), wc_ref[j], preferred_element_type=F32)
            if has_acc:
                y = y + acc_ref[:, j * LANE:(j + 1) * LANE]
            y_ref[:, j * LANE:(j + 1) * LANE] = y

    in_specs = [pl.BlockSpec((tb, SW), lambda b: (tix(b), ucb)),
                pl.BlockSpec(wb.shape, lambda b: (0, 0, 0)), pl.BlockSpec(wc.shape, lambda b: (0, 0, 0)),
                pl.BlockSpec(lt.shape, lambda b: (0, 0, 0, 0))]
    args = [hsrc, wb, wc, lt]
    if has_acc:
        in_specs.append(pl.BlockSpec((tb, SW), lambda b: (tix(b), 0)))
        args.append(acc)
    y, cin = pl.pallas_call(
        body, name=name, grid=(nblk,), in_specs=in_specs,
        out_specs=[pl.BlockSpec((tb, SW), lambda b: (tix(b), 0)),
                   pl.BlockSpec((8, NB * 2 * BS), lambda b: (tix(b), 0))],
        out_shape=[jax.ShapeDtypeStruct((L, SW), F32), jax.ShapeDtypeStruct((nblk * 8, NB * 2 * BS), F32)],
        scratch_shapes=[pltpu.VMEM((tb, NB * 2 * BS), F32), pltpu.VMEM((8, NB * 2 * BS), F32)],
        compiler_params=_cparams(("arbitrary",)),
    )(*args)
    return y, cin


def _s5_bwd(hsrc, ucb, dy, cin, wb, wc, lt, lt_adj, *, rev, acc, tb, name, d):
    L, SW, NB, BS = hsrc.shape[0], d['SW'], d['NB'], d['BS']
    tb = _pick(tb, L)
    nblk, ngr = L // tb, tb // 8
    arev = not rev
    tix = (lambda b: nblk - 1 - b) if arev else (lambda b: b)
    has_acc = acc is not None
    NT = (((1,), (1,)), ((), ()))
    TN = (((0,), (0,)), ((), ()))

    def body(*refs):
        if has_acc:
            (u_ref, dy_ref, cin_ref, wb_ref, wc_ref, lt_ref, la_ref, acc_ref,
             du_ref, dwb_ref, dwc_ref, dlam_ref, s_s, g_s, car, acar) = refs
        else:
            (u_ref, dy_ref, cin_ref, wb_ref, wc_ref, lt_ref, la_ref,
             du_ref, dwb_ref, dwc_ref, dlam_ref, s_s, g_s, car, acar) = refs

        @pl.when(pl.program_id(0) == 0)
        def _():
            acar[...] = jnp.zeros_like(acar)
            dwb_ref[...] = jnp.zeros_like(dwb_ref)
            dwc_ref[...] = jnp.zeros_like(dwc_ref)
            dlam_ref[...] = jnp.zeros_like(dlam_ref)

        car[...] = cin_ref[...]
        for j in range(NB):
            s_s[:, j * 2 * BS:(j + 1) * 2 * BS] = jnp.dot(
                u_ref[:, j * LANE:(j + 1) * LANE].astype(BF16), wb_ref[j], preferred_element_type=F32)
            g_s[:, j * 2 * BS:(j + 1) * 2 * BS] = lax.dot_general(
                dy_ref[:, j * LANE:(j + 1) * LANE].astype(BF16), wc_ref[j], NT, preferred_element_type=F32)

        def fgrp(r, _):
            base = pl.multiple_of((ngr - 1 - r if rev else r) * 8, 8)
            for j in range(NB):
                c0 = j * 2 * BS
                xr, xi = _scan_group(s_s[pl.ds(base, 8), c0:c0 + BS], s_s[pl.ds(base, 8), c0 + BS:c0 + 2 * BS],
                                     lt_ref, j, car[:, c0:c0 + BS], car[:, c0 + BS:c0 + 2 * BS], rev)
                s_s[pl.ds(base, 8), c0:c0 + BS] = xr
                s_s[pl.ds(base, 8), c0 + BS:c0 + 2 * BS] = xi
                car[:, c0:c0 + BS] = _last_row(xr, rev)
                car[:, c0 + BS:c0 + 2 * BS] = _last_row(xi, rev)
            return 0

        lax.fori_loop(0, ngr, fgrp, 0)

        row = lax.broadcasted_iota(jnp.int32, (8, BS), 0)

        def agrp(r, _):
            gi = ngr - 1 - r if arev else r
            base = pl.multiple_of(gi * 8, 8)
            pgi = gi + 1 if rev else gi - 1
            inside = jnp.logical_and(pgi >= 0, pgi < ngr)
            pbase = pl.multiple_of(jnp.clip(pgi, 0, ngr - 1) * 8, 8)
            for j in range(NB):
                c0 = j * 2 * BS
                ar, ai = _scan_group(g_s[pl.ds(base, 8), c0:c0 + BS], g_s[pl.ds(base, 8), c0 + BS:c0 + 2 * BS],
                                     la_ref, j, acar[:, c0:c0 + BS], acar[:, c0 + BS:c0 + 2 * BS], arev)
                g_s[pl.ds(base, 8), c0:c0 + BS] = ar
                g_s[pl.ds(base, 8), c0 + BS:c0 + 2 * BS] = ai
                acar[:, c0:c0 + BS] = _last_row(ar, arev)
                acar[:, c0 + BS:c0 + 2 * BS] = _last_row(ai, arev)
                sr, si = s_s[pl.ds(base, 8), c0:c0 + BS], s_s[pl.ds(base, 8), c0 + BS:c0 + 2 * BS]
                edge_r = jnp.where(inside, _last_row(s_s[pl.ds(pbase, 8), c0:c0 + BS], rev), cin_ref[:, c0:c0 + BS])
                edge_i = jnp.where(inside, _last_row(s_s[pl.ds(pbase, 8), c0 + BS:c0 + 2 * BS], rev),
                                   cin_ref[:, c0 + BS:c0 + 2 * BS])
                sh = 7 if rev else 1
                first = 7 if rev else 0
                pr = jnp.where(row == first, edge_r, pltpu.roll(sr, sh, 0))
                pi = jnp.where(row == first, edge_i, pltpu.roll(si, sh, 0))
                dlam_ref[:, c0:c0 + BS] += ar * pr + ai * pi
                dlam_ref[:, c0 + BS:c0 + 2 * BS] += ai * pr - ar * pi
            return 0

        lax.fori_loop(0, ngr, agrp, 0)
        for j in range(NB):
            a_j = g_s[:, j * 2 * BS:(j + 1) * 2 * BS].astype(BF16)
            u_j = u_ref[:, j * LANE:(j + 1) * LANE].astype(BF16)
            du = lax.dot_general(a_j, wb_ref[j], NT, preferred_element_type=F32)
            if has_acc:
                du = du + acc_ref[:, j * LANE:(j + 1) * LANE]
            du_ref[:, j * LANE:(j + 1) * LANE] = du
            dwb_ref[j] += lax.dot_general(u_j, a_j, TN, preferred_element_type=F32)
            dwc_ref[j] += lax.dot_general(s_s[:, j * 2 * BS:(j + 1) * 2 * BS].astype(BF16),
                                          dy_ref[:, j * LANE:(j + 1) * LANE].astype(BF16), TN,
                                          preferred_element_type=F32)

    W2 = NB * 2 * BS
    in_specs = [pl.BlockSpec((tb, SW), lambda b: (tix(b), ucb)), pl.BlockSpec((tb, SW), lambda b: (tix(b), 0)),
                pl.BlockSpec((8, W2), lambda b: (tix(b), 0)),
                pl.BlockSpec(wb.shape, lambda b: (0, 0, 0)), pl.BlockSpec(wc.shape, lambda b: (0, 0, 0)),
                pl.BlockSpec(lt.shape, lambda b: (0, 0, 0, 0)), pl.BlockSpec(lt_adj.shape, lambda b: (0, 0, 0, 0))]
    args = [hsrc, dy, cin, wb, wc, lt, lt_adj]
    if has_acc:
        in_specs.append(pl.BlockSpec((tb, SW), lambda b: (tix(b), 0)))
        args.append(acc)
    return pl.pallas_call(
        body, name=name, grid=(nblk,), in_specs=in_specs,
        out_specs=[pl.BlockSpec((tb, SW), lambda b: (tix(b), 0)),
                   pl.BlockSpec(wb.shape, lambda b: (0, 0, 0)), pl.BlockSpec(wc.shape, lambda b: (0, 0, 0)),
                   pl.BlockSpec((8, W2), lambda b: (0, 0))],
        out_shape=[jax.ShapeDtypeStruct((L, SW), F32), jax.ShapeDtypeStruct(wb.shape, F32),
                   jax.ShapeDtypeStruct(wc.shape, F32), jax.ShapeDtypeStruct((8, W2), F32)],
        scratch_shapes=[pltpu.VMEM((tb, W2), F32), pltpu.VMEM((tb, W2), F32),
                        pltpu.VMEM((8, W2), F32), pltpu.VMEM((8, W2), F32)],
        compiler_params=_cparams(("arbitrary",)),
    )(*args)


_NN = (((1,), (0,)), ((), ()))
_NT = (((1,), (1,)), ((), ()))
_TN = (((0,), (0,)), ((), ()))


def _dotb(a, b, dn=_NN):
    return lax.dot_general(a.astype(BF16), b.astype(BF16), dn, preferred_element_type=F32)


def _split(x):
    hi = x.astype(BF16)
    return hi, (x - hi.astype(F32)).astype(BF16)


def _dot3(a, b, dn=_NN):
    ah, al = _split(a)
    bh, bl = _split(b)
    f = lambda x, y: lax.dot_general(x, y, dn, preferred_element_type=F32)
    return f(ah, bh) + (f(ah, bl) + f(al, bh))


def _delta_chunk(q, k, v, beta, gc, gt, s_in, rev):
    c, hd = q.shape
    ii = lax.broadcasted_iota(jnp.int32, (c, c), 0)
    jj = lax.broadcasted_iota(jnp.int32, (c, c), 1)
    incl = (ii <= jj) if rev else (ii >= jj)
    strict = (ii < jj) if rev else (ii > jj)
    gc_i = gc[:, :c]
    gc_j = jnp.transpose(gc)[:c, :]
    decay = jnp.where(incl, jnp.exp(jnp.where(incl, gc_i - gc_j, 0.0)), 0.0)
    kb = k * beta
    a = jnp.where(strict, _dot3(kb, k, _NT) * decay, 0.0)
    eye = (ii == jj).astype(F32)
    tinv = eye - a
    p = a
    n = 2
    while n < c:
        p = _dot3(p, p)
        tinv = _dot3(tinv, eye + p)
        n *= 2
    eg = jnp.exp(gc)
    u = _dot3(tinv, v * beta)
    w = _dot3(tinv, kb * eg)
    intra = _dotb(q, k, _NT) * decay
    qd = q * eg
    kd = k * jnp.exp(gt - gc)
    v_new = u - _dotb(w, s_in)
    o = _dotb(qd, s_in) + _dotb(intra, v_new)
    s_out = s_in * jnp.exp(jnp.broadcast_to(gt[0:1, :], (hd, hd))) + _dotb(kd, v_new, _TN)
    return o, s_out


def _delta_fwd(q, k, v, gates, *, vcb, rev, acc, name, d):
    L, H, hd, C = q.shape[0], d['DNH'], d['DNK'], d['CHUNK']
    nc = L // C
    cix = (lambda i: nc - 1 - i) if rev else (lambda i: i)
    has_acc = acc is not None

    def body(*refs):
        if has_acc:
            q_ref, k_ref, v_ref, b_ref, gc_ref, gt_ref, acc_ref, o_ref, ss_ref, st = refs
        else:
            q_ref, k_ref, v_ref, b_ref, gc_ref, gt_ref, o_ref, ss_ref, st = refs

        @pl.when(pl.program_id(0) == 0)
        def _():
            st[...] = jnp.zeros_like(st)

        for h in range(H):
            sl = slice(h * hd, (h + 1) * hd)
            s_in = st[h]
            ss_ref[0, h] = s_in
            o, s_out = _delta_chunk(q_ref[:, sl], k_ref[:, sl], v_ref[:, sl], b_ref[:, sl], gc_ref[:, sl],
                                    gt_ref[:, sl], s_in, rev)
            if has_acc:
                o = o + acc_ref[:, sl]
            o_ref[:, sl] = o
            st[h] = s_out

    blk = pl.BlockSpec((C, H * hd), lambda i: (cix(i), 0))
    in_specs = [blk, blk, pl.BlockSpec((C, H * hd), lambda i: (cix(i), vcb)), blk, blk, blk]
    args = [q, k, v, *gates]
    if has_acc:
        in_specs.append(blk)
        args.append(acc)
    return pl.pallas_call(
        body, name=name, grid=(nc,), in_specs=in_specs,
        out_specs=[blk, pl.BlockSpec((1, H, hd, hd), lambda i: (cix(i), 0, 0, 0))],
        out_shape=[jax.ShapeDtypeStruct((L, H * hd), F32), jax.ShapeDtypeStruct((nc, H, hd, hd), F32)],
        scratch_shapes=[pltpu.VMEM((H, hd, hd), F32)],
        compiler_params=_cparams(("arbitrary",)),
    )(*args)


def _delta_bwd(q, k, v, gates, ssave, do, *, vcb, rev, accs, name, d):
    L, H, hd, C = q.shape[0], d['DNH'], d['DNK'], d['CHUNK']
    nc = L // C
    cix = (lambda i: i) if rev else (lambda i: nc - 1 - i)
    has_acc = accs is not None

    def body(*refs):
        if has_acc:
            (q_ref, k_ref, v_ref, b_ref, gc_ref, gt_ref, ss_ref, do_ref, aq_ref, ak_ref, av_ref,
             dq_ref, dk_ref, dv_ref, db_ref, dgc_ref, dgt_ref, dst) = refs
        else:
            (q_ref, k_ref, v_ref, b_ref, gc_ref, gt_ref, ss_ref, do_ref,
             dq_ref, dk_ref, dv_ref, db_ref, dgc_ref, dgt_ref, dst) = refs

        @pl.when(pl.program_id(0) == 0)
        def _():
            dst[...] = jnp.zeros_like(dst)

        for h in range(H):
            sl = slice(h * hd, (h + 1) * hd)
            _, vjp = jax.vjp(functools.partial(_delta_chunk, rev=rev), q_ref[:, sl], k_ref[:, sl], v_ref[:, sl],
                             b_ref[:, sl], gc_ref[:, sl], gt_ref[:, sl], ss_ref[0, h])
            dq, dk, dv, db, dgc, dgt, ds = vjp((do_ref[:, sl], dst[h]))
            dst[h] = ds
            if has_acc:
                dq, dk, dv = dq + aq_ref[:, sl], dk + ak_ref[:, sl], dv + av_ref[:, sl]
            dq_ref[:, sl] = dq
            dk_ref[:, sl] = dk
            dv_ref[:, sl] = dv
            db_ref[:, sl] = db
            dgc_ref[:, sl] = dgc
            dgt_ref[:, sl] = dgt

    blk = pl.BlockSpec((C, H * hd), lambda i: (cix(i), 0))
    in_specs = [blk, blk, pl.BlockSpec((C, H * hd), lambda i: (cix(i), vcb)), blk, blk, blk,
                pl.BlockSpec((1, H, hd, hd), lambda i: (cix(i), 0, 0, 0)), blk]
    args = [q, k, v, *gates, ssave, do]
    if has_acc:
        in_specs += [blk, blk, blk]
        args += list(accs)
    return pl.pallas_call(
        body, name=name, grid=(nc,), in_specs=in_specs, out_specs=[blk] * 6,
        out_shape=[jax.ShapeDtypeStruct((L, H * hd), F32)] * 6,
        scratch_shapes=[pltpu.VMEM((H, hd, hd), F32)],
        compiler_params=_cparams(("arbitrary",)),
    )(*args)


def _conv_specs(tm, w, cb0, nrb, L):
    hb = tm // 8
    last8 = L // 8 - 1
    cur = pl.BlockSpec((tm, w), lambda s, i: (i, cb0 + s))
    prev = pl.BlockSpec((8, w), lambda s, i: (jnp.maximum(i * hb - 1, 0), cb0 + s))
    nxt = pl.BlockSpec((8, w), lambda s, i: (jnp.minimum((i + 1) * hb, last8), cb0 + s))
    return [prev, cur, nxt]


def _fill_halo(dst, prev_ref, cur_ref, next_ref, i, nrb, tm):
    dst[pl.ds(0, 8), :] = jnp.where(i > 0, prev_ref[...], 0.0)
    dst[pl.ds(8, tm), :] = cur_ref[...]
    dst[pl.ds(8 + tm, 8), :] = jnp.where(i < nrb - 1, next_ref[...], 0.0)


def _conv_fwd(hsrc, cb0, wt, *, tm, name, d):
    L, w, K = hsrc.shape[0], d['DW'], d['CONV']
    tm = _pick(tm, L)
    nrb = L // tm

    def body(prev_ref, cur_ref, next_ref, w_ref, o_ref, xs):
        i = pl.program_id(1)
        _fill_halo(xs, prev_ref, cur_ref, next_ref, i, nrb, tm)
        y = jnp.zeros((tm, w), F32)
        for kk in range(K):
            y = y + w_ref[0, pl.ds(kk, 1), :] * xs[pl.ds(8 - K // 2 + kk, tm), :]
        o_ref[...] = _silu(y)

    return pl.pallas_call(
        body, name=name, grid=(3, nrb),
        in_specs=_conv_specs(tm, w, cb0, nrb, L) + [pl.BlockSpec((1, 8, w), lambda s, i: (s, 0, 0))],
        out_specs=pl.BlockSpec((tm, w), lambda s, i: (i, s)),
        out_shape=jax.ShapeDtypeStruct((L, 3 * w), F32),
        scratch_shapes=[pltpu.VMEM((tm + 16, w), F32)],
        compiler_params=_cparams(("parallel", "parallel")),
    )(hsrc, hsrc, hsrc, wt)


def _conv_bwd(hsrc, cb0, wt, dact, *, tm, name, d):
    L, w, K = hsrc.shape[0], d['DW'], d['CONV']
    tm = _pick(tm, L)
    nrb = L // tm
    half = K // 2

    def body(xp_ref, xc_ref, xn_ref, gp_ref, gc_ref, gn_ref, w_ref, dx_ref, dw_ref, xs, gs, dys):
        i = pl.program_id(1)
        _fill_halo(xs, xp_ref, xc_ref, xn_ref, i, nrb, tm)
        _fill_halo(gs, gp_ref, gc_ref, gn_ref, i, nrb, tm)
        y = jnp.zeros((tm + 8, w), F32)
        for kk in range(K):
            y = y + w_ref[0, pl.ds(kk, 1), :] * xs[pl.ds(4 - half + kk, tm + 8), :]
        sg = jax.nn.sigmoid(y)
        dys[...] = gs[pl.ds(4, tm + 8), :] * (sg * (1.0 + y * (1.0 - sg)))
        dx = jnp.zeros((tm, w), F32)
        for kk in range(K):
            dx = dx + w_ref[0, pl.ds(kk, 1), :] * dys[pl.ds(4 + half - kk, tm), :]
        dx_ref[...] = dx

        @pl.when(i == 0)
        def _():
            dw_ref[...] = jnp.zeros_like(dw_ref)

        dy = dys[pl.ds(4, tm), :]
        for kk in range(K):
            dw_ref[0, pl.ds(kk, 1), :] += jnp.sum(dy * xs[pl.ds(8 - half + kk, tm), :], axis=0, keepdims=True)

    gspecs = _conv_specs(tm, w, 0, nrb, L)
    return pl.pallas_call(
        body, name=name, grid=(3, nrb),
        in_specs=_conv_specs(tm, w, cb0, nrb, L) + gspecs + [pl.BlockSpec((1, 8, w), lambda s, i: (s, 0, 0))],
        out_specs=[pl.BlockSpec((tm, w), lambda s, i: (i, s)), pl.BlockSpec((1, 8, w), lambda s, i: (s, 0, 0))],
        out_shape=[jax.ShapeDtypeStruct((L, 3 * w), F32), jax.ShapeDtypeStruct((3, 8, w), F32)],
        scratch_shapes=[pltpu.VMEM((tm + 16, w), F32), pltpu.VMEM((tm + 16, w), F32), pltpu.VMEM((tm + 8, w), F32)],
        compiler_params=_cparams(("parallel", "arbitrary")),
    )(hsrc, hsrc, hsrc, dact, dact, dact, wt)


def _wide(v, n):
    return v if n == LANE else jnp.tile(v, (1, n // LANE))


def _attn_fwd(qh, kh, vh, *, tq, tk, name, d):
    L, H, KVH, hd = qh.shape[0], d['AH'], d['AKV'], d['AD']
    grp = H // KVH
    tq, tk = _pick(tq, L), _pick(tk, L)
    nk = L // tk

    def body(q_ref, k_ref, v_ref, o_ref, lse_ref, m_s, l_s, acc):
        j = pl.program_id(2)

        @pl.when(j == 0)
        def _():
            m_s[...] = jnp.full_like(m_s, -1e30)
            l_s[...] = jnp.zeros_like(l_s)
            acc[...] = jnp.zeros_like(acc)

        s = lax.dot_general(q_ref[...], k_ref[...], _NT, preferred_element_type=F32)
        m_old = m_s[...]
        m_new = jnp.maximum(m_old, jnp.max(s, axis=-1, keepdims=True))
        alpha = jnp.exp(m_old - m_new)
        p = jnp.exp(s - _wide(m_new, tk))
        l_s[...] = alpha * l_s[...] + jnp.sum(p, axis=-1, keepdims=True)
        acc[...] = alpha * acc[...] + jnp.dot(p.astype(BF16), v_ref[...], preferred_element_type=F32)
        m_s[...] = m_new

        @pl.when(j == nk - 1)
        def _():
            o_ref[...] = acc[...] / l_s[...]
            lse_ref[...] = m_s[...] + jnp.log(l_s[...])

    qspec = pl.BlockSpec((tq, hd), lambda h, i, j: (i, h))
    kspec = pl.BlockSpec((tk, hd), lambda h, i, j: (j, h // grp))
    return pl.pallas_call(
        body, name=name, grid=(H, L // tq, nk), in_specs=[qspec, kspec, kspec], out_specs=[qspec, qspec],
        out_shape=[jax.ShapeDtypeStruct((L, H * hd), F32), jax.ShapeDtypeStruct((L, H * hd), F32)],
        scratch_shapes=[pltpu.VMEM((tq, hd), F32), pltpu.VMEM((tq, hd), F32), pltpu.VMEM((tq, hd), F32)],
        compiler_params=_cparams(("parallel", "parallel", "arbitrary")),
    )(qh, kh, vh)


def _attn_bwd(qh, kh, vh, do, lse, delta, *, tq, tk, name, d):
    L, H, KVH, hd = qh.shape[0], d['AH'], d['AKV'], d['AD']
    grp = H // KVH
    tq, tk = _pick(tq, L), _pick(tk, L)
    nk = L // tk

    def body(q_ref, k_ref, v_ref, do_ref, lse_ref, dl_ref, dq_ref, dk_ref, dv_ref, dq_s):
        g, i, j = pl.program_id(1), pl.program_id(2), pl.program_id(3)

        @pl.when(jnp.logical_and(jnp.logical_and(g == 0, i == 0), j == 0))
        def _():
            dk_ref[...] = jnp.zeros_like(dk_ref)
            dv_ref[...] = jnp.zeros_like(dv_ref)

        @pl.when(j == 0)
        def _():
            dq_s[...] = jnp.zeros_like(dq_s)

        q, k, do_ = q_ref[...], k_ref[...], do_ref[...].astype(BF16)
        s = lax.dot_general(q, k, _NT, preferred_element_type=F32)
        p = jnp.exp(s - _wide(lse_ref[...], tk))
        dp = lax.dot_general(do_, v_ref[...], _NT, preferred_element_type=F32)
        ds = (p * (dp - _wide(dl_ref[...], tk))).astype(BF16)
        dq_s[...] += jnp.dot(ds, k, preferred_element_type=F32)
        rows = pl.ds(pl.multiple_of(j * tk, tk), tk)
        dv_ref[rows, :] += lax.dot_general(p.astype(BF16), do_, _TN, preferred_element_type=F32)
        dk_ref[rows, :] += lax.dot_general(ds, q, _TN, preferred_element_type=F32)

        @pl.when(j == nk - 1)
        def _():
            dq_ref[...] = dq_s[...]

    qspec = pl.BlockSpec((tq, hd), lambda kv, g, i, j: (i, kv * grp + g))
    kspec = pl.BlockSpec((tk, hd), lambda kv, g, i, j: (j, kv))
    colspec = pl.BlockSpec((L, hd), lambda kv, g, i, j: (0, kv))
    return pl.pallas_call(
        body, name=name, grid=(KVH, grp, L // tq, nk),
        in_specs=[qspec, kspec, kspec, qspec, qspec, qspec], out_specs=[qspec, colspec, colspec],
        out_shape=[jax.ShapeDtypeStruct((L, H * hd), F32)] + [jax.ShapeDtypeStruct((L, KVH * hd), F32)] * 2,
        scratch_shapes=[pltpu.VMEM((tq, hd), F32)],
        compiler_params=_cparams(("parallel", "arbitrary", "arbitrary", "arbitrary")),
    )(qh, kh, vh, do, lse, delta)


def _attn_dkv(qh, kh, vh, do, lse, delta, *, tq, tk, name, d):
    L, H, KVH, hd = qh.shape[0], d['AH'], d['AKV'], d['AD']
    grp = H // KVH
    tq, tk = _pick(tq, L), _pick(tk, L)
    nq = L // tq

    def body(q_ref, k_ref, v_ref, do_ref, lse_ref, dl_ref, dk_ref, dv_ref, dk_s, dv_s):
        g, i = pl.program_id(2), pl.program_id(3)

        @pl.when(jnp.logical_and(g == 0, i == 0))
        def _():
            dk_s[...] = jnp.zeros_like(dk_s)
            dv_s[...] = jnp.zeros_like(dv_s)

        q, do_ = q_ref[...], do_ref[...].astype(BF16)
        s = lax.dot_general(q, k_ref[...], _NT, preferred_element_type=F32)
        p = jnp.exp(s - _wide(lse_ref[...], tk))
        dv_s[...] += lax.dot_general(p.astype(BF16), do_, _TN, preferred_element_type=F32)
        dp = lax.dot_general(do_, v_ref[...], _NT, preferred_element_type=F32)
        ds = (p * (dp - _wide(dl_ref[...], tk))).astype(BF16)
        dk_s[...] += lax.dot_general(ds, q, _TN, preferred_element_type=F32)

        @pl.when(jnp.logical_and(g == grp - 1, i == nq - 1))
        def _():
            dk_ref[...] = dk_s[...]
            dv_ref[...] = dv_s[...]

    qspec = pl.BlockSpec((tq, hd), lambda kv, j, g, i: (i, kv * grp + g))
    kspec = pl.BlockSpec((tk, hd), lambda kv, j, g, i: (j, kv))
    return pl.pallas_call(
        body, name=name, grid=(KVH, L // tk, grp, nq),
        in_specs=[qspec, kspec, kspec, qspec, qspec, qspec], out_specs=[kspec, kspec],
        out_shape=[jax.ShapeDtypeStruct((L, KVH * hd), F32)] * 2,
        scratch_shapes=[pltpu.VMEM((tk, hd), F32), pltpu.VMEM((tk, hd), F32)],
        compiler_params=_cparams(("parallel", "parallel", "arbitrary", "arbitrary")),
    )(qh, kh, vh, do, lse, delta)


def _attn_dq(qh, kh, vh, do, lse, delta, *, tq, tk, name, d):
    L, H, KVH, hd = qh.shape[0], d['AH'], d['AKV'], d['AD']
    grp = H // KVH
    tq, tk = _pick(tq, L), _pick(tk, L)
    nk = L // tk

    def body(q_ref, k_ref, v_ref, do_ref, lse_ref, dl_ref, dq_ref, dq_s):
        j = pl.program_id(2)

        @pl.when(j == 0)
        def _():
            dq_s[...] = jnp.zeros_like(dq_s)

        k = k_ref[...]
        s = lax.dot_general(q_ref[...], k, _NT, preferred_element_type=F32)
        p = jnp.exp(s - _wide(lse_ref[...], tk))
        dp = lax.dot_general(do_ref[...].astype(BF16), v_ref[...], _NT, preferred_element_type=F32)
        ds = (p * (dp - _wide(dl_ref[...], tk))).astype(BF16)
        dq_s[...] += jnp.dot(ds, k, preferred_element_type=F32)

        @pl.when(j == nk - 1)
        def _():
            dq_ref[...] = dq_s[...]

    qspec = pl.BlockSpec((tq, hd), lambda h, i, j: (i, h))
    kspec = pl.BlockSpec((tk, hd), lambda h, i, j: (j, h // grp))
    return pl.pallas_call(
        body, name=name, grid=(H, L // tq, nk),
        in_specs=[qspec, kspec, kspec, qspec, qspec, qspec], out_specs=qspec,
        out_shape=jax.ShapeDtypeStruct((L, H * hd), F32),
        scratch_shapes=[pltpu.VMEM((tq, hd), F32)],
        compiler_params=_cparams(("parallel", "parallel", "arbitrary")),
    )(qh, kh, vh, do, lse, delta)


def _loss_grad(x, g, tgt, *, tm, name):
    L, D = x.shape
    tm = _pick(tm, L)

    def body(x_ref, g_ref, t_ref, loss_ref, dx_ref, dg_ref):
        def f(xv, gv):
            err = _rms(xv, gv) - t_ref[...]
            return 0.5 * jnp.sum(jnp.mean(err * err, axis=-1, keepdims=True))

        val, vjp = jax.vjp(f, x_ref[...], g_ref[...])
        dx, dg = vjp(jnp.ones((), F32))
        dx_ref[...] = dx

        @pl.when(pl.program_id(0) == 0)
        def _():
            loss_ref[...] = jnp.zeros_like(loss_ref)
            dg_ref[...] = jnp.zeros_like(dg_ref)

        loss_ref[...] += val
        dg_ref[...] += dg

    return pl.pallas_call(
        body, name=name, grid=(L // tm,),
        in_specs=[pl.BlockSpec((tm, D), lambda i: (i, 0)), pl.BlockSpec((1, D), lambda i: (0, 0)),
                  pl.BlockSpec((tm, D), lambda i: (i, 0))],
        out_specs=[pl.BlockSpec((8, LANE), lambda i: (0, 0)), pl.BlockSpec((tm, D), lambda i: (i, 0)),
                   pl.BlockSpec((1, D), lambda i: (0, 0))],
        out_shape=[jax.ShapeDtypeStruct((8, LANE), F32), jax.ShapeDtypeStruct((L, D), F32),
                   jax.ShapeDtypeStruct((1, D), F32)],
        compiler_params=_cparams(("arbitrary",)),
    )(x, g, tgt)


def _sum_slots(recv, *, tr, name):
    n, R, W = recv.shape
    tr = _pick(tr, R)

    def body(r_ref, o_ref):
        s = r_ref[0].astype(F32)
        for i in range(1, n):
            s = s + r_ref[i].astype(F32)
        o_ref[...] = s

    return pl.pallas_call(
        body, name=name, grid=(R // tr,),
        in_specs=[pl.BlockSpec((n, tr, W), lambda i: (0, i, 0))], out_specs=pl.BlockSpec((tr, W), lambda i: (i, 0)),
        out_shape=jax.ShapeDtypeStruct((R, W), F32), compiler_params=_cparams(("parallel",)),
    )(recv)


def _adamw(w, g, m, v, *, tr, name):
    R, W = w.shape
    tr = _pick(tr, R)
    c1 = 1.0 - ADAM_B1 ** ADAM_STEP
    c2 = 1.0 - ADAM_B2 ** ADAM_STEP

    def body(w_ref, g_ref, m_ref, v_ref, d_ref, nm_ref, nv_ref):
        gv = g_ref[...]
        nm = ADAM_B1 * m_ref[...] + (1.0 - ADAM_B1) * gv
        nv = ADAM_B2 * v_ref[...] + (1.0 - ADAM_B2) * (gv * gv)
        d_ref[...] = -ADAM_LR * ((nm / c1) / (jnp.sqrt(nv / c2) + ADAM_EPS) + ADAM_WD * w_ref[...])
        nm_ref[...] = nm
        nv_ref[...] = nv

    spec = pl.BlockSpec((tr, W), lambda i: (i, 0))
    return pl.pallas_call(
        body, name=name, grid=(R // tr,), in_specs=[spec] * 4, out_specs=[spec] * 3,
        out_shape=[jax.ShapeDtypeStruct((R, W), F32)] * 3, compiler_params=_cparams(("parallel",)),
    )(w, g, m, v)


_MESH = pl.DeviceIdType.MESH


def _all_gather(xs, *, name):
    R, W = xs.shape

    def body(x_ref, out_ref, send_sems, recv_sems, local_sem):
        x, y, c = lax.axis_index("x"), lax.axis_index("y"), lax.axis_index("c")
        me, sibling = (x, y, c), (x, y, 1 - c)
        chips = [(1 - x, y), (x, 1 - y), (1 - x, 1 - y)]

        def slot(px, py, pc):
            return out_ref.at[4 * px + 2 * py + pc]

        def copy(k, block, to, src=None):
            return pltpu.make_async_remote_copy(
                src_ref=slot(*block) if src is None else src, dst_ref=slot(*block),
                send_sem=send_sems.at[k], recv_sem=recv_sems.at[k], device_id=to, device_id_type=_MESH)

        mine = pltpu.make_async_copy(x_ref, slot(*me), local_sem)
        mine.start()
        first = [copy(0, me, sibling, src=x_ref)]
        first += [copy(1 + j, me, (*chip, c), src=x_ref) for j, chip in enumerate(chips)]
        for cp in first:
            cp.start()
        passed = [copy(4 + j, (*chip, c), sibling) for j, chip in enumerate(chips)]
        for j, chip in enumerate(chips):
            copy(1 + j, (*chip, c), me).wait_recv()
            passed[j].start()
        copy(0, sibling, me).wait_recv()
        for j, chip in enumerate(chips):
            copy(4 + j, (*chip, 1 - c), me).wait_recv()
        for cp in first + passed:
            cp.wait_send()
        mine.wait()

    return pl.pallas_call(
        body, name=name,
        out_shape=jax.ShapeDtypeStruct((N_DEV, R, W), xs.dtype),
        in_specs=[pl.BlockSpec(memory_space=pl.ANY)], out_specs=pl.BlockSpec(memory_space=pl.ANY),
        scratch_shapes=[pltpu.SemaphoreType.DMA((7,)), pltpu.SemaphoreType.DMA((7,)), pltpu.SemaphoreType.DMA],
    )(xs)


def _all_to_all(gs, *, name):
    n, R, W = gs.shape

    def body(g_ref, out_ref, send_sems, recv_sems, local_sem):
        x, y, c = lax.axis_index("x"), lax.axis_index("y"), lax.axis_index("c")
        me = 4 * x + 2 * y + c

        def peer(mask):
            return (x ^ (mask >> 2), y ^ ((mask >> 1) & 1), c ^ (mask & 1))

        def copy(mask):
            px, py, pc = peer(mask)
            return pltpu.make_async_remote_copy(
                src_ref=g_ref.at[4 * px + 2 * py + pc], dst_ref=out_ref.at[me],
                send_sem=send_sems.at[mask - 1], recv_sem=recv_sems.at[mask - 1],
                device_id=(px, py, pc), device_id_type=_MESH)

        def arrival(mask):
            px, py, pc = peer(mask)
            return pltpu.make_async_remote_copy(
                src_ref=g_ref.at[me], dst_ref=out_ref.at[4 * px + 2 * py + pc],
                send_sem=send_sems.at[mask - 1], recv_sem=recv_sems.at[mask - 1],
                device_id=(px, py, pc), device_id_type=_MESH)

        mine = pltpu.make_async_copy(g_ref.at[me], out_ref.at[me], local_sem)
        mine.start()
        sends = [copy(mask) for mask in range(1, n)]
        for cp in sends:
            cp.start()
        for mask in range(1, n):
            arrival(mask).wait_recv()
        for cp in sends:
            cp.wait_send()
        mine.wait()

    return pl.pallas_call(
        body, name=name,
        out_shape=jax.ShapeDtypeStruct((n, R, W), gs.dtype),
        in_specs=[pl.BlockSpec(memory_space=pl.ANY)], out_specs=pl.BlockSpec(memory_space=pl.ANY),
        scratch_shapes=[pltpu.SemaphoreType.DMA((n - 1,)), pltpu.SemaphoreType.DMA((n - 1,)),
                        pltpu.SemaphoreType.DMA],
    )(gs)


def _rows_of(shape):
    return -(-int(np.prod(shape)) // PACK_W)


def _pack(arrs, dtype, lead=0, total_rows=None):
    pieces = []
    for a in arrs:
        f = a.reshape(a.shape[:lead] + (-1,)).astype(dtype)
        pad = (-f.shape[-1]) % PACK_W
        if pad:
            f = jnp.pad(f, [(0, 0)] * lead + [(0, pad)])
        pieces.append(f.reshape(a.shape[:lead] + (-1, PACK_W)))
    buf = jnp.concatenate(pieces, axis=lead)
    if total_rows is not None and buf.shape[lead] < total_rows:
        buf = jnp.pad(buf, [(0, 0)] * lead + [(0, total_rows - buf.shape[lead]), (0, 0)])
    return buf


def _unpack(buf, shapes, lead=0):
    out, r = [], 0
    for shp in shapes:
        n, rows = int(np.prod(shp)), _rows_of(shp)
        piece = buf[(slice(None),) * lead + (slice(r, r + rows),)]
        piece = piece.reshape(buf.shape[:lead] + (-1,))[..., :n]
        out.append(piece.reshape(buf.shape[:lead] + tuple(shp)))
        r += rows
    return out


def _to_full(parts):
    dep, r = parts.shape[1:3]
    return jnp.transpose(parts, (1, 0) + tuple(range(2, parts.ndim))).reshape((dep, N_DEV * r) + parts.shape[3:])


def _to_slabs(full):
    dep, r = full.shape[:2]
    t = full.reshape((dep, N_DEV, r // N_DEV) + full.shape[2:])
    return jnp.transpose(t, (1, 0) + tuple(range(2, t.ndim)))


def _ref_cols(parts, ro, wd):
    w, out = parts.shape[2], []
    for dev in range(N_DEV):
        lo, hi = max(ro, dev * w), min(ro + wd, (dev + 1) * w)
        if lo < hi:
            out.append(parts[dev][:, lo - dev * w:hi - dev * w])
    return out


def _w_in_to_layout(parts, seg, rseg, nh2):
    D = parts.shape[1]
    cols, off = [], 0
    names = sorted([k for k in seg if not k.startswith('_')], key=lambda k: seg[k][0])
    for nm in names:
        o, wd = seg[nm]
        if o > off:
            cols.append(jnp.zeros((D, o - off), parts.dtype))
        if nm == 'dadb':
            cols += _ref_cols(parts, rseg['da'][0], nh2) + _ref_cols(parts, rseg['db'][0], nh2)
            cols.append(jnp.zeros((D, wd - 2 * nh2), parts.dtype))
        else:
            cols += _ref_cols(parts, rseg[nm][0], wd)
        off = o + wd
    if seg['_total'] > off:
        cols.append(jnp.zeros((D, seg['_total'] - off), parts.dtype))
    return jnp.concatenate(cols, axis=1)


def _w_in_slabs(dw, seg, rseg, nh2):
    w = rseg['_total'] // N_DEV
    ref = []
    for nm in sorted([k for k in rseg if not k.startswith('_')], key=lambda k: rseg[k][0]):
        lo = {'da': seg['dadb'][0], 'db': seg['dadb'][0] + nh2}.get(nm)
        ref.append((rseg[nm][0], rseg[nm][1], seg[nm][0] if lo is None else lo))
    slabs = []
    for dev in range(N_DEV):
        cols = []
        for ro, wd, lo in ref:
            a, b = max(ro, dev * w), min(ro + wd, (dev + 1) * w)
            if a < b:
                cols.append(dw[:, lo + a - ro:lo + b - ro])
        slabs.append(jnp.concatenate(cols, axis=1))
    return jnp.stack(slabs, axis=0)


def _assemble_dh(pieces, seg, L):
    cols, off = [], 0
    for nm in sorted(pieces, key=lambda k: seg[k][0]):
        o = seg[nm][0]
        if o > off:
            cols.append(jnp.zeros((L, o - off), F32))
        cols.append(pieces[nm])
        off = o + pieces[nm].shape[1]
    if seg['_total'] > off:
        cols.append(jnp.zeros((L, seg['_total'] - off), F32))
    return jnp.concatenate(cols, axis=1)


def _lane_pad(v):
    v = v.reshape(1, -1)
    return jnp.pad(v, ((0, 0), (0, LANE - v.shape[1])))


def _rope_tables(L, c):
    rows = L // c['GRID_W']
    row = jnp.repeat(jnp.arange(rows), c['GRID_W']).astype(F32)
    col = jnp.tile(jnp.arange(c['GRID_W']), rows).astype(F32)
    axis_dim = c['AD'] // 2
    freqs = c['ROPE_THETA'] ** (-jnp.arange(0, axis_dim, 2, dtype=F32) / axis_dim)
    ang = jnp.concatenate([row[:, None] * freqs, col[:, None] * freqs], axis=-1)
    cosf = jnp.repeat(jnp.cos(ang), 2, axis=1)
    sn = jnp.sin(ang)
    sins = jnp.stack([-sn, sn], axis=-1).reshape(L, c['AD'])
    idx = np.arange(c['AD'])
    perm = np.zeros((c['AD'], c['AD']), np.float32)
    perm[idx, idx ^ 1] = 1.0
    return cosf, sins, jnp.asarray(perm)


def _s5_dir_params(a, l, dr):
    return (a['ssm_a_re'][l, dr], a['ssm_a_im'][l, dr], a['ssm_log_step'][l, dr], a['ssm_b_re'][l, dr],
            a['ssm_b_im'][l, dr], a['ssm_c_re'][l, dr], a['ssm_c_im'][l, dr])


def _layer_fwd(x, mem, l, wt, a, rope, c, d, seg):
    L, D = x.shape
    SW, DW, AW, AKW, MW, H = d['SW'], d['DW'], d['AW'], d['AKW'], d['MW'], d['DNH']
    cb = lambda nm: seg[nm][0] // seg[nm][1]
    sv = {'x': x}
    p = f"l{l}_"
    sv['g_norm'] = a['norm_g'][l][None, :]
    xn, = _rowwise(_f_norm, [(x, D, 0)], [sv['g_norm']], [(D, BF16)], tm=256, name=p + "norm")
    h = _mm(xn, wt['wp'], name=p + "in_proj", tm=1024, tn=1536, tk=1024)
    sv['xn'], sv['h'] = xn, h

    ysum, sv['s5'] = None, []
    for dr in range(2):
        wb, wc, lr, li = _s5_prep(*_s5_dir_params(a, l, dr), d)
        wb16, wc16 = wb.astype(BF16), wc.astype(BF16)
        lt = _s5_tables(lr, li, bool(dr), False)
        ysum, cin = _s5_fwd(h, cb('u_a'), wb16, wc16, lt, rev=bool(dr), acc=ysum, tb=256, name=p + f"s5_fwd{dr}", d=d)
        sv['s5'].append((wb16, wc16, lt, _s5_tables(lr, li, not bool(dr), True), cin))
    sv['ysum'] = ysum
    sv['s5_par'] = [a['ssm_d'][l][None, :], wt['w_glu'], a['ssm_b_glu'][l][None, :]]
    sv['s5_rows'] = [(ysum, SW, 0), (h, SW, cb('u_a')), (h, SW, cb('z_a'))]
    y_a, = _rowwise(_f_s5tail, sv['s5_rows'], sv['s5_par'], [(SW, F32)], tm=256, name=p + "s5_tail")

    act = _conv_fwd(h, cb('dq'), wt['conv'], tm=256, name=p + "dn_conv", d=d)
    sv['act'] = act
    sv['dn_par'] = [_lane_pad(a['dn_a_log'][l]), _lane_pad(a['dn_dt_bias'][l])]
    sv['dn_rows'] = [(act, DW, 0), (act, DW, 1), (h, LANE, seg['dadb'][0] // LANE)]
    dn_out = _rowwise(_make_f_dnpre(H, d['DNK'], c['CHUNK']), sv['dn_rows'], sv['dn_par'], [(DW, F32)] * 8,
                      tm=256, name=p + "dn_pre")
    qn, kn = dn_out[:2]
    sv['qn'], sv['kn'], sv['gates'] = qn, kn, [dn_out[2:5], dn_out[5:8]]
    o_dn, sv['dn_state'] = None, []
    for dr in range(2):
        o_dn, ss = _delta_fwd(qn, kn, act, sv['gates'][dr], vcb=2, rev=bool(dr), acc=o_dn,
                              name=p + f"dn_fwd{dr}", d=d)
        sv['dn_state'].append(ss)
    sv['dnpost_rows'] = [(o_dn, DW, 0), (h, DW, cb('z_b'))]
    sv['dnpost_par'] = [a['dn_norm_g'][l][None, :]]
    y_b, = _rowwise(_make_f_dnpost(d['DNK']), sv['dnpost_rows'], sv['dnpost_par'], [(DW, F32)], tm=256,
                    name=p + "dn_post")

    cosf, sins, perm = rope
    sv['att_par'] = [perm, a['attn_q_norm'][l][None, :], a['attn_k_norm'][l][None, :]]
    qh, kh, vh = _rowwise(_make_f_attpre(d['AD'], True),
                          [(h, AW, cb('aq')), (h, AKW, cb('ak')), (h, AKW, cb('av')), (cosf, d['AD'], 0),
                           (sins, d['AD'], 0)], sv['att_par'], [(AW, BF16), (AKW, BF16), (AKW, BF16)],
                          tm=256, name=p + "att_pre")
    o_att, lse = _attn_fwd(qh, kh, vh, tq=TILES['att_q'], tk=TILES['att_k'], name=p + "att_fwd", d=d)
    sv['qh'], sv['kh'], sv['vh'], sv['o_att'], sv['lse'] = qh, kh, vh, o_att, lse
    y_c, = _rowwise(_f_gate, [(o_att, AW, 0), (h, AW, cb('z_c'))], [], [(AW, F32)], tm=256, name=p + "att_post")

    sv['g_mem'] = a['mem_norm_g'][l][None, :]
    memn, = _rowwise(_f_norm, [(mem, D, 0)], [sv['g_mem']], [(D, BF16)], tm=256, name=p + "mem_norm")
    kv = _mm(memn, wt['w_mem_kv'], name=p + "mem_kv")
    sv['memn'], sv['kv'] = memn, kv
    y_m, = _rowwise(_make_f_mem(d['MH'], d['MD']), [(h, MW, cb('mq')), (h, MW, cb('z_m'))], [kv], [(MW, F32)],
                    tm=256, name=p + "mem_attn")

    ys = [y_a, y_b, y_c, y_m]
    ps = [_mm(y, wb_, name=p + f"branch_proj{i}") for i, (y, wb_) in enumerate(zip(ys, wt['w_branch']))]
    gcb = seg['gates'][0] // D
    sv['merge_rows'] = [(pp, D, 0) for pp in ps] + [(h, D, gcb + i) for i in range(4)]
    merged, = _rowwise(_f_merge, sv['merge_rows'], [], [(D, BF16)], tm=128, name=p + "merge")
    sv['ys'], sv['merged'] = ys, merged
    return _mm(merged, wt['w_out'], add=x, name=p + "out_proj"), sv


def _layer_bwd(dx, mem, l, wt, a, rope, sv, c, d, seg):
    L, D = dx.shape
    SW, DW, AW, AKW, MW, H = d['SW'], d['DW'], d['AW'], d['AKW'], d['MW'], d['DNH']
    cb = lambda nm: seg[nm][0] // seg[nm][1]
    p = f"l{l}_"
    h = sv['h']
    gr = {}
    dmerged = _mm(dx, wt['w_out'], tb=True, name=p + "d_merged")
    gr['w_out'] = _mm(sv['merged'], dx, ta=True, name=p + "dw_out")
    dmr, _ = _rowwise_bwd(_f_merge, sv['merge_rows'], [], [[(dmerged, D, 0)]], [True] * 8, [], tm=128,
                          name=p + "merge_bwd")
    dps, dgates = dmr[:4], dmr[4:]
    dys = [_mm(dp, wb_, tb=True, name=p + f"d_branch{i}") for i, (dp, wb_) in enumerate(zip(dps, wt['w_branch']))]
    gr['w_branch'] = jnp.concatenate(
        [_mm(y, dp, ta=True, name=p + f"dw_branch{i}") for i, (y, dp) in enumerate(zip(sv['ys'], dps))], axis=0)

    (dmq, dzm), (dkv,) = _rowwise_bwd(_make_f_mem(d['MH'], d['MD']), [(h, MW, cb('mq')), (h, MW, cb('z_m'))],
                                      [sv['kv']], [[(dys[3], MW, 0)]], [True, True], [True], tm=256,
                                      name=p + "mem_attn_bwd")
    gr['w_mem_kv'] = _mm(sv['memn'], dkv, ta=True, name=p + "dw_mem_kv")
    dmemn = _mm(dkv, wt['w_mem_kv'], tb=True, name=p + "d_memn")
    _, (dg_mem,) = _rowwise_bwd(_f_norm, [(mem, D, 0)], [sv['g_mem']], [[(dmemn, D, 0)]], [False], [True], tm=256,
                                name=p + "mem_norm_bwd")
    gr['mem_norm_g'] = dg_mem[0]

    (do_att, dzc), _ = _rowwise_bwd(_f_gate, [(sv['o_att'], AW, 0), (h, AW, cb('z_c'))], [], [[(dys[2], AW, 0)]],
                                    [True, True], [], tm=256, name=p + "att_post_bwd")
    delta, = _rowwise(_make_f_delta(d['AD']), [(do_att, AW, 0), (sv['o_att'], AW, 0)], [], [(AW, F32)], tm=256,
                      name=p + "att_delta")
    att_in = (sv['qh'], sv['kh'], sv['vh'], do_att, sv['lse'], delta)
    dqh, dkh, dvh = _attn_bwd(*att_in, tq=TILES['att_q'], tk=TILES['att_k'], name=p + "att_bwd", d=d)
    cosf, sins, _ = rope
    (daq, dak), (dqg, dkg) = _rowwise_bwd(
        _make_f_attpre(d['AD'], False),
        [(h, AW, cb('aq')), (h, AKW, cb('ak')), (cosf, d['AD'], 0), (sins, d['AD'], 0)], sv['att_par'],
        [[(dqh, AW, 0)], [(dkh, AKW, 0)]], [True, True, False, False], [False, True, True], tm=256,
        name=p + "att_pre_bwd")
    gr['attn_q_norm'], gr['attn_k_norm'] = dqg[0], dkg[0]

    (do_dn, dzb), (dng,) = _rowwise_bwd(_make_f_dnpost(d['DNK']), sv['dnpost_rows'], sv['dnpost_par'],
                                        [[(dys[1], DW, 0)]], [True, True], [True], tm=256, name=p + "dn_post_bwd")
    gr['dn_norm_g'] = dng[0]
    accs, dn_dgates = None, []
    for dr in range(2):
        res = _delta_bwd(sv['qn'], sv['kn'], sv['act'], sv['gates'][dr], sv['dn_state'][dr], do_dn, vcb=2,
                         rev=bool(dr), accs=accs, name=p + f"dn_bwd{dr}", d=d)
        accs = res[:3]
        dn_dgates += res[3:]
    dqn, dkn, dvc = accs
    (dqc, dkc, ddadb), (dalog, ddtb) = _rowwise_bwd(
        _make_f_dnpre(H, d['DNK'], c['CHUNK']), sv['dn_rows'], sv['dn_par'],
        [[(t, DW, 0)] for t in [dqn, dkn] + dn_dgates], [True] * 3, [True, True], tm=256, name=p + "dn_pre_bwd")
    gr['dn_a_log'] = dalog[0, :2 * H].reshape(2, H)
    gr['dn_dt_bias'] = ddtb[0, :2 * H].reshape(2, H)
    dconv_x, dconv_w = _conv_bwd(h, cb('dq'), wt['conv'], jnp.concatenate([dqc, dkc, dvc], axis=1), tm=256,
                                 name=p + "dn_conv_bwd", d=d)
    gr['dn_conv'] = jnp.transpose(dconv_w[:, :c['CONV'], :], (0, 2, 1)).reshape(3 * DW, c['CONV'])

    (dysum, du, dza), (dd, dwglu, dbglu) = _rowwise_bwd(_f_s5tail, sv['s5_rows'], sv['s5_par'], [[(dys[0], SW, 0)]],
                                                        [True] * 3, [True] * 3, tm=256, name=p + "s5_tail_bwd")
    gr['ssm_d'], gr['ssm_w_glu'], gr['ssm_b_glu'] = dd[0], dwglu, dbglu[0]
    s5g = []
    for dr in range(2):
        wb16, wc16, lt, lt_adj, cin = sv['s5'][dr]
        du, dwb, dwc, dlam = _s5_bwd(h, cb('u_a'), dysum, cin, wb16, wc16, lt, lt_adj, rev=bool(dr), acc=du, tb=256,
                                     name=p + f"s5_bwd{dr}", d=d)
        dl = jnp.sum(dlam, axis=0).reshape(d['NB'], 2, d['BS'])
        _, prep_vjp = jax.vjp(lambda *pp: _s5_prep(*pp, d), *_s5_dir_params(a, l, dr))
        s5g.append(prep_vjp((dwb, dwc, dl[:, 0], dl[:, 1])))
    for i, nm in enumerate(['ssm_a_re', 'ssm_a_im', 'ssm_log_step', 'ssm_b_re', 'ssm_b_im', 'ssm_c_re', 'ssm_c_im']):
        gr[nm] = jnp.stack([s5g[0][i], s5g[1][i]], axis=0)

    dh = _assemble_dh({'u_a': du, 'z_a': dza, 'dq': dconv_x, 'z_b': dzb, 'ak': dak, 'av': dvh, 'aq': daq,
                       'z_c': dzc, 'mq': dmq, 'z_m': dzm, 'gates': jnp.concatenate(dgates, axis=1),
                       'dadb': ddadb}, seg, L).astype(BF16)
    gr['wp'] = _mm(sv['xn'], dh, ta=True, name=p + "dw_in", tm=1024, tn=1536, tk=1024)
    dxn = _mm(dh, wt['wp'], tb=True, name=p + "d_xn", tm=1024, tn=1024, tk=1536)
    (dx_in,), (dg_norm,) = _rowwise_bwd(_f_norm, [(sv['x'], D, 0)], [sv['g_norm']], [[(dxn, D, 0)]], [True], [True],
                                        tm=256, name=p + "norm_bwd", accs={0: (dx, D, 0)})
    gr['norm_g'] = dg_norm[0]
    return dx_in, gr


_ARG_NAMES = (['x', 'mem'] + WEIGHTS + ['loss_target'] + ['m_' + w for w in WEIGHTS] + ['v_' + w for w in WEIGHTS])


def kernel(x, mem, norm_g, w_in, ssm_a_re, ssm_a_im, ssm_log_step, ssm_b_re, ssm_b_im, ssm_c_re, ssm_c_im,
           ssm_d, ssm_w_glu, ssm_b_glu, dn_conv, dn_a_log, dn_dt_bias, dn_norm_g, attn_q_norm, attn_k_norm,
           mem_norm_g, w_mem_kv, w_branch, w_out, final_norm_g, loss_target, m_norm_g, m_w_in, m_ssm_a_re,
           m_ssm_a_im, m_ssm_log_step, m_ssm_b_re, m_ssm_b_im, m_ssm_c_re, m_ssm_c_im, m_ssm_d, m_ssm_w_glu,
           m_ssm_b_glu, m_dn_conv, m_dn_a_log, m_dn_dt_bias, m_dn_norm_g, m_attn_q_norm, m_attn_k_norm,
           m_mem_norm_g, m_w_mem_kv, m_w_branch, m_w_out, m_final_norm_g, v_norm_g, v_w_in, v_ssm_a_re,
           v_ssm_a_im, v_ssm_log_step, v_ssm_b_re, v_ssm_b_im, v_ssm_c_re, v_ssm_c_im, v_ssm_d, v_ssm_w_glu,
           v_ssm_b_glu, v_dn_conv, v_dn_a_log, v_dn_dt_bias, v_dn_norm_g, v_attn_q_norm, v_attn_k_norm,
           v_mem_norm_g, v_w_mem_kv, v_w_branch, v_w_out, v_final_norm_g):
    given = locals()
    return _train_step({n: given[n] for n in _ARG_NAMES})


def _train_step(a):
    c = CFG
    d = _dims(c)
    seg, rseg = _layout(c)
    depth, nh2 = c['DEPTH'], 2 * c['DNH']
    x, mem, tgt = a['x'][0], a['mem'][0], a['loss_target'][0]
    L, D = x.shape

    packed = [n for n in SHARDED if n != 'w_in']
    shard_shapes = [a[n].shape for n in packed]
    rw = _round_up(sum(_rows_of(s) for s in shard_shapes), LANE)
    win_shape = a['w_in'].shape
    wcols = win_shape[2]
    g_win = _all_gather(a['w_in'].astype(BF16).reshape(depth * D, wcols), name="w_in_all_gather")
    gathered = _all_gather(_pack([a[n] for n in packed], BF16, total_rows=rw), name="weights_all_gather")
    full = {n: _to_full(p_) for n, p_ in zip(packed, _unpack(gathered, shard_shapes, lead=1))}
    offs = np.cumsum([0, d['SW'], d['DW'], d['AW'], d['MW']])
    wts = []
    for l in range(depth):
        conv = jnp.transpose(full['dn_conv'][l].astype(F32).reshape(3, d['DW'], c['CONV']), (0, 2, 1))
        wts.append(dict(
            wp=_w_in_to_layout(g_win[:, l * D:(l + 1) * D], seg, rseg, nh2),
            w_branch=[full['w_branch'][l, offs[i]:offs[i + 1]] for i in range(4)],
            w_out=full['w_out'][l], w_mem_kv=full['w_mem_kv'][l], w_glu=full['ssm_w_glu'][l].astype(F32),
            conv=jnp.pad(conv, ((0, 0), (0, 8 - c['CONV']), (0, 0)))))
    rope = _rope_tables(L, c)

    saved = []
    for l in range(depth):
        x, sv = _layer_fwd(x, mem, l, wts[l], a, rope, c, d, seg)
        saved.append(sv)
    loss_part, dx, dg_final = _loss_grad(x, a['final_norm_g'][None, :], tgt, tm=256, name="final_norm_loss")
    grads = [None] * depth
    for l in reversed(range(depth)):
        dx, grads[l] = _layer_bwd(dx, mem, l, wts[l], a, rope, saved[l], c, d, seg)

    gfull = {n: jnp.stack([grads[l][n] for l in range(depth)], axis=0) for n in WEIGHTS
             if n not in ('w_in', 'final_norm_g')}
    gfull['final_norm_g'] = dg_final[0]

    win_slabs = jnp.concatenate([_w_in_slabs(grads[l]['wp'], seg, rseg, nh2) for l in range(depth)], axis=1)
    g_win_sum = _sum_slots(_all_to_all(win_slabs.astype(BF16), name="w_in_grad_all_to_all"), tr=256,
                           name="w_in_grad_sum")
    flat = lambda t: t.reshape(depth * D, wcols)
    d_win, m_win, v_win = _adamw(flat(a['w_in']), g_win_sum, flat(a['m_w_in']), flat(a['v_w_in']), tr=256,
                                 name="w_in_adamw")
    win_out = [t.reshape(win_shape) for t in (g_win_sum, d_win, m_win, v_win)]

    small_shapes = [a[n].shape for n in SMALL] + [(1,)]
    rs = _round_up(sum(_rows_of(s) for s in small_shapes), LANE)
    g_shard = _pack([_to_slabs(gfull[n]) for n in packed], F32, lead=1, total_rows=rw)
    g_small = _pack([gfull[n] for n in SMALL] + [loss_part[0, :1]], F32, total_rows=rs)
    slabs = jnp.concatenate([g_shard, jnp.broadcast_to(g_small[None], (N_DEV,) + g_small.shape)], axis=1)
    gsum = _sum_slots(_all_to_all(slabs, name="grads_all_to_all"), tr=256, name="grad_sum")

    def local_pack(prefix):
        zero = jnp.zeros((1,), F32)
        return jnp.concatenate([_pack([a[prefix + n] for n in packed], F32, total_rows=rw),
                                _pack([a[prefix + n] for n in SMALL] + [zero], F32, total_rows=rs)], axis=0)

    delta, new_m, new_v = _adamw(local_pack(''), gsum, local_pack('m_'), local_pack('v_'), tr=256, name="adamw")

    def split(buf):
        vals = dict(zip(packed, _unpack(buf[:rw], shard_shapes)))
        small = _unpack(buf[rw:], small_shapes)
        vals.update(zip(SMALL, small[:-1]))
        return vals, small[-1]

    _, loss = split(gsum)
    outs = [loss.reshape(()), dx[None]]
    for i, buf in enumerate((gsum, delta, new_m, new_v)):
        vals, _ = split(buf)
        vals['w_in'] = win_out[i]
        outs += [vals[n] for n in WEIGHTS]
    return tuple(outs)
```

```python
import functools
import math

import numpy as np
import jax
import jax.numpy as jnp
from jax import lax
from jax.experimental import pallas as pl
from jax.experimental.pallas import tpu as pltpu

F32 = jnp.float32
BF16 = jnp.bfloat16
HI = lax.Precision.HIGHEST
EPS = 1e-6
LANE = 128
SUBLANE = 8
VMEM_LIMIT = 56 * 1024 * 1024
N_DEV = 8
PACK_W = 1024

ADAM_LR, ADAM_B1, ADAM_B2, ADAM_EPS, ADAM_WD, ADAM_STEP = 0.001, 0.9, 0.999, 1e-08, 0.01, 10

CFG = dict(D=2048, L=8192, GRID_W=64, NMEM=256, DEPTH=2,
           SG=48, SP=16, SN=64,
           DNH=6, DNK=128, CONV=5, CHUNK=64,
           AH=8, AKV=2, AD=128, ROPE_THETA=10000.0,
           MH=4, MD=128)

TILES = dict(att_q=512, att_k=512)

WEIGHTS = ['norm_g', 'w_in', 'ssm_a_re', 'ssm_a_im', 'ssm_log_step', 'ssm_b_re', 'ssm_b_im', 'ssm_c_re',
           'ssm_c_im', 'ssm_d', 'ssm_w_glu', 'ssm_b_glu', 'dn_conv', 'dn_a_log', 'dn_dt_bias', 'dn_norm_g',
           'attn_q_norm', 'attn_k_norm', 'mem_norm_g', 'w_mem_kv', 'w_branch', 'w_out', 'final_norm_g']
SHARDED = ['w_in', 'w_branch', 'w_out', 'w_mem_kv', 'ssm_w_glu', 'dn_conv']
SMALL = [w for w in WEIGHTS if w not in SHARDED]


def _dims(c):
    d = dict(c)
    d['SW'] = c['SG'] * c['SP']
    d['NB'] = d['SW'] // LANE
    d['GPB'] = LANE // c['SP']
    d['BS'] = d['GPB'] * c['SN']
    d['DW'] = c['DNH'] * c['DNK']
    d['AW'] = c['AH'] * c['AD']
    d['AKW'] = c['AKV'] * c['AD']
    d['MW'] = c['MH'] * c['MD']
    d['BT'] = d['SW'] + d['DW'] + d['AW'] + d['MW']
    return d


def _round_up(a, b):
    return (a + b - 1) // b * b


def _layout(c):
    d = _dims(c)
    D, SW, DW, AW, AKW, MW = d['D'], d['SW'], d['DW'], d['AW'], d['AKW'], d['MW']
    order = [('u_a', SW, SW), ('z_a', SW, SW), ('dq', DW, DW), ('dk', DW, DW), ('dv', DW, DW), ('z_b', DW, DW),
             ('ak', AKW, AKW), ('av', AKW, AKW), ('aq', AW, AW), ('z_c', AW, AW), ('mq', MW, MW), ('z_m', MW, MW),
             ('gates', 4 * D, D), ('dadb', LANE, LANE)]
    off, seg = 0, {}
    for name, w, al in order:
        off = _round_up(off, al)
        seg[name] = (off, w)
        off += w
    seg['_total'] = _round_up(off, 512)
    ref_order = [('u_a', SW), ('z_a', SW), ('dq', DW), ('dk', DW), ('dv', DW), ('da', 2 * d['DNH']),
                 ('db', 2 * d['DNH']), ('z_b', DW), ('aq', AW), ('ak', AKW), ('av', AKW), ('z_c', AW),
                 ('mq', MW), ('z_m', MW), ('gates', 4 * D)]
    roff, rseg = 0, {}
    for name, w in ref_order:
        rseg[name] = (roff, w)
        roff += w
    rseg['_total'] = roff
    return seg, rseg


def _cparams(sem):
    return pltpu.CompilerParams(dimension_semantics=sem, vmem_limit_bytes=VMEM_LIMIT)


def _pick(t, n):
    if n <= t:
        return n
    for align in (LANE, 2 * SUBLANE):
        for cand in range(t - t % align, 0, -align):
            if n % cand == 0:
                return cand
    return n


def _mm(a, b, *, name, ta=False, tb=False, add=None, out_dtype=F32, tm=1024, tn=1024, tk=512):
    M, K = (a.shape[1], a.shape[0]) if ta else a.shape
    N = b.shape[0] if tb else b.shape[1]
    assert (b.shape[1] if tb else b.shape[0]) == K
    tm, tn, tk = _pick(tm, M), _pick(tn, N), _pick(tk, K)
    nk = K // tk
    dn = (((0 if ta else 1,), (1 if tb else 0,)), ((), ()))
    has_add = add is not None

    def body(*refs):
        if has_add:
            a_ref, b_ref, add_ref, o_ref, acc = refs
        else:
            a_ref, b_ref, o_ref, acc = refs
        k = pl.program_id(2)

        @pl.when(k == 0)
        def _():
            acc[...] = jnp.zeros_like(acc)

        acc[...] += lax.dot_general(a_ref[...].astype(BF16), b_ref[...].astype(BF16), dn,
                                    preferred_element_type=F32)

        @pl.when(k == nk - 1)
        def _():
            r = acc[...]
            if has_add:
                r = r + add_ref[...]
            o_ref[...] = r.astype(o_ref.dtype)

    a_spec = pl.BlockSpec((tk, tm), lambda i, j, k: (k, i)) if ta else pl.BlockSpec((tm, tk), lambda i, j, k: (i, k))
    b_spec = pl.BlockSpec((tn, tk), lambda i, j, k: (j, k)) if tb else pl.BlockSpec((tk, tn), lambda i, j, k: (k, j))
    in_specs = [a_spec, b_spec]
    args = [a, b]
    if has_add:
        in_specs.append(pl.BlockSpec((tm, tn), lambda i, j, k: (i, j)))
        args.append(add)
    return pl.pallas_call(
        body, name=name, grid=(M // tm, N // tn, nk),
        in_specs=in_specs, out_specs=pl.BlockSpec((tm, tn), lambda i, j, k: (i, j)),
        out_shape=jax.ShapeDtypeStruct((M, N), out_dtype),
        scratch_shapes=[pltpu.VMEM((tm, tn), F32)],
        compiler_params=_cparams(("parallel", "parallel", "arbitrary")),
    )(*args)


def _row_spec(tm, w, cb):
    return pl.BlockSpec((tm, w), lambda i, cb=cb: (i, cb))


def _rowwise(fn, rows, params, outs, *, tm, name):
    L = rows[0][0].shape[0]
    tm = _pick(tm, L)
    nr, npar = len(rows), len(params)

    def body(*refs):
        vals = [r[...] for r in refs[:nr + npar]]
        res = fn(*vals)
        for o_ref, v in zip(refs[nr + npar:], res):
            o_ref[...] = v.astype(o_ref.dtype)

    in_specs = [_row_spec(tm, w, cb) for (_, w, cb) in rows]
    in_specs += [pl.BlockSpec(p.shape, lambda i: (0, 0)) for p in params]
    res = pl.pallas_call(
        body, name=name, grid=(L // tm,), in_specs=in_specs,
        out_specs=[pl.BlockSpec((tm, w), lambda i: (i, 0)) for (w, _) in outs],
        out_shape=[jax.ShapeDtypeStruct((L, w), dt) for (w, dt) in outs],
        compiler_params=_cparams(("parallel",)),
    )(*[r[0] for r in rows], *params)
    return list(res)


def _rowwise_bwd(fn, rows, params, cts, drows, dparams, *, tm, name, accs=None):
    L = rows[0][0].shape[0]
    tm = _pick(tm, L)
    nr, npar = len(rows), len(params)
    accs = accs or {}
    ct_flat = [c for grp in cts for c in grp]
    ct_sizes = [len(grp) for grp in cts]
    acc_keys = sorted(accs)
    d_r = [i for i in range(nr) if drows[i]]
    d_p = [i for i in range(npar) if dparams[i]]
    n_in = nr + npar + len(ct_flat) + len(acc_keys)

    def body(*refs):
        vals = [r[...] for r in refs[:nr + npar]]
        ct_refs = refs[nr + npar:nr + npar + len(ct_flat)]
        acc_refs = refs[nr + npar + len(ct_flat):n_in]
        o_refs = refs[n_in:]
        ct_vals, pos = [], 0
        for n in ct_sizes:
            v = ct_refs[pos][...].astype(F32)
            for r in ct_refs[pos + 1:pos + n]:
                v = v + r[...].astype(F32)
            ct_vals.append(v)
            pos += n
        diff_idx = d_r + [nr + i for i in d_p]

        def g(*dv):
            full = list(vals)
            for i, v in zip(diff_idx, dv):
                full[i] = v
            return tuple(o.astype(F32) for o in fn(*full))

        _, vjp = jax.vjp(g, *[vals[i] for i in diff_idx])
        grads = vjp(tuple(ct_vals))
        for n, i in enumerate(d_r):
            gv = grads[n].astype(F32)
            if i in accs:
                gv = gv + acc_refs[acc_keys.index(i)][...]
            o_refs[n][...] = gv
        step = pl.program_id(0)
        for n, i in enumerate(d_p):
            o_ref = o_refs[len(d_r) + n]

            @pl.when(step == 0)
            def _(o_ref=o_ref):
                o_ref[...] = jnp.zeros_like(o_ref)

            o_ref[...] += grads[len(d_r) + n].astype(F32)

    in_specs = [_row_spec(tm, w, cb) for (_, w, cb) in rows]
    in_specs += [pl.BlockSpec(p.shape, lambda i: (0, 0)) for p in params]
    in_specs += [_row_spec(tm, w, cb) for (_, w, cb) in ct_flat]
    in_specs += [_row_spec(tm, accs[k][1], accs[k][2]) for k in acc_keys]
    out_specs = [pl.BlockSpec((tm, rows[i][1]), lambda i_: (i_, 0)) for i in d_r]
    out_specs += [pl.BlockSpec(params[i].shape, lambda i_: (0, 0)) for i in d_p]
    out_shape = [jax.ShapeDtypeStruct((L, rows[i][1]), F32) for i in d_r]
    out_shape += [jax.ShapeDtypeStruct(params[i].shape, F32) for i in d_p]
    res = pl.pallas_call(
        body, name=name, grid=(L // tm,), in_specs=in_specs, out_specs=out_specs, out_shape=out_shape,
        compiler_params=_cparams(("arbitrary",)),
    )(*[r[0] for r in rows], *params, *[c[0] for c in ct_flat], *[accs[k][0] for k in acc_keys])
    res = list(res)
    return res[:len(d_r)], res[len(d_r):]


def _silu(x):
    return x * jax.nn.sigmoid(x)


def _rms(x, g):
    return x * lax.rsqrt(jnp.mean(x * x, axis=-1, keepdims=True) + EPS) * g


def _softplus(x):
    return jnp.maximum(x, 0.0) + jnp.log1p(jnp.exp(-jnp.abs(x)))


def _heads(x, hd):
    return [x[:, i * hd:(i + 1) * hd] for i in range(x.shape[1] // hd)]


def _f_norm(x, g):
    return (_rms(x, g),)


def _f_s5tail(ys, u, z, d, wglu, bglu):
    y = jax.nn.gelu(ys + d * u)
    gate = jax.nn.sigmoid(jnp.dot(y.astype(BF16), wglu.astype(BF16), preferred_element_type=F32) + bglu)
    return (y * gate * _silu(z),)


def _make_f_dnpre(nh, hd, chunk):
    def f(qc, kc, dadb, alog, dtb):
        tm = qc.shape[0]
        qn = [q * lax.rsqrt(jnp.sum(q * q, axis=-1, keepdims=True) + EPS) * (hd ** -0.5) for q in _heads(qc, hd)]
        kn = [k * lax.rsqrt(jnp.sum(k * k, axis=-1, keepdims=True) + EPS) for k in _heads(kc, hd)]
        g = -jnp.exp(alog) * _softplus(dadb + dtb)
        beta = jax.nn.sigmoid(dadb)
        ii = lax.broadcasted_iota(jnp.int32, (tm, tm), 0)
        jj = lax.broadcasted_iota(jnp.int32, (tm, tm), 1)
        same = (ii // chunk) == (jj // chunk)
        outs = [jnp.concatenate(qn, axis=1), jnp.concatenate(kn, axis=1)]
        gt = jnp.dot(same.astype(F32), g, precision=HI, preferred_element_type=F32)
        for dr in range(2):
            tri = jnp.logical_and(same, (ii <= jj) if dr else (ii >= jj)).astype(F32)
            gc = jnp.dot(tri, g, precision=HI, preferred_element_type=F32)

            def spread(t, lane0):
                return jnp.concatenate([jnp.broadcast_to(t[:, lane0 + h:lane0 + h + 1], (tm, hd))
                                        for h in range(nh)], axis=1)

            outs += [spread(beta, 2 * nh + dr * nh), spread(gc, dr * nh), spread(gt, dr * nh)]
        return tuple(outs)
    return f


def _make_f_dnpost(hd):
    def f(o, z, ng):
        y = [_rms(oh, ng) for oh in _heads(o, hd)]
        return (jnp.concatenate(y, axis=1) * _silu(z),)
    return f


def _make_f_attpre(hd, with_v):
    def rope(x, g, cosf, sins, perm, scale):
        xn = _rms(x, g)
        xs = jnp.dot(xn, perm, precision=HI, preferred_element_type=F32)
        return (xn * cosf + xs * sins) * scale

    def f(aq, ak, *rest):
        if with_v:
            av, cosf, sins, perm, qg, kg = rest
        else:
            cosf, sins, perm, qg, kg = rest
        qh = jnp.concatenate([rope(x, qg, cosf, sins, perm, hd ** -0.5) for x in _heads(aq, hd)], axis=1)
        kh = jnp.concatenate([rope(x, kg, cosf, sins, perm, 1.0) for x in _heads(ak, hd)], axis=1)
        return (qh, kh, av) if with_v else (qh, kh)
    return f


def _f_gate(o, z):
    return (o * _silu(z),)


def _make_f_mem(nh, hd):
    def f(mq, z, kv):
        mw = nh * hd
        outs = []
        for h, q in enumerate(_heads(mq, hd)):
            k = kv[:, h * hd:(h + 1) * hd]
            v = kv[:, mw + h * hd:mw + (h + 1) * hd]
            s = lax.dot_general(q.astype(BF16), k.astype(BF16), (((1,), (1,)), ((), ())),
                                preferred_element_type=F32) * (hd ** -0.5)
            s = s - jnp.max(s, axis=-1, keepdims=True)
            p = jnp.exp(s)
            p = p / jnp.sum(p, axis=-1, keepdims=True)
            outs.append(jnp.dot(p.astype(BF16), v.astype(BF16), preferred_element_type=F32))
        return (jnp.concatenate(outs, axis=1) * _silu(z),)
    return f


def _f_merge(p0, p1, p2, p3, g0, g1, g2, g3):
    return (jax.nn.sigmoid(g0) * p0 + jax.nn.sigmoid(g1) * p1 + jax.nn.sigmoid(g2) * p2 + jax.nn.sigmoid(g3) * p3,)


def _make_f_delta(hd):
    def f(do, o):
        out = [jnp.broadcast_to(jnp.sum(a * b, axis=-1, keepdims=True), a.shape)
               for a, b in zip(_heads(do, hd), _heads(o, hd))]
        return (jnp.concatenate(out, axis=1),)
    return f


def _s5_prep(a_re, a_im, log_step, b_re, b_im, c_re, c_im, d):
    nb, gpb, sn, sp = d['NB'], d['GPB'], d['SN'], d['SP']
    step = jnp.exp(log_step)[:, None]
    mag = jnp.exp(a_re * step)
    lam_re = mag * jnp.cos(a_im * step)
    lam_im = mag * jnp.sin(a_im * step)
    den = a_re * a_re + a_im * a_im
    nr, ni = lam_re - 1.0, lam_im
    coef_re = (nr * a_re + ni * a_im) / den
    coef_im = (ni * a_re - nr * a_im) / den
    bb_re = coef_re[..., None] * b_re - coef_im[..., None] * b_im
    bb_im = coef_re[..., None] * b_im + coef_im[..., None] * b_re
    eye = jnp.eye(gpb, dtype=F32)

    def blk_in(bb):
        t = bb.reshape(nb, gpb, sn, sp)
        return jnp.einsum("jgnp,gh->jgphn", t, eye).reshape(nb, gpb * sp, gpb * sn)

    def blk_out(cc):
        t = cc.reshape(nb, gpb, sp, sn)
        return jnp.einsum("jgpn,gh->jgnhp", t, eye).reshape(nb, gpb * sn, gpb * sp)

    wb = jnp.concatenate([blk_in(bb_re), blk_in(bb_im)], axis=2)
    wc = jnp.concatenate([blk_out(c_re), blk_out(-c_im)], axis=1)
    return wb, wc, lam_re.reshape(nb, gpb * sn), lam_im.reshape(nb, gpb * sn)


def _s5_tables(lam_re, lam_im, rev, conj):
    lr, li = lam_re, (-lam_im if conj else lam_im)

    def cmul(a, b):
        return a[0] * b[0] - a[1] * b[1], a[0] * b[1] + a[1] * b[0]

    pw = [(lr, li)]
    for _ in range(7):
        pw.append(cmul(pw[-1], (lr, li)))
    bc = lambda t: jnp.broadcast_to(t[:, None, :], (t.shape[0], 8, t.shape[1]))
    order = list(range(8))[::-1] if rev else list(range(8))
    pwr = jnp.stack([pw[i][0] for i in order], axis=1)
    pwi = jnp.stack([pw[i][1] for i in order], axis=1)
    tabs = [bc(pw[0][0]), bc(pw[0][1]), bc(pw[1][0]), bc(pw[1][1]), bc(pw[3][0]), bc(pw[3][1]), pwr, pwi]
    return jnp.stack(tabs, axis=1)


def _scan_group(xr, xi, lt_ref, j, cr, ci, rev):
    row = lax.broadcasted_iota(jnp.int32, xr.shape, 0)
    for lvl, k in enumerate((1, 2, 4)):
        l_r, l_i = lt_ref[j, 2 * lvl], lt_ref[j, 2 * lvl + 1]
        sh = (8 - k) if rev else k
        keep = (row < 8 - k) if rev else (row >= k)
        sr = jnp.where(keep, pltpu.roll(xr, sh, 0), 0.0)
        si = jnp.where(keep, pltpu.roll(xi, sh, 0), 0.0)
        xr, xi = xr + l_r * sr - l_i * si, xi + l_r * si + l_i * sr
    p_r, p_i = lt_ref[j, 6], lt_ref[j, 7]
    return xr + p_r * cr - p_i * ci, xi + p_r * ci + p_i * cr


def _last_row(x, rev):
    row = lax.broadcasted_iota(jnp.int32, x.shape, 0)
    v = jnp.sum(jnp.where(row == (0 if rev else 7), x, 0.0), axis=0, keepdims=True)
    return jnp.broadcast_to(v, x.shape)


def _s5_fwd(hsrc, ucb, wb, wc, lt, *, rev, acc, tb, name, d):
    L, SW, NB, BS = hsrc.shape[0], d['SW'], d['NB'], d['BS']
    tb = _pick(tb, L)
    nblk, ngr = L // tb, tb // 8
    tix = (lambda b: nblk - 1 - b) if rev else (lambda b: b)
    has_acc = acc is not None

    def body(*refs):
        if has_acc:
            u_ref, wb_ref, wc_ref, lt_ref, acc_ref, y_ref, cin_ref, bu_s, car = refs
        else:
            u_ref, wb_ref, wc_ref, lt_ref, y_ref, cin_ref, bu_s, car = refs

        @pl.when(pl.program_id(0) == 0)
        def _():
            car[...] = jnp.zeros_like(car)

        cin_ref[...] = car[...]
        for j in range(NB):
            bu_s[:, j * 2 * BS:(j + 1) * 2 * BS] = jnp.dot(
                u_ref[:, j * LANE:(j + 1) * LANE].astype(BF16), wb_ref[j], preferred_element_type=F32)

        def grp(r, _):
            base = pl.multiple_of((ngr - 1 - r if rev else r) * 8, 8)
            for j in range(NB):
                c0 = j * 2 * BS
                xr, xi = _scan_group(bu_s[pl.ds(base, 8), c0:c0 + BS], bu_s[pl.ds(base, 8), c0 + BS:c0 + 2 * BS],
                                     lt_ref, j, car[:, c0:c0 + BS], car[:, c0 + BS:c0 + 2 * BS], rev)
                bu_s[pl.ds(base, 8), c0:c0 + BS] = xr
                bu_s[pl.ds(base, 8), c0 + BS:c0 + 2 * BS] = xi
                car[:, c0:c0 + BS] = _last_row(xr, rev)
                car[:, c0 + BS:c0 + 2 * BS] = _last_row(xi, rev)
            return 0

        lax.fori_loop(0, ngr, grp, 0)
        for j in range(NB):
            y = jnp.dot(bu_s[:, j * 2 * BS:(j + 1) * 2 * BS].astype(BF16), wc_ref[j], preferred_element_type=F32)
            if has_acc:
                y = y + acc_ref[:, j * LANE:(j + 1) * LANE]
            y_ref[:, j * LANE:(j + 1) * LANE] = y

    in_specs = [pl.BlockSpec((tb, SW), lambda b: (tix(b), ucb)),
                pl.BlockSpec(wb.shape, lambda b: (0, 0, 0)), pl.BlockSpec(wc.shape, lambda b: (0, 0, 0)),
                pl.BlockSpec(lt.shape, lambda b: (0, 0, 0, 0))]
    args = [hsrc, wb, wc, lt]
    if has_acc:
        in_specs.append(pl.BlockSpec((tb, SW), lambda b: (tix(b), 0)))
        args.append(acc)
    y, cin = pl.pallas_call(
        body, name=name, grid=(nblk,), in_specs=in_specs,
        out_specs=[pl.BlockSpec((tb, SW), lambda b: (tix(b), 0)),
                   pl.BlockSpec((8, NB * 2 * BS), lambda b: (tix(b), 0))],
        out_shape=[jax.ShapeDtypeStruct((L, SW), F32), jax.ShapeDtypeStruct((nblk * 8, NB * 2 * BS), F32)],
        scratch_shapes=[pltpu.VMEM((tb, NB * 2 * BS), F32), pltpu.VMEM((8, NB * 2 * BS), F32)],
        compiler_params=_cparams(("arbitrary",)),
    )(*args)
    return y, cin


def _s5_bwd(hsrc, ucb, dy, cin, wb, wc, lt, lt_adj, *, rev, acc, tb, name, d):
    L, SW, NB, BS = hsrc.shape[0], d['SW'], d['NB'], d['BS']
    tb = _pick(tb, L)
    nblk, ngr = L // tb, tb // 8
    arev = not rev
    tix = (lambda b: nblk - 1 - b) if arev else (lambda b: b)
    has_acc = acc is not None
    NT = (((1,), (1,)), ((), ()))
    TN = (((0,), (0,)), ((), ()))

    def body(*refs):
        if has_acc:
            (u_ref, dy_ref, cin_ref, wb_ref, wc_ref, lt_ref, la_ref, acc_ref,
             du_ref, dwb_ref, dwc_ref, dlam_ref, s_s, g_s, car, acar) = refs
        else:
            (u_ref, dy_ref, cin_ref, wb_ref, wc_ref, lt_ref, la_ref,
             du_ref, dwb_ref, dwc_ref, dlam_ref, s_s, g_s, car, acar) = refs

        @pl.when(pl.program_id(0) == 0)
        def _():
            acar[...] = jnp.zeros_like(acar)
            dwb_ref[...] = jnp.zeros_like(dwb_ref)
            dwc_ref[...] = jnp.zeros_like(dwc_ref)
            dlam_ref[...] = jnp.zeros_like(dlam_ref)

        car[...] = cin_ref[...]
        for j in range(NB):
            s_s[:, j * 2 * BS:(j + 1) * 2 * BS] = jnp.dot(
                u_ref[:, j * LANE:(j + 1) * LANE].astype(BF16), wb_ref[j], preferred_element_type=F32)
            g_s[:, j * 2 * BS:(j + 1) * 2 * BS] = lax.dot_general(
                dy_ref[:, j * LANE:(j + 1) * LANE].astype(BF16), wc_ref[j], NT, preferred_element_type=F32)

        def fgrp(r, _):
            base = pl.multiple_of((ngr - 1 - r if rev else r) * 8, 8)
            for j in range(NB):
                c0 = j * 2 * BS
                xr, xi = _scan_group(s_s[pl.ds(base, 8), c0:c0 + BS], s_s[pl.ds(base, 8), c0 + BS:c0 + 2 * BS],
                                     lt_ref, j, car[:, c0:c0 + BS], car[:, c0 + BS:c0 + 2 * BS], rev)
                s_s[pl.ds(base, 8), c0:c0 + BS] = xr
                s_s[pl.ds(base, 8), c0 + BS:c0 + 2 * BS] = xi
                car[:, c0:c0 + BS] = _last_row(xr, rev)
                car[:, c0 + BS:c0 + 2 * BS] = _last_row(xi, rev)
            return 0

        lax.fori_loop(0, ngr, fgrp, 0)

        row = lax.broadcasted_iota(jnp.int32, (8, BS), 0)

        def agrp(r, _):
            gi = ngr - 1 - r if arev else r
            base = pl.multiple_of(gi * 8, 8)
            pgi = gi + 1 if rev else gi - 1
            inside = jnp.logical_and(pgi >= 0, pgi < ngr)
            pbase = pl.multiple_of(jnp.clip(pgi, 0, ngr - 1) * 8, 8)
            for j in range(NB):
                c0 = j * 2 * BS
                ar, ai = _scan_group(g_s[pl.ds(base, 8), c0:c0 + BS], g_s[pl.ds(base, 8), c0 + BS:c0 + 2 * BS],
                                     la_ref, j, acar[:, c0:c0 + BS], acar[:, c0 + BS:c0 + 2 * BS], arev)
                g_s[pl.ds(base, 8), c0:c0 + BS] = ar
                g_s[pl.ds(base, 8), c0 + BS:c0 + 2 * BS] = ai
                acar[:, c0:c0 + BS] = _last_row(ar, arev)
                acar[:, c0 + BS:c0 + 2 * BS] = _last_row(ai, arev)
                sr, si = s_s[pl.ds(base, 8), c0:c0 + BS], s_s[pl.ds(base, 8), c0 + BS:c0 + 2 * BS]
                edge_r = jnp.where(inside, _last_row(s_s[pl.ds(pbase, 8), c0:c0 + BS], rev), cin_ref[:, c0:c0 + BS])
                edge_i = jnp.where(inside, _last_row(s_s[pl.ds(pbase, 8), c0 + BS:c0 + 2 * BS], rev),
                                   cin_ref[:, c0 + BS:c0 + 2 * BS])
                sh = 7 if rev else 1
                first = 7 if rev else 0
                pr = jnp.where(row == first, edge_r, pltpu.roll(sr, sh, 0))
                pi = jnp.where(row == first, edge_i, pltpu.roll(si, sh, 0))
                dlam_ref[:, c0:c0 + BS] += ar * pr + ai * pi
                dlam_ref[:, c0 + BS:c0 + 2 * BS] += ai * pr - ar * pi
            return 0

        lax.fori_loop(0, ngr, agrp, 0)
        for j in range(NB):
            a_j = g_s[:, j * 2 * BS:(j + 1) * 2 * BS].astype(BF16)
            u_j = u_ref[:, j * LANE:(j + 1) * LANE].astype(BF16)
            du = lax.dot_general(a_j, wb_ref[j], NT, preferred_element_type=F32)
            if has_acc:
                du = du + acc_ref[:, j * LANE:(j + 1) * LANE]
            du_ref[:, j * LANE:(j + 1) * LANE] = du
            dwb_ref[j] += lax.dot_general(u_j, a_j, TN, preferred_element_type=F32)
            dwc_ref[j] += lax.dot_general(s_s[:, j * 2 * BS:(j + 1) * 2 * BS].astype(BF16),
                                          dy_ref[:, j * LANE:(j + 1) * LANE].astype(BF16), TN,
                                          preferred_element_type=F32)

    W2 = NB * 2 * BS
    in_specs = [pl.BlockSpec((tb, SW), lambda b: (tix(b), ucb)), pl.BlockSpec((tb, SW), lambda b: (tix(b), 0)),
                pl.BlockSpec((8, W2), lambda b: (tix(b), 0)),
                pl.BlockSpec(wb.shape, lambda b: (0, 0, 0)), pl.BlockSpec(wc.shape, lambda b: (0, 0, 0)),
                pl.BlockSpec(lt.shape, lambda b: (0, 0, 0, 0)), pl.BlockSpec(lt_adj.shape, lambda b: (0, 0, 0, 0))]
    args = [hsrc, dy, cin, wb, wc, lt, lt_adj]
    if has_acc:
        in_specs.append(pl.BlockSpec((tb, SW), lambda b: (tix(b), 0)))
        args.append(acc)
    return pl.pallas_call(
        body, name=name, grid=(nblk,), in_specs=in_specs,
        out_specs=[pl.BlockSpec((tb, SW), lambda b: (tix(b), 0)),
                   pl.BlockSpec(wb.shape, lambda b: (0, 0, 0)), pl.BlockSpec(wc.shape, lambda b: (0, 0, 0)),
                   pl.BlockSpec((8, W2), lambda b: (0, 0))],
        out_shape=[jax.ShapeDtypeStruct((L, SW), F32), jax.ShapeDtypeStruct(wb.shape, F32),
                   jax.ShapeDtypeStruct(wc.shape, F32), jax.ShapeDtypeStruct((8, W2), F32)],
        scratch_shapes=[pltpu.VMEM((tb, W2), F32), pltpu.VMEM((tb, W2), F32),
                        pltpu.VMEM((8, W2), F32), pltpu.VMEM((8, W2), F32)],
        compiler_params=_cparams(("arbitrary",)),
    )(*args)


_NN = (((1,), (0,)), ((), ()))
_NT = (((1,), (1,)), ((), ()))
_TN = (((0,), (0,)), ((), ()))


def _dotb(a, b, dn=_NN):
    return lax.dot_general(a.astype(BF16), b.astype(BF16), dn, preferred_element_type=F32)


def _split(x):
    hi = x.astype(BF16)
    return hi, (x - hi.astype(F32)).astype(BF16)


def _dot3(a, b, dn=_NN):
    ah, al = _split(a)
    bh, bl = _split(b)
    f = lambda x, y: lax.dot_general(x, y, dn, preferred_element_type=F32)
    return f(ah, bh) + (f(ah, bl) + f(al, bh))


def _delta_chunk(rev, *flat):
    heads = [flat[i:i + 7] for i in range(0, len(flat), 7)]
    q, k, v, beta, gc, gt, s_in = [list(t) for t in zip(*heads)]
    c, hd = q[0].shape
    each = lambda f, *ls: [f(*t) for t in zip(*ls)]
    ii = lax.broadcasted_iota(jnp.int32, (c, c), 0)
    jj = lax.broadcasted_iota(jnp.int32, (c, c), 1)
    incl = (ii <= jj) if rev else (ii >= jj)
    strict = (ii < jj) if rev else (ii > jj)
    eye = (ii == jj).astype(F32)
    decay = each(lambda g: jnp.where(incl, jnp.exp(jnp.where(incl, g[:, :c] - jnp.transpose(g)[:c, :], 0.0)), 0.0), gc)
    kb = each(lambda a, b: a * b, k, beta)
    a = each(lambda x, y, dc: jnp.where(strict, _dot3(x, y, _NT) * dc, 0.0), kb, k, decay)
    tinv = each(lambda x: eye - x, a)
    p = a
    n = 2
    while n < c:
        p = each(lambda x: _dot3(x, x), p)
        tinv = each(lambda t, x: _dot3(t, eye + x), tinv, p)
        n *= 2
    eg = each(jnp.exp, gc)
    u = each(lambda t, x, b: _dot3(t, x * b), tinv, v, beta)
    w = each(lambda t, x, e: _dot3(t, x * e), tinv, kb, eg)
    intra = each(lambda x, y, dc: _dotb(x, y, _NT) * dc, q, k, decay)
    v_new = each(lambda x, y, s: x - _dotb(y, s), u, w, s_in)
    o = each(lambda x, e, s, m, vn: _dotb(x * e, s) + _dotb(m, vn), q, eg, s_in, intra, v_new)
    s_out = each(lambda s, t, x, g, vn: s * jnp.exp(jnp.broadcast_to(t[0:1, :], (hd, hd)))
                 + _dotb(x * jnp.exp(t - g), vn, _TN), s_in, gt, k, gc, v_new)
    return tuple(x for pair in zip(o, s_out) for x in pair)


def _delta_fwd(q, k, v, gates, *, vcb, rev, acc, name, d):
    L, H, hd, C = q.shape[0], d['DNH'], d['DNK'], d['CHUNK']
    nc = L // C
    cix = (lambda i: nc - 1 - i) if rev else (lambda i: i)
    has_acc = acc is not None

    def body(*refs):
        if has_acc:
            q_ref, k_ref, v_ref, b_ref, gc_ref, gt_ref, acc_ref, o_ref, ss_ref, st = refs
        else:
            q_ref, k_ref, v_ref, b_ref, gc_ref, gt_ref, o_ref, ss_ref, st = refs

        @pl.when(pl.program_id(0) == 0)
        def _():
            st[...] = jnp.zeros_like(st)

        sls = [slice(h * hd, (h + 1) * hd) for h in range(H)]
        ins = [(q_ref[:, sl], k_ref[:, sl], v_ref[:, sl], b_ref[:, sl], gc_ref[:, sl], gt_ref[:, sl], st[h])
               for h, sl in enumerate(sls)]
        accv = [acc_ref[:, sl] for sl in sls] if has_acc else None
        res = _delta_chunk(rev, *[t for head in ins for t in head])
        for h, sl in enumerate(sls):
            o, s_out = res[2 * h], res[2 * h + 1]
            ss_ref[0, h] = ins[h][6]
            o_ref[:, sl] = o + accv[h] if has_acc else o
            st[h] = s_out

    blk = pl.BlockSpec((C, H * hd), lambda i: (cix(i), 0))
    in_specs = [blk, blk, pl.BlockSpec((C, H * hd), lambda i: (cix(i), vcb)), blk, blk, blk]
    args = [q, k, v, *gates]
    if has_acc:
        in_specs.append(blk)
        args.append(acc)
    return pl.pallas_call(
        body, name=name, grid=(nc,), in_specs=in_specs,
        out_specs=[blk, pl.BlockSpec((1, H, hd, hd), lambda i: (cix(i), 0, 0, 0))],
        out_shape=[jax.ShapeDtypeStruct((L, H * hd), F32), jax.ShapeDtypeStruct((nc, H, hd, hd), F32)],
        scratch_shapes=[pltpu.VMEM((H, hd, hd), F32)],
        compiler_params=_cparams(("arbitrary",)),
    )(*args)


def _delta_bwd(q, k, v, gates, ssave, do, *, vcb, rev, accs, name, d):
    L, H, hd, C = q.shape[0], d['DNH'], d['DNK'], d['CHUNK']
    nc = L // C
    cix = (lambda i: i) if rev else (lambda i: nc - 1 - i)
    has_acc = accs is not None

    def body(*refs):
        if has_acc:
            (q_ref, k_ref, v_ref, b_ref, gc_ref, gt_ref, ss_ref, do_ref, aq_ref, ak_ref, av_ref,
             dq_ref, dk_ref, dv_ref, db_ref, dgc_ref, dgt_ref, dst) = refs
        else:
            (q_ref, k_ref, v_ref, b_ref, gc_ref, gt_ref, ss_ref, do_ref,
             dq_ref, dk_ref, dv_ref, db_ref, dgc_ref, dgt_ref, dst) = refs

        @pl.when(pl.program_id(0) == 0)
        def _():
            dst[...] = jnp.zeros_like(dst)

        sls = [slice(h * hd, (h + 1) * hd) for h in range(H)]
        ins = [(q_ref[:, sl], k_ref[:, sl], v_ref[:, sl], b_ref[:, sl], gc_ref[:, sl], gt_ref[:, sl], ss_ref[0, h])
               for h, sl in enumerate(sls)]
        cts = tuple(t for h, sl in enumerate(sls) for t in (do_ref[:, sl], dst[h]))
        accv = [(aq_ref[:, sl], ak_ref[:, sl], av_ref[:, sl]) for sl in sls] if has_acc else None
        _, vjp = jax.vjp(functools.partial(_delta_chunk, rev), *[t for head in ins for t in head])
        res = vjp(cts)
        for h, sl in enumerate(sls):
            dq, dk, dv, db, dgc, dgt, ds = res[7 * h:7 * h + 7]
            dst[h] = ds
            if has_acc:
                dq, dk, dv = dq + accv[h][0], dk + accv[h][1], dv + accv[h][2]
            dq_ref[:, sl] = dq
            dk_ref[:, sl] = dk
            dv_ref[:, sl] = dv
            db_ref[:, sl] = db
            dgc_ref[:, sl] = dgc
            dgt_ref[:, sl] = dgt

    blk = pl.BlockSpec((C, H * hd), lambda i: (cix(i), 0))
    in_specs = [blk, blk, pl.BlockSpec((C, H * hd), lambda i: (cix(i), vcb)), blk, blk, blk,
                pl.BlockSpec((1, H, hd, hd), lambda i: (cix(i), 0, 0, 0)), blk]
    args = [q, k, v, *gates, ssave, do]
    if has_acc:
        in_specs += [blk, blk, blk]
        args += list(accs)
    return pl.pallas_call(
        body, name=name, grid=(nc,), in_specs=in_specs, out_specs=[blk] * 6,
        out_shape=[jax.ShapeDtypeStruct((L, H * hd), F32)] * 6,
        scratch_shapes=[pltpu.VMEM((H, hd, hd), F32)],
        compiler_params=_cparams(("arbitrary",)),
    )(*args)


def _conv_specs(tm, w, cb0, nrb, L):
    hb = tm // 8
    last8 = L // 8 - 1
    cur = pl.BlockSpec((tm, w), lambda s, i: (i, cb0 + s))
    prev = pl.BlockSpec((8, w), lambda s, i: (jnp.maximum(i * hb - 1, 0), cb0 + s))
    nxt = pl.BlockSpec((8, w), lambda s, i: (jnp.minimum((i + 1) * hb, last8), cb0 + s))
    return [prev, cur, nxt]


def _fill_halo(dst, prev_ref, cur_ref, next_ref, i, nrb, tm):
    dst[pl.ds(0, 8), :] = jnp.where(i > 0, prev_ref[...], 0.0)
    dst[pl.ds(8, tm), :] = cur_ref[...]
    dst[pl.ds(8 + tm, 8), :] = jnp.where(i < nrb - 1, next_ref[...], 0.0)


def _conv_fwd(hsrc, cb0, wt, *, tm, name, d):
    L, w, K = hsrc.shape[0], d['DW'], d['CONV']
    tm = _pick(tm, L)
    nrb = L // tm

    def body(prev_ref, cur_ref, next_ref, w_ref, o_ref, xs):
        i = pl.program_id(1)
        _fill_halo(xs, prev_ref, cur_ref, next_ref, i, nrb, tm)
        y = jnp.zeros((tm, w), F32)
        for kk in range(K):
            y = y + w_ref[0, pl.ds(kk, 1), :] * xs[pl.ds(8 - K // 2 + kk, tm), :]
        o_ref[...] = _silu(y)

    return pl.pallas_call(
        body, name=name, grid=(3, nrb),
        in_specs=_conv_specs(tm, w, cb0, nrb, L) + [pl.BlockSpec((1, 8, w), lambda s, i: (s, 0, 0))],
        out_specs=pl.BlockSpec((tm, w), lambda s, i: (i, s)),
        out_shape=jax.ShapeDtypeStruct((L, 3 * w), F32),
        scratch_shapes=[pltpu.VMEM((tm + 16, w), F32)],
        compiler_params=_cparams(("parallel", "parallel")),
    )(hsrc, hsrc, hsrc, wt)


def _conv_bwd(hsrc, cb0, wt, dact, *, tm, name, d):
    L, w, K = hsrc.shape[0], d['DW'], d['CONV']
    tm = _pick(tm, L)
    nrb = L // tm
    half = K // 2

    def body(xp_ref, xc_ref, xn_ref, gp_ref, gc_ref, gn_ref, w_ref, dx_ref, dw_ref, xs, gs, dys):
        i = pl.program_id(1)
        _fill_halo(xs, xp_ref, xc_ref, xn_ref, i, nrb, tm)
        _fill_halo(gs, gp_ref, gc_ref, gn_ref, i, nrb, tm)
        y = jnp.zeros((tm + 8, w), F32)
        for kk in range(K):
            y = y + w_ref[0, pl.ds(kk, 1), :] * xs[pl.ds(4 - half + kk, tm + 8), :]
        sg = jax.nn.sigmoid(y)
        dys[...] = gs[pl.ds(4, tm + 8), :] * (sg * (1.0 + y * (1.0 - sg)))
        dx = jnp.zeros((tm, w), F32)
        for kk in range(K):
            dx = dx + w_ref[0, pl.ds(kk, 1), :] * dys[pl.ds(4 + half - kk, tm), :]
        dx_ref[...] = dx

        @pl.when(i == 0)
        def _():
            dw_ref[...] = jnp.zeros_like(dw_ref)

        dy = dys[pl.ds(4, tm), :]
        for kk in range(K):
            dw_ref[0, pl.ds(kk, 1), :] += jnp.sum(dy * xs[pl.ds(8 - half + kk, tm), :], axis=0, keepdims=True)

    gspecs = _conv_specs(tm, w, 0, nrb, L)
    return pl.pallas_call(
        body, name=name, grid=(3, nrb),
        in_specs=_conv_specs(tm, w, cb0, nrb, L) + gspecs + [pl.BlockSpec((1, 8, w), lambda s, i: (s, 0, 0))],
        out_specs=[pl.BlockSpec((tm, w), lambda s, i: (i, s)), pl.BlockSpec((1, 8, w), lambda s, i: (s, 0, 0))],
        out_shape=[jax.ShapeDtypeStruct((L, 3 * w), F32), jax.ShapeDtypeStruct((3, 8, w), F32)],
        scratch_shapes=[pltpu.VMEM((tm + 16, w), F32), pltpu.VMEM((tm + 16, w), F32), pltpu.VMEM((tm + 8, w), F32)],
        compiler_params=_cparams(("parallel", "arbitrary")),
    )(hsrc, hsrc, hsrc, dact, dact, dact, wt)


def _wide(v, n):
    return v if n == LANE else jnp.tile(v, (1, n // LANE))


def _attn_fwd(qh, kh, vh, *, tq, tk, name, d):
    L, H, KVH, hd = qh.shape[0], d['AH'], d['AKV'], d['AD']
    grp = H // KVH
    tq, tk = _pick(tq, L), _pick(tk, L)
    nk = L // tk

    def body(q_ref, k_ref, v_ref, o_ref, lse_ref, m_s, l_s, acc):
        j = pl.program_id(2)

        @pl.when(j == 0)
        def _():
            m_s[...] = jnp.full_like(m_s, -1e30)
            l_s[...] = jnp.zeros_like(l_s)
            acc[...] = jnp.zeros_like(acc)

        s = lax.dot_general(q_ref[...], k_ref[...], _NT, preferred_element_type=F32)
        m_old = m_s[...]
        m_new = jnp.maximum(m_old, jnp.max(s, axis=-1, keepdims=True))
        alpha = jnp.exp(m_old - m_new)
        p = jnp.exp(s - _wide(m_new, tk))
        l_s[...] = alpha * l_s[...] + jnp.sum(p, axis=-1, keepdims=True)
        acc[...] = alpha * acc[...] + jnp.dot(p.astype(BF16), v_ref[...], preferred_element_type=F32)
        m_s[...] = m_new

        @pl.when(j == nk - 1)
        def _():
            o_ref[...] = acc[...] / l_s[...]
            lse_ref[...] = m_s[...] + jnp.log(l_s[...])

    qspec = pl.BlockSpec((tq, hd), lambda h, i, j: (i, h))
    kspec = pl.BlockSpec((tk, hd), lambda h, i, j: (j, h // grp))
    return pl.pallas_call(
        body, name=name, grid=(H, L // tq, nk), in_specs=[qspec, kspec, kspec], out_specs=[qspec, qspec],
        out_shape=[jax.ShapeDtypeStruct((L, H * hd), F32), jax.ShapeDtypeStruct((L, H * hd), F32)],
        scratch_shapes=[pltpu.VMEM((tq, hd), F32), pltpu.VMEM((tq, hd), F32), pltpu.VMEM((tq, hd), F32)],
        compiler_params=_cparams(("parallel", "parallel", "arbitrary")),
    )(qh, kh, vh)


def _attn_bwd(qh, kh, vh, do, lse, delta, *, tq, tk, name, d):
    L, H, KVH, hd = qh.shape[0], d['AH'], d['AKV'], d['AD']
    grp = H // KVH
    tq, tk = _pick(tq, L), _pick(tk, L)
    nk = L // tk

    def body(q_ref, k_ref, v_ref, do_ref, lse_ref, dl_ref, dq_ref, dk_ref, dv_ref, dq_s):
        g, i, j = pl.program_id(1), pl.program_id(2), pl.program_id(3)

        @pl.when(jnp.logical_and(jnp.logical_and(g == 0, i == 0), j == 0))
        def _():
            dk_ref[...] = jnp.zeros_like(dk_ref)
            dv_ref[...] = jnp.zeros_like(dv_ref)

        @pl.when(j == 0)
        def _():
            dq_s[...] = jnp.zeros_like(dq_s)

        q, k, do_ = q_ref[...], k_ref[...], do_ref[...].astype(BF16)
        s = lax.dot_general(q, k, _NT, preferred_element_type=F32)
        p = jnp.exp(s - _wide(lse_ref[...], tk))
        dp = lax.dot_general(do_, v_ref[...], _NT, preferred_element_type=F32)
        ds = (p * (dp - _wide(dl_ref[...], tk))).astype(BF16)
        dq_s[...] += jnp.dot(ds, k, preferred_element_type=F32)
        rows = pl.ds(pl.multiple_of(j * tk, tk), tk)
        dv_ref[rows, :] += lax.dot_general(p.astype(BF16), do_, _TN, preferred_element_type=F32)
        dk_ref[rows, :] += lax.dot_general(ds, q, _TN, preferred_element_type=F32)

        @pl.when(j == nk - 1)
        def _():
            dq_ref[...] = dq_s[...]

    qspec = pl.BlockSpec((tq, hd), lambda kv, g, i, j: (i, kv * grp + g))
    kspec = pl.BlockSpec((tk, hd), lambda kv, g, i, j: (j, kv))
    colspec = pl.BlockSpec((L, hd), lambda kv, g, i, j: (0, kv))
    return pl.pallas_call(
        body, name=name, grid=(KVH, grp, L // tq, nk),
        in_specs=[qspec, kspec, kspec, qspec, qspec, qspec], out_specs=[qspec, colspec, colspec],
        out_shape=[jax.ShapeDtypeStruct((L, H * hd), F32)] + [jax.ShapeDtypeStruct((L, KVH * hd), F32)] * 2,
        scratch_shapes=[pltpu.VMEM((tq, hd), F32)],
        compiler_params=_cparams(("parallel", "arbitrary", "arbitrary", "arbitrary")),
    )(qh, kh, vh, do, lse, delta)


def _loss_grad(x, g, tgt, *, tm, name):
    L, D = x.shape
    tm = _pick(tm, L)

    def body(x_ref, g_ref, t_ref, loss_ref, dx_ref, dg_ref):
        def f(xv, gv):
            err = _rms(xv, gv) - t_ref[...]
            return 0.5 * jnp.sum(jnp.mean(err * err, axis=-1, keepdims=True))

        val, vjp = jax.vjp(f, x_ref[...], g_ref[...])
        dx, dg = vjp(jnp.ones((), F32))
        dx_ref[...] = dx

        @pl.when(pl.program_id(0) == 0)
        def _():
            loss_ref[...] = jnp.zeros_like(loss_ref)
            dg_ref[...] = jnp.zeros_like(dg_ref)

        loss_ref[...] += val
        dg_ref[...] += dg

    return pl.pallas_call(
        body, name=name, grid=(L // tm,),
        in_specs=[pl.BlockSpec((tm, D), lambda i: (i, 0)), pl.BlockSpec((1, D), lambda i: (0, 0)),
                  pl.BlockSpec((tm, D), lambda i: (i, 0))],
        out_specs=[pl.BlockSpec((8, LANE), lambda i: (0, 0)), pl.BlockSpec((tm, D), lambda i: (i, 0)),
                   pl.BlockSpec((1, D), lambda i: (0, 0))],
        out_shape=[jax.ShapeDtypeStruct((8, LANE), F32), jax.ShapeDtypeStruct((L, D), F32),
                   jax.ShapeDtypeStruct((1, D), F32)],
        compiler_params=_cparams(("arbitrary",)),
    )(x, g, tgt)


def _sum_slots(recv, *, tr, name):
    n, R, W = recv.shape
    tr = _pick(tr, R)

    def body(r_ref, o_ref):
        s = r_ref[0].astype(F32)
        for i in range(1, n):
            s = s + r_ref[i].astype(F32)
        o_ref[...] = s

    return pl.pallas_call(
        body, name=name, grid=(R // tr,),
        in_specs=[pl.BlockSpec((n, tr, W), lambda i: (0, i, 0))], out_specs=pl.BlockSpec((tr, W), lambda i: (i, 0)),
        out_shape=jax.ShapeDtypeStruct((R, W), F32), compiler_params=_cparams(("parallel",)),
    )(recv)


def _adamw(w, g, m, v, *, tr, name):
    R, W = w.shape
    tr = _pick(tr, R)
    c1 = 1.0 - ADAM_B1 ** ADAM_STEP
    c2 = 1.0 - ADAM_B2 ** ADAM_STEP

    def body(w_ref, g_ref, m_ref, v_ref, d_ref, nm_ref, nv_ref):
        gv = g_ref[...]
        nm = ADAM_B1 * m_ref[...] + (1.0 - ADAM_B1) * gv
        nv = ADAM_B2 * v_ref[...] + (1.0 - ADAM_B2) * (gv * gv)
        d_ref[...] = -ADAM_LR * ((nm / c1) / (jnp.sqrt(nv / c2) + ADAM_EPS) + ADAM_WD * w_ref[...])
        nm_ref[...] = nm
        nv_ref[...] = nv

    spec = pl.BlockSpec((tr, W), lambda i: (i, 0))
    return pl.pallas_call(
        body, name=name, grid=(R // tr,), in_specs=[spec] * 4, out_specs=[spec] * 3,
        out_shape=[jax.ShapeDtypeStruct((R, W), F32)] * 3, compiler_params=_cparams(("parallel",)),
    )(w, g, m, v)


_MESH = pl.DeviceIdType.MESH


def _all_gather(xs, *, name):
    na = len(xs)

    def body(*refs):
        x_refs, out_refs = refs[:na], refs[na:2 * na]
        send_sems, recv_sems, local_sems = refs[2 * na:]
        x, y, c = lax.axis_index("x"), lax.axis_index("y"), lax.axis_index("c")
        me, sibling = (x, y, c), (x, y, 1 - c)
        chips = [(1 - x, y), (x, 1 - y), (1 - x, 1 - y)]

        def slot(a, px, py, pc):
            return out_refs[a].at[4 * px + 2 * py + pc]

        def copy(a, k, block, to, src=None):
            return pltpu.make_async_remote_copy(
                src_ref=slot(a, *block) if src is None else src, dst_ref=slot(a, *block),
                send_sem=send_sems.at[7 * a + k], recv_sem=recv_sems.at[7 * a + k], device_id=to,
                device_id_type=_MESH)

        mine = [pltpu.make_async_copy(x_refs[a], slot(a, *me), local_sems.at[a]) for a in range(na)]
        for cp in mine:
            cp.start()
        first = []
        for a in range(na):
            first.append(copy(a, 0, me, sibling, src=x_refs[a]))
            first += [copy(a, 1 + j, me, (*chip, c), src=x_refs[a]) for j, chip in enumerate(chips)]
        for cp in first:
            cp.start()
        passed = []
        for a in range(na):
            for j, chip in enumerate(chips):
                copy(a, 1 + j, (*chip, c), me).wait_recv()
                passed.append(copy(a, 4 + j, (*chip, c), sibling))
                passed[-1].start()
        for a in range(na):
            copy(a, 0, sibling, me).wait_recv()
            for j, chip in enumerate(chips):
                copy(a, 4 + j, (*chip, 1 - c), me).wait_recv()
        for cp in first + passed:
            cp.wait_send()
        for cp in mine:
            cp.wait()

    return pl.pallas_call(
        body, name=name,
        out_shape=[jax.ShapeDtypeStruct((N_DEV,) + t.shape, t.dtype) for t in xs],
        in_specs=[pl.BlockSpec(memory_space=pl.ANY)] * na, out_specs=[pl.BlockSpec(memory_space=pl.ANY)] * na,
        scratch_shapes=[pltpu.SemaphoreType.DMA((7 * na,)), pltpu.SemaphoreType.DMA((7 * na,)),
                        pltpu.SemaphoreType.DMA((na,))],
    )(*xs)


def _all_to_all(gs, *, name):
    na, n = len(gs), N_DEV

    def body(*refs):
        g_refs, out_refs = refs[:na], refs[na:2 * na]
        send_sems, recv_sems, local_sems = refs[2 * na:]
        x, y, c = lax.axis_index("x"), lax.axis_index("y"), lax.axis_index("c")
        me = 4 * x + 2 * y + c

        def peer(mask):
            return (x ^ (mask >> 2), y ^ ((mask >> 1) & 1), c ^ (mask & 1))

        def copy(a, mask):
            px, py, pc = peer(mask)
            return pltpu.make_async_remote_copy(
                src_ref=g_refs[a].at[4 * px + 2 * py + pc], dst_ref=out_refs[a].at[me],
                send_sem=send_sems.at[7 * a + mask - 1], recv_sem=recv_sems.at[7 * a + mask - 1],
                device_id=(px, py, pc), device_id_type=_MESH)

        def arrival(a, mask):
            px, py, pc = peer(mask)
            return pltpu.make_async_remote_copy(
                src_ref=g_refs[a].at[me], dst_ref=out_refs[a].at[4 * px + 2 * py + pc],
                send_sem=send_sems.at[7 * a + mask - 1], recv_sem=recv_sems.at[7 * a + mask - 1],
                device_id=(px, py, pc), device_id_type=_MESH)

        mine = [pltpu.make_async_copy(g_refs[a].at[me], out_refs[a].at[me], local_sems.at[a]) for a in range(na)]
        for cp in mine:
            cp.start()
        sends = [copy(a, mask) for a in range(na) for mask in range(1, n)]
        for cp in sends:
            cp.start()
        for a in range(na):
            for mask in range(1, n):
                arrival(a, mask).wait_recv()
        for cp in sends:
            cp.wait_send()
        for cp in mine:
            cp.wait()

    return pl.pallas_call(
        body, name=name,
        out_shape=[jax.ShapeDtypeStruct(t.shape, t.dtype) for t in gs],
        in_specs=[pl.BlockSpec(memory_space=pl.ANY)] * na, out_specs=[pl.BlockSpec(memory_space=pl.ANY)] * na,
        scratch_shapes=[pltpu.SemaphoreType.DMA((7 * na,)), pltpu.SemaphoreType.DMA((7 * na,)),
                        pltpu.SemaphoreType.DMA((na,))],
    )(*gs)


def _rows_of(shape):
    return -(-int(np.prod(shape)) // PACK_W)


def _pack(arrs, dtype, lead=0, total_rows=None):
    pieces = []
    for a in arrs:
        f = a.reshape(a.shape[:lead] + (-1,)).astype(dtype)
        pad = (-f.shape[-1]) % PACK_W
        if pad:
            f = jnp.pad(f, [(0, 0)] * lead + [(0, pad)])
        pieces.append(f.reshape(a.shape[:lead] + (-1, PACK_W)))
    buf = jnp.concatenate(pieces, axis=lead)
    if total_rows is not None and buf.shape[lead] < total_rows:
        buf = jnp.pad(buf, [(0, 0)] * lead + [(0, total_rows - buf.shape[lead]), (0, 0)])
    return buf


def _unpack(buf, shapes, lead=0):
    out, r = [], 0
    for shp in shapes:
        n, rows = int(np.prod(shp)), _rows_of(shp)
        piece = buf[(slice(None),) * lead + (slice(r, r + rows),)]
        piece = piece.reshape(buf.shape[:lead] + (-1,))[..., :n]
        out.append(piece.reshape(buf.shape[:lead] + tuple(shp)))
        r += rows
    return out


def _to_full(parts):
    dep, r = parts.shape[1:3]
    return jnp.transpose(parts, (1, 0) + tuple(range(2, parts.ndim))).reshape((dep, N_DEV * r) + parts.shape[3:])


def _to_slabs(full):
    dep, r = full.shape[:2]
    t = full.reshape((dep, N_DEV, r // N_DEV) + full.shape[2:])
    return jnp.transpose(t, (1, 0) + tuple(range(2, t.ndim)))


def _ref_cols(parts, ro, wd):
    w, out = parts.shape[2], []
    for dev in range(N_DEV):
        lo, hi = max(ro, dev * w), min(ro + wd, (dev + 1) * w)
        if lo < hi:
            out.append(parts[dev][:, lo - dev * w:hi - dev * w])
    return out


def _w_in_to_layout(parts, seg, rseg, nh2):
    D = parts.shape[1]
    cols, off = [], 0
    names = sorted([k for k in seg if not k.startswith('_')], key=lambda k: seg[k][0])
    for nm in names:
        o, wd = seg[nm]
        if o > off:
            cols.append(jnp.zeros((D, o - off), parts.dtype))
        if nm == 'dadb':
            cols += _ref_cols(parts, rseg['da'][0], nh2) + _ref_cols(parts, rseg['db'][0], nh2)
            cols.append(jnp.zeros((D, wd - 2 * nh2), parts.dtype))
        else:
            cols += _ref_cols(parts, rseg[nm][0], wd)
        off = o + wd
    if seg['_total'] > off:
        cols.append(jnp.zeros((D, seg['_total'] - off), parts.dtype))
    return jnp.concatenate(cols, axis=1)


def _w_in_slabs(dw, seg, rseg, nh2):
    w = rseg['_total'] // N_DEV
    ref = []
    for nm in sorted([k for k in rseg if not k.startswith('_')], key=lambda k: rseg[k][0]):
        lo = {'da': seg['dadb'][0], 'db': seg['dadb'][0] + nh2}.get(nm)
        ref.append((rseg[nm][0], rseg[nm][1], seg[nm][0] if lo is None else lo))
    slabs = []
    for dev in range(N_DEV):
        cols = []
        for ro, wd, lo in ref:
            a, b = max(ro, dev * w), min(ro + wd, (dev + 1) * w)
            if a < b:
                cols.append(dw[:, lo + a - ro:lo + b - ro])
        slabs.append(jnp.concatenate(cols, axis=1))
    return jnp.stack(slabs, axis=0)


def _assemble_dh(pieces, seg, L):
    cols, off = [], 0
    for nm in sorted(pieces, key=lambda k: seg[k][0]):
        o = seg[nm][0]
        if o > off:
            cols.append(jnp.zeros((L, o - off), F32))
        cols.append(pieces[nm])
        off = o + pieces[nm].shape[1]
    if seg['_total'] > off:
        cols.append(jnp.zeros((L, seg['_total'] - off), F32))
    return jnp.concatenate(cols, axis=1)


def _lane_pad(v):
    v = v.reshape(1, -1)
    return jnp.pad(v, ((0, 0), (0, LANE - v.shape[1])))


def _rope_tables(L, c):
    rows = L // c['GRID_W']
    row = jnp.repeat(jnp.arange(rows), c['GRID_W']).astype(F32)
    col = jnp.tile(jnp.arange(c['GRID_W']), rows).astype(F32)
    axis_dim = c['AD'] // 2
    freqs = c['ROPE_THETA'] ** (-jnp.arange(0, axis_dim, 2, dtype=F32) / axis_dim)
    ang = jnp.concatenate([row[:, None] * freqs, col[:, None] * freqs], axis=-1)
    cosf = jnp.repeat(jnp.cos(ang), 2, axis=1)
    sn = jnp.sin(ang)
    sins = jnp.stack([-sn, sn], axis=-1).reshape(L, c['AD'])
    idx = np.arange(c['AD'])
    perm = np.zeros((c['AD'], c['AD']), np.float32)
    perm[idx, idx ^ 1] = 1.0
    return cosf, sins, jnp.asarray(perm)


def _s5_dir_params(a, l, dr):
    return (a['ssm_a_re'][l, dr], a['ssm_a_im'][l, dr], a['ssm_log_step'][l, dr], a['ssm_b_re'][l, dr],
            a['ssm_b_im'][l, dr], a['ssm_c_re'][l, dr], a['ssm_c_im'][l, dr])


def _layer_fwd(x, mem, l, wt, a, rope, c, d, seg):
    L, D = x.shape
    SW, DW, AW, AKW, MW, H = d['SW'], d['DW'], d['AW'], d['AKW'], d['MW'], d['DNH']
    cb = lambda nm: seg[nm][0] // seg[nm][1]
    sv = {'x': x}
    p = f"l{l}_"
    sv['g_norm'] = a['norm_g'][l][None, :]
    xn, = _rowwise(_f_norm, [(x, D, 0)], [sv['g_norm']], [(D, BF16)], tm=256, name=p + "norm")
    h = _mm(xn, wt['wp'], name=p + "in_proj", tm=1024, tn=1536, tk=1024)
    sv['xn'], sv['h'] = xn, h

    ysum, sv['s5'] = None, []
    for dr in range(2):
        wb, wc, lr, li = _s5_prep(*_s5_dir_params(a, l, dr), d)
        wb16, wc16 = wb.astype(BF16), wc.astype(BF16)
        lt = _s5_tables(lr, li, bool(dr), False)
        ysum, cin = _s5_fwd(h, cb('u_a'), wb16, wc16, lt, rev=bool(dr), acc=ysum, tb=256, name=p + f"s5_fwd{dr}", d=d)
        sv['s5'].append((wb16, wc16, lt, _s5_tables(lr, li, not bool(dr), True), cin))
    sv['ysum'] = ysum
    sv['s5_par'] = [a['ssm_d'][l][None, :], wt['w_glu'], a['ssm_b_glu'][l][None, :]]
    sv['s5_rows'] = [(ysum, SW, 0), (h, SW, cb('u_a')), (h, SW, cb('z_a'))]
    y_a, = _rowwise(_f_s5tail, sv['s5_rows'], sv['s5_par'], [(SW, F32)], tm=256, name=p + "s5_tail")

    act = _conv_fwd(h, cb('dq'), wt['conv'], tm=256, name=p + "dn_conv", d=d)
    sv['act'] = act
    sv['dn_par'] = [_lane_pad(a['dn_a_log'][l]), _lane_pad(a['dn_dt_bias'][l])]
    sv['dn_rows'] = [(act, DW, 0), (act, DW, 1), (h, LANE, seg['dadb'][0] // LANE)]
    dn_out = _rowwise(_make_f_dnpre(H, d['DNK'], c['CHUNK']), sv['dn_rows'], sv['dn_par'], [(DW, F32)] * 8,
                      tm=256, name=p + "dn_pre")
    qn, kn = dn_out[:2]
    sv['qn'], sv['kn'], sv['gates'] = qn, kn, [dn_out[2:5], dn_out[5:8]]
    o_dn, sv['dn_state'] = None, []
    for dr in range(2):
        o_dn, ss = _delta_fwd(qn, kn, act, sv['gates'][dr], vcb=2, rev=bool(dr), acc=o_dn,
                              name=p + f"dn_fwd{dr}", d=d)
        sv['dn_state'].append(ss)
    sv['dnpost_rows'] = [(o_dn, DW, 0), (h, DW, cb('z_b'))]
    sv['dnpost_par'] = [a['dn_norm_g'][l][None, :]]
    y_b, = _rowwise(_make_f_dnpost(d['DNK']), sv['dnpost_rows'], sv['dnpost_par'], [(DW, F32)], tm=256,
                    name=p + "dn_post")

    cosf, sins, perm = rope
    sv['att_par'] = [perm, a['attn_q_norm'][l][None, :], a['attn_k_norm'][l][None, :]]
    qh, kh, vh = _rowwise(_make_f_attpre(d['AD'], True),
                          [(h, AW, cb('aq')), (h, AKW, cb('ak')), (h, AKW, cb('av')), (cosf, d['AD'], 0),
                           (sins, d['AD'], 0)], sv['att_par'], [(AW, BF16), (AKW, BF16), (AKW, BF16)],
                          tm=256, name=p + "att_pre")
    o_att, lse = _attn_fwd(qh, kh, vh, tq=TILES['att_q'], tk=TILES['att_k'], name=p + "att_fwd", d=d)
    sv['qh'], sv['kh'], sv['vh'], sv['o_att'], sv['lse'] = qh, kh, vh, o_att, lse
    y_c, = _rowwise(_f_gate, [(o_att, AW, 0), (h, AW, cb('z_c'))], [], [(AW, F32)], tm=256, name=p + "att_post")

    sv['g_mem'] = a['mem_norm_g'][l][None, :]
    memn, = _rowwise(_f_norm, [(mem, D, 0)], [sv['g_mem']], [(D, BF16)], tm=256, name=p + "mem_norm")
    kv = _mm(memn, wt['w_mem_kv'], name=p + "mem_kv")
    sv['memn'], sv['kv'] = memn, kv
    y_m, = _rowwise(_make_f_mem(d['MH'], d['MD']), [(h, MW, cb('mq')), (h, MW, cb('z_m'))], [kv], [(MW, F32)],
                    tm=256, name=p + "mem_attn")

    ys = [y_a, y_b, y_c, y_m]
    ps = [_mm(y, wb_, name=p + f"branch_proj{i}") for i, (y, wb_) in enumerate(zip(ys, wt['w_branch']))]
    gcb = seg['gates'][0] // D
    sv['merge_rows'] = [(pp, D, 0) for pp in ps] + [(h, D, gcb + i) for i in range(4)]
    merged, = _rowwise(_f_merge, sv['merge_rows'], [], [(D, BF16)], tm=128, name=p + "merge")
    sv['ys'], sv['merged'] = ys, merged
    return _mm(merged, wt['w_out'], add=x, name=p + "out_proj"), sv


def _layer_bwd(dx, mem, l, wt, a, rope, sv, c, d, seg):
    L, D = dx.shape
    SW, DW, AW, AKW, MW, H = d['SW'], d['DW'], d['AW'], d['AKW'], d['MW'], d['DNH']
    cb = lambda nm: seg[nm][0] // seg[nm][1]
    p = f"l{l}_"
    h = sv['h']
    gr = {}
    dmerged = _mm(dx, wt['w_out'], tb=True, name=p + "d_merged")
    gr['w_out'] = _mm(sv['merged'], dx, ta=True, name=p + "dw_out")
    dmr, _ = _rowwise_bwd(_f_merge, sv['merge_rows'], [], [[(dmerged, D, 0)]], [True] * 8, [], tm=128,
                          name=p + "merge_bwd")
    dps, dgates = dmr[:4], dmr[4:]
    dys = [_mm(dp, wb_, tb=True, name=p + f"d_branch{i}") for i, (dp, wb_) in enumerate(zip(dps, wt['w_branch']))]
    gr['w_branch'] = jnp.concatenate(
        [_mm(y, dp, ta=True, name=p + f"dw_branch{i}") for i, (y, dp) in enumerate(zip(sv['ys'], dps))], axis=0)

    (dmq, dzm), (dkv,) = _rowwise_bwd(_make_f_mem(d['MH'], d['MD']), [(h, MW, cb('mq')), (h, MW, cb('z_m'))],
                                      [sv['kv']], [[(dys[3], MW, 0)]], [True, True], [True], tm=256,
                                      name=p + "mem_attn_bwd")
    gr['w_mem_kv'] = _mm(sv['memn'], dkv, ta=True, name=p + "dw_mem_kv")
    dmemn = _mm(dkv, wt['w_mem_kv'], tb=True, name=p + "d_memn")
    _, (dg_mem,) = _rowwise_bwd(_f_norm, [(mem, D, 0)], [sv['g_mem']], [[(dmemn, D, 0)]], [False], [True], tm=256,
                                name=p + "mem_norm_bwd")
    gr['mem_norm_g'] = dg_mem[0]

    (do_att, dzc), _ = _rowwise_bwd(_f_gate, [(sv['o_att'], AW, 0), (h, AW, cb('z_c'))], [], [[(dys[2], AW, 0)]],
                                    [True, True], [], tm=256, name=p + "att_post_bwd")
    delta, = _rowwise(_make_f_delta(d['AD']), [(do_att, AW, 0), (sv['o_att'], AW, 0)], [], [(AW, F32)], tm=256,
                      name=p + "att_delta")
    att_in = (sv['qh'], sv['kh'], sv['vh'], do_att, sv['lse'], delta)
    dqh, dkh, dvh = _attn_bwd(*att_in, tq=TILES['att_q'], tk=TILES['att_k'], name=p + "att_bwd", d=d)
    cosf, sins, _ = rope
    (daq, dak), (dqg, dkg) = _rowwise_bwd(
        _make_f_attpre(d['AD'], False),
        [(h, AW, cb('aq')), (h, AKW, cb('ak')), (cosf, d['AD'], 0), (sins, d['AD'], 0)], sv['att_par'],
        [[(dqh, AW, 0)], [(dkh, AKW, 0)]], [True, True, False, False], [False, True, True], tm=256,
        name=p + "att_pre_bwd")
    gr['attn_q_norm'], gr['attn_k_norm'] = dqg[0], dkg[0]

    (do_dn, dzb), (dng,) = _rowwise_bwd(_make_f_dnpost(d['DNK']), sv['dnpost_rows'], sv['dnpost_par'],
                                        [[(dys[1], DW, 0)]], [True, True], [True], tm=256, name=p + "dn_post_bwd")
    gr['dn_norm_g'] = dng[0]
    accs, dn_dgates = None, []
    for dr in range(2):
        res = _delta_bwd(sv['qn'], sv['kn'], sv['act'], sv['gates'][dr], sv['dn_state'][dr], do_dn, vcb=2,
                         rev=bool(dr), accs=accs, name=p + f"dn_bwd{dr}", d=d)
        accs = res[:3]
        dn_dgates += res[3:]
    dqn, dkn, dvc = accs
    (dqc, dkc, ddadb), (dalog, ddtb) = _rowwise_bwd(
        _make_f_dnpre(H, d['DNK'], c['CHUNK']), sv['dn_rows'], sv['dn_par'],
        [[(t, DW, 0)] for t in [dqn, dkn] + dn_dgates], [True] * 3, [True, True], tm=256, name=p + "dn_pre_bwd")
    gr['dn_a_log'] = dalog[0, :2 * H].reshape(2, H)
    gr['dn_dt_bias'] = ddtb[0, :2 * H].reshape(2, H)
    dconv_x, dconv_w = _conv_bwd(h, cb('dq'), wt['conv'], jnp.concatenate([dqc, dkc, dvc], axis=1), tm=256,
                                 name=p + "dn_conv_bwd", d=d)
    gr['dn_conv'] = jnp.transpose(dconv_w[:, :c['CONV'], :], (0, 2, 1)).reshape(3 * DW, c['CONV'])

    (dysum, du, dza), (dd, dwglu, dbglu) = _rowwise_bwd(_f_s5tail, sv['s5_rows'], sv['s5_par'], [[(dys[0], SW, 0)]],
                                                        [True] * 3, [True] * 3, tm=256, name=p + "s5_tail_bwd")
    gr['ssm_d'], gr['ssm_w_glu'], gr['ssm_b_glu'] = dd[0], dwglu, dbglu[0]
    s5g = []
    for dr in range(2):
        wb16, wc16, lt, lt_adj, cin = sv['s5'][dr]
        du, dwb, dwc, dlam = _s5_bwd(h, cb('u_a'), dysum, cin, wb16, wc16, lt, lt_adj, rev=bool(dr), acc=du, tb=256,
                                     name=p + f"s5_bwd{dr}", d=d)
        dl = jnp.sum(dlam, axis=0).reshape(d['NB'], 2, d['BS'])
        _, prep_vjp = jax.vjp(lambda *pp: _s5_prep(*pp, d), *_s5_dir_params(a, l, dr))
        s5g.append(prep_vjp((dwb, dwc, dl[:, 0], dl[:, 1])))
    for i, nm in enumerate(['ssm_a_re', 'ssm_a_im', 'ssm_log_step', 'ssm_b_re', 'ssm_b_im', 'ssm_c_re', 'ssm_c_im']):
        gr[nm] = jnp.stack([s5g[0][i], s5g[1][i]], axis=0)

    dh = _assemble_dh({'u_a': du, 'z_a': dza, 'dq': dconv_x, 'z_b': dzb, 'ak': dak, 'av': dvh, 'aq': daq,
                       'z_c': dzc, 'mq': dmq, 'z_m': dzm, 'gates': jnp.concatenate(dgates, axis=1),
                       'dadb': ddadb}, seg, L).astype(BF16)
    gr['wp'] = _mm(sv['xn'], dh, ta=True, name=p + "dw_in", tm=1024, tn=1536, tk=1024)
    dxn = _mm(dh, wt['wp'], tb=True, name=p + "d_xn", tm=1024, tn=1024, tk=1536)
    (dx_in,), (dg_norm,) = _rowwise_bwd(_f_norm, [(sv['x'], D, 0)], [sv['g_norm']], [[(dxn, D, 0)]], [True], [True],
                                        tm=256, name=p + "norm_bwd", accs={0: (dx, D, 0)})
    gr['norm_g'] = dg_norm[0]
    return dx_in, gr


_ARG_NAMES = (['x', 'mem'] + WEIGHTS + ['loss_target'] + ['m_' + w for w in WEIGHTS] + ['v_' + w for w in WEIGHTS])


def kernel(x, mem, norm_g, w_in, ssm_a_re, ssm_a_im, ssm_log_step, ssm_b_re, ssm_b_im, ssm_c_re, ssm_c_im,
           ssm_d, ssm_w_glu, ssm_b_glu, dn_conv, dn_a_log, dn_dt_bias, dn_norm_g, attn_q_norm, attn_k_norm,
           mem_norm_g, w_mem_kv, w_branch, w_out, final_norm_g, loss_target, m_norm_g, m_w_in, m_ssm_a_re,
           m_ssm_a_im, m_ssm_log_step, m_ssm_b_re, m_ssm_b_im, m_ssm_c_re, m_ssm_c_im, m_ssm_d, m_ssm_w_glu,
           m_ssm_b_glu, m_dn_conv, m_dn_a_log, m_dn_dt_bias, m_dn_norm_g, m_attn_q_norm, m_attn_k_norm,
           m_mem_norm_g, m_w_mem_kv, m_w_branch, m_w_out, m_final_norm_g, v_norm_g, v_w_in, v_ssm_a_re,
           v_ssm_a_im, v_ssm_log_step, v_ssm_b_re, v_ssm_b_im, v_ssm_c_re, v_ssm_c_im, v_ssm_d, v_ssm_w_glu,
           v_ssm_b_glu, v_dn_conv, v_dn_a_log, v_dn_dt_bias, v_dn_norm_g, v_attn_q_norm, v_attn_k_norm,
           v_mem_norm_g, v_w_mem_kv, v_w_branch, v_w_out, v_final_norm_g):
    given = locals()
    return _train_step({n: given[n] for n in _ARG_NAMES})


def _train_step(a):
    c = CFG
    d = _dims(c)
    seg, rseg = _layout(c)
    depth, nh2 = c['DEPTH'], 2 * c['DNH']
    x, mem, tgt = a['x'][0], a['mem'][0], a['loss_target'][0]
    L, D = x.shape

    packed = [n for n in SHARDED if n != 'w_in']
    shard_shapes = [a[n].shape for n in packed]
    rw = _round_up(sum(_rows_of(s) for s in shard_shapes), LANE)
    win_shape = a['w_in'].shape
    wcols = win_shape[2]
    g_win, gathered = _all_gather([a['w_in'].astype(BF16).reshape(depth * D, wcols),
                                   _pack([a[n] for n in packed], BF16, total_rows=rw)], name="weights_all_gather")
    full = {n: _to_full(p_) for n, p_ in zip(packed, _unpack(gathered, shard_shapes, lead=1))}
    offs = np.cumsum([0, d['SW'], d['DW'], d['AW'], d['MW']])
    wts = []
    for l in range(depth):
        conv = jnp.transpose(full['dn_conv'][l].astype(F32).reshape(3, d['DW'], c['CONV']), (0, 2, 1))
        wts.append(dict(
            wp=_w_in_to_layout(g_win[:, l * D:(l + 1) * D], seg, rseg, nh2),
            w_branch=[full['w_branch'][l, offs[i]:offs[i + 1]] for i in range(4)],
            w_out=full['w_out'][l], w_mem_kv=full['w_mem_kv'][l], w_glu=full['ssm_w_glu'][l].astype(F32),
            conv=jnp.pad(conv, ((0, 0), (0, 8 - c['CONV']), (0, 0)))))
    rope = _rope_tables(L, c)

    saved = []
    for l in range(depth):
        x, sv = _layer_fwd(x, mem, l, wts[l], a, rope, c, d, seg)
        saved.append(sv)
    loss_part, dx, dg_final = _loss_grad(x, a['final_norm_g'][None, :], tgt, tm=256, name="final_norm_loss")
    grads = [None] * depth
    for l in reversed(range(depth)):
        dx, grads[l] = _layer_bwd(dx, mem, l, wts[l], a, rope, saved[l], c, d, seg)

    gfull = {n: jnp.stack([grads[l][n] for l in range(depth)], axis=0) for n in WEIGHTS
             if n not in ('w_in', 'final_norm_g')}
    gfull['final_norm_g'] = dg_final[0]

    win_slabs = jnp.concatenate([_w_in_slabs(grads[l]['wp'], seg, rseg, nh2) for l in range(depth)], axis=1)
    small_shapes = [a[n].shape for n in SMALL] + [(1,)]
    rs = _round_up(sum(_rows_of(s) for s in small_shapes), LANE)
    g_shard = _pack([_to_slabs(gfull[n]) for n in packed], BF16, lead=1, total_rows=rw)
    g_small = _pack([gfull[n] for n in SMALL] + [loss_part[0, :1]], F32, total_rows=rs)
    recv = _all_to_all([win_slabs.astype(BF16), g_shard, jnp.broadcast_to(g_small[None], (N_DEV,) + g_small.shape)],
                       name="grads_all_to_all")
    g_win_sum = _sum_slots(recv[0], tr=256, name="w_in_grad_sum")
    gsum = jnp.concatenate([_sum_slots(recv[1], tr=256, name="shard_grad_sum"),
                            _sum_slots(recv[2], tr=256, name="small_grad_sum")], axis=0)
    flat = lambda t: t.reshape(depth * D, wcols)
    d_win, m_win, v_win = _adamw(flat(a['w_in']), g_win_sum, flat(a['m_w_in']), flat(a['v_w_in']), tr=256,
                                 name="w_in_adamw")
    win_out = [t.reshape(win_shape) for t in (g_win_sum, d_win, m_win, v_win)]

    def local_pack(prefix):
        zero = jnp.zeros((1,), F32)
        return jnp.concatenate([_pack([a[prefix + n] for n in packed], F32, total_rows=rw),
                                _pack([a[prefix + n] for n in SMALL] + [zero], F32, total_rows=rs)], axis=0)

    delta, new_m, new_v = _adamw(local_pack(''), gsum, local_pack('m_'), local_pack('v_'), tr=256, name="adamw")

    def split(buf):
        vals = dict(zip(packed, _unpack(buf[:rw], shard_shapes)))
        small = _unpack(buf[rw:], small_shapes)
        vals.update(zip(SMALL, small[:-1]))
        return vals, small[-1]

    _, loss = split(gsum)
    outs = [loss.reshape(()), dx[None]]
    for i, buf in enumerate((gsum, delta, new_m, new_v)):
        vals, _ = split(buf)
        vals['w_in'] = win_out[i]
        outs += [vals[n] for n in WEIGHTS]
    return tuple(outs)
```

```python
import functools
import math

import numpy as np
import jax
import jax.numpy as jnp
from jax import lax
from jax.experimental import pallas as pl
from jax.experimental.pallas import tpu as pltpu

F32 = jnp.float32
BF16 = jnp.bfloat16
HI = lax.Precision.HIGHEST
EPS = 1e-6
LANE = 128
SUBLANE = 8
VMEM_LIMIT = 56 * 1024 * 1024
N_DEV = 8
PACK_W = 1024

ADAM_LR, ADAM_B1, ADAM_B2, ADAM_EPS, ADAM_WD, ADAM_STEP = 0.001, 0.9, 0.999, 1e-08, 0.01, 10

CFG = dict(D=2048, L=8192, GRID_W=64, NMEM=256, DEPTH=2,
           SG=48, SP=16, SN=64,
           DNH=6, DNK=128, CONV=5, CHUNK=64,
           AH=8, AKV=2, AD=128, ROPE_THETA=10000.0,
           MH=4, MD=128)

TILES = dict(att_q=512, att_k=512)

WEIGHTS = ['norm_g', 'w_in', 'ssm_a_re', 'ssm_a_im', 'ssm_log_step', 'ssm_b_re', 'ssm_b_im', 'ssm_c_re',
           'ssm_c_im', 'ssm_d', 'ssm_w_glu', 'ssm_b_glu', 'dn_conv', 'dn_a_log', 'dn_dt_bias', 'dn_norm_g',
           'attn_q_norm', 'attn_k_norm', 'mem_norm_g', 'w_mem_kv', 'w_branch', 'w_out', 'final_norm_g']
SHARDED = ['w_in', 'w_branch', 'w_out', 'w_mem_kv', 'ssm_w_glu', 'dn_conv']
SMALL = [w for w in WEIGHTS if w not in SHARDED]


def _dims(c):
    d = dict(c)
    d['SW'] = c['SG'] * c['SP']
    d['NB'] = d['SW'] // LANE
    d['GPB'] = LANE // c['SP']
    d['BS'] = d['GPB'] * c['SN']
    d['DW'] = c['DNH'] * c['DNK']
    d['AW'] = c['AH'] * c['AD']
    d['AKW'] = c['AKV'] * c['AD']
    d['MW'] = c['MH'] * c['MD']
    d['BT'] = d['SW'] + d['DW'] + d['AW'] + d['MW']
    return d


def _round_up(a, b):
    return (a + b - 1) // b * b


def _layout(c):
    d = _dims(c)
    D, SW, DW, AW, AKW, MW = d['D'], d['SW'], d['DW'], d['AW'], d['AKW'], d['MW']
    order = [('u_a', SW, SW), ('z_a', SW, SW), ('dq', DW, DW), ('dk', DW, DW), ('dv', DW, DW), ('z_b', DW, DW),
             ('ak', AKW, AKW), ('av', AKW, AKW), ('aq', AW, AW), ('z_c', AW, AW), ('mq', MW, MW), ('z_m', MW, MW),
             ('gates', 4 * D, D), ('dadb', LANE, LANE)]
    off, seg = 0, {}
    for name, w, al in order:
        off = _round_up(off, al)
        seg[name] = (off, w)
        off += w
    seg['_total'] = _round_up(off, 512)
    ref_order = [('u_a', SW), ('z_a', SW), ('dq', DW), ('dk', DW), ('dv', DW), ('da', 2 * d['DNH']),
                 ('db', 2 * d['DNH']), ('z_b', DW), ('aq', AW), ('ak', AKW), ('av', AKW), ('z_c', AW),
                 ('mq', MW), ('z_m', MW), ('gates', 4 * D)]
    roff, rseg = 0, {}
    for name, w in ref_order:
        rseg[name] = (roff, w)
        roff += w
    rseg['_total'] = roff
    return seg, rseg


def _cparams(sem):
    return pltpu.CompilerParams(dimension_semantics=sem, vmem_limit_bytes=VMEM_LIMIT)


def _pick(t, n):
    if n <= t:
        return n
    for align in (LANE, 2 * SUBLANE):
        for cand in range(t - t % align, 0, -align):
            if n % cand == 0:
                return cand
    return n


def _mm(a, b, *, name, ta=False, tb=False, add=None, out_dtype=F32, tm=1024, tn=1024, tk=512):
    M, K = (a.shape[1], a.shape[0]) if ta else a.shape
    N = b.shape[0] if tb else b.shape[1]
    assert (b.shape[1] if tb else b.shape[0]) == K
    tm, tn, tk = _pick(tm, M), _pick(tn, N), _pick(tk, K)
    nk = K // tk
    dn = (((0 if ta else 1,), (1 if tb else 0,)), ((), ()))
    has_add = add is not None

    def body(*refs):
        if has_add:
            a_ref, b_ref, add_ref, o_ref, acc = refs
        else:
            a_ref, b_ref, o_ref, acc = refs
        k = pl.program_id(2)

        @pl.when(k == 0)
        def _():
            acc[...] = jnp.zeros_like(acc)

        acc[...] += lax.dot_general(a_ref[...].astype(BF16), b_ref[...].astype(BF16), dn,
                                    preferred_element_type=F32)

        @pl.when(k == nk - 1)
        def _():
            r = acc[...]
            if has_add:
                r = r + add_ref[...]
            o_ref[...] = r.astype(o_ref.dtype)

    a_spec = pl.BlockSpec((tk, tm), lambda i, j, k: (k, i)) if ta else pl.BlockSpec((tm, tk), lambda i, j, k: (i, k))
    b_spec = pl.BlockSpec((tn, tk), lambda i, j, k: (j, k)) if tb else pl.BlockSpec((tk, tn), lambda i, j, k: (k, j))
    in_specs = [a_spec, b_spec]
    args = [a, b]
    if has_add:
        in_specs.append(pl.BlockSpec((tm, tn), lambda i, j, k: (i, j)))
        args.append(add)
    return pl.pallas_call(
        body, name=name, grid=(M // tm, N // tn, nk),
        in_specs=in_specs, out_specs=pl.BlockSpec((tm, tn), lambda i, j, k: (i, j)),
        out_shape=jax.ShapeDtypeStruct((M, N), out_dtype),
        scratch_shapes=[pltpu.VMEM((tm, tn), F32)],
        compiler_params=_cparams(("parallel", "parallel", "arbitrary")),
    )(*args)


def _row_spec(tm, w, cb):
    return pl.BlockSpec((tm, w), lambda i, cb=cb: (i, cb))


def _rowwise(fn, rows, params, outs, *, tm, name):
    L = rows[0][0].shape[0]
    tm = _pick(tm, L)
    nr, npar = len(rows), len(params)

    def body(*refs):
        vals = [r[...] for r in refs[:nr + npar]]
        res = fn(*vals)
        for o_ref, v in zip(refs[nr + npar:], res):
            o_ref[...] = v.astype(o_ref.dtype)

    in_specs = [_row_spec(tm, w, cb) for (_, w, cb) in rows]
    in_specs += [pl.BlockSpec(p.shape, lambda i: (0, 0)) for p in params]
    res = pl.pallas_call(
        body, name=name, grid=(L // tm,), in_specs=in_specs,
        out_specs=[pl.BlockSpec((tm, w), lambda i: (i, 0)) for (w, _) in outs],
        out_shape=[jax.ShapeDtypeStruct((L, w), dt) for (w, dt) in outs],
        compiler_params=_cparams(("parallel",)),
    )(*[r[0] for r in rows], *params)
    return list(res)


def _rowwise_bwd(fn, rows, params, cts, drows, dparams, *, tm, name, accs=None):
    L = rows[0][0].shape[0]
    tm = _pick(tm, L)
    nr, npar = len(rows), len(params)
    accs = accs or {}
    ct_flat = [c for grp in cts for c in grp]
    ct_sizes = [len(grp) for grp in cts]
    acc_keys = sorted(accs)
    d_r = [i for i in range(nr) if drows[i]]
    d_p = [i for i in range(npar) if dparams[i]]
    n_in = nr + npar + len(ct_flat) + len(acc_keys)

    def body(*refs):
        vals = [r[...] for r in refs[:nr + npar]]
        ct_refs = refs[nr + npar:nr + npar + len(ct_flat)]
        acc_refs = refs[nr + npar + len(ct_flat):n_in]
        o_refs = refs[n_in:]
        ct_vals, pos = [], 0
        for n in ct_sizes:
            v = ct_refs[pos][...].astype(F32)
            for r in ct_refs[pos + 1:pos + n]:
                v = v + r[...].astype(F32)
            ct_vals.append(v)
            pos += n
        diff_idx = d_r + [nr + i for i in d_p]

        def g(*dv):
            full = list(vals)
            for i, v in zip(diff_idx, dv):
                full[i] = v
            return tuple(o.astype(F32) for o in fn(*full))

        _, vjp = jax.vjp(g, *[vals[i] for i in diff_idx])
        grads = vjp(tuple(ct_vals))
        for n, i in enumerate(d_r):
            gv = grads[n].astype(F32)
            if i in accs:
                gv = gv + acc_refs[acc_keys.index(i)][...]
            o_refs[n][...] = gv
        step = pl.program_id(0)
        for n, i in enumerate(d_p):
            o_ref = o_refs[len(d_r) + n]

            @pl.when(step == 0)
            def _(o_ref=o_ref):
                o_ref[...] = jnp.zeros_like(o_ref)

            o_ref[...] += grads[len(d_r) + n].astype(F32)

    in_specs = [_row_spec(tm, w, cb) for (_, w, cb) in rows]
    in_specs += [pl.BlockSpec(p.shape, lambda i: (0, 0)) for p in params]
    in_specs += [_row_spec(tm, w, cb) for (_, w, cb) in ct_flat]
    in_specs += [_row_spec(tm, accs[k][1], accs[k][2]) for k in acc_keys]
    out_specs = [pl.BlockSpec((tm, rows[i][1]), lambda i_: (i_, 0)) for i in d_r]
    out_specs += [pl.BlockSpec(params[i].shape, lambda i_: (0, 0)) for i in d_p]
    out_shape = [jax.ShapeDtypeStruct((L, rows[i][1]), F32) for i in d_r]
    out_shape += [jax.ShapeDtypeStruct(params[i].shape, F32) for i in d_p]
    res = pl.pallas_call(
        body, name=name, grid=(L // tm,), in_specs=in_specs, out_specs=out_specs, out_shape=out_shape,
        compiler_params=_cparams(("arbitrary",)),
    )(*[r[0] for r in rows], *params, *[c[0] for c in ct_flat], *[accs[k][0] for k in acc_keys])
    res = list(res)
    return res[:len(d_r)], res[len(d_r):]


def _silu(x):
    return x * jax.nn.sigmoid(x)


def _rms(x, g):
    return x * lax.rsqrt(jnp.mean(x * x, axis=-1, keepdims=True) + EPS) * g


def _softplus(x):
    return jnp.maximum(x, 0.0) + jnp.log1p(jnp.exp(-jnp.abs(x)))


def _heads(x, hd):
    return [x[:, i * hd:(i + 1) * hd] for i in range(x.shape[1] // hd)]


def _f_norm(x, g):
    return (_rms(x, g),)


def _f_s5tail(ys, u, z, d, wglu, bglu):
    y = jax.nn.gelu(ys + d * u)
    gate = jax.nn.sigmoid(jnp.dot(y.astype(BF16), wglu.astype(BF16), preferred_element_type=F32) + bglu)
    return (y * gate * _silu(z),)


def _make_f_dnpre(nh, hd, chunk):
    def f(qc, kc, dadb, alog, dtb):
        tm = qc.shape[0]
        qn = [q * lax.rsqrt(jnp.sum(q * q, axis=-1, keepdims=True) + EPS) * (hd ** -0.5) for q in _heads(qc, hd)]
        kn = [k * lax.rsqrt(jnp.sum(k * k, axis=-1, keepdims=True) + EPS) for k in _heads(kc, hd)]
        g = -jnp.exp(alog) * _softplus(dadb + dtb)
        beta = jax.nn.sigmoid(dadb)
        ii = lax.broadcasted_iota(jnp.int32, (tm, tm), 0)
        jj = lax.broadcasted_iota(jnp.int32, (tm, tm), 1)
        same = (ii // chunk) == (jj // chunk)
        outs = [jnp.concatenate(qn, axis=1), jnp.concatenate(kn, axis=1)]
        gt = jnp.dot(same.astype(F32), g, precision=HI, preferred_element_type=F32)
        for dr in range(2):
            tri = jnp.logical_and(same, (ii <= jj) if dr else (ii >= jj)).astype(F32)
            gc = jnp.dot(tri, g, precision=HI, preferred_element_type=F32)

            def spread(t, lane0):
                return jnp.concatenate([jnp.broadcast_to(t[:, lane0 + h:lane0 + h + 1], (tm, hd))
                                        for h in range(nh)], axis=1)

            outs += [spread(beta, 2 * nh + dr * nh), spread(gc, dr * nh), spread(gt, dr * nh)]
        return tuple(outs)
    return f


def _make_f_dnpost(hd):
    def f(o, z, ng):
        y = [_rms(oh, ng) for oh in _heads(o, hd)]
        return (jnp.concatenate(y, axis=1) * _silu(z),)
    return f


def _make_f_attpre(hd, with_v):
    def rope(x, g, cosf, sins, perm, scale):
        xn = _rms(x, g)
        xs = jnp.dot(xn, perm, precision=HI, preferred_element_type=F32)
        return (xn * cosf + xs * sins) * scale

    def f(aq, ak, *rest):
        if with_v:
            av, cosf, sins, perm, qg, kg = rest
        else:
            cosf, sins, perm, qg, kg = rest
        qh = jnp.concatenate([rope(x, qg, cosf, sins, perm, hd ** -0.5) for x in _heads(aq, hd)], axis=1)
        kh = jnp.concatenate([rope(x, kg, cosf, sins, perm, 1.0) for x in _heads(ak, hd)], axis=1)
        return (qh, kh, av) if with_v else (qh, kh)
    return f


def _f_gate(o, z):
    return (o * _silu(z),)


def _make_f_mem(nh, hd):
    def f(mq, z, kv):
        mw = nh * hd
        outs = []
        for h, q in enumerate(_heads(mq, hd)):
            k = kv[:, h * hd:(h + 1) * hd]
            v = kv[:, mw + h * hd:mw + (h + 1) * hd]
            s = lax.dot_general(q.astype(BF16), k.astype(BF16), (((1,), (1,)), ((), ())),
                                preferred_element_type=F32) * (hd ** -0.5)
            s = s - jnp.max(s, axis=-1, keepdims=True)
            p = jnp.exp(s)
            p = p / jnp.sum(p, axis=-1, keepdims=True)
            outs.append(jnp.dot(p.astype(BF16), v.astype(BF16), preferred_element_type=F32))
        return (jnp.concatenate(outs, axis=1) * _silu(z),)
    return f


def _f_merge(p0, p1, p2, p3, g0, g1, g2, g3):
    return (jax.nn.sigmoid(g0) * p0 + jax.nn.sigmoid(g1) * p1 + jax.nn.sigmoid(g2) * p2 + jax.nn.sigmoid(g3) * p3,)


def _make_f_delta(hd):
    def f(do, o):
        out = [jnp.broadcast_to(jnp.sum(a * b, axis=-1, keepdims=True), a.shape)
               for a, b in zip(_heads(do, hd), _heads(o, hd))]
        return (jnp.concatenate(out, axis=1),)
    return f


def _s5_prep(a_re, a_im, log_step, b_re, b_im, c_re, c_im, d):
    nb, gpb, sn, sp = d['NB'], d['GPB'], d['SN'], d['SP']
    step = jnp.exp(log_step)[:, None]
    mag = jnp.exp(a_re * step)
    lam_re = mag * jnp.cos(a_im * step)
    lam_im = mag * jnp.sin(a_im * step)
    den = a_re * a_re + a_im * a_im
    nr, ni = lam_re - 1.0, lam_im
    coef_re = (nr * a_re + ni * a_im) / den
    coef_im = (ni * a_re - nr * a_im) / den
    bb_re = coef_re[..., None] * b_re - coef_im[..., None] * b_im
    bb_im = coef_re[..., None] * b_im + coef_im[..., None] * b_re
    eye = jnp.eye(gpb, dtype=F32)

    def blk_in(bb):
        t = bb.reshape(nb, gpb, sn, sp)
        return jnp.einsum("jgnp,gh->jgphn", t, eye).reshape(nb, gpb * sp, gpb * sn)

    def blk_out(cc):
        t = cc.reshape(nb, gpb, sp, sn)
        return jnp.einsum("jgpn,gh->jgnhp", t, eye).reshape(nb, gpb * sn, gpb * sp)

    wb = jnp.concatenate([blk_in(bb_re), blk_in(bb_im)], axis=2)
    wc = jnp.concatenate([blk_out(c_re), blk_out(-c_im)], axis=1)
    return wb, wc, lam_re.reshape(nb, gpb * sn), lam_im.reshape(nb, gpb * sn)


def _s5_tables(lam_re, lam_im, rev, conj):
    lr, li = lam_re, (-lam_im if conj else lam_im)

    def cmul(a, b):
        return a[0] * b[0] - a[1] * b[1], a[0] * b[1] + a[1] * b[0]

    pw = [(lr, li)]
    for _ in range(7):
        pw.append(cmul(pw[-1], (lr, li)))
    rows = jnp.arange(8)

    def bc(t, k):
        keep = (rows < 8 - k) if rev else (rows >= k)
        return t[:, None, :] * keep.astype(F32)[None, :, None]

    order = list(range(8))[::-1] if rev else list(range(8))
    pwr = jnp.stack([pw[i][0] for i in order], axis=1)
    pwi = jnp.stack([pw[i][1] for i in order], axis=1)
    tabs = [bc(pw[0][0], 1), bc(pw[0][1], 1), bc(pw[1][0], 2), bc(pw[1][1], 2), bc(pw[3][0], 4), bc(pw[3][1], 4),
            pwr, pwi]
    return jnp.stack(tabs, axis=1)


def _scan_group(xr, xi, lt_ref, j, cr, ci, rev):
    for lvl, k in enumerate((1, 2, 4)):
        l_r, l_i = lt_ref[j, 2 * lvl], lt_ref[j, 2 * lvl + 1]
        sh = (8 - k) if rev else k
        sr, si = pltpu.roll(xr, sh, 0), pltpu.roll(xi, sh, 0)
        xr, xi = xr + l_r * sr - l_i * si, xi + l_r * si + l_i * sr
    p_r, p_i = lt_ref[j, 6], lt_ref[j, 7]
    return xr + p_r * cr - p_i * ci, xi + p_r * ci + p_i * cr


def _last_row(x, rev):
    last = 0 if rev else 7
    return jnp.broadcast_to(x[last:last + 1, :], x.shape)


def _s5_fwd(hsrc, ucb, wb, wc, lt, *, rev, acc, tb, name, d):
    L, SW, NB, BS = hsrc.shape[0], d['SW'], d['NB'], d['BS']
    tb = _pick(tb, L)
    nblk, ngr = L // tb, tb // 8
    tix = (lambda b: nblk - 1 - b) if rev else (lambda b: b)
    has_acc = acc is not None

    def body(*refs):
        if has_acc:
            u_ref, wb_ref, wc_ref, lt_ref, acc_ref, y_ref, cin_ref, bu_s, car = refs
        else:
            u_ref, wb_ref, wc_ref, lt_ref, y_ref, cin_ref, bu_s, car = refs

        @pl.when(pl.program_id(0) == 0)
        def _():
            car[...] = jnp.zeros_like(car)

        cin_ref[...] = car[...]
        for j in range(NB):
            bu_s[:, j * 2 * BS:(j + 1) * 2 * BS] = jnp.dot(
                u_ref[:, j * LANE:(j + 1) * LANE].astype(BF16), wb_ref[j], preferred_element_type=F32)

        def grp(r, _):
            base = pl.multiple_of((ngr - 1 - r if rev else r) * 8, 8)
            for j in range(NB):
                c0 = j * 2 * BS
                xr, xi = _scan_group(bu_s[pl.ds(base, 8), c0:c0 + BS], bu_s[pl.ds(base, 8), c0 + BS:c0 + 2 * BS],
                                     lt_ref, j, car[:, c0:c0 + BS], car[:, c0 + BS:c0 + 2 * BS], rev)
                bu_s[pl.ds(base, 8), c0:c0 + BS] = xr
                bu_s[pl.ds(base, 8), c0 + BS:c0 + 2 * BS] = xi
                car[:, c0:c0 + BS] = _last_row(xr, rev)
                car[:, c0 + BS:c0 + 2 * BS] = _last_row(xi, rev)
            return 0

        lax.fori_loop(0, ngr, grp, 0)
        for j in range(NB):
            y = jnp.dot(bu_s[:, j * 2 * BS:(j + 1) * 2 * BS].astype(BF16), wc_ref[j], preferred_element_type=F32)
            if has_acc:
                y = y + acc_ref[:, j * LANE:(j + 1) * LANE]
            y_ref[:, j * LANE:(j + 1) * LANE] = y

    in_specs = [pl.BlockSpec((tb, SW), lambda b: (tix(b), ucb)),
                pl.BlockSpec(wb.shape, lambda b: (0, 0, 0)), pl.BlockSpec(wc.shape, lambda b: (0, 0, 0)),
                pl.BlockSpec(lt.shape, lambda b: (0, 0, 0, 0))]
    args = [hsrc, wb, wc, lt]
    if has_acc:
        in_specs.append(pl.BlockSpec((tb, SW), lambda b: (tix(b), 0)))
        args.append(acc)
    y, cin = pl.pallas_call(
        body, name=name, grid=(nblk,), in_specs=in_specs,
        out_specs=[pl.BlockSpec((tb, SW), lambda b: (tix(b), 0)),
                   pl.BlockSpec((8, NB * 2 * BS), lambda b: (tix(b), 0))],
        out_shape=[jax.ShapeDtypeStruct((L, SW), F32), jax.ShapeDtypeStruct((nblk * 8, NB * 2 * BS), F32)],
        scratch_shapes=[pltpu.VMEM((tb, NB * 2 * BS), F32), pltpu.VMEM((8, NB * 2 * BS), F32)],
        compiler_params=_cparams(("arbitrary",)),
    )(*args)
    return y, cin


def _s5_bwd(hsrc, ucb, dy, cin, wb, wc, lt, lt_adj, *, rev, acc, tb, name, d):
    L, SW, NB, BS = hsrc.shape[0], d['SW'], d['NB'], d['BS']
    tb = _pick(tb, L)
    nblk, ngr = L // tb, tb // 8
    arev = not rev
    tix = (lambda b: nblk - 1 - b) if arev else (lambda b: b)
    has_acc = acc is not None
    NT = (((1,), (1,)), ((), ()))
    TN = (((0,), (0,)), ((), ()))

    def body(*refs):
        if has_acc:
            (u_ref, dy_ref, cin_ref, wb_ref, wc_ref, lt_ref, la_ref, acc_ref,
             du_ref, dwb_ref, dwc_ref, dlam_ref, s_s, g_s, car, acar) = refs
        else:
            (u_ref, dy_ref, cin_ref, wb_ref, wc_ref, lt_ref, la_ref,
             du_ref, dwb_ref, dwc_ref, dlam_ref, s_s, g_s, car, acar) = refs

        @pl.when(pl.program_id(0) == 0)
        def _():
            acar[...] = jnp.zeros_like(acar)
            dwb_ref[...] = jnp.zeros_like(dwb_ref)
            dwc_ref[...] = jnp.zeros_like(dwc_ref)
            dlam_ref[...] = jnp.zeros_like(dlam_ref)

        car[...] = cin_ref[...]
        for j in range(NB):
            s_s[:, j * 2 * BS:(j + 1) * 2 * BS] = jnp.dot(
                u_ref[:, j * LANE:(j + 1) * LANE].astype(BF16), wb_ref[j], preferred_element_type=F32)
            g_s[:, j * 2 * BS:(j + 1) * 2 * BS] = lax.dot_general(
                dy_ref[:, j * LANE:(j + 1) * LANE].astype(BF16), wc_ref[j], NT, preferred_element_type=F32)

        def fgrp(r, _):
            base = pl.multiple_of((ngr - 1 - r if rev else r) * 8, 8)
            for j in range(NB):
                c0 = j * 2 * BS
                xr, xi = _scan_group(s_s[pl.ds(base, 8), c0:c0 + BS], s_s[pl.ds(base, 8), c0 + BS:c0 + 2 * BS],
                                     lt_ref, j, car[:, c0:c0 + BS], car[:, c0 + BS:c0 + 2 * BS], rev)
                s_s[pl.ds(base, 8), c0:c0 + BS] = xr
                s_s[pl.ds(base, 8), c0 + BS:c0 + 2 * BS] = xi
                car[:, c0:c0 + BS] = _last_row(xr, rev)
                car[:, c0 + BS:c0 + 2 * BS] = _last_row(xi, rev)
            return 0

        lax.fori_loop(0, ngr, fgrp, 0)

        row = lax.broadcasted_iota(jnp.int32, (8, BS), 0)

        def agrp(r, _):
            gi = ngr - 1 - r if arev else r
            base = pl.multiple_of(gi * 8, 8)
            pgi = gi + 1 if rev else gi - 1
            inside = jnp.logical_and(pgi >= 0, pgi < ngr)
            pbase = pl.multiple_of(jnp.clip(pgi, 0, ngr - 1) * 8, 8)
            for j in range(NB):
                c0 = j * 2 * BS
                ar, ai = _scan_group(g_s[pl.ds(base, 8), c0:c0 + BS], g_s[pl.ds(base, 8), c0 + BS:c0 + 2 * BS],
                                     la_ref, j, acar[:, c0:c0 + BS], acar[:, c0 + BS:c0 + 2 * BS], arev)
                g_s[pl.ds(base, 8), c0:c0 + BS] = ar
                g_s[pl.ds(base, 8), c0 + BS:c0 + 2 * BS] = ai
                acar[:, c0:c0 + BS] = _last_row(ar, arev)
                acar[:, c0 + BS:c0 + 2 * BS] = _last_row(ai, arev)
                sr, si = s_s[pl.ds(base, 8), c0:c0 + BS], s_s[pl.ds(base, 8), c0 + BS:c0 + 2 * BS]
                edge_r = jnp.where(inside, _last_row(s_s[pl.ds(pbase, 8), c0:c0 + BS], rev), cin_ref[:, c0:c0 + BS])
                edge_i = jnp.where(inside, _last_row(s_s[pl.ds(pbase, 8), c0 + BS:c0 + 2 * BS], rev),
                                   cin_ref[:, c0 + BS:c0 + 2 * BS])
                sh = 7 if rev else 1
                first = 7 if rev else 0
                pr = jnp.where(row == first, edge_r, pltpu.roll(sr, sh, 0))
                pi = jnp.where(row == first, edge_i, pltpu.roll(si, sh, 0))
                dlam_ref[:, c0:c0 + BS] += ar * pr + ai * pi
                dlam_ref[:, c0 + BS:c0 + 2 * BS] += ai * pr - ar * pi
            return 0

        lax.fori_loop(0, ngr, agrp, 0)
        for j in range(NB):
            a_j = g_s[:, j * 2 * BS:(j + 1) * 2 * BS].astype(BF16)
            u_j = u_ref[:, j * LANE:(j + 1) * LANE].astype(BF16)
            du = lax.dot_general(a_j, wb_ref[j], NT, preferred_element_type=F32)
            if has_acc:
                du = du + acc_ref[:, j * LANE:(j + 1) * LANE]
            du_ref[:, j * LANE:(j + 1) * LANE] = du
            dwb_ref[j] += lax.dot_general(u_j, a_j, TN, preferred_element_type=F32)
            dwc_ref[j] += lax.dot_general(s_s[:, j * 2 * BS:(j + 1) * 2 * BS].astype(BF16),
                                          dy_ref[:, j * LANE:(j + 1) * LANE].astype(BF16), TN,
                                          preferred_element_type=F32)

    W2 = NB * 2 * BS
    in_specs = [pl.BlockSpec((tb, SW), lambda b: (tix(b), ucb)), pl.BlockSpec((tb, SW), lambda b: (tix(b), 0)),
                pl.BlockSpec((8, W2), lambda b: (tix(b), 0)),
                pl.BlockSpec(wb.shape, lambda b: (0, 0, 0)), pl.BlockSpec(wc.shape, lambda b: (0, 0, 0)),
                pl.BlockSpec(lt.shape, lambda b: (0, 0, 0, 0)), pl.BlockSpec(lt_adj.shape, lambda b: (0, 0, 0, 0))]
    args = [hsrc, dy, cin, wb, wc, lt, lt_adj]
    if has_acc:
        in_specs.append(pl.BlockSpec((tb, SW), lambda b: (tix(b), 0)))
        args.append(acc)
    return pl.pallas_call(
        body, name=name, grid=(nblk,), in_specs=in_specs,
        out_specs=[pl.BlockSpec((tb, SW), lambda b: (tix(b), 0)),
                   pl.BlockSpec(wb.shape, lambda b: (0, 0, 0)), pl.BlockSpec(wc.shape, lambda b: (0, 0, 0)),
                   pl.BlockSpec((8, W2), lambda b: (0, 0))],
        out_shape=[jax.ShapeDtypeStruct((L, SW), F32), jax.ShapeDtypeStruct(wb.shape, F32),
                   jax.ShapeDtypeStruct(wc.shape, F32), jax.ShapeDtypeStruct((8, W2), F32)],
        scratch_shapes=[pltpu.VMEM((tb, W2), F32), pltpu.VMEM((tb, W2), F32),
                        pltpu.VMEM((8, W2), F32), pltpu.VMEM((8, W2), F32)],
        compiler_params=_cparams(("arbitrary",)),
    )(*args)


_NN = (((1,), (0,)), ((), ()))
_NT = (((1,), (1,)), ((), ()))
_TN = (((0,), (0,)), ((), ()))


def _dotb(a, b, dn=_NN):
    return lax.dot_general(a.astype(BF16), b.astype(BF16), dn, preferred_element_type=F32)


def _split(x):
    hi = x.astype(BF16)
    return hi, (x - hi.astype(F32)).astype(BF16)


def _dot3(a, b, dn=_NN):
    ah, al = _split(a)
    bh, bl = _split(b)
    f = lambda x, y: lax.dot_general(x, y, dn, preferred_element_type=F32)
    return f(ah, bh) + (f(ah, bl) + f(al, bh))


@jax.custom_vjp
def _dot3_nn(a, b):
    return _dot3(a, b, _NN)


_dot3_nn.defvjp(lambda a, b: (_dot3(a, b, _NN), (a, b)),
                lambda res, g: (_dotb(g, res[1], _NT), _dotb(res[0], g, _TN)))


@jax.custom_vjp
def _dot3_nt(a, b):
    return _dot3(a, b, _NT)


_dot3_nt.defvjp(lambda a, b: (_dot3(a, b, _NT), (a, b)),
                lambda res, g: (_dotb(g, res[1], _NN), _dotb(g, res[0], _TN)))


def _delta_chunk(rev, one_pass_grads, *flat):
    heads = [flat[i:i + 7] for i in range(0, len(flat), 7)]
    q, k, v, beta, gc, gt, s_in = [list(t) for t in zip(*heads)]
    c, hd = q[0].shape
    each = lambda f, *ls: [f(*t) for t in zip(*ls)]
    mm_nn = _dot3_nn if one_pass_grads else _dot3
    mm_nt = _dot3_nt if one_pass_grads else (lambda x, y: _dot3(x, y, _NT))
    ii = lax.broadcasted_iota(jnp.int32, (c, c), 0)
    jj = lax.broadcasted_iota(jnp.int32, (c, c), 1)
    incl = (ii <= jj) if rev else (ii >= jj)
    strict = (ii < jj) if rev else (ii > jj)
    eye = (ii == jj).astype(F32)
    decay = each(lambda g: jnp.where(incl, jnp.exp(jnp.where(incl, g[:, :c] - jnp.transpose(g)[:c, :], 0.0)), 0.0), gc)
    kb = each(lambda a, b: a * b, k, beta)
    a = each(lambda x, y, dc: jnp.where(strict, mm_nt(x, y) * dc, 0.0), kb, k, decay)
    tinv = each(lambda x: eye - x, a)
    p = a
    n = 2
    while n < c:
        p = each(lambda x: mm_nn(x, x), p)
        tinv = each(lambda t, x: mm_nn(t, eye + x), tinv, p)
        n *= 2
    eg = each(jnp.exp, gc)
    u = each(lambda t, x, b: mm_nn(t, x * b), tinv, v, beta)
    w = each(lambda t, x, e: mm_nn(t, x * e), tinv, kb, eg)
    intra = each(lambda x, y, dc: _dotb(x, y, _NT) * dc, q, k, decay)
    v_new = each(lambda x, y, s: x - _dotb(y, s), u, w, s_in)
    o = each(lambda x, e, s, m, vn: _dotb(x * e, s) + _dotb(m, vn), q, eg, s_in, intra, v_new)
    s_out = each(lambda s, t, x, g, vn: s * jnp.exp(jnp.broadcast_to(t[0:1, :], (hd, hd)))
                 + _dotb(x * jnp.exp(t - g), vn, _TN), s_in, gt, k, gc, v_new)
    return tuple(x for pair in zip(o, s_out) for x in pair)


def _delta_fwd(q, k, v, gates, *, vcb, rev, acc, name, d):
    L, H, hd, C = q.shape[0], d['DNH'], d['DNK'], d['CHUNK']
    nc = L // C
    cix = (lambda i: nc - 1 - i) if rev else (lambda i: i)
    has_acc = acc is not None

    def body(*refs):
        if has_acc:
            q_ref, k_ref, v_ref, b_ref, gc_ref, gt_ref, acc_ref, o_ref, ss_ref, st = refs
        else:
            q_ref, k_ref, v_ref, b_ref, gc_ref, gt_ref, o_ref, ss_ref, st = refs

        @pl.when(pl.program_id(0) == 0)
        def _():
            st[...] = jnp.zeros_like(st)

        sls = [slice(h * hd, (h + 1) * hd) for h in range(H)]
        ins = [(q_ref[:, sl], k_ref[:, sl], v_ref[:, sl], b_ref[:, sl], gc_ref[:, sl], gt_ref[:, sl], st[h])
               for h, sl in enumerate(sls)]
        accv = [acc_ref[:, sl] for sl in sls] if has_acc else None
        res = _delta_chunk(rev, False, *[t for head in ins for t in head])
        for h, sl in enumerate(sls):
            o, s_out = res[2 * h], res[2 * h + 1]
            ss_ref[0, h] = ins[h][6]
            o_ref[:, sl] = o + accv[h] if has_acc else o
            st[h] = s_out

    blk = pl.BlockSpec((C, H * hd), lambda i: (cix(i), 0))
    in_specs = [blk, blk, pl.BlockSpec((C, H * hd), lambda i: (cix(i), vcb)), blk, blk, blk]
    args = [q, k, v, *gates]
    if has_acc:
        in_specs.append(blk)
        args.append(acc)
    return pl.pallas_call(
        body, name=name, grid=(nc,), in_specs=in_specs,
        out_specs=[blk, pl.BlockSpec((1, H, hd, hd), lambda i: (cix(i), 0, 0, 0))],
        out_shape=[jax.ShapeDtypeStruct((L, H * hd), F32), jax.ShapeDtypeStruct((nc, H, hd, hd), F32)],
        scratch_shapes=[pltpu.VMEM((H, hd, hd), F32)],
        compiler_params=_cparams(("arbitrary",)),
    )(*args)


def _delta_bwd(q, k, v, gates, ssave, do, *, vcb, rev, accs, name, d):
    L, H, hd, C = q.shape[0], d['DNH'], d['DNK'], d['CHUNK']
    nc = L // C
    cix = (lambda i: i) if rev else (lambda i: nc - 1 - i)
    has_acc = accs is not None

    def body(*refs):
        if has_acc:
            (q_ref, k_ref, v_ref, b_ref, gc_ref, gt_ref, ss_ref, do_ref, aq_ref, ak_ref, av_ref,
             dq_ref, dk_ref, dv_ref, db_ref, dgc_ref, dgt_ref, dst) = refs
        else:
            (q_ref, k_ref, v_ref, b_ref, gc_ref, gt_ref, ss_ref, do_ref,
             dq_ref, dk_ref, dv_ref, db_ref, dgc_ref, dgt_ref, dst) = refs

        @pl.when(pl.program_id(0) == 0)
        def _():
            dst[...] = jnp.zeros_like(dst)

        sls = [slice(h * hd, (h + 1) * hd) for h in range(H)]
        ins = [(q_ref[:, sl], k_ref[:, sl], v_ref[:, sl], b_ref[:, sl], gc_ref[:, sl], gt_ref[:, sl], ss_ref[0, h])
               for h, sl in enumerate(sls)]
        cts = tuple(t for h, sl in enumerate(sls) for t in (do_ref[:, sl], dst[h]))
        accv = [(aq_ref[:, sl], ak_ref[:, sl], av_ref[:, sl]) for sl in sls] if has_acc else None
        _, vjp = jax.vjp(functools.partial(_delta_chunk, rev, True), *[t for head in ins for t in head])
        res = vjp(cts)
        for h, sl in enumerate(sls):
            dq, dk, dv, db, dgc, dgt, ds = res[7 * h:7 * h + 7]
            dst[h] = ds
            if has_acc:
                dq, dk, dv = dq + accv[h][0], dk + accv[h][1], dv + accv[h][2]
            dq_ref[:, sl] = dq
            dk_ref[:, sl] = dk
            dv_ref[:, sl] = dv
            db_ref[:, sl] = db
            dgc_ref[:, sl] = dgc
            dgt_ref[:, sl] = dgt

    blk = pl.BlockSpec((C, H * hd), lambda i: (cix(i), 0))
    in_specs = [blk, blk, pl.BlockSpec((C, H * hd), lambda i: (cix(i), vcb)), blk, blk, blk,
                pl.BlockSpec((1, H, hd, hd), lambda i: (cix(i), 0, 0, 0)), blk]
    args = [q, k, v, *gates, ssave, do]
    if has_acc:
        in_specs += [blk, blk, blk]
        args += list(accs)
    return pl.pallas_call(
        body, name=name, grid=(nc,), in_specs=in_specs, out_specs=[blk] * 6,
        out_shape=[jax.ShapeDtypeStruct((L, H * hd), F32)] * 6,
        scratch_shapes=[pltpu.VMEM((H, hd, hd), F32)],
        compiler_params=_cparams(("arbitrary",)),
    )(*args)


def _conv_specs(tm, w, cb0, nrb, L):
    hb = tm // 8
    last8 = L // 8 - 1
    cur = pl.BlockSpec((tm, w), lambda s, i: (i, cb0 + s))
    prev = pl.BlockSpec((8, w), lambda s, i: (jnp.maximum(i * hb - 1, 0), cb0 + s))
    nxt = pl.BlockSpec((8, w), lambda s, i: (jnp.minimum((i + 1) * hb, last8), cb0 + s))
    return [prev, cur, nxt]


def _fill_halo(dst, prev_ref, cur_ref, next_ref, i, nrb, tm):
    dst[pl.ds(0, 8), :] = jnp.where(i > 0, prev_ref[...], 0.0)
    dst[pl.ds(8, tm), :] = cur_ref[...]
    dst[pl.ds(8 + tm, 8), :] = jnp.where(i < nrb - 1, next_ref[...], 0.0)


def _conv_fwd(hsrc, cb0, wt, *, tm, name, d):
    L, w, K = hsrc.shape[0], d['DW'], d['CONV']
    tm = _pick(tm, L)
    nrb = L // tm

    def body(prev_ref, cur_ref, next_ref, w_ref, o_ref, xs):
        i = pl.program_id(1)
        _fill_halo(xs, prev_ref, cur_ref, next_ref, i, nrb, tm)
        y = jnp.zeros((tm, w), F32)
        for kk in range(K):
            y = y + w_ref[0, pl.ds(kk, 1), :] * xs[pl.ds(8 - K // 2 + kk, tm), :]
        o_ref[...] = _silu(y)

    return pl.pallas_call(
        body, name=name, grid=(3, nrb),
        in_specs=_conv_specs(tm, w, cb0, nrb, L) + [pl.BlockSpec((1, 8, w), lambda s, i: (s, 0, 0))],
        out_specs=pl.BlockSpec((tm, w), lambda s, i: (i, s)),
        out_shape=jax.ShapeDtypeStruct((L, 3 * w), F32),
        scratch_shapes=[pltpu.VMEM((tm + 16, w), F32)],
        compiler_params=_cparams(("parallel", "parallel")),
    )(hsrc, hsrc, hsrc, wt)


def _conv_bwd(hsrc, cb0, wt, dact, *, tm, name, d):
    L, w, K = hsrc.shape[0], d['DW'], d['CONV']
    tm = _pick(tm, L)
    nrb = L // tm
    half = K // 2

    def body(xp_ref, xc_ref, xn_ref, gp_ref, gc_ref, gn_ref, w_ref, dx_ref, dw_ref, xs, gs, dys):
        i = pl.program_id(1)
        _fill_halo(xs, xp_ref, xc_ref, xn_ref, i, nrb, tm)
        _fill_halo(gs, gp_ref, gc_ref, gn_ref, i, nrb, tm)
        y = jnp.zeros((tm + 8, w), F32)
        for kk in range(K):
            y = y + w_ref[0, pl.ds(kk, 1), :] * xs[pl.ds(4 - half + kk, tm + 8), :]
        sg = jax.nn.sigmoid(y)
        dys[...] = gs[pl.ds(4, tm + 8), :] * (sg * (1.0 + y * (1.0 - sg)))
        dx = jnp.zeros((tm, w), F32)
        for kk in range(K):
            dx = dx + w_ref[0, pl.ds(kk, 1), :] * dys[pl.ds(4 + half - kk, tm), :]
        dx_ref[...] = dx

        @pl.when(i == 0)
        def _():
            dw_ref[...] = jnp.zeros_like(dw_ref)

        dy = dys[pl.ds(4, tm), :]
        for kk in range(K):
            dw_ref[0, pl.ds(kk, 1), :] += jnp.sum(dy * xs[pl.ds(8 - half + kk, tm), :], axis=0, keepdims=True)

    gspecs = _conv_specs(tm, w, 0, nrb, L)
    return pl.pallas_call(
        body, name=name, grid=(3, nrb),
        in_specs=_conv_specs(tm, w, cb0, nrb, L) + gspecs + [pl.BlockSpec((1, 8, w), lambda s, i: (s, 0, 0))],
        out_specs=[pl.BlockSpec((tm, w), lambda s, i: (i, s)), pl.BlockSpec((1, 8, w), lambda s, i: (s, 0, 0))],
        out_shape=[jax.ShapeDtypeStruct((L, 3 * w), F32), jax.ShapeDtypeStruct((3, 8, w), F32)],
        scratch_shapes=[pltpu.VMEM((tm + 16, w), F32), pltpu.VMEM((tm + 16, w), F32), pltpu.VMEM((tm + 8, w), F32)],
        compiler_params=_cparams(("parallel", "arbitrary")),
    )(hsrc, hsrc, hsrc, dact, dact, dact, wt)


def _wide(v, n):
    return v if n == LANE else jnp.tile(v, (1, n // LANE))


def _attn_fwd(qh, kh, vh, *, tq, tk, name, d):
    L, H, KVH, hd = qh.shape[0], d['AH'], d['AKV'], d['AD']
    grp = H // KVH
    tq, tk = _pick(tq, L), _pick(tk, L)
    nk = L // tk

    def body(q_ref, k_ref, v_ref, o_ref, lse_ref, m_s, l_s, acc):
        j = pl.program_id(2)

        @pl.when(j == 0)
        def _():
            m_s[...] = jnp.full_like(m_s, -1e30)
            l_s[...] = jnp.zeros_like(l_s)
            acc[...] = jnp.zeros_like(acc)

        s = lax.dot_general(q_ref[...], k_ref[...], _NT, preferred_element_type=F32)
        m_old = m_s[...]
        m_new = jnp.maximum(m_old, jnp.max(s, axis=-1, keepdims=True))
        alpha = jnp.exp(m_old - m_new)
        p = jnp.exp(s - _wide(m_new, tk))
        l_s[...] = alpha * l_s[...] + jnp.sum(p, axis=-1, keepdims=True)
        acc[...] = alpha * acc[...] + jnp.dot(p.astype(BF16), v_ref[...], preferred_element_type=F32)
        m_s[...] = m_new

        @pl.when(j == nk - 1)
        def _():
            o_ref[...] = acc[...] / l_s[...]
            lse_ref[...] = m_s[...] + jnp.log(l_s[...])

    qspec = pl.BlockSpec((tq, hd), lambda h, i, j: (i, h))
    kspec = pl.BlockSpec((tk, hd), lambda h, i, j: (j, h // grp))
    return pl.pallas_call(
        body, name=name, grid=(H, L // tq, nk), in_specs=[qspec, kspec, kspec], out_specs=[qspec, qspec],
        out_shape=[jax.ShapeDtypeStruct((L, H * hd), F32), jax.ShapeDtypeStruct((L, H * hd), F32)],
        scratch_shapes=[pltpu.VMEM((tq, hd), F32), pltpu.VMEM((tq, hd), F32), pltpu.VMEM((tq, hd), F32)],
        compiler_params=_cparams(("parallel", "parallel", "arbitrary")),
    )(qh, kh, vh)


def _attn_bwd(qh, kh, vh, do, lse, delta, *, tq, tk, name, d):
    L, H, KVH, hd = qh.shape[0], d['AH'], d['AKV'], d['AD']
    grp = H // KVH
    tq, tk = _pick(tq, L), _pick(tk, L)
    nk = L // tk

    def body(q_ref, k_ref, v_ref, do_ref, lse_ref, dl_ref, dq_ref, dk_ref, dv_ref, dq_s):
        g, i, j = pl.program_id(1), pl.program_id(2), pl.program_id(3)

        @pl.when(jnp.logical_and(jnp.logical_and(g == 0, i == 0), j == 0))
        def _():
            dk_ref[...] = jnp.zeros_like(dk_ref)
            dv_ref[...] = jnp.zeros_like(dv_ref)

        @pl.when(j == 0)
        def _():
            dq_s[...] = jnp.zeros_like(dq_s)

        q, k, do_ = q_ref[...], k_ref[...], do_ref[...].astype(BF16)
        s = lax.dot_general(q, k, _NT, preferred_element_type=F32)
        p = jnp.exp(s - _wide(lse_ref[...], tk))
        dp = lax.dot_general(do_, v_ref[...], _NT, preferred_element_type=F32)
        ds = (p * (dp - _wide(dl_ref[...], tk))).astype(BF16)
        dq_s[...] += jnp.dot(ds, k, preferred_element_type=F32)
        rows = pl.ds(pl.multiple_of(j * tk, tk), tk)
        dv_ref[rows, :] += lax.dot_general(p.astype(BF16), do_, _TN, preferred_element_type=F32)
        dk_ref[rows, :] += lax.dot_general(ds, q, _TN, preferred_element_type=F32)

        @pl.when(j == nk - 1)
        def _():
            dq_ref[...] = dq_s[...]

    qspec = pl.BlockSpec((tq, hd), lambda kv, g, i, j: (i, kv * grp + g))
    kspec = pl.BlockSpec((tk, hd), lambda kv, g, i, j: (j, kv))
    colspec = pl.BlockSpec((L, hd), lambda kv, g, i, j: (0, kv))
    return pl.pallas_call(
        body, name=name, grid=(KVH, grp, L // tq, nk),
        in_specs=[qspec, kspec, kspec, qspec, qspec, qspec], out_specs=[qspec, colspec, colspec],
        out_shape=[jax.ShapeDtypeStruct((L, H * hd), F32)] + [jax.ShapeDtypeStruct((L, KVH * hd), F32)] * 2,
        scratch_shapes=[pltpu.VMEM((tq, hd), F32)],
        compiler_params=_cparams(("parallel", "arbitrary", "arbitrary", "arbitrary")),
    )(qh, kh, vh, do, lse, delta)


def _loss_grad(x, g, tgt, *, tm, name):
    L, D = x.shape
    tm = _pick(tm, L)

    def body(x_ref, g_ref, t_ref, loss_ref, dx_ref, dg_ref):
        def f(xv, gv):
            err = _rms(xv, gv) - t_ref[...]
            return 0.5 * jnp.sum(jnp.mean(err * err, axis=-1, keepdims=True))

        val, vjp = jax.vjp(f, x_ref[...], g_ref[...])
        dx, dg = vjp(jnp.ones((), F32))
        dx_ref[...] = dx

        @pl.when(pl.program_id(0) == 0)
        def _():
            loss_ref[...] = jnp.zeros_like(loss_ref)
            dg_ref[...] = jnp.zeros_like(dg_ref)

        loss_ref[...] += val
        dg_ref[...] += dg

    return pl.pallas_call(
        body, name=name, grid=(L // tm,),
        in_specs=[pl.BlockSpec((tm, D), lambda i: (i, 0)), pl.BlockSpec((1, D), lambda i: (0, 0)),
                  pl.BlockSpec((tm, D), lambda i: (i, 0))],
        out_specs=[pl.BlockSpec((8, LANE), lambda i: (0, 0)), pl.BlockSpec((tm, D), lambda i: (i, 0)),
                   pl.BlockSpec((1, D), lambda i: (0, 0))],
        out_shape=[jax.ShapeDtypeStruct((8, LANE), F32), jax.ShapeDtypeStruct((L, D), F32),
                   jax.ShapeDtypeStruct((1, D), F32)],
        compiler_params=_cparams(("arbitrary",)),
    )(x, g, tgt)


def _sum_slots(recv, *, tr, name):
    n, R, W = recv.shape
    tr = _pick(tr, R)

    def body(r_ref, o_ref):
        s = r_ref[0].astype(F32)
        for i in range(1, n):
            s = s + r_ref[i].astype(F32)
        o_ref[...] = s

    return pl.pallas_call(
        body, name=name, grid=(R // tr,),
        in_specs=[pl.BlockSpec((n, tr, W), lambda i: (0, i, 0))], out_specs=pl.BlockSpec((tr, W), lambda i: (i, 0)),
        out_shape=jax.ShapeDtypeStruct((R, W), F32), compiler_params=_cparams(("parallel",)),
    )(recv)


def _adamw(w, g, m, v, *, tr, name):
    R, W = w.shape
    tr = _pick(tr, R)
    c1 = 1.0 - ADAM_B1 ** ADAM_STEP
    c2 = 1.0 - ADAM_B2 ** ADAM_STEP

    def body(w_ref, g_ref, m_ref, v_ref, d_ref, nm_ref, nv_ref):
        gv = g_ref[...]
        nm = ADAM_B1 * m_ref[...] + (1.0 - ADAM_B1) * gv
        nv = ADAM_B2 * v_ref[...] + (1.0 - ADAM_B2) * (gv * gv)
        d_ref[...] = -ADAM_LR * ((nm / c1) / (jnp.sqrt(nv / c2) + ADAM_EPS) + ADAM_WD * w_ref[...])
        nm_ref[...] = nm
        nv_ref[...] = nv

    spec = pl.BlockSpec((tr, W), lambda i: (i, 0))
    return pl.pallas_call(
        body, name=name, grid=(R // tr,), in_specs=[spec] * 4, out_specs=[spec] * 3,
        out_shape=[jax.ShapeDtypeStruct((R, W), F32)] * 3, compiler_params=_cparams(("parallel",)),
    )(w, g, m, v)


_MESH = pl.DeviceIdType.MESH


def _all_gather(xs, *, name):
    na = len(xs)

    def body(*refs):
        x_refs, out_refs = refs[:na], refs[na:2 * na]
        send_sems, recv_sems, local_sems = refs[2 * na:]
        x, y, c = lax.axis_index("x"), lax.axis_index("y"), lax.axis_index("c")
        me, sibling = (x, y, c), (x, y, 1 - c)
        chips = [(1 - x, y), (x, 1 - y), (1 - x, 1 - y)]

        def slot(a, px, py, pc):
            return out_refs[a].at[4 * px + 2 * py + pc]

        def copy(a, k, block, to, src=None):
            return pltpu.make_async_remote_copy(
                src_ref=slot(a, *block) if src is None else src, dst_ref=slot(a, *block),
                send_sem=send_sems.at[7 * a + k], recv_sem=recv_sems.at[7 * a + k], device_id=to,
                device_id_type=_MESH)

        mine = [pltpu.make_async_copy(x_refs[a], slot(a, *me), local_sems.at[a]) for a in range(na)]
        for cp in mine:
            cp.start()
        first = []
        for a in range(na):
            first.append(copy(a, 0, me, sibling, src=x_refs[a]))
            first += [copy(a, 1 + j, me, (*chip, c), src=x_refs[a]) for j, chip in enumerate(chips)]
        for cp in first:
            cp.start()
        passed = []
        for a in range(na):
            for j, chip in enumerate(chips):
                copy(a, 1 + j, (*chip, c), me).wait_recv()
                passed.append(copy(a, 4 + j, (*chip, c), sibling))
                passed[-1].start()
        for a in range(na):
            copy(a, 0, sibling, me).wait_recv()
            for j, chip in enumerate(chips):
                copy(a, 4 + j, (*chip, 1 - c), me).wait_recv()
        for cp in first + passed:
            cp.wait_send()
        for cp in mine:
            cp.wait()

    return pl.pallas_call(
        body, name=name,
        out_shape=[jax.ShapeDtypeStruct((N_DEV,) + t.shape, t.dtype) for t in xs],
        in_specs=[pl.BlockSpec(memory_space=pl.ANY)] * na, out_specs=[pl.BlockSpec(memory_space=pl.ANY)] * na,
        scratch_shapes=[pltpu.SemaphoreType.DMA((7 * na,)), pltpu.SemaphoreType.DMA((7 * na,)),
                        pltpu.SemaphoreType.DMA((na,))],
    )(*xs)


def _all_to_all(gs, *, name):
    na, n = len(gs), N_DEV

    def body(*refs):
        g_refs, out_refs = refs[:na], refs[na:2 * na]
        send_sems, recv_sems, local_sems = refs[2 * na:]
        x, y, c = lax.axis_index("x"), lax.axis_index("y"), lax.axis_index("c")
        me = 4 * x + 2 * y + c

        def peer(mask):
            return (x ^ (mask >> 2), y ^ ((mask >> 1) & 1), c ^ (mask & 1))

        def copy(a, mask):
            px, py, pc = peer(mask)
            return pltpu.make_async_remote_copy(
                src_ref=g_refs[a].at[4 * px + 2 * py + pc], dst_ref=out_refs[a].at[me],
                send_sem=send_sems.at[7 * a + mask - 1], recv_sem=recv_sems.at[7 * a + mask - 1],
                device_id=(px, py, pc), device_id_type=_MESH)

        def arrival(a, mask):
            px, py, pc = peer(mask)
            return pltpu.make_async_remote_copy(
                src_ref=g_refs[a].at[me], dst_ref=out_refs[a].at[4 * px + 2 * py + pc],
                send_sem=send_sems.at[7 * a + mask - 1], recv_sem=recv_sems.at[7 * a + mask - 1],
                device_id=(px, py, pc), device_id_type=_MESH)

        mine = [pltpu.make_async_copy(g_refs[a].at[me], out_refs[a].at[me], local_sems.at[a]) for a in range(na)]
        for cp in mine:
            cp.start()
        sends = [copy(a, mask) for a in range(na) for mask in range(1, n)]
        for cp in sends:
            cp.start()
        for a in range(na):
            for mask in range(1, n):
                arrival(a, mask).wait_recv()
        for cp in sends:
            cp.wait_send()
        for cp in mine:
            cp.wait()

    return pl.pallas_call(
        body, name=name,
        out_shape=[jax.ShapeDtypeStruct(t.shape, t.dtype) for t in gs],
        in_specs=[pl.BlockSpec(memory_space=pl.ANY)] * na, out_specs=[pl.BlockSpec(memory_space=pl.ANY)] * na,
        scratch_shapes=[pltpu.SemaphoreType.DMA((7 * na,)), pltpu.SemaphoreType.DMA((7 * na,)),
                        pltpu.SemaphoreType.DMA((na,))],
    )(*gs)


def _rows_of(shape):
    return -(-int(np.prod(shape)) // PACK_W)


def _pack(arrs, dtype, lead=0, total_rows=None):
    pieces = []
    for a in arrs:
        f = a.reshape(a.shape[:lead] + (-1,)).astype(dtype)
        pad = (-f.shape[-1]) % PACK_W
        if pad:
            f = jnp.pad(f, [(0, 0)] * lead + [(0, pad)])
        pieces.append(f.reshape(a.shape[:lead] + (-1, PACK_W)))
    buf = jnp.concatenate(pieces, axis=lead)
    if total_rows is not None and buf.shape[lead] < total_rows:
        buf = jnp.pad(buf, [(0, 0)] * lead + [(0, total_rows - buf.shape[lead]), (0, 0)])
    return buf


def _unpack(buf, shapes, lead=0):
    out, r = [], 0
    for shp in shapes:
        n, rows = int(np.prod(shp)), _rows_of(shp)
        piece = buf[(slice(None),) * lead + (slice(r, r + rows),)]
        piece = piece.reshape(buf.shape[:lead] + (-1,))[..., :n]
        out.append(piece.reshape(buf.shape[:lead] + tuple(shp)))
        r += rows
    return out


def _to_full(parts):
    dep, r = parts.shape[1:3]
    return jnp.transpose(parts, (1, 0) + tuple(range(2, parts.ndim))).reshape((dep, N_DEV * r) + parts.shape[3:])


def _to_slabs(full):
    dep, r = full.shape[:2]
    t = full.reshape((dep, N_DEV, r // N_DEV) + full.shape[2:])
    return jnp.transpose(t, (1, 0) + tuple(range(2, t.ndim)))


def _ref_cols(parts, ro, wd):
    w, out = parts.shape[2], []
    for dev in range(N_DEV):
        lo, hi = max(ro, dev * w), min(ro + wd, (dev + 1) * w)
        if lo < hi:
            out.append(parts[dev][:, lo - dev * w:hi - dev * w])
    return out


def _w_in_to_layout(parts, seg, rseg, nh2):
    D = parts.shape[1]
    cols, off = [], 0
    names = sorted([k for k in seg if not k.startswith('_')], key=lambda k: seg[k][0])
    for nm in names:
        o, wd = seg[nm]
        if o > off:
            cols.append(jnp.zeros((D, o - off), parts.dtype))
        if nm == 'dadb':
            cols += _ref_cols(parts, rseg['da'][0], nh2) + _ref_cols(parts, rseg['db'][0], nh2)
            cols.append(jnp.zeros((D, wd - 2 * nh2), parts.dtype))
        else:
            cols += _ref_cols(parts, rseg[nm][0], wd)
        off = o + wd
    if seg['_total'] > off:
        cols.append(jnp.zeros((D, seg['_total'] - off), parts.dtype))
    return jnp.concatenate(cols, axis=1)


def _w_in_slabs(dw, seg, rseg, nh2):
    w = rseg['_total'] // N_DEV
    ref = []
    for nm in sorted([k for k in rseg if not k.startswith('_')], key=lambda k: rseg[k][0]):
        lo = {'da': seg['dadb'][0], 'db': seg['dadb'][0] + nh2}.get(nm)
        ref.append((rseg[nm][0], rseg[nm][1], seg[nm][0] if lo is None else lo))
    slabs = []
    for dev in range(N_DEV):
        cols = []
        for ro, wd, lo in ref:
            a, b = max(ro, dev * w), min(ro + wd, (dev + 1) * w)
            if a < b:
                cols.append(dw[:, lo + a - ro:lo + b - ro])
        slabs.append(jnp.concatenate(cols, axis=1))
    return jnp.stack(slabs, axis=0)


def _assemble_dh(pieces, seg, L):
    cols, off = [], 0
    for nm in sorted(pieces, key=lambda k: seg[k][0]):
        o = seg[nm][0]
        if o > off:
            cols.append(jnp.zeros((L, o - off), F32))
        cols.append(pieces[nm])
        off = o + pieces[nm].shape[1]
    if seg['_total'] > off:
        cols.append(jnp.zeros((L, seg['_total'] - off), F32))
    return jnp.concatenate(cols, axis=1)


def _lane_pad(v):
    v = v.reshape(1, -1)
    return jnp.pad(v, ((0, 0), (0, LANE - v.shape[1])))


def _rope_tables(L, c):
    rows = L // c['GRID_W']
    row = jnp.repeat(jnp.arange(rows), c['GRID_W']).astype(F32)
    col = jnp.tile(jnp.arange(c['GRID_W']), rows).astype(F32)
    axis_dim = c['AD'] // 2
    freqs = c['ROPE_THETA'] ** (-jnp.arange(0, axis_dim, 2, dtype=F32) / axis_dim)
    ang = jnp.concatenate([row[:, None] * freqs, col[:, None] * freqs], axis=-1)
    cosf = jnp.repeat(jnp.cos(ang), 2, axis=1)
    sn = jnp.sin(ang)
    sins = jnp.stack([-sn, sn], axis=-1).reshape(L, c['AD'])
    idx = np.arange(c['AD'])
    perm = np.zeros((c['AD'], c['AD']), np.float32)
    perm[idx, idx ^ 1] = 1.0
    return cosf, sins, jnp.asarray(perm)


def _s5_dir_params(a, l, dr):
    return (a['ssm_a_re'][l, dr], a['ssm_a_im'][l, dr], a['ssm_log_step'][l, dr], a['ssm_b_re'][l, dr],
            a['ssm_b_im'][l, dr], a['ssm_c_re'][l, dr], a['ssm_c_im'][l, dr])


def _layer_fwd(x, mem, l, wt, a, rope, c, d, seg):
    L, D = x.shape
    SW, DW, AW, AKW, MW, H = d['SW'], d['DW'], d['AW'], d['AKW'], d['MW'], d['DNH']
    cb = lambda nm: seg[nm][0] // seg[nm][1]
    sv = {'x': x}
    p = f"l{l}_"
    sv['g_norm'] = a['norm_g'][l][None, :]
    xn, = _rowwise(_f_norm, [(x, D, 0)], [sv['g_norm']], [(D, BF16)], tm=256, name=p + "norm")
    h = _mm(xn, wt['wp'], name=p + "in_proj", tm=1024, tn=1536, tk=1024)
    sv['xn'], sv['h'] = xn, h

    ysum, sv['s5'] = None, []
    for dr in range(2):
        wb, wc, lr, li = _s5_prep(*_s5_dir_params(a, l, dr), d)
        wb16, wc16 = wb.astype(BF16), wc.astype(BF16)
        lt = _s5_tables(lr, li, bool(dr), False)
        ysum, cin = _s5_fwd(h, cb('u_a'), wb16, wc16, lt, rev=bool(dr), acc=ysum, tb=256, name=p + f"s5_fwd{dr}", d=d)
        sv['s5'].append((wb16, wc16, lt, _s5_tables(lr, li, not bool(dr), True), cin))
    sv['ysum'] = ysum
    sv['s5_par'] = [a['ssm_d'][l][None, :], wt['w_glu'], a['ssm_b_glu'][l][None, :]]
    sv['s5_rows'] = [(ysum, SW, 0), (h, SW, cb('u_a')), (h, SW, cb('z_a'))]
    y_a, = _rowwise(_f_s5tail, sv['s5_rows'], sv['s5_par'], [(SW, F32)], tm=256, name=p + "s5_tail")

    act = _conv_fwd(h, cb('dq'), wt['conv'], tm=256, name=p + "dn_conv", d=d)
    sv['act'] = act
    sv['dn_par'] = [_lane_pad(a['dn_a_log'][l]), _lane_pad(a['dn_dt_bias'][l])]
    sv['dn_rows'] = [(act, DW, 0), (act, DW, 1), (h, LANE, seg['dadb'][0] // LANE)]
    dn_out = _rowwise(_make_f_dnpre(H, d['DNK'], c['CHUNK']), sv['dn_rows'], sv['dn_par'], [(DW, F32)] * 8,
                      tm=256, name=p + "dn_pre")
    qn, kn = dn_out[:2]
    sv['qn'], sv['kn'], sv['gates'] = qn, kn, [dn_out[2:5], dn_out[5:8]]
    o_dn, sv['dn_state'] = None, []
    for dr in range(2):
        o_dn, ss = _delta_fwd(qn, kn, act, sv['gates'][dr], vcb=2, rev=bool(dr), acc=o_dn,
                              name=p + f"dn_fwd{dr}", d=d)
        sv['dn_state'].append(ss)
    sv['dnpost_rows'] = [(o_dn, DW, 0), (h, DW, cb('z_b'))]
    sv['dnpost_par'] = [a['dn_norm_g'][l][None, :]]
    y_b, = _rowwise(_make_f_dnpost(d['DNK']), sv['dnpost_rows'], sv['dnpost_par'], [(DW, F32)], tm=256,
                    name=p + "dn_post")

    cosf, sins, perm = rope
    sv['att_par'] = [perm, a['attn_q_norm'][l][None, :], a['attn_k_norm'][l][None, :]]
    qh, kh, vh = _rowwise(_make_f_attpre(d['AD'], True),
                          [(h, AW, cb('aq')), (h, AKW, cb('ak')), (h, AKW, cb('av')), (cosf, d['AD'], 0),
                           (sins, d['AD'], 0)], sv['att_par'], [(AW, BF16), (AKW, BF16), (AKW, BF16)],
                          tm=256, name=p + "att_pre")
    o_att, lse = _attn_fwd(qh, kh, vh, tq=TILES['att_q'], tk=TILES['att_k'], name=p + "att_fwd", d=d)
    sv['qh'], sv['kh'], sv['vh'], sv['o_att'], sv['lse'] = qh, kh, vh, o_att, lse
    y_c, = _rowwise(_f_gate, [(o_att, AW, 0), (h, AW, cb('z_c'))], [], [(AW, F32)], tm=256, name=p + "att_post")

    sv['g_mem'] = a['mem_norm_g'][l][None, :]
    memn, = _rowwise(_f_norm, [(mem, D, 0)], [sv['g_mem']], [(D, BF16)], tm=256, name=p + "mem_norm")
    kv = _mm(memn, wt['w_mem_kv'], name=p + "mem_kv")
    sv['memn'], sv['kv'] = memn, kv
    y_m, = _rowwise(_make_f_mem(d['MH'], d['MD']), [(h, MW, cb('mq')), (h, MW, cb('z_m'))], [kv], [(MW, F32)],
                    tm=256, name=p + "mem_attn")

    ys = [y_a, y_b, y_c, y_m]
    ps = [_mm(y, wb_, name=p + f"branch_proj{i}") for i, (y, wb_) in enumerate(zip(ys, wt['w_branch']))]
    gcb = seg['gates'][0] // D
    sv['merge_rows'] = [(pp, D, 0) for pp in ps] + [(h, D, gcb + i) for i in range(4)]
    merged, = _rowwise(_f_merge, sv['merge_rows'], [], [(D, BF16)], tm=128, name=p + "merge")
    sv['ys'], sv['merged'] = ys, merged
    return _mm(merged, wt['w_out'], add=x, name=p + "out_proj"), sv


def _layer_bwd(dx, mem, l, wt, a, rope, sv, c, d, seg):
    L, D = dx.shape
    SW, DW, AW, AKW, MW, H = d['SW'], d['DW'], d['AW'], d['AKW'], d['MW'], d['DNH']
    cb = lambda nm: seg[nm][0] // seg[nm][1]
    p = f"l{l}_"
    h = sv['h']
    gr = {}
    dmerged = _mm(dx, wt['w_out'], tb=True, name=p + "d_merged")
    gr['w_out'] = _mm(sv['merged'], dx, ta=True, name=p + "dw_out")
    dmr, _ = _rowwise_bwd(_f_merge, sv['merge_rows'], [], [[(dmerged, D, 0)]], [True] * 8, [], tm=128,
                          name=p + "merge_bwd")
    dps, dgates = dmr[:4], dmr[4:]
    dys = [_mm(dp, wb_, tb=True, name=p + f"d_branch{i}") for i, (dp, wb_) in enumerate(zip(dps, wt['w_branch']))]
    gr['w_branch'] = jnp.concatenate(
        [_mm(y, dp, ta=True, name=p + f"dw_branch{i}") for i, (y, dp) in enumerate(zip(sv['ys'], dps))], axis=0)

    (dmq, dzm), (dkv,) = _rowwise_bwd(_make_f_mem(d['MH'], d['MD']), [(h, MW, cb('mq')), (h, MW, cb('z_m'))],
                                      [sv['kv']], [[(dys[3], MW, 0)]], [True, True], [True], tm=256,
                                      name=p + "mem_attn_bwd")
    gr['w_mem_kv'] = _mm(sv['memn'], dkv, ta=True, name=p + "dw_mem_kv")
    dmemn = _mm(dkv, wt['w_mem_kv'], tb=True, name=p + "d_memn")
    _, (dg_mem,) = _rowwise_bwd(_f_norm, [(mem, D, 0)], [sv['g_mem']], [[(dmemn, D, 0)]], [False], [True], tm=256,
                                name=p + "mem_norm_bwd")
    gr['mem_norm_g'] = dg_mem[0]

    (do_att, dzc), _ = _rowwise_bwd(_f_gate, [(sv['o_att'], AW, 0), (h, AW, cb('z_c'))], [], [[(dys[2], AW, 0)]],
                                    [True, True], [], tm=256, name=p + "att_post_bwd")
    delta, = _rowwise(_make_f_delta(d['AD']), [(do_att, AW, 0), (sv['o_att'], AW, 0)], [], [(AW, F32)], tm=256,
                      name=p + "att_delta")
    att_in = (sv['qh'], sv['kh'], sv['vh'], do_att, sv['lse'], delta)
    dqh, dkh, dvh = _attn_bwd(*att_in, tq=TILES['att_q'], tk=TILES['att_k'], name=p + "att_bwd", d=d)
    cosf, sins, _ = rope
    (daq, dak), (dqg, dkg) = _rowwise_bwd(
        _make_f_attpre(d['AD'], False),
        [(h, AW, cb('aq')), (h, AKW, cb('ak')), (cosf, d['AD'], 0), (sins, d['AD'], 0)], sv['att_par'],
        [[(dqh, AW, 0)], [(dkh, AKW, 0)]], [True, True, False, False], [False, True, True], tm=256,
        name=p + "att_pre_bwd")
    gr['attn_q_norm'], gr['attn_k_norm'] = dqg[0], dkg[0]

    (do_dn, dzb), (dng,) = _rowwise_bwd(_make_f_dnpost(d['DNK']), sv['dnpost_rows'], sv['dnpost_par'],
                                        [[(dys[1], DW, 0)]], [True, True], [True], tm=256, name=p + "dn_post_bwd")
    gr['dn_norm_g'] = dng[0]
    accs, dn_dgates = None, []
    for dr in range(2):
        res = _delta_bwd(sv['qn'], sv['kn'], sv['act'], sv['gates'][dr], sv['dn_state'][dr], do_dn, vcb=2,
                         rev=bool(dr), accs=accs, name=p + f"dn_bwd{dr}", d=d)
        accs = res[:3]
        dn_dgates += res[3:]
    dqn, dkn, dvc = accs
    (dqc, dkc, ddadb), (dalog, ddtb) = _rowwise_bwd(
        _make_f_dnpre(H, d['DNK'], c['CHUNK']), sv['dn_rows'], sv['dn_par'],
        [[(t, DW, 0)] for t in [dqn, dkn] + dn_dgates], [True] * 3, [True, True], tm=256, name=p + "dn_pre_bwd")
    gr['dn_a_log'] = dalog[0, :2 * H].reshape(2, H)
    gr['dn_dt_bias'] = ddtb[0, :2 * H].reshape(2, H)
    dconv_x, dconv_w = _conv_bwd(h, cb('dq'), wt['conv'], jnp.concatenate([dqc, dkc, dvc], axis=1), tm=256,
                                 name=p + "dn_conv_bwd", d=d)
    gr['dn_conv'] = jnp.transpose(dconv_w[:, :c['CONV'], :], (0, 2, 1)).reshape(3 * DW, c['CONV'])

    (dysum, du, dza), (dd, dwglu, dbglu) = _rowwise_bwd(_f_s5tail, sv['s5_rows'], sv['s5_par'], [[(dys[0], SW, 0)]],
                                                        [True] * 3, [True] * 3, tm=256, name=p + "s5_tail_bwd")
    gr['ssm_d'], gr['ssm_w_glu'], gr['ssm_b_glu'] = dd[0], dwglu, dbglu[0]
    s5g = []
    for dr in range(2):
        wb16, wc16, lt, lt_adj, cin = sv['s5'][dr]
        du, dwb, dwc, dlam = _s5_bwd(h, cb('u_a'), dysum, cin, wb16, wc16, lt, lt_adj, rev=bool(dr), acc=du, tb=256,
                                     name=p + f"s5_bwd{dr}", d=d)
        dl = jnp.sum(dlam, axis=0).reshape(d['NB'], 2, d['BS'])
        _, prep_vjp = jax.vjp(lambda *pp: _s5_prep(*pp, d), *_s5_dir_params(a, l, dr))
        s5g.append(prep_vjp((dwb, dwc, dl[:, 0], dl[:, 1])))
    for i, nm in enumerate(['ssm_a_re', 'ssm_a_im', 'ssm_log_step', 'ssm_b_re', 'ssm_b_im', 'ssm_c_re', 'ssm_c_im']):
        gr[nm] = jnp.stack([s5g[0][i], s5g[1][i]], axis=0)

    dh = _assemble_dh({'u_a': du, 'z_a': dza, 'dq': dconv_x, 'z_b': dzb, 'ak': dak, 'av': dvh, 'aq': daq,
                       'z_c': dzc, 'mq': dmq, 'z_m': dzm, 'gates': jnp.concatenate(dgates, axis=1),
                       'dadb': ddadb}, seg, L).astype(BF16)
    gr['wp'] = _mm(sv['xn'], dh, ta=True, name=p + "dw_in", tm=1024, tn=1536, tk=1024)
    dxn = _mm(dh, wt['wp'], tb=True, name=p + "d_xn", tm=1024, tn=1024, tk=1536)
    (dx_in,), (dg_norm,) = _rowwise_bwd(_f_norm, [(sv['x'], D, 0)], [sv['g_norm']], [[(dxn, D, 0)]], [True], [True],
                                        tm=256, name=p + "norm_bwd", accs={0: (dx, D, 0)})
    gr['norm_g'] = dg_norm[0]
    return dx_in, gr


_ARG_NAMES = (['x', 'mem'] + WEIGHTS + ['loss_target'] + ['m_' + w for w in WEIGHTS] + ['v_' + w for w in WEIGHTS])


def kernel(x, mem, norm_g, w_in, ssm_a_re, ssm_a_im, ssm_log_step, ssm_b_re, ssm_b_im, ssm_c_re, ssm_c_im,
           ssm_d, ssm_w_glu, ssm_b_glu, dn_conv, dn_a_log, dn_dt_bias, dn_norm_g, attn_q_norm, attn_k_norm,
           mem_norm_g, w_mem_kv, w_branch, w_out, final_norm_g, loss_target, m_norm_g, m_w_in, m_ssm_a_re,
           m_ssm_a_im, m_ssm_log_step, m_ssm_b_re, m_ssm_b_im, m_ssm_c_re, m_ssm_c_im, m_ssm_d, m_ssm_w_glu,
           m_ssm_b_glu, m_dn_conv, m_dn_a_log, m_dn_dt_bias, m_dn_norm_g, m_attn_q_norm, m_attn_k_norm,
           m_mem_norm_g, m_w_mem_kv, m_w_branch, m_w_out, m_final_norm_g, v_norm_g, v_w_in, v_ssm_a_re,
           v_ssm_a_im, v_ssm_log_step, v_ssm_b_re, v_ssm_b_im, v_ssm_c_re, v_ssm_c_im, v_ssm_d, v_ssm_w_glu,
           v_ssm_b_glu, v_dn_conv, v_dn_a_log, v_dn_dt_bias, v_dn_norm_g, v_attn_q_norm, v_attn_k_norm,
           v_mem_norm_g, v_w_mem_kv, v_w_branch, v_w_out, v_final_norm_g):
    given = locals()
    return _train_step({n: given[n] for n in _ARG_NAMES})


def _train_step(a):
    c = CFG
    d = _dims(c)
    seg, rseg = _layout(c)
    depth, nh2 = c['DEPTH'], 2 * c['DNH']
    x, mem, tgt = a['x'][0], a['mem'][0], a['loss_target'][0]
    L, D = x.shape

    packed = [n for n in SHARDED if n != 'w_in']
    shard_shapes = [a[n].shape for n in packed]
    rw = _round_up(sum(_rows_of(s) for s in shard_shapes), LANE)
    win_shape = a['w_in'].shape
    wcols = win_shape[2]
    g_win, gathered = _all_gather([a['w_in'].astype(BF16).reshape(depth * D, wcols),
                                   _pack([a[n] for n in packed], BF16, total_rows=rw)], name="weights_all_gather")
    full = {n: _to_full(p_) for n, p_ in zip(packed, _unpack(gathered, shard_shapes, lead=1))}
    offs = np.cumsum([0, d['SW'], d['DW'], d['AW'], d['MW']])
    wts = []
    for l in range(depth):
        conv = jnp.transpose(full['dn_conv'][l].astype(F32).reshape(3, d['DW'], c['CONV']), (0, 2, 1))
        wts.append(dict(
            wp=_w_in_to_layout(g_win[:, l * D:(l + 1) * D], seg, rseg, nh2),
            w_branch=[full['w_branch'][l, offs[i]:offs[i + 1]] for i in range(4)],
            w_out=full['w_out'][l], w_mem_kv=full['w_mem_kv'][l], w_glu=full['ssm_w_glu'][l].astype(F32),
            conv=jnp.pad(conv, ((0, 0), (0, 8 - c['CONV']), (0, 0)))))
    rope = _rope_tables(L, c)

    saved = []
    for l in range(depth):
        x, sv = _layer_fwd(x, mem, l, wts[l], a, rope, c, d, seg)
        saved.append(sv)
    loss_part, dx, dg_final = _loss_grad(x, a['final_norm_g'][None, :], tgt, tm=256, name="final_norm_loss")
    grads = [None] * depth
    for l in reversed(range(depth)):
        dx, grads[l] = _layer_bwd(dx, mem, l, wts[l], a, rope, saved[l], c, d, seg)

    gfull = {n: jnp.stack([grads[l][n] for l in range(depth)], axis=0) for n in WEIGHTS
             if n not in ('w_in', 'final_norm_g')}
    gfull['final_norm_g'] = dg_final[0]

    win_slabs = jnp.concatenate([_w_in_slabs(grads[l]['wp'], seg, rseg, nh2) for l in range(depth)], axis=1)
    small_shapes = [a[n].shape for n in SMALL] + [(1,)]
    rs = _round_up(sum(_rows_of(s) for s in small_shapes), LANE)
    g_shard = _pack([_to_slabs(gfull[n]) for n in packed], BF16, lead=1, total_rows=rw)
    g_small = _pack([gfull[n] for n in SMALL] + [loss_part[0, :1]], F32, total_rows=rs)
    recv = _all_to_all([win_slabs.astype(BF16), g_shard, jnp.broadcast_to(g_small[None], (N_DEV,) + g_small.shape)],
                       name="grads_all_to_all")
    g_win_sum = _sum_slots(recv[0], tr=256, name="w_in_grad_sum")
    gsum = jnp.concatenate([_sum_slots(recv[1], tr=256, name="shard_grad_sum"),
                            _sum_slots(recv[2], tr=256, name="small_grad_sum")], axis=0)
    flat = lambda t: t.reshape(depth * D, wcols)
    d_win, m_win, v_win = _adamw(flat(a['w_in']), g_win_sum, flat(a['m_w_in']), flat(a['v_w_in']), tr=256,
                                 name="w_in_adamw")
    win_out = [t.reshape(win_shape) for t in (g_win_sum, d_win, m_win, v_win)]

    def local_pack(prefix):
        zero = jnp.zeros((1,), F32)
        return jnp.concatenate([_pack([a[prefix + n] for n in packed], F32, total_rows=rw),
                                _pack([a[prefix + n] for n in SMALL] + [zero], F32, total_rows=rs)], axis=0)

    delta, new_m, new_v = _adamw(local_pack(''), gsum, local_pack('m_'), local_pack('v_'), tr=256, name="adamw")

    def split(buf):
        vals = dict(zip(packed, _unpack(buf[:rw], shard_shapes)))
        small = _unpack(buf[rw:], small_shapes)
        vals.update(zip(SMALL, small[:-1]))
        return vals, small[-1]

    _, loss = split(gsum)
    outs = [loss.reshape(()), dx[None]]
    for i, buf in enumerate((gsum, delta, new_m, new_v)):
        vals, _ = split(buf)
        vals['w_in'] = win_out[i]
        outs += [vals[n] for n in WEIGHTS]
    return tuple(outs)
```

```python
import functools
import math

import numpy as np
import jax
import jax.numpy as jnp
from jax import lax
from jax.experimental import pallas as pl
from jax.experimental.pallas import tpu as pltpu

F32 = jnp.float32
BF16 = jnp.bfloat16
HI = lax.Precision.HIGHEST
EPS = 1e-6
LANE = 128
SUBLANE = 8
VMEM_LIMIT = 56 * 1024 * 1024
N_DEV = 8
PACK_W = 1024

ADAM_LR, ADAM_B1, ADAM_B2, ADAM_EPS, ADAM_WD, ADAM_STEP = 0.001, 0.9, 0.999, 1e-08, 0.01, 10

CFG = dict(D=2048, L=8192, GRID_W=64, NMEM=256, DEPTH=2,
           SG=48, SP=16, SN=64,
           DNH=6, DNK=128, CONV=5, CHUNK=64,
           AH=8, AKV=2, AD=128, ROPE_THETA=10000.0,
           MH=4, MD=128)

TILES = dict(att_q=1024, att_k=2048, s5_t=512)

WEIGHTS = ['norm_g', 'w_in', 'ssm_a_re', 'ssm_a_im', 'ssm_log_step', 'ssm_b_re', 'ssm_b_im', 'ssm_c_re',
           'ssm_c_im', 'ssm_d', 'ssm_w_glu', 'ssm_b_glu', 'dn_conv', 'dn_a_log', 'dn_dt_bias', 'dn_norm_g',
           'attn_q_norm', 'attn_k_norm', 'mem_norm_g', 'w_mem_kv', 'w_branch', 'w_out', 'final_norm_g']
SHARDED = ['w_in', 'w_branch', 'w_out', 'w_mem_kv', 'ssm_w_glu', 'dn_conv']
SMALL = [w for w in WEIGHTS if w not in SHARDED]


def _dims(c):
    d = dict(c)
    d['SW'] = c['SG'] * c['SP']
    d['NB'] = d['SW'] // LANE
    d['GPB'] = LANE // c['SP']
    d['BS'] = d['GPB'] * c['SN']
    d['DW'] = c['DNH'] * c['DNK']
    d['AW'] = c['AH'] * c['AD']
    d['AKW'] = c['AKV'] * c['AD']
    d['MW'] = c['MH'] * c['MD']
    d['BT'] = d['SW'] + d['DW'] + d['AW'] + d['MW']
    return d


def _round_up(a, b):
    return (a + b - 1) // b * b


def _layout(c):
    d = _dims(c)
    D, SW, DW, AW, AKW, MW = d['D'], d['SW'], d['DW'], d['AW'], d['AKW'], d['MW']
    order = [('u_a', SW, SW), ('z_a', SW, SW), ('dq', DW, DW), ('dk', DW, DW), ('dv', DW, DW), ('z_b', DW, DW),
             ('ak', AKW, AKW), ('av', AKW, AKW), ('aq', AW, AW), ('z_c', AW, AW), ('mq', MW, MW), ('z_m', MW, MW),
             ('gates', 4 * D, D), ('dadb', LANE, LANE)]
    off, seg = 0, {}
    for name, w, al in order:
        off = _round_up(off, al)
        seg[name] = (off, w)
        off += w
    seg['_total'] = _round_up(off, 512)
    ref_order = [('u_a', SW), ('z_a', SW), ('dq', DW), ('dk', DW), ('dv', DW), ('da', 2 * d['DNH']),
                 ('db', 2 * d['DNH']), ('z_b', DW), ('aq', AW), ('ak', AKW), ('av', AKW), ('z_c', AW),
                 ('mq', MW), ('z_m', MW), ('gates', 4 * D)]
    roff, rseg = 0, {}
    for name, w in ref_order:
        rseg[name] = (roff, w)
        roff += w
    rseg['_total'] = roff
    return seg, rseg


def _cparams(sem):
    return pltpu.CompilerParams(dimension_semantics=sem, vmem_limit_bytes=VMEM_LIMIT)


def _pick(t, n):
    if n <= t:
        return n
    for align in (LANE, 2 * SUBLANE):
        for cand in range(t - t % align, 0, -align):
            if n % cand == 0:
                return cand
    return n


def _mm(a, b, *, name, ta=False, tb=False, add=None, out_dtype=F32, tm=1024, tn=1024, tk=512):
    M, K = (a.shape[1], a.shape[0]) if ta else a.shape
    N = b.shape[0] if tb else b.shape[1]
    assert (b.shape[1] if tb else b.shape[0]) == K
    tm, tn, tk = _pick(tm, M), _pick(tn, N), _pick(tk, K)
    nk = K // tk
    dn = (((0 if ta else 1,), (1 if tb else 0,)), ((), ()))
    has_add = add is not None

    def body(*refs):
        if has_add:
            a_ref, b_ref, add_ref, o_ref, acc = refs
        else:
            a_ref, b_ref, o_ref, acc = refs
        k = pl.program_id(2)

        @pl.when(k == 0)
        def _():
            acc[...] = jnp.zeros_like(acc)

        acc[...] += lax.dot_general(a_ref[...].astype(BF16), b_ref[...].astype(BF16), dn,
                                    preferred_element_type=F32)

        @pl.when(k == nk - 1)
        def _():
            r = acc[...]
            if has_add:
                r = r + add_ref[...]
            o_ref[...] = r.astype(o_ref.dtype)

    a_spec = pl.BlockSpec((tk, tm), lambda i, j, k: (k, i)) if ta else pl.BlockSpec((tm, tk), lambda i, j, k: (i, k))
    b_spec = pl.BlockSpec((tn, tk), lambda i, j, k: (j, k)) if tb else pl.BlockSpec((tk, tn), lambda i, j, k: (k, j))
    in_specs = [a_spec, b_spec]
    args = [a, b]
    if has_add:
        in_specs.append(pl.BlockSpec((tm, tn), lambda i, j, k: (i, j)))
        args.append(add)
    return pl.pallas_call(
        body, name=name, grid=(M // tm, N // tn, nk),
        in_specs=in_specs, out_specs=pl.BlockSpec((tm, tn), lambda i, j, k: (i, j)),
        out_shape=jax.ShapeDtypeStruct((M, N), out_dtype),
        scratch_shapes=[pltpu.VMEM((tm, tn), F32)],
        compiler_params=_cparams(("parallel", "parallel", "arbitrary")),
    )(*args)


def _row_spec(tm, w, cb):
    return pl.BlockSpec((tm, w), lambda i, cb=cb: (i, cb))


def _rowwise(fn, rows, params, outs, *, tm, name):
    L = rows[0][0].shape[0]
    tm = _pick(tm, L)
    nr, npar = len(rows), len(params)

    def body(*refs):
        vals = [r[...] for r in refs[:nr + npar]]
        res = fn(*vals)
        for o_ref, v in zip(refs[nr + npar:], res):
            o_ref[...] = v.astype(o_ref.dtype)

    in_specs = [_row_spec(tm, w, cb) for (_, w, cb) in rows]
    in_specs += [pl.BlockSpec(p.shape, lambda i: (0, 0)) for p in params]
    res = pl.pallas_call(
        body, name=name, grid=(L // tm,), in_specs=in_specs,
        out_specs=[pl.BlockSpec((tm, w), lambda i: (i, 0)) for (w, _) in outs],
        out_shape=[jax.ShapeDtypeStruct((L, w), dt) for (w, dt) in outs],
        compiler_params=_cparams(("parallel",)),
    )(*[r[0] for r in rows], *params)
    return list(res)


def _rowwise_bwd(fn, rows, params, cts, drows, dparams, *, tm, name, accs=None):
    L = rows[0][0].shape[0]
    tm = _pick(tm, L)
    nr, npar = len(rows), len(params)
    accs = accs or {}
    ct_flat = [c for grp in cts for c in grp]
    ct_sizes = [len(grp) for grp in cts]
    acc_keys = sorted(accs)
    d_r = [i for i in range(nr) if drows[i]]
    d_p = [i for i in range(npar) if dparams[i]]
    n_in = nr + npar + len(ct_flat) + len(acc_keys)

    def body(*refs):
        vals = [r[...] for r in refs[:nr + npar]]
        ct_refs = refs[nr + npar:nr + npar + len(ct_flat)]
        acc_refs = refs[nr + npar + len(ct_flat):n_in]
        o_refs = refs[n_in:]
        ct_vals, pos = [], 0
        for n in ct_sizes:
            v = ct_refs[pos][...].astype(F32)
            for r in ct_refs[pos + 1:pos + n]:
                v = v + r[...].astype(F32)
            ct_vals.append(v)
            pos += n
        diff_idx = d_r + [nr + i for i in d_p]

        def g(*dv):
            full = list(vals)
            for i, v in zip(diff_idx, dv):
                full[i] = v
            return tuple(o.astype(F32) for o in fn(*full))

        _, vjp = jax.vjp(g, *[vals[i] for i in diff_idx])
        grads = vjp(tuple(ct_vals))
        for n, i in enumerate(d_r):
            gv = grads[n].astype(F32)
            if i in accs:
                gv = gv + acc_refs[acc_keys.index(i)][...]
            o_refs[n][...] = gv
        step = pl.program_id(0)
        for n, i in enumerate(d_p):
            o_ref = o_refs[len(d_r) + n]

            @pl.when(step == 0)
            def _(o_ref=o_ref):
                o_ref[...] = jnp.zeros_like(o_ref)

            o_ref[...] += grads[len(d_r) + n].astype(F32)

    in_specs = [_row_spec(tm, w, cb) for (_, w, cb) in rows]
    in_specs += [pl.BlockSpec(p.shape, lambda i: (0, 0)) for p in params]
    in_specs += [_row_spec(tm, w, cb) for (_, w, cb) in ct_flat]
    in_specs += [_row_spec(tm, accs[k][1], accs[k][2]) for k in acc_keys]
    out_specs = [pl.BlockSpec((tm, rows[i][1]), lambda i_: (i_, 0)) for i in d_r]
    out_specs += [pl.BlockSpec(params[i].shape, lambda i_: (0, 0)) for i in d_p]
    out_shape = [jax.ShapeDtypeStruct((L, rows[i][1]), F32) for i in d_r]
    out_shape += [jax.ShapeDtypeStruct(params[i].shape, F32) for i in d_p]
    res = pl.pallas_call(
        body, name=name, grid=(L // tm,), in_specs=in_specs, out_specs=out_specs, out_shape=out_shape,
        compiler_params=_cparams(("arbitrary",)),
    )(*[r[0] for r in rows], *params, *[c[0] for c in ct_flat], *[accs[k][0] for k in acc_keys])
    res = list(res)
    return res[:len(d_r)], res[len(d_r):]


def _silu(x):
    return x * jax.nn.sigmoid(x)


def _rms(x, g):
    return x * lax.rsqrt(jnp.mean(x * x, axis=-1, keepdims=True) + EPS) * g


def _softplus(x):
    return jnp.maximum(x, 0.0) + jnp.log1p(jnp.exp(-jnp.abs(x)))


def _heads(x, hd):
    return [x[:, i * hd:(i + 1) * hd] for i in range(x.shape[1] // hd)]


def _f_norm(x, g):
    return (_rms(x, g),)


def _f_s5tail(ys, u, z, d, wglu, bglu):
    y = jax.nn.gelu(ys + d * u)
    gate = jax.nn.sigmoid(jnp.dot(y.astype(BF16), wglu.astype(BF16), preferred_element_type=F32) + bglu)
    return (y * gate * _silu(z),)


def _make_f_dnpre(nh, hd, chunk):
    def f(qc, kc, dadb, alog, dtb):
        tm = qc.shape[0]
        qn = [q * lax.rsqrt(jnp.sum(q * q, axis=-1, keepdims=True) + EPS) * (hd ** -0.5) for q in _heads(qc, hd)]
        kn = [k * lax.rsqrt(jnp.sum(k * k, axis=-1, keepdims=True) + EPS) for k in _heads(kc, hd)]
        g = -jnp.exp(alog) * _softplus(dadb + dtb)
        beta = jax.nn.sigmoid(dadb)
        ii = lax.broadcasted_iota(jnp.int32, (tm, tm), 0)
        jj = lax.broadcasted_iota(jnp.int32, (tm, tm), 1)
        same = (ii // chunk) == (jj // chunk)
        outs = [jnp.concatenate(qn, axis=1), jnp.concatenate(kn, axis=1)]
        gt = jnp.dot(same.astype(F32), g, precision=HI, preferred_element_type=F32)
        for dr in range(2):
            tri = jnp.logical_and(same, (ii <= jj) if dr else (ii >= jj)).astype(F32)
            gc = jnp.dot(tri, g, precision=HI, preferred_element_type=F32)

            def spread(t, lane0):
                return jnp.concatenate([jnp.broadcast_to(t[:, lane0 + h:lane0 + h + 1], (tm, hd))
                                        for h in range(nh)], axis=1)

            outs += [spread(beta, 2 * nh + dr * nh), spread(gc, dr * nh), spread(gt, dr * nh)]
        return tuple(outs)
    return f


def _make_f_dnpost(hd):
    def f(o, z, ng):
        y = [_rms(oh, ng) for oh in _heads(o, hd)]
        return (jnp.concatenate(y, axis=1) * _silu(z),)
    return f


def _make_f_attpre(hd, with_v):
    def rope(x, g, cosf, sins, perm, scale):
        xn = _rms(x, g)
        xs = jnp.dot(xn, perm, precision=HI, preferred_element_type=F32)
        return (xn * cosf + xs * sins) * scale

    def f(aq, ak, *rest):
        if with_v:
            av, cosf, sins, perm, qg, kg = rest
        else:
            cosf, sins, perm, qg, kg = rest
        qh = jnp.concatenate([rope(x, qg, cosf, sins, perm, hd ** -0.5) for x in _heads(aq, hd)], axis=1)
        kh = jnp.concatenate([rope(x, kg, cosf, sins, perm, 1.0) for x in _heads(ak, hd)], axis=1)
        return (qh, kh, av) if with_v else (qh, kh)
    return f


def _f_gate(o, z):
    return (o * _silu(z),)


def _make_f_mem(nh, hd):
    def f(mq, z, kv):
        mw = nh * hd
        outs = []
        for h, q in enumerate(_heads(mq, hd)):
            k = kv[:, h * hd:(h + 1) * hd]
            v = kv[:, mw + h * hd:mw + (h + 1) * hd]
            s = lax.dot_general(q.astype(BF16), k.astype(BF16), (((1,), (1,)), ((), ())),
                                preferred_element_type=F32) * (hd ** -0.5)
            s = s - jnp.max(s, axis=-1, keepdims=True)
            p = jnp.exp(s)
            p = p / jnp.sum(p, axis=-1, keepdims=True)
            outs.append(jnp.dot(p.astype(BF16), v.astype(BF16), preferred_element_type=F32))
        return (jnp.concatenate(outs, axis=1) * _silu(z),)
    return f


def _f_merge(p0, p1, p2, p3, g0, g1, g2, g3):
    return (jax.nn.sigmoid(g0) * p0 + jax.nn.sigmoid(g1) * p1 + jax.nn.sigmoid(g2) * p2 + jax.nn.sigmoid(g3) * p3,)


def _make_f_delta(hd):
    def f(do, o):
        out = [jnp.broadcast_to(jnp.sum(a * b, axis=-1, keepdims=True), a.shape)
               for a, b in zip(_heads(do, hd), _heads(o, hd))]
        return (jnp.concatenate(out, axis=1),)
    return f


def _s5_prep(a_re, a_im, log_step, b_re, b_im, c_re, c_im, d):
    nb, gpb, sn, sp = d['NB'], d['GPB'], d['SN'], d['SP']
    step = jnp.exp(log_step)[:, None]
    mag = jnp.exp(a_re * step)
    lam_re = mag * jnp.cos(a_im * step)
    lam_im = mag * jnp.sin(a_im * step)
    den = a_re * a_re + a_im * a_im
    nr, ni = lam_re - 1.0, lam_im
    coef_re = (nr * a_re + ni * a_im) / den
    coef_im = (ni * a_re - nr * a_im) / den
    bb_re = coef_re[..., None] * b_re - coef_im[..., None] * b_im
    bb_im = coef_re[..., None] * b_im + coef_im[..., None] * b_re
    eye = jnp.eye(gpb, dtype=F32)

    def blk_in(bb):
        t = bb.reshape(nb, gpb, sn, sp)
        return jnp.einsum("jgnp,gh->jgphn", t, eye).reshape(nb, gpb * sp, gpb * sn)

    def blk_out(cc):
        t = cc.reshape(nb, gpb, sp, sn)
        return jnp.einsum("jgpn,gh->jgnhp", t, eye).reshape(nb, gpb * sn, gpb * sp)

    wb = jnp.concatenate([blk_in(bb_re), blk_in(bb_im)], axis=2)
    wc = jnp.concatenate([blk_out(c_re), blk_out(-c_im)], axis=1)
    return wb, wc, lam_re.reshape(nb, gpb * sn), lam_im.reshape(nb, gpb * sn)


def _s5_tables(lam_re, lam_im, rev, conj):
    lr, li = lam_re, (-lam_im if conj else lam_im)

    def cmul(a, b):
        return a[0] * b[0] - a[1] * b[1], a[0] * b[1] + a[1] * b[0]

    pw = [(lr, li)]
    for _ in range(7):
        pw.append(cmul(pw[-1], (lr, li)))
    rows = jnp.arange(8)

    def bc(t, k):
        keep = (rows < 8 - k) if rev else (rows >= k)
        return t[:, None, :] * keep.astype(F32)[None, :, None]

    order = list(range(8))[::-1] if rev else list(range(8))
    pwr = jnp.stack([pw[i][0] for i in order], axis=1)
    pwi = jnp.stack([pw[i][1] for i in order], axis=1)
    tabs = [bc(pw[0][0], 1), bc(pw[0][1], 1), bc(pw[1][0], 2), bc(pw[1][1], 2), bc(pw[3][0], 4), bc(pw[3][1], 4),
            pwr, pwi]
    return jnp.stack(tabs, axis=1)


def _scan_group(xr, xi, lt_ref, j, cr, ci, rev):
    for lvl, k in enumerate((1, 2, 4)):
        l_r, l_i = lt_ref[j, 2 * lvl], lt_ref[j, 2 * lvl + 1]
        sh = (8 - k) if rev else k
        sr, si = pltpu.roll(xr, sh, 0), pltpu.roll(xi, sh, 0)
        xr, xi = xr + l_r * sr - l_i * si, xi + l_r * si + l_i * sr
    p_r, p_i = lt_ref[j, 6], lt_ref[j, 7]
    return xr + p_r * cr - p_i * ci, xi + p_r * ci + p_i * cr


def _last_row(x, rev):
    last = 0 if rev else 7
    return jnp.broadcast_to(x[last:last + 1, :], x.shape)


def _s5_fwd(hsrc, ucb, wb, wc, lt, *, rev, acc, tb, name, d):
    L, SW, NB, BS = hsrc.shape[0], d['SW'], d['NB'], d['BS']
    tb = _pick(tb, L)
    nblk, ngr = L // tb, tb // 8
    tix = (lambda b: nblk - 1 - b) if rev else (lambda b: b)
    has_acc = acc is not None

    def body(*refs):
        if has_acc:
            u_ref, wb_ref, wc_ref, lt_ref, acc_ref, y_ref, cin_ref, bu_s, car = refs
        else:
            u_ref, wb_ref, wc_ref, lt_ref, y_ref, cin_ref, bu_s, car = refs

        @pl.when(pl.program_id(0) == 0)
        def _():
            car[...] = jnp.zeros_like(car)

        cin_ref[...] = car[...]
        for j in range(NB):
            bu_s[:, j * 2 * BS:(j + 1) * 2 * BS] = jnp.dot(
                u_ref[:, j * LANE:(j + 1) * LANE].astype(BF16), wb_ref[j], preferred_element_type=F32)

        def grp(r, _):
            base = pl.multiple_of((ngr - 1 - r if rev else r) * 8, 8)
            for j in range(NB):
                c0 = j * 2 * BS
                xr, xi = _scan_group(bu_s[pl.ds(base, 8), c0:c0 + BS], bu_s[pl.ds(base, 8), c0 + BS:c0 + 2 * BS],
                                     lt_ref, j, car[:, c0:c0 + BS], car[:, c0 + BS:c0 + 2 * BS], rev)
                bu_s[pl.ds(base, 8), c0:c0 + BS] = xr
                bu_s[pl.ds(base, 8), c0 + BS:c0 + 2 * BS] = xi
                car[:, c0:c0 + BS] = _last_row(xr, rev)
                car[:, c0 + BS:c0 + 2 * BS] = _last_row(xi, rev)
            return 0

        lax.fori_loop(0, ngr, grp, 0)
        for j in range(NB):
            y = jnp.dot(bu_s[:, j * 2 * BS:(j + 1) * 2 * BS].astype(BF16), wc_ref[j], preferred_element_type=F32)
            if has_acc:
                y = y + acc_ref[:, j * LANE:(j + 1) * LANE]
            y_ref[:, j * LANE:(j + 1) * LANE] = y

    in_specs = [pl.BlockSpec((tb, SW), lambda b: (tix(b), ucb)),
                pl.BlockSpec(wb.shape, lambda b: (0, 0, 0)), pl.BlockSpec(wc.shape, lambda b: (0, 0, 0)),
                pl.BlockSpec(lt.shape, lambda b: (0, 0, 0, 0))]
    args = [hsrc, wb, wc, lt]
    if has_acc:
        in_specs.append(pl.BlockSpec((tb, SW), lambda b: (tix(b), 0)))
        args.append(acc)
    y, cin = pl.pallas_call(
        body, name=name, grid=(nblk,), in_specs=in_specs,
        out_specs=[pl.BlockSpec((tb, SW), lambda b: (tix(b), 0)),
                   pl.BlockSpec((8, NB * 2 * BS), lambda b: (tix(b), 0))],
        out_shape=[jax.ShapeDtypeStruct((L, SW), F32), jax.ShapeDtypeStruct((nblk * 8, NB * 2 * BS), F32)],
        scratch_shapes=[pltpu.VMEM((tb, NB * 2 * BS), F32), pltpu.VMEM((8, NB * 2 * BS), F32)],
        compiler_params=_cparams(("arbitrary",)),
    )(*args)
    return y, cin


def _s5_bwd(hsrc, ucb, dy, cin, wb, wc, lt, lt_adj, *, rev, acc, tb, name, d):
    L, SW, NB, BS = hsrc.shape[0], d['SW'], d['NB'], d['BS']
    tb = _pick(tb, L)
    nblk, ngr = L // tb, tb // 8
    arev = not rev
    tix = (lambda b: nblk - 1 - b) if arev else (lambda b: b)
    has_acc = acc is not None
    NT = (((1,), (1,)), ((), ()))
    TN = (((0,), (0,)), ((), ()))

    def body(*refs):
        if has_acc:
            (u_ref, dy_ref, cin_ref, wb_ref, wc_ref, lt_ref, la_ref, acc_ref,
             du_ref, dwb_ref, dwc_ref, dlam_ref, s_s, g_s, car, acar) = refs
        else:
            (u_ref, dy_ref, cin_ref, wb_ref, wc_ref, lt_ref, la_ref,
             du_ref, dwb_ref, dwc_ref, dlam_ref, s_s, g_s, car, acar) = refs

        @pl.when(pl.program_id(0) == 0)
        def _():
            acar[...] = jnp.zeros_like(acar)
            dwb_ref[...] = jnp.zeros_like(dwb_ref)
            dwc_ref[...] = jnp.zeros_like(dwc_ref)
            dlam_ref[...] = jnp.zeros_like(dlam_ref)

        car[...] = cin_ref[...]
        for j in range(NB):
            s_s[:, j * 2 * BS:(j + 1) * 2 * BS] = jnp.dot(
                u_ref[:, j * LANE:(j + 1) * LANE].astype(BF16), wb_ref[j], preferred_element_type=F32)
            g_s[:, j * 2 * BS:(j + 1) * 2 * BS] = lax.dot_general(
                dy_ref[:, j * LANE:(j + 1) * LANE].astype(BF16), wc_ref[j], NT, preferred_element_type=F32)

        def fgrp(r, _):
            base = pl.multiple_of((ngr - 1 - r if rev else r) * 8, 8)
            for j in range(NB):
                c0 = j * 2 * BS
                xr, xi = _scan_group(s_s[pl.ds(base, 8), c0:c0 + BS], s_s[pl.ds(base, 8), c0 + BS:c0 + 2 * BS],
                                     lt_ref, j, car[:, c0:c0 + BS], car[:, c0 + BS:c0 + 2 * BS], rev)
                s_s[pl.ds(base, 8), c0:c0 + BS] = xr
                s_s[pl.ds(base, 8), c0 + BS:c0 + 2 * BS] = xi
                car[:, c0:c0 + BS] = _last_row(xr, rev)
                car[:, c0 + BS:c0 + 2 * BS] = _last_row(xi, rev)
            return 0

        lax.fori_loop(0, ngr, fgrp, 0)

        row = lax.broadcasted_iota(jnp.int32, (8, BS), 0)

        def agrp(r, _):
            gi = ngr - 1 - r if arev else r
            base = pl.multiple_of(gi * 8, 8)
            pgi = gi + 1 if rev else gi - 1
            inside = jnp.logical_and(pgi >= 0, pgi < ngr)
            pbase = pl.multiple_of(jnp.clip(pgi, 0, ngr - 1) * 8, 8)
            for j in range(NB):
                c0 = j * 2 * BS
                ar, ai = _scan_group(g_s[pl.ds(base, 8), c0:c0 + BS], g_s[pl.ds(base, 8), c0 + BS:c0 + 2 * BS],
                                     la_ref, j, acar[:, c0:c0 + BS], acar[:, c0 + BS:c0 + 2 * BS], arev)
                g_s[pl.ds(base, 8), c0:c0 + BS] = ar
                g_s[pl.ds(base, 8), c0 + BS:c0 + 2 * BS] = ai
                acar[:, c0:c0 + BS] = _last_row(ar, arev)
                acar[:, c0 + BS:c0 + 2 * BS] = _last_row(ai, arev)
                sr, si = s_s[pl.ds(base, 8), c0:c0 + BS], s_s[pl.ds(base, 8), c0 + BS:c0 + 2 * BS]
                edge_r = jnp.where(inside, _last_row(s_s[pl.ds(pbase, 8), c0:c0 + BS], rev), cin_ref[:, c0:c0 + BS])
                edge_i = jnp.where(inside, _last_row(s_s[pl.ds(pbase, 8), c0 + BS:c0 + 2 * BS], rev),
                                   cin_ref[:, c0 + BS:c0 + 2 * BS])
                sh = 7 if rev else 1
                first = 7 if rev else 0
                pr = jnp.where(row == first, edge_r, pltpu.roll(sr, sh, 0))
                pi = jnp.where(row == first, edge_i, pltpu.roll(si, sh, 0))
                dlam_ref[:, c0:c0 + BS] += ar * pr + ai * pi
                dlam_ref[:, c0 + BS:c0 + 2 * BS] += ai * pr - ar * pi
            return 0

        lax.fori_loop(0, ngr, agrp, 0)
        for j in range(NB):
            a_j = g_s[:, j * 2 * BS:(j + 1) * 2 * BS].astype(BF16)
            u_j = u_ref[:, j * LANE:(j + 1) * LANE].astype(BF16)
            du = lax.dot_general(a_j, wb_ref[j], NT, preferred_element_type=F32)
            if has_acc:
                du = du + acc_ref[:, j * LANE:(j + 1) * LANE]
            du_ref[:, j * LANE:(j + 1) * LANE] = du
            dwb_ref[j] += lax.dot_general(u_j, a_j, TN, preferred_element_type=F32)
            dwc_ref[j] += lax.dot_general(s_s[:, j * 2 * BS:(j + 1) * 2 * BS].astype(BF16),
                                          dy_ref[:, j * LANE:(j + 1) * LANE].astype(BF16), TN,
                                          preferred_element_type=F32)

    W2 = NB * 2 * BS
    in_specs = [pl.BlockSpec((tb, SW), lambda b: (tix(b), ucb)), pl.BlockSpec((tb, SW), lambda b: (tix(b), 0)),
                pl.BlockSpec((8, W2), lambda b: (tix(b), 0)),
                pl.BlockSpec(wb.shape, lambda b: (0, 0, 0)), pl.BlockSpec(wc.shape, lambda b: (0, 0, 0)),
                pl.BlockSpec(lt.shape, lambda b: (0, 0, 0, 0)), pl.BlockSpec(lt_adj.shape, lambda b: (0, 0, 0, 0))]
    args = [hsrc, dy, cin, wb, wc, lt, lt_adj]
    if has_acc:
        in_specs.append(pl.BlockSpec((tb, SW), lambda b: (tix(b), 0)))
        args.append(acc)
    return pl.pallas_call(
        body, name=name, grid=(nblk,), in_specs=in_specs,
        out_specs=[pl.BlockSpec((tb, SW), lambda b: (tix(b), 0)),
                   pl.BlockSpec(wb.shape, lambda b: (0, 0, 0)), pl.BlockSpec(wc.shape, lambda b: (0, 0, 0)),
                   pl.BlockSpec((8, W2), lambda b: (0, 0))],
        out_shape=[jax.ShapeDtypeStruct((L, SW), F32), jax.ShapeDtypeStruct(wb.shape, F32),
                   jax.ShapeDtypeStruct(wc.shape, F32), jax.ShapeDtypeStruct((8, W2), F32)],
        scratch_shapes=[pltpu.VMEM((tb, W2), F32), pltpu.VMEM((tb, W2), F32),
                        pltpu.VMEM((8, W2), F32), pltpu.VMEM((8, W2), F32)],
        compiler_params=_cparams(("arbitrary",)),
    )(*args)


_NN = (((1,), (0,)), ((), ()))
_NT = (((1,), (1,)), ((), ()))
_TN = (((0,), (0,)), ((), ()))


def _dotb(a, b, dn=_NN):
    return lax.dot_general(a.astype(BF16), b.astype(BF16), dn, preferred_element_type=F32)


def _split(x):
    hi = x.astype(BF16)
    return hi, (x - hi.astype(F32)).astype(BF16)


def _dot3(a, b, dn=_NN):
    ah, al = _split(a)
    bh, bl = _split(b)
    f = lambda x, y: lax.dot_general(x, y, dn, preferred_element_type=F32)
    return f(ah, bh) + (f(ah, bl) + f(al, bh))


@jax.custom_vjp
def _dot3_nn(a, b):
    return _dot3(a, b, _NN)


_dot3_nn.defvjp(lambda a, b: (_dot3(a, b, _NN), (a, b)),
                lambda res, g: (_dotb(g, res[1], _NT), _dotb(res[0], g, _TN)))


@jax.custom_vjp
def _dot3_nt(a, b):
    return _dot3(a, b, _NT)


_dot3_nt.defvjp(lambda a, b: (_dot3(a, b, _NT), (a, b)),
                lambda res, g: (_dotb(g, res[1], _NN), _dotb(g, res[0], _TN)))


def _delta_chunk(rev, one_pass_grads, *flat):
    heads = [flat[i:i + 7] for i in range(0, len(flat), 7)]
    q, k, v, beta, gc, gt, s_in = [list(t) for t in zip(*heads)]
    c, hd = q[0].shape
    each = lambda f, *ls: [f(*t) for t in zip(*ls)]
    mm_nn = _dot3_nn if one_pass_grads else _dot3
    mm_nt = _dot3_nt if one_pass_grads else (lambda x, y: _dot3(x, y, _NT))
    ii = lax.broadcasted_iota(jnp.int32, (c, c), 0)
    jj = lax.broadcasted_iota(jnp.int32, (c, c), 1)
    incl = (ii <= jj) if rev else (ii >= jj)
    strict = (ii < jj) if rev else (ii > jj)
    eye = (ii == jj).astype(F32)
    decay = each(lambda g: jnp.where(incl, jnp.exp(jnp.where(incl, g[:, :c] - jnp.transpose(g)[:c, :], 0.0)), 0.0), gc)
    kb = each(lambda a, b: a * b, k, beta)
    a = each(lambda x, y, dc: jnp.where(strict, mm_nt(x, y) * dc, 0.0), kb, k, decay)
    tinv = each(lambda x: eye - x, a)
    p = a
    n = 2
    while n < c:
        p = each(lambda x: mm_nn(x, x), p)
        tinv = each(lambda t, x: mm_nn(t, eye + x), tinv, p)
        n *= 2
    eg = each(jnp.exp, gc)
    u = each(lambda t, x, b: mm_nn(t, x * b), tinv, v, beta)
    w = each(lambda t, x, e: mm_nn(t, x * e), tinv, kb, eg)
    intra = each(lambda x, y, dc: _dotb(x, y, _NT) * dc, q, k, decay)
    v_new = each(lambda x, y, s: x - _dotb(y, s), u, w, s_in)
    o = each(lambda x, e, s, m, vn: _dotb(x * e, s) + _dotb(m, vn), q, eg, s_in, intra, v_new)
    s_out = each(lambda s, t, x, g, vn: s * jnp.exp(jnp.broadcast_to(t[0:1, :], (hd, hd)))
                 + _dotb(x * jnp.exp(t - g), vn, _TN), s_in, gt, k, gc, v_new)
    return tuple(x for pair in zip(o, s_out) for x in pair)


def _delta_fwd(q, k, v, gates, *, vcb, rev, acc, name, d):
    L, H, hd, C = q.shape[0], d['DNH'], d['DNK'], d['CHUNK']
    nc = L // C
    cix = (lambda i: nc - 1 - i) if rev else (lambda i: i)
    has_acc = acc is not None

    def body(*refs):
        if has_acc:
            q_ref, k_ref, v_ref, b_ref, gc_ref, gt_ref, acc_ref, o_ref, ss_ref, st = refs
        else:
            q_ref, k_ref, v_ref, b_ref, gc_ref, gt_ref, o_ref, ss_ref, st = refs

        @pl.when(pl.program_id(0) == 0)
        def _():
            st[...] = jnp.zeros_like(st)

        sls = [slice(h * hd, (h + 1) * hd) for h in range(H)]
        ins = [(q_ref[:, sl], k_ref[:, sl], v_ref[:, sl], b_ref[:, sl], gc_ref[:, sl], gt_ref[:, sl], st[h])
               for h, sl in enumerate(sls)]
        accv = [acc_ref[:, sl] for sl in sls] if has_acc else None
        res = _delta_chunk(rev, False, *[t for head in ins for t in head])
        for h, sl in enumerate(sls):
            o, s_out = res[2 * h], res[2 * h + 1]
            ss_ref[0, h] = ins[h][6]
            o_ref[:, sl] = o + accv[h] if has_acc else o
            st[h] = s_out

    blk = pl.BlockSpec((C, H * hd), lambda i: (cix(i), 0))
    in_specs = [blk, blk, pl.BlockSpec((C, H * hd), lambda i: (cix(i), vcb)), blk, blk, blk]
    args = [q, k, v, *gates]
    if has_acc:
        in_specs.append(blk)
        args.append(acc)
    return pl.pallas_call(
        body, name=name, grid=(nc,), in_specs=in_specs,
        out_specs=[blk, pl.BlockSpec((1, H, hd, hd), lambda i: (cix(i), 0, 0, 0))],
        out_shape=[jax.ShapeDtypeStruct((L, H * hd), F32), jax.ShapeDtypeStruct((nc, H, hd, hd), F32)],
        scratch_shapes=[pltpu.VMEM((H, hd, hd), F32)],
        compiler_params=_cparams(("arbitrary",)),
    )(*args)


def _delta_bwd(q, k, v, gates, ssave, do, *, vcb, rev, accs, name, d):
    L, H, hd, C = q.shape[0], d['DNH'], d['DNK'], d['CHUNK']
    nc = L // C
    cix = (lambda i: i) if rev else (lambda i: nc - 1 - i)
    has_acc = accs is not None

    def body(*refs):
        if has_acc:
            (q_ref, k_ref, v_ref, b_ref, gc_ref, gt_ref, ss_ref, do_ref, aq_ref, ak_ref, av_ref,
             dq_ref, dk_ref, dv_ref, db_ref, dgc_ref, dgt_ref, dst) = refs
        else:
            (q_ref, k_ref, v_ref, b_ref, gc_ref, gt_ref, ss_ref, do_ref,
             dq_ref, dk_ref, dv_ref, db_ref, dgc_ref, dgt_ref, dst) = refs

        @pl.when(pl.program_id(0) == 0)
        def _():
            dst[...] = jnp.zeros_like(dst)

        sls = [slice(h * hd, (h + 1) * hd) for h in range(H)]
        ins = [(q_ref[:, sl], k_ref[:, sl], v_ref[:, sl], b_ref[:, sl], gc_ref[:, sl], gt_ref[:, sl], ss_ref[0, h])
               for h, sl in enumerate(sls)]
        cts = tuple(t for h, sl in enumerate(sls) for t in (do_ref[:, sl], dst[h]))
        accv = [(aq_ref[:, sl], ak_ref[:, sl], av_ref[:, sl]) for sl in sls] if has_acc else None
        _, vjp = jax.vjp(functools.partial(_delta_chunk, rev, True), *[t for head in ins for t in head])
        res = vjp(cts)
        for h, sl in enumerate(sls):
            dq, dk, dv, db, dgc, dgt, ds = res[7 * h:7 * h + 7]
            dst[h] = ds
            if has_acc:
                dq, dk, dv = dq + accv[h][0], dk + accv[h][1], dv + accv[h][2]
            dq_ref[:, sl] = dq
            dk_ref[:, sl] = dk
            dv_ref[:, sl] = dv
            db_ref[:, sl] = db
            dgc_ref[:, sl] = dgc
            dgt_ref[:, sl] = dgt

    blk = pl.BlockSpec((C, H * hd), lambda i: (cix(i), 0))
    in_specs = [blk, blk, pl.BlockSpec((C, H * hd), lambda i: (cix(i), vcb)), blk, blk, blk,
                pl.BlockSpec((1, H, hd, hd), lambda i: (cix(i), 0, 0, 0)), blk]
    args = [q, k, v, *gates, ssave, do]
    if has_acc:
        in_specs += [blk, blk, blk]
        args += list(accs)
    return pl.pallas_call(
        body, name=name, grid=(nc,), in_specs=in_specs, out_specs=[blk] * 6,
        out_shape=[jax.ShapeDtypeStruct((L, H * hd), F32)] * 6,
        scratch_shapes=[pltpu.VMEM((H, hd, hd), F32)],
        compiler_params=_cparams(("arbitrary",)),
    )(*args)


def _conv_specs(tm, w, cb0, nrb, L):
    hb = tm // 8
    last8 = L // 8 - 1
    cur = pl.BlockSpec((tm, w), lambda s, i: (i, cb0 + s))
    prev = pl.BlockSpec((8, w), lambda s, i: (jnp.maximum(i * hb - 1, 0), cb0 + s))
    nxt = pl.BlockSpec((8, w), lambda s, i: (jnp.minimum((i + 1) * hb, last8), cb0 + s))
    return [prev, cur, nxt]


def _fill_halo(dst, prev_ref, cur_ref, next_ref, i, nrb, tm):
    dst[pl.ds(0, 8), :] = jnp.where(i > 0, prev_ref[...], 0.0)
    dst[pl.ds(8, tm), :] = cur_ref[...]
    dst[pl.ds(8 + tm, 8), :] = jnp.where(i < nrb - 1, next_ref[...], 0.0)


def _conv_fwd(hsrc, cb0, wt, *, tm, name, d):
    L, w, K = hsrc.shape[0], d['DW'], d['CONV']
    tm = _pick(tm, L)
    nrb = L // tm

    def body(prev_ref, cur_ref, next_ref, w_ref, o_ref, xs):
        i = pl.program_id(1)
        _fill_halo(xs, prev_ref, cur_ref, next_ref, i, nrb, tm)
        y = jnp.zeros((tm, w), F32)
        for kk in range(K):
            y = y + w_ref[0, pl.ds(kk, 1), :] * xs[pl.ds(8 - K // 2 + kk, tm), :]
        o_ref[...] = _silu(y)

    return pl.pallas_call(
        body, name=name, grid=(3, nrb),
        in_specs=_conv_specs(tm, w, cb0, nrb, L) + [pl.BlockSpec((1, 8, w), lambda s, i: (s, 0, 0))],
        out_specs=pl.BlockSpec((tm, w), lambda s, i: (i, s)),
        out_shape=jax.ShapeDtypeStruct((L, 3 * w), F32),
        scratch_shapes=[pltpu.VMEM((tm + 16, w), F32)],
        compiler_params=_cparams(("parallel", "parallel")),
    )(hsrc, hsrc, hsrc, wt)


def _conv_bwd(hsrc, cb0, wt, dact, *, tm, name, d):
    L, w, K = hsrc.shape[0], d['DW'], d['CONV']
    tm = _pick(tm, L)
    nrb = L // tm
    half = K // 2

    def body(xp_ref, xc_ref, xn_ref, gp_ref, gc_ref, gn_ref, w_ref, dx_ref, dw_ref, xs, gs, dys):
        i = pl.program_id(1)
        _fill_halo(xs, xp_ref, xc_ref, xn_ref, i, nrb, tm)
        _fill_halo(gs, gp_ref, gc_ref, gn_ref, i, nrb, tm)
        y = jnp.zeros((tm + 8, w), F32)
        for kk in range(K):
            y = y + w_ref[0, pl.ds(kk, 1), :] * xs[pl.ds(4 - half + kk, tm + 8), :]
        sg = jax.nn.sigmoid(y)
        dys[...] = gs[pl.ds(4, tm + 8), :] * (sg * (1.0 + y * (1.0 - sg)))
        dx = jnp.zeros((tm, w), F32)
        for kk in range(K):
            dx = dx + w_ref[0, pl.ds(kk, 1), :] * dys[pl.ds(4 + half - kk, tm), :]
        dx_ref[...] = dx

        @pl.when(i == 0)
        def _():
            dw_ref[...] = jnp.zeros_like(dw_ref)

        dy = dys[pl.ds(4, tm), :]
        for kk in range(K):
            dw_ref[0, pl.ds(kk, 1), :] += jnp.sum(dy * xs[pl.ds(8 - half + kk, tm), :], axis=0, keepdims=True)

    gspecs = _conv_specs(tm, w, 0, nrb, L)
    return pl.pallas_call(
        body, name=name, grid=(3, nrb),
        in_specs=_conv_specs(tm, w, cb0, nrb, L) + gspecs + [pl.BlockSpec((1, 8, w), lambda s, i: (s, 0, 0))],
        out_specs=[pl.BlockSpec((tm, w), lambda s, i: (i, s)), pl.BlockSpec((1, 8, w), lambda s, i: (s, 0, 0))],
        out_shape=[jax.ShapeDtypeStruct((L, 3 * w), F32), jax.ShapeDtypeStruct((3, 8, w), F32)],
        scratch_shapes=[pltpu.VMEM((tm + 16, w), F32), pltpu.VMEM((tm + 16, w), F32), pltpu.VMEM((tm + 8, w), F32)],
        compiler_params=_cparams(("parallel", "arbitrary")),
    )(hsrc, hsrc, hsrc, dact, dact, dact, wt)


def _wide(v, n):
    return v if n == LANE else jnp.tile(v, (1, n // LANE))


def _attn_fwd(qh, kh, vh, *, tq, tk, name, d):
    L, H, KVH, hd = qh.shape[0], d['AH'], d['AKV'], d['AD']
    grp = H // KVH
    tq, tk = _pick(tq, L), _pick(tk, L)
    nk = L // tk

    def body(q_ref, k_ref, v_ref, o_ref, lse_ref, m_s, l_s, acc):
        j = pl.program_id(2)

        @pl.when(j == 0)
        def _():
            m_s[...] = jnp.full_like(m_s, -1e30)
            l_s[...] = jnp.zeros_like(l_s)
            acc[...] = jnp.zeros_like(acc)

        s = lax.dot_general(q_ref[...], k_ref[...], _NT, preferred_element_type=F32)
        m_old = m_s[...]
        m_new = jnp.maximum(m_old, jnp.max(s, axis=-1, keepdims=True))
        alpha = jnp.exp(m_old - m_new)
        p = jnp.exp(s - _wide(m_new, tk))
        l_s[...] = alpha * l_s[...] + jnp.sum(p, axis=-1, keepdims=True)
        acc[...] = alpha * acc[...] + jnp.dot(p.astype(BF16), v_ref[...], preferred_element_type=F32)
        m_s[...] = m_new

        @pl.when(j == nk - 1)
        def _():
            o_ref[...] = acc[...] / l_s[...]
            lse_ref[...] = m_s[...] + jnp.log(l_s[...])

    qspec = pl.BlockSpec((tq, hd), lambda h, i, j: (i, h))
    kspec = pl.BlockSpec((tk, hd), lambda h, i, j: (j, h // grp))
    return pl.pallas_call(
        body, name=name, grid=(H, L // tq, nk), in_specs=[qspec, kspec, kspec], out_specs=[qspec, qspec],
        out_shape=[jax.ShapeDtypeStruct((L, H * hd), F32), jax.ShapeDtypeStruct((L, H * hd), F32)],
        scratch_shapes=[pltpu.VMEM((tq, hd), F32), pltpu.VMEM((tq, hd), F32), pltpu.VMEM((tq, hd), F32)],
        compiler_params=_cparams(("parallel", "parallel", "arbitrary")),
    )(qh, kh, vh)


def _attn_bwd(qh, kh, vh, do, lse, delta, *, tq, tk, name, d):
    L, H, KVH, hd = qh.shape[0], d['AH'], d['AKV'], d['AD']
    grp = H // KVH
    tq, tk = _pick(tq, L), _pick(tk, L)
    nk = L // tk

    def body(q_ref, k_ref, v_ref, do_ref, lse_ref, dl_ref, dq_ref, dk_ref, dv_ref, dq_s):
        g, i, j = pl.program_id(1), pl.program_id(2), pl.program_id(3)

        @pl.when(jnp.logical_and(jnp.logical_and(g == 0, i == 0), j == 0))
        def _():
            dk_ref[...] = jnp.zeros_like(dk_ref)
            dv_ref[...] = jnp.zeros_like(dv_ref)

        @pl.when(j == 0)
        def _():
            dq_s[...] = jnp.zeros_like(dq_s)

        q, k, do_ = q_ref[...], k_ref[...], do_ref[...].astype(BF16)
        s = lax.dot_general(q, k, _NT, preferred_element_type=F32)
        p = jnp.exp(s - _wide(lse_ref[...], tk))
        dp = lax.dot_general(do_, v_ref[...], _NT, preferred_element_type=F32)
        ds = (p * (dp - _wide(dl_ref[...], tk))).astype(BF16)
        dq_s[...] += jnp.dot(ds, k, preferred_element_type=F32)
        rows = pl.ds(pl.multiple_of(j * tk, tk), tk)
        dv_ref[rows, :] += lax.dot_general(p.astype(BF16), do_, _TN, preferred_element_type=F32)
        dk_ref[rows, :] += lax.dot_general(ds, q, _TN, preferred_element_type=F32)

        @pl.when(j == nk - 1)
        def _():
            dq_ref[...] = dq_s[...]

    qspec = pl.BlockSpec((tq, hd), lambda kv, g, i, j: (i, kv * grp + g))
    kspec = pl.BlockSpec((tk, hd), lambda kv, g, i, j: (j, kv))
    colspec = pl.BlockSpec((L, hd), lambda kv, g, i, j: (0, kv))
    return pl.pallas_call(
        body, name=name, grid=(KVH, grp, L // tq, nk),
        in_specs=[qspec, kspec, kspec, qspec, qspec, qspec], out_specs=[qspec, colspec, colspec],
        out_shape=[jax.ShapeDtypeStruct((L, H * hd), F32)] + [jax.ShapeDtypeStruct((L, KVH * hd), F32)] * 2,
        scratch_shapes=[pltpu.VMEM((tq, hd), F32)],
        compiler_params=_cparams(("parallel", "arbitrary", "arbitrary", "arbitrary")),
    )(qh, kh, vh, do, lse, delta)


def _loss_grad(x, g, tgt, *, tm, name):
    L, D = x.shape
    tm = _pick(tm, L)

    def body(x_ref, g_ref, t_ref, loss_ref, dx_ref, dg_ref):
        def f(xv, gv):
            err = _rms(xv, gv) - t_ref[...]
            return 0.5 * jnp.sum(jnp.mean(err * err, axis=-1, keepdims=True))

        val, vjp = jax.vjp(f, x_ref[...], g_ref[...])
        dx, dg = vjp(jnp.ones((), F32))
        dx_ref[...] = dx

        @pl.when(pl.program_id(0) == 0)
        def _():
            loss_ref[...] = jnp.zeros_like(loss_ref)
            dg_ref[...] = jnp.zeros_like(dg_ref)

        loss_ref[...] += val
        dg_ref[...] += dg

    return pl.pallas_call(
        body, name=name, grid=(L // tm,),
        in_specs=[pl.BlockSpec((tm, D), lambda i: (i, 0)), pl.BlockSpec((1, D), lambda i: (0, 0)),
                  pl.BlockSpec((tm, D), lambda i: (i, 0))],
        out_specs=[pl.BlockSpec((8, LANE), lambda i: (0, 0)), pl.BlockSpec((tm, D), lambda i: (i, 0)),
                   pl.BlockSpec((1, D), lambda i: (0, 0))],
        out_shape=[jax.ShapeDtypeStruct((8, LANE), F32), jax.ShapeDtypeStruct((L, D), F32),
                   jax.ShapeDtypeStruct((1, D), F32)],
        compiler_params=_cparams(("arbitrary",)),
    )(x, g, tgt)


def _sum_slots(recv, *, tr, name):
    n, R, W = recv.shape
    tr = _pick(tr, R)

    def body(r_ref, o_ref):
        s = r_ref[0].astype(F32)
        for i in range(1, n):
            s = s + r_ref[i].astype(F32)
        o_ref[...] = s

    return pl.pallas_call(
        body, name=name, grid=(R // tr,),
        in_specs=[pl.BlockSpec((n, tr, W), lambda i: (0, i, 0))], out_specs=pl.BlockSpec((tr, W), lambda i: (i, 0)),
        out_shape=jax.ShapeDtypeStruct((R, W), F32), compiler_params=_cparams(("parallel",)),
    )(recv)


def _adamw(w, g, m, v, *, tr, name):
    R, W = w.shape
    tr = _pick(tr, R)
    c1 = 1.0 - ADAM_B1 ** ADAM_STEP
    c2 = 1.0 - ADAM_B2 ** ADAM_STEP

    def body(w_ref, g_ref, m_ref, v_ref, d_ref, nm_ref, nv_ref):
        gv = g_ref[...]
        nm = ADAM_B1 * m_ref[...] + (1.0 - ADAM_B1) * gv
        nv = ADAM_B2 * v_ref[...] + (1.0 - ADAM_B2) * (gv * gv)
        d_ref[...] = -ADAM_LR * ((nm / c1) / (jnp.sqrt(nv / c2) + ADAM_EPS) + ADAM_WD * w_ref[...])
        nm_ref[...] = nm
        nv_ref[...] = nv

    spec = pl.BlockSpec((tr, W), lambda i: (i, 0))
    return pl.pallas_call(
        body, name=name, grid=(R // tr,), in_specs=[spec] * 4, out_specs=[spec] * 3,
        out_shape=[jax.ShapeDtypeStruct((R, W), F32)] * 3, compiler_params=_cparams(("parallel",)),
    )(w, g, m, v)


_MESH = pl.DeviceIdType.MESH


def _all_gather(xs, *, name):
    na = len(xs)

    def body(*refs):
        x_refs, out_refs = refs[:na], refs[na:2 * na]
        send_sems, recv_sems, local_sems = refs[2 * na:]
        x, y, c = lax.axis_index("x"), lax.axis_index("y"), lax.axis_index("c")
        me, sibling = (x, y, c), (x, y, 1 - c)
        chips = [(1 - x, y), (x, 1 - y), (1 - x, 1 - y)]

        def slot(a, px, py, pc):
            return out_refs[a].at[4 * px + 2 * py + pc]

        def copy(a, k, block, to, src=None):
            return pltpu.make_async_remote_copy(
                src_ref=slot(a, *block) if src is None else src, dst_ref=slot(a, *block),
                send_sem=send_sems.at[7 * a + k], recv_sem=recv_sems.at[7 * a + k], device_id=to,
                device_id_type=_MESH)

        mine = [pltpu.make_async_copy(x_refs[a], slot(a, *me), local_sems.at[a]) for a in range(na)]
        for cp in mine:
            cp.start()
        first = []
        for a in range(na):
            first.append(copy(a, 0, me, sibling, src=x_refs[a]))
            first += [copy(a, 1 + j, me, (*chip, c), src=x_refs[a]) for j, chip in enumerate(chips)]
        for cp in first:
            cp.start()
        passed = []
        for a in range(na):
            for j, chip in enumerate(chips):
                copy(a, 1 + j, (*chip, c), me).wait_recv()
                passed.append(copy(a, 4 + j, (*chip, c), sibling))
                passed[-1].start()
        for a in range(na):
            copy(a, 0, sibling, me).wait_recv()
            for j, chip in enumerate(chips):
                copy(a, 4 + j, (*chip, 1 - c), me).wait_recv()
        for cp in first + passed:
            cp.wait_send()
        for cp in mine:
            cp.wait()

    return pl.pallas_call(
        body, name=name,
        out_shape=[jax.ShapeDtypeStruct((N_DEV,) + t.shape, t.dtype) for t in xs],
        in_specs=[pl.BlockSpec(memory_space=pl.ANY)] * na, out_specs=[pl.BlockSpec(memory_space=pl.ANY)] * na,
        scratch_shapes=[pltpu.SemaphoreType.DMA((7 * na,)), pltpu.SemaphoreType.DMA((7 * na,)),
                        pltpu.SemaphoreType.DMA((na,))],
    )(*xs)


def _all_to_all(gs, *, name):
    na, n = len(gs), N_DEV

    def body(*refs):
        g_refs, out_refs = refs[:na], refs[na:2 * na]
        send_sems, recv_sems, local_sems = refs[2 * na:]
        x, y, c = lax.axis_index("x"), lax.axis_index("y"), lax.axis_index("c")
        me = 4 * x + 2 * y + c

        def peer(mask):
            return (x ^ (mask >> 2), y ^ ((mask >> 1) & 1), c ^ (mask & 1))

        def copy(a, mask):
            px, py, pc = peer(mask)
            return pltpu.make_async_remote_copy(
                src_ref=g_refs[a].at[4 * px + 2 * py + pc], dst_ref=out_refs[a].at[me],
                send_sem=send_sems.at[7 * a + mask - 1], recv_sem=recv_sems.at[7 * a + mask - 1],
                device_id=(px, py, pc), device_id_type=_MESH)

        def arrival(a, mask):
            px, py, pc = peer(mask)
            return pltpu.make_async_remote_copy(
                src_ref=g_refs[a].at[me], dst_ref=out_refs[a].at[4 * px + 2 * py + pc],
                send_sem=send_sems.at[7 * a + mask - 1], recv_sem=recv_sems.at[7 * a + mask - 1],
                device_id=(px, py, pc), device_id_type=_MESH)

        mine = [pltpu.make_async_copy(g_refs[a].at[me], out_refs[a].at[me], local_sems.at[a]) for a in range(na)]
        for cp in mine:
            cp.start()
        sends = [copy(a, mask) for a in range(na) for mask in range(1, n)]
        for cp in sends:
            cp.start()
        for a in range(na):
            for mask in range(1, n):
                arrival(a, mask).wait_recv()
        for cp in sends:
            cp.wait_send()
        for cp in mine:
            cp.wait()

    return pl.pallas_call(
        body, name=name,
        out_shape=[jax.ShapeDtypeStruct(t.shape, t.dtype) for t in gs],
        in_specs=[pl.BlockSpec(memory_space=pl.ANY)] * na, out_specs=[pl.BlockSpec(memory_space=pl.ANY)] * na,
        scratch_shapes=[pltpu.SemaphoreType.DMA((7 * na,)), pltpu.SemaphoreType.DMA((7 * na,)),
                        pltpu.SemaphoreType.DMA((na,))],
    )(*gs)


def _rows_of(shape):
    return -(-int(np.prod(shape)) // PACK_W)


def _pack(arrs, dtype, lead=0, total_rows=None):
    pieces = []
    for a in arrs:
        f = a.reshape(a.shape[:lead] + (-1,)).astype(dtype)
        pad = (-f.shape[-1]) % PACK_W
        if pad:
            f = jnp.pad(f, [(0, 0)] * lead + [(0, pad)])
        pieces.append(f.reshape(a.shape[:lead] + (-1, PACK_W)))
    buf = jnp.concatenate(pieces, axis=lead)
    if total_rows is not None and buf.shape[lead] < total_rows:
        buf = jnp.pad(buf, [(0, 0)] * lead + [(0, total_rows - buf.shape[lead]), (0, 0)])
    return buf


def _unpack(buf, shapes, lead=0):
    out, r = [], 0
    for shp in shapes:
        n, rows = int(np.prod(shp)), _rows_of(shp)
        piece = buf[(slice(None),) * lead + (slice(r, r + rows),)]
        piece = piece.reshape(buf.shape[:lead] + (-1,))[..., :n]
        out.append(piece.reshape(buf.shape[:lead] + tuple(shp)))
        r += rows
    return out


def _to_full(parts):
    dep, r = parts.shape[1:3]
    return jnp.transpose(parts, (1, 0) + tuple(range(2, parts.ndim))).reshape((dep, N_DEV * r) + parts.shape[3:])


def _to_slabs(full):
    dep, r = full.shape[:2]
    t = full.reshape((dep, N_DEV, r // N_DEV) + full.shape[2:])
    return jnp.transpose(t, (1, 0) + tuple(range(2, t.ndim)))


def _ref_cols(parts, ro, wd):
    w, out = parts.shape[2], []
    for dev in range(N_DEV):
        lo, hi = max(ro, dev * w), min(ro + wd, (dev + 1) * w)
        if lo < hi:
            out.append(parts[dev][:, lo - dev * w:hi - dev * w])
    return out


def _w_in_to_layout(parts, seg, rseg, nh2):
    D = parts.shape[1]
    cols, off = [], 0
    names = sorted([k for k in seg if not k.startswith('_')], key=lambda k: seg[k][0])
    for nm in names:
        o, wd = seg[nm]
        if o > off:
            cols.append(jnp.zeros((D, o - off), parts.dtype))
        if nm == 'dadb':
            cols += _ref_cols(parts, rseg['da'][0], nh2) + _ref_cols(parts, rseg['db'][0], nh2)
            cols.append(jnp.zeros((D, wd - 2 * nh2), parts.dtype))
        else:
            cols += _ref_cols(parts, rseg[nm][0], wd)
        off = o + wd
    if seg['_total'] > off:
        cols.append(jnp.zeros((D, seg['_total'] - off), parts.dtype))
    return jnp.concatenate(cols, axis=1)


def _w_in_slabs(dw, seg, rseg, nh2):
    w = rseg['_total'] // N_DEV
    ref = []
    for nm in sorted([k for k in rseg if not k.startswith('_')], key=lambda k: rseg[k][0]):
        lo = {'da': seg['dadb'][0], 'db': seg['dadb'][0] + nh2}.get(nm)
        ref.append((rseg[nm][0], rseg[nm][1], seg[nm][0] if lo is None else lo))
    slabs = []
    for dev in range(N_DEV):
        cols = []
        for ro, wd, lo in ref:
            a, b = max(ro, dev * w), min(ro + wd, (dev + 1) * w)
            if a < b:
                cols.append(dw[:, lo + a - ro:lo + b - ro])
        slabs.append(jnp.concatenate(cols, axis=1))
    return jnp.stack(slabs, axis=0)


def _assemble_dh(pieces, seg, L):
    cols, off = [], 0
    for nm in sorted(pieces, key=lambda k: seg[k][0]):
        o = seg[nm][0]
        if o > off:
            cols.append(jnp.zeros((L, o - off), F32))
        cols.append(pieces[nm])
        off = o + pieces[nm].shape[1]
    if seg['_total'] > off:
        cols.append(jnp.zeros((L, seg['_total'] - off), F32))
    return jnp.concatenate(cols, axis=1)


def _lane_pad(v):
    v = v.reshape(1, -1)
    return jnp.pad(v, ((0, 0), (0, LANE - v.shape[1])))


def _rope_tables(L, c):
    rows = L // c['GRID_W']
    row = jnp.repeat(jnp.arange(rows), c['GRID_W']).astype(F32)
    col = jnp.tile(jnp.arange(c['GRID_W']), rows).astype(F32)
    axis_dim = c['AD'] // 2
    freqs = c['ROPE_THETA'] ** (-jnp.arange(0, axis_dim, 2, dtype=F32) / axis_dim)
    ang = jnp.concatenate([row[:, None] * freqs, col[:, None] * freqs], axis=-1)
    cosf = jnp.repeat(jnp.cos(ang), 2, axis=1)
    sn = jnp.sin(ang)
    sins = jnp.stack([-sn, sn], axis=-1).reshape(L, c['AD'])
    idx = np.arange(c['AD'])
    perm = np.zeros((c['AD'], c['AD']), np.float32)
    perm[idx, idx ^ 1] = 1.0
    return cosf, sins, jnp.asarray(perm)


def _s5_dir_params(a, l, dr):
    return (a['ssm_a_re'][l, dr], a['ssm_a_im'][l, dr], a['ssm_log_step'][l, dr], a['ssm_b_re'][l, dr],
            a['ssm_b_im'][l, dr], a['ssm_c_re'][l, dr], a['ssm_c_im'][l, dr])


def _layer_fwd(x, mem, l, wt, a, rope, c, d, seg):
    L, D = x.shape
    SW, DW, AW, AKW, MW, H = d['SW'], d['DW'], d['AW'], d['AKW'], d['MW'], d['DNH']
    cb = lambda nm: seg[nm][0] // seg[nm][1]
    sv = {'x': x}
    p = f"l{l}_"
    sv['g_norm'] = a['norm_g'][l][None, :]
    xn, = _rowwise(_f_norm, [(x, D, 0)], [sv['g_norm']], [(D, BF16)], tm=256, name=p + "norm")
    h = _mm(xn, wt['wp'], name=p + "in_proj", tm=1024, tn=1536, tk=1024)
    sv['xn'], sv['h'] = xn, h

    ysum, sv['s5'] = None, []
    for dr in range(2):
        wb, wc, lr, li = _s5_prep(*_s5_dir_params(a, l, dr), d)
        wb16, wc16 = wb.astype(BF16), wc.astype(BF16)
        lt = _s5_tables(lr, li, bool(dr), False)
        ysum, cin = _s5_fwd(h, cb('u_a'), wb16, wc16, lt, rev=bool(dr), acc=ysum, tb=TILES['s5_t'],
                            name=p + f"s5_fwd{dr}", d=d)
        sv['s5'].append((wb16, wc16, lt, _s5_tables(lr, li, not bool(dr), True), cin))
    sv['ysum'] = ysum
    sv['s5_par'] = [a['ssm_d'][l][None, :], wt['w_glu'], a['ssm_b_glu'][l][None, :]]
    sv['s5_rows'] = [(ysum, SW, 0), (h, SW, cb('u_a')), (h, SW, cb('z_a'))]
    y_a, = _rowwise(_f_s5tail, sv['s5_rows'], sv['s5_par'], [(SW, F32)], tm=256, name=p + "s5_tail")

    act = _conv_fwd(h, cb('dq'), wt['conv'], tm=256, name=p + "dn_conv", d=d)
    sv['act'] = act
    sv['dn_par'] = [_lane_pad(a['dn_a_log'][l]), _lane_pad(a['dn_dt_bias'][l])]
    sv['dn_rows'] = [(act, DW, 0), (act, DW, 1), (h, LANE, seg['dadb'][0] // LANE)]
    dn_out = _rowwise(_make_f_dnpre(H, d['DNK'], c['CHUNK']), sv['dn_rows'], sv['dn_par'], [(DW, F32)] * 8,
                      tm=256, name=p + "dn_pre")
    qn, kn = dn_out[:2]
    sv['qn'], sv['kn'], sv['gates'] = qn, kn, [dn_out[2:5], dn_out[5:8]]
    o_dn, sv['dn_state'] = None, []
    for dr in range(2):
        o_dn, ss = _delta_fwd(qn, kn, act, sv['gates'][dr], vcb=2, rev=bool(dr), acc=o_dn,
                              name=p + f"dn_fwd{dr}", d=d)
        sv['dn_state'].append(ss)
    sv['dnpost_rows'] = [(o_dn, DW, 0), (h, DW, cb('z_b'))]
    sv['dnpost_par'] = [a['dn_norm_g'][l][None, :]]
    y_b, = _rowwise(_make_f_dnpost(d['DNK']), sv['dnpost_rows'], sv['dnpost_par'], [(DW, F32)], tm=256,
                    name=p + "dn_post")

    cosf, sins, perm = rope
    sv['att_par'] = [perm, a['attn_q_norm'][l][None, :], a['attn_k_norm'][l][None, :]]
    qh, kh, vh = _rowwise(_make_f_attpre(d['AD'], True),
                          [(h, AW, cb('aq')), (h, AKW, cb('ak')), (h, AKW, cb('av')), (cosf, d['AD'], 0),
                           (sins, d['AD'], 0)], sv['att_par'], [(AW, BF16), (AKW, BF16), (AKW, BF16)],
                          tm=256, name=p + "att_pre")
    o_att, lse = _attn_fwd(qh, kh, vh, tq=TILES['att_q'], tk=TILES['att_k'], name=p + "att_fwd", d=d)
    sv['qh'], sv['kh'], sv['vh'], sv['o_att'], sv['lse'] = qh, kh, vh, o_att, lse
    y_c, = _rowwise(_f_gate, [(o_att, AW, 0), (h, AW, cb('z_c'))], [], [(AW, F32)], tm=256, name=p + "att_post")

    sv['g_mem'] = a['mem_norm_g'][l][None, :]
    memn, = _rowwise(_f_norm, [(mem, D, 0)], [sv['g_mem']], [(D, BF16)], tm=256, name=p + "mem_norm")
    kv = _mm(memn, wt['w_mem_kv'], name=p + "mem_kv")
    sv['memn'], sv['kv'] = memn, kv
    y_m, = _rowwise(_make_f_mem(d['MH'], d['MD']), [(h, MW, cb('mq')), (h, MW, cb('z_m'))], [kv], [(MW, F32)],
                    tm=256, name=p + "mem_attn")

    ys = [y_a, y_b, y_c, y_m]
    ps = [_mm(y, wb_, name=p + f"branch_proj{i}") for i, (y, wb_) in enumerate(zip(ys, wt['w_branch']))]
    gcb = seg['gates'][0] // D
    sv['merge_rows'] = [(pp, D, 0) for pp in ps] + [(h, D, gcb + i) for i in range(4)]
    merged, = _rowwise(_f_merge, sv['merge_rows'], [], [(D, BF16)], tm=128, name=p + "merge")
    sv['ys'], sv['merged'] = ys, merged
    return _mm(merged, wt['w_out'], add=x, name=p + "out_proj"), sv


def _layer_bwd(dx, mem, l, wt, a, rope, sv, c, d, seg):
    L, D = dx.shape
    SW, DW, AW, AKW, MW, H = d['SW'], d['DW'], d['AW'], d['AKW'], d['MW'], d['DNH']
    cb = lambda nm: seg[nm][0] // seg[nm][1]
    p = f"l{l}_"
    h = sv['h']
    gr = {}
    dmerged = _mm(dx, wt['w_out'], tb=True, name=p + "d_merged")
    gr['w_out'] = _mm(sv['merged'], dx, ta=True, name=p + "dw_out")
    dmr, _ = _rowwise_bwd(_f_merge, sv['merge_rows'], [], [[(dmerged, D, 0)]], [True] * 8, [], tm=128,
                          name=p + "merge_bwd")
    dps, dgates = dmr[:4], dmr[4:]
    dys = [_mm(dp, wb_, tb=True, name=p + f"d_branch{i}") for i, (dp, wb_) in enumerate(zip(dps, wt['w_branch']))]
    gr['w_branch'] = jnp.concatenate(
        [_mm(y, dp, ta=True, name=p + f"dw_branch{i}") for i, (y, dp) in enumerate(zip(sv['ys'], dps))], axis=0)

    (dmq, dzm), (dkv,) = _rowwise_bwd(_make_f_mem(d['MH'], d['MD']), [(h, MW, cb('mq')), (h, MW, cb('z_m'))],
                                      [sv['kv']], [[(dys[3], MW, 0)]], [True, True], [True], tm=256,
                                      name=p + "mem_attn_bwd")
    gr['w_mem_kv'] = _mm(sv['memn'], dkv, ta=True, name=p + "dw_mem_kv")
    dmemn = _mm(dkv, wt['w_mem_kv'], tb=True, name=p + "d_memn")
    _, (dg_mem,) = _rowwise_bwd(_f_norm, [(mem, D, 0)], [sv['g_mem']], [[(dmemn, D, 0)]], [False], [True], tm=256,
                                name=p + "mem_norm_bwd")
    gr['mem_norm_g'] = dg_mem[0]

    (do_att, dzc), _ = _rowwise_bwd(_f_gate, [(sv['o_att'], AW, 0), (h, AW, cb('z_c'))], [], [[(dys[2], AW, 0)]],
                                    [True, True], [], tm=256, name=p + "att_post_bwd")
    delta, = _rowwise(_make_f_delta(d['AD']), [(do_att, AW, 0), (sv['o_att'], AW, 0)], [], [(AW, F32)], tm=256,
                      name=p + "att_delta")
    att_in = (sv['qh'], sv['kh'], sv['vh'], do_att, sv['lse'], delta)
    dqh, dkh, dvh = _attn_bwd(*att_in, tq=TILES['att_q'], tk=TILES['att_k'], name=p + "att_bwd", d=d)
    cosf, sins, _ = rope
    (daq, dak), (dqg, dkg) = _rowwise_bwd(
        _make_f_attpre(d['AD'], False),
        [(h, AW, cb('aq')), (h, AKW, cb('ak')), (cosf, d['AD'], 0), (sins, d['AD'], 0)], sv['att_par'],
        [[(dqh, AW, 0)], [(dkh, AKW, 0)]], [True, True, False, False], [False, True, True], tm=256,
        name=p + "att_pre_bwd")
    gr['attn_q_norm'], gr['attn_k_norm'] = dqg[0], dkg[0]

    (do_dn, dzb), (dng,) = _rowwise_bwd(_make_f_dnpost(d['DNK']), sv['dnpost_rows'], sv['dnpost_par'],
                                        [[(dys[1], DW, 0)]], [True, True], [True], tm=256, name=p + "dn_post_bwd")
    gr['dn_norm_g'] = dng[0]
    accs, dn_dgates = None, []
    for dr in range(2):
        res = _delta_bwd(sv['qn'], sv['kn'], sv['act'], sv['gates'][dr], sv['dn_state'][dr], do_dn, vcb=2,
                         rev=bool(dr), accs=accs, name=p + f"dn_bwd{dr}", d=d)
        accs = res[:3]
        dn_dgates += res[3:]
    dqn, dkn, dvc = accs
    (dqc, dkc, ddadb), (dalog, ddtb) = _rowwise_bwd(
        _make_f_dnpre(H, d['DNK'], c['CHUNK']), sv['dn_rows'], sv['dn_par'],
        [[(t, DW, 0)] for t in [dqn, dkn] + dn_dgates], [True] * 3, [True, True], tm=256, name=p + "dn_pre_bwd")
    gr['dn_a_log'] = dalog[0, :2 * H].reshape(2, H)
    gr['dn_dt_bias'] = ddtb[0, :2 * H].reshape(2, H)
    dconv_x, dconv_w = _conv_bwd(h, cb('dq'), wt['conv'], jnp.concatenate([dqc, dkc, dvc], axis=1), tm=256,
                                 name=p + "dn_conv_bwd", d=d)
    gr['dn_conv'] = jnp.transpose(dconv_w[:, :c['CONV'], :], (0, 2, 1)).reshape(3 * DW, c['CONV'])

    (dysum, du, dza), (dd, dwglu, dbglu) = _rowwise_bwd(_f_s5tail, sv['s5_rows'], sv['s5_par'], [[(dys[0], SW, 0)]],
                                                        [True] * 3, [True] * 3, tm=256, name=p + "s5_tail_bwd")
    gr['ssm_d'], gr['ssm_w_glu'], gr['ssm_b_glu'] = dd[0], dwglu, dbglu[0]
    s5g = []
    for dr in range(2):
        wb16, wc16, lt, lt_adj, cin = sv['s5'][dr]
        du, dwb, dwc, dlam = _s5_bwd(h, cb('u_a'), dysum, cin, wb16, wc16, lt, lt_adj, rev=bool(dr), acc=du,
                                     tb=TILES['s5_t'],
                                     name=p + f"s5_bwd{dr}", d=d)
        dl = jnp.sum(dlam, axis=0).reshape(d['NB'], 2, d['BS'])
        _, prep_vjp = jax.vjp(lambda *pp: _s5_prep(*pp, d), *_s5_dir_params(a, l, dr))
        s5g.append(prep_vjp((dwb, dwc, dl[:, 0], dl[:, 1])))
    for i, nm in enumerate(['ssm_a_re', 'ssm_a_im', 'ssm_log_step', 'ssm_b_re', 'ssm_b_im', 'ssm_c_re', 'ssm_c_im']):
        gr[nm] = jnp.stack([s5g[0][i], s5g[1][i]], axis=0)

    dh = _assemble_dh({'u_a': du, 'z_a': dza, 'dq': dconv_x, 'z_b': dzb, 'ak': dak, 'av': dvh, 'aq': daq,
                       'z_c': dzc, 'mq': dmq, 'z_m': dzm, 'gates': jnp.concatenate(dgates, axis=1),
                       'dadb': ddadb}, seg, L).astype(BF16)
    gr['wp'] = _mm(sv['xn'], dh, ta=True, name=p + "dw_in", tm=1024, tn=1536, tk=1024)
    dxn = _mm(dh, wt['wp'], tb=True, name=p + "d_xn", tm=1024, tn=1024, tk=1536)
    (dx_in,), (dg_norm,) = _rowwise_bwd(_f_norm, [(sv['x'], D, 0)], [sv['g_norm']], [[(dxn, D, 0)]], [True], [True],
                                        tm=256, name=p + "norm_bwd", accs={0: (dx, D, 0)})
    gr['norm_g'] = dg_norm[0]
    return dx_in, gr


_ARG_NAMES = (['x', 'mem'] + WEIGHTS + ['loss_target'] + ['m_' + w for w in WEIGHTS] + ['v_' + w for w in WEIGHTS])


def kernel(x, mem, norm_g, w_in, ssm_a_re, ssm_a_im, ssm_log_step, ssm_b_re, ssm_b_im, ssm_c_re, ssm_c_im,
           ssm_d, ssm_w_glu, ssm_b_glu, dn_conv, dn_a_log, dn_dt_bias, dn_norm_g, attn_q_norm, attn_k_norm,
           mem_norm_g, w_mem_kv, w_branch, w_out, final_norm_g, loss_target, m_norm_g, m_w_in, m_ssm_a_re,
           m_ssm_a_im, m_ssm_log_step, m_ssm_b_re, m_ssm_b_im, m_ssm_c_re, m_ssm_c_im, m_ssm_d, m_ssm_w_glu,
           m_ssm_b_glu, m_dn_conv, m_dn_a_log, m_dn_dt_bias, m_dn_norm_g, m_attn_q_norm, m_attn_k_norm,
           m_mem_norm_g, m_w_mem_kv, m_w_branch, m_w_out, m_final_norm_g, v_norm_g, v_w_in, v_ssm_a_re,
           v_ssm_a_im, v_ssm_log_step, v_ssm_b_re, v_ssm_b_im, v_ssm_c_re, v_ssm_c_im, v_ssm_d, v_ssm_w_glu,
           v_ssm_b_glu, v_dn_conv, v_dn_a_log, v_dn_dt_bias, v_dn_norm_g, v_attn_q_norm, v_attn_k_norm,
           v_mem_norm_g, v_w_mem_kv, v_w_branch, v_w_out, v_final_norm_g):
    given = locals()
    return _train_step({n: given[n] for n in _ARG_NAMES})


def _train_step(a):
    c = CFG
    d = _dims(c)
    seg, rseg = _layout(c)
    depth, nh2 = c['DEPTH'], 2 * c['DNH']
    x, mem, tgt = a['x'][0], a['mem'][0], a['loss_target'][0]
    L, D = x.shape

    packed = [n for n in SHARDED if n != 'w_in']
    shard_shapes = [a[n].shape for n in packed]
    rw = _round_up(sum(_rows_of(s) for s in shard_shapes), LANE)
    win_shape = a['w_in'].shape
    wcols = win_shape[2]
    g_win, gathered = _all_gather([a['w_in'].astype(BF16).reshape(depth * D, wcols),
                                   _pack([a[n] for n in packed], BF16, total_rows=rw)], name="weights_all_gather")
    full = {n: _to_full(p_) for n, p_ in zip(packed, _unpack(gathered, shard_shapes, lead=1))}
    offs = np.cumsum([0, d['SW'], d['DW'], d['AW'], d['MW']])
    wts = []
    for l in range(depth):
        conv = jnp.transpose(full['dn_conv'][l].astype(F32).reshape(3, d['DW'], c['CONV']), (0, 2, 1))
        wts.append(dict(
            wp=_w_in_to_layout(g_win[:, l * D:(l + 1) * D], seg, rseg, nh2),
            w_branch=[full['w_branch'][l, offs[i]:offs[i + 1]] for i in range(4)],
            w_out=full['w_out'][l], w_mem_kv=full['w_mem_kv'][l], w_glu=full['ssm_w_glu'][l].astype(F32),
            conv=jnp.pad(conv, ((0, 0), (0, 8 - c['CONV']), (0, 0)))))
    rope = _rope_tables(L, c)

    saved = []
    for l in range(depth):
        x, sv = _layer_fwd(x, mem, l, wts[l], a, rope, c, d, seg)
        saved.append(sv)
    loss_part, dx, dg_final = _loss_grad(x, a['final_norm_g'][None, :], tgt, tm=256, name="final_norm_loss")
    grads = [None] * depth
    for l in reversed(range(depth)):
        dx, grads[l] = _layer_bwd(dx, mem, l, wts[l], a, rope, saved[l], c, d, seg)

    gfull = {n: jnp.stack([grads[l][n] for l in range(depth)], axis=0) for n in WEIGHTS
             if n not in ('w_in', 'final_norm_g')}
    gfull['final_norm_g'] = dg_final[0]

    win_slabs = jnp.concatenate([_w_in_slabs(grads[l]['wp'], seg, rseg, nh2) for l in range(depth)], axis=1)
    small_shapes = [a[n].shape for n in SMALL] + [(1,)]
    rs = _round_up(sum(_rows_of(s) for s in small_shapes), LANE)
    g_shard = _pack([_to_slabs(gfull[n]) for n in packed], BF16, lead=1, total_rows=rw)
    g_small = _pack([gfull[n] for n in SMALL] + [loss_part[0, :1]], F32, total_rows=rs)
    recv = _all_to_all([win_slabs.astype(BF16), g_shard, jnp.broadcast_to(g_small[None], (N_DEV,) + g_small.shape)],
                       name="grads_all_to_all")
    g_win_sum = _sum_slots(recv[0], tr=256, name="w_in_grad_sum")
    gsum = jnp.concatenate([_sum_slots(recv[1], tr=256, name="shard_grad_sum"),
                            _sum_slots(recv[2], tr=256, name="small_grad_sum")], axis=0)
    flat = lambda t: t.reshape(depth * D, wcols)
    d_win, m_win, v_win = _adamw(flat(a['w_in']), g_win_sum, flat(a['m_w_in']), flat(a['v_w_in']), tr=256,
                                 name="w_in_adamw")
    win_out = [t.reshape(win_shape) for t in (g_win_sum, d_win, m_win, v_win)]

    def local_pack(prefix):
        zero = jnp.zeros((1,), F32)
        return jnp.concatenate([_pack([a[prefix + n] for n in packed], F32, total_rows=rw),
                                _pack([a[prefix + n] for n in SMALL] + [zero], F32, total_rows=rs)], axis=0)

    delta, new_m, new_v = _adamw(local_pack(''), gsum, local_pack('m_'), local_pack('v_'), tr=256, name="adamw")

    def split(buf):
        vals = dict(zip(packed, _unpack(buf[:rw], shard_shapes)))
        small = _unpack(buf[rw:], small_shapes)
        vals.update(zip(SMALL, small[:-1]))
        return vals, small[-1]

    _, loss = split(gsum)
    outs = [loss.reshape(()), dx[None]]
    for i, buf in enumerate((gsum, delta, new_m, new_v)):
        vals, _ = split(buf)
        vals['w_in'] = win_out[i]
        outs += [vals[n] for n in WEIGHTS]
    return tuple(outs)
```

```python
import functools
import math

import numpy as np
import jax
import jax.numpy as jnp
from jax import lax
from jax.experimental import pallas as pl
from jax.experimental.pallas import tpu as pltpu

F32 = jnp.float32
BF16 = jnp.bfloat16
HI = lax.Precision.HIGHEST
EPS = 1e-6
LANE = 128
SUBLANE = 8
VMEM_LIMIT = 56 * 1024 * 1024
N_DEV = 8
PACK_W = 1024

ADAM_LR, ADAM_B1, ADAM_B2, ADAM_EPS, ADAM_WD, ADAM_STEP = 0.001, 0.9, 0.999, 1e-08, 0.01, 10

CFG = dict(D=2048, L=8192, GRID_W=64, NMEM=256, DEPTH=2,
           SG=48, SP=16, SN=64,
           DNH=6, DNK=128, CONV=5, CHUNK=64,
           AH=8, AKV=2, AD=128, ROPE_THETA=10000.0,
           MH=4, MD=128)

TILES = dict(att_q=1024, att_k=2048, s5_t=512)

WEIGHTS = ['norm_g', 'w_in', 'ssm_a_re', 'ssm_a_im', 'ssm_log_step', 'ssm_b_re', 'ssm_b_im', 'ssm_c_re',
           'ssm_c_im', 'ssm_d', 'ssm_w_glu', 'ssm_b_glu', 'dn_conv', 'dn_a_log', 'dn_dt_bias', 'dn_norm_g',
           'attn_q_norm', 'attn_k_norm', 'mem_norm_g', 'w_mem_kv', 'w_branch', 'w_out', 'final_norm_g']
SHARDED = ['w_in', 'w_branch', 'w_out', 'w_mem_kv', 'ssm_w_glu', 'dn_conv']
SMALL = [w for w in WEIGHTS if w not in SHARDED]


def _dims(c):
    d = dict(c)
    d['SW'] = c['SG'] * c['SP']
    d['NB'] = d['SW'] // LANE
    d['GPB'] = LANE // c['SP']
    d['BS'] = d['GPB'] * c['SN']
    d['DW'] = c['DNH'] * c['DNK']
    d['AW'] = c['AH'] * c['AD']
    d['AKW'] = c['AKV'] * c['AD']
    d['MW'] = c['MH'] * c['MD']
    d['BT'] = d['SW'] + d['DW'] + d['AW'] + d['MW']
    return d


def _round_up(a, b):
    return (a + b - 1) // b * b


def _layout(c):
    d = _dims(c)
    D, SW, DW, AW, AKW, MW = d['D'], d['SW'], d['DW'], d['AW'], d['AKW'], d['MW']
    order = [('u_a', SW, SW), ('z_a', SW, SW), ('dq', DW, DW), ('dk', DW, DW), ('dv', DW, DW), ('z_b', DW, DW),
             ('ak', AKW, AKW), ('av', AKW, AKW), ('aq', AW, AW), ('z_c', AW, AW), ('mq', MW, MW), ('z_m', MW, MW),
             ('gates', 4 * D, D), ('dadb', LANE, LANE)]
    off, seg = 0, {}
    for name, w, al in order:
        off = _round_up(off, al)
        seg[name] = (off, w)
        off += w
    seg['_total'] = _round_up(off, 512)
    ref_order = [('u_a', SW), ('z_a', SW), ('dq', DW), ('dk', DW), ('dv', DW), ('da', 2 * d['DNH']),
                 ('db', 2 * d['DNH']), ('z_b', DW), ('aq', AW), ('ak', AKW), ('av', AKW), ('z_c', AW),
                 ('mq', MW), ('z_m', MW), ('gates', 4 * D)]
    roff, rseg = 0, {}
    for name, w in ref_order:
        rseg[name] = (roff, w)
        roff += w
    rseg['_total'] = roff
    return seg, rseg


def _cparams(sem):
    return pltpu.CompilerParams(dimension_semantics=sem, vmem_limit_bytes=VMEM_LIMIT)


def _pick(t, n):
    if n <= t:
        return n
    for align in (LANE, 2 * SUBLANE):
        for cand in range(t - t % align, 0, -align):
            if n % cand == 0:
                return cand
    return n


def _mm(a, b, *, name, ta=False, tb=False, add=None, out_dtype=F32, tm=1024, tn=1024, tk=512):
    M, K = (a.shape[1], a.shape[0]) if ta else a.shape
    N = b.shape[0] if tb else b.shape[1]
    assert (b.shape[1] if tb else b.shape[0]) == K
    tm, tn, tk = _pick(tm, M), _pick(tn, N), _pick(tk, K)
    nk = K // tk
    dn = (((0 if ta else 1,), (1 if tb else 0,)), ((), ()))
    has_add = add is not None

    def body(*refs):
        if has_add:
            a_ref, b_ref, add_ref, o_ref, acc = refs
        else:
            a_ref, b_ref, o_ref, acc = refs
        k = pl.program_id(2)

        @pl.when(k == 0)
        def _():
            acc[...] = jnp.zeros_like(acc)

        acc[...] += lax.dot_general(a_ref[...].astype(BF16), b_ref[...].astype(BF16), dn,
                                    preferred_element_type=F32)

        @pl.when(k == nk - 1)
        def _():
            r = acc[...]
            if has_add:
                r = r + add_ref[...]
            o_ref[...] = r.astype(o_ref.dtype)

    a_spec = pl.BlockSpec((tk, tm), lambda i, j, k: (k, i)) if ta else pl.BlockSpec((tm, tk), lambda i, j, k: (i, k))
    b_spec = pl.BlockSpec((tn, tk), lambda i, j, k: (j, k)) if tb else pl.BlockSpec((tk, tn), lambda i, j, k: (k, j))
    in_specs = [a_spec, b_spec]
    args = [a, b]
    if has_add:
        in_specs.append(pl.BlockSpec((tm, tn), lambda i, j, k: (i, j)))
        args.append(add)
    return pl.pallas_call(
        body, name=name, grid=(M // tm, N // tn, nk),
        in_specs=in_specs, out_specs=pl.BlockSpec((tm, tn), lambda i, j, k: (i, j)),
        out_shape=jax.ShapeDtypeStruct((M, N), out_dtype),
        scratch_shapes=[pltpu.VMEM((tm, tn), F32)],
        compiler_params=_cparams(("parallel", "parallel", "arbitrary")),
    )(*args)


def _row_spec(tm, w, cb):
    return pl.BlockSpec((tm, w), lambda i, cb=cb: (i, cb))


def _rowwise(fn, rows, params, outs, *, tm, name):
    L = rows[0][0].shape[0]
    tm = _pick(tm, L)
    nr, npar = len(rows), len(params)

    def body(*refs):
        vals = [r[...] for r in refs[:nr + npar]]
        res = fn(*vals)
        for o_ref, v in zip(refs[nr + npar:], res):
            o_ref[...] = v.astype(o_ref.dtype)

    in_specs = [_row_spec(tm, w, cb) for (_, w, cb) in rows]
    in_specs += [pl.BlockSpec(p.shape, lambda i: (0, 0)) for p in params]
    res = pl.pallas_call(
        body, name=name, grid=(L // tm,), in_specs=in_specs,
        out_specs=[pl.BlockSpec((tm, w), lambda i: (i, 0)) for (w, _) in outs],
        out_shape=[jax.ShapeDtypeStruct((L, w), dt) for (w, dt) in outs],
        compiler_params=_cparams(("parallel",)),
    )(*[r[0] for r in rows], *params)
    return list(res)


def _rowwise_bwd(fn, rows, params, cts, drows, dparams, *, tm, name, accs=None, row_grad_dtype=F32):
    L = rows[0][0].shape[0]
    tm = _pick(tm, L)
    nr, npar = len(rows), len(params)
    accs = accs or {}
    ct_flat = [c for grp in cts for c in grp]
    ct_sizes = [len(grp) for grp in cts]
    acc_keys = sorted(accs)
    d_r = [i for i in range(nr) if drows[i]]
    d_p = [i for i in range(npar) if dparams[i]]
    n_in = nr + npar + len(ct_flat) + len(acc_keys)

    def body(*refs):
        vals = [r[...] for r in refs[:nr + npar]]
        ct_refs = refs[nr + npar:nr + npar + len(ct_flat)]
        acc_refs = refs[nr + npar + len(ct_flat):n_in]
        o_refs = refs[n_in:]
        ct_vals, pos = [], 0
        for n in ct_sizes:
            v = ct_refs[pos][...].astype(F32)
            for r in ct_refs[pos + 1:pos + n]:
                v = v + r[...].astype(F32)
            ct_vals.append(v)
            pos += n
        diff_idx = d_r + [nr + i for i in d_p]

        def g(*dv):
            full = list(vals)
            for i, v in zip(diff_idx, dv):
                full[i] = v
            return tuple(o.astype(F32) for o in fn(*full))

        _, vjp = jax.vjp(g, *[vals[i] for i in diff_idx])
        grads = vjp(tuple(ct_vals))
        for n, i in enumerate(d_r):
            gv = grads[n].astype(F32)
            if i in accs:
                gv = gv + acc_refs[acc_keys.index(i)][...]
            o_refs[n][...] = gv.astype(o_refs[n].dtype)
        step = pl.program_id(0)
        for n, i in enumerate(d_p):
            o_ref = o_refs[len(d_r) + n]

            @pl.when(step == 0)
            def _(o_ref=o_ref):
                o_ref[...] = jnp.zeros_like(o_ref)

            o_ref[...] += grads[len(d_r) + n].astype(F32)

    in_specs = [_row_spec(tm, w, cb) for (_, w, cb) in rows]
    in_specs += [pl.BlockSpec(p.shape, lambda i: (0, 0)) for p in params]
    in_specs += [_row_spec(tm, w, cb) for (_, w, cb) in ct_flat]
    in_specs += [_row_spec(tm, accs[k][1], accs[k][2]) for k in acc_keys]
    out_specs = [pl.BlockSpec((tm, rows[i][1]), lambda i_: (i_, 0)) for i in d_r]
    out_specs += [pl.BlockSpec(params[i].shape, lambda i_: (0, 0)) for i in d_p]
    out_shape = [jax.ShapeDtypeStruct((L, rows[i][1]), row_grad_dtype) for i in d_r]
    out_shape += [jax.ShapeDtypeStruct(params[i].shape, F32) for i in d_p]
    res = pl.pallas_call(
        body, name=name, grid=(L // tm,), in_specs=in_specs, out_specs=out_specs, out_shape=out_shape,
        compiler_params=_cparams(("arbitrary",)),
    )(*[r[0] for r in rows], *params, *[c[0] for c in ct_flat], *[accs[k][0] for k in acc_keys])
    res = list(res)
    return res[:len(d_r)], res[len(d_r):]


def _silu(x):
    return x * jax.nn.sigmoid(x)


def _rms(x, g):
    return x * lax.rsqrt(jnp.mean(x * x, axis=-1, keepdims=True) + EPS) * g


def _softplus(x):
    return jnp.maximum(x, 0.0) + jnp.log1p(jnp.exp(-jnp.abs(x)))


def _heads(x, hd):
    return [x[:, i * hd:(i + 1) * hd] for i in range(x.shape[1] // hd)]


def _f_norm(x, g):
    return (_rms(x, g),)


def _f_s5tail(ys, u, z, d, wglu, bglu):
    y = jax.nn.gelu(ys + d * u)
    gate = jax.nn.sigmoid(jnp.dot(y.astype(BF16), wglu.astype(BF16), preferred_element_type=F32) + bglu)
    return (y * gate * _silu(z),)


def _make_f_dnpre(nh, hd, chunk):
    def f(qc, kc, dadb, alog, dtb):
        tm = qc.shape[0]
        qn = [q * lax.rsqrt(jnp.sum(q * q, axis=-1, keepdims=True) + EPS) * (hd ** -0.5) for q in _heads(qc, hd)]
        kn = [k * lax.rsqrt(jnp.sum(k * k, axis=-1, keepdims=True) + EPS) for k in _heads(kc, hd)]
        g = -jnp.exp(alog) * _softplus(dadb + dtb)
        beta = jax.nn.sigmoid(dadb)
        ii = lax.broadcasted_iota(jnp.int32, (tm, tm), 0)
        jj = lax.broadcasted_iota(jnp.int32, (tm, tm), 1)
        same = (ii // chunk) == (jj // chunk)
        outs = [jnp.concatenate(qn, axis=1), jnp.concatenate(kn, axis=1)]
        gt = jnp.dot(same.astype(F32), g, precision=HI, preferred_element_type=F32)
        for dr in range(2):
            tri = jnp.logical_and(same, (ii <= jj) if dr else (ii >= jj)).astype(F32)
            gc = jnp.dot(tri, g, precision=HI, preferred_element_type=F32)

            def spread(t, lane0):
                return jnp.concatenate([jnp.broadcast_to(t[:, lane0 + h:lane0 + h + 1], (tm, hd))
                                        for h in range(nh)], axis=1)

            outs += [spread(beta, 2 * nh + dr * nh), spread(gc, dr * nh), spread(gt, dr * nh)]
        return tuple(outs)
    return f


def _make_f_dnpost(hd):
    def f(o, z, ng):
        y = [_rms(oh, ng) for oh in _heads(o, hd)]
        return (jnp.concatenate(y, axis=1) * _silu(z),)
    return f


def _make_f_attpre(hd, with_v):
    def rope(x, g, cosf, sins, perm, scale):
        xn = _rms(x, g)
        xs = jnp.dot(xn, perm, precision=HI, preferred_element_type=F32)
        return (xn * cosf + xs * sins) * scale

    def f(aq, ak, *rest):
        if with_v:
            av, cosf, sins, perm, qg, kg = rest
        else:
            cosf, sins, perm, qg, kg = rest
        qh = jnp.concatenate([rope(x, qg, cosf, sins, perm, hd ** -0.5) for x in _heads(aq, hd)], axis=1)
        kh = jnp.concatenate([rope(x, kg, cosf, sins, perm, 1.0) for x in _heads(ak, hd)], axis=1)
        return (qh, kh, av) if with_v else (qh, kh)
    return f


def _f_gate(o, z):
    return (o * _silu(z),)


def _make_f_mem(nh, hd):
    def f(mq, z, kv):
        mw = nh * hd
        outs = []
        for h, q in enumerate(_heads(mq, hd)):
            k = kv[:, h * hd:(h + 1) * hd]
            v = kv[:, mw + h * hd:mw + (h + 1) * hd]
            s = lax.dot_general(q.astype(BF16), k.astype(BF16), (((1,), (1,)), ((), ())),
                                preferred_element_type=F32) * (hd ** -0.5)
            s = s - jnp.max(s, axis=-1, keepdims=True)
            p = jnp.exp(s)
            p = p / jnp.sum(p, axis=-1, keepdims=True)
            outs.append(jnp.dot(p.astype(BF16), v.astype(BF16), preferred_element_type=F32))
        return (jnp.concatenate(outs, axis=1) * _silu(z),)
    return f


def _f_merge(p0, p1, p2, p3, g0, g1, g2, g3):
    return (jax.nn.sigmoid(g0) * p0 + jax.nn.sigmoid(g1) * p1 + jax.nn.sigmoid(g2) * p2 + jax.nn.sigmoid(g3) * p3,)


def _make_f_delta(hd):
    def f(do, o):
        out = [jnp.broadcast_to(jnp.sum(a * b, axis=-1, keepdims=True), a.shape)
               for a, b in zip(_heads(do, hd), _heads(o, hd))]
        return (jnp.concatenate(out, axis=1),)
    return f


def _s5_prep(a_re, a_im, log_step, b_re, b_im, c_re, c_im, d):
    nb, gpb, sn, sp = d['NB'], d['GPB'], d['SN'], d['SP']
    step = jnp.exp(log_step)[:, None]
    mag = jnp.exp(a_re * step)
    lam_re = mag * jnp.cos(a_im * step)
    lam_im = mag * jnp.sin(a_im * step)
    den = a_re * a_re + a_im * a_im
    nr, ni = lam_re - 1.0, lam_im
    coef_re = (nr * a_re + ni * a_im) / den
    coef_im = (ni * a_re - nr * a_im) / den
    bb_re = coef_re[..., None] * b_re - coef_im[..., None] * b_im
    bb_im = coef_re[..., None] * b_im + coef_im[..., None] * b_re
    eye = jnp.eye(gpb, dtype=F32)

    def blk_in(bb):
        t = bb.reshape(nb, gpb, sn, sp)
        return jnp.einsum("jgnp,gh->jgphn", t, eye).reshape(nb, gpb * sp, gpb * sn)

    def blk_out(cc):
        t = cc.reshape(nb, gpb, sp, sn)
        return jnp.einsum("jgpn,gh->jgnhp", t, eye).reshape(nb, gpb * sn, gpb * sp)

    wb = jnp.concatenate([blk_in(bb_re), blk_in(bb_im)], axis=2)
    wc = jnp.concatenate([blk_out(c_re), blk_out(-c_im)], axis=1)
    return wb, wc, lam_re.reshape(nb, gpb * sn), lam_im.reshape(nb, gpb * sn)


def _s5_tables(lam_re, lam_im, rev, conj):
    lr, li = lam_re, (-lam_im if conj else lam_im)

    def cmul(a, b):
        return a[0] * b[0] - a[1] * b[1], a[0] * b[1] + a[1] * b[0]

    pw = [(lr, li)]
    for _ in range(7):
        pw.append(cmul(pw[-1], (lr, li)))
    rows = jnp.arange(8)

    def bc(t, k):
        keep = (rows < 8 - k) if rev else (rows >= k)
        return t[:, None, :] * keep.astype(F32)[None, :, None]

    order = list(range(8))[::-1] if rev else list(range(8))
    pwr = jnp.stack([pw[i][0] for i in order], axis=1)
    pwi = jnp.stack([pw[i][1] for i in order], axis=1)
    tabs = [bc(pw[0][0], 1), bc(pw[0][1], 1), bc(pw[1][0], 2), bc(pw[1][1], 2), bc(pw[3][0], 4), bc(pw[3][1], 4),
            pwr, pwi]
    return jnp.stack(tabs, axis=1)


def _scan_group(xr, xi, lt_ref, j, cr, ci, rev):
    for lvl, k in enumerate((1, 2, 4)):
        l_r, l_i = lt_ref[j, 2 * lvl], lt_ref[j, 2 * lvl + 1]
        sh = (8 - k) if rev else k
        sr, si = pltpu.roll(xr, sh, 0), pltpu.roll(xi, sh, 0)
        xr, xi = xr + l_r * sr - l_i * si, xi + l_r * si + l_i * sr
    p_r, p_i = lt_ref[j, 6], lt_ref[j, 7]
    return xr + p_r * cr - p_i * ci, xi + p_r * ci + p_i * cr


def _last_row(x, rev):
    last = 0 if rev else 7
    return jnp.broadcast_to(x[last:last + 1, :], x.shape)


def _s5_fwd(hsrc, ucb, wb, wc, lt, *, rev, acc, tb, name, d):
    L, SW, NB, BS = hsrc.shape[0], d['SW'], d['NB'], d['BS']
    tb = _pick(tb, L)
    nblk, ngr = L // tb, tb // 8
    tix = (lambda b: nblk - 1 - b) if rev else (lambda b: b)
    has_acc = acc is not None

    def body(*refs):
        if has_acc:
            u_ref, wb_ref, wc_ref, lt_ref, acc_ref, y_ref, cin_ref, bu_s, car = refs
        else:
            u_ref, wb_ref, wc_ref, lt_ref, y_ref, cin_ref, bu_s, car = refs

        @pl.when(pl.program_id(0) == 0)
        def _():
            car[...] = jnp.zeros_like(car)

        cin_ref[...] = car[...]
        for j in range(NB):
            bu_s[:, j * 2 * BS:(j + 1) * 2 * BS] = jnp.dot(
                u_ref[:, j * LANE:(j + 1) * LANE].astype(BF16), wb_ref[j], preferred_element_type=F32)

        def grp(r, _):
            base = pl.multiple_of((ngr - 1 - r if rev else r) * 8, 8)
            for j in range(NB):
                c0 = j * 2 * BS
                xr, xi = _scan_group(bu_s[pl.ds(base, 8), c0:c0 + BS], bu_s[pl.ds(base, 8), c0 + BS:c0 + 2 * BS],
                                     lt_ref, j, car[:, c0:c0 + BS], car[:, c0 + BS:c0 + 2 * BS], rev)
                bu_s[pl.ds(base, 8), c0:c0 + BS] = xr
                bu_s[pl.ds(base, 8), c0 + BS:c0 + 2 * BS] = xi
                car[:, c0:c0 + BS] = _last_row(xr, rev)
                car[:, c0 + BS:c0 + 2 * BS] = _last_row(xi, rev)
            return 0

        lax.fori_loop(0, ngr, grp, 0)
        for j in range(NB):
            y = jnp.dot(bu_s[:, j * 2 * BS:(j + 1) * 2 * BS].astype(BF16), wc_ref[j], preferred_element_type=F32)
            if has_acc:
                y = y + acc_ref[:, j * LANE:(j + 1) * LANE]
            y_ref[:, j * LANE:(j + 1) * LANE] = y

    in_specs = [pl.BlockSpec((tb, SW), lambda b: (tix(b), ucb)),
                pl.BlockSpec(wb.shape, lambda b: (0, 0, 0)), pl.BlockSpec(wc.shape, lambda b: (0, 0, 0)),
                pl.BlockSpec(lt.shape, lambda b: (0, 0, 0, 0))]
    args = [hsrc, wb, wc, lt]
    if has_acc:
        in_specs.append(pl.BlockSpec((tb, SW), lambda b: (tix(b), 0)))
        args.append(acc)
    y, cin = pl.pallas_call(
        body, name=name, grid=(nblk,), in_specs=in_specs,
        out_specs=[pl.BlockSpec((tb, SW), lambda b: (tix(b), 0)),
                   pl.BlockSpec((8, NB * 2 * BS), lambda b: (tix(b), 0))],
        out_shape=[jax.ShapeDtypeStruct((L, SW), F32), jax.ShapeDtypeStruct((nblk * 8, NB * 2 * BS), F32)],
        scratch_shapes=[pltpu.VMEM((tb, NB * 2 * BS), F32), pltpu.VMEM((8, NB * 2 * BS), F32)],
        compiler_params=_cparams(("arbitrary",)),
    )(*args)
    return y, cin


def _s5_bwd(hsrc, ucb, dy, cin, wb, wc, lt, lt_adj, *, rev, acc, tb, name, d):
    L, SW, NB, BS = hsrc.shape[0], d['SW'], d['NB'], d['BS']
    tb = _pick(tb, L)
    nblk, ngr = L // tb, tb // 8
    arev = not rev
    tix = (lambda b: nblk - 1 - b) if arev else (lambda b: b)
    has_acc = acc is not None
    NT = (((1,), (1,)), ((), ()))
    TN = (((0,), (0,)), ((), ()))

    def body(*refs):
        if has_acc:
            (u_ref, dy_ref, cin_ref, wb_ref, wc_ref, lt_ref, la_ref, acc_ref,
             du_ref, dwb_ref, dwc_ref, dlam_ref, s_s, g_s, car, acar) = refs
        else:
            (u_ref, dy_ref, cin_ref, wb_ref, wc_ref, lt_ref, la_ref,
             du_ref, dwb_ref, dwc_ref, dlam_ref, s_s, g_s, car, acar) = refs

        @pl.when(pl.program_id(0) == 0)
        def _():
            acar[...] = jnp.zeros_like(acar)
            dwb_ref[...] = jnp.zeros_like(dwb_ref)
            dwc_ref[...] = jnp.zeros_like(dwc_ref)
            dlam_ref[...] = jnp.zeros_like(dlam_ref)

        car[...] = cin_ref[...]
        for j in range(NB):
            s_s[:, j * 2 * BS:(j + 1) * 2 * BS] = jnp.dot(
                u_ref[:, j * LANE:(j + 1) * LANE].astype(BF16), wb_ref[j], preferred_element_type=F32)
            g_s[:, j * 2 * BS:(j + 1) * 2 * BS] = lax.dot_general(
                dy_ref[:, j * LANE:(j + 1) * LANE].astype(BF16), wc_ref[j], NT, preferred_element_type=F32)

        def fgrp(r, _):
            base = pl.multiple_of((ngr - 1 - r if rev else r) * 8, 8)
            for j in range(NB):
                c0 = j * 2 * BS
                xr, xi = _scan_group(s_s[pl.ds(base, 8), c0:c0 + BS], s_s[pl.ds(base, 8), c0 + BS:c0 + 2 * BS],
                                     lt_ref, j, car[:, c0:c0 + BS], car[:, c0 + BS:c0 + 2 * BS], rev)
                s_s[pl.ds(base, 8), c0:c0 + BS] = xr
                s_s[pl.ds(base, 8), c0 + BS:c0 + 2 * BS] = xi
                car[:, c0:c0 + BS] = _last_row(xr, rev)
                car[:, c0 + BS:c0 + 2 * BS] = _last_row(xi, rev)
            return 0

        lax.fori_loop(0, ngr, fgrp, 0)

        row = lax.broadcasted_iota(jnp.int32, (8, BS), 0)

        def agrp(r, _):
            gi = ngr - 1 - r if arev else r
            base = pl.multiple_of(gi * 8, 8)
            pgi = gi + 1 if rev else gi - 1
            inside = jnp.logical_and(pgi >= 0, pgi < ngr)
            pbase = pl.multiple_of(jnp.clip(pgi, 0, ngr - 1) * 8, 8)
            for j in range(NB):
                c0 = j * 2 * BS
                ar, ai = _scan_group(g_s[pl.ds(base, 8), c0:c0 + BS], g_s[pl.ds(base, 8), c0 + BS:c0 + 2 * BS],
                                     la_ref, j, acar[:, c0:c0 + BS], acar[:, c0 + BS:c0 + 2 * BS], arev)
                g_s[pl.ds(base, 8), c0:c0 + BS] = ar
                g_s[pl.ds(base, 8), c0 + BS:c0 + 2 * BS] = ai
                acar[:, c0:c0 + BS] = _last_row(ar, arev)
                acar[:, c0 + BS:c0 + 2 * BS] = _last_row(ai, arev)
                sr, si = s_s[pl.ds(base, 8), c0:c0 + BS], s_s[pl.ds(base, 8), c0 + BS:c0 + 2 * BS]
                edge_r = jnp.where(inside, _last_row(s_s[pl.ds(pbase, 8), c0:c0 + BS], rev), cin_ref[:, c0:c0 + BS])
                edge_i = jnp.where(inside, _last_row(s_s[pl.ds(pbase, 8), c0 + BS:c0 + 2 * BS], rev),
                                   cin_ref[:, c0 + BS:c0 + 2 * BS])
                sh = 7 if rev else 1
                first = 7 if rev else 0
                pr = jnp.where(row == first, edge_r, pltpu.roll(sr, sh, 0))
                pi = jnp.where(row == first, edge_i, pltpu.roll(si, sh, 0))
                dlam_ref[:, c0:c0 + BS] += ar * pr + ai * pi
                dlam_ref[:, c0 + BS:c0 + 2 * BS] += ai * pr - ar * pi
            return 0

        lax.fori_loop(0, ngr, agrp, 0)
        for j in range(NB):
            a_j = g_s[:, j * 2 * BS:(j + 1) * 2 * BS].astype(BF16)
            u_j = u_ref[:, j * LANE:(j + 1) * LANE].astype(BF16)
            du = lax.dot_general(a_j, wb_ref[j], NT, preferred_element_type=F32)
            if has_acc:
                du = du + acc_ref[:, j * LANE:(j + 1) * LANE]
            du_ref[:, j * LANE:(j + 1) * LANE] = du
            dwb_ref[j] += lax.dot_general(u_j, a_j, TN, preferred_element_type=F32)
            dwc_ref[j] += lax.dot_general(s_s[:, j * 2 * BS:(j + 1) * 2 * BS].astype(BF16),
                                          dy_ref[:, j * LANE:(j + 1) * LANE].astype(BF16), TN,
                                          preferred_element_type=F32)

    W2 = NB * 2 * BS
    in_specs = [pl.BlockSpec((tb, SW), lambda b: (tix(b), ucb)), pl.BlockSpec((tb, SW), lambda b: (tix(b), 0)),
                pl.BlockSpec((8, W2), lambda b: (tix(b), 0)),
                pl.BlockSpec(wb.shape, lambda b: (0, 0, 0)), pl.BlockSpec(wc.shape, lambda b: (0, 0, 0)),
                pl.BlockSpec(lt.shape, lambda b: (0, 0, 0, 0)), pl.BlockSpec(lt_adj.shape, lambda b: (0, 0, 0, 0))]
    args = [hsrc, dy, cin, wb, wc, lt, lt_adj]
    if has_acc:
        in_specs.append(pl.BlockSpec((tb, SW), lambda b: (tix(b), 0)))
        args.append(acc)
    return pl.pallas_call(
        body, name=name, grid=(nblk,), in_specs=in_specs,
        out_specs=[pl.BlockSpec((tb, SW), lambda b: (tix(b), 0)),
                   pl.BlockSpec(wb.shape, lambda b: (0, 0, 0)), pl.BlockSpec(wc.shape, lambda b: (0, 0, 0)),
                   pl.BlockSpec((8, W2), lambda b: (0, 0))],
        out_shape=[jax.ShapeDtypeStruct((L, SW), F32), jax.ShapeDtypeStruct(wb.shape, F32),
                   jax.ShapeDtypeStruct(wc.shape, F32), jax.ShapeDtypeStruct((8, W2), F32)],
        scratch_shapes=[pltpu.VMEM((tb, W2), F32), pltpu.VMEM((tb, W2), F32),
                        pltpu.VMEM((8, W2), F32), pltpu.VMEM((8, W2), F32)],
        compiler_params=_cparams(("arbitrary",)),
    )(*args)


_NN = (((1,), (0,)), ((), ()))
_NT = (((1,), (1,)), ((), ()))
_TN = (((0,), (0,)), ((), ()))


def _dotb(a, b, dn=_NN):
    return lax.dot_general(a.astype(BF16), b.astype(BF16), dn, preferred_element_type=F32)


def _split(x):
    hi = x.astype(BF16)
    return hi, (x - hi.astype(F32)).astype(BF16)


def _dot3(a, b, dn=_NN):
    ah, al = _split(a)
    bh, bl = _split(b)
    f = lambda x, y: lax.dot_general(x, y, dn, preferred_element_type=F32)
    return f(ah, bh) + (f(ah, bl) + f(al, bh))


@jax.custom_vjp
def _dot3_nn(a, b):
    return _dot3(a, b, _NN)


_dot3_nn.defvjp(lambda a, b: (_dot3(a, b, _NN), (a, b)),
                lambda res, g: (_dotb(g, res[1], _NT), _dotb(res[0], g, _TN)))


@jax.custom_vjp
def _dot3_nt(a, b):
    return _dot3(a, b, _NT)


_dot3_nt.defvjp(lambda a, b: (_dot3(a, b, _NT), (a, b)),
                lambda res, g: (_dotb(g, res[1], _NN), _dotb(g, res[0], _TN)))


def _delta_chunk(rev, one_pass_grads, *flat):
    heads = [flat[i:i + 7] for i in range(0, len(flat), 7)]
    q, k, v, beta, gc, gt, s_in = [list(t) for t in zip(*heads)]
    c, hd = q[0].shape
    each = lambda f, *ls: [f(*t) for t in zip(*ls)]
    mm_nn = _dot3_nn if one_pass_grads else _dot3
    mm_nt = _dot3_nt if one_pass_grads else (lambda x, y: _dot3(x, y, _NT))
    ii = lax.broadcasted_iota(jnp.int32, (c, c), 0)
    jj = lax.broadcasted_iota(jnp.int32, (c, c), 1)
    incl = (ii <= jj) if rev else (ii >= jj)
    strict = (ii < jj) if rev else (ii > jj)
    eye = (ii == jj).astype(F32)
    decay = each(lambda g: jnp.where(incl, jnp.exp(jnp.where(incl, g[:, :c] - jnp.transpose(g)[:c, :], 0.0)), 0.0), gc)
    kb = each(lambda a, b: a * b, k, beta)
    a = each(lambda x, y, dc: jnp.where(strict, mm_nt(x, y) * dc, 0.0), kb, k, decay)
    tinv = each(lambda x: eye - x, a)
    p = a
    n = 2
    while n < c:
        p = each(lambda x: mm_nn(x, x), p)
        tinv = each(lambda t, x: mm_nn(t, eye + x), tinv, p)
        n *= 2
    eg = each(jnp.exp, gc)
    u = each(lambda t, x, b: mm_nn(t, x * b), tinv, v, beta)
    w = each(lambda t, x, e: mm_nn(t, x * e), tinv, kb, eg)
    intra = each(lambda x, y, dc: _dotb(x, y, _NT) * dc, q, k, decay)
    v_new = each(lambda x, y, s: x - _dotb(y, s), u, w, s_in)
    o = each(lambda x, e, s, m, vn: _dotb(x * e, s) + _dotb(m, vn), q, eg, s_in, intra, v_new)
    s_out = each(lambda s, t, x, g, vn: s * jnp.exp(jnp.broadcast_to(t[0:1, :], (hd, hd)))
                 + _dotb(x * jnp.exp(t - g), vn, _TN), s_in, gt, k, gc, v_new)
    return tuple(x for pair in zip(o, s_out) for x in pair)


def _delta_fwd(q, k, v, gates, *, vcb, rev, acc, name, d):
    L, H, hd, C = q.shape[0], d['DNH'], d['DNK'], d['CHUNK']
    nc = L // C
    cix = (lambda i: nc - 1 - i) if rev else (lambda i: i)
    has_acc = acc is not None

    def body(*refs):
        if has_acc:
            q_ref, k_ref, v_ref, b_ref, gc_ref, gt_ref, acc_ref, o_ref, ss_ref, st = refs
        else:
            q_ref, k_ref, v_ref, b_ref, gc_ref, gt_ref, o_ref, ss_ref, st = refs

        @pl.when(pl.program_id(0) == 0)
        def _():
            st[...] = jnp.zeros_like(st)

        sls = [slice(h * hd, (h + 1) * hd) for h in range(H)]
        ins = [(q_ref[:, sl], k_ref[:, sl], v_ref[:, sl], b_ref[:, sl], gc_ref[:, sl], gt_ref[:, sl], st[h])
               for h, sl in enumerate(sls)]
        accv = [acc_ref[:, sl] for sl in sls] if has_acc else None
        res = _delta_chunk(rev, False, *[t for head in ins for t in head])
        for h, sl in enumerate(sls):
            o, s_out = res[2 * h], res[2 * h + 1]
            ss_ref[0, h] = ins[h][6]
            o_ref[:, sl] = o + accv[h] if has_acc else o
            st[h] = s_out

    blk = pl.BlockSpec((C, H * hd), lambda i: (cix(i), 0))
    in_specs = [blk, blk, pl.BlockSpec((C, H * hd), lambda i: (cix(i), vcb)), blk, blk, blk]
    args = [q, k, v, *gates]
    if has_acc:
        in_specs.append(blk)
        args.append(acc)
    return pl.pallas_call(
        body, name=name, grid=(nc,), in_specs=in_specs,
        out_specs=[blk, pl.BlockSpec((1, H, hd, hd), lambda i: (cix(i), 0, 0, 0))],
        out_shape=[jax.ShapeDtypeStruct((L, H * hd), F32), jax.ShapeDtypeStruct((nc, H, hd, hd), F32)],
        scratch_shapes=[pltpu.VMEM((H, hd, hd), F32)],
        compiler_params=_cparams(("arbitrary",)),
    )(*args)


def _delta_bwd(q, k, v, gates, ssave, do, *, vcb, rev, accs, name, d):
    L, H, hd, C = q.shape[0], d['DNH'], d['DNK'], d['CHUNK']
    nc = L // C
    cix = (lambda i: i) if rev else (lambda i: nc - 1 - i)
    has_acc = accs is not None

    def body(*refs):
        if has_acc:
            (q_ref, k_ref, v_ref, b_ref, gc_ref, gt_ref, ss_ref, do_ref, aq_ref, ak_ref, av_ref,
             dq_ref, dk_ref, dv_ref, db_ref, dgc_ref, dgt_ref, dst) = refs
        else:
            (q_ref, k_ref, v_ref, b_ref, gc_ref, gt_ref, ss_ref, do_ref,
             dq_ref, dk_ref, dv_ref, db_ref, dgc_ref, dgt_ref, dst) = refs

        @pl.when(pl.program_id(0) == 0)
        def _():
            dst[...] = jnp.zeros_like(dst)

        sls = [slice(h * hd, (h + 1) * hd) for h in range(H)]
        ins = [(q_ref[:, sl], k_ref[:, sl], v_ref[:, sl], b_ref[:, sl], gc_ref[:, sl], gt_ref[:, sl], ss_ref[0, h])
               for h, sl in enumerate(sls)]
        cts = tuple(t for h, sl in enumerate(sls) for t in (do_ref[:, sl], dst[h]))
        accv = [(aq_ref[:, sl], ak_ref[:, sl], av_ref[:, sl]) for sl in sls] if has_acc else None
        _, vjp = jax.vjp(functools.partial(_delta_chunk, rev, True), *[t for head in ins for t in head])
        res = vjp(cts)
        for h, sl in enumerate(sls):
            dq, dk, dv, db, dgc, dgt, ds = res[7 * h:7 * h + 7]
            dst[h] = ds
            if has_acc:
                dq, dk, dv = dq + accv[h][0], dk + accv[h][1], dv + accv[h][2]
            dq_ref[:, sl] = dq
            dk_ref[:, sl] = dk
            dv_ref[:, sl] = dv
            db_ref[:, sl] = db
            dgc_ref[:, sl] = dgc
            dgt_ref[:, sl] = dgt

    blk = pl.BlockSpec((C, H * hd), lambda i: (cix(i), 0))
    in_specs = [blk, blk, pl.BlockSpec((C, H * hd), lambda i: (cix(i), vcb)), blk, blk, blk,
                pl.BlockSpec((1, H, hd, hd), lambda i: (cix(i), 0, 0, 0)), blk]
    args = [q, k, v, *gates, ssave, do]
    if has_acc:
        in_specs += [blk, blk, blk]
        args += list(accs)
    return pl.pallas_call(
        body, name=name, grid=(nc,), in_specs=in_specs, out_specs=[blk] * 6,
        out_shape=[jax.ShapeDtypeStruct((L, H * hd), F32)] * 6,
        scratch_shapes=[pltpu.VMEM((H, hd, hd), F32)],
        compiler_params=_cparams(("arbitrary",)),
    )(*args)


def _conv_specs(tm, w, cb0, nrb, L):
    hb = tm // 8
    last8 = L // 8 - 1
    cur = pl.BlockSpec((tm, w), lambda s, i: (i, cb0 + s))
    prev = pl.BlockSpec((8, w), lambda s, i: (jnp.maximum(i * hb - 1, 0), cb0 + s))
    nxt = pl.BlockSpec((8, w), lambda s, i: (jnp.minimum((i + 1) * hb, last8), cb0 + s))
    return [prev, cur, nxt]


def _fill_halo(dst, prev_ref, cur_ref, next_ref, i, nrb, tm):
    dst[pl.ds(0, 8), :] = jnp.where(i > 0, prev_ref[...], 0.0)
    dst[pl.ds(8, tm), :] = cur_ref[...]
    dst[pl.ds(8 + tm, 8), :] = jnp.where(i < nrb - 1, next_ref[...], 0.0)


def _conv_fwd(hsrc, cb0, wt, *, tm, name, d):
    L, w, K = hsrc.shape[0], d['DW'], d['CONV']
    tm = _pick(tm, L)
    nrb = L // tm

    def body(prev_ref, cur_ref, next_ref, w_ref, o_ref, xs):
        i = pl.program_id(1)
        _fill_halo(xs, prev_ref, cur_ref, next_ref, i, nrb, tm)
        y = jnp.zeros((tm, w), F32)
        for kk in range(K):
            y = y + w_ref[0, pl.ds(kk, 1), :] * xs[pl.ds(8 - K // 2 + kk, tm), :]
        o_ref[...] = _silu(y)

    return pl.pallas_call(
        body, name=name, grid=(3, nrb),
        in_specs=_conv_specs(tm, w, cb0, nrb, L) + [pl.BlockSpec((1, 8, w), lambda s, i: (s, 0, 0))],
        out_specs=pl.BlockSpec((tm, w), lambda s, i: (i, s)),
        out_shape=jax.ShapeDtypeStruct((L, 3 * w), F32),
        scratch_shapes=[pltpu.VMEM((tm + 16, w), F32)],
        compiler_params=_cparams(("parallel", "parallel")),
    )(hsrc, hsrc, hsrc, wt)


def _conv_bwd(hsrc, cb0, wt, dact, *, tm, name, d):
    L, w, K = hsrc.shape[0], d['DW'], d['CONV']
    tm = _pick(tm, L)
    nrb = L // tm
    half = K // 2

    def body(xp_ref, xc_ref, xn_ref, gp_ref, gc_ref, gn_ref, w_ref, dx_ref, dw_ref, xs, gs, dys):
        i = pl.program_id(1)
        _fill_halo(xs, xp_ref, xc_ref, xn_ref, i, nrb, tm)
        _fill_halo(gs, gp_ref, gc_ref, gn_ref, i, nrb, tm)
        y = jnp.zeros((tm + 8, w), F32)
        for kk in range(K):
            y = y + w_ref[0, pl.ds(kk, 1), :] * xs[pl.ds(4 - half + kk, tm + 8), :]
        sg = jax.nn.sigmoid(y)
        dys[...] = gs[pl.ds(4, tm + 8), :] * (sg * (1.0 + y * (1.0 - sg)))
        dx = jnp.zeros((tm, w), F32)
        for kk in range(K):
            dx = dx + w_ref[0, pl.ds(kk, 1), :] * dys[pl.ds(4 + half - kk, tm), :]
        dx_ref[...] = dx

        @pl.when(i == 0)
        def _():
            dw_ref[...] = jnp.zeros_like(dw_ref)

        dy = dys[pl.ds(4, tm), :]
        for kk in range(K):
            dw_ref[0, pl.ds(kk, 1), :] += jnp.sum(dy * xs[pl.ds(8 - half + kk, tm), :], axis=0, keepdims=True)

    gspecs = _conv_specs(tm, w, 0, nrb, L)
    return pl.pallas_call(
        body, name=name, grid=(3, nrb),
        in_specs=_conv_specs(tm, w, cb0, nrb, L) + gspecs + [pl.BlockSpec((1, 8, w), lambda s, i: (s, 0, 0))],
        out_specs=[pl.BlockSpec((tm, w), lambda s, i: (i, s)), pl.BlockSpec((1, 8, w), lambda s, i: (s, 0, 0))],
        out_shape=[jax.ShapeDtypeStruct((L, 3 * w), F32), jax.ShapeDtypeStruct((3, 8, w), F32)],
        scratch_shapes=[pltpu.VMEM((tm + 16, w), F32), pltpu.VMEM((tm + 16, w), F32), pltpu.VMEM((tm + 8, w), F32)],
        compiler_params=_cparams(("parallel", "arbitrary")),
    )(hsrc, hsrc, hsrc, dact, dact, dact, wt)


def _wide(v, n):
    return v if n == LANE else jnp.tile(v, (1, n // LANE))


def _attn_fwd(qh, kh, vh, *, tq, tk, name, d):
    L, H, KVH, hd = qh.shape[0], d['AH'], d['AKV'], d['AD']
    grp = H // KVH
    tq, tk = _pick(tq, L), _pick(tk, L)
    nk = L // tk

    def body(q_ref, k_ref, v_ref, o_ref, lse_ref, m_s, l_s, acc):
        j = pl.program_id(2)

        @pl.when(j == 0)
        def _():
            m_s[...] = jnp.full_like(m_s, -1e30)
            l_s[...] = jnp.zeros_like(l_s)
            acc[...] = jnp.zeros_like(acc)

        s = lax.dot_general(q_ref[...], k_ref[...], _NT, preferred_element_type=F32)
        m_old = m_s[...]
        m_new = jnp.maximum(m_old, jnp.max(s, axis=-1, keepdims=True))
        alpha = jnp.exp(m_old - m_new)
        p = jnp.exp(s - _wide(m_new, tk))
        l_s[...] = alpha * l_s[...] + jnp.sum(p, axis=-1, keepdims=True)
        acc[...] = alpha * acc[...] + jnp.dot(p.astype(BF16), v_ref[...], preferred_element_type=F32)
        m_s[...] = m_new

        @pl.when(j == nk - 1)
        def _():
            o_ref[...] = acc[...] / l_s[...]
            lse_ref[...] = m_s[...] + jnp.log(l_s[...])

    qspec = pl.BlockSpec((tq, hd), lambda h, i, j: (i, h))
    kspec = pl.BlockSpec((tk, hd), lambda h, i, j: (j, h // grp))
    return pl.pallas_call(
        body, name=name, grid=(H, L // tq, nk), in_specs=[qspec, kspec, kspec], out_specs=[qspec, qspec],
        out_shape=[jax.ShapeDtypeStruct((L, H * hd), F32), jax.ShapeDtypeStruct((L, H * hd), F32)],
        scratch_shapes=[pltpu.VMEM((tq, hd), F32), pltpu.VMEM((tq, hd), F32), pltpu.VMEM((tq, hd), F32)],
        compiler_params=_cparams(("parallel", "parallel", "arbitrary")),
    )(qh, kh, vh)


def _attn_bwd(qh, kh, vh, do, lse, delta, *, tq, tk, name, d):
    L, H, KVH, hd = qh.shape[0], d['AH'], d['AKV'], d['AD']
    grp = H // KVH
    tq, tk = _pick(tq, L), _pick(tk, L)
    nk = L // tk

    def body(q_ref, k_ref, v_ref, do_ref, lse_ref, dl_ref, dq_ref, dk_ref, dv_ref, dq_s):
        g, i, j = pl.program_id(1), pl.program_id(2), pl.program_id(3)

        @pl.when(jnp.logical_and(jnp.logical_and(g == 0, i == 0), j == 0))
        def _():
            dk_ref[...] = jnp.zeros_like(dk_ref)
            dv_ref[...] = jnp.zeros_like(dv_ref)

        @pl.when(j == 0)
        def _():
            dq_s[...] = jnp.zeros_like(dq_s)

        q, k, do_ = q_ref[...], k_ref[...], do_ref[...].astype(BF16)
        s = lax.dot_general(q, k, _NT, preferred_element_type=F32)
        p = jnp.exp(s - _wide(lse_ref[...], tk))
        dp = lax.dot_general(do_, v_ref[...], _NT, preferred_element_type=F32)
        ds = (p * (dp - _wide(dl_ref[...], tk))).astype(BF16)
        dq_s[...] += jnp.dot(ds, k, preferred_element_type=F32)
        rows = pl.ds(pl.multiple_of(j * tk, tk), tk)
        dv_ref[rows, :] += lax.dot_general(p.astype(BF16), do_, _TN, preferred_element_type=F32)
        dk_ref[rows, :] += lax.dot_general(ds, q, _TN, preferred_element_type=F32)

        @pl.when(j == nk - 1)
        def _():
            dq_ref[...] = dq_s[...]

    qspec = pl.BlockSpec((tq, hd), lambda kv, g, i, j: (i, kv * grp + g))
    kspec = pl.BlockSpec((tk, hd), lambda kv, g, i, j: (j, kv))
    colspec = pl.BlockSpec((L, hd), lambda kv, g, i, j: (0, kv))
    return pl.pallas_call(
        body, name=name, grid=(KVH, grp, L // tq, nk),
        in_specs=[qspec, kspec, kspec, qspec, qspec, qspec], out_specs=[qspec, colspec, colspec],
        out_shape=[jax.ShapeDtypeStruct((L, H * hd), F32)] + [jax.ShapeDtypeStruct((L, KVH * hd), F32)] * 2,
        scratch_shapes=[pltpu.VMEM((tq, hd), F32)],
        compiler_params=_cparams(("parallel", "arbitrary", "arbitrary", "arbitrary")),
    )(qh, kh, vh, do, lse, delta)


def _loss_grad(x, g, tgt, *, tm, name):
    L, D = x.shape
    tm = _pick(tm, L)

    def body(x_ref, g_ref, t_ref, loss_ref, dx_ref, dg_ref):
        def f(xv, gv):
            err = _rms(xv, gv) - t_ref[...]
            return 0.5 * jnp.sum(jnp.mean(err * err, axis=-1, keepdims=True))

        val, vjp = jax.vjp(f, x_ref[...], g_ref[...])
        dx, dg = vjp(jnp.ones((), F32))
        dx_ref[...] = dx

        @pl.when(pl.program_id(0) == 0)
        def _():
            loss_ref[...] = jnp.zeros_like(loss_ref)
            dg_ref[...] = jnp.zeros_like(dg_ref)

        loss_ref[...] += val
        dg_ref[...] += dg

    return pl.pallas_call(
        body, name=name, grid=(L // tm,),
        in_specs=[pl.BlockSpec((tm, D), lambda i: (i, 0)), pl.BlockSpec((1, D), lambda i: (0, 0)),
                  pl.BlockSpec((tm, D), lambda i: (i, 0))],
        out_specs=[pl.BlockSpec((8, LANE), lambda i: (0, 0)), pl.BlockSpec((tm, D), lambda i: (i, 0)),
                   pl.BlockSpec((1, D), lambda i: (0, 0))],
        out_shape=[jax.ShapeDtypeStruct((8, LANE), F32), jax.ShapeDtypeStruct((L, D), F32),
                   jax.ShapeDtypeStruct((1, D), F32)],
        compiler_params=_cparams(("arbitrary",)),
    )(x, g, tgt)


def _sum_slots(recv, *, tr, name):
    n, R, W = recv.shape
    tr = _pick(tr, R)

    def body(r_ref, o_ref):
        s = r_ref[0].astype(F32)
        for i in range(1, n):
            s = s + r_ref[i].astype(F32)
        o_ref[...] = s

    return pl.pallas_call(
        body, name=name, grid=(R // tr,),
        in_specs=[pl.BlockSpec((n, tr, W), lambda i: (0, i, 0))], out_specs=pl.BlockSpec((tr, W), lambda i: (i, 0)),
        out_shape=jax.ShapeDtypeStruct((R, W), F32), compiler_params=_cparams(("parallel",)),
    )(recv)


def _adamw(w, g, m, v, *, tr, name):
    R, W = w.shape
    tr = _pick(tr, R)
    c1 = 1.0 - ADAM_B1 ** ADAM_STEP
    c2 = 1.0 - ADAM_B2 ** ADAM_STEP

    def body(w_ref, g_ref, m_ref, v_ref, d_ref, nm_ref, nv_ref):
        gv = g_ref[...]
        nm = ADAM_B1 * m_ref[...] + (1.0 - ADAM_B1) * gv
        nv = ADAM_B2 * v_ref[...] + (1.0 - ADAM_B2) * (gv * gv)
        d_ref[...] = -ADAM_LR * ((nm / c1) / (jnp.sqrt(nv / c2) + ADAM_EPS) + ADAM_WD * w_ref[...])
        nm_ref[...] = nm
        nv_ref[...] = nv

    spec = pl.BlockSpec((tr, W), lambda i: (i, 0))
    return pl.pallas_call(
        body, name=name, grid=(R // tr,), in_specs=[spec] * 4, out_specs=[spec] * 3,
        out_shape=[jax.ShapeDtypeStruct((R, W), F32)] * 3, compiler_params=_cparams(("parallel",)),
    )(w, g, m, v)


_MESH = pl.DeviceIdType.MESH


def _all_gather(xs, *, name):
    na = len(xs)

    def body(*refs):
        x_refs, out_refs = refs[:na], refs[na:2 * na]
        send_sems, recv_sems, local_sems = refs[2 * na:]
        x, y, c = lax.axis_index("x"), lax.axis_index("y"), lax.axis_index("c")
        me, sibling = (x, y, c), (x, y, 1 - c)
        chips = [(1 - x, y), (x, 1 - y), (1 - x, 1 - y)]

        def slot(a, px, py, pc):
            return out_refs[a].at[4 * px + 2 * py + pc]

        def copy(a, k, block, to, src=None):
            return pltpu.make_async_remote_copy(
                src_ref=slot(a, *block) if src is None else src, dst_ref=slot(a, *block),
                send_sem=send_sems.at[7 * a + k], recv_sem=recv_sems.at[7 * a + k], device_id=to,
                device_id_type=_MESH)

        mine = [pltpu.make_async_copy(x_refs[a], slot(a, *me), local_sems.at[a]) for a in range(na)]
        for cp in mine:
            cp.start()
        first = []
        for a in range(na):
            first.append(copy(a, 0, me, sibling, src=x_refs[a]))
            first += [copy(a, 1 + j, me, (*chip, c), src=x_refs[a]) for j, chip in enumerate(chips)]
        for cp in first:
            cp.start()
        passed = []
        for a in range(na):
            for j, chip in enumerate(chips):
                copy(a, 1 + j, (*chip, c), me).wait_recv()
                passed.append(copy(a, 4 + j, (*chip, c), sibling))
                passed[-1].start()
        for a in range(na):
            copy(a, 0, sibling, me).wait_recv()
            for j, chip in enumerate(chips):
                copy(a, 4 + j, (*chip, 1 - c), me).wait_recv()
        for cp in first + passed:
            cp.wait_send()
        for cp in mine:
            cp.wait()

    return pl.pallas_call(
        body, name=name,
        out_shape=[jax.ShapeDtypeStruct((N_DEV,) + t.shape, t.dtype) for t in xs],
        in_specs=[pl.BlockSpec(memory_space=pl.ANY)] * na, out_specs=[pl.BlockSpec(memory_space=pl.ANY)] * na,
        scratch_shapes=[pltpu.SemaphoreType.DMA((7 * na,)), pltpu.SemaphoreType.DMA((7 * na,)),
                        pltpu.SemaphoreType.DMA((na,))],
    )(*xs)


N_CHIP = N_DEV // 2


def _pair_exchange(gs, *, name):
    na = len(gs)

    def body(*refs):
        g_refs, out_refs = refs[:na], refs[na:2 * na]
        send_sems, recv_sems = refs[2 * na:]
        x, y, c = lax.axis_index("x"), lax.axis_index("y"), lax.axis_index("c")

        def copy(a, chip, core):
            return pltpu.make_async_remote_copy(
                src_ref=g_refs[a].at[2 * chip + core], dst_ref=out_refs[a].at[chip],
                send_sem=send_sems.at[N_CHIP * a + chip], recv_sem=recv_sems.at[N_CHIP * a + chip],
                device_id=(x, y, 1 - c), device_id_type=_MESH)

        sends = [copy(a, chip, 1 - c) for a in range(na) for chip in range(N_CHIP)]
        for cp in sends:
            cp.start()
        for cp in sends:
            cp.wait_recv()
        for cp in sends:
            cp.wait_send()

    return pl.pallas_call(
        body, name=name,
        out_shape=[jax.ShapeDtypeStruct((N_CHIP,) + t.shape[1:], t.dtype) for t in gs],
        in_specs=[pl.BlockSpec(memory_space=pl.ANY)] * na, out_specs=[pl.BlockSpec(memory_space=pl.ANY)] * na,
        scratch_shapes=[pltpu.SemaphoreType.DMA((N_CHIP * na,)), pltpu.SemaphoreType.DMA((N_CHIP * na,))],
    )(*gs)


def _pair_sum(g, r, core, *, out_dtype, tr, name):
    _, R, W = g.shape
    tr = _pick(tr, R)

    def body(core_ref, g_ref, r_ref, o_ref):
        o_ref[...] = (g_ref[...] + r_ref[...]).astype(o_ref.dtype)

    return pl.pallas_call(
        body, name=name,
        grid_spec=pltpu.PrefetchScalarGridSpec(
            num_scalar_prefetch=1, grid=(N_CHIP, R // tr),
            in_specs=[pl.BlockSpec((1, tr, W), lambda ch, i, core_ref: (2 * ch + core_ref[0], i, 0)),
                      pl.BlockSpec((1, tr, W), lambda ch, i, core_ref: (ch, i, 0))],
            out_specs=pl.BlockSpec((1, tr, W), lambda ch, i, core_ref: (ch, i, 0))),
        out_shape=jax.ShapeDtypeStruct((N_CHIP, R, W), out_dtype),
        compiler_params=_cparams(("parallel", "parallel")),
    )(core, g, r)


def _chip_exchange(hs, *, name):
    na = len(hs)

    def body(*refs):
        h_refs, out_refs = refs[:na], refs[na:2 * na]
        send_sems, recv_sems, local_sems = refs[2 * na:]
        x, y, c = lax.axis_index("x"), lax.axis_index("y"), lax.axis_index("c")
        me = 2 * x + y

        def copy(a, mask, started):
            px, py = x ^ (mask >> 1), y ^ (mask & 1)
            mine_, theirs = me, 2 * px + py
            return pltpu.make_async_remote_copy(
                src_ref=h_refs[a].at[theirs if started else mine_],
                dst_ref=out_refs[a].at[mine_ if started else theirs],
                send_sem=send_sems.at[3 * a + mask - 1], recv_sem=recv_sems.at[3 * a + mask - 1],
                device_id=(px, py, c), device_id_type=_MESH)

        mine = [pltpu.make_async_copy(h_refs[a].at[me], out_refs[a].at[me], local_sems.at[a]) for a in range(na)]
        for cp in mine:
            cp.start()
        sends = [copy(a, mask, True) for a in range(na) for mask in range(1, N_CHIP)]
        for cp in sends:
            cp.start()
        for a in range(na):
            for mask in range(1, N_CHIP):
                copy(a, mask, False).wait_recv()
        for cp in sends:
            cp.wait_send()
        for cp in mine:
            cp.wait()

    return pl.pallas_call(
        body, name=name,
        out_shape=[jax.ShapeDtypeStruct(t.shape, t.dtype) for t in hs],
        in_specs=[pl.BlockSpec(memory_space=pl.ANY)] * na, out_specs=[pl.BlockSpec(memory_space=pl.ANY)] * na,
        scratch_shapes=[pltpu.SemaphoreType.DMA((3 * na,)), pltpu.SemaphoreType.DMA((3 * na,)),
                        pltpu.SemaphoreType.DMA((na,))],
    )(*hs)


def _rows_of(shape):
    return -(-int(np.prod(shape)) // PACK_W)


def _pack(arrs, dtype, lead=0, total_rows=None):
    pieces = []
    for a in arrs:
        f = a.reshape(a.shape[:lead] + (-1,)).astype(dtype)
        pad = (-f.shape[-1]) % PACK_W
        if pad:
            f = jnp.pad(f, [(0, 0)] * lead + [(0, pad)])
        pieces.append(f.reshape(a.shape[:lead] + (-1, PACK_W)))
    buf = jnp.concatenate(pieces, axis=lead)
    if total_rows is not None and buf.shape[lead] < total_rows:
        buf = jnp.pad(buf, [(0, 0)] * lead + [(0, total_rows - buf.shape[lead]), (0, 0)])
    return buf


def _unpack(buf, shapes, lead=0):
    out, r = [], 0
    for shp in shapes:
        n, rows = int(np.prod(shp)), _rows_of(shp)
        piece = buf[(slice(None),) * lead + (slice(r, r + rows),)]
        piece = piece.reshape(buf.shape[:lead] + (-1,))[..., :n]
        out.append(piece.reshape(buf.shape[:lead] + tuple(shp)))
        r += rows
    return out


def _to_full(parts):
    dep, r = parts.shape[1:3]
    return jnp.transpose(parts, (1, 0) + tuple(range(2, parts.ndim))).reshape((dep, N_DEV * r) + parts.shape[3:])


def _to_slabs(full):
    dep, r = full.shape[:2]
    t = full.reshape((dep, N_DEV, r // N_DEV) + full.shape[2:])
    return jnp.transpose(t, (1, 0) + tuple(range(2, t.ndim)))


def _ref_cols(parts, ro, wd):
    w, out = parts.shape[2], []
    for dev in range(N_DEV):
        lo, hi = max(ro, dev * w), min(ro + wd, (dev + 1) * w)
        if lo < hi:
            out.append(parts[dev][:, lo - dev * w:hi - dev * w])
    return out


def _w_in_to_layout(parts, seg, rseg, nh2):
    D = parts.shape[1]
    cols, off = [], 0
    names = sorted([k for k in seg if not k.startswith('_')], key=lambda k: seg[k][0])
    for nm in names:
        o, wd = seg[nm]
        if o > off:
            cols.append(jnp.zeros((D, o - off), parts.dtype))
        if nm == 'dadb':
            cols += _ref_cols(parts, rseg['da'][0], nh2) + _ref_cols(parts, rseg['db'][0], nh2)
            cols.append(jnp.zeros((D, wd - 2 * nh2), parts.dtype))
        else:
            cols += _ref_cols(parts, rseg[nm][0], wd)
        off = o + wd
    if seg['_total'] > off:
        cols.append(jnp.zeros((D, seg['_total'] - off), parts.dtype))
    return jnp.concatenate(cols, axis=1)


def _w_in_slabs(dw, seg, rseg, nh2):
    w = rseg['_total'] // N_DEV
    ref = []
    for nm in sorted([k for k in rseg if not k.startswith('_')], key=lambda k: rseg[k][0]):
        lo = {'da': seg['dadb'][0], 'db': seg['dadb'][0] + nh2}.get(nm)
        ref.append((rseg[nm][0], rseg[nm][1], seg[nm][0] if lo is None else lo))
    slabs = []
    for dev in range(N_DEV):
        cols = []
        for ro, wd, lo in ref:
            a, b = max(ro, dev * w), min(ro + wd, (dev + 1) * w)
            if a < b:
                cols.append(dw[:, lo + a - ro:lo + b - ro])
        slabs.append(jnp.concatenate(cols, axis=1))
    return jnp.stack(slabs, axis=0)


def _assemble_dh(pieces, seg, L):
    cols, off = [], 0
    for nm in sorted(pieces, key=lambda k: seg[k][0]):
        o = seg[nm][0]
        if o > off:
            cols.append(jnp.zeros((L, o - off), F32))
        cols.append(pieces[nm])
        off = o + pieces[nm].shape[1]
    if seg['_total'] > off:
        cols.append(jnp.zeros((L, seg['_total'] - off), F32))
    return jnp.concatenate(cols, axis=1)


def _lane_pad(v):
    v = v.reshape(1, -1)
    return jnp.pad(v, ((0, 0), (0, LANE - v.shape[1])))


def _rope_tables(L, c):
    rows = L // c['GRID_W']
    row = jnp.repeat(jnp.arange(rows), c['GRID_W']).astype(F32)
    col = jnp.tile(jnp.arange(c['GRID_W']), rows).astype(F32)
    axis_dim = c['AD'] // 2
    freqs = c['ROPE_THETA'] ** (-jnp.arange(0, axis_dim, 2, dtype=F32) / axis_dim)
    ang = jnp.concatenate([row[:, None] * freqs, col[:, None] * freqs], axis=-1)
    cosf = jnp.repeat(jnp.cos(ang), 2, axis=1)
    sn = jnp.sin(ang)
    sins = jnp.stack([-sn, sn], axis=-1).reshape(L, c['AD'])
    idx = np.arange(c['AD'])
    perm = np.zeros((c['AD'], c['AD']), np.float32)
    perm[idx, idx ^ 1] = 1.0
    return cosf, sins, jnp.asarray(perm)


def _s5_dir_params(a, l, dr):
    return (a['ssm_a_re'][l, dr], a['ssm_a_im'][l, dr], a['ssm_log_step'][l, dr], a['ssm_b_re'][l, dr],
            a['ssm_b_im'][l, dr], a['ssm_c_re'][l, dr], a['ssm_c_im'][l, dr])


def _layer_fwd(x, mem, l, wt, a, rope, c, d, seg):
    L, D = x.shape
    SW, DW, AW, AKW, MW, H = d['SW'], d['DW'], d['AW'], d['AKW'], d['MW'], d['DNH']
    cb = lambda nm: seg[nm][0] // seg[nm][1]
    sv = {'x': x}
    p = f"l{l}_"
    sv['g_norm'] = a['norm_g'][l][None, :]
    xn, = _rowwise(_f_norm, [(x, D, 0)], [sv['g_norm']], [(D, BF16)], tm=256, name=p + "norm")
    h = _mm(xn, wt['wp'], name=p + "in_proj", tm=1024, tn=1536, tk=1024)
    sv['xn'], sv['h'] = xn, h

    ysum, sv['s5'] = None, []
    for dr in range(2):
        wb, wc, lr, li = _s5_prep(*_s5_dir_params(a, l, dr), d)
        wb16, wc16 = wb.astype(BF16), wc.astype(BF16)
        lt = _s5_tables(lr, li, bool(dr), False)
        ysum, cin = _s5_fwd(h, cb('u_a'), wb16, wc16, lt, rev=bool(dr), acc=ysum, tb=TILES['s5_t'],
                            name=p + f"s5_fwd{dr}", d=d)
        sv['s5'].append((wb16, wc16, lt, _s5_tables(lr, li, not bool(dr), True), cin))
    sv['ysum'] = ysum
    sv['s5_par'] = [a['ssm_d'][l][None, :], wt['w_glu'], a['ssm_b_glu'][l][None, :]]
    sv['s5_rows'] = [(ysum, SW, 0), (h, SW, cb('u_a')), (h, SW, cb('z_a'))]
    y_a, = _rowwise(_f_s5tail, sv['s5_rows'], sv['s5_par'], [(SW, F32)], tm=256, name=p + "s5_tail")

    act = _conv_fwd(h, cb('dq'), wt['conv'], tm=256, name=p + "dn_conv", d=d)
    sv['act'] = act
    sv['dn_par'] = [_lane_pad(a['dn_a_log'][l]), _lane_pad(a['dn_dt_bias'][l])]
    sv['dn_rows'] = [(act, DW, 0), (act, DW, 1), (h, LANE, seg['dadb'][0] // LANE)]
    dn_out = _rowwise(_make_f_dnpre(H, d['DNK'], c['CHUNK']), sv['dn_rows'], sv['dn_par'], [(DW, F32)] * 8,
                      tm=256, name=p + "dn_pre")
    qn, kn = dn_out[:2]
    sv['qn'], sv['kn'], sv['gates'] = qn, kn, [dn_out[2:5], dn_out[5:8]]
    o_dn, sv['dn_state'] = None, []
    for dr in range(2):
        o_dn, ss = _delta_fwd(qn, kn, act, sv['gates'][dr], vcb=2, rev=bool(dr), acc=o_dn,
                              name=p + f"dn_fwd{dr}", d=d)
        sv['dn_state'].append(ss)
    sv['dnpost_rows'] = [(o_dn, DW, 0), (h, DW, cb('z_b'))]
    sv['dnpost_par'] = [a['dn_norm_g'][l][None, :]]
    y_b, = _rowwise(_make_f_dnpost(d['DNK']), sv['dnpost_rows'], sv['dnpost_par'], [(DW, F32)], tm=256,
                    name=p + "dn_post")

    cosf, sins, perm = rope
    sv['att_par'] = [perm, a['attn_q_norm'][l][None, :], a['attn_k_norm'][l][None, :]]
    qh, kh, vh = _rowwise(_make_f_attpre(d['AD'], True),
                          [(h, AW, cb('aq')), (h, AKW, cb('ak')), (h, AKW, cb('av')), (cosf, d['AD'], 0),
                           (sins, d['AD'], 0)], sv['att_par'], [(AW, BF16), (AKW, BF16), (AKW, BF16)],
                          tm=256, name=p + "att_pre")
    o_att, lse = _attn_fwd(qh, kh, vh, tq=TILES['att_q'], tk=TILES['att_k'], name=p + "att_fwd", d=d)
    sv['qh'], sv['kh'], sv['vh'], sv['o_att'], sv['lse'] = qh, kh, vh, o_att, lse
    y_c, = _rowwise(_f_gate, [(o_att, AW, 0), (h, AW, cb('z_c'))], [], [(AW, F32)], tm=256, name=p + "att_post")

    sv['g_mem'] = a['mem_norm_g'][l][None, :]
    memn, = _rowwise(_f_norm, [(mem, D, 0)], [sv['g_mem']], [(D, BF16)], tm=256, name=p + "mem_norm")
    kv = _mm(memn, wt['w_mem_kv'], name=p + "mem_kv")
    sv['memn'], sv['kv'] = memn, kv
    y_m, = _rowwise(_make_f_mem(d['MH'], d['MD']), [(h, MW, cb('mq')), (h, MW, cb('z_m'))], [kv], [(MW, F32)],
                    tm=256, name=p + "mem_attn")

    ys = [y_a, y_b, y_c, y_m]
    ps = [_mm(y, wb_, name=p + f"branch_proj{i}", out_dtype=BF16)
          for i, (y, wb_) in enumerate(zip(ys, wt['w_branch']))]
    gcb = seg['gates'][0] // D
    sv['merge_rows'] = [(pp, D, 0) for pp in ps] + [(h, D, gcb + i) for i in range(4)]
    merged, = _rowwise(_f_merge, sv['merge_rows'], [], [(D, BF16)], tm=128, name=p + "merge")
    sv['ys'], sv['merged'] = ys, merged
    return _mm(merged, wt['w_out'], add=x, name=p + "out_proj"), sv


def _layer_bwd(dx, mem, l, wt, a, rope, sv, c, d, seg):
    L, D = dx.shape
    SW, DW, AW, AKW, MW, H = d['SW'], d['DW'], d['AW'], d['AKW'], d['MW'], d['DNH']
    cb = lambda nm: seg[nm][0] // seg[nm][1]
    p = f"l{l}_"
    h = sv['h']
    gr = {}
    dmerged = _mm(dx, wt['w_out'], tb=True, name=p + "d_merged")
    gr['w_out'] = _mm(sv['merged'], dx, ta=True, name=p + "dw_out")
    dmr, _ = _rowwise_bwd(_f_merge, sv['merge_rows'], [], [[(dmerged, D, 0)]], [True] * 8, [], tm=128,
                          name=p + "merge_bwd", row_grad_dtype=BF16)
    dps, dgates = dmr[:4], dmr[4:]
    dys = [_mm(dp, wb_, tb=True, name=p + f"d_branch{i}") for i, (dp, wb_) in enumerate(zip(dps, wt['w_branch']))]
    gr['w_branch'] = jnp.concatenate(
        [_mm(y, dp, ta=True, name=p + f"dw_branch{i}") for i, (y, dp) in enumerate(zip(sv['ys'], dps))], axis=0)

    (dmq, dzm), (dkv,) = _rowwise_bwd(_make_f_mem(d['MH'], d['MD']), [(h, MW, cb('mq')), (h, MW, cb('z_m'))],
                                      [sv['kv']], [[(dys[3], MW, 0)]], [True, True], [True], tm=256,
                                      name=p + "mem_attn_bwd")
    gr['w_mem_kv'] = _mm(sv['memn'], dkv, ta=True, name=p + "dw_mem_kv")
    dmemn = _mm(dkv, wt['w_mem_kv'], tb=True, name=p + "d_memn")
    _, (dg_mem,) = _rowwise_bwd(_f_norm, [(mem, D, 0)], [sv['g_mem']], [[(dmemn, D, 0)]], [False], [True], tm=256,
                                name=p + "mem_norm_bwd")
    gr['mem_norm_g'] = dg_mem[0]

    (do_att, dzc), _ = _rowwise_bwd(_f_gate, [(sv['o_att'], AW, 0), (h, AW, cb('z_c'))], [], [[(dys[2], AW, 0)]],
                                    [True, True], [], tm=256, name=p + "att_post_bwd")
    delta, = _rowwise(_make_f_delta(d['AD']), [(do_att, AW, 0), (sv['o_att'], AW, 0)], [], [(AW, F32)], tm=256,
                      name=p + "att_delta")
    att_in = (sv['qh'], sv['kh'], sv['vh'], do_att, sv['lse'], delta)
    dqh, dkh, dvh = _attn_bwd(*att_in, tq=TILES['att_q'], tk=TILES['att_k'], name=p + "att_bwd", d=d)
    cosf, sins, _ = rope
    (daq, dak), (dqg, dkg) = _rowwise_bwd(
        _make_f_attpre(d['AD'], False),
        [(h, AW, cb('aq')), (h, AKW, cb('ak')), (cosf, d['AD'], 0), (sins, d['AD'], 0)], sv['att_par'],
        [[(dqh, AW, 0)], [(dkh, AKW, 0)]], [True, True, False, False], [False, True, True], tm=256,
        name=p + "att_pre_bwd")
    gr['attn_q_norm'], gr['attn_k_norm'] = dqg[0], dkg[0]

    (do_dn, dzb), (dng,) = _rowwise_bwd(_make_f_dnpost(d['DNK']), sv['dnpost_rows'], sv['dnpost_par'],
                                        [[(dys[1], DW, 0)]], [True, True], [True], tm=256, name=p + "dn_post_bwd")
    gr['dn_norm_g'] = dng[0]
    accs, dn_dgates = None, []
    for dr in range(2):
        res = _delta_bwd(sv['qn'], sv['kn'], sv['act'], sv['gates'][dr], sv['dn_state'][dr], do_dn, vcb=2,
                         rev=bool(dr), accs=accs, name=p + f"dn_bwd{dr}", d=d)
        accs = res[:3]
        dn_dgates += res[3:]
    dqn, dkn, dvc = accs
    (dqc, dkc, ddadb), (dalog, ddtb) = _rowwise_bwd(
        _make_f_dnpre(H, d['DNK'], c['CHUNK']), sv['dn_rows'], sv['dn_par'],
        [[(t, DW, 0)] for t in [dqn, dkn] + dn_dgates], [True] * 3, [True, True], tm=256, name=p + "dn_pre_bwd")
    gr['dn_a_log'] = dalog[0, :2 * H].reshape(2, H)
    gr['dn_dt_bias'] = ddtb[0, :2 * H].reshape(2, H)
    dconv_x, dconv_w = _conv_bwd(h, cb('dq'), wt['conv'], jnp.concatenate([dqc, dkc, dvc], axis=1), tm=256,
                                 name=p + "dn_conv_bwd", d=d)
    gr['dn_conv'] = jnp.transpose(dconv_w[:, :c['CONV'], :], (0, 2, 1)).reshape(3 * DW, c['CONV'])

    (dysum, du, dza), (dd, dwglu, dbglu) = _rowwise_bwd(_f_s5tail, sv['s5_rows'], sv['s5_par'], [[(dys[0], SW, 0)]],
                                                        [True] * 3, [True] * 3, tm=256, name=p + "s5_tail_bwd")
    gr['ssm_d'], gr['ssm_w_glu'], gr['ssm_b_glu'] = dd[0], dwglu, dbglu[0]
    s5g = []
    for dr in range(2):
        wb16, wc16, lt, lt_adj, cin = sv['s5'][dr]
        du, dwb, dwc, dlam = _s5_bwd(h, cb('u_a'), dysum, cin, wb16, wc16, lt, lt_adj, rev=bool(dr), acc=du,
                                     tb=TILES['s5_t'],
                                     name=p + f"s5_bwd{dr}", d=d)
        dl = jnp.sum(dlam, axis=0).reshape(d['NB'], 2, d['BS'])
        _, prep_vjp = jax.vjp(lambda *pp: _s5_prep(*pp, d), *_s5_dir_params(a, l, dr))
        s5g.append(prep_vjp((dwb, dwc, dl[:, 0], dl[:, 1])))
    for i, nm in enumerate(['ssm_a_re', 'ssm_a_im', 'ssm_log_step', 'ssm_b_re', 'ssm_b_im', 'ssm_c_re', 'ssm_c_im']):
        gr[nm] = jnp.stack([s5g[0][i], s5g[1][i]], axis=0)

    dh = _assemble_dh({'u_a': du, 'z_a': dza, 'dq': dconv_x, 'z_b': dzb, 'ak': dak, 'av': dvh, 'aq': daq,
                       'z_c': dzc, 'mq': dmq, 'z_m': dzm, 'gates': jnp.concatenate(dgates, axis=1),
                       'dadb': ddadb}, seg, L).astype(BF16)
    gr['wp'] = _mm(sv['xn'], dh, ta=True, name=p + "dw_in", tm=1024, tn=1536, tk=1024)
    dxn = _mm(dh, wt['wp'], tb=True, name=p + "d_xn", tm=1024, tn=1024, tk=1536)
    (dx_in,), (dg_norm,) = _rowwise_bwd(_f_norm, [(sv['x'], D, 0)], [sv['g_norm']], [[(dxn, D, 0)]], [True], [True],
                                        tm=256, name=p + "norm_bwd", accs={0: (dx, D, 0)})
    gr['norm_g'] = dg_norm[0]
    return dx_in, gr


_ARG_NAMES = (['x', 'mem'] + WEIGHTS + ['loss_target'] + ['m_' + w for w in WEIGHTS] + ['v_' + w for w in WEIGHTS])


def kernel(x, mem, norm_g, w_in, ssm_a_re, ssm_a_im, ssm_log_step, ssm_b_re, ssm_b_im, ssm_c_re, ssm_c_im,
           ssm_d, ssm_w_glu, ssm_b_glu, dn_conv, dn_a_log, dn_dt_bias, dn_norm_g, attn_q_norm, attn_k_norm,
           mem_norm_g, w_mem_kv, w_branch, w_out, final_norm_g, loss_target, m_norm_g, m_w_in, m_ssm_a_re,
           m_ssm_a_im, m_ssm_log_step, m_ssm_b_re, m_ssm_b_im, m_ssm_c_re, m_ssm_c_im, m_ssm_d, m_ssm_w_glu,
           m_ssm_b_glu, m_dn_conv, m_dn_a_log, m_dn_dt_bias, m_dn_norm_g, m_attn_q_norm, m_attn_k_norm,
           m_mem_norm_g, m_w_mem_kv, m_w_branch, m_w_out, m_final_norm_g, v_norm_g, v_w_in, v_ssm_a_re,
           v_ssm_a_im, v_ssm_log_step, v_ssm_b_re, v_ssm_b_im, v_ssm_c_re, v_ssm_c_im, v_ssm_d, v_ssm_w_glu,
           v_ssm_b_glu, v_dn_conv, v_dn_a_log, v_dn_dt_bias, v_dn_norm_g, v_attn_q_norm, v_attn_k_norm,
           v_mem_norm_g, v_w_mem_kv, v_w_branch, v_w_out, v_final_norm_g):
    given = locals()
    return _train_step({n: given[n] for n in _ARG_NAMES})


def _train_step(a):
    c = CFG
    d = _dims(c)
    seg, rseg = _layout(c)
    depth, nh2 = c['DEPTH'], 2 * c['DNH']
    x, mem, tgt = a['x'][0], a['mem'][0], a['loss_target'][0]
    L, D = x.shape

    packed = [n for n in SHARDED if n != 'w_in']
    shard_shapes = [a[n].shape for n in packed]
    rw = _round_up(sum(_rows_of(s) for s in shard_shapes), LANE)
    win_shape = a['w_in'].shape
    wcols = win_shape[2]
    g_win, gathered = _all_gather([a['w_in'].astype(BF16).reshape(depth * D, wcols),
                                   _pack([a[n] for n in packed], BF16, total_rows=rw)], name="weights_all_gather")
    full = {n: _to_full(p_) for n, p_ in zip(packed, _unpack(gathered, shard_shapes, lead=1))}
    offs = np.cumsum([0, d['SW'], d['DW'], d['AW'], d['MW']])
    wts = []
    for l in range(depth):
        conv = jnp.transpose(full['dn_conv'][l].astype(F32).reshape(3, d['DW'], c['CONV']), (0, 2, 1))
        wts.append(dict(
            wp=_w_in_to_layout(g_win[:, l * D:(l + 1) * D], seg, rseg, nh2),
            w_branch=[full['w_branch'][l, offs[i]:offs[i + 1]] for i in range(4)],
            w_out=full['w_out'][l], w_mem_kv=full['w_mem_kv'][l], w_glu=full['ssm_w_glu'][l].astype(F32),
            conv=jnp.pad(conv, ((0, 0), (0, 8 - c['CONV']), (0, 0)))))
    rope = _rope_tables(L, c)

    saved = []
    for l in range(depth):
        x, sv = _layer_fwd(x, mem, l, wts[l], a, rope, c, d, seg)
        saved.append(sv)
    loss_part, dx, dg_final = _loss_grad(x, a['final_norm_g'][None, :], tgt, tm=256, name="final_norm_loss")
    grads = [None] * depth
    for l in reversed(range(depth)):
        dx, grads[l] = _layer_bwd(dx, mem, l, wts[l], a, rope, saved[l], c, d, seg)

    gfull = {n: jnp.stack([grads[l][n] for l in range(depth)], axis=0) for n in WEIGHTS
             if n not in ('w_in', 'final_norm_g')}
    gfull['final_norm_g'] = dg_final[0]

    win_slabs = jnp.concatenate([_w_in_slabs(grads[l]['wp'], seg, rseg, nh2) for l in range(depth)], axis=1)
    small_shapes = [a[n].shape for n in SMALL] + [(1,)]
    rs = _round_up(sum(_rows_of(s) for s in small_shapes), LANE)
    g_shard = _pack([_to_slabs(gfull[n]) for n in packed], F32, lead=1, total_rows=rw)
    g_small = _pack([gfull[n] for n in SMALL] + [loss_part[0, :1]], F32, total_rows=rs)
    slabs = [win_slabs, g_shard, jnp.broadcast_to(g_small[None], (N_DEV,) + g_small.shape)]
    core = lax.axis_index("c").astype(jnp.int32).reshape(1)
    from_sibling = _pair_exchange(slabs, name="grads_pair_exchange")
    pair_sums = [_pair_sum(g, r, core, out_dtype=dt, tr=256, name=f"grads_pair_sum{i}")
                 for i, (g, r, dt) in enumerate(zip(slabs, from_sibling, (BF16, BF16, F32)))]
    recv = _chip_exchange(pair_sums, name="grads_chip_exchange")
    g_win_sum = _sum_slots(recv[0], tr=256, name="w_in_grad_sum")
    gsum = jnp.concatenate([_sum_slots(recv[1], tr=256, name="shard_grad_sum"),
                            _sum_slots(recv[2], tr=256, name="small_grad_sum")], axis=0)
    flat = lambda t: t.reshape(depth * D, wcols)
    d_win, m_win, v_win = _adamw(flat(a['w_in']), g_win_sum, flat(a['m_w_in']), flat(a['v_w_in']), tr=256,
                                 name="w_in_adamw")
    win_out = [t.reshape(win_shape) for t in (g_win_sum, d_win, m_win, v_win)]

    def local_pack(prefix):
        zero = jnp.zeros((1,), F32)
        return jnp.concatenate([_pack([a[prefix + n] for n in packed], F32, total_rows=rw),
                                _pack([a[prefix + n] for n in SMALL] + [zero], F32, total_rows=rs)], axis=0)

    delta, new_m, new_v = _adamw(local_pack(''), gsum, local_pack('m_'), local_pack('v_'), tr=256, name="adamw")

    def split(buf):
        vals = dict(zip(packed, _unpack(buf[:rw], shard_shapes)))
        small = _unpack(buf[rw:], small_shapes)
        vals.update(zip(SMALL, small[:-1]))
        return vals, small[-1]

    _, loss = split(gsum)
    outs = [loss.reshape(()), dx[None]]
    for i, buf in enumerate((gsum, delta, new_m, new_v)):
        vals, _ = split(buf)
        vals['w_in'] = win_out[i]
        outs += [vals[n] for n in WEIGHTS]
    return tuple(outs)
```

```python
import functools
import math

import numpy as np
import jax
import jax.numpy as jnp
from jax import lax
from jax.experimental import pallas as pl
from jax.experimental.pallas import tpu as pltpu

F32 = jnp.float32
BF16 = jnp.bfloat16
HI = lax.Precision.HIGHEST
EPS = 1e-6
LANE = 128
SUBLANE = 8
VMEM_LIMIT = 56 * 1024 * 1024
N_DEV = 8
PACK_W = 1024

ADAM_LR, ADAM_B1, ADAM_B2, ADAM_EPS, ADAM_WD, ADAM_STEP = 0.001, 0.9, 0.999, 1e-08, 0.01, 10

CFG = dict(D=2048, L=8192, GRID_W=64, NMEM=256, DEPTH=2,
           SG=48, SP=16, SN=64,
           DNH=6, DNK=128, CONV=5, CHUNK=64,
           AH=8, AKV=2, AD=128, ROPE_THETA=10000.0,
           MH=4, MD=128)

TILES = dict(att_q=1024, att_k=2048, s5_t=512)

WEIGHTS = ['norm_g', 'w_in', 'ssm_a_re', 'ssm_a_im', 'ssm_log_step', 'ssm_b_re', 'ssm_b_im', 'ssm_c_re',
           'ssm_c_im', 'ssm_d', 'ssm_w_glu', 'ssm_b_glu', 'dn_conv', 'dn_a_log', 'dn_dt_bias', 'dn_norm_g',
           'attn_q_norm', 'attn_k_norm', 'mem_norm_g', 'w_mem_kv', 'w_branch', 'w_out', 'final_norm_g']
SHARDED = ['w_in', 'w_branch', 'w_out', 'w_mem_kv', 'ssm_w_glu', 'dn_conv']
SMALL = [w for w in WEIGHTS if w not in SHARDED]


def _dims(c):
    d = dict(c)
    d['SW'] = c['SG'] * c['SP']
    d['NB'] = d['SW'] // LANE
    d['GPB'] = LANE // c['SP']
    d['BS'] = d['GPB'] * c['SN']
    d['DW'] = c['DNH'] * c['DNK']
    d['AW'] = c['AH'] * c['AD']
    d['AKW'] = c['AKV'] * c['AD']
    d['MW'] = c['MH'] * c['MD']
    d['BT'] = d['SW'] + d['DW'] + d['AW'] + d['MW']
    return d


def _round_up(a, b):
    return (a + b - 1) // b * b


def _layout(c):
    d = _dims(c)
    D, SW, DW, AW, AKW, MW = d['D'], d['SW'], d['DW'], d['AW'], d['AKW'], d['MW']
    order = [('u_a', SW, SW), ('z_a', SW, SW), ('dq', DW, DW), ('dk', DW, DW), ('dv', DW, DW), ('z_b', DW, DW),
             ('ak', AKW, AKW), ('av', AKW, AKW), ('aq', AW, AW), ('z_c', AW, AW), ('mq', MW, MW), ('z_m', MW, MW),
             ('gates', 4 * D, D), ('dadb', LANE, LANE)]
    off, seg = 0, {}
    for name, w, al in order:
        off = _round_up(off, al)
        seg[name] = (off, w)
        off += w
    seg['_total'] = _round_up(off, 512)
    ref_order = [('u_a', SW), ('z_a', SW), ('dq', DW), ('dk', DW), ('dv', DW), ('da', 2 * d['DNH']),
                 ('db', 2 * d['DNH']), ('z_b', DW), ('aq', AW), ('ak', AKW), ('av', AKW), ('z_c', AW),
                 ('mq', MW), ('z_m', MW), ('gates', 4 * D)]
    roff, rseg = 0, {}
    for name, w in ref_order:
        rseg[name] = (roff, w)
        roff += w
    rseg['_total'] = roff
    return seg, rseg


def _cparams(sem):
    return pltpu.CompilerParams(dimension_semantics=sem, vmem_limit_bytes=VMEM_LIMIT)


def _pick(t, n):
    if n <= t:
        return n
    for align in (LANE, 2 * SUBLANE):
        for cand in range(t - t % align, 0, -align):
            if n % cand == 0:
                return cand
    return n


def _mm(a, b, *, name, ta=False, tb=False, add=None, out_dtype=F32, tm=1024, tn=1024, tk=512):
    M, K = (a.shape[1], a.shape[0]) if ta else a.shape
    N = b.shape[0] if tb else b.shape[1]
    assert (b.shape[1] if tb else b.shape[0]) == K
    tm, tn, tk = _pick(tm, M), _pick(tn, N), _pick(tk, K)
    nk = K // tk
    dn = (((0 if ta else 1,), (1 if tb else 0,)), ((), ()))
    has_add = add is not None

    def body(*refs):
        if has_add:
            a_ref, b_ref, add_ref, o_ref, acc = refs
        else:
            a_ref, b_ref, o_ref, acc = refs
        k = pl.program_id(2)

        @pl.when(k == 0)
        def _():
            acc[...] = jnp.zeros_like(acc)

        acc[...] += lax.dot_general(a_ref[...].astype(BF16), b_ref[...].astype(BF16), dn,
                                    preferred_element_type=F32)

        @pl.when(k == nk - 1)
        def _():
            r = acc[...]
            if has_add:
                r = r + add_ref[...]
            o_ref[...] = r.astype(o_ref.dtype)

    a_spec = pl.BlockSpec((tk, tm), lambda i, j, k: (k, i)) if ta else pl.BlockSpec((tm, tk), lambda i, j, k: (i, k))
    b_spec = pl.BlockSpec((tn, tk), lambda i, j, k: (j, k)) if tb else pl.BlockSpec((tk, tn), lambda i, j, k: (k, j))
    in_specs = [a_spec, b_spec]
    args = [a, b]
    if has_add:
        in_specs.append(pl.BlockSpec((tm, tn), lambda i, j, k: (i, j)))
        args.append(add)
    return pl.pallas_call(
        body, name=name, grid=(M // tm, N // tn, nk),
        in_specs=in_specs, out_specs=pl.BlockSpec((tm, tn), lambda i, j, k: (i, j)),
        out_shape=jax.ShapeDtypeStruct((M, N), out_dtype),
        scratch_shapes=[pltpu.VMEM((tm, tn), F32)],
        compiler_params=_cparams(("parallel", "parallel", "arbitrary")),
    )(*args)


def _row_spec(tm, w, cb):
    return pl.BlockSpec((tm, w), lambda i, cb=cb: (i, cb))


def _rowwise(fn, rows, params, outs, *, tm, name):
    L = rows[0][0].shape[0]
    tm = _pick(tm, L)
    nr, npar = len(rows), len(params)

    def body(*refs):
        vals = [r[...] for r in refs[:nr + npar]]
        res = fn(*vals)
        for o_ref, v in zip(refs[nr + npar:], res):
            o_ref[...] = v.astype(o_ref.dtype)

    in_specs = [_row_spec(tm, w, cb) for (_, w, cb) in rows]
    in_specs += [pl.BlockSpec(p.shape, lambda i: (0, 0)) for p in params]
    res = pl.pallas_call(
        body, name=name, grid=(L // tm,), in_specs=in_specs,
        out_specs=[pl.BlockSpec((tm, w), lambda i: (i, 0)) for (w, _) in outs],
        out_shape=[jax.ShapeDtypeStruct((L, w), dt) for (w, dt) in outs],
        compiler_params=_cparams(("parallel",)),
    )(*[r[0] for r in rows], *params)
    return list(res)


def _rowwise_bwd(fn, rows, params, cts, drows, dparams, *, tm, name, accs=None, row_grad_dtype=F32):
    L = rows[0][0].shape[0]
    tm = _pick(tm, L)
    nr, npar = len(rows), len(params)
    accs = accs or {}
    ct_flat = [c for grp in cts for c in grp]
    ct_sizes = [len(grp) for grp in cts]
    acc_keys = sorted(accs)
    d_r = [i for i in range(nr) if drows[i]]
    d_p = [i for i in range(npar) if dparams[i]]
    n_in = nr + npar + len(ct_flat) + len(acc_keys)

    def body(*refs):
        vals = [r[...] for r in refs[:nr + npar]]
        ct_refs = refs[nr + npar:nr + npar + len(ct_flat)]
        acc_refs = refs[nr + npar + len(ct_flat):n_in]
        o_refs = refs[n_in:]
        ct_vals, pos = [], 0
        for n in ct_sizes:
            v = ct_refs[pos][...].astype(F32)
            for r in ct_refs[pos + 1:pos + n]:
                v = v + r[...].astype(F32)
            ct_vals.append(v)
            pos += n
        diff_idx = d_r + [nr + i for i in d_p]

        def g(*dv):
            full = list(vals)
            for i, v in zip(diff_idx, dv):
                full[i] = v
            return tuple(o.astype(F32) for o in fn(*full))

        _, vjp = jax.vjp(g, *[vals[i] for i in diff_idx])
        grads = vjp(tuple(ct_vals))
        for n, i in enumerate(d_r):
            gv = grads[n].astype(F32)
            if i in accs:
                gv = gv + acc_refs[acc_keys.index(i)][...]
            o_refs[n][...] = gv.astype(o_refs[n].dtype)
        step = pl.program_id(0)
        for n, i in enumerate(d_p):
            o_ref = o_refs[len(d_r) + n]

            @pl.when(step == 0)
            def _(o_ref=o_ref):
                o_ref[...] = jnp.zeros_like(o_ref)

            o_ref[...] += grads[len(d_r) + n].astype(F32)

    in_specs = [_row_spec(tm, w, cb) for (_, w, cb) in rows]
    in_specs += [pl.BlockSpec(p.shape, lambda i: (0, 0)) for p in params]
    in_specs += [_row_spec(tm, w, cb) for (_, w, cb) in ct_flat]
    in_specs += [_row_spec(tm, accs[k][1], accs[k][2]) for k in acc_keys]
    out_specs = [pl.BlockSpec((tm, rows[i][1]), lambda i_: (i_, 0)) for i in d_r]
    out_specs += [pl.BlockSpec(params[i].shape, lambda i_: (0, 0)) for i in d_p]
    out_shape = [jax.ShapeDtypeStruct((L, rows[i][1]), row_grad_dtype) for i in d_r]
    out_shape += [jax.ShapeDtypeStruct(params[i].shape, F32) for i in d_p]
    res = pl.pallas_call(
        body, name=name, grid=(L // tm,), in_specs=in_specs, out_specs=out_specs, out_shape=out_shape,
        compiler_params=_cparams(("arbitrary",)),
    )(*[r[0] for r in rows], *params, *[c[0] for c in ct_flat], *[accs[k][0] for k in acc_keys])
    res = list(res)
    return res[:len(d_r)], res[len(d_r):]


def _silu(x):
    return x * jax.nn.sigmoid(x)


def _rms(x, g):
    return x * lax.rsqrt(jnp.mean(x * x, axis=-1, keepdims=True) + EPS) * g


def _softplus(x):
    return jnp.maximum(x, 0.0) + jnp.log1p(jnp.exp(-jnp.abs(x)))


def _heads(x, hd):
    return [x[:, i * hd:(i + 1) * hd] for i in range(x.shape[1] // hd)]


def _f_norm(x, g):
    return (_rms(x, g),)


def _f_s5tail(ys, u, z, d, wglu, bglu):
    y = jax.nn.gelu(ys + d * u)
    gate = jax.nn.sigmoid(jnp.dot(y.astype(BF16), wglu.astype(BF16), preferred_element_type=F32) + bglu)
    return (y * gate * _silu(z),)


def _make_f_dnpre(nh, hd, chunk):
    def f(qc, kc, dadb, alog, dtb):
        tm = qc.shape[0]
        qn = [q * lax.rsqrt(jnp.sum(q * q, axis=-1, keepdims=True) + EPS) * (hd ** -0.5) for q in _heads(qc, hd)]
        kn = [k * lax.rsqrt(jnp.sum(k * k, axis=-1, keepdims=True) + EPS) for k in _heads(kc, hd)]
        g = -jnp.exp(alog) * _softplus(dadb + dtb)
        beta = jax.nn.sigmoid(dadb)
        ii = lax.broadcasted_iota(jnp.int32, (tm, tm), 0)
        jj = lax.broadcasted_iota(jnp.int32, (tm, tm), 1)
        same = (ii // chunk) == (jj // chunk)
        outs = [jnp.concatenate(qn, axis=1), jnp.concatenate(kn, axis=1)]
        gt = jnp.dot(same.astype(F32), g, precision=HI, preferred_element_type=F32)
        for dr in range(2):
            tri = jnp.logical_and(same, (ii <= jj) if dr else (ii >= jj)).astype(F32)
            gc = jnp.dot(tri, g, precision=HI, preferred_element_type=F32)

            def spread(t, lane0):
                return jnp.concatenate([jnp.broadcast_to(t[:, lane0 + h:lane0 + h + 1], (tm, hd))
                                        for h in range(nh)], axis=1)

            outs += [spread(beta, 2 * nh + dr * nh), spread(gc, dr * nh), spread(gt, dr * nh)]
        return tuple(outs)
    return f


def _make_f_dnpost(hd):
    def f(o, z, ng):
        y = [_rms(oh, ng) for oh in _heads(o, hd)]
        return (jnp.concatenate(y, axis=1) * _silu(z),)
    return f


def _make_f_attpre(hd, with_v):
    def rope(x, g, cosf, sins, perm, scale):
        xn = _rms(x, g)
        xs = jnp.dot(xn, perm, precision=HI, preferred_element_type=F32)
        return (xn * cosf + xs * sins) * scale

    def f(aq, ak, *rest):
        if with_v:
            av, cosf, sins, perm, qg, kg = rest
        else:
            cosf, sins, perm, qg, kg = rest
        qh = jnp.concatenate([rope(x, qg, cosf, sins, perm, hd ** -0.5) for x in _heads(aq, hd)], axis=1)
        kh = jnp.concatenate([rope(x, kg, cosf, sins, perm, 1.0) for x in _heads(ak, hd)], axis=1)
        return (qh, kh, av) if with_v else (qh, kh)
    return f


def _f_gate(o, z):
    return (o * _silu(z),)


def _make_f_mem(nh, hd):
    def f(mq, z, kv):
        mw = nh * hd
        outs = []
        for h, q in enumerate(_heads(mq, hd)):
            k = kv[:, h * hd:(h + 1) * hd]
            v = kv[:, mw + h * hd:mw + (h + 1) * hd]
            s = lax.dot_general(q.astype(BF16), k.astype(BF16), (((1,), (1,)), ((), ())),
                                preferred_element_type=F32) * (hd ** -0.5)
            s = s - jnp.max(s, axis=-1, keepdims=True)
            p = jnp.exp(s)
            p = p / jnp.sum(p, axis=-1, keepdims=True)
            outs.append(jnp.dot(p.astype(BF16), v.astype(BF16), preferred_element_type=F32))
        return (jnp.concatenate(outs, axis=1) * _silu(z),)
    return f


def _f_merge(p0, p1, p2, p3, g0, g1, g2, g3):
    return (jax.nn.sigmoid(g0) * p0 + jax.nn.sigmoid(g1) * p1 + jax.nn.sigmoid(g2) * p2 + jax.nn.sigmoid(g3) * p3,)


def _make_f_delta(hd):
    def f(do, o):
        out = [jnp.broadcast_to(jnp.sum(a * b, axis=-1, keepdims=True), a.shape)
               for a, b in zip(_heads(do, hd), _heads(o, hd))]
        return (jnp.concatenate(out, axis=1),)
    return f


def _s5_prep(a_re, a_im, log_step, b_re, b_im, c_re, c_im, d):
    nb, gpb, sn, sp = d['NB'], d['GPB'], d['SN'], d['SP']
    step = jnp.exp(log_step)[:, None]
    mag = jnp.exp(a_re * step)
    lam_re = mag * jnp.cos(a_im * step)
    lam_im = mag * jnp.sin(a_im * step)
    den = a_re * a_re + a_im * a_im
    nr, ni = lam_re - 1.0, lam_im
    coef_re = (nr * a_re + ni * a_im) / den
    coef_im = (ni * a_re - nr * a_im) / den
    bb_re = coef_re[..., None] * b_re - coef_im[..., None] * b_im
    bb_im = coef_re[..., None] * b_im + coef_im[..., None] * b_re
    eye = jnp.eye(gpb, dtype=F32)

    def blk_in(bb):
        t = bb.reshape(nb, gpb, sn, sp)
        return jnp.einsum("jgnp,gh->jgphn", t, eye).reshape(nb, gpb * sp, gpb * sn)

    def blk_out(cc):
        t = cc.reshape(nb, gpb, sp, sn)
        return jnp.einsum("jgpn,gh->jgnhp", t, eye).reshape(nb, gpb * sn, gpb * sp)

    wb = jnp.concatenate([blk_in(bb_re), blk_in(bb_im)], axis=2)
    wc = jnp.concatenate([blk_out(c_re), blk_out(-c_im)], axis=1)
    return wb, wc, lam_re.reshape(nb, gpb * sn), lam_im.reshape(nb, gpb * sn)


def _s5_tables(lam_re, lam_im, rev, conj):
    lr, li = lam_re, (-lam_im if conj else lam_im)

    def cmul(a, b):
        return a[0] * b[0] - a[1] * b[1], a[0] * b[1] + a[1] * b[0]

    pw = [(lr, li)]
    for _ in range(7):
        pw.append(cmul(pw[-1], (lr, li)))
    rows = jnp.arange(8)

    def bc(t, k):
        keep = (rows < 8 - k) if rev else (rows >= k)
        return t[:, None, :] * keep.astype(F32)[None, :, None]

    order = list(range(8))[::-1] if rev else list(range(8))
    pwr = jnp.stack([pw[i][0] for i in order], axis=1)
    pwi = jnp.stack([pw[i][1] for i in order], axis=1)
    tabs = [bc(pw[0][0], 1), bc(pw[0][1], 1), bc(pw[1][0], 2), bc(pw[1][1], 2), bc(pw[3][0], 4), bc(pw[3][1], 4),
            pwr, pwi]
    return jnp.stack(tabs, axis=1)


def _scan_group(xr, xi, lt_ref, j, cr, ci, rev):
    for lvl, k in enumerate((1, 2, 4)):
        l_r, l_i = lt_ref[j, 2 * lvl], lt_ref[j, 2 * lvl + 1]
        sh = (8 - k) if rev else k
        sr, si = pltpu.roll(xr, sh, 0), pltpu.roll(xi, sh, 0)
        xr, xi = xr + l_r * sr - l_i * si, xi + l_r * si + l_i * sr
    p_r, p_i = lt_ref[j, 6], lt_ref[j, 7]
    return xr + p_r * cr - p_i * ci, xi + p_r * ci + p_i * cr


def _last_row(x, rev):
    last = 0 if rev else 7
    return jnp.broadcast_to(x[last:last + 1, :], x.shape)


def _s5_fwd(hsrc, ucb, wb, wc, lt, *, rev, acc, tb, name, d):
    L, SW, NB, BS = hsrc.shape[0], d['SW'], d['NB'], d['BS']
    tb = _pick(tb, L)
    nblk, ngr = L // tb, tb // 8
    tix = (lambda b: nblk - 1 - b) if rev else (lambda b: b)
    has_acc = acc is not None

    def body(*refs):
        if has_acc:
            u_ref, wb_ref, wc_ref, lt_ref, acc_ref, y_ref, cin_ref, bu_s, car = refs
        else:
            u_ref, wb_ref, wc_ref, lt_ref, y_ref, cin_ref, bu_s, car = refs

        @pl.when(pl.program_id(0) == 0)
        def _():
            car[...] = jnp.zeros_like(car)

        cin_ref[...] = car[...]
        for j in range(NB):
            bu_s[:, j * 2 * BS:(j + 1) * 2 * BS] = jnp.dot(
                u_ref[:, j * LANE:(j + 1) * LANE].astype(BF16), wb_ref[j], preferred_element_type=F32)

        def grp(r, _):
            base = pl.multiple_of((ngr - 1 - r if rev else r) * 8, 8)
            for j in range(NB):
                c0 = j * 2 * BS
                xr, xi = _scan_group(bu_s[pl.ds(base, 8), c0:c0 + BS], bu_s[pl.ds(base, 8), c0 + BS:c0 + 2 * BS],
                                     lt_ref, j, car[:, c0:c0 + BS], car[:, c0 + BS:c0 + 2 * BS], rev)
                bu_s[pl.ds(base, 8), c0:c0 + BS] = xr
                bu_s[pl.ds(base, 8), c0 + BS:c0 + 2 * BS] = xi
                car[:, c0:c0 + BS] = _last_row(xr, rev)
                car[:, c0 + BS:c0 + 2 * BS] = _last_row(xi, rev)
            return 0

        lax.fori_loop(0, ngr, grp, 0)
        for j in range(NB):
            y = jnp.dot(bu_s[:, j * 2 * BS:(j + 1) * 2 * BS].astype(BF16), wc_ref[j], preferred_element_type=F32)
            if has_acc:
                y = y + acc_ref[:, j * LANE:(j + 1) * LANE]
            y_ref[:, j * LANE:(j + 1) * LANE] = y

    in_specs = [pl.BlockSpec((tb, SW), lambda b: (tix(b), ucb)),
                pl.BlockSpec(wb.shape, lambda b: (0, 0, 0)), pl.BlockSpec(wc.shape, lambda b: (0, 0, 0)),
                pl.BlockSpec(lt.shape, lambda b: (0, 0, 0, 0))]
    args = [hsrc, wb, wc, lt]
    if has_acc:
        in_specs.append(pl.BlockSpec((tb, SW), lambda b: (tix(b), 0)))
        args.append(acc)
    y, cin = pl.pallas_call(
        body, name=name, grid=(nblk,), in_specs=in_specs,
        out_specs=[pl.BlockSpec((tb, SW), lambda b: (tix(b), 0)),
                   pl.BlockSpec((8, NB * 2 * BS), lambda b: (tix(b), 0))],
        out_shape=[jax.ShapeDtypeStruct((L, SW), F32), jax.ShapeDtypeStruct((nblk * 8, NB * 2 * BS), F32)],
        scratch_shapes=[pltpu.VMEM((tb, NB * 2 * BS), F32), pltpu.VMEM((8, NB * 2 * BS), F32)],
        compiler_params=_cparams(("arbitrary",)),
    )(*args)
    return y, cin


def _s5_bwd(hsrc, ucb, dy, cin, wb, wc, lt, lt_adj, *, rev, acc, tb, name, d):
    L, SW, NB, BS = hsrc.shape[0], d['SW'], d['NB'], d['BS']
    tb = _pick(tb, L)
    nblk, ngr = L // tb, tb // 8
    arev = not rev
    tix = (lambda b: nblk - 1 - b) if arev else (lambda b: b)
    has_acc = acc is not None
    NT = (((1,), (1,)), ((), ()))
    TN = (((0,), (0,)), ((), ()))

    def body(*refs):
        if has_acc:
            (u_ref, dy_ref, cin_ref, wb_ref, wc_ref, lt_ref, la_ref, acc_ref,
             du_ref, dwb_ref, dwc_ref, dlam_ref, s_s, g_s, car, acar) = refs
        else:
            (u_ref, dy_ref, cin_ref, wb_ref, wc_ref, lt_ref, la_ref,
             du_ref, dwb_ref, dwc_ref, dlam_ref, s_s, g_s, car, acar) = refs

        @pl.when(pl.program_id(0) == 0)
        def _():
            acar[...] = jnp.zeros_like(acar)
            dwb_ref[...] = jnp.zeros_like(dwb_ref)
            dwc_ref[...] = jnp.zeros_like(dwc_ref)
            dlam_ref[...] = jnp.zeros_like(dlam_ref)

        car[...] = cin_ref[...]
        for j in range(NB):
            s_s[:, j * 2 * BS:(j + 1) * 2 * BS] = jnp.dot(
                u_ref[:, j * LANE:(j + 1) * LANE].astype(BF16), wb_ref[j], preferred_element_type=F32)
            g_s[:, j * 2 * BS:(j + 1) * 2 * BS] = lax.dot_general(
                dy_ref[:, j * LANE:(j + 1) * LANE].astype(BF16), wc_ref[j], NT, preferred_element_type=F32)

        def fgrp(r, _):
            base = pl.multiple_of((ngr - 1 - r if rev else r) * 8, 8)
            for j in range(NB):
                c0 = j * 2 * BS
                xr, xi = _scan_group(s_s[pl.ds(base, 8), c0:c0 + BS], s_s[pl.ds(base, 8), c0 + BS:c0 + 2 * BS],
                                     lt_ref, j, car[:, c0:c0 + BS], car[:, c0 + BS:c0 + 2 * BS], rev)
                s_s[pl.ds(base, 8), c0:c0 + BS] = xr
                s_s[pl.ds(base, 8), c0 + BS:c0 + 2 * BS] = xi
                car[:, c0:c0 + BS] = _last_row(xr, rev)
                car[:, c0 + BS:c0 + 2 * BS] = _last_row(xi, rev)
            return 0

        lax.fori_loop(0, ngr, fgrp, 0)

        row = lax.broadcasted_iota(jnp.int32, (8, BS), 0)

        def agrp(r, _):
            gi = ngr - 1 - r if arev else r
            base = pl.multiple_of(gi * 8, 8)
            pgi = gi + 1 if rev else gi - 1
            inside = jnp.logical_and(pgi >= 0, pgi < ngr)
            pbase = pl.multiple_of(jnp.clip(pgi, 0, ngr - 1) * 8, 8)
            for j in range(NB):
                c0 = j * 2 * BS
                ar, ai = _scan_group(g_s[pl.ds(base, 8), c0:c0 + BS], g_s[pl.ds(base, 8), c0 + BS:c0 + 2 * BS],
                                     la_ref, j, acar[:, c0:c0 + BS], acar[:, c0 + BS:c0 + 2 * BS], arev)
                g_s[pl.ds(base, 8), c0:c0 + BS] = ar
                g_s[pl.ds(base, 8), c0 + BS:c0 + 2 * BS] = ai
                acar[:, c0:c0 + BS] = _last_row(ar, arev)
                acar[:, c0 + BS:c0 + 2 * BS] = _last_row(ai, arev)
                sr, si = s_s[pl.ds(base, 8), c0:c0 + BS], s_s[pl.ds(base, 8), c0 + BS:c0 + 2 * BS]
                edge_r = jnp.where(inside, _last_row(s_s[pl.ds(pbase, 8), c0:c0 + BS], rev), cin_ref[:, c0:c0 + BS])
                edge_i = jnp.where(inside, _last_row(s_s[pl.ds(pbase, 8), c0 + BS:c0 + 2 * BS], rev),
                                   cin_ref[:, c0 + BS:c0 + 2 * BS])
                sh = 7 if rev else 1
                first = 7 if rev else 0
                pr = jnp.where(row == first, edge_r, pltpu.roll(sr, sh, 0))
                pi = jnp.where(row == first, edge_i, pltpu.roll(si, sh, 0))
                dlam_ref[:, c0:c0 + BS] += ar * pr + ai * pi
                dlam_ref[:, c0 + BS:c0 + 2 * BS] += ai * pr - ar * pi
            return 0

        lax.fori_loop(0, ngr, agrp, 0)
        for j in range(NB):
            a_j = g_s[:, j * 2 * BS:(j + 1) * 2 * BS].astype(BF16)
            u_j = u_ref[:, j * LANE:(j + 1) * LANE].astype(BF16)
            du = lax.dot_general(a_j, wb_ref[j], NT, preferred_element_type=F32)
            if has_acc:
                du = du + acc_ref[:, j * LANE:(j + 1) * LANE]
            du_ref[:, j * LANE:(j + 1) * LANE] = du
            dwb_ref[j] += lax.dot_general(u_j, a_j, TN, preferred_element_type=F32)
            dwc_ref[j] += lax.dot_general(s_s[:, j * 2 * BS:(j + 1) * 2 * BS].astype(BF16),
                                          dy_ref[:, j * LANE:(j + 1) * LANE].astype(BF16), TN,
                                          preferred_element_type=F32)

    W2 = NB * 2 * BS
    in_specs = [pl.BlockSpec((tb, SW), lambda b: (tix(b), ucb)), pl.BlockSpec((tb, SW), lambda b: (tix(b), 0)),
                pl.BlockSpec((8, W2), lambda b: (tix(b), 0)),
                pl.BlockSpec(wb.shape, lambda b: (0, 0, 0)), pl.BlockSpec(wc.shape, lambda b: (0, 0, 0)),
                pl.BlockSpec(lt.shape, lambda b: (0, 0, 0, 0)), pl.BlockSpec(lt_adj.shape, lambda b: (0, 0, 0, 0))]
    args = [hsrc, dy, cin, wb, wc, lt, lt_adj]
    if has_acc:
        in_specs.append(pl.BlockSpec((tb, SW), lambda b: (tix(b), 0)))
        args.append(acc)
    return pl.pallas_call(
        body, name=name, grid=(nblk,), in_specs=in_specs,
        out_specs=[pl.BlockSpec((tb, SW), lambda b: (tix(b), 0)),
                   pl.BlockSpec(wb.shape, lambda b: (0, 0, 0)), pl.BlockSpec(wc.shape, lambda b: (0, 0, 0)),
                   pl.BlockSpec((8, W2), lambda b: (0, 0))],
        out_shape=[jax.ShapeDtypeStruct((L, SW), F32), jax.ShapeDtypeStruct(wb.shape, F32),
                   jax.ShapeDtypeStruct(wc.shape, F32), jax.ShapeDtypeStruct((8, W2), F32)],
        scratch_shapes=[pltpu.VMEM((tb, W2), F32), pltpu.VMEM((tb, W2), F32),
                        pltpu.VMEM((8, W2), F32), pltpu.VMEM((8, W2), F32)],
        compiler_params=_cparams(("arbitrary",)),
    )(*args)


_NN = (((1,), (0,)), ((), ()))
_NT = (((1,), (1,)), ((), ()))
_TN = (((0,), (0,)), ((), ()))


def _dotb(a, b, dn=_NN):
    return lax.dot_general(a.astype(BF16), b.astype(BF16), dn, preferred_element_type=F32)


def _split(x):
    hi = x.astype(BF16)
    return hi, (x - hi.astype(F32)).astype(BF16)


def _dot3(a, b, dn=_NN):
    ah, al = _split(a)
    bh, bl = _split(b)
    f = lambda x, y: lax.dot_general(x, y, dn, preferred_element_type=F32)
    return f(ah, bh) + (f(ah, bl) + f(al, bh))


@jax.custom_vjp
def _dot3_nn(a, b):
    return _dot3(a, b, _NN)


_dot3_nn.defvjp(lambda a, b: (_dot3(a, b, _NN), (a, b)),
                lambda res, g: (_dotb(g, res[1], _NT), _dotb(res[0], g, _TN)))


@jax.custom_vjp
def _dot3_nt(a, b):
    return _dot3(a, b, _NT)


_dot3_nt.defvjp(lambda a, b: (_dot3(a, b, _NT), (a, b)),
                lambda res, g: (_dotb(g, res[1], _NN), _dotb(g, res[0], _TN)))


def _delta_chunk(rev, one_pass_grads, *flat):
    heads = [flat[i:i + 7] for i in range(0, len(flat), 7)]
    q, k, v, beta, gc, gt, s_in = [list(t) for t in zip(*heads)]
    c, hd = q[0].shape
    each = lambda f, *ls: [f(*t) for t in zip(*ls)]
    mm_nn = _dot3_nn if one_pass_grads else _dot3
    mm_nt = _dot3_nt if one_pass_grads else (lambda x, y: _dot3(x, y, _NT))
    ii = lax.broadcasted_iota(jnp.int32, (c, c), 0)
    jj = lax.broadcasted_iota(jnp.int32, (c, c), 1)
    incl = (ii <= jj) if rev else (ii >= jj)
    strict = (ii < jj) if rev else (ii > jj)
    eye = (ii == jj).astype(F32)
    decay = each(lambda g: jnp.where(incl, jnp.exp(jnp.where(incl, g[:, :c] - jnp.transpose(g)[:c, :], 0.0)), 0.0), gc)
    kb = each(lambda a, b: a * b, k, beta)
    a = each(lambda x, y, dc: jnp.where(strict, mm_nt(x, y) * dc, 0.0), kb, k, decay)
    tinv = each(lambda x: eye - x, a)
    p = a
    n = 2
    while n < c:
        p = each(lambda x: mm_nn(x, x), p)
        tinv = each(lambda t, x: mm_nn(t, eye + x), tinv, p)
        n *= 2
    eg = each(jnp.exp, gc)
    u = each(lambda t, x, b: mm_nn(t, x * b), tinv, v, beta)
    w = each(lambda t, x, e: mm_nn(t, x * e), tinv, kb, eg)
    intra = each(lambda x, y, dc: _dotb(x, y, _NT) * dc, q, k, decay)
    v_new = each(lambda x, y, s: x - _dotb(y, s), u, w, s_in)
    o = each(lambda x, e, s, m, vn: _dotb(x * e, s) + _dotb(m, vn), q, eg, s_in, intra, v_new)
    s_out = each(lambda s, t, x, g, vn: s * jnp.exp(jnp.broadcast_to(t[0:1, :], (hd, hd)))
                 + _dotb(x * jnp.exp(t - g), vn, _TN), s_in, gt, k, gc, v_new)
    return tuple(x for pair in zip(o, s_out) for x in pair)


def _delta_fwd(q, k, v, gates, *, vcb, rev, acc, name, d):
    L, H, hd, C = q.shape[0], d['DNH'], d['DNK'], d['CHUNK']
    nc = L // C
    cix = (lambda i: nc - 1 - i) if rev else (lambda i: i)
    has_acc = acc is not None

    def body(*refs):
        if has_acc:
            q_ref, k_ref, v_ref, b_ref, gc_ref, gt_ref, acc_ref, o_ref, ss_ref, st = refs
        else:
            q_ref, k_ref, v_ref, b_ref, gc_ref, gt_ref, o_ref, ss_ref, st = refs

        @pl.when(pl.program_id(0) == 0)
        def _():
            st[...] = jnp.zeros_like(st)

        sls = [slice(h * hd, (h + 1) * hd) for h in range(H)]
        ins = [(q_ref[:, sl], k_ref[:, sl], v_ref[:, sl], b_ref[:, sl], gc_ref[:, sl], gt_ref[:, sl], st[h])
               for h, sl in enumerate(sls)]
        accv = [acc_ref[:, sl] for sl in sls] if has_acc else None
        res = _delta_chunk(rev, False, *[t for head in ins for t in head])
        for h, sl in enumerate(sls):
            o, s_out = res[2 * h], res[2 * h + 1]
            ss_ref[0, h] = ins[h][6]
            o_ref[:, sl] = o + accv[h] if has_acc else o
            st[h] = s_out

    blk = pl.BlockSpec((C, H * hd), lambda i: (cix(i), 0))
    in_specs = [blk, blk, pl.BlockSpec((C, H * hd), lambda i: (cix(i), vcb)), blk, blk, blk]
    args = [q, k, v, *gates]
    if has_acc:
        in_specs.append(blk)
        args.append(acc)
    return pl.pallas_call(
        body, name=name, grid=(nc,), in_specs=in_specs,
        out_specs=[blk, pl.BlockSpec((1, H, hd, hd), lambda i: (cix(i), 0, 0, 0))],
        out_shape=[jax.ShapeDtypeStruct((L, H * hd), F32), jax.ShapeDtypeStruct((nc, H, hd, hd), F32)],
        scratch_shapes=[pltpu.VMEM((H, hd, hd), F32)],
        compiler_params=_cparams(("arbitrary",)),
    )(*args)


def _delta_bwd(q, k, v, gates, ssave, do, *, vcb, rev, accs, name, d):
    L, H, hd, C = q.shape[0], d['DNH'], d['DNK'], d['CHUNK']
    nc = L // C
    cix = (lambda i: i) if rev else (lambda i: nc - 1 - i)
    has_acc = accs is not None

    def body(*refs):
        if has_acc:
            (q_ref, k_ref, v_ref, b_ref, gc_ref, gt_ref, ss_ref, do_ref, aq_ref, ak_ref, av_ref,
             dq_ref, dk_ref, dv_ref, db_ref, dgc_ref, dgt_ref, dst) = refs
        else:
            (q_ref, k_ref, v_ref, b_ref, gc_ref, gt_ref, ss_ref, do_ref,
             dq_ref, dk_ref, dv_ref, db_ref, dgc_ref, dgt_ref, dst) = refs

        @pl.when(pl.program_id(0) == 0)
        def _():
            dst[...] = jnp.zeros_like(dst)

        sls = [slice(h * hd, (h + 1) * hd) for h in range(H)]
        ins = [(q_ref[:, sl], k_ref[:, sl], v_ref[:, sl], b_ref[:, sl], gc_ref[:, sl], gt_ref[:, sl], ss_ref[0, h])
               for h, sl in enumerate(sls)]
        cts = tuple(t for h, sl in enumerate(sls) for t in (do_ref[:, sl], dst[h]))
        accv = [(aq_ref[:, sl], ak_ref[:, sl], av_ref[:, sl]) for sl in sls] if has_acc else None
        _, vjp = jax.vjp(functools.partial(_delta_chunk, rev, True), *[t for head in ins for t in head])
        res = vjp(cts)
        for h, sl in enumerate(sls):
            dq, dk, dv, db, dgc, dgt, ds = res[7 * h:7 * h + 7]
            dst[h] = ds
            if has_acc:
                dq, dk, dv = dq + accv[h][0], dk + accv[h][1], dv + accv[h][2]
            dq_ref[:, sl] = dq
            dk_ref[:, sl] = dk
            dv_ref[:, sl] = dv
            db_ref[:, sl] = db
            dgc_ref[:, sl] = dgc
            dgt_ref[:, sl] = dgt

    blk = pl.BlockSpec((C, H * hd), lambda i: (cix(i), 0))
    in_specs = [blk, blk, pl.BlockSpec((C, H * hd), lambda i: (cix(i), vcb)), blk, blk, blk,
                pl.BlockSpec((1, H, hd, hd), lambda i: (cix(i), 0, 0, 0)), blk]
    args = [q, k, v, *gates, ssave, do]
    if has_acc:
        in_specs += [blk, blk, blk]
        args += list(accs)
    return pl.pallas_call(
        body, name=name, grid=(nc,), in_specs=in_specs, out_specs=[blk] * 6,
        out_shape=[jax.ShapeDtypeStruct((L, H * hd), F32)] * 6,
        scratch_shapes=[pltpu.VMEM((H, hd, hd), F32)],
        compiler_params=_cparams(("arbitrary",)),
    )(*args)


def _conv_specs(tm, w, cb0, nrb, L):
    hb = tm // 8
    last8 = L // 8 - 1
    cur = pl.BlockSpec((tm, w), lambda s, i: (i, cb0 + s))
    prev = pl.BlockSpec((8, w), lambda s, i: (jnp.maximum(i * hb - 1, 0), cb0 + s))
    nxt = pl.BlockSpec((8, w), lambda s, i: (jnp.minimum((i + 1) * hb, last8), cb0 + s))
    return [prev, cur, nxt]


def _fill_halo(dst, prev_ref, cur_ref, next_ref, i, nrb, tm):
    dst[pl.ds(0, 8), :] = jnp.where(i > 0, prev_ref[...], 0.0)
    dst[pl.ds(8, tm), :] = cur_ref[...]
    dst[pl.ds(8 + tm, 8), :] = jnp.where(i < nrb - 1, next_ref[...], 0.0)


def _conv_fwd(hsrc, cb0, wt, *, tm, name, d):
    L, w, K = hsrc.shape[0], d['DW'], d['CONV']
    tm = _pick(tm, L)
    nrb = L // tm

    def body(prev_ref, cur_ref, next_ref, w_ref, o_ref, xs):
        i = pl.program_id(1)
        _fill_halo(xs, prev_ref, cur_ref, next_ref, i, nrb, tm)
        y = jnp.zeros((tm, w), F32)
        for kk in range(K):
            y = y + w_ref[0, pl.ds(kk, 1), :] * xs[pl.ds(8 - K // 2 + kk, tm), :]
        o_ref[...] = _silu(y)

    return pl.pallas_call(
        body, name=name, grid=(3, nrb),
        in_specs=_conv_specs(tm, w, cb0, nrb, L) + [pl.BlockSpec((1, 8, w), lambda s, i: (s, 0, 0))],
        out_specs=pl.BlockSpec((tm, w), lambda s, i: (i, s)),
        out_shape=jax.ShapeDtypeStruct((L, 3 * w), F32),
        scratch_shapes=[pltpu.VMEM((tm + 16, w), F32)],
        compiler_params=_cparams(("parallel", "parallel")),
    )(hsrc, hsrc, hsrc, wt)


def _conv_bwd(hsrc, cb0, wt, dact, *, tm, name, d):
    L, w, K = hsrc.shape[0], d['DW'], d['CONV']
    tm = _pick(tm, L)
    nrb = L // tm
    half = K // 2

    def body(xp_ref, xc_ref, xn_ref, gp_ref, gc_ref, gn_ref, w_ref, dx_ref, dw_ref, xs, gs, dys):
        i = pl.program_id(1)
        _fill_halo(xs, xp_ref, xc_ref, xn_ref, i, nrb, tm)
        _fill_halo(gs, gp_ref, gc_ref, gn_ref, i, nrb, tm)
        y = jnp.zeros((tm + 8, w), F32)
        for kk in range(K):
            y = y + w_ref[0, pl.ds(kk, 1), :] * xs[pl.ds(4 - half + kk, tm + 8), :]
        sg = jax.nn.sigmoid(y)
        dys[...] = gs[pl.ds(4, tm + 8), :] * (sg * (1.0 + y * (1.0 - sg)))
        dx = jnp.zeros((tm, w), F32)
        for kk in range(K):
            dx = dx + w_ref[0, pl.ds(kk, 1), :] * dys[pl.ds(4 + half - kk, tm), :]
        dx_ref[...] = dx

        @pl.when(i == 0)
        def _():
            dw_ref[...] = jnp.zeros_like(dw_ref)

        dy = dys[pl.ds(4, tm), :]
        for kk in range(K):
            dw_ref[0, pl.ds(kk, 1), :] += jnp.sum(dy * xs[pl.ds(8 - half + kk, tm), :], axis=0, keepdims=True)

    gspecs = _conv_specs(tm, w, 0, nrb, L)
    return pl.pallas_call(
        body, name=name, grid=(3, nrb),
        in_specs=_conv_specs(tm, w, cb0, nrb, L) + gspecs + [pl.BlockSpec((1, 8, w), lambda s, i: (s, 0, 0))],
        out_specs=[pl.BlockSpec((tm, w), lambda s, i: (i, s)), pl.BlockSpec((1, 8, w), lambda s, i: (s, 0, 0))],
        out_shape=[jax.ShapeDtypeStruct((L, 3 * w), F32), jax.ShapeDtypeStruct((3, 8, w), F32)],
        scratch_shapes=[pltpu.VMEM((tm + 16, w), F32), pltpu.VMEM((tm + 16, w), F32), pltpu.VMEM((tm + 8, w), F32)],
        compiler_params=_cparams(("parallel", "arbitrary")),
    )(hsrc, hsrc, hsrc, dact, dact, dact, wt)


def _wide(v, n):
    return v if n == LANE else jnp.tile(v, (1, n // LANE))


def _attn_fwd(qh, kh, vh, *, tq, tk, name, d):
    L, H, KVH, hd = qh.shape[0], d['AH'], d['AKV'], d['AD']
    grp = H // KVH
    tq, tk = _pick(tq, L), _pick(tk, L)
    nk = L // tk

    def body(q_ref, k_ref, v_ref, o_ref, lse_ref, m_s, l_s, acc):
        j = pl.program_id(2)

        @pl.when(j == 0)
        def _():
            m_s[...] = jnp.full_like(m_s, -1e30)
            l_s[...] = jnp.zeros_like(l_s)
            acc[...] = jnp.zeros_like(acc)

        s = lax.dot_general(q_ref[...], k_ref[...], _NT, preferred_element_type=F32)
        m_old = m_s[...]
        m_new = jnp.maximum(m_old, jnp.max(s, axis=-1, keepdims=True))
        alpha = jnp.exp(m_old - m_new)
        p = jnp.exp(s - _wide(m_new, tk))
        l_s[...] = alpha * l_s[...] + jnp.sum(p, axis=-1, keepdims=True)
        acc[...] = alpha * acc[...] + jnp.dot(p.astype(BF16), v_ref[...], preferred_element_type=F32)
        m_s[...] = m_new

        @pl.when(j == nk - 1)
        def _():
            o_ref[...] = acc[...] / l_s[...]
            lse_ref[...] = m_s[...] + jnp.log(l_s[...])

    qspec = pl.BlockSpec((tq, hd), lambda h, i, j: (i, h))
    kspec = pl.BlockSpec((tk, hd), lambda h, i, j: (j, h // grp))
    return pl.pallas_call(
        body, name=name, grid=(H, L // tq, nk), in_specs=[qspec, kspec, kspec], out_specs=[qspec, qspec],
        out_shape=[jax.ShapeDtypeStruct((L, H * hd), F32), jax.ShapeDtypeStruct((L, H * hd), F32)],
        scratch_shapes=[pltpu.VMEM((tq, hd), F32), pltpu.VMEM((tq, hd), F32), pltpu.VMEM((tq, hd), F32)],
        compiler_params=_cparams(("parallel", "parallel", "arbitrary")),
    )(qh, kh, vh)


def _attn_bwd(qh, kh, vh, do, lse, delta, *, tq, tk, name, d):
    L, H, KVH, hd = qh.shape[0], d['AH'], d['AKV'], d['AD']
    grp = H // KVH
    tq, tk = _pick(tq, L), _pick(tk, L)
    nk = L // tk

    def body(q_ref, k_ref, v_ref, do_ref, lse_ref, dl_ref, dq_ref, dk_ref, dv_ref, dq_s):
        g, i, j = pl.program_id(1), pl.program_id(2), pl.program_id(3)

        @pl.when(jnp.logical_and(jnp.logical_and(g == 0, i == 0), j == 0))
        def _():
            dk_ref[...] = jnp.zeros_like(dk_ref)
            dv_ref[...] = jnp.zeros_like(dv_ref)

        @pl.when(j == 0)
        def _():
            dq_s[...] = jnp.zeros_like(dq_s)

        q, k, do_ = q_ref[...], k_ref[...], do_ref[...].astype(BF16)
        s = lax.dot_general(q, k, _NT, preferred_element_type=F32)
        p = jnp.exp(s - _wide(lse_ref[...], tk))
        dp = lax.dot_general(do_, v_ref[...], _NT, preferred_element_type=F32)
        ds = (p * (dp - _wide(dl_ref[...], tk))).astype(BF16)
        dq_s[...] += jnp.dot(ds, k, preferred_element_type=F32)
        rows = pl.ds(pl.multiple_of(j * tk, tk), tk)
        dv_ref[rows, :] += lax.dot_general(p.astype(BF16), do_, _TN, preferred_element_type=F32)
        dk_ref[rows, :] += lax.dot_general(ds, q, _TN, preferred_element_type=F32)

        @pl.when(j == nk - 1)
        def _():
            dq_ref[...] = dq_s[...]

    qspec = pl.BlockSpec((tq, hd), lambda kv, g, i, j: (i, kv * grp + g))
    kspec = pl.BlockSpec((tk, hd), lambda kv, g, i, j: (j, kv))
    colspec = pl.BlockSpec((L, hd), lambda kv, g, i, j: (0, kv))
    return pl.pallas_call(
        body, name=name, grid=(KVH, grp, L // tq, nk),
        in_specs=[qspec, kspec, kspec, qspec, qspec, qspec], out_specs=[qspec, colspec, colspec],
        out_shape=[jax.ShapeDtypeStruct((L, H * hd), F32)] + [jax.ShapeDtypeStruct((L, KVH * hd), F32)] * 2,
        scratch_shapes=[pltpu.VMEM((tq, hd), F32)],
        compiler_params=_cparams(("parallel", "arbitrary", "arbitrary", "arbitrary")),
    )(qh, kh, vh, do, lse, delta)


def _loss_grad(x, g, tgt, *, tm, name):
    L, D = x.shape
    tm = _pick(tm, L)

    def body(x_ref, g_ref, t_ref, loss_ref, dx_ref, dg_ref):
        def f(xv, gv):
            err = _rms(xv, gv) - t_ref[...]
            return 0.5 * jnp.sum(jnp.mean(err * err, axis=-1, keepdims=True))

        val, vjp = jax.vjp(f, x_ref[...], g_ref[...])
        dx, dg = vjp(jnp.ones((), F32))
        dx_ref[...] = dx

        @pl.when(pl.program_id(0) == 0)
        def _():
            loss_ref[...] = jnp.zeros_like(loss_ref)
            dg_ref[...] = jnp.zeros_like(dg_ref)

        loss_ref[...] += val
        dg_ref[...] += dg

    return pl.pallas_call(
        body, name=name, grid=(L // tm,),
        in_specs=[pl.BlockSpec((tm, D), lambda i: (i, 0)), pl.BlockSpec((1, D), lambda i: (0, 0)),
                  pl.BlockSpec((tm, D), lambda i: (i, 0))],
        out_specs=[pl.BlockSpec((8, LANE), lambda i: (0, 0)), pl.BlockSpec((tm, D), lambda i: (i, 0)),
                   pl.BlockSpec((1, D), lambda i: (0, 0))],
        out_shape=[jax.ShapeDtypeStruct((8, LANE), F32), jax.ShapeDtypeStruct((L, D), F32),
                   jax.ShapeDtypeStruct((1, D), F32)],
        compiler_params=_cparams(("arbitrary",)),
    )(x, g, tgt)


def _sum_slots(recv, *, tr, name):
    n, R, W = recv.shape
    tr = _pick(tr, R)

    def body(r_ref, o_ref):
        s = r_ref[0].astype(F32)
        for i in range(1, n):
            s = s + r_ref[i].astype(F32)
        o_ref[...] = s

    return pl.pallas_call(
        body, name=name, grid=(R // tr,),
        in_specs=[pl.BlockSpec((n, tr, W), lambda i: (0, i, 0))], out_specs=pl.BlockSpec((tr, W), lambda i: (i, 0)),
        out_shape=jax.ShapeDtypeStruct((R, W), F32), compiler_params=_cparams(("parallel",)),
    )(recv)


def _adamw(w, g, m, v, *, tr, name):
    R, W = w.shape
    tr = _pick(tr, R)
    c1 = 1.0 - ADAM_B1 ** ADAM_STEP
    c2 = 1.0 - ADAM_B2 ** ADAM_STEP

    def body(w_ref, g_ref, m_ref, v_ref, d_ref, nm_ref, nv_ref):
        gv = g_ref[...]
        nm = ADAM_B1 * m_ref[...] + (1.0 - ADAM_B1) * gv
        nv = ADAM_B2 * v_ref[...] + (1.0 - ADAM_B2) * (gv * gv)
        d_ref[...] = -ADAM_LR * ((nm / c1) / (jnp.sqrt(nv / c2) + ADAM_EPS) + ADAM_WD * w_ref[...])
        nm_ref[...] = nm
        nv_ref[...] = nv

    spec = pl.BlockSpec((tr, W), lambda i: (i, 0))
    return pl.pallas_call(
        body, name=name, grid=(R // tr,), in_specs=[spec] * 4, out_specs=[spec] * 3,
        out_shape=[jax.ShapeDtypeStruct((R, W), F32)] * 3, compiler_params=_cparams(("parallel",)),
    )(w, g, m, v)


_MESH = pl.DeviceIdType.MESH


def _all_gather(xs, *, name):
    na = len(xs)

    def body(*refs):
        x_refs, out_refs = refs[:na], refs[na:2 * na]
        send_sems, recv_sems, local_sems = refs[2 * na:]
        x, y, c = lax.axis_index("x"), lax.axis_index("y"), lax.axis_index("c")
        me, sibling = (x, y, c), (x, y, 1 - c)
        chips = [(1 - x, y), (x, 1 - y), (1 - x, 1 - y)]

        def slot(a, px, py, pc):
            return out_refs[a].at[4 * px + 2 * py + pc]

        def copy(a, k, block, to, src=None):
            return pltpu.make_async_remote_copy(
                src_ref=slot(a, *block) if src is None else src, dst_ref=slot(a, *block),
                send_sem=send_sems.at[7 * a + k], recv_sem=recv_sems.at[7 * a + k], device_id=to,
                device_id_type=_MESH)

        mine = [pltpu.make_async_copy(x_refs[a], slot(a, *me), local_sems.at[a]) for a in range(na)]
        for cp in mine:
            cp.start()
        first = []
        for a in range(na):
            first.append(copy(a, 0, me, sibling, src=x_refs[a]))
            first += [copy(a, 1 + j, me, (*chip, c), src=x_refs[a]) for j, chip in enumerate(chips)]
        for cp in first:
            cp.start()
        passed = []
        for a in range(na):
            for j, chip in enumerate(chips):
                copy(a, 1 + j, (*chip, c), me).wait_recv()
                passed.append(copy(a, 4 + j, (*chip, c), sibling))
                passed[-1].start()
        for a in range(na):
            copy(a, 0, sibling, me).wait_recv()
            for j, chip in enumerate(chips):
                copy(a, 4 + j, (*chip, 1 - c), me).wait_recv()
        for cp in first + passed:
            cp.wait_send()
        for cp in mine:
            cp.wait()

    return pl.pallas_call(
        body, name=name,
        out_shape=[jax.ShapeDtypeStruct((N_DEV,) + t.shape, t.dtype) for t in xs],
        in_specs=[pl.BlockSpec(memory_space=pl.ANY)] * na, out_specs=[pl.BlockSpec(memory_space=pl.ANY)] * na,
        scratch_shapes=[pltpu.SemaphoreType.DMA((7 * na,)), pltpu.SemaphoreType.DMA((7 * na,)),
                        pltpu.SemaphoreType.DMA((na,))],
    )(*xs)


N_CHIP = N_DEV // 2


def _pair_exchange(gs, *, name):
    na = len(gs)

    def body(*refs):
        g_refs, out_refs = refs[:na], refs[na:2 * na]
        send_sems, recv_sems = refs[2 * na:]
        x, y, c = lax.axis_index("x"), lax.axis_index("y"), lax.axis_index("c")

        def copy(a, chip, core):
            return pltpu.make_async_remote_copy(
                src_ref=g_refs[a].at[2 * chip + core], dst_ref=out_refs[a].at[chip],
                send_sem=send_sems.at[N_CHIP * a + chip], recv_sem=recv_sems.at[N_CHIP * a + chip],
                device_id=(x, y, 1 - c), device_id_type=_MESH)

        sends = [copy(a, chip, 1 - c) for a in range(na) for chip in range(N_CHIP)]
        for cp in sends:
            cp.start()
        for cp in sends:
            cp.wait_recv()
        for cp in sends:
            cp.wait_send()

    return pl.pallas_call(
        body, name=name,
        out_shape=[jax.ShapeDtypeStruct((N_CHIP,) + t.shape[1:], t.dtype) for t in gs],
        in_specs=[pl.BlockSpec(memory_space=pl.ANY)] * na, out_specs=[pl.BlockSpec(memory_space=pl.ANY)] * na,
        scratch_shapes=[pltpu.SemaphoreType.DMA((N_CHIP * na,)), pltpu.SemaphoreType.DMA((N_CHIP * na,))],
    )(*gs)


def _pair_sum(g, r, core, *, out_dtype, tr, name):
    _, R, W = g.shape
    tr = _pick(tr, R)

    def body(core_ref, g_ref, r_ref, o_ref):
        o_ref[...] = (g_ref[...] + r_ref[...]).astype(o_ref.dtype)

    return pl.pallas_call(
        body, name=name,
        grid_spec=pltpu.PrefetchScalarGridSpec(
            num_scalar_prefetch=1, grid=(N_CHIP, R // tr),
            in_specs=[pl.BlockSpec((1, tr, W), lambda ch, i, core_ref: (2 * ch + core_ref[0], i, 0)),
                      pl.BlockSpec((1, tr, W), lambda ch, i, core_ref: (ch, i, 0))],
            out_specs=pl.BlockSpec((1, tr, W), lambda ch, i, core_ref: (ch, i, 0))),
        out_shape=jax.ShapeDtypeStruct((N_CHIP, R, W), out_dtype),
        compiler_params=_cparams(("parallel", "parallel")),
    )(core, g, r)


def _chip_exchange(hs, *, name):
    na = len(hs)

    def body(*refs):
        h_refs, out_refs = refs[:na], refs[na:2 * na]
        send_sems, recv_sems, local_sems = refs[2 * na:]
        x, y, c = lax.axis_index("x"), lax.axis_index("y"), lax.axis_index("c")
        me = 2 * x + y

        def copy(a, mask, started):
            px, py = x ^ (mask >> 1), y ^ (mask & 1)
            mine_, theirs = me, 2 * px + py
            return pltpu.make_async_remote_copy(
                src_ref=h_refs[a].at[theirs if started else mine_],
                dst_ref=out_refs[a].at[mine_ if started else theirs],
                send_sem=send_sems.at[3 * a + mask - 1], recv_sem=recv_sems.at[3 * a + mask - 1],
                device_id=(px, py, c), device_id_type=_MESH)

        mine = [pltpu.make_async_copy(h_refs[a].at[me], out_refs[a].at[me], local_sems.at[a]) for a in range(na)]
        for cp in mine:
            cp.start()
        sends = [copy(a, mask, True) for a in range(na) for mask in range(1, N_CHIP)]
        for cp in sends:
            cp.start()
        for a in range(na):
            for mask in range(1, N_CHIP):
                copy(a, mask, False).wait_recv()
        for cp in sends:
            cp.wait_send()
        for cp in mine:
            cp.wait()

    return pl.pallas_call(
        body, name=name,
        out_shape=[jax.ShapeDtypeStruct(t.shape, t.dtype) for t in hs],
        in_specs=[pl.BlockSpec(memory_space=pl.ANY)] * na, out_specs=[pl.BlockSpec(memory_space=pl.ANY)] * na,
        scratch_shapes=[pltpu.SemaphoreType.DMA((3 * na,)), pltpu.SemaphoreType.DMA((3 * na,)),
                        pltpu.SemaphoreType.DMA((na,))],
    )(*hs)


def _rows_of(shape):
    return -(-int(np.prod(shape)) // PACK_W)


def _pack(arrs, dtype, lead=0, total_rows=None):
    pieces = []
    for a in arrs:
        f = a.reshape(a.shape[:lead] + (-1,)).astype(dtype)
        pad = (-f.shape[-1]) % PACK_W
        if pad:
            f = jnp.pad(f, [(0, 0)] * lead + [(0, pad)])
        pieces.append(f.reshape(a.shape[:lead] + (-1, PACK_W)))
    buf = jnp.concatenate(pieces, axis=lead)
    if total_rows is not None and buf.shape[lead] < total_rows:
        buf = jnp.pad(buf, [(0, 0)] * lead + [(0, total_rows - buf.shape[lead]), (0, 0)])
    return buf


def _unpack(buf, shapes, lead=0):
    out, r = [], 0
    for shp in shapes:
        n, rows = int(np.prod(shp)), _rows_of(shp)
        piece = buf[(slice(None),) * lead + (slice(r, r + rows),)]
        piece = piece.reshape(buf.shape[:lead] + (-1,))[..., :n]
        out.append(piece.reshape(buf.shape[:lead] + tuple(shp)))
        r += rows
    return out


def _pack_flat(arrs, total_rows):
    flat = jnp.concatenate([t.reshape(-1).astype(F32) for t in arrs])
    return jnp.pad(flat, (0, total_rows * PACK_W - flat.shape[0])).reshape(total_rows, PACK_W)


def _unpack_flat(buf, shapes):
    flat, out, off = buf.reshape(-1), [], 0
    for shp in shapes:
        n = int(np.prod(shp))
        out.append(flat[off:off + n].reshape(shp))
        off += n
    return out


def _to_full(parts):
    dep, r = parts.shape[1:3]
    return jnp.transpose(parts, (1, 0) + tuple(range(2, parts.ndim))).reshape((dep, N_DEV * r) + parts.shape[3:])


def _to_slabs(full):
    dep, r = full.shape[:2]
    t = full.reshape((dep, N_DEV, r // N_DEV) + full.shape[2:])
    return jnp.transpose(t, (1, 0) + tuple(range(2, t.ndim)))


def _ref_cols(parts, ro, wd):
    w, out = parts.shape[2], []
    for dev in range(N_DEV):
        lo, hi = max(ro, dev * w), min(ro + wd, (dev + 1) * w)
        if lo < hi:
            out.append(parts[dev][:, lo - dev * w:hi - dev * w])
    return out


def _w_in_to_layout(parts, seg, rseg, nh2):
    D = parts.shape[1]
    cols, off = [], 0
    names = sorted([k for k in seg if not k.startswith('_')], key=lambda k: seg[k][0])
    for nm in names:
        o, wd = seg[nm]
        if o > off:
            cols.append(jnp.zeros((D, o - off), parts.dtype))
        if nm == 'dadb':
            cols += _ref_cols(parts, rseg['da'][0], nh2) + _ref_cols(parts, rseg['db'][0], nh2)
            cols.append(jnp.zeros((D, wd - 2 * nh2), parts.dtype))
        else:
            cols += _ref_cols(parts, rseg[nm][0], wd)
        off = o + wd
    if seg['_total'] > off:
        cols.append(jnp.zeros((D, seg['_total'] - off), parts.dtype))
    return jnp.concatenate(cols, axis=1)


def _w_in_slabs(dw, seg, rseg, nh2):
    w = rseg['_total'] // N_DEV
    ref = []
    for nm in sorted([k for k in rseg if not k.startswith('_')], key=lambda k: rseg[k][0]):
        lo = {'da': seg['dadb'][0], 'db': seg['dadb'][0] + nh2}.get(nm)
        ref.append((rseg[nm][0], rseg[nm][1], seg[nm][0] if lo is None else lo))
    slabs = []
    for dev in range(N_DEV):
        cols = []
        for ro, wd, lo in ref:
            a, b = max(ro, dev * w), min(ro + wd, (dev + 1) * w)
            if a < b:
                cols.append(dw[:, lo + a - ro:lo + b - ro])
        slabs.append(jnp.concatenate(cols, axis=1))
    return jnp.stack(slabs, axis=0)


def _assemble_dh(pieces, seg, L):
    cols, off = [], 0
    for nm in sorted(pieces, key=lambda k: seg[k][0]):
        o = seg[nm][0]
        if o > off:
            cols.append(jnp.zeros((L, o - off), F32))
        cols.append(pieces[nm])
        off = o + pieces[nm].shape[1]
    if seg['_total'] > off:
        cols.append(jnp.zeros((L, seg['_total'] - off), F32))
    return jnp.concatenate(cols, axis=1)


def _lane_pad(v):
    v = v.reshape(1, -1)
    return jnp.pad(v, ((0, 0), (0, LANE - v.shape[1])))


def _rope_tables(L, c):
    rows = L // c['GRID_W']
    row = jnp.repeat(jnp.arange(rows), c['GRID_W']).astype(F32)
    col = jnp.tile(jnp.arange(c['GRID_W']), rows).astype(F32)
    axis_dim = c['AD'] // 2
    freqs = c['ROPE_THETA'] ** (-jnp.arange(0, axis_dim, 2, dtype=F32) / axis_dim)
    ang = jnp.concatenate([row[:, None] * freqs, col[:, None] * freqs], axis=-1)
    cosf = jnp.repeat(jnp.cos(ang), 2, axis=1)
    sn = jnp.sin(ang)
    sins = jnp.stack([-sn, sn], axis=-1).reshape(L, c['AD'])
    idx = np.arange(c['AD'])
    perm = np.zeros((c['AD'], c['AD']), np.float32)
    perm[idx, idx ^ 1] = 1.0
    return cosf, sins, jnp.asarray(perm)


def _s5_dir_params(a, l, dr):
    return (a['ssm_a_re'][l, dr], a['ssm_a_im'][l, dr], a['ssm_log_step'][l, dr], a['ssm_b_re'][l, dr],
            a['ssm_b_im'][l, dr], a['ssm_c_re'][l, dr], a['ssm_c_im'][l, dr])


def _layer_fwd(x, mem, l, wt, a, rope, c, d, seg):
    L, D = x.shape
    SW, DW, AW, AKW, MW, H = d['SW'], d['DW'], d['AW'], d['AKW'], d['MW'], d['DNH']
    cb = lambda nm: seg[nm][0] // seg[nm][1]
    sv = {'x': x}
    p = f"l{l}_"
    sv['g_norm'] = a['norm_g'][l][None, :]
    xn, = _rowwise(_f_norm, [(x, D, 0)], [sv['g_norm']], [(D, BF16)], tm=256, name=p + "norm")
    h = _mm(xn, wt['wp'], name=p + "in_proj", tm=1024, tn=1536, tk=1024)
    sv['xn'], sv['h'] = xn, h

    ysum, sv['s5'] = None, []
    for dr in range(2):
        wb, wc, lr, li = _s5_prep(*_s5_dir_params(a, l, dr), d)
        wb16, wc16 = wb.astype(BF16), wc.astype(BF16)
        lt = _s5_tables(lr, li, bool(dr), False)
        ysum, cin = _s5_fwd(h, cb('u_a'), wb16, wc16, lt, rev=bool(dr), acc=ysum, tb=TILES['s5_t'],
                            name=p + f"s5_fwd{dr}", d=d)
        sv['s5'].append((wb16, wc16, lt, _s5_tables(lr, li, not bool(dr), True), cin))
    sv['ysum'] = ysum
    sv['s5_par'] = [a['ssm_d'][l][None, :], wt['w_glu'], a['ssm_b_glu'][l][None, :]]
    sv['s5_rows'] = [(ysum, SW, 0), (h, SW, cb('u_a')), (h, SW, cb('z_a'))]
    y_a, = _rowwise(_f_s5tail, sv['s5_rows'], sv['s5_par'], [(SW, F32)], tm=256, name=p + "s5_tail")

    act = _conv_fwd(h, cb('dq'), wt['conv'], tm=256, name=p + "dn_conv", d=d)
    sv['act'] = act
    sv['dn_par'] = [_lane_pad(a['dn_a_log'][l]), _lane_pad(a['dn_dt_bias'][l])]
    sv['dn_rows'] = [(act, DW, 0), (act, DW, 1), (h, LANE, seg['dadb'][0] // LANE)]
    dn_out = _rowwise(_make_f_dnpre(H, d['DNK'], c['CHUNK']), sv['dn_rows'], sv['dn_par'], [(DW, F32)] * 8,
                      tm=256, name=p + "dn_pre")
    qn, kn = dn_out[:2]
    sv['qn'], sv['kn'], sv['gates'] = qn, kn, [dn_out[2:5], dn_out[5:8]]
    o_dn, sv['dn_state'] = None, []
    for dr in range(2):
        o_dn, ss = _delta_fwd(qn, kn, act, sv['gates'][dr], vcb=2, rev=bool(dr), acc=o_dn,
                              name=p + f"dn_fwd{dr}", d=d)
        sv['dn_state'].append(ss)
    sv['dnpost_rows'] = [(o_dn, DW, 0), (h, DW, cb('z_b'))]
    sv['dnpost_par'] = [a['dn_norm_g'][l][None, :]]
    y_b, = _rowwise(_make_f_dnpost(d['DNK']), sv['dnpost_rows'], sv['dnpost_par'], [(DW, F32)], tm=256,
                    name=p + "dn_post")

    cosf, sins, perm = rope
    sv['att_par'] = [perm, a['attn_q_norm'][l][None, :], a['attn_k_norm'][l][None, :]]
    qh, kh, vh = _rowwise(_make_f_attpre(d['AD'], True),
                          [(h, AW, cb('aq')), (h, AKW, cb('ak')), (h, AKW, cb('av')), (cosf, d['AD'], 0),
                           (sins, d['AD'], 0)], sv['att_par'], [(AW, BF16), (AKW, BF16), (AKW, BF16)],
                          tm=256, name=p + "att_pre")
    o_att, lse = _attn_fwd(qh, kh, vh, tq=TILES['att_q'], tk=TILES['att_k'], name=p + "att_fwd", d=d)
    sv['qh'], sv['kh'], sv['vh'], sv['o_att'], sv['lse'] = qh, kh, vh, o_att, lse
    y_c, = _rowwise(_f_gate, [(o_att, AW, 0), (h, AW, cb('z_c'))], [], [(AW, F32)], tm=256, name=p + "att_post")

    sv['g_mem'] = a['mem_norm_g'][l][None, :]
    memn, = _rowwise(_f_norm, [(mem, D, 0)], [sv['g_mem']], [(D, BF16)], tm=256, name=p + "mem_norm")
    kv = _mm(memn, wt['w_mem_kv'], name=p + "mem_kv")
    sv['memn'], sv['kv'] = memn, kv
    y_m, = _rowwise(_make_f_mem(d['MH'], d['MD']), [(h, MW, cb('mq')), (h, MW, cb('z_m'))], [kv], [(MW, F32)],
                    tm=256, name=p + "mem_attn")

    ys = [y_a, y_b, y_c, y_m]
    ps = [_mm(y, wb_, name=p + f"branch_proj{i}", out_dtype=BF16)
          for i, (y, wb_) in enumerate(zip(ys, wt['w_branch']))]
    gcb = seg['gates'][0] // D
    sv['merge_rows'] = [(pp, D, 0) for pp in ps] + [(h, D, gcb + i) for i in range(4)]
    merged, = _rowwise(_f_merge, sv['merge_rows'], [], [(D, BF16)], tm=128, name=p + "merge")
    sv['ys'], sv['merged'] = ys, merged
    return _mm(merged, wt['w_out'], add=x, name=p + "out_proj"), sv


def _layer_bwd(dx, mem, l, wt, a, rope, sv, c, d, seg):
    L, D = dx.shape
    SW, DW, AW, AKW, MW, H = d['SW'], d['DW'], d['AW'], d['AKW'], d['MW'], d['DNH']
    cb = lambda nm: seg[nm][0] // seg[nm][1]
    p = f"l{l}_"
    h = sv['h']
    gr = {}
    dmerged = _mm(dx, wt['w_out'], tb=True, name=p + "d_merged")
    gr['w_out'] = _mm(sv['merged'], dx, ta=True, name=p + "dw_out")
    dmr, _ = _rowwise_bwd(_f_merge, sv['merge_rows'], [], [[(dmerged, D, 0)]], [True] * 8, [], tm=128,
                          name=p + "merge_bwd", row_grad_dtype=BF16)
    dps, dgates = dmr[:4], dmr[4:]
    dys = [_mm(dp, wb_, tb=True, name=p + f"d_branch{i}") for i, (dp, wb_) in enumerate(zip(dps, wt['w_branch']))]
    gr['w_branch'] = jnp.concatenate(
        [_mm(y, dp, ta=True, name=p + f"dw_branch{i}") for i, (y, dp) in enumerate(zip(sv['ys'], dps))], axis=0)

    (dmq, dzm), (dkv,) = _rowwise_bwd(_make_f_mem(d['MH'], d['MD']), [(h, MW, cb('mq')), (h, MW, cb('z_m'))],
                                      [sv['kv']], [[(dys[3], MW, 0)]], [True, True], [True], tm=256,
                                      name=p + "mem_attn_bwd")
    gr['w_mem_kv'] = _mm(sv['memn'], dkv, ta=True, name=p + "dw_mem_kv")
    dmemn = _mm(dkv, wt['w_mem_kv'], tb=True, name=p + "d_memn")
    _, (dg_mem,) = _rowwise_bwd(_f_norm, [(mem, D, 0)], [sv['g_mem']], [[(dmemn, D, 0)]], [False], [True], tm=256,
                                name=p + "mem_norm_bwd")
    gr['mem_norm_g'] = dg_mem[0]

    (do_att, dzc), _ = _rowwise_bwd(_f_gate, [(sv['o_att'], AW, 0), (h, AW, cb('z_c'))], [], [[(dys[2], AW, 0)]],
                                    [True, True], [], tm=256, name=p + "att_post_bwd")
    delta, = _rowwise(_make_f_delta(d['AD']), [(do_att, AW, 0), (sv['o_att'], AW, 0)], [], [(AW, F32)], tm=256,
                      name=p + "att_delta")
    att_in = (sv['qh'], sv['kh'], sv['vh'], do_att, sv['lse'], delta)
    dqh, dkh, dvh = _attn_bwd(*att_in, tq=TILES['att_q'], tk=TILES['att_k'], name=p + "att_bwd", d=d)
    cosf, sins, _ = rope
    (daq, dak), (dqg, dkg) = _rowwise_bwd(
        _make_f_attpre(d['AD'], False),
        [(h, AW, cb('aq')), (h, AKW, cb('ak')), (cosf, d['AD'], 0), (sins, d['AD'], 0)], sv['att_par'],
        [[(dqh, AW, 0)], [(dkh, AKW, 0)]], [True, True, False, False], [False, True, True], tm=256,
        name=p + "att_pre_bwd")
    gr['attn_q_norm'], gr['attn_k_norm'] = dqg[0], dkg[0]

    (do_dn, dzb), (dng,) = _rowwise_bwd(_make_f_dnpost(d['DNK']), sv['dnpost_rows'], sv['dnpost_par'],
                                        [[(dys[1], DW, 0)]], [True, True], [True], tm=256, name=p + "dn_post_bwd")
    gr['dn_norm_g'] = dng[0]
    accs, dn_dgates = None, []
    for dr in range(2):
        res = _delta_bwd(sv['qn'], sv['kn'], sv['act'], sv['gates'][dr], sv['dn_state'][dr], do_dn, vcb=2,
                         rev=bool(dr), accs=accs, name=p + f"dn_bwd{dr}", d=d)
        accs = res[:3]
        dn_dgates += res[3:]
    dqn, dkn, dvc = accs
    (dqc, dkc, ddadb), (dalog, ddtb) = _rowwise_bwd(
        _make_f_dnpre(H, d['DNK'], c['CHUNK']), sv['dn_rows'], sv['dn_par'],
        [[(t, DW, 0)] for t in [dqn, dkn] + dn_dgates], [True] * 3, [True, True], tm=256, name=p + "dn_pre_bwd")
    gr['dn_a_log'] = dalog[0, :2 * H].reshape(2, H)
    gr['dn_dt_bias'] = ddtb[0, :2 * H].reshape(2, H)
    dconv_x, dconv_w = _conv_bwd(h, cb('dq'), wt['conv'], jnp.concatenate([dqc, dkc, dvc], axis=1), tm=256,
                                 name=p + "dn_conv_bwd", d=d)
    gr['dn_conv'] = jnp.transpose(dconv_w[:, :c['CONV'], :], (0, 2, 1)).reshape(3 * DW, c['CONV'])

    (dysum, du, dza), (dd, dwglu, dbglu) = _rowwise_bwd(_f_s5tail, sv['s5_rows'], sv['s5_par'], [[(dys[0], SW, 0)]],
                                                        [True] * 3, [True] * 3, tm=256, name=p + "s5_tail_bwd")
    gr['ssm_d'], gr['ssm_w_glu'], gr['ssm_b_glu'] = dd[0], dwglu, dbglu[0]
    s5g = []
    for dr in range(2):
        wb16, wc16, lt, lt_adj, cin = sv['s5'][dr]
        du, dwb, dwc, dlam = _s5_bwd(h, cb('u_a'), dysum, cin, wb16, wc16, lt, lt_adj, rev=bool(dr), acc=du,
                                     tb=TILES['s5_t'],
                                     name=p + f"s5_bwd{dr}", d=d)
        dl = jnp.sum(dlam, axis=0).reshape(d['NB'], 2, d['BS'])
        _, prep_vjp = jax.vjp(lambda *pp: _s5_prep(*pp, d), *_s5_dir_params(a, l, dr))
        s5g.append(prep_vjp((dwb, dwc, dl[:, 0], dl[:, 1])))
    for i, nm in enumerate(['ssm_a_re', 'ssm_a_im', 'ssm_log_step', 'ssm_b_re', 'ssm_b_im', 'ssm_c_re', 'ssm_c_im']):
        gr[nm] = jnp.stack([s5g[0][i], s5g[1][i]], axis=0)

    dh = _assemble_dh({'u_a': du, 'z_a': dza, 'dq': dconv_x, 'z_b': dzb, 'ak': dak, 'av': dvh, 'aq': daq,
                       'z_c': dzc, 'mq': dmq, 'z_m': dzm, 'gates': jnp.concatenate(dgates, axis=1),
                       'dadb': ddadb}, seg, L).astype(BF16)
    gr['wp'] = _mm(sv['xn'], dh, ta=True, name=p + "dw_in", tm=1024, tn=1536, tk=1024)
    dxn = _mm(dh, wt['wp'], tb=True, name=p + "d_xn", tm=1024, tn=1024, tk=1536)
    (dx_in,), (dg_norm,) = _rowwise_bwd(_f_norm, [(sv['x'], D, 0)], [sv['g_norm']], [[(dxn, D, 0)]], [True], [True],
                                        tm=256, name=p + "norm_bwd", accs={0: (dx, D, 0)})
    gr['norm_g'] = dg_norm[0]
    return dx_in, gr


_ARG_NAMES = (['x', 'mem'] + WEIGHTS + ['loss_target'] + ['m_' + w for w in WEIGHTS] + ['v_' + w for w in WEIGHTS])


def kernel(x, mem, norm_g, w_in, ssm_a_re, ssm_a_im, ssm_log_step, ssm_b_re, ssm_b_im, ssm_c_re, ssm_c_im,
           ssm_d, ssm_w_glu, ssm_b_glu, dn_conv, dn_a_log, dn_dt_bias, dn_norm_g, attn_q_norm, attn_k_norm,
           mem_norm_g, w_mem_kv, w_branch, w_out, final_norm_g, loss_target, m_norm_g, m_w_in, m_ssm_a_re,
           m_ssm_a_im, m_ssm_log_step, m_ssm_b_re, m_ssm_b_im, m_ssm_c_re, m_ssm_c_im, m_ssm_d, m_ssm_w_glu,
           m_ssm_b_glu, m_dn_conv, m_dn_a_log, m_dn_dt_bias, m_dn_norm_g, m_attn_q_norm, m_attn_k_norm,
           m_mem_norm_g, m_w_mem_kv, m_w_branch, m_w_out, m_final_norm_g, v_norm_g, v_w_in, v_ssm_a_re,
           v_ssm_a_im, v_ssm_log_step, v_ssm_b_re, v_ssm_b_im, v_ssm_c_re, v_ssm_c_im, v_ssm_d, v_ssm_w_glu,
           v_ssm_b_glu, v_dn_conv, v_dn_a_log, v_dn_dt_bias, v_dn_norm_g, v_attn_q_norm, v_attn_k_norm,
           v_mem_norm_g, v_w_mem_kv, v_w_branch, v_w_out, v_final_norm_g):
    given = locals()
    return _train_step({n: given[n] for n in _ARG_NAMES})


def _train_step(a):
    c = CFG
    d = _dims(c)
    seg, rseg = _layout(c)
    depth, nh2 = c['DEPTH'], 2 * c['DNH']
    x, mem, tgt = a['x'][0], a['mem'][0], a['loss_target'][0]
    L, D = x.shape

    packed = [n for n in SHARDED if n != 'w_in']
    shard_shapes = [a[n].shape for n in packed]
    rw = _round_up(sum(_rows_of(s) for s in shard_shapes), LANE)
    win_shape = a['w_in'].shape
    wcols = win_shape[2]
    g_win, gathered = _all_gather([a['w_in'].astype(BF16).reshape(depth * D, wcols),
                                   _pack([a[n] for n in packed], BF16, total_rows=rw)], name="weights_all_gather")
    full = {n: _to_full(p_) for n, p_ in zip(packed, _unpack(gathered, shard_shapes, lead=1))}
    offs = np.cumsum([0, d['SW'], d['DW'], d['AW'], d['MW']])
    wts = []
    for l in range(depth):
        conv = jnp.transpose(full['dn_conv'][l].astype(F32).reshape(3, d['DW'], c['CONV']), (0, 2, 1))
        wts.append(dict(
            wp=_w_in_to_layout(g_win[:, l * D:(l + 1) * D], seg, rseg, nh2),
            w_branch=[full['w_branch'][l, offs[i]:offs[i + 1]] for i in range(4)],
            w_out=full['w_out'][l], w_mem_kv=full['w_mem_kv'][l], w_glu=full['ssm_w_glu'][l].astype(F32),
            conv=jnp.pad(conv, ((0, 0), (0, 8 - c['CONV']), (0, 0)))))
    rope = _rope_tables(L, c)

    saved = []
    for l in range(depth):
        x, sv = _layer_fwd(x, mem, l, wts[l], a, rope, c, d, seg)
        saved.append(sv)
    loss_part, dx, dg_final = _loss_grad(x, a['final_norm_g'][None, :], tgt, tm=256, name="final_norm_loss")
    grads = [None] * depth
    for l in reversed(range(depth)):
        dx, grads[l] = _layer_bwd(dx, mem, l, wts[l], a, rope, saved[l], c, d, seg)

    gfull = {n: jnp.stack([grads[l][n] for l in range(depth)], axis=0) for n in WEIGHTS
             if n not in ('w_in', 'final_norm_g')}
    gfull['final_norm_g'] = dg_final[0]

    win_slabs = jnp.concatenate([_w_in_slabs(grads[l]['wp'], seg, rseg, nh2) for l in range(depth)], axis=1)
    small_shapes = [a[n].shape for n in SMALL] + [(1,)]
    rs = _round_up(_rows_of((sum(int(np.prod(s)) for s in small_shapes),)), LANE)
    g_shard = _pack([_to_slabs(gfull[n]) for n in packed], F32, lead=1, total_rows=rw)
    g_small = _pack_flat([gfull[n] for n in SMALL] + [loss_part[0, :1]], rs)
    slabs = [win_slabs, g_shard, jnp.broadcast_to(g_small[None], (N_DEV,) + g_small.shape)]
    core = lax.axis_index("c").astype(jnp.int32).reshape(1)
    from_sibling = _pair_exchange(slabs, name="grads_pair_exchange")
    pair_sums = [_pair_sum(g, r, core, out_dtype=dt, tr=256, name=f"grads_pair_sum{i}")
                 for i, (g, r, dt) in enumerate(zip(slabs, from_sibling, (BF16, BF16, F32)))]
    recv = _chip_exchange(pair_sums, name="grads_chip_exchange")
    g_win_sum = _sum_slots(recv[0], tr=256, name="w_in_grad_sum")
    gsum = jnp.concatenate([_sum_slots(recv[1], tr=256, name="shard_grad_sum"),
                            _sum_slots(recv[2], tr=256, name="small_grad_sum")], axis=0)
    flat = lambda t: t.reshape(depth * D, wcols)
    d_win, m_win, v_win = _adamw(flat(a['w_in']), g_win_sum, flat(a['m_w_in']), flat(a['v_w_in']), tr=256,
                                 name="w_in_adamw")
    win_out = [t.reshape(win_shape) for t in (g_win_sum, d_win, m_win, v_win)]

    def local_pack(prefix):
        zero = jnp.zeros((1,), F32)
        return jnp.concatenate([_pack([a[prefix + n] for n in packed], F32, total_rows=rw),
                                _pack_flat([a[prefix + n] for n in SMALL] + [zero], rs)], axis=0)

    delta, new_m, new_v = _adamw(local_pack(''), gsum, local_pack('m_'), local_pack('v_'), tr=256, name="adamw")

    def split(buf):
        vals = dict(zip(packed, _unpack(buf[:rw], shard_shapes)))
        small = _unpack_flat(buf[rw:], small_shapes)
        vals.update(zip(SMALL, small[:-1]))
        return vals, small[-1]

    _, loss = split(gsum)
    outs = [loss.reshape(()), dx[None]]
    for i, buf in enumerate((gsum, delta, new_m, new_v)):
        vals, _ = split(buf)
        vals['w_in'] = win_out[i]
        outs += [vals[n] for n in WEIGHTS]
    return tuple(outs)
```

```python
import functools
import math

import numpy as np
import jax
import jax.numpy as jnp
from jax import lax
from jax.experimental import pallas as pl
from jax.experimental.pallas import tpu as pltpu

F32 = jnp.float32
BF16 = jnp.bfloat16
HI = lax.Precision.HIGHEST
EPS = 1e-6
LANE = 128
SUBLANE = 8
VMEM_LIMIT = 56 * 1024 * 1024
N_DEV = 8
PACK_W = 1024

ADAM_LR, ADAM_B1, ADAM_B2, ADAM_EPS, ADAM_WD, ADAM_STEP = 0.001, 0.9, 0.999, 1e-08, 0.01, 10

CFG = dict(D=2048, L=8192, GRID_W=64, NMEM=256, DEPTH=2,
           SG=48, SP=16, SN=64,
           DNH=6, DNK=128, CONV=5, CHUNK=64,
           AH=8, AKV=2, AD=128, ROPE_THETA=10000.0,
           MH=4, MD=128)

TILES = dict(att_q=1024, att_k=2048, s5_t=512)

WEIGHTS = ['norm_g', 'w_in', 'ssm_a_re', 'ssm_a_im', 'ssm_log_step', 'ssm_b_re', 'ssm_b_im', 'ssm_c_re',
           'ssm_c_im', 'ssm_d', 'ssm_w_glu', 'ssm_b_glu', 'dn_conv', 'dn_a_log', 'dn_dt_bias', 'dn_norm_g',
           'attn_q_norm', 'attn_k_norm', 'mem_norm_g', 'w_mem_kv', 'w_branch', 'w_out', 'final_norm_g']
SHARDED = ['w_in', 'w_branch', 'w_out', 'w_mem_kv', 'ssm_w_glu', 'dn_conv']
SMALL = [w for w in WEIGHTS if w not in SHARDED]


def _dims(c):
    d = dict(c)
    d['SW'] = c['SG'] * c['SP']
    d['NB'] = d['SW'] // LANE
    d['GPB'] = LANE // c['SP']
    d['BS'] = d['GPB'] * c['SN']
    d['DW'] = c['DNH'] * c['DNK']
    d['AW'] = c['AH'] * c['AD']
    d['AKW'] = c['AKV'] * c['AD']
    d['MW'] = c['MH'] * c['MD']
    d['BT'] = d['SW'] + d['DW'] + d['AW'] + d['MW']
    return d


def _round_up(a, b):
    return (a + b - 1) // b * b


def _layout(c):
    d = _dims(c)
    D, SW, DW, AW, AKW, MW = d['D'], d['SW'], d['DW'], d['AW'], d['AKW'], d['MW']
    order = [('u_a', SW, SW), ('z_a', SW, SW), ('dq', DW, DW), ('dk', DW, DW), ('dv', DW, DW), ('z_b', DW, DW),
             ('ak', AKW, AKW), ('av', AKW, AKW), ('aq', AW, AW), ('z_c', AW, AW), ('mq', MW, MW), ('z_m', MW, MW),
             ('gates', 4 * D, D), ('dadb', LANE, LANE)]
    off, seg = 0, {}
    for name, w, al in order:
        off = _round_up(off, al)
        seg[name] = (off, w)
        off += w
    seg['_total'] = _round_up(off, 512)
    ref_order = [('u_a', SW), ('z_a', SW), ('dq', DW), ('dk', DW), ('dv', DW), ('da', 2 * d['DNH']),
                 ('db', 2 * d['DNH']), ('z_b', DW), ('aq', AW), ('ak', AKW), ('av', AKW), ('z_c', AW),
                 ('mq', MW), ('z_m', MW), ('gates', 4 * D)]
    roff, rseg = 0, {}
    for name, w in ref_order:
        rseg[name] = (roff, w)
        roff += w
    rseg['_total'] = roff
    return seg, rseg


def _cparams(sem):
    return pltpu.CompilerParams(dimension_semantics=sem, vmem_limit_bytes=VMEM_LIMIT)


def _pick(t, n):
    if n <= t:
        return n
    for align in (LANE, 2 * SUBLANE):
        for cand in range(t - t % align, 0, -align):
            if n % cand == 0:
                return cand
    return n


def _mm(a, b, *, name, ta=False, tb=False, add=None, out_dtype=F32, tm=1024, tn=1024, tk=1024):
    M, K = (a.shape[1], a.shape[0]) if ta else a.shape
    N = b.shape[0] if tb else b.shape[1]
    assert (b.shape[1] if tb else b.shape[0]) == K
    tm, tn, tk = _pick(tm, M), _pick(tn, N), _pick(tk, K)
    nk = K // tk
    dn = (((0 if ta else 1,), (1 if tb else 0,)), ((), ()))
    has_add = add is not None

    def body(*refs):
        if has_add:
            a_ref, b_ref, add_ref, o_ref, acc = refs
        else:
            a_ref, b_ref, o_ref, acc = refs
        k = pl.program_id(2)

        @pl.when(k == 0)
        def _():
            acc[...] = jnp.zeros_like(acc)

        acc[...] += lax.dot_general(a_ref[...].astype(BF16), b_ref[...].astype(BF16), dn,
                                    preferred_element_type=F32)

        @pl.when(k == nk - 1)
        def _():
            r = acc[...]
            if has_add:
                r = r + add_ref[...]
            o_ref[...] = r.astype(o_ref.dtype)

    a_spec = pl.BlockSpec((tk, tm), lambda i, j, k: (k, i)) if ta else pl.BlockSpec((tm, tk), lambda i, j, k: (i, k))
    b_spec = pl.BlockSpec((tn, tk), lambda i, j, k: (j, k)) if tb else pl.BlockSpec((tk, tn), lambda i, j, k: (k, j))
    in_specs = [a_spec, b_spec]
    args = [a, b]
    if has_add:
        in_specs.append(pl.BlockSpec((tm, tn), lambda i, j, k: (i, j)))
        args.append(add)
    return pl.pallas_call(
        body, name=name, grid=(M // tm, N // tn, nk),
        in_specs=in_specs, out_specs=pl.BlockSpec((tm, tn), lambda i, j, k: (i, j)),
        out_shape=jax.ShapeDtypeStruct((M, N), out_dtype),
        scratch_shapes=[pltpu.VMEM((tm, tn), F32)],
        compiler_params=_cparams(("parallel", "parallel", "arbitrary")),
    )(*args)


def _row_spec(tm, w, cb):
    return pl.BlockSpec((tm, w), lambda i, cb=cb: (i, cb))


def _rowwise(fn, rows, params, outs, *, tm, name):
    L = rows[0][0].shape[0]
    tm = _pick(tm, L)
    nr, npar = len(rows), len(params)

    def body(*refs):
        vals = [r[...] for r in refs[:nr + npar]]
        res = fn(*vals)
        for o_ref, v in zip(refs[nr + npar:], res):
            o_ref[...] = v.astype(o_ref.dtype)

    in_specs = [_row_spec(tm, w, cb) for (_, w, cb) in rows]
    in_specs += [pl.BlockSpec(p.shape, lambda i: (0, 0)) for p in params]
    res = pl.pallas_call(
        body, name=name, grid=(L // tm,), in_specs=in_specs,
        out_specs=[pl.BlockSpec((tm, w), lambda i: (i, 0)) for (w, _) in outs],
        out_shape=[jax.ShapeDtypeStruct((L, w), dt) for (w, dt) in outs],
        compiler_params=_cparams(("parallel",)),
    )(*[r[0] for r in rows], *params)
    return list(res)


def _rowwise_bwd(fn, rows, params, cts, drows, dparams, *, tm, name, accs=None, row_grad_dtype=F32):
    L = rows[0][0].shape[0]
    tm = _pick(tm, L)
    nr, npar = len(rows), len(params)
    accs = accs or {}
    ct_flat = [c for grp in cts for c in grp]
    ct_sizes = [len(grp) for grp in cts]
    acc_keys = sorted(accs)
    d_r = [i for i in range(nr) if drows[i]]
    d_p = [i for i in range(npar) if dparams[i]]
    n_in = nr + npar + len(ct_flat) + len(acc_keys)

    def body(*refs):
        vals = [r[...] for r in refs[:nr + npar]]
        ct_refs = refs[nr + npar:nr + npar + len(ct_flat)]
        acc_refs = refs[nr + npar + len(ct_flat):n_in]
        o_refs = refs[n_in:]
        ct_vals, pos = [], 0
        for n in ct_sizes:
            v = ct_refs[pos][...].astype(F32)
            for r in ct_refs[pos + 1:pos + n]:
                v = v + r[...].astype(F32)
            ct_vals.append(v)
            pos += n
        diff_idx = d_r + [nr + i for i in d_p]

        def g(*dv):
            full = list(vals)
            for i, v in zip(diff_idx, dv):
                full[i] = v
            return tuple(o.astype(F32) for o in fn(*full))

        _, vjp = jax.vjp(g, *[vals[i] for i in diff_idx])
        grads = vjp(tuple(ct_vals))
        for n, i in enumerate(d_r):
            gv = grads[n].astype(F32)
            if i in accs:
                gv = gv + acc_refs[acc_keys.index(i)][...]
            o_refs[n][...] = gv.astype(o_refs[n].dtype)
        step = pl.program_id(0)
        for n, i in enumerate(d_p):
            o_ref = o_refs[len(d_r) + n]

            @pl.when(step == 0)
            def _(o_ref=o_ref):
                o_ref[...] = jnp.zeros_like(o_ref)

            o_ref[...] += grads[len(d_r) + n].astype(F32)

    in_specs = [_row_spec(tm, w, cb) for (_, w, cb) in rows]
    in_specs += [pl.BlockSpec(p.shape, lambda i: (0, 0)) for p in params]
    in_specs += [_row_spec(tm, w, cb) for (_, w, cb) in ct_flat]
    in_specs += [_row_spec(tm, accs[k][1], accs[k][2]) for k in acc_keys]
    out_specs = [pl.BlockSpec((tm, rows[i][1]), lambda i_: (i_, 0)) for i in d_r]
    out_specs += [pl.BlockSpec(params[i].shape, lambda i_: (0, 0)) for i in d_p]
    out_shape = [jax.ShapeDtypeStruct((L, rows[i][1]), row_grad_dtype) for i in d_r]
    out_shape += [jax.ShapeDtypeStruct(params[i].shape, F32) for i in d_p]
    res = pl.pallas_call(
        body, name=name, grid=(L // tm,), in_specs=in_specs, out_specs=out_specs, out_shape=out_shape,
        compiler_params=_cparams(("arbitrary",)),
    )(*[r[0] for r in rows], *params, *[c[0] for c in ct_flat], *[accs[k][0] for k in acc_keys])
    res = list(res)
    return res[:len(d_r)], res[len(d_r):]


def _silu(x):
    return x * jax.nn.sigmoid(x)


def _rms(x, g):
    return x * lax.rsqrt(jnp.mean(x * x, axis=-1, keepdims=True) + EPS) * g


def _softplus(x):
    return jnp.maximum(x, 0.0) + jnp.log1p(jnp.exp(-jnp.abs(x)))


def _heads(x, hd):
    return [x[:, i * hd:(i + 1) * hd] for i in range(x.shape[1] // hd)]


def _f_norm(x, g):
    return (_rms(x, g),)


def _f_s5tail(ys, u, z, d, wglu, bglu):
    y = jax.nn.gelu(ys + d * u)
    gate = jax.nn.sigmoid(jnp.dot(y.astype(BF16), wglu.astype(BF16), preferred_element_type=F32) + bglu)
    return (y * gate * _silu(z),)


def _make_f_dnpre(nh, hd, chunk):
    def f(qc, kc, dadb, alog, dtb):
        tm = qc.shape[0]
        qn = [q * lax.rsqrt(jnp.sum(q * q, axis=-1, keepdims=True) + EPS) * (hd ** -0.5) for q in _heads(qc, hd)]
        kn = [k * lax.rsqrt(jnp.sum(k * k, axis=-1, keepdims=True) + EPS) for k in _heads(kc, hd)]
        g = -jnp.exp(alog) * _softplus(dadb + dtb)
        beta = jax.nn.sigmoid(dadb)
        ii = lax.broadcasted_iota(jnp.int32, (tm, tm), 0)
        jj = lax.broadcasted_iota(jnp.int32, (tm, tm), 1)
        same = (ii // chunk) == (jj // chunk)
        outs = [jnp.concatenate(qn, axis=1), jnp.concatenate(kn, axis=1)]
        gt = jnp.dot(same.astype(F32), g, precision=HI, preferred_element_type=F32)
        for dr in range(2):
            tri = jnp.logical_and(same, (ii <= jj) if dr else (ii >= jj)).astype(F32)
            gc = jnp.dot(tri, g, precision=HI, preferred_element_type=F32)

            def spread(t, lane0):
                return jnp.concatenate([jnp.broadcast_to(t[:, lane0 + h:lane0 + h + 1], (tm, hd))
                                        for h in range(nh)], axis=1)

            outs += [spread(beta, 2 * nh + dr * nh), spread(gc, dr * nh), spread(gt, dr * nh)]
        return tuple(outs)
    return f


def _make_f_dnpost(hd):
    def f(o, z, ng):
        y = [_rms(oh, ng) for oh in _heads(o, hd)]
        return (jnp.concatenate(y, axis=1) * _silu(z),)
    return f


def _make_f_attpre(hd, with_v):
    def rope(x, g, cosf, sins, perm, scale):
        xn = _rms(x, g)
        xs = jnp.dot(xn, perm, precision=HI, preferred_element_type=F32)
        return (xn * cosf + xs * sins) * scale

    def f(aq, ak, *rest):
        if with_v:
            av, cosf, sins, perm, qg, kg = rest
        else:
            cosf, sins, perm, qg, kg = rest
        qh = jnp.concatenate([rope(x, qg, cosf, sins, perm, hd ** -0.5) for x in _heads(aq, hd)], axis=1)
        kh = jnp.concatenate([rope(x, kg, cosf, sins, perm, 1.0) for x in _heads(ak, hd)], axis=1)
        return (qh, kh, av) if with_v else (qh, kh)
    return f


def _f_gate(o, z):
    return (o * _silu(z),)


def _make_f_mem(nh, hd):
    def f(mq, z, kv):
        mw = nh * hd
        outs = []
        for h, q in enumerate(_heads(mq, hd)):
            k = kv[:, h * hd:(h + 1) * hd]
            v = kv[:, mw + h * hd:mw + (h + 1) * hd]
            s = lax.dot_general(q.astype(BF16), k.astype(BF16), (((1,), (1,)), ((), ())),
                                preferred_element_type=F32) * (hd ** -0.5)
            s = s - jnp.max(s, axis=-1, keepdims=True)
            p = jnp.exp(s)
            p = p / jnp.sum(p, axis=-1, keepdims=True)
            outs.append(jnp.dot(p.astype(BF16), v.astype(BF16), preferred_element_type=F32))
        return (jnp.concatenate(outs, axis=1) * _silu(z),)
    return f


def _f_merge(p0, p1, p2, p3, g0, g1, g2, g3):
    return (jax.nn.sigmoid(g0) * p0 + jax.nn.sigmoid(g1) * p1 + jax.nn.sigmoid(g2) * p2 + jax.nn.sigmoid(g3) * p3,)


def _make_f_delta(hd):
    def f(do, o):
        out = [jnp.broadcast_to(jnp.sum(a * b, axis=-1, keepdims=True), a.shape)
               for a, b in zip(_heads(do, hd), _heads(o, hd))]
        return (jnp.concatenate(out, axis=1),)
    return f


def _s5_prep(a_re, a_im, log_step, b_re, b_im, c_re, c_im, d):
    nb, gpb, sn, sp = d['NB'], d['GPB'], d['SN'], d['SP']
    step = jnp.exp(log_step)[:, None]
    mag = jnp.exp(a_re * step)
    lam_re = mag * jnp.cos(a_im * step)
    lam_im = mag * jnp.sin(a_im * step)
    den = a_re * a_re + a_im * a_im
    nr, ni = lam_re - 1.0, lam_im
    coef_re = (nr * a_re + ni * a_im) / den
    coef_im = (ni * a_re - nr * a_im) / den
    bb_re = coef_re[..., None] * b_re - coef_im[..., None] * b_im
    bb_im = coef_re[..., None] * b_im + coef_im[..., None] * b_re
    eye = jnp.eye(gpb, dtype=F32)

    def blk_in(bb):
        t = bb.reshape(nb, gpb, sn, sp)
        return jnp.einsum("jgnp,gh->jgphn", t, eye).reshape(nb, gpb * sp, gpb * sn)

    def blk_out(cc):
        t = cc.reshape(nb, gpb, sp, sn)
        return jnp.einsum("jgpn,gh->jgnhp", t, eye).reshape(nb, gpb * sn, gpb * sp)

    wb = jnp.concatenate([blk_in(bb_re), blk_in(bb_im)], axis=2)
    wc = jnp.concatenate([blk_out(c_re), blk_out(-c_im)], axis=1)
    return wb, wc, lam_re.reshape(nb, gpb * sn), lam_im.reshape(nb, gpb * sn)


def _s5_tables(lam_re, lam_im, rev, conj):
    lr, li = lam_re, (-lam_im if conj else lam_im)

    def cmul(a, b):
        return a[0] * b[0] - a[1] * b[1], a[0] * b[1] + a[1] * b[0]

    pw = [(lr, li)]
    for _ in range(7):
        pw.append(cmul(pw[-1], (lr, li)))
    rows = jnp.arange(8)

    def bc(t, k):
        keep = (rows < 8 - k) if rev else (rows >= k)
        return t[:, None, :] * keep.astype(F32)[None, :, None]

    order = list(range(8))[::-1] if rev else list(range(8))
    pwr = jnp.stack([pw[i][0] for i in order], axis=1)
    pwi = jnp.stack([pw[i][1] for i in order], axis=1)
    tabs = [bc(pw[0][0], 1), bc(pw[0][1], 1), bc(pw[1][0], 2), bc(pw[1][1], 2), bc(pw[3][0], 4), bc(pw[3][1], 4),
            pwr, pwi]
    return jnp.stack(tabs, axis=1)


def _scan_group(xr, xi, lt_ref, j, cr, ci, rev):
    for lvl, k in enumerate((1, 2, 4)):
        l_r, l_i = lt_ref[j, 2 * lvl], lt_ref[j, 2 * lvl + 1]
        sh = (8 - k) if rev else k
        sr, si = pltpu.roll(xr, sh, 0), pltpu.roll(xi, sh, 0)
        xr, xi = xr + l_r * sr - l_i * si, xi + l_r * si + l_i * sr
    p_r, p_i = lt_ref[j, 6], lt_ref[j, 7]
    return xr + p_r * cr - p_i * ci, xi + p_r * ci + p_i * cr


def _last_row(x, rev):
    last = 0 if rev else 7
    return jnp.broadcast_to(x[last:last + 1, :], x.shape)


def _s5_fwd(hsrc, ucb, wb, wc, lt, *, rev, acc, tb, name, d):
    L, SW, NB, BS = hsrc.shape[0], d['SW'], d['NB'], d['BS']
    tb = _pick(tb, L)
    nblk, ngr = L // tb, tb // 8
    tix = (lambda b: nblk - 1 - b) if rev else (lambda b: b)
    has_acc = acc is not None

    def body(*refs):
        if has_acc:
            u_ref, wb_ref, wc_ref, lt_ref, acc_ref, y_ref, cin_ref, bu_s, car = refs
        else:
            u_ref, wb_ref, wc_ref, lt_ref, y_ref, cin_ref, bu_s, car = refs

        @pl.when(pl.program_id(0) == 0)
        def _():
            car[...] = jnp.zeros_like(car)

        cin_ref[...] = car[...]
        for j in range(NB):
            bu_s[:, j * 2 * BS:(j + 1) * 2 * BS] = jnp.dot(
                u_ref[:, j * LANE:(j + 1) * LANE].astype(BF16), wb_ref[j], preferred_element_type=F32)

        def grp(r, _):
            base = pl.multiple_of((ngr - 1 - r if rev else r) * 8, 8)
            for j in range(NB):
                c0 = j * 2 * BS
                xr, xi = _scan_group(bu_s[pl.ds(base, 8), c0:c0 + BS], bu_s[pl.ds(base, 8), c0 + BS:c0 + 2 * BS],
                                     lt_ref, j, car[:, c0:c0 + BS], car[:, c0 + BS:c0 + 2 * BS], rev)
                bu_s[pl.ds(base, 8), c0:c0 + BS] = xr
                bu_s[pl.ds(base, 8), c0 + BS:c0 + 2 * BS] = xi
                car[:, c0:c0 + BS] = _last_row(xr, rev)
                car[:, c0 + BS:c0 + 2 * BS] = _last_row(xi, rev)
            return 0

        lax.fori_loop(0, ngr, grp, 0)
        for j in range(NB):
            y = jnp.dot(bu_s[:, j * 2 * BS:(j + 1) * 2 * BS].astype(BF16), wc_ref[j], preferred_element_type=F32)
            if has_acc:
                y = y + acc_ref[:, j * LANE:(j + 1) * LANE]
            y_ref[:, j * LANE:(j + 1) * LANE] = y

    in_specs = [pl.BlockSpec((tb, SW), lambda b: (tix(b), ucb)),
                pl.BlockSpec(wb.shape, lambda b: (0, 0, 0)), pl.BlockSpec(wc.shape, lambda b: (0, 0, 0)),
                pl.BlockSpec(lt.shape, lambda b: (0, 0, 0, 0))]
    args = [hsrc, wb, wc, lt]
    if has_acc:
        in_specs.append(pl.BlockSpec((tb, SW), lambda b: (tix(b), 0)))
        args.append(acc)
    y, cin = pl.pallas_call(
        body, name=name, grid=(nblk,), in_specs=in_specs,
        out_specs=[pl.BlockSpec((tb, SW), lambda b: (tix(b), 0)),
                   pl.BlockSpec((8, NB * 2 * BS), lambda b: (tix(b), 0))],
        out_shape=[jax.ShapeDtypeStruct((L, SW), F32), jax.ShapeDtypeStruct((nblk * 8, NB * 2 * BS), F32)],
        scratch_shapes=[pltpu.VMEM((tb, NB * 2 * BS), F32), pltpu.VMEM((8, NB * 2 * BS), F32)],
        compiler_params=_cparams(("arbitrary",)),
    )(*args)
    return y, cin


def _s5_bwd(hsrc, ucb, dy, cin, wb, wc, lt, lt_adj, *, rev, acc, tb, name, d):
    L, SW, NB, BS = hsrc.shape[0], d['SW'], d['NB'], d['BS']
    tb = _pick(tb, L)
    nblk, ngr = L // tb, tb // 8
    arev = not rev
    tix = (lambda b: nblk - 1 - b) if arev else (lambda b: b)
    has_acc = acc is not None
    NT = (((1,), (1,)), ((), ()))
    TN = (((0,), (0,)), ((), ()))

    def body(*refs):
        if has_acc:
            (u_ref, dy_ref, cin_ref, wb_ref, wc_ref, lt_ref, la_ref, acc_ref,
             du_ref, dwb_ref, dwc_ref, dlam_ref, s_s, g_s, car, acar) = refs
        else:
            (u_ref, dy_ref, cin_ref, wb_ref, wc_ref, lt_ref, la_ref,
             du_ref, dwb_ref, dwc_ref, dlam_ref, s_s, g_s, car, acar) = refs

        @pl.when(pl.program_id(0) == 0)
        def _():
            acar[...] = jnp.zeros_like(acar)
            dwb_ref[...] = jnp.zeros_like(dwb_ref)
            dwc_ref[...] = jnp.zeros_like(dwc_ref)
            dlam_ref[...] = jnp.zeros_like(dlam_ref)

        car[...] = cin_ref[...]
        for j in range(NB):
            s_s[:, j * 2 * BS:(j + 1) * 2 * BS] = jnp.dot(
                u_ref[:, j * LANE:(j + 1) * LANE].astype(BF16), wb_ref[j], preferred_element_type=F32)
            g_s[:, j * 2 * BS:(j + 1) * 2 * BS] = lax.dot_general(
                dy_ref[:, j * LANE:(j + 1) * LANE].astype(BF16), wc_ref[j], NT, preferred_element_type=F32)

        def fgrp(r, _):
            base = pl.multiple_of((ngr - 1 - r if rev else r) * 8, 8)
            for j in range(NB):
                c0 = j * 2 * BS
                xr, xi = _scan_group(s_s[pl.ds(base, 8), c0:c0 + BS], s_s[pl.ds(base, 8), c0 + BS:c0 + 2 * BS],
                                     lt_ref, j, car[:, c0:c0 + BS], car[:, c0 + BS:c0 + 2 * BS], rev)
                s_s[pl.ds(base, 8), c0:c0 + BS] = xr
                s_s[pl.ds(base, 8), c0 + BS:c0 + 2 * BS] = xi
                car[:, c0:c0 + BS] = _last_row(xr, rev)
                car[:, c0 + BS:c0 + 2 * BS] = _last_row(xi, rev)
            return 0

        lax.fori_loop(0, ngr, fgrp, 0)

        row = lax.broadcasted_iota(jnp.int32, (8, BS), 0)

        def agrp(r, _):
            gi = ngr - 1 - r if arev else r
            base = pl.multiple_of(gi * 8, 8)
            pgi = gi + 1 if rev else gi - 1
            inside = jnp.logical_and(pgi >= 0, pgi < ngr)
            pbase = pl.multiple_of(jnp.clip(pgi, 0, ngr - 1) * 8, 8)
            for j in range(NB):
                c0 = j * 2 * BS
                ar, ai = _scan_group(g_s[pl.ds(base, 8), c0:c0 + BS], g_s[pl.ds(base, 8), c0 + BS:c0 + 2 * BS],
                                     la_ref, j, acar[:, c0:c0 + BS], acar[:, c0 + BS:c0 + 2 * BS], arev)
                g_s[pl.ds(base, 8), c0:c0 + BS] = ar
                g_s[pl.ds(base, 8), c0 + BS:c0 + 2 * BS] = ai
                acar[:, c0:c0 + BS] = _last_row(ar, arev)
                acar[:, c0 + BS:c0 + 2 * BS] = _last_row(ai, arev)
                sr, si = s_s[pl.ds(base, 8), c0:c0 + BS], s_s[pl.ds(base, 8), c0 + BS:c0 + 2 * BS]
                edge_r = jnp.where(inside, _last_row(s_s[pl.ds(pbase, 8), c0:c0 + BS], rev), cin_ref[:, c0:c0 + BS])
                edge_i = jnp.where(inside, _last_row(s_s[pl.ds(pbase, 8), c0 + BS:c0 + 2 * BS], rev),
                                   cin_ref[:, c0 + BS:c0 + 2 * BS])
                sh = 7 if rev else 1
                first = 7 if rev else 0
                pr = jnp.where(row == first, edge_r, pltpu.roll(sr, sh, 0))
                pi = jnp.where(row == first, edge_i, pltpu.roll(si, sh, 0))
                dlam_ref[:, c0:c0 + BS] += ar * pr + ai * pi
                dlam_ref[:, c0 + BS:c0 + 2 * BS] += ai * pr - ar * pi
            return 0

        lax.fori_loop(0, ngr, agrp, 0)
        for j in range(NB):
            a_j = g_s[:, j * 2 * BS:(j + 1) * 2 * BS].astype(BF16)
            u_j = u_ref[:, j * LANE:(j + 1) * LANE].astype(BF16)
            du = lax.dot_general(a_j, wb_ref[j], NT, preferred_element_type=F32)
            if has_acc:
                du = du + acc_ref[:, j * LANE:(j + 1) * LANE]
            du_ref[:, j * LANE:(j + 1) * LANE] = du
            dwb_ref[j] += lax.dot_general(u_j, a_j, TN, preferred_element_type=F32)
            dwc_ref[j] += lax.dot_general(s_s[:, j * 2 * BS:(j + 1) * 2 * BS].astype(BF16),
                                          dy_ref[:, j * LANE:(j + 1) * LANE].astype(BF16), TN,
                                          preferred_element_type=F32)

    W2 = NB * 2 * BS
    in_specs = [pl.BlockSpec((tb, SW), lambda b: (tix(b), ucb)), pl.BlockSpec((tb, SW), lambda b: (tix(b), 0)),
                pl.BlockSpec((8, W2), lambda b: (tix(b), 0)),
                pl.BlockSpec(wb.shape, lambda b: (0, 0, 0)), pl.BlockSpec(wc.shape, lambda b: (0, 0, 0)),
                pl.BlockSpec(lt.shape, lambda b: (0, 0, 0, 0)), pl.BlockSpec(lt_adj.shape, lambda b: (0, 0, 0, 0))]
    args = [hsrc, dy, cin, wb, wc, lt, lt_adj]
    if has_acc:
        in_specs.append(pl.BlockSpec((tb, SW), lambda b: (tix(b), 0)))
        args.append(acc)
    return pl.pallas_call(
        body, name=name, grid=(nblk,), in_specs=in_specs,
        out_specs=[pl.BlockSpec((tb, SW), lambda b: (tix(b), 0)),
                   pl.BlockSpec(wb.shape, lambda b: (0, 0, 0)), pl.BlockSpec(wc.shape, lambda b: (0, 0, 0)),
                   pl.BlockSpec((8, W2), lambda b: (0, 0))],
        out_shape=[jax.ShapeDtypeStruct((L, SW), F32), jax.ShapeDtypeStruct(wb.shape, F32),
                   jax.ShapeDtypeStruct(wc.shape, F32), jax.ShapeDtypeStruct((8, W2), F32)],
        scratch_shapes=[pltpu.VMEM((tb, W2), F32), pltpu.VMEM((tb, W2), F32),
                        pltpu.VMEM((8, W2), F32), pltpu.VMEM((8, W2), F32)],
        compiler_params=_cparams(("arbitrary",)),
    )(*args)


_NN = (((1,), (0,)), ((), ()))
_NT = (((1,), (1,)), ((), ()))
_TN = (((0,), (0,)), ((), ()))


def _dotb(a, b, dn=_NN):
    return lax.dot_general(a.astype(BF16), b.astype(BF16), dn, preferred_element_type=F32)


def _split(x):
    hi = x.astype(BF16)
    return hi, (x - hi.astype(F32)).astype(BF16)


def _dot3(a, b, dn=_NN):
    ah, al = _split(a)
    bh, bl = _split(b)
    f = lambda x, y: lax.dot_general(x, y, dn, preferred_element_type=F32)
    return f(ah, bh) + (f(ah, bl) + f(al, bh))


@jax.custom_vjp
def _dot3_nn(a, b):
    return _dot3(a, b, _NN)


_dot3_nn.defvjp(lambda a, b: (_dot3(a, b, _NN), (a, b)),
                lambda res, g: (_dotb(g, res[1], _NT), _dotb(res[0], g, _TN)))


@jax.custom_vjp
def _dot3_nt(a, b):
    return _dot3(a, b, _NT)


_dot3_nt.defvjp(lambda a, b: (_dot3(a, b, _NT), (a, b)),
                lambda res, g: (_dotb(g, res[1], _NN), _dotb(g, res[0], _TN)))


def _delta_chunk(rev, one_pass_grads, *flat):
    heads = [flat[i:i + 7] for i in range(0, len(flat), 7)]
    q, k, v, beta, gc, gt, s_in = [list(t) for t in zip(*heads)]
    c, hd = q[0].shape
    each = lambda f, *ls: [f(*t) for t in zip(*ls)]
    mm_nn = _dot3_nn if one_pass_grads else _dot3
    mm_nt = _dot3_nt if one_pass_grads else (lambda x, y: _dot3(x, y, _NT))
    ii = lax.broadcasted_iota(jnp.int32, (c, c), 0)
    jj = lax.broadcasted_iota(jnp.int32, (c, c), 1)
    incl = (ii <= jj) if rev else (ii >= jj)
    strict = (ii < jj) if rev else (ii > jj)
    eye = (ii == jj).astype(F32)
    decay = each(lambda g: jnp.where(incl, jnp.exp(jnp.where(incl, g[:, :c] - jnp.transpose(g)[:c, :], 0.0)), 0.0), gc)
    kb = each(lambda a, b: a * b, k, beta)
    a = each(lambda x, y, dc: jnp.where(strict, mm_nt(x, y) * dc, 0.0), kb, k, decay)
    tinv = each(lambda x: eye - x, a)
    p = a
    n = 2
    while n < c:
        p = each(lambda x: mm_nn(x, x), p)
        tinv = each(lambda t, x: mm_nn(t, eye + x), tinv, p)
        n *= 2
    eg = each(jnp.exp, gc)
    u = each(lambda t, x, b: mm_nn(t, x * b), tinv, v, beta)
    w = each(lambda t, x, e: mm_nn(t, x * e), tinv, kb, eg)
    intra = each(lambda x, y, dc: _dotb(x, y, _NT) * dc, q, k, decay)
    v_new = each(lambda x, y, s: x - _dotb(y, s), u, w, s_in)
    o = each(lambda x, e, s, m, vn: _dotb(x * e, s) + _dotb(m, vn), q, eg, s_in, intra, v_new)
    s_out = each(lambda s, t, x, g, vn: s * jnp.exp(jnp.broadcast_to(t[0:1, :], (hd, hd)))
                 + _dotb(x * jnp.exp(t - g), vn, _TN), s_in, gt, k, gc, v_new)
    return tuple(x for pair in zip(o, s_out) for x in pair)


def _delta_fwd(q, k, v, gates, *, vcb, rev, acc, name, d):
    L, H, hd, C = q.shape[0], d['DNH'], d['DNK'], d['CHUNK']
    nc = L // C
    cix = (lambda i: nc - 1 - i) if rev else (lambda i: i)
    has_acc = acc is not None

    def body(*refs):
        if has_acc:
            q_ref, k_ref, v_ref, b_ref, gc_ref, gt_ref, acc_ref, o_ref, ss_ref, st = refs
        else:
            q_ref, k_ref, v_ref, b_ref, gc_ref, gt_ref, o_ref, ss_ref, st = refs

        @pl.when(pl.program_id(0) == 0)
        def _():
            st[...] = jnp.zeros_like(st)

        sls = [slice(h * hd, (h + 1) * hd) for h in range(H)]
        ins = [(q_ref[:, sl], k_ref[:, sl], v_ref[:, sl], b_ref[:, sl], gc_ref[:, sl], gt_ref[:, sl], st[h])
               for h, sl in enumerate(sls)]
        accv = [acc_ref[:, sl] for sl in sls] if has_acc else None
        res = _delta_chunk(rev, False, *[t for head in ins for t in head])
        for h, sl in enumerate(sls):
            o, s_out = res[2 * h], res[2 * h + 1]
            ss_ref[0, h] = ins[h][6]
            o_ref[:, sl] = o + accv[h] if has_acc else o
            st[h] = s_out

    blk = pl.BlockSpec((C, H * hd), lambda i: (cix(i), 0))
    in_specs = [blk, blk, pl.BlockSpec((C, H * hd), lambda i: (cix(i), vcb)), blk, blk, blk]
    args = [q, k, v, *gates]
    if has_acc:
        in_specs.append(blk)
        args.append(acc)
    return pl.pallas_call(
        body, name=name, grid=(nc,), in_specs=in_specs,
        out_specs=[blk, pl.BlockSpec((1, H, hd, hd), lambda i: (cix(i), 0, 0, 0))],
        out_shape=[jax.ShapeDtypeStruct((L, H * hd), F32), jax.ShapeDtypeStruct((nc, H, hd, hd), F32)],
        scratch_shapes=[pltpu.VMEM((H, hd, hd), F32)],
        compiler_params=_cparams(("arbitrary",)),
    )(*args)


def _delta_bwd(q, k, v, gates, ssave, do, *, vcb, rev, accs, name, d):
    L, H, hd, C = q.shape[0], d['DNH'], d['DNK'], d['CHUNK']
    nc = L // C
    cix = (lambda i: i) if rev else (lambda i: nc - 1 - i)
    has_acc = accs is not None

    def body(*refs):
        if has_acc:
            (q_ref, k_ref, v_ref, b_ref, gc_ref, gt_ref, ss_ref, do_ref, aq_ref, ak_ref, av_ref,
             dq_ref, dk_ref, dv_ref, db_ref, dgc_ref, dgt_ref, dst) = refs
        else:
            (q_ref, k_ref, v_ref, b_ref, gc_ref, gt_ref, ss_ref, do_ref,
             dq_ref, dk_ref, dv_ref, db_ref, dgc_ref, dgt_ref, dst) = refs

        @pl.when(pl.program_id(0) == 0)
        def _():
            dst[...] = jnp.zeros_like(dst)

        sls = [slice(h * hd, (h + 1) * hd) for h in range(H)]
        ins = [(q_ref[:, sl], k_ref[:, sl], v_ref[:, sl], b_ref[:, sl], gc_ref[:, sl], gt_ref[:, sl], ss_ref[0, h])
               for h, sl in enumerate(sls)]
        cts = tuple(t for h, sl in enumerate(sls) for t in (do_ref[:, sl], dst[h]))
        accv = [(aq_ref[:, sl], ak_ref[:, sl], av_ref[:, sl]) for sl in sls] if has_acc else None
        _, vjp = jax.vjp(functools.partial(_delta_chunk, rev, True), *[t for head in ins for t in head])
        res = vjp(cts)
        for h, sl in enumerate(sls):
            dq, dk, dv, db, dgc, dgt, ds = res[7 * h:7 * h + 7]
            dst[h] = ds
            if has_acc:
                dq, dk, dv = dq + accv[h][0], dk + accv[h][1], dv + accv[h][2]
            dq_ref[:, sl] = dq
            dk_ref[:, sl] = dk
            dv_ref[:, sl] = dv
            db_ref[:, sl] = db
            dgc_ref[:, sl] = dgc
            dgt_ref[:, sl] = dgt

    blk = pl.BlockSpec((C, H * hd), lambda i: (cix(i), 0))
    in_specs = [blk, blk, pl.BlockSpec((C, H * hd), lambda i: (cix(i), vcb)), blk, blk, blk,
                pl.BlockSpec((1, H, hd, hd), lambda i: (cix(i), 0, 0, 0)), blk]
    args = [q, k, v, *gates, ssave, do]
    if has_acc:
        in_specs += [blk, blk, blk]
        args += list(accs)
    return pl.pallas_call(
        body, name=name, grid=(nc,), in_specs=in_specs, out_specs=[blk] * 6,
        out_shape=[jax.ShapeDtypeStruct((L, H * hd), F32)] * 6,
        scratch_shapes=[pltpu.VMEM((H, hd, hd), F32)],
        compiler_params=_cparams(("arbitrary",)),
    )(*args)


def _conv_specs(tm, w, cb0, nrb, L):
    hb = tm // 8
    last8 = L // 8 - 1
    cur = pl.BlockSpec((tm, w), lambda s, i: (i, cb0 + s))
    prev = pl.BlockSpec((8, w), lambda s, i: (jnp.maximum(i * hb - 1, 0), cb0 + s))
    nxt = pl.BlockSpec((8, w), lambda s, i: (jnp.minimum((i + 1) * hb, last8), cb0 + s))
    return [prev, cur, nxt]


def _fill_halo(dst, prev_ref, cur_ref, next_ref, i, nrb, tm):
    dst[pl.ds(0, 8), :] = jnp.where(i > 0, prev_ref[...], 0.0)
    dst[pl.ds(8, tm), :] = cur_ref[...]
    dst[pl.ds(8 + tm, 8), :] = jnp.where(i < nrb - 1, next_ref[...], 0.0)


def _conv_fwd(hsrc, cb0, wt, *, tm, name, d):
    L, w, K = hsrc.shape[0], d['DW'], d['CONV']
    tm = _pick(tm, L)
    nrb = L // tm

    def body(prev_ref, cur_ref, next_ref, w_ref, o_ref, xs):
        i = pl.program_id(1)
        _fill_halo(xs, prev_ref, cur_ref, next_ref, i, nrb, tm)
        y = jnp.zeros((tm, w), F32)
        for kk in range(K):
            y = y + w_ref[0, pl.ds(kk, 1), :] * xs[pl.ds(8 - K // 2 + kk, tm), :]
        o_ref[...] = _silu(y)

    return pl.pallas_call(
        body, name=name, grid=(3, nrb),
        in_specs=_conv_specs(tm, w, cb0, nrb, L) + [pl.BlockSpec((1, 8, w), lambda s, i: (s, 0, 0))],
        out_specs=pl.BlockSpec((tm, w), lambda s, i: (i, s)),
        out_shape=jax.ShapeDtypeStruct((L, 3 * w), F32),
        scratch_shapes=[pltpu.VMEM((tm + 16, w), F32)],
        compiler_params=_cparams(("parallel", "parallel")),
    )(hsrc, hsrc, hsrc, wt)


def _conv_bwd(hsrc, cb0, wt, dact, *, tm, name, d):
    L, w, K = hsrc.shape[0], d['DW'], d['CONV']
    tm = _pick(tm, L)
    nrb = L // tm
    half = K // 2

    def body(xp_ref, xc_ref, xn_ref, gp_ref, gc_ref, gn_ref, w_ref, dx_ref, dw_ref, xs, gs, dys):
        i = pl.program_id(1)
        _fill_halo(xs, xp_ref, xc_ref, xn_ref, i, nrb, tm)
        _fill_halo(gs, gp_ref, gc_ref, gn_ref, i, nrb, tm)
        y = jnp.zeros((tm + 8, w), F32)
        for kk in range(K):
            y = y + w_ref[0, pl.ds(kk, 1), :] * xs[pl.ds(4 - half + kk, tm + 8), :]
        sg = jax.nn.sigmoid(y)
        dys[...] = gs[pl.ds(4, tm + 8), :] * (sg * (1.0 + y * (1.0 - sg)))
        dx = jnp.zeros((tm, w), F32)
        for kk in range(K):
            dx = dx + w_ref[0, pl.ds(kk, 1), :] * dys[pl.ds(4 + half - kk, tm), :]
        dx_ref[...] = dx

        @pl.when(i == 0)
        def _():
            dw_ref[...] = jnp.zeros_like(dw_ref)

        dy = dys[pl.ds(4, tm), :]
        for kk in range(K):
            dw_ref[0, pl.ds(kk, 1), :] += jnp.sum(dy * xs[pl.ds(8 - half + kk, tm), :], axis=0, keepdims=True)

    gspecs = _conv_specs(tm, w, 0, nrb, L)
    return pl.pallas_call(
        body, name=name, grid=(3, nrb),
        in_specs=_conv_specs(tm, w, cb0, nrb, L) + gspecs + [pl.BlockSpec((1, 8, w), lambda s, i: (s, 0, 0))],
        out_specs=[pl.BlockSpec((tm, w), lambda s, i: (i, s)), pl.BlockSpec((1, 8, w), lambda s, i: (s, 0, 0))],
        out_shape=[jax.ShapeDtypeStruct((L, 3 * w), F32), jax.ShapeDtypeStruct((3, 8, w), F32)],
        scratch_shapes=[pltpu.VMEM((tm + 16, w), F32), pltpu.VMEM((tm + 16, w), F32), pltpu.VMEM((tm + 8, w), F32)],
        compiler_params=_cparams(("parallel", "arbitrary")),
    )(hsrc, hsrc, hsrc, dact, dact, dact, wt)


def _wide(v, n):
    return v if n == LANE else jnp.tile(v, (1, n // LANE))


def _attn_fwd(qh, kh, vh, *, tq, tk, name, d):
    L, H, KVH, hd = qh.shape[0], d['AH'], d['AKV'], d['AD']
    grp = H // KVH
    tq, tk = _pick(tq, L), _pick(tk, L)
    nk = L // tk

    def body(q_ref, k_ref, v_ref, o_ref, lse_ref, m_s, l_s, acc):
        j = pl.program_id(2)

        @pl.when(j == 0)
        def _():
            m_s[...] = jnp.full_like(m_s, -1e30)
            l_s[...] = jnp.zeros_like(l_s)
            acc[...] = jnp.zeros_like(acc)

        s = lax.dot_general(q_ref[...], k_ref[...], _NT, preferred_element_type=F32)
        m_old = m_s[...]
        m_new = jnp.maximum(m_old, jnp.max(s, axis=-1, keepdims=True))
        alpha = jnp.exp(m_old - m_new)
        p = jnp.exp(s - _wide(m_new, tk))
        l_s[...] = alpha * l_s[...] + jnp.sum(p, axis=-1, keepdims=True)
        acc[...] = alpha * acc[...] + jnp.dot(p.astype(BF16), v_ref[...], preferred_element_type=F32)
        m_s[...] = m_new

        @pl.when(j == nk - 1)
        def _():
            o_ref[...] = acc[...] / l_s[...]
            lse_ref[...] = m_s[...] + jnp.log(l_s[...])

    qspec = pl.BlockSpec((tq, hd), lambda h, i, j: (i, h))
    kspec = pl.BlockSpec((tk, hd), lambda h, i, j: (j, h // grp))
    return pl.pallas_call(
        body, name=name, grid=(H, L // tq, nk), in_specs=[qspec, kspec, kspec], out_specs=[qspec, qspec],
        out_shape=[jax.ShapeDtypeStruct((L, H * hd), F32), jax.ShapeDtypeStruct((L, H * hd), F32)],
        scratch_shapes=[pltpu.VMEM((tq, hd), F32), pltpu.VMEM((tq, hd), F32), pltpu.VMEM((tq, hd), F32)],
        compiler_params=_cparams(("parallel", "parallel", "arbitrary")),
    )(qh, kh, vh)


def _attn_bwd(qh, kh, vh, do, lse, delta, *, tq, tk, name, d):
    L, H, KVH, hd = qh.shape[0], d['AH'], d['AKV'], d['AD']
    grp = H // KVH
    tq, tk = _pick(tq, L), _pick(tk, L)
    nk = L // tk

    def body(q_ref, k_ref, v_ref, do_ref, lse_ref, dl_ref, dq_ref, dk_ref, dv_ref, dq_s):
        g, i, j = pl.program_id(1), pl.program_id(2), pl.program_id(3)

        @pl.when(jnp.logical_and(jnp.logical_and(g == 0, i == 0), j == 0))
        def _():
            dk_ref[...] = jnp.zeros_like(dk_ref)
            dv_ref[...] = jnp.zeros_like(dv_ref)

        @pl.when(j == 0)
        def _():
            dq_s[...] = jnp.zeros_like(dq_s)

        q, k, do_ = q_ref[...], k_ref[...], do_ref[...].astype(BF16)
        s = lax.dot_general(q, k, _NT, preferred_element_type=F32)
        p = jnp.exp(s - _wide(lse_ref[...], tk))
        dp = lax.dot_general(do_, v_ref[...], _NT, preferred_element_type=F32)
        ds = (p * (dp - _wide(dl_ref[...], tk))).astype(BF16)
        dq_s[...] += jnp.dot(ds, k, preferred_element_type=F32)
        rows = pl.ds(pl.multiple_of(j * tk, tk), tk)
        dv_ref[rows, :] += lax.dot_general(p.astype(BF16), do_, _TN, preferred_element_type=F32)
        dk_ref[rows, :] += lax.dot_general(ds, q, _TN, preferred_element_type=F32)

        @pl.when(j == nk - 1)
        def _():
            dq_ref[...] = dq_s[...]

    qspec = pl.BlockSpec((tq, hd), lambda kv, g, i, j: (i, kv * grp + g))
    kspec = pl.BlockSpec((tk, hd), lambda kv, g, i, j: (j, kv))
    colspec = pl.BlockSpec((L, hd), lambda kv, g, i, j: (0, kv))
    return pl.pallas_call(
        body, name=name, grid=(KVH, grp, L // tq, nk),
        in_specs=[qspec, kspec, kspec, qspec, qspec, qspec], out_specs=[qspec, colspec, colspec],
        out_shape=[jax.ShapeDtypeStruct((L, H * hd), F32)] + [jax.ShapeDtypeStruct((L, KVH * hd), F32)] * 2,
        scratch_shapes=[pltpu.VMEM((tq, hd), F32)],
        compiler_params=_cparams(("parallel", "arbitrary", "arbitrary", "arbitrary")),
    )(qh, kh, vh, do, lse, delta)


def _loss_grad(x, g, tgt, *, tm, name):
    L, D = x.shape
    tm = _pick(tm, L)

    def body(x_ref, g_ref, t_ref, loss_ref, dx_ref, dg_ref):
        def f(xv, gv):
            err = _rms(xv, gv) - t_ref[...]
            return 0.5 * jnp.sum(jnp.mean(err * err, axis=-1, keepdims=True))

        val, vjp = jax.vjp(f, x_ref[...], g_ref[...])
        dx, dg = vjp(jnp.ones((), F32))
        dx_ref[...] = dx

        @pl.when(pl.program_id(0) == 0)
        def _():
            loss_ref[...] = jnp.zeros_like(loss_ref)
            dg_ref[...] = jnp.zeros_like(dg_ref)

        loss_ref[...] += val
        dg_ref[...] += dg

    return pl.pallas_call(
        body, name=name, grid=(L // tm,),
        in_specs=[pl.BlockSpec((tm, D), lambda i: (i, 0)), pl.BlockSpec((1, D), lambda i: (0, 0)),
                  pl.BlockSpec((tm, D), lambda i: (i, 0))],
        out_specs=[pl.BlockSpec((8, LANE), lambda i: (0, 0)), pl.BlockSpec((tm, D), lambda i: (i, 0)),
                   pl.BlockSpec((1, D), lambda i: (0, 0))],
        out_shape=[jax.ShapeDtypeStruct((8, LANE), F32), jax.ShapeDtypeStruct((L, D), F32),
                   jax.ShapeDtypeStruct((1, D), F32)],
        compiler_params=_cparams(("arbitrary",)),
    )(x, g, tgt)


def _sum_slots(recv, *, tr, name):
    n, R, W = recv.shape
    tr = _pick(tr, R)

    def body(r_ref, o_ref):
        s = r_ref[0].astype(F32)
        for i in range(1, n):
            s = s + r_ref[i].astype(F32)
        o_ref[...] = s

    return pl.pallas_call(
        body, name=name, grid=(R // tr,),
        in_specs=[pl.BlockSpec((n, tr, W), lambda i: (0, i, 0))], out_specs=pl.BlockSpec((tr, W), lambda i: (i, 0)),
        out_shape=jax.ShapeDtypeStruct((R, W), F32), compiler_params=_cparams(("parallel",)),
    )(recv)


def _adamw(w, g, m, v, *, tr, name):
    R, W = w.shape
    tr = _pick(tr, R)
    c1 = 1.0 - ADAM_B1 ** ADAM_STEP
    c2 = 1.0 - ADAM_B2 ** ADAM_STEP

    def body(w_ref, g_ref, m_ref, v_ref, d_ref, nm_ref, nv_ref):
        gv = g_ref[...]
        nm = ADAM_B1 * m_ref[...] + (1.0 - ADAM_B1) * gv
        nv = ADAM_B2 * v_ref[...] + (1.0 - ADAM_B2) * (gv * gv)
        d_ref[...] = -ADAM_LR * ((nm / c1) / (jnp.sqrt(nv / c2) + ADAM_EPS) + ADAM_WD * w_ref[...])
        nm_ref[...] = nm
        nv_ref[...] = nv

    spec = pl.BlockSpec((tr, W), lambda i: (i, 0))
    return pl.pallas_call(
        body, name=name, grid=(R // tr,), in_specs=[spec] * 4, out_specs=[spec] * 3,
        out_shape=[jax.ShapeDtypeStruct((R, W), F32)] * 3, compiler_params=_cparams(("parallel",)),
    )(w, g, m, v)


_MESH = pl.DeviceIdType.MESH


def _all_gather(xs, *, name):
    na = len(xs)

    def body(*refs):
        x_refs, out_refs = refs[:na], refs[na:2 * na]
        send_sems, recv_sems, local_sems = refs[2 * na:]
        x, y, c = lax.axis_index("x"), lax.axis_index("y"), lax.axis_index("c")
        me, sibling = (x, y, c), (x, y, 1 - c)
        chips = [(1 - x, y), (x, 1 - y), (1 - x, 1 - y)]

        def slot(a, px, py, pc):
            return out_refs[a].at[4 * px + 2 * py + pc]

        def copy(a, k, block, to, src=None):
            return pltpu.make_async_remote_copy(
                src_ref=slot(a, *block) if src is None else src, dst_ref=slot(a, *block),
                send_sem=send_sems.at[7 * a + k], recv_sem=recv_sems.at[7 * a + k], device_id=to,
                device_id_type=_MESH)

        mine = [pltpu.make_async_copy(x_refs[a], slot(a, *me), local_sems.at[a]) for a in range(na)]
        for cp in mine:
            cp.start()
        first = []
        for a in range(na):
            first.append(copy(a, 0, me, sibling, src=x_refs[a]))
            first += [copy(a, 1 + j, me, (*chip, c), src=x_refs[a]) for j, chip in enumerate(chips)]
        for cp in first:
            cp.start()
        passed = []
        for a in range(na):
            for j, chip in enumerate(chips):
                copy(a, 1 + j, (*chip, c), me).wait_recv()
                passed.append(copy(a, 4 + j, (*chip, c), sibling))
                passed[-1].start()
        for a in range(na):
            copy(a, 0, sibling, me).wait_recv()
            for j, chip in enumerate(chips):
                copy(a, 4 + j, (*chip, 1 - c), me).wait_recv()
        for cp in first + passed:
            cp.wait_send()
        for cp in mine:
            cp.wait()

    return pl.pallas_call(
        body, name=name,
        out_shape=[jax.ShapeDtypeStruct((N_DEV,) + t.shape, t.dtype) for t in xs],
        in_specs=[pl.BlockSpec(memory_space=pl.ANY)] * na, out_specs=[pl.BlockSpec(memory_space=pl.ANY)] * na,
        scratch_shapes=[pltpu.SemaphoreType.DMA((7 * na,)), pltpu.SemaphoreType.DMA((7 * na,)),
                        pltpu.SemaphoreType.DMA((na,))],
    )(*xs)


N_CHIP = N_DEV // 2


def _pair_exchange(gs, *, name):
    na = len(gs)

    def body(*refs):
        g_refs, out_refs = refs[:na], refs[na:2 * na]
        send_sems, recv_sems = refs[2 * na:]
        x, y, c = lax.axis_index("x"), lax.axis_index("y"), lax.axis_index("c")

        def copy(a, chip, core):
            return pltpu.make_async_remote_copy(
                src_ref=g_refs[a].at[2 * chip + core], dst_ref=out_refs[a].at[chip],
                send_sem=send_sems.at[N_CHIP * a + chip], recv_sem=recv_sems.at[N_CHIP * a + chip],
                device_id=(x, y, 1 - c), device_id_type=_MESH)

        sends = [copy(a, chip, 1 - c) for a in range(na) for chip in range(N_CHIP)]
        for cp in sends:
            cp.start()
        for cp in sends:
            cp.wait_recv()
        for cp in sends:
            cp.wait_send()

    return pl.pallas_call(
        body, name=name,
        out_shape=[jax.ShapeDtypeStruct((N_CHIP,) + t.shape[1:], t.dtype) for t in gs],
        in_specs=[pl.BlockSpec(memory_space=pl.ANY)] * na, out_specs=[pl.BlockSpec(memory_space=pl.ANY)] * na,
        scratch_shapes=[pltpu.SemaphoreType.DMA((N_CHIP * na,)), pltpu.SemaphoreType.DMA((N_CHIP * na,))],
    )(*gs)


def _pair_sum(g, r, core, *, out_dtype, tr, name):
    _, R, W = g.shape
    tr = _pick(tr, R)

    def body(core_ref, g_ref, r_ref, o_ref):
        o_ref[...] = (g_ref[...] + r_ref[...]).astype(o_ref.dtype)

    return pl.pallas_call(
        body, name=name,
        grid_spec=pltpu.PrefetchScalarGridSpec(
            num_scalar_prefetch=1, grid=(N_CHIP, R // tr),
            in_specs=[pl.BlockSpec((1, tr, W), lambda ch, i, core_ref: (2 * ch + core_ref[0], i, 0)),
                      pl.BlockSpec((1, tr, W), lambda ch, i, core_ref: (ch, i, 0))],
            out_specs=pl.BlockSpec((1, tr, W), lambda ch, i, core_ref: (ch, i, 0))),
        out_shape=jax.ShapeDtypeStruct((N_CHIP, R, W), out_dtype),
        compiler_params=_cparams(("parallel", "parallel")),
    )(core, g, r)


def _chip_exchange(hs, *, name):
    na = len(hs)

    def body(*refs):
        h_refs, out_refs = refs[:na], refs[na:2 * na]
        send_sems, recv_sems, local_sems = refs[2 * na:]
        x, y, c = lax.axis_index("x"), lax.axis_index("y"), lax.axis_index("c")
        me = 2 * x + y

        def copy(a, mask, started):
            px, py = x ^ (mask >> 1), y ^ (mask & 1)
            mine_, theirs = me, 2 * px + py
            return pltpu.make_async_remote_copy(
                src_ref=h_refs[a].at[theirs if started else mine_],
                dst_ref=out_refs[a].at[mine_ if started else theirs],
                send_sem=send_sems.at[3 * a + mask - 1], recv_sem=recv_sems.at[3 * a + mask - 1],
                device_id=(px, py, c), device_id_type=_MESH)

        mine = [pltpu.make_async_copy(h_refs[a].at[me], out_refs[a].at[me], local_sems.at[a]) for a in range(na)]
        for cp in mine:
            cp.start()
        sends = [copy(a, mask, True) for a in range(na) for mask in range(1, N_CHIP)]
        for cp in sends:
            cp.start()
        for a in range(na):
            for mask in range(1, N_CHIP):
                copy(a, mask, False).wait_recv()
        for cp in sends:
            cp.wait_send()
        for cp in mine:
            cp.wait()

    return pl.pallas_call(
        body, name=name,
        out_shape=[jax.ShapeDtypeStruct(t.shape, t.dtype) for t in hs],
        in_specs=[pl.BlockSpec(memory_space=pl.ANY)] * na, out_specs=[pl.BlockSpec(memory_space=pl.ANY)] * na,
        scratch_shapes=[pltpu.SemaphoreType.DMA((3 * na,)), pltpu.SemaphoreType.DMA((3 * na,)),
                        pltpu.SemaphoreType.DMA((na,))],
    )(*hs)


def _rows_of(shape):
    return -(-int(np.prod(shape)) // PACK_W)


def _pack(arrs, dtype, lead=0, total_rows=None):
    pieces = []
    for a in arrs:
        f = a.reshape(a.shape[:lead] + (-1,)).astype(dtype)
        pad = (-f.shape[-1]) % PACK_W
        if pad:
            f = jnp.pad(f, [(0, 0)] * lead + [(0, pad)])
        pieces.append(f.reshape(a.shape[:lead] + (-1, PACK_W)))
    buf = jnp.concatenate(pieces, axis=lead)
    if total_rows is not None and buf.shape[lead] < total_rows:
        buf = jnp.pad(buf, [(0, 0)] * lead + [(0, total_rows - buf.shape[lead]), (0, 0)])
    return buf


def _unpack(buf, shapes, lead=0):
    out, r = [], 0
    for shp in shapes:
        n, rows = int(np.prod(shp)), _rows_of(shp)
        piece = buf[(slice(None),) * lead + (slice(r, r + rows),)]
        piece = piece.reshape(buf.shape[:lead] + (-1,))[..., :n]
        out.append(piece.reshape(buf.shape[:lead] + tuple(shp)))
        r += rows
    return out


def _pack_flat(arrs, total_rows):
    flat = jnp.concatenate([t.reshape(-1).astype(F32) for t in arrs])
    return jnp.pad(flat, (0, total_rows * PACK_W - flat.shape[0])).reshape(total_rows, PACK_W)


def _unpack_flat(buf, shapes):
    flat, out, off = buf.reshape(-1), [], 0
    for shp in shapes:
        n = int(np.prod(shp))
        out.append(flat[off:off + n].reshape(shp))
        off += n
    return out


def _to_full(parts):
    dep, r = parts.shape[1:3]
    return jnp.transpose(parts, (1, 0) + tuple(range(2, parts.ndim))).reshape((dep, N_DEV * r) + parts.shape[3:])


def _to_slabs(full):
    dep, r = full.shape[:2]
    t = full.reshape((dep, N_DEV, r // N_DEV) + full.shape[2:])
    return jnp.transpose(t, (1, 0) + tuple(range(2, t.ndim)))


def _ref_cols(parts, ro, wd):
    w, out = parts.shape[2], []
    for dev in range(N_DEV):
        lo, hi = max(ro, dev * w), min(ro + wd, (dev + 1) * w)
        if lo < hi:
            out.append(parts[dev][:, lo - dev * w:hi - dev * w])
    return out


def _w_in_to_layout(parts, seg, rseg, nh2):
    D = parts.shape[1]
    cols, off = [], 0
    names = sorted([k for k in seg if not k.startswith('_')], key=lambda k: seg[k][0])
    for nm in names:
        o, wd = seg[nm]
        if o > off:
            cols.append(jnp.zeros((D, o - off), parts.dtype))
        if nm == 'dadb':
            cols += _ref_cols(parts, rseg['da'][0], nh2) + _ref_cols(parts, rseg['db'][0], nh2)
            cols.append(jnp.zeros((D, wd - 2 * nh2), parts.dtype))
        else:
            cols += _ref_cols(parts, rseg[nm][0], wd)
        off = o + wd
    if seg['_total'] > off:
        cols.append(jnp.zeros((D, seg['_total'] - off), parts.dtype))
    return jnp.concatenate(cols, axis=1)


def _w_in_slabs(dw, seg, rseg, nh2):
    w = rseg['_total'] // N_DEV
    ref = []
    for nm in sorted([k for k in rseg if not k.startswith('_')], key=lambda k: rseg[k][0]):
        lo = {'da': seg['dadb'][0], 'db': seg['dadb'][0] + nh2}.get(nm)
        ref.append((rseg[nm][0], rseg[nm][1], seg[nm][0] if lo is None else lo))
    slabs = []
    for dev in range(N_DEV):
        cols = []
        for ro, wd, lo in ref:
            a, b = max(ro, dev * w), min(ro + wd, (dev + 1) * w)
            if a < b:
                cols.append(dw[:, lo + a - ro:lo + b - ro])
        slabs.append(jnp.concatenate(cols, axis=1))
    return jnp.stack(slabs, axis=0)


def _assemble_dh(pieces, seg, L):
    cols, off = [], 0
    for nm in sorted(pieces, key=lambda k: seg[k][0]):
        o = seg[nm][0]
        if o > off:
            cols.append(jnp.zeros((L, o - off), F32))
        cols.append(pieces[nm])
        off = o + pieces[nm].shape[1]
    if seg['_total'] > off:
        cols.append(jnp.zeros((L, seg['_total'] - off), F32))
    return jnp.concatenate(cols, axis=1)


def _lane_pad(v):
    v = v.reshape(1, -1)
    return jnp.pad(v, ((0, 0), (0, LANE - v.shape[1])))


def _rope_tables(L, c):
    rows = L // c['GRID_W']
    row = jnp.repeat(jnp.arange(rows), c['GRID_W']).astype(F32)
    col = jnp.tile(jnp.arange(c['GRID_W']), rows).astype(F32)
    axis_dim = c['AD'] // 2
    freqs = c['ROPE_THETA'] ** (-jnp.arange(0, axis_dim, 2, dtype=F32) / axis_dim)
    ang = jnp.concatenate([row[:, None] * freqs, col[:, None] * freqs], axis=-1)
    cosf = jnp.repeat(jnp.cos(ang), 2, axis=1)
    sn = jnp.sin(ang)
    sins = jnp.stack([-sn, sn], axis=-1).reshape(L, c['AD'])
    idx = np.arange(c['AD'])
    perm = np.zeros((c['AD'], c['AD']), np.float32)
    perm[idx, idx ^ 1] = 1.0
    return cosf, sins, jnp.asarray(perm)


def _s5_dir_params(a, l, dr):
    return (a['ssm_a_re'][l, dr], a['ssm_a_im'][l, dr], a['ssm_log_step'][l, dr], a['ssm_b_re'][l, dr],
            a['ssm_b_im'][l, dr], a['ssm_c_re'][l, dr], a['ssm_c_im'][l, dr])


def _layer_fwd(x, mem, l, wt, a, rope, c, d, seg):
    L, D = x.shape
    SW, DW, AW, AKW, MW, H = d['SW'], d['DW'], d['AW'], d['AKW'], d['MW'], d['DNH']
    cb = lambda nm: seg[nm][0] // seg[nm][1]
    sv = {'x': x}
    p = f"l{l}_"
    sv['g_norm'] = a['norm_g'][l][None, :]
    xn, = _rowwise(_f_norm, [(x, D, 0)], [sv['g_norm']], [(D, BF16)], tm=256, name=p + "norm")
    h = _mm(xn, wt['wp'], name=p + "in_proj", tm=1024, tn=1536, tk=2048)
    sv['xn'], sv['h'] = xn, h

    ysum, sv['s5'] = None, []
    for dr in range(2):
        wb, wc, lr, li = _s5_prep(*_s5_dir_params(a, l, dr), d)
        wb16, wc16 = wb.astype(BF16), wc.astype(BF16)
        lt = _s5_tables(lr, li, bool(dr), False)
        ysum, cin = _s5_fwd(h, cb('u_a'), wb16, wc16, lt, rev=bool(dr), acc=ysum, tb=TILES['s5_t'],
                            name=p + f"s5_fwd{dr}", d=d)
        sv['s5'].append((wb16, wc16, lt, _s5_tables(lr, li, not bool(dr), True), cin))
    sv['ysum'] = ysum
    sv['s5_par'] = [a['ssm_d'][l][None, :], wt['w_glu'], a['ssm_b_glu'][l][None, :]]
    sv['s5_rows'] = [(ysum, SW, 0), (h, SW, cb('u_a')), (h, SW, cb('z_a'))]
    y_a, = _rowwise(_f_s5tail, sv['s5_rows'], sv['s5_par'], [(SW, F32)], tm=256, name=p + "s5_tail")

    act = _conv_fwd(h, cb('dq'), wt['conv'], tm=256, name=p + "dn_conv", d=d)
    sv['act'] = act
    sv['dn_par'] = [_lane_pad(a['dn_a_log'][l]), _lane_pad(a['dn_dt_bias'][l])]
    sv['dn_rows'] = [(act, DW, 0), (act, DW, 1), (h, LANE, seg['dadb'][0] // LANE)]
    dn_out = _rowwise(_make_f_dnpre(H, d['DNK'], c['CHUNK']), sv['dn_rows'], sv['dn_par'], [(DW, F32)] * 8,
                      tm=256, name=p + "dn_pre")
    qn, kn = dn_out[:2]
    sv['qn'], sv['kn'], sv['gates'] = qn, kn, [dn_out[2:5], dn_out[5:8]]
    o_dn, sv['dn_state'] = None, []
    for dr in range(2):
        o_dn, ss = _delta_fwd(qn, kn, act, sv['gates'][dr], vcb=2, rev=bool(dr), acc=o_dn,
                              name=p + f"dn_fwd{dr}", d=d)
        sv['dn_state'].append(ss)
    sv['dnpost_rows'] = [(o_dn, DW, 0), (h, DW, cb('z_b'))]
    sv['dnpost_par'] = [a['dn_norm_g'][l][None, :]]
    y_b, = _rowwise(_make_f_dnpost(d['DNK']), sv['dnpost_rows'], sv['dnpost_par'], [(DW, F32)], tm=256,
                    name=p + "dn_post")

    cosf, sins, perm = rope
    sv['att_par'] = [perm, a['attn_q_norm'][l][None, :], a['attn_k_norm'][l][None, :]]
    qh, kh, vh = _rowwise(_make_f_attpre(d['AD'], True),
                          [(h, AW, cb('aq')), (h, AKW, cb('ak')), (h, AKW, cb('av')), (cosf, d['AD'], 0),
                           (sins, d['AD'], 0)], sv['att_par'], [(AW, BF16), (AKW, BF16), (AKW, BF16)],
                          tm=256, name=p + "att_pre")
    o_att, lse = _attn_fwd(qh, kh, vh, tq=TILES['att_q'], tk=TILES['att_k'], name=p + "att_fwd", d=d)
    sv['qh'], sv['kh'], sv['vh'], sv['o_att'], sv['lse'] = qh, kh, vh, o_att, lse
    y_c, = _rowwise(_f_gate, [(o_att, AW, 0), (h, AW, cb('z_c'))], [], [(AW, F32)], tm=256, name=p + "att_post")

    sv['g_mem'] = a['mem_norm_g'][l][None, :]
    memn, = _rowwise(_f_norm, [(mem, D, 0)], [sv['g_mem']], [(D, BF16)], tm=256, name=p + "mem_norm")
    kv = _mm(memn, wt['w_mem_kv'], name=p + "mem_kv")
    sv['memn'], sv['kv'] = memn, kv
    y_m, = _rowwise(_make_f_mem(d['MH'], d['MD']), [(h, MW, cb('mq')), (h, MW, cb('z_m'))], [kv], [(MW, F32)],
                    tm=256, name=p + "mem_attn")

    ys = [y_a, y_b, y_c, y_m]
    ps = [_mm(y, wb_, name=p + f"branch_proj{i}", out_dtype=BF16)
          for i, (y, wb_) in enumerate(zip(ys, wt['w_branch']))]
    gcb = seg['gates'][0] // D
    sv['merge_rows'] = [(pp, D, 0) for pp in ps] + [(h, D, gcb + i) for i in range(4)]
    merged, = _rowwise(_f_merge, sv['merge_rows'], [], [(D, BF16)], tm=128, name=p + "merge")
    sv['ys'], sv['merged'] = ys, merged
    return _mm(merged, wt['w_out'], add=x, name=p + "out_proj"), sv


def _layer_bwd(dx, mem, l, wt, a, rope, sv, c, d, seg):
    L, D = dx.shape
    SW, DW, AW, AKW, MW, H = d['SW'], d['DW'], d['AW'], d['AKW'], d['MW'], d['DNH']
    cb = lambda nm: seg[nm][0] // seg[nm][1]
    p = f"l{l}_"
    h = sv['h']
    gr = {}
    dmerged = _mm(dx, wt['w_out'], tb=True, name=p + "d_merged")
    gr['w_out'] = _mm(sv['merged'], dx, ta=True, name=p + "dw_out")
    dmr, _ = _rowwise_bwd(_f_merge, sv['merge_rows'], [], [[(dmerged, D, 0)]], [True] * 8, [], tm=128,
                          name=p + "merge_bwd", row_grad_dtype=BF16)
    dps, dgates = dmr[:4], dmr[4:]
    dys = [_mm(dp, wb_, tb=True, name=p + f"d_branch{i}") for i, (dp, wb_) in enumerate(zip(dps, wt['w_branch']))]
    gr['w_branch'] = jnp.concatenate(
        [_mm(y, dp, ta=True, name=p + f"dw_branch{i}") for i, (y, dp) in enumerate(zip(sv['ys'], dps))], axis=0)

    (dmq, dzm), (dkv,) = _rowwise_bwd(_make_f_mem(d['MH'], d['MD']), [(h, MW, cb('mq')), (h, MW, cb('z_m'))],
                                      [sv['kv']], [[(dys[3], MW, 0)]], [True, True], [True], tm=256,
                                      name=p + "mem_attn_bwd")
    gr['w_mem_kv'] = _mm(sv['memn'], dkv, ta=True, name=p + "dw_mem_kv")
    dmemn = _mm(dkv, wt['w_mem_kv'], tb=True, name=p + "d_memn")
    _, (dg_mem,) = _rowwise_bwd(_f_norm, [(mem, D, 0)], [sv['g_mem']], [[(dmemn, D, 0)]], [False], [True], tm=256,
                                name=p + "mem_norm_bwd")
    gr['mem_norm_g'] = dg_mem[0]

    (do_att, dzc), _ = _rowwise_bwd(_f_gate, [(sv['o_att'], AW, 0), (h, AW, cb('z_c'))], [], [[(dys[2], AW, 0)]],
                                    [True, True], [], tm=256, name=p + "att_post_bwd")
    delta, = _rowwise(_make_f_delta(d['AD']), [(do_att, AW, 0), (sv['o_att'], AW, 0)], [], [(AW, F32)], tm=256,
                      name=p + "att_delta")
    att_in = (sv['qh'], sv['kh'], sv['vh'], do_att, sv['lse'], delta)
    dqh, dkh, dvh = _attn_bwd(*att_in, tq=TILES['att_q'], tk=TILES['att_k'], name=p + "att_bwd", d=d)
    cosf, sins, _ = rope
    (daq, dak), (dqg, dkg) = _rowwise_bwd(
        _make_f_attpre(d['AD'], False),
        [(h, AW, cb('aq')), (h, AKW, cb('ak')), (cosf, d['AD'], 0), (sins, d['AD'], 0)], sv['att_par'],
        [[(dqh, AW, 0)], [(dkh, AKW, 0)]], [True, True, False, False], [False, True, True], tm=256,
        name=p + "att_pre_bwd")
    gr['attn_q_norm'], gr['attn_k_norm'] = dqg[0], dkg[0]

    (do_dn, dzb), (dng,) = _rowwise_bwd(_make_f_dnpost(d['DNK']), sv['dnpost_rows'], sv['dnpost_par'],
                                        [[(dys[1], DW, 0)]], [True, True], [True], tm=256, name=p + "dn_post_bwd")
    gr['dn_norm_g'] = dng[0]
    accs, dn_dgates = None, []
    for dr in range(2):
        res = _delta_bwd(sv['qn'], sv['kn'], sv['act'], sv['gates'][dr], sv['dn_state'][dr], do_dn, vcb=2,
                         rev=bool(dr), accs=accs, name=p + f"dn_bwd{dr}", d=d)
        accs = res[:3]
        dn_dgates += res[3:]
    dqn, dkn, dvc = accs
    (dqc, dkc, ddadb), (dalog, ddtb) = _rowwise_bwd(
        _make_f_dnpre(H, d['DNK'], c['CHUNK']), sv['dn_rows'], sv['dn_par'],
        [[(t, DW, 0)] for t in [dqn, dkn] + dn_dgates], [True] * 3, [True, True], tm=256, name=p + "dn_pre_bwd")
    gr['dn_a_log'] = dalog[0, :2 * H].reshape(2, H)
    gr['dn_dt_bias'] = ddtb[0, :2 * H].reshape(2, H)
    dconv_x, dconv_w = _conv_bwd(h, cb('dq'), wt['conv'], jnp.concatenate([dqc, dkc, dvc], axis=1), tm=256,
                                 name=p + "dn_conv_bwd", d=d)
    gr['dn_conv'] = jnp.transpose(dconv_w[:, :c['CONV'], :], (0, 2, 1)).reshape(3 * DW, c['CONV'])

    (dysum, du, dza), (dd, dwglu, dbglu) = _rowwise_bwd(_f_s5tail, sv['s5_rows'], sv['s5_par'], [[(dys[0], SW, 0)]],
                                                        [True] * 3, [True] * 3, tm=256, name=p + "s5_tail_bwd")
    gr['ssm_d'], gr['ssm_w_glu'], gr['ssm_b_glu'] = dd[0], dwglu, dbglu[0]
    s5g = []
    for dr in range(2):
        wb16, wc16, lt, lt_adj, cin = sv['s5'][dr]
        du, dwb, dwc, dlam = _s5_bwd(h, cb('u_a'), dysum, cin, wb16, wc16, lt, lt_adj, rev=bool(dr), acc=du,
                                     tb=TILES['s5_t'],
                                     name=p + f"s5_bwd{dr}", d=d)
        dl = jnp.sum(dlam, axis=0).reshape(d['NB'], 2, d['BS'])
        _, prep_vjp = jax.vjp(lambda *pp: _s5_prep(*pp, d), *_s5_dir_params(a, l, dr))
        s5g.append(prep_vjp((dwb, dwc, dl[:, 0], dl[:, 1])))
    for i, nm in enumerate(['ssm_a_re', 'ssm_a_im', 'ssm_log_step', 'ssm_b_re', 'ssm_b_im', 'ssm_c_re', 'ssm_c_im']):
        gr[nm] = jnp.stack([s5g[0][i], s5g[1][i]], axis=0)

    dh = _assemble_dh({'u_a': du, 'z_a': dza, 'dq': dconv_x, 'z_b': dzb, 'ak': dak, 'av': dvh, 'aq': daq,
                       'z_c': dzc, 'mq': dmq, 'z_m': dzm, 'gates': jnp.concatenate(dgates, axis=1),
                       'dadb': ddadb}, seg, L).astype(BF16)
    gr['wp'] = _mm(sv['xn'], dh, ta=True, name=p + "dw_in", tm=1024, tn=1536, tk=2048)
    dxn = _mm(dh, wt['wp'], tb=True, name=p + "d_xn", tm=1024, tn=1024, tk=1536)
    (dx_in,), (dg_norm,) = _rowwise_bwd(_f_norm, [(sv['x'], D, 0)], [sv['g_norm']], [[(dxn, D, 0)]], [True], [True],
                                        tm=256, name=p + "norm_bwd", accs={0: (dx, D, 0)})
    gr['norm_g'] = dg_norm[0]
    return dx_in, gr


_ARG_NAMES = (['x', 'mem'] + WEIGHTS + ['loss_target'] + ['m_' + w for w in WEIGHTS] + ['v_' + w for w in WEIGHTS])


def kernel(x, mem, norm_g, w_in, ssm_a_re, ssm_a_im, ssm_log_step, ssm_b_re, ssm_b_im, ssm_c_re, ssm_c_im,
           ssm_d, ssm_w_glu, ssm_b_glu, dn_conv, dn_a_log, dn_dt_bias, dn_norm_g, attn_q_norm, attn_k_norm,
           mem_norm_g, w_mem_kv, w_branch, w_out, final_norm_g, loss_target, m_norm_g, m_w_in, m_ssm_a_re,
           m_ssm_a_im, m_ssm_log_step, m_ssm_b_re, m_ssm_b_im, m_ssm_c_re, m_ssm_c_im, m_ssm_d, m_ssm_w_glu,
           m_ssm_b_glu, m_dn_conv, m_dn_a_log, m_dn_dt_bias, m_dn_norm_g, m_attn_q_norm, m_attn_k_norm,
           m_mem_norm_g, m_w_mem_kv, m_w_branch, m_w_out, m_final_norm_g, v_norm_g, v_w_in, v_ssm_a_re,
           v_ssm_a_im, v_ssm_log_step, v_ssm_b_re, v_ssm_b_im, v_ssm_c_re, v_ssm_c_im, v_ssm_d, v_ssm_w_glu,
           v_ssm_b_glu, v_dn_conv, v_dn_a_log, v_dn_dt_bias, v_dn_norm_g, v_attn_q_norm, v_attn_k_norm,
           v_mem_norm_g, v_w_mem_kv, v_w_branch, v_w_out, v_final_norm_g):
    given = locals()
    return _train_step({n: given[n] for n in _ARG_NAMES})


def _train_step(a):
    c = CFG
    d = _dims(c)
    seg, rseg = _layout(c)
    depth, nh2 = c['DEPTH'], 2 * c['DNH']
    x, mem, tgt = a['x'][0], a['mem'][0], a['loss_target'][0]
    L, D = x.shape

    packed = [n for n in SHARDED if n != 'w_in']
    shard_shapes = [a[n].shape for n in packed]
    rw = _round_up(sum(_rows_of(s) for s in shard_shapes), LANE)
    win_shape = a['w_in'].shape
    wcols = win_shape[2]
    g_win, gathered = _all_gather([a['w_in'].astype(BF16).reshape(depth * D, wcols),
                                   _pack([a[n] for n in packed], BF16, total_rows=rw)], name="weights_all_gather")
    full = {n: _to_full(p_) for n, p_ in zip(packed, _unpack(gathered, shard_shapes, lead=1))}
    offs = np.cumsum([0, d['SW'], d['DW'], d['AW'], d['MW']])
    wts = []
    for l in range(depth):
        conv = jnp.transpose(full['dn_conv'][l].astype(F32).reshape(3, d['DW'], c['CONV']), (0, 2, 1))
        wts.append(dict(
            wp=_w_in_to_layout(g_win[:, l * D:(l + 1) * D], seg, rseg, nh2),
            w_branch=[full['w_branch'][l, offs[i]:offs[i + 1]] for i in range(4)],
            w_out=full['w_out'][l], w_mem_kv=full['w_mem_kv'][l], w_glu=full['ssm_w_glu'][l].astype(F32),
            conv=jnp.pad(conv, ((0, 0), (0, 8 - c['CONV']), (0, 0)))))
    rope = _rope_tables(L, c)

    saved = []
    for l in range(depth):
        x, sv = _layer_fwd(x, mem, l, wts[l], a, rope, c, d, seg)
        saved.append(sv)
    loss_part, dx, dg_final = _loss_grad(x, a['final_norm_g'][None, :], tgt, tm=256, name="final_norm_loss")
    grads = [None] * depth
    for l in reversed(range(depth)):
        dx, grads[l] = _layer_bwd(dx, mem, l, wts[l], a, rope, saved[l], c, d, seg)

    gfull = {n: jnp.stack([grads[l][n] for l in range(depth)], axis=0) for n in WEIGHTS
             if n not in ('w_in', 'final_norm_g')}
    gfull['final_norm_g'] = dg_final[0]

    win_slabs = jnp.concatenate([_w_in_slabs(grads[l]['wp'], seg, rseg, nh2) for l in range(depth)], axis=1)
    small_shapes = [a[n].shape for n in SMALL] + [(1,)]
    rs = _round_up(_rows_of((sum(int(np.prod(s)) for s in small_shapes),)), LANE)
    g_shard = _pack([_to_slabs(gfull[n]) for n in packed], F32, lead=1, total_rows=rw)
    g_small = _pack_flat([gfull[n] for n in SMALL] + [loss_part[0, :1]], rs)
    slabs = [win_slabs, g_shard, jnp.broadcast_to(g_small[None], (N_DEV,) + g_small.shape)]
    core = lax.axis_index("c").astype(jnp.int32).reshape(1)
    from_sibling = _pair_exchange(slabs, name="grads_pair_exchange")
    pair_sums = [_pair_sum(g, r, core, out_dtype=dt, tr=256, name=f"grads_pair_sum{i}")
                 for i, (g, r, dt) in enumerate(zip(slabs, from_sibling, (BF16, BF16, F32)))]
    recv = _chip_exchange(pair_sums, name="grads_chip_exchange")
    g_win_sum = _sum_slots(recv[0], tr=256, name="w_in_grad_sum")
    gsum = jnp.concatenate([_sum_slots(recv[1], tr=256, name="shard_grad_sum"),
                            _sum_slots(recv[2], tr=256, name="small_grad_sum")], axis=0)
    flat = lambda t: t.reshape(depth * D, wcols)
    d_win, m_win, v_win = _adamw(flat(a['w_in']), g_win_sum, flat(a['m_w_in']), flat(a['v_w_in']), tr=256,
                                 name="w_in_adamw")
    win_out = [t.reshape(win_shape) for t in (g_win_sum, d_win, m_win, v_win)]

    def local_pack(prefix):
        zero = jnp.zeros((1,), F32)
        return jnp.concatenate([_pack([a[prefix + n] for n in packed], F32, total_rows=rw),
                                _pack_flat([a[prefix + n] for n in SMALL] + [zero], rs)], axis=0)

    delta, new_m, new_v = _adamw(local_pack(''), gsum, local_pack('m_'), local_pack('v_'), tr=256, name="adamw")

    def split(buf):
        vals = dict(zip(packed, _unpack(buf[:rw], shard_shapes)))
        small = _unpack_flat(buf[rw:], small_shapes)
        vals.update(zip(SMALL, small[:-1]))
        return vals, small[-1]

    _, loss = split(gsum)
    outs = [loss.reshape(()), dx[None]]
    for i, buf in enumerate((gsum, delta, new_m, new_v)):
        vals, _ = split(buf)
        vals['w_in'] = win_out[i]
        outs += [vals[n] for n in WEIGHTS]
    return tuple(outs)
```

```python
import functools
import math

import numpy as np
import jax
import jax.numpy as jnp
from jax import lax
from jax.experimental import pallas as pl
from jax.experimental.pallas import tpu as pltpu

F32 = jnp.float32
BF16 = jnp.bfloat16
HI = lax.Precision.HIGHEST
EPS = 1e-6
LANE = 128
SUBLANE = 8
VMEM_LIMIT = 56 * 1024 * 1024
N_DEV = 8
PACK_W = 1024

ADAM_LR, ADAM_B1, ADAM_B2, ADAM_EPS, ADAM_WD, ADAM_STEP = 0.001, 0.9, 0.999, 1e-08, 0.01, 10

CFG = dict(D=2048, L=8192, GRID_W=64, NMEM=256, DEPTH=2,
           SG=48, SP=16, SN=64,
           DNH=6, DNK=128, CONV=5, CHUNK=64,
           AH=8, AKV=2, AD=128, ROPE_THETA=10000.0,
           MH=4, MD=128)

TILES = dict(att_q=2048, att_k=2048, s5_t=512)

WEIGHTS = ['norm_g', 'w_in', 'ssm_a_re', 'ssm_a_im', 'ssm_log_step', 'ssm_b_re', 'ssm_b_im', 'ssm_c_re',
           'ssm_c_im', 'ssm_d', 'ssm_w_glu', 'ssm_b_glu', 'dn_conv', 'dn_a_log', 'dn_dt_bias', 'dn_norm_g',
           'attn_q_norm', 'attn_k_norm', 'mem_norm_g', 'w_mem_kv', 'w_branch', 'w_out', 'final_norm_g']
SHARDED = ['w_in', 'w_branch', 'w_out', 'w_mem_kv', 'ssm_w_glu', 'dn_conv']
SMALL = [w for w in WEIGHTS if w not in SHARDED]


def _dims(c):
    d = dict(c)
    d['SW'] = c['SG'] * c['SP']
    d['NB'] = d['SW'] // LANE
    d['GPB'] = LANE // c['SP']
    d['BS'] = d['GPB'] * c['SN']
    d['DW'] = c['DNH'] * c['DNK']
    d['AW'] = c['AH'] * c['AD']
    d['AKW'] = c['AKV'] * c['AD']
    d['MW'] = c['MH'] * c['MD']
    d['BT'] = d['SW'] + d['DW'] + d['AW'] + d['MW']
    return d


def _round_up(a, b):
    return (a + b - 1) // b * b


def _layout(c):
    d = _dims(c)
    D, SW, DW, AW, AKW, MW = d['D'], d['SW'], d['DW'], d['AW'], d['AKW'], d['MW']
    order = [('u_a', SW, SW), ('z_a', SW, SW), ('dq', DW, DW), ('dk', DW, DW), ('dv', DW, DW), ('z_b', DW, DW),
             ('ak', AKW, AKW), ('av', AKW, AKW), ('aq', AW, AW), ('z_c', AW, AW), ('mq', MW, MW), ('z_m', MW, MW),
             ('gates', 4 * D, D), ('dadb', LANE, LANE)]
    off, seg = 0, {}
    for name, w, al in order:
        off = _round_up(off, al)
        seg[name] = (off, w)
        off += w
    seg['_total'] = _round_up(off, 512)
    ref_order = [('u_a', SW), ('z_a', SW), ('dq', DW), ('dk', DW), ('dv', DW), ('da', 2 * d['DNH']),
                 ('db', 2 * d['DNH']), ('z_b', DW), ('aq', AW), ('ak', AKW), ('av', AKW), ('z_c', AW),
                 ('mq', MW), ('z_m', MW), ('gates', 4 * D)]
    roff, rseg = 0, {}
    for name, w in ref_order:
        rseg[name] = (roff, w)
        roff += w
    rseg['_total'] = roff
    return seg, rseg


def _cparams(sem):
    return pltpu.CompilerParams(dimension_semantics=sem, vmem_limit_bytes=VMEM_LIMIT)


def _pick(t, n):
    if n <= t:
        return n
    for align in (LANE, 2 * SUBLANE):
        for cand in range(t - t % align, 0, -align):
            if n % cand == 0:
                return cand
    return n


def _mm(a, b, *, name, ta=False, tb=False, add=None, out_dtype=F32, tm=1024, tn=1024, tk=1024):
    M, K = (a.shape[1], a.shape[0]) if ta else a.shape
    N = b.shape[0] if tb else b.shape[1]
    assert (b.shape[1] if tb else b.shape[0]) == K
    tm, tn, tk = _pick(tm, M), _pick(tn, N), _pick(tk, K)
    nk = K // tk
    dn = (((0 if ta else 1,), (1 if tb else 0,)), ((), ()))
    has_add = add is not None

    def body(*refs):
        if has_add:
            a_ref, b_ref, add_ref, o_ref, acc = refs
        else:
            a_ref, b_ref, o_ref, acc = refs
        k = pl.program_id(2)

        @pl.when(k == 0)
        def _():
            acc[...] = jnp.zeros_like(acc)

        acc[...] += lax.dot_general(a_ref[...].astype(BF16), b_ref[...].astype(BF16), dn,
                                    preferred_element_type=F32)

        @pl.when(k == nk - 1)
        def _():
            r = acc[...]
            if has_add:
                r = r + add_ref[...]
            o_ref[...] = r.astype(o_ref.dtype)

    a_spec = pl.BlockSpec((tk, tm), lambda i, j, k: (k, i)) if ta else pl.BlockSpec((tm, tk), lambda i, j, k: (i, k))
    b_spec = pl.BlockSpec((tn, tk), lambda i, j, k: (j, k)) if tb else pl.BlockSpec((tk, tn), lambda i, j, k: (k, j))
    in_specs = [a_spec, b_spec]
    args = [a, b]
    if has_add:
        in_specs.append(pl.BlockSpec((tm, tn), lambda i, j, k: (i, j)))
        args.append(add)
    return pl.pallas_call(
        body, name=name, grid=(M // tm, N // tn, nk),
        in_specs=in_specs, out_specs=pl.BlockSpec((tm, tn), lambda i, j, k: (i, j)),
        out_shape=jax.ShapeDtypeStruct((M, N), out_dtype),
        scratch_shapes=[pltpu.VMEM((tm, tn), F32)],
        compiler_params=_cparams(("parallel", "parallel", "arbitrary")),
    )(*args)


def _row_spec(tm, w, cb):
    return pl.BlockSpec((tm, w), lambda i, cb=cb: (i, cb))


def _rowwise(fn, rows, params, outs, *, tm, name):
    L = rows[0][0].shape[0]
    tm = _pick(tm, L)
    nr, npar = len(rows), len(params)

    def body(*refs):
        vals = [r[...] for r in refs[:nr + npar]]
        res = fn(*vals)
        for o_ref, v in zip(refs[nr + npar:], res):
            o_ref[...] = v.astype(o_ref.dtype)

    in_specs = [_row_spec(tm, w, cb) for (_, w, cb) in rows]
    in_specs += [pl.BlockSpec(p.shape, lambda i: (0, 0)) for p in params]
    res = pl.pallas_call(
        body, name=name, grid=(L // tm,), in_specs=in_specs,
        out_specs=[pl.BlockSpec((tm, w), lambda i: (i, 0)) for (w, _) in outs],
        out_shape=[jax.ShapeDtypeStruct((L, w), dt) for (w, dt) in outs],
        compiler_params=_cparams(("parallel",)),
    )(*[r[0] for r in rows], *params)
    return list(res)


def _rowwise_bwd(fn, rows, params, cts, drows, dparams, *, tm, name, accs=None, row_grad_dtype=F32):
    L = rows[0][0].shape[0]
    tm = _pick(tm, L)
    nr, npar = len(rows), len(params)
    accs = accs or {}
    ct_flat = [c for grp in cts for c in grp]
    ct_sizes = [len(grp) for grp in cts]
    acc_keys = sorted(accs)
    d_r = [i for i in range(nr) if drows[i]]
    d_p = [i for i in range(npar) if dparams[i]]
    n_in = nr + npar + len(ct_flat) + len(acc_keys)

    def body(*refs):
        vals = [r[...] for r in refs[:nr + npar]]
        ct_refs = refs[nr + npar:nr + npar + len(ct_flat)]
        acc_refs = refs[nr + npar + len(ct_flat):n_in]
        o_refs = refs[n_in:]
        ct_vals, pos = [], 0
        for n in ct_sizes:
            v = ct_refs[pos][...].astype(F32)
            for r in ct_refs[pos + 1:pos + n]:
                v = v + r[...].astype(F32)
            ct_vals.append(v)
            pos += n
        diff_idx = d_r + [nr + i for i in d_p]

        def g(*dv):
            full = list(vals)
            for i, v in zip(diff_idx, dv):
                full[i] = v
            return tuple(o.astype(F32) for o in fn(*full))

        _, vjp = jax.vjp(g, *[vals[i] for i in diff_idx])
        grads = vjp(tuple(ct_vals))
        for n, i in enumerate(d_r):
            gv = grads[n].astype(F32)
            if i in accs:
                gv = gv + acc_refs[acc_keys.index(i)][...]
            o_refs[n][...] = gv.astype(o_refs[n].dtype)
        step = pl.program_id(0)
        for n, i in enumerate(d_p):
            o_ref = o_refs[len(d_r) + n]

            @pl.when(step == 0)
            def _(o_ref=o_ref):
                o_ref[...] = jnp.zeros_like(o_ref)

            o_ref[...] += grads[len(d_r) + n].astype(F32)

    in_specs = [_row_spec(tm, w, cb) for (_, w, cb) in rows]
    in_specs += [pl.BlockSpec(p.shape, lambda i: (0, 0)) for p in params]
    in_specs += [_row_spec(tm, w, cb) for (_, w, cb) in ct_flat]
    in_specs += [_row_spec(tm, accs[k][1], accs[k][2]) for k in acc_keys]
    out_specs = [pl.BlockSpec((tm, rows[i][1]), lambda i_: (i_, 0)) for i in d_r]
    out_specs += [pl.BlockSpec(params[i].shape, lambda i_: (0, 0)) for i in d_p]
    out_shape = [jax.ShapeDtypeStruct((L, rows[i][1]), row_grad_dtype) for i in d_r]
    out_shape += [jax.ShapeDtypeStruct(params[i].shape, F32) for i in d_p]
    res = pl.pallas_call(
        body, name=name, grid=(L // tm,), in_specs=in_specs, out_specs=out_specs, out_shape=out_shape,
        compiler_params=_cparams(("arbitrary",)),
    )(*[r[0] for r in rows], *params, *[c[0] for c in ct_flat], *[accs[k][0] for k in acc_keys])
    res = list(res)
    return res[:len(d_r)], res[len(d_r):]


def _silu(x):
    return x * jax.nn.sigmoid(x)


def _rms(x, g):
    return x * lax.rsqrt(jnp.mean(x * x, axis=-1, keepdims=True) + EPS) * g


def _softplus(x):
    return jnp.maximum(x, 0.0) + jnp.log1p(jnp.exp(-jnp.abs(x)))


def _heads(x, hd):
    return [x[:, i * hd:(i + 1) * hd] for i in range(x.shape[1] // hd)]


def _f_norm(x, g):
    return (_rms(x, g),)


def _f_s5tail(ys, u, z, d, wglu, bglu):
    y = jax.nn.gelu(ys + d * u)
    gate = jax.nn.sigmoid(jnp.dot(y.astype(BF16), wglu.astype(BF16), preferred_element_type=F32) + bglu)
    return (y * gate * _silu(z),)


def _make_f_dnpre(nh, hd, chunk):
    def f(qc, kc, dadb, alog, dtb):
        tm = qc.shape[0]
        qn = [q * lax.rsqrt(jnp.sum(q * q, axis=-1, keepdims=True) + EPS) * (hd ** -0.5) for q in _heads(qc, hd)]
        kn = [k * lax.rsqrt(jnp.sum(k * k, axis=-1, keepdims=True) + EPS) for k in _heads(kc, hd)]
        g = -jnp.exp(alog) * _softplus(dadb + dtb)
        beta = jax.nn.sigmoid(dadb)
        ii = lax.broadcasted_iota(jnp.int32, (tm, tm), 0)
        jj = lax.broadcasted_iota(jnp.int32, (tm, tm), 1)
        same = (ii // chunk) == (jj // chunk)
        outs = [jnp.concatenate(qn, axis=1), jnp.concatenate(kn, axis=1)]
        gt = jnp.dot(same.astype(F32), g, precision=HI, preferred_element_type=F32)
        for dr in range(2):
            tri = jnp.logical_and(same, (ii <= jj) if dr else (ii >= jj)).astype(F32)
            gc = jnp.dot(tri, g, precision=HI, preferred_element_type=F32)

            def spread(t, lane0):
                return jnp.concatenate([jnp.broadcast_to(t[:, lane0 + h:lane0 + h + 1], (tm, hd))
                                        for h in range(nh)], axis=1)

            outs += [spread(beta, 2 * nh + dr * nh), spread(gc, dr * nh), spread(gt, dr * nh)]
        return tuple(outs)
    return f


def _make_f_dnpost(hd):
    def f(o, z, ng):
        y = [_rms(oh, ng) for oh in _heads(o, hd)]
        return (jnp.concatenate(y, axis=1) * _silu(z),)
    return f


def _make_f_attpre(hd, with_v):
    def rope(x, g, cosf, sins, perm, scale):
        xn = _rms(x, g)
        xs = jnp.dot(xn, perm, precision=HI, preferred_element_type=F32)
        return (xn * cosf + xs * sins) * scale

    def f(aq, ak, *rest):
        if with_v:
            av, cosf, sins, perm, qg, kg = rest
        else:
            cosf, sins, perm, qg, kg = rest
        qh = jnp.concatenate([rope(x, qg, cosf, sins, perm, hd ** -0.5) for x in _heads(aq, hd)], axis=1)
        kh = jnp.concatenate([rope(x, kg, cosf, sins, perm, 1.0) for x in _heads(ak, hd)], axis=1)
        return (qh, kh, av) if with_v else (qh, kh)
    return f


def _f_gate(o, z):
    return (o * _silu(z),)


def _make_f_mem(nh, hd):
    def f(mq, z, kv):
        mw = nh * hd
        outs = []
        for h, q in enumerate(_heads(mq, hd)):
            k = kv[:, h * hd:(h + 1) * hd]
            v = kv[:, mw + h * hd:mw + (h + 1) * hd]
            s = lax.dot_general(q.astype(BF16), k.astype(BF16), (((1,), (1,)), ((), ())),
                                preferred_element_type=F32) * (hd ** -0.5)
            s = s - jnp.max(s, axis=-1, keepdims=True)
            p = jnp.exp(s)
            p = p / jnp.sum(p, axis=-1, keepdims=True)
            outs.append(jnp.dot(p.astype(BF16), v.astype(BF16), preferred_element_type=F32))
        return (jnp.concatenate(outs, axis=1) * _silu(z),)
    return f


def _f_merge(p0, p1, p2, p3, g0, g1, g2, g3):
    return (jax.nn.sigmoid(g0) * p0 + jax.nn.sigmoid(g1) * p1 + jax.nn.sigmoid(g2) * p2 + jax.nn.sigmoid(g3) * p3,)


def _make_f_delta(hd):
    def f(do, o):
        out = [jnp.broadcast_to(jnp.sum(a * b, axis=-1, keepdims=True), a.shape)
               for a, b in zip(_heads(do, hd), _heads(o, hd))]
        return (jnp.concatenate(out, axis=1),)
    return f


def _s5_prep(a_re, a_im, log_step, b_re, b_im, c_re, c_im, d):
    nb, gpb, sn, sp = d['NB'], d['GPB'], d['SN'], d['SP']
    step = jnp.exp(log_step)[:, None]
    mag = jnp.exp(a_re * step)
    lam_re = mag * jnp.cos(a_im * step)
    lam_im = mag * jnp.sin(a_im * step)
    den = a_re * a_re + a_im * a_im
    nr, ni = lam_re - 1.0, lam_im
    coef_re = (nr * a_re + ni * a_im) / den
    coef_im = (ni * a_re - nr * a_im) / den
    bb_re = coef_re[..., None] * b_re - coef_im[..., None] * b_im
    bb_im = coef_re[..., None] * b_im + coef_im[..., None] * b_re
    eye = jnp.eye(gpb, dtype=F32)

    def blk_in(bb):
        t = bb.reshape(nb, gpb, sn, sp)
        return jnp.einsum("jgnp,gh->jgphn", t, eye).reshape(nb, gpb * sp, gpb * sn)

    def blk_out(cc):
        t = cc.reshape(nb, gpb, sp, sn)
        return jnp.einsum("jgpn,gh->jgnhp", t, eye).reshape(nb, gpb * sn, gpb * sp)

    wb = jnp.concatenate([blk_in(bb_re), blk_in(bb_im)], axis=2)
    wc = jnp.concatenate([blk_out(c_re), blk_out(-c_im)], axis=1)
    return wb, wc, lam_re.reshape(nb, gpb * sn), lam_im.reshape(nb, gpb * sn)


def _s5_tables(lam_re, lam_im, rev, conj):
    lr, li = lam_re, (-lam_im if conj else lam_im)

    def cmul(a, b):
        return a[0] * b[0] - a[1] * b[1], a[0] * b[1] + a[1] * b[0]

    pw = [(lr, li)]
    for _ in range(7):
        pw.append(cmul(pw[-1], (lr, li)))
    rows = jnp.arange(8)

    def bc(t, k):
        keep = (rows < 8 - k) if rev else (rows >= k)
        return t[:, None, :] * keep.astype(F32)[None, :, None]

    order = list(range(8))[::-1] if rev else list(range(8))
    pwr = jnp.stack([pw[i][0] for i in order], axis=1)
    pwi = jnp.stack([pw[i][1] for i in order], axis=1)
    tabs = [bc(pw[0][0], 1), bc(pw[0][1], 1), bc(pw[1][0], 2), bc(pw[1][1], 2), bc(pw[3][0], 4), bc(pw[3][1], 4),
            pwr, pwi]
    return jnp.stack(tabs, axis=1)


def _scan_group(xr, xi, lt_ref, j, cr, ci, rev):
    for lvl, k in enumerate((1, 2, 4)):
        l_r, l_i = lt_ref[j, 2 * lvl], lt_ref[j, 2 * lvl + 1]
        sh = (8 - k) if rev else k
        sr, si = pltpu.roll(xr, sh, 0), pltpu.roll(xi, sh, 0)
        xr, xi = xr + l_r * sr - l_i * si, xi + l_r * si + l_i * sr
    p_r, p_i = lt_ref[j, 6], lt_ref[j, 7]
    return xr + p_r * cr - p_i * ci, xi + p_r * ci + p_i * cr


def _last_row(x, rev):
    last = 0 if rev else 7
    return jnp.broadcast_to(x[last:last + 1, :], x.shape)


def _s5_fwd(hsrc, ucb, wb, wc, lt, *, rev, acc, tb, name, d):
    L, SW, NB, BS = hsrc.shape[0], d['SW'], d['NB'], d['BS']
    tb = _pick(tb, L)
    nblk, ngr = L // tb, tb // 8
    tix = (lambda b: nblk - 1 - b) if rev else (lambda b: b)
    has_acc = acc is not None

    def body(*refs):
        if has_acc:
            u_ref, wb_ref, wc_ref, lt_ref, acc_ref, y_ref, cin_ref, bu_s, car = refs
        else:
            u_ref, wb_ref, wc_ref, lt_ref, y_ref, cin_ref, bu_s, car = refs

        @pl.when(pl.program_id(0) == 0)
        def _():
            car[...] = jnp.zeros_like(car)

        cin_ref[...] = car[...]
        for j in range(NB):
            bu_s[:, j * 2 * BS:(j + 1) * 2 * BS] = jnp.dot(
                u_ref[:, j * LANE:(j + 1) * LANE].astype(BF16), wb_ref[j], preferred_element_type=F32)

        def grp(r, _):
            base = pl.multiple_of((ngr - 1 - r if rev else r) * 8, 8)
            for j in range(NB):
                c0 = j * 2 * BS
                xr, xi = _scan_group(bu_s[pl.ds(base, 8), c0:c0 + BS], bu_s[pl.ds(base, 8), c0 + BS:c0 + 2 * BS],
                                     lt_ref, j, car[:, c0:c0 + BS], car[:, c0 + BS:c0 + 2 * BS], rev)
                bu_s[pl.ds(base, 8), c0:c0 + BS] = xr
                bu_s[pl.ds(base, 8), c0 + BS:c0 + 2 * BS] = xi
                car[:, c0:c0 + BS] = _last_row(xr, rev)
                car[:, c0 + BS:c0 + 2 * BS] = _last_row(xi, rev)
            return 0

        lax.fori_loop(0, ngr, grp, 0)
        for j in range(NB):
            y = jnp.dot(bu_s[:, j * 2 * BS:(j + 1) * 2 * BS].astype(BF16), wc_ref[j], preferred_element_type=F32)
            if has_acc:
                y = y + acc_ref[:, j * LANE:(j + 1) * LANE]
            y_ref[:, j * LANE:(j + 1) * LANE] = y

    in_specs = [pl.BlockSpec((tb, SW), lambda b: (tix(b), ucb)),
                pl.BlockSpec(wb.shape, lambda b: (0, 0, 0)), pl.BlockSpec(wc.shape, lambda b: (0, 0, 0)),
                pl.BlockSpec(lt.shape, lambda b: (0, 0, 0, 0))]
    args = [hsrc, wb, wc, lt]
    if has_acc:
        in_specs.append(pl.BlockSpec((tb, SW), lambda b: (tix(b), 0)))
        args.append(acc)
    y, cin = pl.pallas_call(
        body, name=name, grid=(nblk,), in_specs=in_specs,
        out_specs=[pl.BlockSpec((tb, SW), lambda b: (tix(b), 0)),
                   pl.BlockSpec((8, NB * 2 * BS), lambda b: (tix(b), 0))],
        out_shape=[jax.ShapeDtypeStruct((L, SW), F32), jax.ShapeDtypeStruct((nblk * 8, NB * 2 * BS), F32)],
        scratch_shapes=[pltpu.VMEM((tb, NB * 2 * BS), F32), pltpu.VMEM((8, NB * 2 * BS), F32)],
        compiler_params=_cparams(("arbitrary",)),
    )(*args)
    return y, cin


def _s5_bwd(hsrc, ucb, dy, cin, wb, wc, lt, lt_adj, *, rev, acc, tb, name, d):
    L, SW, NB, BS = hsrc.shape[0], d['SW'], d['NB'], d['BS']
    tb = _pick(tb, L)
    nblk, ngr = L // tb, tb // 8
    arev = not rev
    tix = (lambda b: nblk - 1 - b) if arev else (lambda b: b)
    has_acc = acc is not None
    NT = (((1,), (1,)), ((), ()))
    TN = (((0,), (0,)), ((), ()))

    def body(*refs):
        if has_acc:
            (u_ref, dy_ref, cin_ref, wb_ref, wc_ref, lt_ref, la_ref, acc_ref,
             du_ref, dwb_ref, dwc_ref, dlam_ref, s_s, g_s, car, acar) = refs
        else:
            (u_ref, dy_ref, cin_ref, wb_ref, wc_ref, lt_ref, la_ref,
             du_ref, dwb_ref, dwc_ref, dlam_ref, s_s, g_s, car, acar) = refs

        @pl.when(pl.program_id(0) == 0)
        def _():
            acar[...] = jnp.zeros_like(acar)
            dwb_ref[...] = jnp.zeros_like(dwb_ref)
            dwc_ref[...] = jnp.zeros_like(dwc_ref)
            dlam_ref[...] = jnp.zeros_like(dlam_ref)

        car[...] = cin_ref[...]
        for j in range(NB):
            s_s[:, j * 2 * BS:(j + 1) * 2 * BS] = jnp.dot(
                u_ref[:, j * LANE:(j + 1) * LANE].astype(BF16), wb_ref[j], preferred_element_type=F32)
            g_s[:, j * 2 * BS:(j + 1) * 2 * BS] = lax.dot_general(
                dy_ref[:, j * LANE:(j + 1) * LANE].astype(BF16), wc_ref[j], NT, preferred_element_type=F32)

        def fgrp(r, _):
            base = pl.multiple_of((ngr - 1 - r if rev else r) * 8, 8)
            for j in range(NB):
                c0 = j * 2 * BS
                xr, xi = _scan_group(s_s[pl.ds(base, 8), c0:c0 + BS], s_s[pl.ds(base, 8), c0 + BS:c0 + 2 * BS],
                                     lt_ref, j, car[:, c0:c0 + BS], car[:, c0 + BS:c0 + 2 * BS], rev)
                s_s[pl.ds(base, 8), c0:c0 + BS] = xr
                s_s[pl.ds(base, 8), c0 + BS:c0 + 2 * BS] = xi
                car[:, c0:c0 + BS] = _last_row(xr, rev)
                car[:, c0 + BS:c0 + 2 * BS] = _last_row(xi, rev)
            return 0

        lax.fori_loop(0, ngr, fgrp, 0)

        row = lax.broadcasted_iota(jnp.int32, (8, BS), 0)

        def agrp(r, _):
            gi = ngr - 1 - r if arev else r
            base = pl.multiple_of(gi * 8, 8)
            pgi = gi + 1 if rev else gi - 1
            inside = jnp.logical_and(pgi >= 0, pgi < ngr)
            pbase = pl.multiple_of(jnp.clip(pgi, 0, ngr - 1) * 8, 8)
            for j in range(NB):
                c0 = j * 2 * BS
                ar, ai = _scan_group(g_s[pl.ds(base, 8), c0:c0 + BS], g_s[pl.ds(base, 8), c0 + BS:c0 + 2 * BS],
                                     la_ref, j, acar[:, c0:c0 + BS], acar[:, c0 + BS:c0 + 2 * BS], arev)
                g_s[pl.ds(base, 8), c0:c0 + BS] = ar
                g_s[pl.ds(base, 8), c0 + BS:c0 + 2 * BS] = ai
                acar[:, c0:c0 + BS] = _last_row(ar, arev)
                acar[:, c0 + BS:c0 + 2 * BS] = _last_row(ai, arev)
                sr, si = s_s[pl.ds(base, 8), c0:c0 + BS], s_s[pl.ds(base, 8), c0 + BS:c0 + 2 * BS]
                edge_r = jnp.where(inside, _last_row(s_s[pl.ds(pbase, 8), c0:c0 + BS], rev), cin_ref[:, c0:c0 + BS])
                edge_i = jnp.where(inside, _last_row(s_s[pl.ds(pbase, 8), c0 + BS:c0 + 2 * BS], rev),
                                   cin_ref[:, c0 + BS:c0 + 2 * BS])
                sh = 7 if rev else 1
                first = 7 if rev else 0
                pr = jnp.where(row == first, edge_r, pltpu.roll(sr, sh, 0))
                pi = jnp.where(row == first, edge_i, pltpu.roll(si, sh, 0))
                dlam_ref[:, c0:c0 + BS] += ar * pr + ai * pi
                dlam_ref[:, c0 + BS:c0 + 2 * BS] += ai * pr - ar * pi
            return 0

        lax.fori_loop(0, ngr, agrp, 0)
        for j in range(NB):
            a_j = g_s[:, j * 2 * BS:(j + 1) * 2 * BS].astype(BF16)
            u_j = u_ref[:, j * LANE:(j + 1) * LANE].astype(BF16)
            du = lax.dot_general(a_j, wb_ref[j], NT, preferred_element_type=F32)
            if has_acc:
                du = du + acc_ref[:, j * LANE:(j + 1) * LANE]
            du_ref[:, j * LANE:(j + 1) * LANE] = du
            dwb_ref[j] += lax.dot_general(u_j, a_j, TN, preferred_element_type=F32)
            dwc_ref[j] += lax.dot_general(s_s[:, j * 2 * BS:(j + 1) * 2 * BS].astype(BF16),
                                          dy_ref[:, j * LANE:(j + 1) * LANE].astype(BF16), TN,
                                          preferred_element_type=F32)

    W2 = NB * 2 * BS
    in_specs = [pl.BlockSpec((tb, SW), lambda b: (tix(b), ucb)), pl.BlockSpec((tb, SW), lambda b: (tix(b), 0)),
                pl.BlockSpec((8, W2), lambda b: (tix(b), 0)),
                pl.BlockSpec(wb.shape, lambda b: (0, 0, 0)), pl.BlockSpec(wc.shape, lambda b: (0, 0, 0)),
                pl.BlockSpec(lt.shape, lambda b: (0, 0, 0, 0)), pl.BlockSpec(lt_adj.shape, lambda b: (0, 0, 0, 0))]
    args = [hsrc, dy, cin, wb, wc, lt, lt_adj]
    if has_acc:
        in_specs.append(pl.BlockSpec((tb, SW), lambda b: (tix(b), 0)))
        args.append(acc)
    return pl.pallas_call(
        body, name=name, grid=(nblk,), in_specs=in_specs,
        out_specs=[pl.BlockSpec((tb, SW), lambda b: (tix(b), 0)),
                   pl.BlockSpec(wb.shape, lambda b: (0, 0, 0)), pl.BlockSpec(wc.shape, lambda b: (0, 0, 0)),
                   pl.BlockSpec((8, W2), lambda b: (0, 0))],
        out_shape=[jax.ShapeDtypeStruct((L, SW), F32), jax.ShapeDtypeStruct(wb.shape, F32),
                   jax.ShapeDtypeStruct(wc.shape, F32), jax.ShapeDtypeStruct((8, W2), F32)],
        scratch_shapes=[pltpu.VMEM((tb, W2), F32), pltpu.VMEM((tb, W2), F32),
                        pltpu.VMEM((8, W2), F32), pltpu.VMEM((8, W2), F32)],
        compiler_params=_cparams(("arbitrary",)),
    )(*args)


_NN = (((1,), (0,)), ((), ()))
_NT = (((1,), (1,)), ((), ()))
_TN = (((0,), (0,)), ((), ()))


def _dotb(a, b, dn=_NN):
    return lax.dot_general(a.astype(BF16), b.astype(BF16), dn, preferred_element_type=F32)


def _split(x):
    hi = x.astype(BF16)
    return hi, (x - hi.astype(F32)).astype(BF16)


def _dot3(a, b, dn=_NN):
    ah, al = _split(a)
    bh, bl = _split(b)
    f = lambda x, y: lax.dot_general(x, y, dn, preferred_element_type=F32)
    return f(ah, bh) + (f(ah, bl) + f(al, bh))


@jax.custom_vjp
def _dot3_nn(a, b):
    return _dot3(a, b, _NN)


_dot3_nn.defvjp(lambda a, b: (_dot3(a, b, _NN), (a, b)),
                lambda res, g: (_dotb(g, res[1], _NT), _dotb(res[0], g, _TN)))


@jax.custom_vjp
def _dot3_nt(a, b):
    return _dot3(a, b, _NT)


_dot3_nt.defvjp(lambda a, b: (_dot3(a, b, _NT), (a, b)),
                lambda res, g: (_dotb(g, res[1], _NN), _dotb(g, res[0], _TN)))


def _delta_chunk(rev, one_pass_grads, *flat):
    heads = [flat[i:i + 7] for i in range(0, len(flat), 7)]
    q, k, v, beta, gc, gt, s_in = [list(t) for t in zip(*heads)]
    c, hd = q[0].shape
    each = lambda f, *ls: [f(*t) for t in zip(*ls)]
    mm_nn = _dot3_nn if one_pass_grads else _dot3
    mm_nt = _dot3_nt if one_pass_grads else (lambda x, y: _dot3(x, y, _NT))
    ii = lax.broadcasted_iota(jnp.int32, (c, c), 0)
    jj = lax.broadcasted_iota(jnp.int32, (c, c), 1)
    incl = (ii <= jj) if rev else (ii >= jj)
    strict = (ii < jj) if rev else (ii > jj)
    eye = (ii == jj).astype(F32)
    decay = each(lambda g: jnp.where(incl, jnp.exp(jnp.where(incl, g[:, :c] - jnp.transpose(g)[:c, :], 0.0)), 0.0), gc)
    kb = each(lambda a, b: a * b, k, beta)
    a = each(lambda x, y, dc: jnp.where(strict, mm_nt(x, y) * dc, 0.0), kb, k, decay)
    tinv = each(lambda x: eye - x, a)
    p = a
    n = 2
    while n < c:
        p = each(lambda x: mm_nn(x, x), p)
        tinv = each(lambda t, x: mm_nn(t, eye + x), tinv, p)
        n *= 2
    eg = each(jnp.exp, gc)
    u = each(lambda t, x, b: mm_nn(t, x * b), tinv, v, beta)
    w = each(lambda t, x, e: mm_nn(t, x * e), tinv, kb, eg)
    intra = each(lambda x, y, dc: _dotb(x, y, _NT) * dc, q, k, decay)
    v_new = each(lambda x, y, s: x - _dotb(y, s), u, w, s_in)
    o = each(lambda x, e, s, m, vn: _dotb(x * e, s) + _dotb(m, vn), q, eg, s_in, intra, v_new)
    s_out = each(lambda s, t, x, g, vn: s * jnp.exp(jnp.broadcast_to(t[0:1, :], (hd, hd)))
                 + _dotb(x * jnp.exp(t - g), vn, _TN), s_in, gt, k, gc, v_new)
    return tuple(x for pair in zip(o, s_out) for x in pair)


def _delta_fwd(q, k, v, gates, *, vcb, rev, acc, name, d):
    L, H, hd, C = q.shape[0], d['DNH'], d['DNK'], d['CHUNK']
    nc = L // C
    cix = (lambda i: nc - 1 - i) if rev else (lambda i: i)
    has_acc = acc is not None

    def body(*refs):
        if has_acc:
            q_ref, k_ref, v_ref, b_ref, gc_ref, gt_ref, acc_ref, o_ref, ss_ref, st = refs
        else:
            q_ref, k_ref, v_ref, b_ref, gc_ref, gt_ref, o_ref, ss_ref, st = refs

        @pl.when(pl.program_id(0) == 0)
        def _():
            st[...] = jnp.zeros_like(st)

        sls = [slice(h * hd, (h + 1) * hd) for h in range(H)]
        ins = [(q_ref[:, sl], k_ref[:, sl], v_ref[:, sl], b_ref[:, sl], gc_ref[:, sl], gt_ref[:, sl], st[h])
               for h, sl in enumerate(sls)]
        accv = [acc_ref[:, sl] for sl in sls] if has_acc else None
        res = _delta_chunk(rev, False, *[t for head in ins for t in head])
        for h, sl in enumerate(sls):
            o, s_out = res[2 * h], res[2 * h + 1]
            ss_ref[0, h] = ins[h][6]
            o_ref[:, sl] = o + accv[h] if has_acc else o
            st[h] = s_out

    blk = pl.BlockSpec((C, H * hd), lambda i: (cix(i), 0))
    in_specs = [blk, blk, pl.BlockSpec((C, H * hd), lambda i: (cix(i), vcb)), blk, blk, blk]
    args = [q, k, v, *gates]
    if has_acc:
        in_specs.append(blk)
        args.append(acc)
    return pl.pallas_call(
        body, name=name, grid=(nc,), in_specs=in_specs,
        out_specs=[blk, pl.BlockSpec((1, H, hd, hd), lambda i: (cix(i), 0, 0, 0))],
        out_shape=[jax.ShapeDtypeStruct((L, H * hd), F32), jax.ShapeDtypeStruct((nc, H, hd, hd), F32)],
        scratch_shapes=[pltpu.VMEM((H, hd, hd), F32)],
        compiler_params=_cparams(("arbitrary",)),
    )(*args)


def _delta_bwd(q, k, v, gates, ssave, do, *, vcb, rev, accs, name, d):
    L, H, hd, C = q.shape[0], d['DNH'], d['DNK'], d['CHUNK']
    nc = L // C
    cix = (lambda i: i) if rev else (lambda i: nc - 1 - i)
    has_acc = accs is not None

    def body(*refs):
        if has_acc:
            (q_ref, k_ref, v_ref, b_ref, gc_ref, gt_ref, ss_ref, do_ref, aq_ref, ak_ref, av_ref,
             dq_ref, dk_ref, dv_ref, db_ref, dgc_ref, dgt_ref, dst) = refs
        else:
            (q_ref, k_ref, v_ref, b_ref, gc_ref, gt_ref, ss_ref, do_ref,
             dq_ref, dk_ref, dv_ref, db_ref, dgc_ref, dgt_ref, dst) = refs

        @pl.when(pl.program_id(0) == 0)
        def _():
            dst[...] = jnp.zeros_like(dst)

        sls = [slice(h * hd, (h + 1) * hd) for h in range(H)]
        ins = [(q_ref[:, sl], k_ref[:, sl], v_ref[:, sl], b_ref[:, sl], gc_ref[:, sl], gt_ref[:, sl], ss_ref[0, h])
               for h, sl in enumerate(sls)]
        cts = tuple(t for h, sl in enumerate(sls) for t in (do_ref[:, sl], dst[h]))
        accv = [(aq_ref[:, sl], ak_ref[:, sl], av_ref[:, sl]) for sl in sls] if has_acc else None
        _, vjp = jax.vjp(functools.partial(_delta_chunk, rev, True), *[t for head in ins for t in head])
        res = vjp(cts)
        for h, sl in enumerate(sls):
            dq, dk, dv, db, dgc, dgt, ds = res[7 * h:7 * h + 7]
            dst[h] = ds
            if has_acc:
                dq, dk, dv = dq + accv[h][0], dk + accv[h][1], dv + accv[h][2]
            dq_ref[:, sl] = dq
            dk_ref[:, sl] = dk
            dv_ref[:, sl] = dv
            db_ref[:, sl] = db
            dgc_ref[:, sl] = dgc
            dgt_ref[:, sl] = dgt

    blk = pl.BlockSpec((C, H * hd), lambda i: (cix(i), 0))
    in_specs = [blk, blk, pl.BlockSpec((C, H * hd), lambda i: (cix(i), vcb)), blk, blk, blk,
                pl.BlockSpec((1, H, hd, hd), lambda i: (cix(i), 0, 0, 0)), blk]
    args = [q, k, v, *gates, ssave, do]
    if has_acc:
        in_specs += [blk, blk, blk]
        args += list(accs)
    return pl.pallas_call(
        body, name=name, grid=(nc,), in_specs=in_specs, out_specs=[blk] * 6,
        out_shape=[jax.ShapeDtypeStruct((L, H * hd), F32)] * 6,
        scratch_shapes=[pltpu.VMEM((H, hd, hd), F32)],
        compiler_params=_cparams(("arbitrary",)),
    )(*args)


def _conv_specs(tm, w, cb0, nrb, L):
    hb = tm // 8
    last8 = L // 8 - 1
    cur = pl.BlockSpec((tm, w), lambda s, i: (i, cb0 + s))
    prev = pl.BlockSpec((8, w), lambda s, i: (jnp.maximum(i * hb - 1, 0), cb0 + s))
    nxt = pl.BlockSpec((8, w), lambda s, i: (jnp.minimum((i + 1) * hb, last8), cb0 + s))
    return [prev, cur, nxt]


def _fill_halo(dst, prev_ref, cur_ref, next_ref, i, nrb, tm):
    dst[pl.ds(0, 8), :] = jnp.where(i > 0, prev_ref[...], 0.0)
    dst[pl.ds(8, tm), :] = cur_ref[...]
    dst[pl.ds(8 + tm, 8), :] = jnp.where(i < nrb - 1, next_ref[...], 0.0)


def _conv_fwd(hsrc, cb0, wt, *, tm, name, d):
    L, w, K = hsrc.shape[0], d['DW'], d['CONV']
    tm = _pick(tm, L)
    nrb = L // tm

    def body(prev_ref, cur_ref, next_ref, w_ref, o_ref, xs):
        i = pl.program_id(1)
        _fill_halo(xs, prev_ref, cur_ref, next_ref, i, nrb, tm)
        y = jnp.zeros((tm, w), F32)
        for kk in range(K):
            y = y + w_ref[0, pl.ds(kk, 1), :] * xs[pl.ds(8 - K // 2 + kk, tm), :]
        o_ref[...] = _silu(y)

    return pl.pallas_call(
        body, name=name, grid=(3, nrb),
        in_specs=_conv_specs(tm, w, cb0, nrb, L) + [pl.BlockSpec((1, 8, w), lambda s, i: (s, 0, 0))],
        out_specs=pl.BlockSpec((tm, w), lambda s, i: (i, s)),
        out_shape=jax.ShapeDtypeStruct((L, 3 * w), F32),
        scratch_shapes=[pltpu.VMEM((tm + 16, w), F32)],
        compiler_params=_cparams(("parallel", "parallel")),
    )(hsrc, hsrc, hsrc, wt)


def _conv_bwd(hsrc, cb0, wt, dact, *, tm, name, d):
    L, w, K = hsrc.shape[0], d['DW'], d['CONV']
    tm = _pick(tm, L)
    nrb = L // tm
    half = K // 2

    def body(xp_ref, xc_ref, xn_ref, gp_ref, gc_ref, gn_ref, w_ref, dx_ref, dw_ref, xs, gs, dys):
        i = pl.program_id(1)
        _fill_halo(xs, xp_ref, xc_ref, xn_ref, i, nrb, tm)
        _fill_halo(gs, gp_ref, gc_ref, gn_ref, i, nrb, tm)
        y = jnp.zeros((tm + 8, w), F32)
        for kk in range(K):
            y = y + w_ref[0, pl.ds(kk, 1), :] * xs[pl.ds(4 - half + kk, tm + 8), :]
        sg = jax.nn.sigmoid(y)
        dys[...] = gs[pl.ds(4, tm + 8), :] * (sg * (1.0 + y * (1.0 - sg)))
        dx = jnp.zeros((tm, w), F32)
        for kk in range(K):
            dx = dx + w_ref[0, pl.ds(kk, 1), :] * dys[pl.ds(4 + half - kk, tm), :]
        dx_ref[...] = dx

        @pl.when(i == 0)
        def _():
            dw_ref[...] = jnp.zeros_like(dw_ref)

        dy = dys[pl.ds(4, tm), :]
        for kk in range(K):
            dw_ref[0, pl.ds(kk, 1), :] += jnp.sum(dy * xs[pl.ds(8 - half + kk, tm), :], axis=0, keepdims=True)

    gspecs = _conv_specs(tm, w, 0, nrb, L)
    return pl.pallas_call(
        body, name=name, grid=(3, nrb),
        in_specs=_conv_specs(tm, w, cb0, nrb, L) + gspecs + [pl.BlockSpec((1, 8, w), lambda s, i: (s, 0, 0))],
        out_specs=[pl.BlockSpec((tm, w), lambda s, i: (i, s)), pl.BlockSpec((1, 8, w), lambda s, i: (s, 0, 0))],
        out_shape=[jax.ShapeDtypeStruct((L, 3 * w), F32), jax.ShapeDtypeStruct((3, 8, w), F32)],
        scratch_shapes=[pltpu.VMEM((tm + 16, w), F32), pltpu.VMEM((tm + 16, w), F32), pltpu.VMEM((tm + 8, w), F32)],
        compiler_params=_cparams(("parallel", "arbitrary")),
    )(hsrc, hsrc, hsrc, dact, dact, dact, wt)


def _wide(v, n):
    return v if n == LANE else jnp.tile(v, (1, n // LANE))


def _attn_fwd(qh, kh, vh, *, tq, tk, name, d):
    L, H, KVH, hd = qh.shape[0], d['AH'], d['AKV'], d['AD']
    grp = H // KVH
    tq, tk = _pick(tq, L), _pick(tk, L)
    nk = L // tk

    def body(q_ref, k_ref, v_ref, o_ref, lse_ref, m_s, l_s, acc):
        j = pl.program_id(2)

        @pl.when(j == 0)
        def _():
            m_s[...] = jnp.full_like(m_s, -1e30)
            l_s[...] = jnp.zeros_like(l_s)
            acc[...] = jnp.zeros_like(acc)

        s = lax.dot_general(q_ref[...], k_ref[...], _NT, preferred_element_type=F32)
        m_old = m_s[...]
        m_new = jnp.maximum(m_old, jnp.max(s, axis=-1, keepdims=True))
        alpha = jnp.exp(m_old - m_new)
        p = jnp.exp(s - _wide(m_new, tk))
        l_s[...] = alpha * l_s[...] + jnp.sum(p, axis=-1, keepdims=True)
        acc[...] = alpha * acc[...] + jnp.dot(p.astype(BF16), v_ref[...], preferred_element_type=F32)
        m_s[...] = m_new

        @pl.when(j == nk - 1)
        def _():
            o_ref[...] = acc[...] / l_s[...]
            lse_ref[...] = m_s[...] + jnp.log(l_s[...])

    qspec = pl.BlockSpec((tq, hd), lambda h, i, j: (i, h))
    kspec = pl.BlockSpec((tk, hd), lambda h, i, j: (j, h // grp))
    return pl.pallas_call(
        body, name=name, grid=(H, L // tq, nk), in_specs=[qspec, kspec, kspec], out_specs=[qspec, qspec],
        out_shape=[jax.ShapeDtypeStruct((L, H * hd), F32), jax.ShapeDtypeStruct((L, H * hd), F32)],
        scratch_shapes=[pltpu.VMEM((tq, hd), F32), pltpu.VMEM((tq, hd), F32), pltpu.VMEM((tq, hd), F32)],
        compiler_params=_cparams(("parallel", "parallel", "arbitrary")),
    )(qh, kh, vh)


def _attn_bwd(qh, kh, vh, do, lse, delta, *, tq, tk, name, d):
    L, H, KVH, hd = qh.shape[0], d['AH'], d['AKV'], d['AD']
    grp = H // KVH
    tq, tk = _pick(tq, L), _pick(tk, L)
    nk = L // tk

    def body(q_ref, k_ref, v_ref, do_ref, lse_ref, dl_ref, dq_ref, dk_ref, dv_ref, dq_s):
        g, i, j = pl.program_id(1), pl.program_id(2), pl.program_id(3)

        @pl.when(jnp.logical_and(jnp.logical_and(g == 0, i == 0), j == 0))
        def _():
            dk_ref[...] = jnp.zeros_like(dk_ref)
            dv_ref[...] = jnp.zeros_like(dv_ref)

        @pl.when(j == 0)
        def _():
            dq_s[...] = jnp.zeros_like(dq_s)

        q, k, do_ = q_ref[...], k_ref[...], do_ref[...].astype(BF16)
        s = lax.dot_general(q, k, _NT, preferred_element_type=F32)
        p = jnp.exp(s - _wide(lse_ref[...], tk))
        dp = lax.dot_general(do_, v_ref[...], _NT, preferred_element_type=F32)
        ds = (p * (dp - _wide(dl_ref[...], tk))).astype(BF16)
        dq_s[...] += jnp.dot(ds, k, preferred_element_type=F32)
        rows = pl.ds(pl.multiple_of(j * tk, tk), tk)
        dv_ref[rows, :] += lax.dot_general(p.astype(BF16), do_, _TN, preferred_element_type=F32)
        dk_ref[rows, :] += lax.dot_general(ds, q, _TN, preferred_element_type=F32)

        @pl.when(j == nk - 1)
        def _():
            dq_ref[...] = dq_s[...]

    qspec = pl.BlockSpec((tq, hd), lambda kv, g, i, j: (i, kv * grp + g))
    kspec = pl.BlockSpec((tk, hd), lambda kv, g, i, j: (j, kv))
    colspec = pl.BlockSpec((L, hd), lambda kv, g, i, j: (0, kv))
    return pl.pallas_call(
        body, name=name, grid=(KVH, grp, L // tq, nk),
        in_specs=[qspec, kspec, kspec, qspec, qspec, qspec], out_specs=[qspec, colspec, colspec],
        out_shape=[jax.ShapeDtypeStruct((L, H * hd), F32)] + [jax.ShapeDtypeStruct((L, KVH * hd), F32)] * 2,
        scratch_shapes=[pltpu.VMEM((tq, hd), F32)],
        compiler_params=_cparams(("parallel", "arbitrary", "arbitrary", "arbitrary")),
    )(qh, kh, vh, do, lse, delta)


def _loss_grad(x, g, tgt, *, tm, name):
    L, D = x.shape
    tm = _pick(tm, L)

    def body(x_ref, g_ref, t_ref, loss_ref, dx_ref, dg_ref):
        def f(xv, gv):
            err = _rms(xv, gv) - t_ref[...]
            return 0.5 * jnp.sum(jnp.mean(err * err, axis=-1, keepdims=True))

        val, vjp = jax.vjp(f, x_ref[...], g_ref[...])
        dx, dg = vjp(jnp.ones((), F32))
        dx_ref[...] = dx

        @pl.when(pl.program_id(0) == 0)
        def _():
            loss_ref[...] = jnp.zeros_like(loss_ref)
            dg_ref[...] = jnp.zeros_like(dg_ref)

        loss_ref[...] += val
        dg_ref[...] += dg

    return pl.pallas_call(
        body, name=name, grid=(L // tm,),
        in_specs=[pl.BlockSpec((tm, D), lambda i: (i, 0)), pl.BlockSpec((1, D), lambda i: (0, 0)),
                  pl.BlockSpec((tm, D), lambda i: (i, 0))],
        out_specs=[pl.BlockSpec((8, LANE), lambda i: (0, 0)), pl.BlockSpec((tm, D), lambda i: (i, 0)),
                   pl.BlockSpec((1, D), lambda i: (0, 0))],
        out_shape=[jax.ShapeDtypeStruct((8, LANE), F32), jax.ShapeDtypeStruct((L, D), F32),
                   jax.ShapeDtypeStruct((1, D), F32)],
        compiler_params=_cparams(("arbitrary",)),
    )(x, g, tgt)


def _sum_slots(recv, *, tr, name):
    n, R, W = recv.shape
    tr = _pick(tr, R)

    def body(r_ref, o_ref):
        s = r_ref[0].astype(F32)
        for i in range(1, n):
            s = s + r_ref[i].astype(F32)
        o_ref[...] = s

    return pl.pallas_call(
        body, name=name, grid=(R // tr,),
        in_specs=[pl.BlockSpec((n, tr, W), lambda i: (0, i, 0))], out_specs=pl.BlockSpec((tr, W), lambda i: (i, 0)),
        out_shape=jax.ShapeDtypeStruct((R, W), F32), compiler_params=_cparams(("parallel",)),
    )(recv)


def _adamw(w, g, m, v, *, tr, name):
    R, W = w.shape
    tr = _pick(tr, R)
    c1 = 1.0 - ADAM_B1 ** ADAM_STEP
    c2 = 1.0 - ADAM_B2 ** ADAM_STEP

    def body(w_ref, g_ref, m_ref, v_ref, d_ref, nm_ref, nv_ref):
        gv = g_ref[...]
        nm = ADAM_B1 * m_ref[...] + (1.0 - ADAM_B1) * gv
        nv = ADAM_B2 * v_ref[...] + (1.0 - ADAM_B2) * (gv * gv)
        d_ref[...] = -ADAM_LR * ((nm / c1) / (jnp.sqrt(nv / c2) + ADAM_EPS) + ADAM_WD * w_ref[...])
        nm_ref[...] = nm
        nv_ref[...] = nv

    spec = pl.BlockSpec((tr, W), lambda i: (i, 0))
    return pl.pallas_call(
        body, name=name, grid=(R // tr,), in_specs=[spec] * 4, out_specs=[spec] * 3,
        out_shape=[jax.ShapeDtypeStruct((R, W), F32)] * 3, compiler_params=_cparams(("parallel",)),
    )(w, g, m, v)


_MESH = pl.DeviceIdType.MESH


def _all_gather(xs, *, name):
    na = len(xs)

    def body(*refs):
        x_refs, out_refs = refs[:na], refs[na:2 * na]
        send_sems, recv_sems, local_sems = refs[2 * na:]
        x, y, c = lax.axis_index("x"), lax.axis_index("y"), lax.axis_index("c")
        me, sibling = (x, y, c), (x, y, 1 - c)
        chips = [(1 - x, y), (x, 1 - y), (1 - x, 1 - y)]

        def slot(a, px, py, pc):
            return out_refs[a].at[4 * px + 2 * py + pc]

        def copy(a, k, block, to, src=None):
            return pltpu.make_async_remote_copy(
                src_ref=slot(a, *block) if src is None else src, dst_ref=slot(a, *block),
                send_sem=send_sems.at[7 * a + k], recv_sem=recv_sems.at[7 * a + k], device_id=to,
                device_id_type=_MESH)

        mine = [pltpu.make_async_copy(x_refs[a], slot(a, *me), local_sems.at[a]) for a in range(na)]
        for cp in mine:
            cp.start()
        first = []
        for a in range(na):
            first.append(copy(a, 0, me, sibling, src=x_refs[a]))
            first += [copy(a, 1 + j, me, (*chip, c), src=x_refs[a]) for j, chip in enumerate(chips)]
        for cp in first:
            cp.start()
        passed = []
        for a in range(na):
            for j, chip in enumerate(chips):
                copy(a, 1 + j, (*chip, c), me).wait_recv()
                passed.append(copy(a, 4 + j, (*chip, c), sibling))
                passed[-1].start()
        for a in range(na):
            copy(a, 0, sibling, me).wait_recv()
            for j, chip in enumerate(chips):
                copy(a, 4 + j, (*chip, 1 - c), me).wait_recv()
        for cp in first + passed:
            cp.wait_send()
        for cp in mine:
            cp.wait()

    return pl.pallas_call(
        body, name=name,
        out_shape=[jax.ShapeDtypeStruct((N_DEV,) + t.shape, t.dtype) for t in xs],
        in_specs=[pl.BlockSpec(memory_space=pl.ANY)] * na, out_specs=[pl.BlockSpec(memory_space=pl.ANY)] * na,
        scratch_shapes=[pltpu.SemaphoreType.DMA((7 * na,)), pltpu.SemaphoreType.DMA((7 * na,)),
                        pltpu.SemaphoreType.DMA((na,))],
    )(*xs)


N_CHIP = N_DEV // 2


def _pair_exchange(gs, *, name):
    na = len(gs)

    def body(*refs):
        g_refs, out_refs = refs[:na], refs[na:2 * na]
        send_sems, recv_sems = refs[2 * na:]
        x, y, c = lax.axis_index("x"), lax.axis_index("y"), lax.axis_index("c")

        def copy(a, chip, core):
            return pltpu.make_async_remote_copy(
                src_ref=g_refs[a].at[2 * chip + core], dst_ref=out_refs[a].at[chip],
                send_sem=send_sems.at[N_CHIP * a + chip], recv_sem=recv_sems.at[N_CHIP * a + chip],
                device_id=(x, y, 1 - c), device_id_type=_MESH)

        sends = [copy(a, chip, 1 - c) for a in range(na) for chip in range(N_CHIP)]
        for cp in sends:
            cp.start()
        for cp in sends:
            cp.wait_recv()
        for cp in sends:
            cp.wait_send()

    return pl.pallas_call(
        body, name=name,
        out_shape=[jax.ShapeDtypeStruct((N_CHIP,) + t.shape[1:], t.dtype) for t in gs],
        in_specs=[pl.BlockSpec(memory_space=pl.ANY)] * na, out_specs=[pl.BlockSpec(memory_space=pl.ANY)] * na,
        scratch_shapes=[pltpu.SemaphoreType.DMA((N_CHIP * na,)), pltpu.SemaphoreType.DMA((N_CHIP * na,))],
    )(*gs)


def _pair_sum(g, r, core, *, out_dtype, tr, name):
    _, R, W = g.shape
    tr = _pick(tr, R)

    def body(core_ref, g_ref, r_ref, o_ref):
        o_ref[...] = (g_ref[...] + r_ref[...]).astype(o_ref.dtype)

    return pl.pallas_call(
        body, name=name,
        grid_spec=pltpu.PrefetchScalarGridSpec(
            num_scalar_prefetch=1, grid=(N_CHIP, R // tr),
            in_specs=[pl.BlockSpec((1, tr, W), lambda ch, i, core_ref: (2 * ch + core_ref[0], i, 0)),
                      pl.BlockSpec((1, tr, W), lambda ch, i, core_ref: (ch, i, 0))],
            out_specs=pl.BlockSpec((1, tr, W), lambda ch, i, core_ref: (ch, i, 0))),
        out_shape=jax.ShapeDtypeStruct((N_CHIP, R, W), out_dtype),
        compiler_params=_cparams(("parallel", "parallel")),
    )(core, g, r)


def _chip_exchange(hs, *, name):
    na = len(hs)

    def body(*refs):
        h_refs, out_refs = refs[:na], refs[na:2 * na]
        send_sems, recv_sems, local_sems = refs[2 * na:]
        x, y, c = lax.axis_index("x"), lax.axis_index("y"), lax.axis_index("c")
        me = 2 * x + y

        def copy(a, mask, started):
            px, py = x ^ (mask >> 1), y ^ (mask & 1)
            mine_, theirs = me, 2 * px + py
            return pltpu.make_async_remote_copy(
                src_ref=h_refs[a].at[theirs if started else mine_],
                dst_ref=out_refs[a].at[mine_ if started else theirs],
                send_sem=send_sems.at[3 * a + mask - 1], recv_sem=recv_sems.at[3 * a + mask - 1],
                device_id=(px, py, c), device_id_type=_MESH)

        mine = [pltpu.make_async_copy(h_refs[a].at[me], out_refs[a].at[me], local_sems.at[a]) for a in range(na)]
        for cp in mine:
            cp.start()
        sends = [copy(a, mask, True) for a in range(na) for mask in range(1, N_CHIP)]
        for cp in sends:
            cp.start()
        for a in range(na):
            for mask in range(1, N_CHIP):
                copy(a, mask, False).wait_recv()
        for cp in sends:
            cp.wait_send()
        for cp in mine:
            cp.wait()

    return pl.pallas_call(
        body, name=name,
        out_shape=[jax.ShapeDtypeStruct(t.shape, t.dtype) for t in hs],
        in_specs=[pl.BlockSpec(memory_space=pl.ANY)] * na, out_specs=[pl.BlockSpec(memory_space=pl.ANY)] * na,
        scratch_shapes=[pltpu.SemaphoreType.DMA((3 * na,)), pltpu.SemaphoreType.DMA((3 * na,)),
                        pltpu.SemaphoreType.DMA((na,))],
    )(*hs)


def _rows_of(shape):
    return -(-int(np.prod(shape)) // PACK_W)


def _pack(arrs, dtype, lead=0, total_rows=None):
    pieces = []
    for a in arrs:
        f = a.reshape(a.shape[:lead] + (-1,)).astype(dtype)
        pad = (-f.shape[-1]) % PACK_W
        if pad:
            f = jnp.pad(f, [(0, 0)] * lead + [(0, pad)])
        pieces.append(f.reshape(a.shape[:lead] + (-1, PACK_W)))
    buf = jnp.concatenate(pieces, axis=lead)
    if total_rows is not None and buf.shape[lead] < total_rows:
        buf = jnp.pad(buf, [(0, 0)] * lead + [(0, total_rows - buf.shape[lead]), (0, 0)])
    return buf


def _unpack(buf, shapes, lead=0):
    out, r = [], 0
    for shp in shapes:
        n, rows = int(np.prod(shp)), _rows_of(shp)
        piece = buf[(slice(None),) * lead + (slice(r, r + rows),)]
        piece = piece.reshape(buf.shape[:lead] + (-1,))[..., :n]
        out.append(piece.reshape(buf.shape[:lead] + tuple(shp)))
        r += rows
    return out


def _pack_flat(arrs, total_rows):
    flat = jnp.concatenate([t.reshape(-1).astype(F32) for t in arrs])
    return jnp.pad(flat, (0, total_rows * PACK_W - flat.shape[0])).reshape(total_rows, PACK_W)


def _unpack_flat(buf, shapes):
    flat, out, off = buf.reshape(-1), [], 0
    for shp in shapes:
        n = int(np.prod(shp))
        out.append(flat[off:off + n].reshape(shp))
        off += n
    return out


def _to_full(parts):
    dep, r = parts.shape[1:3]
    return jnp.transpose(parts, (1, 0) + tuple(range(2, parts.ndim))).reshape((dep, N_DEV * r) + parts.shape[3:])


def _to_slabs(full):
    dep, r = full.shape[:2]
    t = full.reshape((dep, N_DEV, r // N_DEV) + full.shape[2:])
    return jnp.transpose(t, (1, 0) + tuple(range(2, t.ndim)))


def _ref_cols(parts, ro, wd):
    w, out = parts.shape[2], []
    for dev in range(N_DEV):
        lo, hi = max(ro, dev * w), min(ro + wd, (dev + 1) * w)
        if lo < hi:
            out.append(parts[dev][:, lo - dev * w:hi - dev * w])
    return out


def _w_in_to_layout(parts, seg, rseg, nh2):
    D = parts.shape[1]
    cols, off = [], 0
    names = sorted([k for k in seg if not k.startswith('_')], key=lambda k: seg[k][0])
    for nm in names:
        o, wd = seg[nm]
        if o > off:
            cols.append(jnp.zeros((D, o - off), parts.dtype))
        if nm == 'dadb':
            cols += _ref_cols(parts, rseg['da'][0], nh2) + _ref_cols(parts, rseg['db'][0], nh2)
            cols.append(jnp.zeros((D, wd - 2 * nh2), parts.dtype))
        else:
            cols += _ref_cols(parts, rseg[nm][0], wd)
        off = o + wd
    if seg['_total'] > off:
        cols.append(jnp.zeros((D, seg['_total'] - off), parts.dtype))
    return jnp.concatenate(cols, axis=1)


def _w_in_slabs(dw, seg, rseg, nh2):
    w = rseg['_total'] // N_DEV
    ref = []
    for nm in sorted([k for k in rseg if not k.startswith('_')], key=lambda k: rseg[k][0]):
        lo = {'da': seg['dadb'][0], 'db': seg['dadb'][0] + nh2}.get(nm)
        ref.append((rseg[nm][0], rseg[nm][1], seg[nm][0] if lo is None else lo))
    slabs = []
    for dev in range(N_DEV):
        cols = []
        for ro, wd, lo in ref:
            a, b = max(ro, dev * w), min(ro + wd, (dev + 1) * w)
            if a < b:
                cols.append(dw[:, lo + a - ro:lo + b - ro])
        slabs.append(jnp.concatenate(cols, axis=1))
    return jnp.stack(slabs, axis=0)


def _assemble_dh(pieces, seg, L):
    cols, off = [], 0
    for nm in sorted(pieces, key=lambda k: seg[k][0]):
        o = seg[nm][0]
        if o > off:
            cols.append(jnp.zeros((L, o - off), F32))
        cols.append(pieces[nm])
        off = o + pieces[nm].shape[1]
    if seg['_total'] > off:
        cols.append(jnp.zeros((L, seg['_total'] - off), F32))
    return jnp.concatenate(cols, axis=1)


def _lane_pad(v):
    v = v.reshape(1, -1)
    return jnp.pad(v, ((0, 0), (0, LANE - v.shape[1])))


def _rope_tables(L, c):
    rows = L // c['GRID_W']
    row = jnp.repeat(jnp.arange(rows), c['GRID_W']).astype(F32)
    col = jnp.tile(jnp.arange(c['GRID_W']), rows).astype(F32)
    axis_dim = c['AD'] // 2
    freqs = c['ROPE_THETA'] ** (-jnp.arange(0, axis_dim, 2, dtype=F32) / axis_dim)
    ang = jnp.concatenate([row[:, None] * freqs, col[:, None] * freqs], axis=-1)
    cosf = jnp.repeat(jnp.cos(ang), 2, axis=1)
    sn = jnp.sin(ang)
    sins = jnp.stack([-sn, sn], axis=-1).reshape(L, c['AD'])
    idx = np.arange(c['AD'])
    perm = np.zeros((c['AD'], c['AD']), np.float32)
    perm[idx, idx ^ 1] = 1.0
    return cosf, sins, jnp.asarray(perm)


def _s5_dir_params(a, l, dr):
    return (a['ssm_a_re'][l, dr], a['ssm_a_im'][l, dr], a['ssm_log_step'][l, dr], a['ssm_b_re'][l, dr],
            a['ssm_b_im'][l, dr], a['ssm_c_re'][l, dr], a['ssm_c_im'][l, dr])


def _layer_fwd(x, mem, l, wt, a, rope, c, d, seg):
    L, D = x.shape
    SW, DW, AW, AKW, MW, H = d['SW'], d['DW'], d['AW'], d['AKW'], d['MW'], d['DNH']
    cb = lambda nm: seg[nm][0] // seg[nm][1]
    sv = {'x': x}
    p = f"l{l}_"
    sv['g_norm'] = a['norm_g'][l][None, :]
    xn, = _rowwise(_f_norm, [(x, D, 0)], [sv['g_norm']], [(D, BF16)], tm=256, name=p + "norm")
    h = _mm(xn, wt['wp'], name=p + "in_proj", tm=1024, tn=1536, tk=2048)
    sv['xn'], sv['h'] = xn, h

    ysum, sv['s5'] = None, []
    for dr in range(2):
        wb, wc, lr, li = _s5_prep(*_s5_dir_params(a, l, dr), d)
        wb16, wc16 = wb.astype(BF16), wc.astype(BF16)
        lt = _s5_tables(lr, li, bool(dr), False)
        ysum, cin = _s5_fwd(h, cb('u_a'), wb16, wc16, lt, rev=bool(dr), acc=ysum, tb=TILES['s5_t'],
                            name=p + f"s5_fwd{dr}", d=d)
        sv['s5'].append((wb16, wc16, lt, _s5_tables(lr, li, not bool(dr), True), cin))
    sv['ysum'] = ysum
    sv['s5_par'] = [a['ssm_d'][l][None, :], wt['w_glu'], a['ssm_b_glu'][l][None, :]]
    sv['s5_rows'] = [(ysum, SW, 0), (h, SW, cb('u_a')), (h, SW, cb('z_a'))]
    y_a, = _rowwise(_f_s5tail, sv['s5_rows'], sv['s5_par'], [(SW, F32)], tm=256, name=p + "s5_tail")

    act = _conv_fwd(h, cb('dq'), wt['conv'], tm=256, name=p + "dn_conv", d=d)
    sv['act'] = act
    sv['dn_par'] = [_lane_pad(a['dn_a_log'][l]), _lane_pad(a['dn_dt_bias'][l])]
    sv['dn_rows'] = [(act, DW, 0), (act, DW, 1), (h, LANE, seg['dadb'][0] // LANE)]
    dn_out = _rowwise(_make_f_dnpre(H, d['DNK'], c['CHUNK']), sv['dn_rows'], sv['dn_par'], [(DW, F32)] * 8,
                      tm=256, name=p + "dn_pre")
    qn, kn = dn_out[:2]
    sv['qn'], sv['kn'], sv['gates'] = qn, kn, [dn_out[2:5], dn_out[5:8]]
    o_dn, sv['dn_state'] = None, []
    for dr in range(2):
        o_dn, ss = _delta_fwd(qn, kn, act, sv['gates'][dr], vcb=2, rev=bool(dr), acc=o_dn,
                              name=p + f"dn_fwd{dr}", d=d)
        sv['dn_state'].append(ss)
    sv['dnpost_rows'] = [(o_dn, DW, 0), (h, DW, cb('z_b'))]
    sv['dnpost_par'] = [a['dn_norm_g'][l][None, :]]
    y_b, = _rowwise(_make_f_dnpost(d['DNK']), sv['dnpost_rows'], sv['dnpost_par'], [(DW, F32)], tm=256,
                    name=p + "dn_post")

    cosf, sins, perm = rope
    sv['att_par'] = [perm, a['attn_q_norm'][l][None, :], a['attn_k_norm'][l][None, :]]
    qh, kh, vh = _rowwise(_make_f_attpre(d['AD'], True),
                          [(h, AW, cb('aq')), (h, AKW, cb('ak')), (h, AKW, cb('av')), (cosf, d['AD'], 0),
                           (sins, d['AD'], 0)], sv['att_par'], [(AW, BF16), (AKW, BF16), (AKW, BF16)],
                          tm=256, name=p + "att_pre")
    o_att, lse = _attn_fwd(qh, kh, vh, tq=TILES['att_q'], tk=TILES['att_k'], name=p + "att_fwd", d=d)
    sv['qh'], sv['kh'], sv['vh'], sv['o_att'], sv['lse'] = qh, kh, vh, o_att, lse
    y_c, = _rowwise(_f_gate, [(o_att, AW, 0), (h, AW, cb('z_c'))], [], [(AW, F32)], tm=256, name=p + "att_post")

    sv['g_mem'] = a['mem_norm_g'][l][None, :]
    memn, = _rowwise(_f_norm, [(mem, D, 0)], [sv['g_mem']], [(D, BF16)], tm=256, name=p + "mem_norm")
    kv = _mm(memn, wt['w_mem_kv'], name=p + "mem_kv")
    sv['memn'], sv['kv'] = memn, kv
    y_m, = _rowwise(_make_f_mem(d['MH'], d['MD']), [(h, MW, cb('mq')), (h, MW, cb('z_m'))], [kv], [(MW, F32)],
                    tm=256, name=p + "mem_attn")

    ys = [y_a, y_b, y_c, y_m]
    ps = [_mm(y, wb_, name=p + f"branch_proj{i}", out_dtype=BF16)
          for i, (y, wb_) in enumerate(zip(ys, wt['w_branch']))]
    gcb = seg['gates'][0] // D
    sv['merge_rows'] = [(pp, D, 0) for pp in ps] + [(h, D, gcb + i) for i in range(4)]
    merged, = _rowwise(_f_merge, sv['merge_rows'], [], [(D, BF16)], tm=128, name=p + "merge")
    sv['ys'], sv['merged'] = ys, merged
    return _mm(merged, wt['w_out'], add=x, name=p + "out_proj"), sv


def _layer_bwd(dx, mem, l, wt, a, rope, sv, c, d, seg):
    L, D = dx.shape
    SW, DW, AW, AKW, MW, H = d['SW'], d['DW'], d['AW'], d['AKW'], d['MW'], d['DNH']
    cb = lambda nm: seg[nm][0] // seg[nm][1]
    p = f"l{l}_"
    h = sv['h']
    gr = {}
    dmerged = _mm(dx, wt['w_out'], tb=True, name=p + "d_merged")
    gr['w_out'] = _mm(sv['merged'], dx, ta=True, name=p + "dw_out")
    dmr, _ = _rowwise_bwd(_f_merge, sv['merge_rows'], [], [[(dmerged, D, 0)]], [True] * 8, [], tm=128,
                          name=p + "merge_bwd", row_grad_dtype=BF16)
    dps, dgates = dmr[:4], dmr[4:]
    dys = [_mm(dp, wb_, tb=True, name=p + f"d_branch{i}") for i, (dp, wb_) in enumerate(zip(dps, wt['w_branch']))]
    gr['w_branch'] = jnp.concatenate(
        [_mm(y, dp, ta=True, name=p + f"dw_branch{i}") for i, (y, dp) in enumerate(zip(sv['ys'], dps))], axis=0)

    (dmq, dzm), (dkv,) = _rowwise_bwd(_make_f_mem(d['MH'], d['MD']), [(h, MW, cb('mq')), (h, MW, cb('z_m'))],
                                      [sv['kv']], [[(dys[3], MW, 0)]], [True, True], [True], tm=256,
                                      name=p + "mem_attn_bwd")
    gr['w_mem_kv'] = _mm(sv['memn'], dkv, ta=True, name=p + "dw_mem_kv")
    dmemn = _mm(dkv, wt['w_mem_kv'], tb=True, name=p + "d_memn")
    _, (dg_mem,) = _rowwise_bwd(_f_norm, [(mem, D, 0)], [sv['g_mem']], [[(dmemn, D, 0)]], [False], [True], tm=256,
                                name=p + "mem_norm_bwd")
    gr['mem_norm_g'] = dg_mem[0]

    (do_att, dzc), _ = _rowwise_bwd(_f_gate, [(sv['o_att'], AW, 0), (h, AW, cb('z_c'))], [], [[(dys[2], AW, 0)]],
                                    [True, True], [], tm=256, name=p + "att_post_bwd")
    delta, = _rowwise(_make_f_delta(d['AD']), [(do_att, AW, 0), (sv['o_att'], AW, 0)], [], [(AW, F32)], tm=256,
                      name=p + "att_delta")
    att_in = (sv['qh'], sv['kh'], sv['vh'], do_att, sv['lse'], delta)
    dqh, dkh, dvh = _attn_bwd(*att_in, tq=TILES['att_q'], tk=TILES['att_k'], name=p + "att_bwd", d=d)
    cosf, sins, _ = rope
    (daq, dak), (dqg, dkg) = _rowwise_bwd(
        _make_f_attpre(d['AD'], False),
        [(h, AW, cb('aq')), (h, AKW, cb('ak')), (cosf, d['AD'], 0), (sins, d['AD'], 0)], sv['att_par'],
        [[(dqh, AW, 0)], [(dkh, AKW, 0)]], [True, True, False, False], [False, True, True], tm=256,
        name=p + "att_pre_bwd")
    gr['attn_q_norm'], gr['attn_k_norm'] = dqg[0], dkg[0]

    (do_dn, dzb), (dng,) = _rowwise_bwd(_make_f_dnpost(d['DNK']), sv['dnpost_rows'], sv['dnpost_par'],
                                        [[(dys[1], DW, 0)]], [True, True], [True], tm=256, name=p + "dn_post_bwd")
    gr['dn_norm_g'] = dng[0]
    accs, dn_dgates = None, []
    for dr in range(2):
        res = _delta_bwd(sv['qn'], sv['kn'], sv['act'], sv['gates'][dr], sv['dn_state'][dr], do_dn, vcb=2,
                         rev=bool(dr), accs=accs, name=p + f"dn_bwd{dr}", d=d)
        accs = res[:3]
        dn_dgates += res[3:]
    dqn, dkn, dvc = accs
    (dqc, dkc, ddadb), (dalog, ddtb) = _rowwise_bwd(
        _make_f_dnpre(H, d['DNK'], c['CHUNK']), sv['dn_rows'], sv['dn_par'],
        [[(t, DW, 0)] for t in [dqn, dkn] + dn_dgates], [True] * 3, [True, True], tm=256, name=p + "dn_pre_bwd")
    gr['dn_a_log'] = dalog[0, :2 * H].reshape(2, H)
    gr['dn_dt_bias'] = ddtb[0, :2 * H].reshape(2, H)
    dconv_x, dconv_w = _conv_bwd(h, cb('dq'), wt['conv'], jnp.concatenate([dqc, dkc, dvc], axis=1), tm=256,
                                 name=p + "dn_conv_bwd", d=d)
    gr['dn_conv'] = jnp.transpose(dconv_w[:, :c['CONV'], :], (0, 2, 1)).reshape(3 * DW, c['CONV'])

    (dysum, du, dza), (dd, dwglu, dbglu) = _rowwise_bwd(_f_s5tail, sv['s5_rows'], sv['s5_par'], [[(dys[0], SW, 0)]],
                                                        [True] * 3, [True] * 3, tm=256, name=p + "s5_tail_bwd")
    gr['ssm_d'], gr['ssm_w_glu'], gr['ssm_b_glu'] = dd[0], dwglu, dbglu[0]
    s5g = []
    for dr in range(2):
        wb16, wc16, lt, lt_adj, cin = sv['s5'][dr]
        du, dwb, dwc, dlam = _s5_bwd(h, cb('u_a'), dysum, cin, wb16, wc16, lt, lt_adj, rev=bool(dr), acc=du,
                                     tb=TILES['s5_t'],
                                     name=p + f"s5_bwd{dr}", d=d)
        dl = jnp.sum(dlam, axis=0).reshape(d['NB'], 2, d['BS'])
        _, prep_vjp = jax.vjp(lambda *pp: _s5_prep(*pp, d), *_s5_dir_params(a, l, dr))
        s5g.append(prep_vjp((dwb, dwc, dl[:, 0], dl[:, 1])))
    for i, nm in enumerate(['ssm_a_re', 'ssm_a_im', 'ssm_log_step', 'ssm_b_re', 'ssm_b_im', 'ssm_c_re', 'ssm_c_im']):
        gr[nm] = jnp.stack([s5g[0][i], s5g[1][i]], axis=0)

    dh = _assemble_dh({'u_a': du, 'z_a': dza, 'dq': dconv_x, 'z_b': dzb, 'ak': dak, 'av': dvh, 'aq': daq,
                       'z_c': dzc, 'mq': dmq, 'z_m': dzm, 'gates': jnp.concatenate(dgates, axis=1),
                       'dadb': ddadb}, seg, L).astype(BF16)
    gr['wp'] = _mm(sv['xn'], dh, ta=True, name=p + "dw_in", tm=1024, tn=1536, tk=2048)
    dxn = _mm(dh, wt['wp'], tb=True, name=p + "d_xn", tm=1024, tn=1024, tk=1536)
    (dx_in,), (dg_norm,) = _rowwise_bwd(_f_norm, [(sv['x'], D, 0)], [sv['g_norm']], [[(dxn, D, 0)]], [True], [True],
                                        tm=256, name=p + "norm_bwd", accs={0: (dx, D, 0)})
    gr['norm_g'] = dg_norm[0]
    return dx_in, gr


_ARG_NAMES = (['x', 'mem'] + WEIGHTS + ['loss_target'] + ['m_' + w for w in WEIGHTS] + ['v_' + w for w in WEIGHTS])


def kernel(x, mem, norm_g, w_in, ssm_a_re, ssm_a_im, ssm_log_step, ssm_b_re, ssm_b_im, ssm_c_re, ssm_c_im,
           ssm_d, ssm_w_glu, ssm_b_glu, dn_conv, dn_a_log, dn_dt_bias, dn_norm_g, attn_q_norm, attn_k_norm,
           mem_norm_g, w_mem_kv, w_branch, w_out, final_norm_g, loss_target, m_norm_g, m_w_in, m_ssm_a_re,
           m_ssm_a_im, m_ssm_log_step, m_ssm_b_re, m_ssm_b_im, m_ssm_c_re, m_ssm_c_im, m_ssm_d, m_ssm_w_glu,
           m_ssm_b_glu, m_dn_conv, m_dn_a_log, m_dn_dt_bias, m_dn_norm_g, m_attn_q_norm, m_attn_k_norm,
           m_mem_norm_g, m_w_mem_kv, m_w_branch, m_w_out, m_final_norm_g, v_norm_g, v_w_in, v_ssm_a_re,
           v_ssm_a_im, v_ssm_log_step, v_ssm_b_re, v_ssm_b_im, v_ssm_c_re, v_ssm_c_im, v_ssm_d, v_ssm_w_glu,
           v_ssm_b_glu, v_dn_conv, v_dn_a_log, v_dn_dt_bias, v_dn_norm_g, v_attn_q_norm, v_attn_k_norm,
           v_mem_norm_g, v_w_mem_kv, v_w_branch, v_w_out, v_final_norm_g):
    given = locals()
    return _train_step({n: given[n] for n in _ARG_NAMES})


def _train_step(a):
    c = CFG
    d = _dims(c)
    seg, rseg = _layout(c)
    depth, nh2 = c['DEPTH'], 2 * c['DNH']
    x, mem, tgt = a['x'][0], a['mem'][0], a['loss_target'][0]
    L, D = x.shape

    packed = [n for n in SHARDED if n != 'w_in']
    shard_shapes = [a[n].shape for n in packed]
    rw = _round_up(sum(_rows_of(s) for s in shard_shapes), LANE)
    win_shape = a['w_in'].shape
    wcols = win_shape[2]
    g_win, gathered = _all_gather([a['w_in'].astype(BF16).reshape(depth * D, wcols),
                                   _pack([a[n] for n in packed], BF16, total_rows=rw)], name="weights_all_gather")
    full = {n: _to_full(p_) for n, p_ in zip(packed, _unpack(gathered, shard_shapes, lead=1))}
    offs = np.cumsum([0, d['SW'], d['DW'], d['AW'], d['MW']])
    wts = []
    for l in range(depth):
        conv = jnp.transpose(full['dn_conv'][l].astype(F32).reshape(3, d['DW'], c['CONV']), (0, 2, 1))
        wts.append(dict(
            wp=_w_in_to_layout(g_win[:, l * D:(l + 1) * D], seg, rseg, nh2),
            w_branch=[full['w_branch'][l, offs[i]:offs[i + 1]] for i in range(4)],
            w_out=full['w_out'][l], w_mem_kv=full['w_mem_kv'][l], w_glu=full['ssm_w_glu'][l].astype(F32),
            conv=jnp.pad(conv, ((0, 0), (0, 8 - c['CONV']), (0, 0)))))
    rope = _rope_tables(L, c)

    saved = []
    for l in range(depth):
        x, sv = _layer_fwd(x, mem, l, wts[l], a, rope, c, d, seg)
        saved.append(sv)
    loss_part, dx, dg_final = _loss_grad(x, a['final_norm_g'][None, :], tgt, tm=256, name="final_norm_loss")
    grads = [None] * depth
    for l in reversed(range(depth)):
        dx, grads[l] = _layer_bwd(dx, mem, l, wts[l], a, rope, saved[l], c, d, seg)

    gfull = {n: jnp.stack([grads[l][n] for l in range(depth)], axis=0) for n in WEIGHTS
             if n not in ('w_in', 'final_norm_g')}
    gfull['final_norm_g'] = dg_final[0]

    win_slabs = jnp.concatenate([_w_in_slabs(grads[l]['wp'], seg, rseg, nh2) for l in range(depth)], axis=1)
    small_shapes = [a[n].shape for n in SMALL] + [(1,)]
    rs = _round_up(_rows_of((sum(int(np.prod(s)) for s in small_shapes),)), LANE)
    g_shard = _pack([_to_slabs(gfull[n]) for n in packed], F32, lead=1, total_rows=rw)
    g_small = _pack_flat([gfull[n] for n in SMALL] + [loss_part[0, :1]], rs)
    slabs = [win_slabs, g_shard, jnp.broadcast_to(g_small[None], (N_DEV,) + g_small.shape)]
    core = lax.axis_index("c").astype(jnp.int32).reshape(1)
    from_sibling = _pair_exchange(slabs, name="grads_pair_exchange")
    pair_sums = [_pair_sum(g, r, core, out_dtype=dt, tr=256, name=f"grads_pair_sum{i}")
                 for i, (g, r, dt) in enumerate(zip(slabs, from_sibling, (BF16, BF16, F32)))]
    recv = _chip_exchange(pair_sums, name="grads_chip_exchange")
    g_win_sum = _sum_slots(recv[0], tr=256, name="w_in_grad_sum")
    gsum = jnp.concatenate([_sum_slots(recv[1], tr=256, name="shard_grad_sum"),
                            _sum_slots(recv[2], tr=256, name="small_grad_sum")], axis=0)
    flat = lambda t: t.reshape(depth * D, wcols)
    d_win, m_win, v_win = _adamw(flat(a['w_in']), g_win_sum, flat(a['m_w_in']), flat(a['v_w_in']), tr=256,
                                 name="w_in_adamw")
    win_out = [t.reshape(win_shape) for t in (g_win_sum, d_win, m_win, v_win)]

    def local_pack(prefix):
        zero = jnp.zeros((1,), F32)
        return jnp.concatenate([_pack([a[prefix + n] for n in packed], F32, total_rows=rw),
                                _pack_flat([a[prefix + n] for n in SMALL] + [zero], rs)], axis=0)

    delta, new_m, new_v = _adamw(local_pack(''), gsum, local_pack('m_'), local_pack('v_'), tr=256, name="adamw")

    def split(buf):
        vals = dict(zip(packed, _unpack(buf[:rw], shard_shapes)))
        small = _unpack_flat(buf[rw:], small_shapes)
        vals.update(zip(SMALL, small[:-1]))
        return vals, small[-1]

    _, loss = split(gsum)
    outs = [loss.reshape(()), dx[None]]
    for i, buf in enumerate((gsum, delta, new_m, new_v)):
        vals, _ = split(buf)
        vals['w_in'] = win_out[i]
        outs += [vals[n] for n in WEIGHTS]
    return tuple(outs)
```

```python
import functools
import math

import numpy as np
import jax
import jax.numpy as jnp
from jax import lax
from jax.experimental import pallas as pl
from jax.experimental.pallas import tpu as pltpu

F32 = jnp.float32
BF16 = jnp.bfloat16
HI = lax.Precision.HIGHEST
EPS = 1e-6
LANE = 128
SUBLANE = 8
VMEM_LIMIT = 56 * 1024 * 1024
N_DEV = 8
PACK_W = 1024

ADAM_LR, ADAM_B1, ADAM_B2, ADAM_EPS, ADAM_WD, ADAM_STEP = 0.001, 0.9, 0.999, 1e-08, 0.01, 10

CFG = dict(D=2048, L=8192, GRID_W=64, NMEM=256, DEPTH=2,
           SG=48, SP=16, SN=64,
           DNH=6, DNK=128, CONV=5, CHUNK=64,
           AH=8, AKV=2, AD=128, ROPE_THETA=10000.0,
           MH=4, MD=128)

TILES = dict(att_q=2048, att_k=2048, s5_t=512)

WEIGHTS = ['norm_g', 'w_in', 'ssm_a_re', 'ssm_a_im', 'ssm_log_step', 'ssm_b_re', 'ssm_b_im', 'ssm_c_re',
           'ssm_c_im', 'ssm_d', 'ssm_w_glu', 'ssm_b_glu', 'dn_conv', 'dn_a_log', 'dn_dt_bias', 'dn_norm_g',
           'attn_q_norm', 'attn_k_norm', 'mem_norm_g', 'w_mem_kv', 'w_branch', 'w_out', 'final_norm_g']
SHARDED = ['w_in', 'w_branch', 'w_out', 'w_mem_kv', 'ssm_w_glu', 'dn_conv']
SMALL = [w for w in WEIGHTS if w not in SHARDED]


def _dims(c):
    d = dict(c)
    d['SW'] = c['SG'] * c['SP']
    d['NB'] = d['SW'] // LANE
    d['GPB'] = LANE // c['SP']
    d['BS'] = d['GPB'] * c['SN']
    d['DW'] = c['DNH'] * c['DNK']
    d['AW'] = c['AH'] * c['AD']
    d['AKW'] = c['AKV'] * c['AD']
    d['MW'] = c['MH'] * c['MD']
    d['BT'] = d['SW'] + d['DW'] + d['AW'] + d['MW']
    return d


def _round_up(a, b):
    return (a + b - 1) // b * b


def _layout(c):
    d = _dims(c)
    D, SW, DW, AW, AKW, MW = d['D'], d['SW'], d['DW'], d['AW'], d['AKW'], d['MW']
    order = [('u_a', SW, SW), ('z_a', SW, SW), ('dq', DW, DW), ('dk', DW, DW), ('dv', DW, DW), ('z_b', DW, DW),
             ('ak', AKW, AKW), ('av', AKW, AKW), ('aq', AW, AW), ('z_c', AW, AW), ('mq', MW, MW), ('z_m', MW, MW),
             ('gates', 4 * D, D), ('dadb', LANE, LANE)]
    off, seg = 0, {}
    for name, w, al in order:
        off = _round_up(off, al)
        seg[name] = (off, w)
        off += w
    seg['_total'] = _round_up(off, 512)
    ref_order = [('u_a', SW), ('z_a', SW), ('dq', DW), ('dk', DW), ('dv', DW), ('da', 2 * d['DNH']),
                 ('db', 2 * d['DNH']), ('z_b', DW), ('aq', AW), ('ak', AKW), ('av', AKW), ('z_c', AW),
                 ('mq', MW), ('z_m', MW), ('gates', 4 * D)]
    roff, rseg = 0, {}
    for name, w in ref_order:
        rseg[name] = (roff, w)
        roff += w
    rseg['_total'] = roff
    return seg, rseg


def _cparams(sem):
    return pltpu.CompilerParams(dimension_semantics=sem, vmem_limit_bytes=VMEM_LIMIT)


def _pick(t, n):
    if n <= t:
        return n
    for align in (LANE, 2 * SUBLANE):
        for cand in range(t - t % align, 0, -align):
            if n % cand == 0:
                return cand
    return n


def _mm(a, b, *, name, ta=False, tb=False, add=None, out_dtype=F32, tm=1024, tn=1024, tk=1024):
    M, K = (a.shape[1], a.shape[0]) if ta else a.shape
    N = b.shape[0] if tb else b.shape[1]
    assert (b.shape[1] if tb else b.shape[0]) == K
    tm, tn, tk = _pick(tm, M), _pick(tn, N), _pick(tk, K)
    nk = K // tk
    dn = (((0 if ta else 1,), (1 if tb else 0,)), ((), ()))
    has_add = add is not None

    def body(*refs):
        if has_add:
            a_ref, b_ref, add_ref, o_ref, acc = refs
        else:
            a_ref, b_ref, o_ref, acc = refs
        k = pl.program_id(2)

        @pl.when(k == 0)
        def _():
            acc[...] = jnp.zeros_like(acc)

        acc[...] += lax.dot_general(a_ref[...].astype(BF16), b_ref[...].astype(BF16), dn,
                                    preferred_element_type=F32)

        @pl.when(k == nk - 1)
        def _():
            r = acc[...]
            if has_add:
                r = r + add_ref[...]
            o_ref[...] = r.astype(o_ref.dtype)

    a_spec = pl.BlockSpec((tk, tm), lambda i, j, k: (k, i)) if ta else pl.BlockSpec((tm, tk), lambda i, j, k: (i, k))
    b_spec = pl.BlockSpec((tn, tk), lambda i, j, k: (j, k)) if tb else pl.BlockSpec((tk, tn), lambda i, j, k: (k, j))
    in_specs = [a_spec, b_spec]
    args = [a, b]
    if has_add:
        in_specs.append(pl.BlockSpec((tm, tn), lambda i, j, k: (i, j)))
        args.append(add)
    return pl.pallas_call(
        body, name=name, grid=(M // tm, N // tn, nk),
        in_specs=in_specs, out_specs=pl.BlockSpec((tm, tn), lambda i, j, k: (i, j)),
        out_shape=jax.ShapeDtypeStruct((M, N), out_dtype),
        scratch_shapes=[pltpu.VMEM((tm, tn), F32)],
        compiler_params=_cparams(("parallel", "parallel", "arbitrary")),
    )(*args)


def _row_spec(tm, w, cb):
    return pl.BlockSpec((tm, w), lambda i, cb=cb: (i, cb))


def _rowwise(fn, rows, params, outs, *, tm, name):
    L = rows[0][0].shape[0]
    tm = _pick(tm, L)
    nr, npar = len(rows), len(params)

    def body(*refs):
        vals = [r[...] for r in refs[:nr + npar]]
        res = fn(*vals)
        for o_ref, v in zip(refs[nr + npar:], res):
            o_ref[...] = v.astype(o_ref.dtype)

    in_specs = [_row_spec(tm, w, cb) for (_, w, cb) in rows]
    in_specs += [pl.BlockSpec(p.shape, lambda i: (0, 0)) for p in params]
    res = pl.pallas_call(
        body, name=name, grid=(L // tm,), in_specs=in_specs,
        out_specs=[pl.BlockSpec((tm, w), lambda i: (i, 0)) for (w, _) in outs],
        out_shape=[jax.ShapeDtypeStruct((L, w), dt) for (w, dt) in outs],
        compiler_params=_cparams(("parallel",)),
    )(*[r[0] for r in rows], *params)
    return list(res)


def _rowwise_bwd(fn, rows, params, cts, drows, dparams, *, tm, name, accs=None, row_grad_dtype=F32):
    L = rows[0][0].shape[0]
    tm = _pick(tm, L)
    nr, npar = len(rows), len(params)
    accs = accs or {}
    ct_flat = [c for grp in cts for c in grp]
    ct_sizes = [len(grp) for grp in cts]
    acc_keys = sorted(accs)
    d_r = [i for i in range(nr) if drows[i]]
    d_p = [i for i in range(npar) if dparams[i]]
    n_in = nr + npar + len(ct_flat) + len(acc_keys)

    def body(*refs):
        vals = [r[...] for r in refs[:nr + npar]]
        ct_refs = refs[nr + npar:nr + npar + len(ct_flat)]
        acc_refs = refs[nr + npar + len(ct_flat):n_in]
        o_refs = refs[n_in:]
        ct_vals, pos = [], 0
        for n in ct_sizes:
            v = ct_refs[pos][...].astype(F32)
            for r in ct_refs[pos + 1:pos + n]:
                v = v + r[...].astype(F32)
            ct_vals.append(v)
            pos += n
        diff_idx = d_r + [nr + i for i in d_p]

        def g(*dv):
            full = list(vals)
            for i, v in zip(diff_idx, dv):
                full[i] = v
            return tuple(o.astype(F32) for o in fn(*full))

        _, vjp = jax.vjp(g, *[vals[i] for i in diff_idx])
        grads = vjp(tuple(ct_vals))
        for n, i in enumerate(d_r):
            gv = grads[n].astype(F32)
            if i in accs:
                gv = gv + acc_refs[acc_keys.index(i)][...]
            o_refs[n][...] = gv.astype(o_refs[n].dtype)
        step = pl.program_id(0)
        for n, i in enumerate(d_p):
            o_ref = o_refs[len(d_r) + n]

            @pl.when(step == 0)
            def _(o_ref=o_ref):
                o_ref[...] = jnp.zeros_like(o_ref)

            o_ref[...] += grads[len(d_r) + n].astype(F32)

    in_specs = [_row_spec(tm, w, cb) for (_, w, cb) in rows]
    in_specs += [pl.BlockSpec(p.shape, lambda i: (0, 0)) for p in params]
    in_specs += [_row_spec(tm, w, cb) for (_, w, cb) in ct_flat]
    in_specs += [_row_spec(tm, accs[k][1], accs[k][2]) for k in acc_keys]
    out_specs = [pl.BlockSpec((tm, rows[i][1]), lambda i_: (i_, 0)) for i in d_r]
    out_specs += [pl.BlockSpec(params[i].shape, lambda i_: (0, 0)) for i in d_p]
    out_shape = [jax.ShapeDtypeStruct((L, rows[i][1]), row_grad_dtype) for i in d_r]
    out_shape += [jax.ShapeDtypeStruct(params[i].shape, F32) for i in d_p]
    res = pl.pallas_call(
        body, name=name, grid=(L // tm,), in_specs=in_specs, out_specs=out_specs, out_shape=out_shape,
        compiler_params=_cparams(("arbitrary",)),
    )(*[r[0] for r in rows], *params, *[c[0] for c in ct_flat], *[accs[k][0] for k in acc_keys])
    res = list(res)
    return res[:len(d_r)], res[len(d_r):]


def _silu(x):
    return x * jax.nn.sigmoid(x)


def _rms(x, g):
    return x * lax.rsqrt(jnp.mean(x * x, axis=-1, keepdims=True) + EPS) * g


def _softplus(x):
    return jnp.maximum(x, 0.0) + jnp.log1p(jnp.exp(-jnp.abs(x)))


def _heads(x, hd):
    return [x[:, i * hd:(i + 1) * hd] for i in range(x.shape[1] // hd)]


def _f_norm(x, g):
    return (_rms(x, g),)


def _f_s5tail(ys, u, z, d, wglu, bglu):
    y = jax.nn.gelu(ys + d * u)
    gate = jax.nn.sigmoid(jnp.dot(y.astype(BF16), wglu.astype(BF16), preferred_element_type=F32) + bglu)
    return (y * gate * _silu(z),)


def _make_f_dnpre(nh, hd, chunk):
    def f(qc, kc, dadb, alog, dtb):
        tm = qc.shape[0]
        qn = [q * lax.rsqrt(jnp.sum(q * q, axis=-1, keepdims=True) + EPS) * (hd ** -0.5) for q in _heads(qc, hd)]
        kn = [k * lax.rsqrt(jnp.sum(k * k, axis=-1, keepdims=True) + EPS) for k in _heads(kc, hd)]
        g = -jnp.exp(alog) * _softplus(dadb + dtb)
        beta = jax.nn.sigmoid(dadb)
        ii = lax.broadcasted_iota(jnp.int32, (tm, tm), 0)
        jj = lax.broadcasted_iota(jnp.int32, (tm, tm), 1)
        same = (ii // chunk) == (jj // chunk)
        outs = [jnp.concatenate(qn, axis=1), jnp.concatenate(kn, axis=1)]
        gt = jnp.dot(same.astype(F32), g, precision=HI, preferred_element_type=F32)
        for dr in range(2):
            tri = jnp.logical_and(same, (ii <= jj) if dr else (ii >= jj)).astype(F32)
            gc = jnp.dot(tri, g, precision=HI, preferred_element_type=F32)

            def spread(t, lane0):
                return jnp.concatenate([jnp.broadcast_to(t[:, lane0 + h:lane0 + h + 1], (tm, hd))
                                        for h in range(nh)], axis=1)

            outs += [spread(beta, 2 * nh + dr * nh), spread(gc, dr * nh), spread(gt, dr * nh)]
        return tuple(outs)
    return f


def _make_f_dnpost(hd):
    def f(o, z, ng):
        y = [_rms(oh, ng) for oh in _heads(o, hd)]
        return (jnp.concatenate(y, axis=1) * _silu(z),)
    return f


def _make_f_attpre(hd, with_v):
    def rope(x, g, cosf, sins, perm, scale):
        xn = _rms(x, g)
        xs = jnp.dot(xn, perm, precision=HI, preferred_element_type=F32)
        return (xn * cosf + xs * sins) * scale

    def f(aq, ak, *rest):
        if with_v:
            av, cosf, sins, perm, qg, kg = rest
        else:
            cosf, sins, perm, qg, kg = rest
        qh = jnp.concatenate([rope(x, qg, cosf, sins, perm, hd ** -0.5) for x in _heads(aq, hd)], axis=1)
        kh = jnp.concatenate([rope(x, kg, cosf, sins, perm, 1.0) for x in _heads(ak, hd)], axis=1)
        return (qh, kh, av) if with_v else (qh, kh)
    return f


def _f_gate(o, z):
    return (o * _silu(z),)


def _make_f_mem(nh, hd):
    def f(mq, z, kv):
        mw = nh * hd
        outs = []
        for h, q in enumerate(_heads(mq, hd)):
            k = kv[:, h * hd:(h + 1) * hd]
            v = kv[:, mw + h * hd:mw + (h + 1) * hd]
            s = lax.dot_general(q.astype(BF16), k.astype(BF16), (((1,), (1,)), ((), ())),
                                preferred_element_type=F32) * (hd ** -0.5)
            s = s - jnp.max(s, axis=-1, keepdims=True)
            p = jnp.exp(s)
            p = p / jnp.sum(p, axis=-1, keepdims=True)
            outs.append(jnp.dot(p.astype(BF16), v.astype(BF16), preferred_element_type=F32))
        return (jnp.concatenate(outs, axis=1) * _silu(z),)
    return f


def _f_merge(p0, p1, p2, p3, g0, g1, g2, g3):
    return (jax.nn.sigmoid(g0) * p0 + jax.nn.sigmoid(g1) * p1 + jax.nn.sigmoid(g2) * p2 + jax.nn.sigmoid(g3) * p3,)


def _make_f_delta(hd):
    def f(do, o):
        out = [jnp.broadcast_to(jnp.sum(a * b, axis=-1, keepdims=True), a.shape)
               for a, b in zip(_heads(do, hd), _heads(o, hd))]
        return (jnp.concatenate(out, axis=1),)
    return f


def _s5_prep(a_re, a_im, log_step, b_re, b_im, c_re, c_im, d):
    nb, gpb, sn, sp = d['NB'], d['GPB'], d['SN'], d['SP']
    step = jnp.exp(log_step)[:, None]
    mag = jnp.exp(a_re * step)
    lam_re = mag * jnp.cos(a_im * step)
    lam_im = mag * jnp.sin(a_im * step)
    den = a_re * a_re + a_im * a_im
    nr, ni = lam_re - 1.0, lam_im
    coef_re = (nr * a_re + ni * a_im) / den
    coef_im = (ni * a_re - nr * a_im) / den
    bb_re = coef_re[..., None] * b_re - coef_im[..., None] * b_im
    bb_im = coef_re[..., None] * b_im + coef_im[..., None] * b_re
    eye = jnp.eye(gpb, dtype=F32)

    def blk_in(bb):
        t = bb.reshape(nb, gpb, sn, sp)
        return jnp.einsum("jgnp,gh->jgphn", t, eye).reshape(nb, gpb * sp, gpb * sn)

    def blk_out(cc):
        t = cc.reshape(nb, gpb, sp, sn)
        return jnp.einsum("jgpn,gh->jgnhp", t, eye).reshape(nb, gpb * sn, gpb * sp)

    wb = jnp.concatenate([blk_in(bb_re), blk_in(bb_im)], axis=2)
    wc = jnp.concatenate([blk_out(c_re), blk_out(-c_im)], axis=1)
    return wb, wc, lam_re.reshape(nb, gpb * sn), lam_im.reshape(nb, gpb * sn)


def _s5_tables(lam_re, lam_im, rev, conj):
    lr, li = lam_re, (-lam_im if conj else lam_im)

    def cmul(a, b):
        return a[0] * b[0] - a[1] * b[1], a[0] * b[1] + a[1] * b[0]

    pw = [(lr, li)]
    for _ in range(7):
        pw.append(cmul(pw[-1], (lr, li)))
    rows = jnp.arange(8)

    def bc(t, k):
        keep = (rows < 8 - k) if rev else (rows >= k)
        return t[:, None, :] * keep.astype(F32)[None, :, None]

    order = list(range(8))[::-1] if rev else list(range(8))
    pwr = jnp.stack([pw[i][0] for i in order], axis=1)
    pwi = jnp.stack([pw[i][1] for i in order], axis=1)
    tabs = [bc(pw[0][0], 1), bc(pw[0][1], 1), bc(pw[1][0], 2), bc(pw[1][1], 2), bc(pw[3][0], 4), bc(pw[3][1], 4),
            pwr, pwi]
    return jnp.stack(tabs, axis=1)


def _scan_group(xr, xi, lt_ref, j, cr, ci, rev):
    for lvl, k in enumerate((1, 2, 4)):
        l_r, l_i = lt_ref[j, 2 * lvl], lt_ref[j, 2 * lvl + 1]
        sh = (8 - k) if rev else k
        sr, si = pltpu.roll(xr, sh, 0), pltpu.roll(xi, sh, 0)
        xr, xi = xr + l_r * sr - l_i * si, xi + l_r * si + l_i * sr
    p_r, p_i = lt_ref[j, 6], lt_ref[j, 7]
    return xr + p_r * cr - p_i * ci, xi + p_r * ci + p_i * cr


def _last_row(x, rev):
    last = 0 if rev else 7
    return jnp.broadcast_to(x[last:last + 1, :], x.shape)


def _s5_fwd(hsrc, ucb, wb, wc, lt, *, rev, acc, tb, name, d):
    L, SW, NB, BS = hsrc.shape[0], d['SW'], d['NB'], d['BS']
    tb = _pick(tb, L)
    nblk, ngr = L // tb, tb // 8
    tix = (lambda b: nblk - 1 - b) if rev else (lambda b: b)
    has_acc = acc is not None

    def body(*refs):
        if has_acc:
            u_ref, wb_ref, wc_ref, lt_ref, acc_ref, y_ref, cin_ref, bu_s, car = refs
        else:
            u_ref, wb_ref, wc_ref, lt_ref, y_ref, cin_ref, bu_s, car = refs

        @pl.when(pl.program_id(0) == 0)
        def _():
            car[...] = jnp.zeros_like(car)

        cin_ref[...] = car[...]
        for j in range(NB):
            bu_s[:, j * 2 * BS:(j + 1) * 2 * BS] = jnp.dot(
                u_ref[:, j * LANE:(j + 1) * LANE].astype(BF16), wb_ref[j], preferred_element_type=F32)

        def grp(r, _):
            base = pl.multiple_of((ngr - 1 - r if rev else r) * 8, 8)
            for j in range(NB):
                c0 = j * 2 * BS
                xr, xi = _scan_group(bu_s[pl.ds(base, 8), c0:c0 + BS], bu_s[pl.ds(base, 8), c0 + BS:c0 + 2 * BS],
                                     lt_ref, j, car[:, c0:c0 + BS], car[:, c0 + BS:c0 + 2 * BS], rev)
                bu_s[pl.ds(base, 8), c0:c0 + BS] = xr
                bu_s[pl.ds(base, 8), c0 + BS:c0 + 2 * BS] = xi
                car[:, c0:c0 + BS] = _last_row(xr, rev)
                car[:, c0 + BS:c0 + 2 * BS] = _last_row(xi, rev)
            return 0

        lax.fori_loop(0, ngr, grp, 0)
        for j in range(NB):
            y = jnp.dot(bu_s[:, j * 2 * BS:(j + 1) * 2 * BS].astype(BF16), wc_ref[j], preferred_element_type=F32)
            if has_acc:
                y = y + acc_ref[:, j * LANE:(j + 1) * LANE]
            y_ref[:, j * LANE:(j + 1) * LANE] = y

    in_specs = [pl.BlockSpec((tb, SW), lambda b: (tix(b), ucb)),
                pl.BlockSpec(wb.shape, lambda b: (0, 0, 0)), pl.BlockSpec(wc.shape, lambda b: (0, 0, 0)),
                pl.BlockSpec(lt.shape, lambda b: (0, 0, 0, 0))]
    args = [hsrc, wb, wc, lt]
    if has_acc:
        in_specs.append(pl.BlockSpec((tb, SW), lambda b: (tix(b), 0)))
        args.append(acc)
    y, cin = pl.pallas_call(
        body, name=name, grid=(nblk,), in_specs=in_specs,
        out_specs=[pl.BlockSpec((tb, SW), lambda b: (tix(b), 0)),
                   pl.BlockSpec((8, NB * 2 * BS), lambda b: (tix(b), 0))],
        out_shape=[jax.ShapeDtypeStruct((L, SW), F32), jax.ShapeDtypeStruct((nblk * 8, NB * 2 * BS), F32)],
        scratch_shapes=[pltpu.VMEM((tb, NB * 2 * BS), F32), pltpu.VMEM((8, NB * 2 * BS), F32)],
        compiler_params=_cparams(("arbitrary",)),
    )(*args)
    return y, cin


def _s5_bwd(hsrc, ucb, dy, cin, wb, wc, lt, lt_adj, *, rev, acc, tb, name, d):
    L, SW, NB, BS = hsrc.shape[0], d['SW'], d['NB'], d['BS']
    tb = _pick(tb, L)
    nblk, ngr = L // tb, tb // 8
    arev = not rev
    tix = (lambda b: nblk - 1 - b) if arev else (lambda b: b)
    has_acc = acc is not None
    NT = (((1,), (1,)), ((), ()))
    TN = (((0,), (0,)), ((), ()))

    def body(*refs):
        if has_acc:
            (u_ref, dy_ref, cin_ref, wb_ref, wc_ref, lt_ref, la_ref, acc_ref,
             du_ref, dwb_ref, dwc_ref, dlam_ref, s_s, g_s, car, acar) = refs
        else:
            (u_ref, dy_ref, cin_ref, wb_ref, wc_ref, lt_ref, la_ref,
             du_ref, dwb_ref, dwc_ref, dlam_ref, s_s, g_s, car, acar) = refs

        @pl.when(pl.program_id(0) == 0)
        def _():
            acar[...] = jnp.zeros_like(acar)
            dwb_ref[...] = jnp.zeros_like(dwb_ref)
            dwc_ref[...] = jnp.zeros_like(dwc_ref)
            dlam_ref[...] = jnp.zeros_like(dlam_ref)

        car[...] = cin_ref[...]
        for j in range(NB):
            s_s[:, j * 2 * BS:(j + 1) * 2 * BS] = jnp.dot(
                u_ref[:, j * LANE:(j + 1) * LANE].astype(BF16), wb_ref[j], preferred_element_type=F32)
            g_s[:, j * 2 * BS:(j + 1) * 2 * BS] = lax.dot_general(
                dy_ref[:, j * LANE:(j + 1) * LANE].astype(BF16), wc_ref[j], NT, preferred_element_type=F32)

        def fgrp(r, _):
            base = pl.multiple_of((ngr - 1 - r if rev else r) * 8, 8)
            for j in range(NB):
                c0 = j * 2 * BS
                xr, xi = _scan_group(s_s[pl.ds(base, 8), c0:c0 + BS], s_s[pl.ds(base, 8), c0 + BS:c0 + 2 * BS],
                                     lt_ref, j, car[:, c0:c0 + BS], car[:, c0 + BS:c0 + 2 * BS], rev)
                s_s[pl.ds(base, 8), c0:c0 + BS] = xr
                s_s[pl.ds(base, 8), c0 + BS:c0 + 2 * BS] = xi
                car[:, c0:c0 + BS] = _last_row(xr, rev)
                car[:, c0 + BS:c0 + 2 * BS] = _last_row(xi, rev)
            return 0

        lax.fori_loop(0, ngr, fgrp, 0)

        row = lax.broadcasted_iota(jnp.int32, (8, BS), 0)

        def agrp(r, _):
            gi = ngr - 1 - r if arev else r
            base = pl.multiple_of(gi * 8, 8)
            pgi = gi + 1 if rev else gi - 1
            inside = jnp.logical_and(pgi >= 0, pgi < ngr)
            pbase = pl.multiple_of(jnp.clip(pgi, 0, ngr - 1) * 8, 8)
            for j in range(NB):
                c0 = j * 2 * BS
                ar, ai = _scan_group(g_s[pl.ds(base, 8), c0:c0 + BS], g_s[pl.ds(base, 8), c0 + BS:c0 + 2 * BS],
                                     la_ref, j, acar[:, c0:c0 + BS], acar[:, c0 + BS:c0 + 2 * BS], arev)
                g_s[pl.ds(base, 8), c0:c0 + BS] = ar
                g_s[pl.ds(base, 8), c0 + BS:c0 + 2 * BS] = ai
                acar[:, c0:c0 + BS] = _last_row(ar, arev)
                acar[:, c0 + BS:c0 + 2 * BS] = _last_row(ai, arev)
                sr, si = s_s[pl.ds(base, 8), c0:c0 + BS], s_s[pl.ds(base, 8), c0 + BS:c0 + 2 * BS]
                edge_r = jnp.where(inside, _last_row(s_s[pl.ds(pbase, 8), c0:c0 + BS], rev), cin_ref[:, c0:c0 + BS])
                edge_i = jnp.where(inside, _last_row(s_s[pl.ds(pbase, 8), c0 + BS:c0 + 2 * BS], rev),
                                   cin_ref[:, c0 + BS:c0 + 2 * BS])
                sh = 7 if rev else 1
                first = 7 if rev else 0
                pr = jnp.where(row == first, edge_r, pltpu.roll(sr, sh, 0))
                pi = jnp.where(row == first, edge_i, pltpu.roll(si, sh, 0))
                dlam_ref[:, c0:c0 + BS] += ar * pr + ai * pi
                dlam_ref[:, c0 + BS:c0 + 2 * BS] += ai * pr - ar * pi
            return 0

        lax.fori_loop(0, ngr, agrp, 0)
        for j in range(NB):
            a_j = g_s[:, j * 2 * BS:(j + 1) * 2 * BS].astype(BF16)
            u_j = u_ref[:, j * LANE:(j + 1) * LANE].astype(BF16)
            du = lax.dot_general(a_j, wb_ref[j], NT, preferred_element_type=F32)
            if has_acc:
                du = du + acc_ref[:, j * LANE:(j + 1) * LANE]
            du_ref[:, j * LANE:(j + 1) * LANE] = du
            dwb_ref[j] += lax.dot_general(u_j, a_j, TN, preferred_element_type=F32)
            dwc_ref[j] += lax.dot_general(s_s[:, j * 2 * BS:(j + 1) * 2 * BS].astype(BF16),
                                          dy_ref[:, j * LANE:(j + 1) * LANE].astype(BF16), TN,
                                          preferred_element_type=F32)

    W2 = NB * 2 * BS
    in_specs = [pl.BlockSpec((tb, SW), lambda b: (tix(b), ucb)), pl.BlockSpec((tb, SW), lambda b: (tix(b), 0)),
                pl.BlockSpec((8, W2), lambda b: (tix(b), 0)),
                pl.BlockSpec(wb.shape, lambda b: (0, 0, 0)), pl.BlockSpec(wc.shape, lambda b: (0, 0, 0)),
                pl.BlockSpec(lt.shape, lambda b: (0, 0, 0, 0)), pl.BlockSpec(lt_adj.shape, lambda b: (0, 0, 0, 0))]
    args = [hsrc, dy, cin, wb, wc, lt, lt_adj]
    if has_acc:
        in_specs.append(pl.BlockSpec((tb, SW), lambda b: (tix(b), 0)))
        args.append(acc)
    return pl.pallas_call(
        body, name=name, grid=(nblk,), in_specs=in_specs,
        out_specs=[pl.BlockSpec((tb, SW), lambda b: (tix(b), 0)),
                   pl.BlockSpec(wb.shape, lambda b: (0, 0, 0)), pl.BlockSpec(wc.shape, lambda b: (0, 0, 0)),
                   pl.BlockSpec((8, W2), lambda b: (0, 0))],
        out_shape=[jax.ShapeDtypeStruct((L, SW), F32), jax.ShapeDtypeStruct(wb.shape, F32),
                   jax.ShapeDtypeStruct(wc.shape, F32), jax.ShapeDtypeStruct((8, W2), F32)],
        scratch_shapes=[pltpu.VMEM((tb, W2), F32), pltpu.VMEM((tb, W2), F32),
                        pltpu.VMEM((8, W2), F32), pltpu.VMEM((8, W2), F32)],
        compiler_params=_cparams(("arbitrary",)),
    )(*args)


_NN = (((1,), (0,)), ((), ()))
_NT = (((1,), (1,)), ((), ()))
_TN = (((0,), (0,)), ((), ()))


def _dotb(a, b, dn=_NN):
    return lax.dot_general(a.astype(BF16), b.astype(BF16), dn, preferred_element_type=F32)


def _split(x):
    hi = x.astype(BF16)
    return hi, (x - hi.astype(F32)).astype(BF16)


def _dot3(a, b, dn=_NN):
    ah, al = _split(a)
    bh, bl = _split(b)
    f = lambda x, y: lax.dot_general(x, y, dn, preferred_element_type=F32)
    return f(ah, bh) + (f(ah, bl) + f(al, bh))


@jax.custom_vjp
def _dot3_nn(a, b):
    return _dot3(a, b, _NN)


_dot3_nn.defvjp(lambda a, b: (_dot3(a, b, _NN), (a, b)),
                lambda res, g: (_dotb(g, res[1], _NT), _dotb(res[0], g, _TN)))


@jax.custom_vjp
def _dot3_nt(a, b):
    return _dot3(a, b, _NT)


_dot3_nt.defvjp(lambda a, b: (_dot3(a, b, _NT), (a, b)),
                lambda res, g: (_dotb(g, res[1], _NN), _dotb(g, res[0], _TN)))


def _delta_chunk(rev, one_pass_grads, *flat):
    heads = [flat[i:i + 7] for i in range(0, len(flat), 7)]
    q, k, v, beta, gc, gt, s_in = [list(t) for t in zip(*heads)]
    c, hd = q[0].shape
    each = lambda f, *ls: [f(*t) for t in zip(*ls)]
    mm_nn = _dot3_nn if one_pass_grads else _dot3
    mm_nt = _dot3_nt if one_pass_grads else (lambda x, y: _dot3(x, y, _NT))
    ii = lax.broadcasted_iota(jnp.int32, (c, c), 0)
    jj = lax.broadcasted_iota(jnp.int32, (c, c), 1)
    incl = (ii <= jj) if rev else (ii >= jj)
    strict = (ii < jj) if rev else (ii > jj)
    eye = (ii == jj).astype(F32)
    decay = each(lambda g: jnp.where(incl, jnp.exp(jnp.where(incl, g[:, :c] - jnp.transpose(g)[:c, :], 0.0)), 0.0), gc)
    kb = each(lambda a, b: a * b, k, beta)
    a = each(lambda x, y, dc: jnp.where(strict, mm_nt(x, y) * dc, 0.0), kb, k, decay)
    tinv = each(lambda x: eye - x, a)
    p = a
    n = 2
    while n < c:
        p = each(lambda x: mm_nn(x, x), p)
        tinv = each(lambda t, x: mm_nn(t, eye + x), tinv, p)
        n *= 2
    eg = each(jnp.exp, gc)
    u = each(lambda t, x, b: mm_nn(t, x * b), tinv, v, beta)
    w = each(lambda t, x, e: mm_nn(t, x * e), tinv, kb, eg)
    intra = each(lambda x, y, dc: _dotb(x, y, _NT) * dc, q, k, decay)
    v_new = each(lambda x, y, s: x - _dotb(y, s), u, w, s_in)
    o = each(lambda x, e, s, m, vn: _dotb(x * e, s) + _dotb(m, vn), q, eg, s_in, intra, v_new)
    s_out = each(lambda s, t, x, g, vn: s * jnp.exp(jnp.broadcast_to(t[0:1, :], (hd, hd)))
                 + _dotb(x * jnp.exp(t - g), vn, _TN), s_in, gt, k, gc, v_new)
    return tuple(x for pair in zip(o, s_out) for x in pair)


def _delta_fwd(q, k, v, gates, *, vcb, rev, acc, name, d):
    L, H, hd, C = q.shape[0], d['DNH'], d['DNK'], d['CHUNK']
    nc = L // C
    cix = (lambda i: nc - 1 - i) if rev else (lambda i: i)
    has_acc = acc is not None

    def body(*refs):
        if has_acc:
            q_ref, k_ref, v_ref, b_ref, gc_ref, gt_ref, acc_ref, o_ref, ss_ref, st = refs
        else:
            q_ref, k_ref, v_ref, b_ref, gc_ref, gt_ref, o_ref, ss_ref, st = refs

        @pl.when(pl.program_id(0) == 0)
        def _():
            st[...] = jnp.zeros_like(st)

        sls = [slice(h * hd, (h + 1) * hd) for h in range(H)]
        ins = [(q_ref[:, sl], k_ref[:, sl], v_ref[:, sl], b_ref[:, sl], gc_ref[:, sl], gt_ref[:, sl], st[h])
               for h, sl in enumerate(sls)]
        accv = [acc_ref[:, sl] for sl in sls] if has_acc else None
        res = _delta_chunk(rev, False, *[t for head in ins for t in head])
        for h, sl in enumerate(sls):
            o, s_out = res[2 * h], res[2 * h + 1]
            ss_ref[0, h] = ins[h][6]
            o_ref[:, sl] = o + accv[h] if has_acc else o
            st[h] = s_out

    blk = pl.BlockSpec((C, H * hd), lambda i: (cix(i), 0))
    in_specs = [blk, blk, pl.BlockSpec((C, H * hd), lambda i: (cix(i), vcb)), blk, blk, blk]
    args = [q, k, v, *gates]
    if has_acc:
        in_specs.append(blk)
        args.append(acc)
    return pl.pallas_call(
        body, name=name, grid=(nc,), in_specs=in_specs,
        out_specs=[blk, pl.BlockSpec((1, H, hd, hd), lambda i: (cix(i), 0, 0, 0))],
        out_shape=[jax.ShapeDtypeStruct((L, H * hd), F32), jax.ShapeDtypeStruct((nc, H, hd, hd), F32)],
        scratch_shapes=[pltpu.VMEM((H, hd, hd), F32)],
        compiler_params=_cparams(("arbitrary",)),
    )(*args)


def _delta_bwd(q, k, v, gates, ssave, do, *, vcb, rev, accs, name, d):
    L, H, hd, C = q.shape[0], d['DNH'], d['DNK'], d['CHUNK']
    nc = L // C
    cix = (lambda i: i) if rev else (lambda i: nc - 1 - i)
    has_acc = accs is not None

    def body(*refs):
        if has_acc:
            (q_ref, k_ref, v_ref, b_ref, gc_ref, gt_ref, ss_ref, do_ref, aq_ref, ak_ref, av_ref,
             dq_ref, dk_ref, dv_ref, db_ref, dgc_ref, dgt_ref, dst) = refs
        else:
            (q_ref, k_ref, v_ref, b_ref, gc_ref, gt_ref, ss_ref, do_ref,
             dq_ref, dk_ref, dv_ref, db_ref, dgc_ref, dgt_ref, dst) = refs

        @pl.when(pl.program_id(0) == 0)
        def _():
            dst[...] = jnp.zeros_like(dst)

        sls = [slice(h * hd, (h + 1) * hd) for h in range(H)]
        ins = [(q_ref[:, sl], k_ref[:, sl], v_ref[:, sl], b_ref[:, sl], gc_ref[:, sl], gt_ref[:, sl], ss_ref[0, h])
               for h, sl in enumerate(sls)]
        cts = tuple(t for h, sl in enumerate(sls) for t in (do_ref[:, sl], dst[h]))
        accv = [(aq_ref[:, sl], ak_ref[:, sl], av_ref[:, sl]) for sl in sls] if has_acc else None
        _, vjp = jax.vjp(functools.partial(_delta_chunk, rev, True), *[t for head in ins for t in head])
        res = vjp(cts)
        for h, sl in enumerate(sls):
            dq, dk, dv, db, dgc, dgt, ds = res[7 * h:7 * h + 7]
            dst[h] = ds
            if has_acc:
                dq, dk, dv = dq + accv[h][0], dk + accv[h][1], dv + accv[h][2]
            dq_ref[:, sl] = dq
            dk_ref[:, sl] = dk
            dv_ref[:, sl] = dv
            db_ref[:, sl] = db
            dgc_ref[:, sl] = dgc
            dgt_ref[:, sl] = dgt

    blk = pl.BlockSpec((C, H * hd), lambda i: (cix(i), 0))
    in_specs = [blk, blk, pl.BlockSpec((C, H * hd), lambda i: (cix(i), vcb)), blk, blk, blk,
                pl.BlockSpec((1, H, hd, hd), lambda i: (cix(i), 0, 0, 0)), blk]
    args = [q, k, v, *gates, ssave, do]
    if has_acc:
        in_specs += [blk, blk, blk]
        args += list(accs)
    return pl.pallas_call(
        body, name=name, grid=(nc,), in_specs=in_specs, out_specs=[blk] * 6,
        out_shape=[jax.ShapeDtypeStruct((L, H * hd), F32)] * 6,
        scratch_shapes=[pltpu.VMEM((H, hd, hd), F32)],
        compiler_params=_cparams(("arbitrary",)),
    )(*args)


def _conv_specs(tm, w, cb0, nrb, L):
    hb = tm // 8
    last8 = L // 8 - 1
    cur = pl.BlockSpec((tm, w), lambda s, i: (i, cb0 + s))
    prev = pl.BlockSpec((8, w), lambda s, i: (jnp.maximum(i * hb - 1, 0), cb0 + s))
    nxt = pl.BlockSpec((8, w), lambda s, i: (jnp.minimum((i + 1) * hb, last8), cb0 + s))
    return [prev, cur, nxt]


def _fill_halo(dst, prev_ref, cur_ref, next_ref, i, nrb, tm):
    dst[pl.ds(0, 8), :] = jnp.where(i > 0, prev_ref[...], 0.0)
    dst[pl.ds(8, tm), :] = cur_ref[...]
    dst[pl.ds(8 + tm, 8), :] = jnp.where(i < nrb - 1, next_ref[...], 0.0)


def _conv_fwd(hsrc, cb0, wt, *, tm, name, d):
    L, w, K = hsrc.shape[0], d['DW'], d['CONV']
    tm = _pick(tm, L)
    nrb = L // tm

    def body(prev_ref, cur_ref, next_ref, w_ref, o_ref, xs):
        i = pl.program_id(1)
        _fill_halo(xs, prev_ref, cur_ref, next_ref, i, nrb, tm)
        y = jnp.zeros((tm, w), F32)
        for kk in range(K):
            y = y + w_ref[0, pl.ds(kk, 1), :] * xs[pl.ds(8 - K // 2 + kk, tm), :]
        o_ref[...] = _silu(y)

    return pl.pallas_call(
        body, name=name, grid=(3, nrb),
        in_specs=_conv_specs(tm, w, cb0, nrb, L) + [pl.BlockSpec((1, 8, w), lambda s, i: (s, 0, 0))],
        out_specs=pl.BlockSpec((tm, w), lambda s, i: (i, s)),
        out_shape=jax.ShapeDtypeStruct((L, 3 * w), F32),
        scratch_shapes=[pltpu.VMEM((tm + 16, w), F32)],
        compiler_params=_cparams(("parallel", "parallel")),
    )(hsrc, hsrc, hsrc, wt)


def _conv_bwd(hsrc, cb0, wt, dact, *, tm, name, d):
    L, w, K = hsrc.shape[0], d['DW'], d['CONV']
    tm = _pick(tm, L)
    nrb = L // tm
    half = K // 2

    def body(xp_ref, xc_ref, xn_ref, gp_ref, gc_ref, gn_ref, w_ref, dx_ref, dw_ref, xs, gs, dys):
        i = pl.program_id(1)
        _fill_halo(xs, xp_ref, xc_ref, xn_ref, i, nrb, tm)
        _fill_halo(gs, gp_ref, gc_ref, gn_ref, i, nrb, tm)
        y = jnp.zeros((tm + 8, w), F32)
        for kk in range(K):
            y = y + w_ref[0, pl.ds(kk, 1), :] * xs[pl.ds(4 - half + kk, tm + 8), :]
        sg = jax.nn.sigmoid(y)
        dys[...] = gs[pl.ds(4, tm + 8), :] * (sg * (1.0 + y * (1.0 - sg)))
        dx = jnp.zeros((tm, w), F32)
        for kk in range(K):
            dx = dx + w_ref[0, pl.ds(kk, 1), :] * dys[pl.ds(4 + half - kk, tm), :]
        dx_ref[...] = dx

        @pl.when(i == 0)
        def _():
            dw_ref[...] = jnp.zeros_like(dw_ref)

        dy = dys[pl.ds(4, tm), :]
        for kk in range(K):
            dw_ref[0, pl.ds(kk, 1), :] += jnp.sum(dy * xs[pl.ds(8 - half + kk, tm), :], axis=0, keepdims=True)

    gspecs = _conv_specs(tm, w, 0, nrb, L)
    return pl.pallas_call(
        body, name=name, grid=(3, nrb),
        in_specs=_conv_specs(tm, w, cb0, nrb, L) + gspecs + [pl.BlockSpec((1, 8, w), lambda s, i: (s, 0, 0))],
        out_specs=[pl.BlockSpec((tm, w), lambda s, i: (i, s)), pl.BlockSpec((1, 8, w), lambda s, i: (s, 0, 0))],
        out_shape=[jax.ShapeDtypeStruct((L, 3 * w), F32), jax.ShapeDtypeStruct((3, 8, w), F32)],
        scratch_shapes=[pltpu.VMEM((tm + 16, w), F32), pltpu.VMEM((tm + 16, w), F32), pltpu.VMEM((tm + 8, w), F32)],
        compiler_params=_cparams(("parallel", "arbitrary")),
    )(hsrc, hsrc, hsrc, dact, dact, dact, wt)


def _wide(v, n):
    return v if n == LANE else jnp.tile(v, (1, n // LANE))


def _attn_fwd(qh, kh, vh, *, tq, tk, name, d):
    L, H, KVH, hd = qh.shape[0], d['AH'], d['AKV'], d['AD']
    grp = H // KVH
    tq, tk = _pick(tq, L), _pick(tk, L)
    nk = L // tk

    def body(q_ref, k_ref, v_ref, o_ref, lse_ref, m_s, l_s, acc):
        j = pl.program_id(2)

        @pl.when(j == 0)
        def _():
            m_s[...] = jnp.full_like(m_s, -1e30)
            l_s[...] = jnp.zeros_like(l_s)
            acc[...] = jnp.zeros_like(acc)

        s = lax.dot_general(q_ref[...], k_ref[...], _NT, preferred_element_type=F32)
        m_old = m_s[...]
        m_new = jnp.maximum(m_old, jnp.max(s, axis=-1, keepdims=True))
        alpha = jnp.exp(m_old - m_new)
        p = jnp.exp(s - _wide(m_new, tk))
        l_s[...] = alpha * l_s[...] + jnp.sum(p, axis=-1, keepdims=True)
        acc[...] = alpha * acc[...] + jnp.dot(p.astype(BF16), v_ref[...], preferred_element_type=F32)
        m_s[...] = m_new

        @pl.when(j == nk - 1)
        def _():
            o_ref[...] = acc[...] / l_s[...]
            lse_ref[...] = m_s[...] + jnp.log(l_s[...])

    qspec = pl.BlockSpec((tq, hd), lambda h, i, j: (i, h))
    kspec = pl.BlockSpec((tk, hd), lambda h, i, j: (j, h // grp))
    return pl.pallas_call(
        body, name=name, grid=(H, L // tq, nk), in_specs=[qspec, kspec, kspec], out_specs=[qspec, qspec],
        out_shape=[jax.ShapeDtypeStruct((L, H * hd), F32), jax.ShapeDtypeStruct((L, H * hd), F32)],
        scratch_shapes=[pltpu.VMEM((tq, hd), F32), pltpu.VMEM((tq, hd), F32), pltpu.VMEM((tq, hd), F32)],
        compiler_params=_cparams(("parallel", "parallel", "arbitrary")),
    )(qh, kh, vh)


def _attn_bwd(qh, kh, vh, do, lse, delta, *, tq, tk, name, d):
    L, H, KVH, hd = qh.shape[0], d['AH'], d['AKV'], d['AD']
    grp = H // KVH
    tq, tk = _pick(tq, L), _pick(tk, L)
    nk = L // tk

    def body(q_ref, k_ref, v_ref, do_ref, lse_ref, dl_ref, dq_ref, dk_ref, dv_ref, dq_s):
        g, i, j = pl.program_id(1), pl.program_id(2), pl.program_id(3)

        @pl.when(jnp.logical_and(jnp.logical_and(g == 0, i == 0), j == 0))
        def _():
            dk_ref[...] = jnp.zeros_like(dk_ref)
            dv_ref[...] = jnp.zeros_like(dv_ref)

        @pl.when(j == 0)
        def _():
            dq_s[...] = jnp.zeros_like(dq_s)

        q, k, do_ = q_ref[...], k_ref[...], do_ref[...].astype(BF16)
        s = lax.dot_general(q, k, _NT, preferred_element_type=F32)
        p = jnp.exp(s - _wide(lse_ref[...], tk))
        dp = lax.dot_general(do_, v_ref[...], _NT, preferred_element_type=F32)
        ds = (p * (dp - _wide(dl_ref[...], tk))).astype(BF16)
        dq_s[...] += jnp.dot(ds, k, preferred_element_type=F32)
        rows = pl.ds(pl.multiple_of(j * tk, tk), tk)
        dv_ref[rows, :] += lax.dot_general(p.astype(BF16), do_, _TN, preferred_element_type=F32)
        dk_ref[rows, :] += lax.dot_general(ds, q, _TN, preferred_element_type=F32)

        @pl.when(j == nk - 1)
        def _():
            dq_ref[...] = dq_s[...]

    qspec = pl.BlockSpec((tq, hd), lambda kv, g, i, j: (i, kv * grp + g))
    kspec = pl.BlockSpec((tk, hd), lambda kv, g, i, j: (j, kv))
    colspec = pl.BlockSpec((L, hd), lambda kv, g, i, j: (0, kv))
    return pl.pallas_call(
        body, name=name, grid=(KVH, grp, L // tq, nk),
        in_specs=[qspec, kspec, kspec, qspec, qspec, qspec], out_specs=[qspec, colspec, colspec],
        out_shape=[jax.ShapeDtypeStruct((L, H * hd), F32)] + [jax.ShapeDtypeStruct((L, KVH * hd), F32)] * 2,
        scratch_shapes=[pltpu.VMEM((tq, hd), F32)],
        compiler_params=_cparams(("parallel", "arbitrary", "arbitrary", "arbitrary")),
    )(qh, kh, vh, do, lse, delta)


def _loss_grad(x, g, tgt, *, tm, name):
    L, D = x.shape
    tm = _pick(tm, L)

    def body(x_ref, g_ref, t_ref, loss_ref, dx_ref, dg_ref):
        def f(xv, gv):
            err = _rms(xv, gv) - t_ref[...]
            return 0.5 * jnp.sum(jnp.mean(err * err, axis=-1, keepdims=True))

        val, vjp = jax.vjp(f, x_ref[...], g_ref[...])
        dx, dg = vjp(jnp.ones((), F32))
        dx_ref[...] = dx

        @pl.when(pl.program_id(0) == 0)
        def _():
            loss_ref[...] = jnp.zeros_like(loss_ref)
            dg_ref[...] = jnp.zeros_like(dg_ref)

        loss_ref[...] += val
        dg_ref[...] += dg

    return pl.pallas_call(
        body, name=name, grid=(L // tm,),
        in_specs=[pl.BlockSpec((tm, D), lambda i: (i, 0)), pl.BlockSpec((1, D), lambda i: (0, 0)),
                  pl.BlockSpec((tm, D), lambda i: (i, 0))],
        out_specs=[pl.BlockSpec((8, LANE), lambda i: (0, 0)), pl.BlockSpec((tm, D), lambda i: (i, 0)),
                   pl.BlockSpec((1, D), lambda i: (0, 0))],
        out_shape=[jax.ShapeDtypeStruct((8, LANE), F32), jax.ShapeDtypeStruct((L, D), F32),
                   jax.ShapeDtypeStruct((1, D), F32)],
        compiler_params=_cparams(("arbitrary",)),
    )(x, g, tgt)


def _sum_slots(recv, *, tr, name):
    n, R, W = recv.shape
    tr = _pick(tr, R)

    def body(r_ref, o_ref):
        s = r_ref[0].astype(F32)
        for i in range(1, n):
            s = s + r_ref[i].astype(F32)
        o_ref[...] = s

    return pl.pallas_call(
        body, name=name, grid=(R // tr,),
        in_specs=[pl.BlockSpec((n, tr, W), lambda i: (0, i, 0))], out_specs=pl.BlockSpec((tr, W), lambda i: (i, 0)),
        out_shape=jax.ShapeDtypeStruct((R, W), F32), compiler_params=_cparams(("parallel",)),
    )(recv)


def _adamw(w, g, m, v, *, tr, name):
    R, W = w.shape
    tr = _pick(tr, R)
    c1 = 1.0 - ADAM_B1 ** ADAM_STEP
    c2 = 1.0 - ADAM_B2 ** ADAM_STEP

    def body(w_ref, g_ref, m_ref, v_ref, d_ref, nm_ref, nv_ref):
        gv = g_ref[...]
        nm = ADAM_B1 * m_ref[...] + (1.0 - ADAM_B1) * gv
        nv = ADAM_B2 * v_ref[...] + (1.0 - ADAM_B2) * (gv * gv)
        d_ref[...] = -ADAM_LR * ((nm / c1) / (jnp.sqrt(nv / c2) + ADAM_EPS) + ADAM_WD * w_ref[...])
        nm_ref[...] = nm
        nv_ref[...] = nv

    spec = pl.BlockSpec((tr, W), lambda i: (i, 0))
    return pl.pallas_call(
        body, name=name, grid=(R // tr,), in_specs=[spec] * 4, out_specs=[spec] * 3,
        out_shape=[jax.ShapeDtypeStruct((R, W), F32)] * 3, compiler_params=_cparams(("parallel",)),
    )(w, g, m, v)


def _sum_adamw(recv, w, m, v, *, tr, name):
    n, R, W = recv.shape
    tr = _pick(tr, R)
    c1 = 1.0 - ADAM_B1 ** ADAM_STEP
    c2 = 1.0 - ADAM_B2 ** ADAM_STEP

    def body(r_ref, w_ref, m_ref, v_ref, g_ref, d_ref, nm_ref, nv_ref):
        gv = r_ref[0].astype(F32)
        for i in range(1, n):
            gv = gv + r_ref[i].astype(F32)
        nm = ADAM_B1 * m_ref[...] + (1.0 - ADAM_B1) * gv
        nv = ADAM_B2 * v_ref[...] + (1.0 - ADAM_B2) * (gv * gv)
        g_ref[...] = gv
        d_ref[...] = -ADAM_LR * ((nm / c1) / (jnp.sqrt(nv / c2) + ADAM_EPS) + ADAM_WD * w_ref[...])
        nm_ref[...] = nm
        nv_ref[...] = nv

    spec = pl.BlockSpec((tr, W), lambda i: (i, 0))
    return pl.pallas_call(
        body, name=name, grid=(R // tr,),
        in_specs=[pl.BlockSpec((n, tr, W), lambda i: (0, i, 0))] + [spec] * 3, out_specs=[spec] * 4,
        out_shape=[jax.ShapeDtypeStruct((R, W), F32)] * 4, compiler_params=_cparams(("parallel",)),
    )(recv, w, m, v)


_MESH = pl.DeviceIdType.MESH


def _all_gather(xs, *, name):
    na = len(xs)

    def body(*refs):
        x_refs, out_refs = refs[:na], refs[na:2 * na]
        send_sems, recv_sems, local_sems = refs[2 * na:]
        x, y, c = lax.axis_index("x"), lax.axis_index("y"), lax.axis_index("c")
        me, sibling = (x, y, c), (x, y, 1 - c)
        chips = [(1 - x, y), (x, 1 - y), (1 - x, 1 - y)]

        def slot(a, px, py, pc):
            return out_refs[a].at[4 * px + 2 * py + pc]

        def copy(a, k, block, to, src=None):
            return pltpu.make_async_remote_copy(
                src_ref=slot(a, *block) if src is None else src, dst_ref=slot(a, *block),
                send_sem=send_sems.at[7 * a + k], recv_sem=recv_sems.at[7 * a + k], device_id=to,
                device_id_type=_MESH)

        mine = [pltpu.make_async_copy(x_refs[a], slot(a, *me), local_sems.at[a]) for a in range(na)]
        for cp in mine:
            cp.start()
        first = []
        for a in range(na):
            first.append(copy(a, 0, me, sibling, src=x_refs[a]))
            first += [copy(a, 1 + j, me, (*chip, c), src=x_refs[a]) for j, chip in enumerate(chips)]
        for cp in first:
            cp.start()
        passed = []
        for a in range(na):
            for j, chip in enumerate(chips):
                copy(a, 1 + j, (*chip, c), me).wait_recv()
                passed.append(copy(a, 4 + j, (*chip, c), sibling))
                passed[-1].start()
        for a in range(na):
            copy(a, 0, sibling, me).wait_recv()
            for j, chip in enumerate(chips):
                copy(a, 4 + j, (*chip, 1 - c), me).wait_recv()
        for cp in first + passed:
            cp.wait_send()
        for cp in mine:
            cp.wait()

    return pl.pallas_call(
        body, name=name,
        out_shape=[jax.ShapeDtypeStruct((N_DEV,) + t.shape, t.dtype) for t in xs],
        in_specs=[pl.BlockSpec(memory_space=pl.ANY)] * na, out_specs=[pl.BlockSpec(memory_space=pl.ANY)] * na,
        scratch_shapes=[pltpu.SemaphoreType.DMA((7 * na,)), pltpu.SemaphoreType.DMA((7 * na,)),
                        pltpu.SemaphoreType.DMA((na,))],
    )(*xs)


N_CHIP = N_DEV // 2


def _pair_exchange(gs, *, name):
    na = len(gs)

    def body(*refs):
        g_refs, out_refs = refs[:na], refs[na:2 * na]
        send_sems, recv_sems = refs[2 * na:]
        x, y, c = lax.axis_index("x"), lax.axis_index("y"), lax.axis_index("c")

        def copy(a, chip, core):
            return pltpu.make_async_remote_copy(
                src_ref=g_refs[a].at[2 * chip + core], dst_ref=out_refs[a].at[chip],
                send_sem=send_sems.at[N_CHIP * a + chip], recv_sem=recv_sems.at[N_CHIP * a + chip],
                device_id=(x, y, 1 - c), device_id_type=_MESH)

        sends = [copy(a, chip, 1 - c) for a in range(na) for chip in range(N_CHIP)]
        for cp in sends:
            cp.start()
        for cp in sends:
            cp.wait_recv()
        for cp in sends:
            cp.wait_send()

    return pl.pallas_call(
        body, name=name,
        out_shape=[jax.ShapeDtypeStruct((N_CHIP,) + t.shape[1:], t.dtype) for t in gs],
        in_specs=[pl.BlockSpec(memory_space=pl.ANY)] * na, out_specs=[pl.BlockSpec(memory_space=pl.ANY)] * na,
        scratch_shapes=[pltpu.SemaphoreType.DMA((N_CHIP * na,)), pltpu.SemaphoreType.DMA((N_CHIP * na,))],
    )(*gs)


def _pair_sum(g, r, core, *, out_dtype, tr, name):
    _, R, W = g.shape
    tr = _pick(tr, R)

    def body(core_ref, g_ref, r_ref, o_ref):
        o_ref[...] = (g_ref[...] + r_ref[...]).astype(o_ref.dtype)

    return pl.pallas_call(
        body, name=name,
        grid_spec=pltpu.PrefetchScalarGridSpec(
            num_scalar_prefetch=1, grid=(N_CHIP, R // tr),
            in_specs=[pl.BlockSpec((1, tr, W), lambda ch, i, core_ref: (2 * ch + core_ref[0], i, 0)),
                      pl.BlockSpec((1, tr, W), lambda ch, i, core_ref: (ch, i, 0))],
            out_specs=pl.BlockSpec((1, tr, W), lambda ch, i, core_ref: (ch, i, 0))),
        out_shape=jax.ShapeDtypeStruct((N_CHIP, R, W), out_dtype),
        compiler_params=_cparams(("parallel", "parallel")),
    )(core, g, r)


def _chip_exchange(hs, *, name):
    na = len(hs)

    def body(*refs):
        h_refs, out_refs = refs[:na], refs[na:2 * na]
        send_sems, recv_sems, local_sems = refs[2 * na:]
        x, y, c = lax.axis_index("x"), lax.axis_index("y"), lax.axis_index("c")
        me = 2 * x + y

        def copy(a, mask, started):
            px, py = x ^ (mask >> 1), y ^ (mask & 1)
            mine_, theirs = me, 2 * px + py
            return pltpu.make_async_remote_copy(
                src_ref=h_refs[a].at[theirs if started else mine_],
                dst_ref=out_refs[a].at[mine_ if started else theirs],
                send_sem=send_sems.at[3 * a + mask - 1], recv_sem=recv_sems.at[3 * a + mask - 1],
                device_id=(px, py, c), device_id_type=_MESH)

        mine = [pltpu.make_async_copy(h_refs[a].at[me], out_refs[a].at[me], local_sems.at[a]) for a in range(na)]
        for cp in mine:
            cp.start()
        sends = [copy(a, mask, True) for a in range(na) for mask in range(1, N_CHIP)]
        for cp in sends:
            cp.start()
        for a in range(na):
            for mask in range(1, N_CHIP):
                copy(a, mask, False).wait_recv()
        for cp in sends:
            cp.wait_send()
        for cp in mine:
            cp.wait()

    return pl.pallas_call(
        body, name=name,
        out_shape=[jax.ShapeDtypeStruct(t.shape, t.dtype) for t in hs],
        in_specs=[pl.BlockSpec(memory_space=pl.ANY)] * na, out_specs=[pl.BlockSpec(memory_space=pl.ANY)] * na,
        scratch_shapes=[pltpu.SemaphoreType.DMA((3 * na,)), pltpu.SemaphoreType.DMA((3 * na,)),
                        pltpu.SemaphoreType.DMA((na,))],
    )(*hs)


def _rows_of(shape):
    return -(-int(np.prod(shape)) // PACK_W)


def _pack(arrs, dtype, lead=0, total_rows=None):
    pieces = []
    for a in arrs:
        f = a.reshape(a.shape[:lead] + (-1,)).astype(dtype)
        pad = (-f.shape[-1]) % PACK_W
        if pad:
            f = jnp.pad(f, [(0, 0)] * lead + [(0, pad)])
        pieces.append(f.reshape(a.shape[:lead] + (-1, PACK_W)))
    buf = jnp.concatenate(pieces, axis=lead)
    if total_rows is not None and buf.shape[lead] < total_rows:
        buf = jnp.pad(buf, [(0, 0)] * lead + [(0, total_rows - buf.shape[lead]), (0, 0)])
    return buf


def _unpack(buf, shapes, lead=0):
    out, r = [], 0
    for shp in shapes:
        n, rows = int(np.prod(shp)), _rows_of(shp)
        piece = buf[(slice(None),) * lead + (slice(r, r + rows),)]
        piece = piece.reshape(buf.shape[:lead] + (-1,))[..., :n]
        out.append(piece.reshape(buf.shape[:lead] + tuple(shp)))
        r += rows
    return out


def _pack_flat(arrs, total_rows):
    flat = jnp.concatenate([t.reshape(-1).astype(F32) for t in arrs])
    return jnp.pad(flat, (0, total_rows * PACK_W - flat.shape[0])).reshape(total_rows, PACK_W)


def _unpack_flat(buf, shapes):
    flat, out, off = buf.reshape(-1), [], 0
    for shp in shapes:
        n = int(np.prod(shp))
        out.append(flat[off:off + n].reshape(shp))
        off += n
    return out


def _to_full(parts):
    dep, r = parts.shape[1:3]
    return jnp.transpose(parts, (1, 0) + tuple(range(2, parts.ndim))).reshape((dep, N_DEV * r) + parts.shape[3:])


def _to_slabs(full):
    dep, r = full.shape[:2]
    t = full.reshape((dep, N_DEV, r // N_DEV) + full.shape[2:])
    return jnp.transpose(t, (1, 0) + tuple(range(2, t.ndim)))


def _ref_cols(parts, ro, wd):
    w, out = parts.shape[2], []
    for dev in range(N_DEV):
        lo, hi = max(ro, dev * w), min(ro + wd, (dev + 1) * w)
        if lo < hi:
            out.append(parts[dev][:, lo - dev * w:hi - dev * w])
    return out


def _w_in_to_layout(parts, seg, rseg, nh2):
    D = parts.shape[1]
    cols, off = [], 0
    names = sorted([k for k in seg if not k.startswith('_')], key=lambda k: seg[k][0])
    for nm in names:
        o, wd = seg[nm]
        if o > off:
            cols.append(jnp.zeros((D, o - off), parts.dtype))
        if nm == 'dadb':
            cols += _ref_cols(parts, rseg['da'][0], nh2) + _ref_cols(parts, rseg['db'][0], nh2)
            cols.append(jnp.zeros((D, wd - 2 * nh2), parts.dtype))
        else:
            cols += _ref_cols(parts, rseg[nm][0], wd)
        off = o + wd
    if seg['_total'] > off:
        cols.append(jnp.zeros((D, seg['_total'] - off), parts.dtype))
    return jnp.concatenate(cols, axis=1)


def _w_in_slabs(dw, seg, rseg, nh2):
    w = rseg['_total'] // N_DEV
    ref = []
    for nm in sorted([k for k in rseg if not k.startswith('_')], key=lambda k: rseg[k][0]):
        lo = {'da': seg['dadb'][0], 'db': seg['dadb'][0] + nh2}.get(nm)
        ref.append((rseg[nm][0], rseg[nm][1], seg[nm][0] if lo is None else lo))
    slabs = []
    for dev in range(N_DEV):
        cols = []
        for ro, wd, lo in ref:
            a, b = max(ro, dev * w), min(ro + wd, (dev + 1) * w)
            if a < b:
                cols.append(dw[:, lo + a - ro:lo + b - ro])
        slabs.append(jnp.concatenate(cols, axis=1))
    return jnp.stack(slabs, axis=0)


def _assemble_dh(pieces, seg, L):
    cols, off = [], 0
    for nm in sorted(pieces, key=lambda k: seg[k][0]):
        o = seg[nm][0]
        if o > off:
            cols.append(jnp.zeros((L, o - off), F32))
        cols.append(pieces[nm])
        off = o + pieces[nm].shape[1]
    if seg['_total'] > off:
        cols.append(jnp.zeros((L, seg['_total'] - off), F32))
    return jnp.concatenate(cols, axis=1)


def _lane_pad(v):
    v = v.reshape(1, -1)
    return jnp.pad(v, ((0, 0), (0, LANE - v.shape[1])))


def _rope_tables(L, c):
    rows = L // c['GRID_W']
    row = jnp.repeat(jnp.arange(rows), c['GRID_W']).astype(F32)
    col = jnp.tile(jnp.arange(c['GRID_W']), rows).astype(F32)
    axis_dim = c['AD'] // 2
    freqs = c['ROPE_THETA'] ** (-jnp.arange(0, axis_dim, 2, dtype=F32) / axis_dim)
    ang = jnp.concatenate([row[:, None] * freqs, col[:, None] * freqs], axis=-1)
    cosf = jnp.repeat(jnp.cos(ang), 2, axis=1)
    sn = jnp.sin(ang)
    sins = jnp.stack([-sn, sn], axis=-1).reshape(L, c['AD'])
    idx = np.arange(c['AD'])
    perm = np.zeros((c['AD'], c['AD']), np.float32)
    perm[idx, idx ^ 1] = 1.0
    return cosf, sins, jnp.asarray(perm)


def _s5_dir_params(a, l, dr):
    return (a['ssm_a_re'][l, dr], a['ssm_a_im'][l, dr], a['ssm_log_step'][l, dr], a['ssm_b_re'][l, dr],
            a['ssm_b_im'][l, dr], a['ssm_c_re'][l, dr], a['ssm_c_im'][l, dr])


def _layer_fwd(x, mem, l, wt, a, rope, c, d, seg):
    L, D = x.shape
    SW, DW, AW, AKW, MW, H = d['SW'], d['DW'], d['AW'], d['AKW'], d['MW'], d['DNH']
    cb = lambda nm: seg[nm][0] // seg[nm][1]
    sv = {'x': x}
    p = f"l{l}_"
    sv['g_norm'] = a['norm_g'][l][None, :]
    xn, = _rowwise(_f_norm, [(x, D, 0)], [sv['g_norm']], [(D, BF16)], tm=256, name=p + "norm")
    h = _mm(xn, wt['wp'], name=p + "in_proj", tm=1024, tn=1536, tk=2048)
    sv['xn'], sv['h'] = xn, h

    ysum, sv['s5'] = None, []
    for dr in range(2):
        wb, wc, lr, li = _s5_prep(*_s5_dir_params(a, l, dr), d)
        wb16, wc16 = wb.astype(BF16), wc.astype(BF16)
        lt = _s5_tables(lr, li, bool(dr), False)
        ysum, cin = _s5_fwd(h, cb('u_a'), wb16, wc16, lt, rev=bool(dr), acc=ysum, tb=TILES['s5_t'],
                            name=p + f"s5_fwd{dr}", d=d)
        sv['s5'].append((wb16, wc16, lt, _s5_tables(lr, li, not bool(dr), True), cin))
    sv['ysum'] = ysum
    sv['s5_par'] = [a['ssm_d'][l][None, :], wt['w_glu'], a['ssm_b_glu'][l][None, :]]
    sv['s5_rows'] = [(ysum, SW, 0), (h, SW, cb('u_a')), (h, SW, cb('z_a'))]
    y_a, = _rowwise(_f_s5tail, sv['s5_rows'], sv['s5_par'], [(SW, F32)], tm=256, name=p + "s5_tail")

    act = _conv_fwd(h, cb('dq'), wt['conv'], tm=256, name=p + "dn_conv", d=d)
    sv['act'] = act
    sv['dn_par'] = [_lane_pad(a['dn_a_log'][l]), _lane_pad(a['dn_dt_bias'][l])]
    sv['dn_rows'] = [(act, DW, 0), (act, DW, 1), (h, LANE, seg['dadb'][0] // LANE)]
    dn_out = _rowwise(_make_f_dnpre(H, d['DNK'], c['CHUNK']), sv['dn_rows'], sv['dn_par'], [(DW, F32)] * 8,
                      tm=256, name=p + "dn_pre")
    qn, kn = dn_out[:2]
    sv['qn'], sv['kn'], sv['gates'] = qn, kn, [dn_out[2:5], dn_out[5:8]]
    o_dn, sv['dn_state'] = None, []
    for dr in range(2):
        o_dn, ss = _delta_fwd(qn, kn, act, sv['gates'][dr], vcb=2, rev=bool(dr), acc=o_dn,
                              name=p + f"dn_fwd{dr}", d=d)
        sv['dn_state'].append(ss)
    sv['dnpost_rows'] = [(o_dn, DW, 0), (h, DW, cb('z_b'))]
    sv['dnpost_par'] = [a['dn_norm_g'][l][None, :]]
    y_b, = _rowwise(_make_f_dnpost(d['DNK']), sv['dnpost_rows'], sv['dnpost_par'], [(DW, F32)], tm=256,
                    name=p + "dn_post")

    cosf, sins, perm = rope
    sv['att_par'] = [perm, a['attn_q_norm'][l][None, :], a['attn_k_norm'][l][None, :]]
    qh, kh, vh = _rowwise(_make_f_attpre(d['AD'], True),
                          [(h, AW, cb('aq')), (h, AKW, cb('ak')), (h, AKW, cb('av')), (cosf, d['AD'], 0),
                           (sins, d['AD'], 0)], sv['att_par'], [(AW, BF16), (AKW, BF16), (AKW, BF16)],
                          tm=256, name=p + "att_pre")
    o_att, lse = _attn_fwd(qh, kh, vh, tq=TILES['att_q'], tk=TILES['att_k'], name=p + "att_fwd", d=d)
    sv['qh'], sv['kh'], sv['vh'], sv['o_att'], sv['lse'] = qh, kh, vh, o_att, lse
    y_c, = _rowwise(_f_gate, [(o_att, AW, 0), (h, AW, cb('z_c'))], [], [(AW, F32)], tm=256, name=p + "att_post")

    sv['g_mem'] = a['mem_norm_g'][l][None, :]
    memn, = _rowwise(_f_norm, [(mem, D, 0)], [sv['g_mem']], [(D, BF16)], tm=256, name=p + "mem_norm")
    kv = _mm(memn, wt['w_mem_kv'], name=p + "mem_kv")
    sv['memn'], sv['kv'] = memn, kv
    y_m, = _rowwise(_make_f_mem(d['MH'], d['MD']), [(h, MW, cb('mq')), (h, MW, cb('z_m'))], [kv], [(MW, F32)],
                    tm=256, name=p + "mem_attn")

    ys = [y_a, y_b, y_c, y_m]
    ps = [_mm(y, wb_, name=p + f"branch_proj{i}", out_dtype=BF16)
          for i, (y, wb_) in enumerate(zip(ys, wt['w_branch']))]
    gcb = seg['gates'][0] // D
    sv['merge_rows'] = [(pp, D, 0) for pp in ps] + [(h, D, gcb + i) for i in range(4)]
    merged, = _rowwise(_f_merge, sv['merge_rows'], [], [(D, BF16)], tm=128, name=p + "merge")
    sv['ys'], sv['merged'] = ys, merged
    return _mm(merged, wt['w_out'], add=x, name=p + "out_proj"), sv


def _layer_bwd(dx, mem, l, wt, a, rope, sv, c, d, seg):
    L, D = dx.shape
    SW, DW, AW, AKW, MW, H = d['SW'], d['DW'], d['AW'], d['AKW'], d['MW'], d['DNH']
    cb = lambda nm: seg[nm][0] // seg[nm][1]
    p = f"l{l}_"
    h = sv['h']
    gr = {}
    dmerged = _mm(dx, wt['w_out'], tb=True, name=p + "d_merged")
    gr['w_out'] = _mm(sv['merged'], dx, ta=True, name=p + "dw_out")
    dmr, _ = _rowwise_bwd(_f_merge, sv['merge_rows'], [], [[(dmerged, D, 0)]], [True] * 8, [], tm=128,
                          name=p + "merge_bwd", row_grad_dtype=BF16)
    dps, dgates = dmr[:4], dmr[4:]
    dys = [_mm(dp, wb_, tb=True, name=p + f"d_branch{i}") for i, (dp, wb_) in enumerate(zip(dps, wt['w_branch']))]
    gr['w_branch'] = jnp.concatenate(
        [_mm(y, dp, ta=True, name=p + f"dw_branch{i}") for i, (y, dp) in enumerate(zip(sv['ys'], dps))], axis=0)

    (dmq, dzm), (dkv,) = _rowwise_bwd(_make_f_mem(d['MH'], d['MD']), [(h, MW, cb('mq')), (h, MW, cb('z_m'))],
                                      [sv['kv']], [[(dys[3], MW, 0)]], [True, True], [True], tm=256,
                                      name=p + "mem_attn_bwd")
    gr['w_mem_kv'] = _mm(sv['memn'], dkv, ta=True, name=p + "dw_mem_kv")
    dmemn = _mm(dkv, wt['w_mem_kv'], tb=True, name=p + "d_memn")
    _, (dg_mem,) = _rowwise_bwd(_f_norm, [(mem, D, 0)], [sv['g_mem']], [[(dmemn, D, 0)]], [False], [True], tm=256,
                                name=p + "mem_norm_bwd")
    gr['mem_norm_g'] = dg_mem[0]

    (do_att, dzc), _ = _rowwise_bwd(_f_gate, [(sv['o_att'], AW, 0), (h, AW, cb('z_c'))], [], [[(dys[2], AW, 0)]],
                                    [True, True], [], tm=256, name=p + "att_post_bwd")
    delta, = _rowwise(_make_f_delta(d['AD']), [(do_att, AW, 0), (sv['o_att'], AW, 0)], [], [(AW, F32)], tm=256,
                      name=p + "att_delta")
    att_in = (sv['qh'], sv['kh'], sv['vh'], do_att, sv['lse'], delta)
    dqh, dkh, dvh = _attn_bwd(*att_in, tq=TILES['att_q'], tk=TILES['att_k'], name=p + "att_bwd", d=d)
    cosf, sins, _ = rope
    (daq, dak), (dqg, dkg) = _rowwise_bwd(
        _make_f_attpre(d['AD'], False),
        [(h, AW, cb('aq')), (h, AKW, cb('ak')), (cosf, d['AD'], 0), (sins, d['AD'], 0)], sv['att_par'],
        [[(dqh, AW, 0)], [(dkh, AKW, 0)]], [True, True, False, False], [False, True, True], tm=256,
        name=p + "att_pre_bwd")
    gr['attn_q_norm'], gr['attn_k_norm'] = dqg[0], dkg[0]

    (do_dn, dzb), (dng,) = _rowwise_bwd(_make_f_dnpost(d['DNK']), sv['dnpost_rows'], sv['dnpost_par'],
                                        [[(dys[1], DW, 0)]], [True, True], [True], tm=256, name=p + "dn_post_bwd")
    gr['dn_norm_g'] = dng[0]
    accs, dn_dgates = None, []
    for dr in range(2):
        res = _delta_bwd(sv['qn'], sv['kn'], sv['act'], sv['gates'][dr], sv['dn_state'][dr], do_dn, vcb=2,
                         rev=bool(dr), accs=accs, name=p + f"dn_bwd{dr}", d=d)
        accs = res[:3]
        dn_dgates += res[3:]
    dqn, dkn, dvc = accs
    (dqc, dkc, ddadb), (dalog, ddtb) = _rowwise_bwd(
        _make_f_dnpre(H, d['DNK'], c['CHUNK']), sv['dn_rows'], sv['dn_par'],
        [[(t, DW, 0)] for t in [dqn, dkn] + dn_dgates], [True] * 3, [True, True], tm=256, name=p + "dn_pre_bwd")
    gr['dn_a_log'] = dalog[0, :2 * H].reshape(2, H)
    gr['dn_dt_bias'] = ddtb[0, :2 * H].reshape(2, H)
    dconv_x, dconv_w = _conv_bwd(h, cb('dq'), wt['conv'], jnp.concatenate([dqc, dkc, dvc], axis=1), tm=256,
                                 name=p + "dn_conv_bwd", d=d)
    gr['dn_conv'] = jnp.transpose(dconv_w[:, :c['CONV'], :], (0, 2, 1)).reshape(3 * DW, c['CONV'])

    (dysum, du, dza), (dd, dwglu, dbglu) = _rowwise_bwd(_f_s5tail, sv['s5_rows'], sv['s5_par'], [[(dys[0], SW, 0)]],
                                                        [True] * 3, [True] * 3, tm=256, name=p + "s5_tail_bwd")
    gr['ssm_d'], gr['ssm_w_glu'], gr['ssm_b_glu'] = dd[0], dwglu, dbglu[0]
    s5g = []
    for dr in range(2):
        wb16, wc16, lt, lt_adj, cin = sv['s5'][dr]
        du, dwb, dwc, dlam = _s5_bwd(h, cb('u_a'), dysum, cin, wb16, wc16, lt, lt_adj, rev=bool(dr), acc=du,
                                     tb=TILES['s5_t'],
                                     name=p + f"s5_bwd{dr}", d=d)
        dl = jnp.sum(dlam, axis=0).reshape(d['NB'], 2, d['BS'])
        _, prep_vjp = jax.vjp(lambda *pp: _s5_prep(*pp, d), *_s5_dir_params(a, l, dr))
        s5g.append(prep_vjp((dwb, dwc, dl[:, 0], dl[:, 1])))
    for i, nm in enumerate(['ssm_a_re', 'ssm_a_im', 'ssm_log_step', 'ssm_b_re', 'ssm_b_im', 'ssm_c_re', 'ssm_c_im']):
        gr[nm] = jnp.stack([s5g[0][i], s5g[1][i]], axis=0)

    dh = _assemble_dh({'u_a': du, 'z_a': dza, 'dq': dconv_x, 'z_b': dzb, 'ak': dak, 'av': dvh, 'aq': daq,
                       'z_c': dzc, 'mq': dmq, 'z_m': dzm, 'gates': jnp.concatenate(dgates, axis=1),
                       'dadb': ddadb}, seg, L).astype(BF16)
    gr['wp'] = _mm(sv['xn'], dh, ta=True, name=p + "dw_in", tm=1024, tn=1536, tk=2048)
    dxn = _mm(dh, wt['wp'], tb=True, name=p + "d_xn", tm=1024, tn=1024, tk=1536)
    (dx_in,), (dg_norm,) = _rowwise_bwd(_f_norm, [(sv['x'], D, 0)], [sv['g_norm']], [[(dxn, D, 0)]], [True], [True],
                                        tm=256, name=p + "norm_bwd", accs={0: (dx, D, 0)})
    gr['norm_g'] = dg_norm[0]
    return dx_in, gr


_ARG_NAMES = (['x', 'mem'] + WEIGHTS + ['loss_target'] + ['m_' + w for w in WEIGHTS] + ['v_' + w for w in WEIGHTS])


def kernel(x, mem, norm_g, w_in, ssm_a_re, ssm_a_im, ssm_log_step, ssm_b_re, ssm_b_im, ssm_c_re, ssm_c_im,
           ssm_d, ssm_w_glu, ssm_b_glu, dn_conv, dn_a_log, dn_dt_bias, dn_norm_g, attn_q_norm, attn_k_norm,
           mem_norm_g, w_mem_kv, w_branch, w_out, final_norm_g, loss_target, m_norm_g, m_w_in, m_ssm_a_re,
           m_ssm_a_im, m_ssm_log_step, m_ssm_b_re, m_ssm_b_im, m_ssm_c_re, m_ssm_c_im, m_ssm_d, m_ssm_w_glu,
           m_ssm_b_glu, m_dn_conv, m_dn_a_log, m_dn_dt_bias, m_dn_norm_g, m_attn_q_norm, m_attn_k_norm,
           m_mem_norm_g, m_w_mem_kv, m_w_branch, m_w_out, m_final_norm_g, v_norm_g, v_w_in, v_ssm_a_re,
           v_ssm_a_im, v_ssm_log_step, v_ssm_b_re, v_ssm_b_im, v_ssm_c_re, v_ssm_c_im, v_ssm_d, v_ssm_w_glu,
           v_ssm_b_glu, v_dn_conv, v_dn_a_log, v_dn_dt_bias, v_dn_norm_g, v_attn_q_norm, v_attn_k_norm,
           v_mem_norm_g, v_w_mem_kv, v_w_branch, v_w_out, v_final_norm_g):
    given = locals()
    return _train_step({n: given[n] for n in _ARG_NAMES})


def _train_step(a):
    c = CFG
    d = _dims(c)
    seg, rseg = _layout(c)
    depth, nh2 = c['DEPTH'], 2 * c['DNH']
    x, mem, tgt = a['x'][0], a['mem'][0], a['loss_target'][0]
    L, D = x.shape

    packed = [n for n in SHARDED if n != 'w_in']
    shard_shapes = [a[n].shape for n in packed]
    rw = _round_up(sum(_rows_of(s) for s in shard_shapes), LANE)
    win_shape = a['w_in'].shape
    wcols = win_shape[2]
    g_win, gathered = _all_gather([a['w_in'].astype(BF16).reshape(depth * D, wcols),
                                   _pack([a[n] for n in packed], BF16, total_rows=rw)], name="weights_all_gather")
    full = {n: _to_full(p_) for n, p_ in zip(packed, _unpack(gathered, shard_shapes, lead=1))}
    offs = np.cumsum([0, d['SW'], d['DW'], d['AW'], d['MW']])
    wts = []
    for l in range(depth):
        conv = jnp.transpose(full['dn_conv'][l].astype(F32).reshape(3, d['DW'], c['CONV']), (0, 2, 1))
        wts.append(dict(
            wp=_w_in_to_layout(g_win[:, l * D:(l + 1) * D], seg, rseg, nh2),
            w_branch=[full['w_branch'][l, offs[i]:offs[i + 1]] for i in range(4)],
            w_out=full['w_out'][l], w_mem_kv=full['w_mem_kv'][l], w_glu=full['ssm_w_glu'][l].astype(F32),
            conv=jnp.pad(conv, ((0, 0), (0, 8 - c['CONV']), (0, 0)))))
    rope = _rope_tables(L, c)

    saved = []
    for l in range(depth):
        x, sv = _layer_fwd(x, mem, l, wts[l], a, rope, c, d, seg)
        saved.append(sv)
    loss_part, dx, dg_final = _loss_grad(x, a['final_norm_g'][None, :], tgt, tm=256, name="final_norm_loss")
    grads = [None] * depth
    for l in reversed(range(depth)):
        dx, grads[l] = _layer_bwd(dx, mem, l, wts[l], a, rope, saved[l], c, d, seg)

    gfull = {n: jnp.stack([grads[l][n] for l in range(depth)], axis=0) for n in WEIGHTS
             if n not in ('w_in', 'final_norm_g')}
    gfull['final_norm_g'] = dg_final[0]

    win_slabs = jnp.concatenate([_w_in_slabs(grads[l]['wp'], seg, rseg, nh2) for l in range(depth)], axis=1)
    small_shapes = [a[n].shape for n in SMALL] + [(1,)]
    rs = _round_up(_rows_of((sum(int(np.prod(s)) for s in small_shapes),)), LANE)
    g_shard = _pack([_to_slabs(gfull[n]) for n in packed], F32, lead=1, total_rows=rw)
    g_small = _pack_flat([gfull[n] for n in SMALL] + [loss_part[0, :1]], rs)
    slabs = [win_slabs, g_shard, jnp.broadcast_to(g_small[None], (N_DEV,) + g_small.shape)]
    core = lax.axis_index("c").astype(jnp.int32).reshape(1)
    from_sibling = _pair_exchange(slabs, name="grads_pair_exchange")
    pair_sums = [_pair_sum(g, r, core, out_dtype=dt, tr=256, name=f"grads_pair_sum{i}")
                 for i, (g, r, dt) in enumerate(zip(slabs, from_sibling, (BF16, BF16, F32)))]
    recv = _chip_exchange(pair_sums, name="grads_chip_exchange")
    gsum = jnp.concatenate([_sum_slots(recv[1], tr=256, name="shard_grad_sum"),
                            _sum_slots(recv[2], tr=256, name="small_grad_sum")], axis=0)
    flat = lambda t: t.reshape(depth * D, wcols)
    win_out = [t.reshape(win_shape) for t in _sum_adamw(recv[0], flat(a['w_in']), flat(a['m_w_in']),
                                                        flat(a['v_w_in']), tr=256, name="w_in_sum_adamw")]

    def local_pack(prefix):
        zero = jnp.zeros((1,), F32)
        return jnp.concatenate([_pack([a[prefix + n] for n in packed], F32, total_rows=rw),
                                _pack_flat([a[prefix + n] for n in SMALL] + [zero], rs)], axis=0)

    delta, new_m, new_v = _adamw(local_pack(''), gsum, local_pack('m_'), local_pack('v_'), tr=256, name="adamw")

    def split(buf):
        vals = dict(zip(packed, _unpack(buf[:rw], shard_shapes)))
        small = _unpack_flat(buf[rw:], small_shapes)
        vals.update(zip(SMALL, small[:-1]))
        return vals, small[-1]

    _, loss = split(gsum)
    outs = [loss.reshape(()), dx[None]]
    for i, buf in enumerate((gsum, delta, new_m, new_v)):
        vals, _ = split(buf)
        vals['w_in'] = win_out[i]
        outs += [vals[n] for n in WEIGHTS]
    return tuple(outs)
```

```python
import functools
import math

import numpy as np
import jax
import jax.numpy as jnp
from jax import lax
from jax.experimental import pallas as pl
from jax.experimental.pallas import tpu as pltpu

F32 = jnp.float32
BF16 = jnp.bfloat16
HI = lax.Precision.HIGHEST
EPS = 1e-6
LANE = 128
SUBLANE = 8
VMEM_LIMIT = 56 * 1024 * 1024
N_DEV = 8
PACK_W = 1024

ADAM_LR, ADAM_B1, ADAM_B2, ADAM_EPS, ADAM_WD, ADAM_STEP = 0.001, 0.9, 0.999, 1e-08, 0.01, 10

CFG = dict(D=2048, L=8192, GRID_W=64, NMEM=256, DEPTH=2,
           SG=48, SP=16, SN=64,
           DNH=6, DNK=128, CONV=5, CHUNK=64,
           AH=8, AKV=2, AD=128, ROPE_THETA=10000.0,
           MH=4, MD=128)

TILES = dict(att_q=2048, att_k=2048, s5_t=512)

WEIGHTS = ['norm_g', 'w_in', 'ssm_a_re', 'ssm_a_im', 'ssm_log_step', 'ssm_b_re', 'ssm_b_im', 'ssm_c_re',
           'ssm_c_im', 'ssm_d', 'ssm_w_glu', 'ssm_b_glu', 'dn_conv', 'dn_a_log', 'dn_dt_bias', 'dn_norm_g',
           'attn_q_norm', 'attn_k_norm', 'mem_norm_g', 'w_mem_kv', 'w_branch', 'w_out', 'final_norm_g']
SHARDED = ['w_in', 'w_branch', 'w_out', 'w_mem_kv', 'ssm_w_glu', 'dn_conv']
SMALL = [w for w in WEIGHTS if w not in SHARDED]


def _dims(c):
    d = dict(c)
    d['SW'] = c['SG'] * c['SP']
    d['NB'] = d['SW'] // LANE
    d['GPB'] = LANE // c['SP']
    d['BS'] = d['GPB'] * c['SN']
    d['DW'] = c['DNH'] * c['DNK']
    d['AW'] = c['AH'] * c['AD']
    d['AKW'] = c['AKV'] * c['AD']
    d['MW'] = c['MH'] * c['MD']
    d['BT'] = d['SW'] + d['DW'] + d['AW'] + d['MW']
    return d


def _round_up(a, b):
    return (a + b - 1) // b * b


def _layout(c):
    d = _dims(c)
    D, SW, DW, AW, AKW, MW = d['D'], d['SW'], d['DW'], d['AW'], d['AKW'], d['MW']
    order = [('u_a', SW, SW), ('z_a', SW, SW), ('dq', DW, DW), ('dk', DW, DW), ('dv', DW, DW), ('z_b', DW, DW),
             ('ak', AKW, AKW), ('av', AKW, AKW), ('aq', AW, AW), ('z_c', AW, AW), ('mq', MW, MW), ('z_m', MW, MW),
             ('gates', 4 * D, D), ('dadb', LANE, LANE)]
    off, seg = 0, {}
    for name, w, al in order:
        off = _round_up(off, al)
        seg[name] = (off, w)
        off += w
    seg['_total'] = _round_up(off, 512)
    ref_order = [('u_a', SW), ('z_a', SW), ('dq', DW), ('dk', DW), ('dv', DW), ('da', 2 * d['DNH']),
                 ('db', 2 * d['DNH']), ('z_b', DW), ('aq', AW), ('ak', AKW), ('av', AKW), ('z_c', AW),
                 ('mq', MW), ('z_m', MW), ('gates', 4 * D)]
    roff, rseg = 0, {}
    for name, w in ref_order:
        rseg[name] = (roff, w)
        roff += w
    rseg['_total'] = roff
    return seg, rseg


def _cparams(sem):
    return pltpu.CompilerParams(dimension_semantics=sem, vmem_limit_bytes=VMEM_LIMIT)


def _pick(t, n):
    if n <= t:
        return n
    for align in (LANE, 2 * SUBLANE):
        for cand in range(t - t % align, 0, -align):
            if n % cand == 0:
                return cand
    return n


def _mm(a, b, *, name, ta=False, tb=False, add=None, out_dtype=F32, tm=1024, tn=1024, tk=1024):
    M, K = (a.shape[1], a.shape[0]) if ta else a.shape
    N = b.shape[0] if tb else b.shape[1]
    assert (b.shape[1] if tb else b.shape[0]) == K
    tm, tn, tk = _pick(tm, M), _pick(tn, N), _pick(tk, K)
    nk = K // tk
    dn = (((0 if ta else 1,), (1 if tb else 0,)), ((), ()))
    has_add = add is not None

    def body(*refs):
        if has_add:
            a_ref, b_ref, add_ref, o_ref, acc = refs
        else:
            a_ref, b_ref, o_ref, acc = refs
        k = pl.program_id(2)

        @pl.when(k == 0)
        def _():
            acc[...] = jnp.zeros_like(acc)

        acc[...] += lax.dot_general(a_ref[...].astype(BF16), b_ref[...].astype(BF16), dn,
                                    preferred_element_type=F32)

        @pl.when(k == nk - 1)
        def _():
            r = acc[...]
            if has_add:
                r = r + add_ref[...]
            o_ref[...] = r.astype(o_ref.dtype)

    a_spec = pl.BlockSpec((tk, tm), lambda i, j, k: (k, i)) if ta else pl.BlockSpec((tm, tk), lambda i, j, k: (i, k))
    b_spec = pl.BlockSpec((tn, tk), lambda i, j, k: (j, k)) if tb else pl.BlockSpec((tk, tn), lambda i, j, k: (k, j))
    in_specs = [a_spec, b_spec]
    args = [a, b]
    if has_add:
        in_specs.append(pl.BlockSpec((tm, tn), lambda i, j, k: (i, j)))
        args.append(add)
    return pl.pallas_call(
        body, name=name, grid=(M // tm, N // tn, nk),
        in_specs=in_specs, out_specs=pl.BlockSpec((tm, tn), lambda i, j, k: (i, j)),
        out_shape=jax.ShapeDtypeStruct((M, N), out_dtype),
        scratch_shapes=[pltpu.VMEM((tm, tn), F32)],
        compiler_params=_cparams(("parallel", "parallel", "arbitrary")),
    )(*args)


def _row_spec(tm, w, cb):
    return pl.BlockSpec((tm, w), lambda i, cb=cb: (i, cb))


def _rowwise(fn, rows, params, outs, *, tm, name):
    L = rows[0][0].shape[0]
    tm = _pick(tm, L)
    nr, npar = len(rows), len(params)

    def body(*refs):
        vals = [r[...] for r in refs[:nr + npar]]
        res = fn(*vals)
        for o_ref, v in zip(refs[nr + npar:], res):
            o_ref[...] = v.astype(o_ref.dtype)

    in_specs = [_row_spec(tm, w, cb) for (_, w, cb) in rows]
    in_specs += [pl.BlockSpec(p.shape, lambda i: (0, 0)) for p in params]
    res = pl.pallas_call(
        body, name=name, grid=(L // tm,), in_specs=in_specs,
        out_specs=[pl.BlockSpec((tm, w), lambda i: (i, 0)) for (w, _) in outs],
        out_shape=[jax.ShapeDtypeStruct((L, w), dt) for (w, dt) in outs],
        compiler_params=_cparams(("parallel",)),
    )(*[r[0] for r in rows], *params)
    return list(res)


def _rowwise_bwd(fn, rows, params, cts, drows, dparams, *, tm, name, accs=None, row_grad_dtype=F32):
    L = rows[0][0].shape[0]
    tm = _pick(tm, L)
    nr, npar = len(rows), len(params)
    accs = accs or {}
    ct_flat = [c for grp in cts for c in grp]
    ct_sizes = [len(grp) for grp in cts]
    acc_keys = sorted(accs)
    d_r = [i for i in range(nr) if drows[i]]
    d_p = [i for i in range(npar) if dparams[i]]
    n_in = nr + npar + len(ct_flat) + len(acc_keys)

    def body(*refs):
        vals = [r[...] for r in refs[:nr + npar]]
        ct_refs = refs[nr + npar:nr + npar + len(ct_flat)]
        acc_refs = refs[nr + npar + len(ct_flat):n_in]
        o_refs = refs[n_in:]
        ct_vals, pos = [], 0
        for n in ct_sizes:
            v = ct_refs[pos][...].astype(F32)
            for r in ct_refs[pos + 1:pos + n]:
                v = v + r[...].astype(F32)
            ct_vals.append(v)
            pos += n
        diff_idx = d_r + [nr + i for i in d_p]

        def g(*dv):
            full = list(vals)
            for i, v in zip(diff_idx, dv):
                full[i] = v
            return tuple(o.astype(F32) for o in fn(*full))

        _, vjp = jax.vjp(g, *[vals[i] for i in diff_idx])
        grads = vjp(tuple(ct_vals))
        for n, i in enumerate(d_r):
            gv = grads[n].astype(F32)
            if i in accs:
                gv = gv + acc_refs[acc_keys.index(i)][...]
            o_refs[n][...] = gv.astype(o_refs[n].dtype)
        step = pl.program_id(0)
        for n, i in enumerate(d_p):
            o_ref = o_refs[len(d_r) + n]

            @pl.when(step == 0)
            def _(o_ref=o_ref):
                o_ref[...] = jnp.zeros_like(o_ref)

            o_ref[...] += grads[len(d_r) + n].astype(F32)

    in_specs = [_row_spec(tm, w, cb) for (_, w, cb) in rows]
    in_specs += [pl.BlockSpec(p.shape, lambda i: (0, 0)) for p in params]
    in_specs += [_row_spec(tm, w, cb) for (_, w, cb) in ct_flat]
    in_specs += [_row_spec(tm, accs[k][1], accs[k][2]) for k in acc_keys]
    out_specs = [pl.BlockSpec((tm, rows[i][1]), lambda i_: (i_, 0)) for i in d_r]
    out_specs += [pl.BlockSpec(params[i].shape, lambda i_: (0, 0)) for i in d_p]
    out_shape = [jax.ShapeDtypeStruct((L, rows[i][1]), row_grad_dtype) for i in d_r]
    out_shape += [jax.ShapeDtypeStruct(params[i].shape, F32) for i in d_p]
    res = pl.pallas_call(
        body, name=name, grid=(L // tm,), in_specs=in_specs, out_specs=out_specs, out_shape=out_shape,
        compiler_params=_cparams(("arbitrary",)),
    )(*[r[0] for r in rows], *params, *[c[0] for c in ct_flat], *[accs[k][0] for k in acc_keys])
    res = list(res)
    return res[:len(d_r)], res[len(d_r):]


def _silu(x):
    return x * jax.nn.sigmoid(x)


def _rms(x, g):
    return x * lax.rsqrt(jnp.mean(x * x, axis=-1, keepdims=True) + EPS) * g


def _softplus(x):
    return jnp.maximum(x, 0.0) + jnp.log1p(jnp.exp(-jnp.abs(x)))


def _heads(x, hd):
    return [x[:, i * hd:(i + 1) * hd] for i in range(x.shape[1] // hd)]


def _f_norm(x, g):
    return (_rms(x, g),)


def _f_s5tail(ys, u, z, d, wglu, bglu):
    y = jax.nn.gelu(ys + d * u)
    gate = jax.nn.sigmoid(jnp.dot(y.astype(BF16), wglu.astype(BF16), preferred_element_type=F32) + bglu)
    return (y * gate * _silu(z),)


def _make_f_dnpre(nh, hd, chunk):
    def f(qc, kc, dadb, alog, dtb):
        tm = qc.shape[0]
        qn = [q * lax.rsqrt(jnp.sum(q * q, axis=-1, keepdims=True) + EPS) * (hd ** -0.5) for q in _heads(qc, hd)]
        kn = [k * lax.rsqrt(jnp.sum(k * k, axis=-1, keepdims=True) + EPS) for k in _heads(kc, hd)]
        g = -jnp.exp(alog) * _softplus(dadb + dtb)
        beta = jax.nn.sigmoid(dadb)
        ii = lax.broadcasted_iota(jnp.int32, (tm, tm), 0)
        jj = lax.broadcasted_iota(jnp.int32, (tm, tm), 1)
        same = (ii // chunk) == (jj // chunk)
        outs = [jnp.concatenate(qn, axis=1), jnp.concatenate(kn, axis=1)]
        gt = jnp.dot(same.astype(F32), g, precision=HI, preferred_element_type=F32)
        for dr in range(2):
            tri = jnp.logical_and(same, (ii <= jj) if dr else (ii >= jj)).astype(F32)
            gc = jnp.dot(tri, g, precision=HI, preferred_element_type=F32)

            def spread(t, lane0):
                return jnp.concatenate([jnp.broadcast_to(t[:, lane0 + h:lane0 + h + 1], (tm, hd))
                                        for h in range(nh)], axis=1)

            outs += [spread(beta, 2 * nh + dr * nh), spread(gc, dr * nh), spread(gt, dr * nh)]
        return tuple(outs)
    return f


def _make_f_dnpost(hd):
    def f(o, z, ng):
        y = [_rms(oh, ng) for oh in _heads(o, hd)]
        return (jnp.concatenate(y, axis=1) * _silu(z),)
    return f


def _make_f_attpre(hd, with_v):
    def rope(x, g, cosf, sins, perm, scale):
        xn = _rms(x, g)
        xs = jnp.dot(xn, perm, precision=HI, preferred_element_type=F32)
        return (xn * cosf + xs * sins) * scale

    def f(aq, ak, *rest):
        if with_v:
            av, cosf, sins, perm, qg, kg = rest
        else:
            cosf, sins, perm, qg, kg = rest
        qh = jnp.concatenate([rope(x, qg, cosf, sins, perm, hd ** -0.5) for x in _heads(aq, hd)], axis=1)
        kh = jnp.concatenate([rope(x, kg, cosf, sins, perm, 1.0) for x in _heads(ak, hd)], axis=1)
        return (qh, kh, av) if with_v else (qh, kh)
    return f


def _f_gate(o, z):
    return (o * _silu(z),)


def _make_f_mem(nh, hd):
    def f(mq, z, kv):
        mw = nh * hd
        outs = []
        for h, q in enumerate(_heads(mq, hd)):
            k = kv[:, h * hd:(h + 1) * hd]
            v = kv[:, mw + h * hd:mw + (h + 1) * hd]
            s = lax.dot_general(q.astype(BF16), k.astype(BF16), (((1,), (1,)), ((), ())),
                                preferred_element_type=F32) * (hd ** -0.5)
            s = s - jnp.max(s, axis=-1, keepdims=True)
            p = jnp.exp(s)
            p = p / jnp.sum(p, axis=-1, keepdims=True)
            outs.append(jnp.dot(p.astype(BF16), v.astype(BF16), preferred_element_type=F32))
        return (jnp.concatenate(outs, axis=1) * _silu(z),)
    return f


def _f_merge(p0, p1, p2, p3, g0, g1, g2, g3):
    return (jax.nn.sigmoid(g0) * p0 + jax.nn.sigmoid(g1) * p1 + jax.nn.sigmoid(g2) * p2 + jax.nn.sigmoid(g3) * p3,)


def _make_f_delta(hd):
    def f(do, o):
        out = [jnp.broadcast_to(jnp.sum(a * b, axis=-1, keepdims=True), a.shape)
               for a, b in zip(_heads(do, hd), _heads(o, hd))]
        return (jnp.concatenate(out, axis=1),)
    return f


def _s5_prep(a_re, a_im, log_step, b_re, b_im, c_re, c_im, d):
    nb, gpb, sn, sp = d['NB'], d['GPB'], d['SN'], d['SP']
    step = jnp.exp(log_step)[:, None]
    mag = jnp.exp(a_re * step)
    lam_re = mag * jnp.cos(a_im * step)
    lam_im = mag * jnp.sin(a_im * step)
    den = a_re * a_re + a_im * a_im
    nr, ni = lam_re - 1.0, lam_im
    coef_re = (nr * a_re + ni * a_im) / den
    coef_im = (ni * a_re - nr * a_im) / den
    bb_re = coef_re[..., None] * b_re - coef_im[..., None] * b_im
    bb_im = coef_re[..., None] * b_im + coef_im[..., None] * b_re
    eye = jnp.eye(gpb, dtype=F32)

    def blk_in(bb):
        t = bb.reshape(nb, gpb, sn, sp)
        return jnp.einsum("jgnp,gh->jgphn", t, eye).reshape(nb, gpb * sp, gpb * sn)

    def blk_out(cc):
        t = cc.reshape(nb, gpb, sp, sn)
        return jnp.einsum("jgpn,gh->jgnhp", t, eye).reshape(nb, gpb * sn, gpb * sp)

    wb = jnp.concatenate([blk_in(bb_re), blk_in(bb_im)], axis=2)
    wc = jnp.concatenate([blk_out(c_re), blk_out(-c_im)], axis=1)
    return wb, wc, lam_re.reshape(nb, gpb * sn), lam_im.reshape(nb, gpb * sn)


def _s5_tables(lam_re, lam_im, rev, conj):
    lr, li = lam_re, (-lam_im if conj else lam_im)

    def cmul(a, b):
        return a[0] * b[0] - a[1] * b[1], a[0] * b[1] + a[1] * b[0]

    pw = [(lr, li)]
    for _ in range(7):
        pw.append(cmul(pw[-1], (lr, li)))
    rows = jnp.arange(8)

    def bc(t, k):
        keep = (rows < 8 - k) if rev else (rows >= k)
        return t[:, None, :] * keep.astype(F32)[None, :, None]

    order = list(range(8))[::-1] if rev else list(range(8))
    pwr = jnp.stack([pw[i][0] for i in order], axis=1)
    pwi = jnp.stack([pw[i][1] for i in order], axis=1)
    tabs = [bc(pw[0][0], 1), bc(pw[0][1], 1), bc(pw[1][0], 2), bc(pw[1][1], 2), bc(pw[3][0], 4), bc(pw[3][1], 4),
            pwr, pwi]
    return jnp.stack(tabs, axis=1)


def _scan_group(xr, xi, lt_ref, j, cr, ci, rev):
    for lvl, k in enumerate((1, 2, 4)):
        l_r, l_i = lt_ref[j, 2 * lvl], lt_ref[j, 2 * lvl + 1]
        sh = (8 - k) if rev else k
        sr, si = pltpu.roll(xr, sh, 0), pltpu.roll(xi, sh, 0)
        xr, xi = xr + l_r * sr - l_i * si, xi + l_r * si + l_i * sr
    p_r, p_i = lt_ref[j, 6], lt_ref[j, 7]
    return xr + p_r * cr - p_i * ci, xi + p_r * ci + p_i * cr


def _last_row(x, rev):
    last = 0 if rev else 7
    return jnp.broadcast_to(x[last:last + 1, :], x.shape)


def _s5_fwd(hsrc, ucb, wb, wc, lt, *, rev, acc, tb, name, d):
    L, SW, NB, BS = hsrc.shape[0], d['SW'], d['NB'], d['BS']
    tb = _pick(tb, L)
    nblk, ngr = L // tb, tb // 8
    tix = (lambda b: nblk - 1 - b) if rev else (lambda b: b)
    has_acc = acc is not None

    def body(*refs):
        if has_acc:
            u_ref, wb_ref, wc_ref, lt_ref, acc_ref, y_ref, cin_ref, bu_s, car = refs
        else:
            u_ref, wb_ref, wc_ref, lt_ref, y_ref, cin_ref, bu_s, car = refs

        @pl.when(pl.program_id(0) == 0)
        def _():
            car[...] = jnp.zeros_like(car)

        cin_ref[...] = car[...]
        for j in range(NB):
            bu_s[:, j * 2 * BS:(j + 1) * 2 * BS] = jnp.dot(
                u_ref[:, j * LANE:(j + 1) * LANE].astype(BF16), wb_ref[j], preferred_element_type=F32)

        def grp(r, _):
            base = pl.multiple_of((ngr - 1 - r if rev else r) * 8, 8)
            for j in range(NB):
                c0 = j * 2 * BS
                xr, xi = _scan_group(bu_s[pl.ds(base, 8), c0:c0 + BS], bu_s[pl.ds(base, 8), c0 + BS:c0 + 2 * BS],
                                     lt_ref, j, car[:, c0:c0 + BS], car[:, c0 + BS:c0 + 2 * BS], rev)
                bu_s[pl.ds(base, 8), c0:c0 + BS] = xr
                bu_s[pl.ds(base, 8), c0 + BS:c0 + 2 * BS] = xi
                car[:, c0:c0 + BS] = _last_row(xr, rev)
                car[:, c0 + BS:c0 + 2 * BS] = _last_row(xi, rev)
            return 0

        lax.fori_loop(0, ngr, grp, 0)
        for j in range(NB):
            y = jnp.dot(bu_s[:, j * 2 * BS:(j + 1) * 2 * BS].astype(BF16), wc_ref[j], preferred_element_type=F32)
            if has_acc:
                y = y + acc_ref[:, j * LANE:(j + 1) * LANE]
            y_ref[:, j * LANE:(j + 1) * LANE] = y

    in_specs = [pl.BlockSpec((tb, SW), lambda b: (tix(b), ucb)),
                pl.BlockSpec(wb.shape, lambda b: (0, 0, 0)), pl.BlockSpec(wc.shape, lambda b: (0, 0, 0)),
                pl.BlockSpec(lt.shape, lambda b: (0, 0, 0, 0))]
    args = [hsrc, wb, wc, lt]
    if has_acc:
        in_specs.append(pl.BlockSpec((tb, SW), lambda b: (tix(b), 0)))
        args.append(acc)
    y, cin = pl.pallas_call(
        body, name=name, grid=(nblk,), in_specs=in_specs,
        out_specs=[pl.BlockSpec((tb, SW), lambda b: (tix(b), 0)),
                   pl.BlockSpec((8, NB * 2 * BS), lambda b: (tix(b), 0))],
        out_shape=[jax.ShapeDtypeStruct((L, SW), F32), jax.ShapeDtypeStruct((nblk * 8, NB * 2 * BS), F32)],
        scratch_shapes=[pltpu.VMEM((tb, NB * 2 * BS), F32), pltpu.VMEM((8, NB * 2 * BS), F32)],
        compiler_params=_cparams(("arbitrary",)),
    )(*args)
    return y, cin


def _s5_bwd(hsrc, ucb, dy, cin, wb, wc, lt, lt_adj, *, rev, acc, tb, name, d):
    L, SW, NB, BS = hsrc.shape[0], d['SW'], d['NB'], d['BS']
    tb = _pick(tb, L)
    nblk, ngr = L // tb, tb // 8
    arev = not rev
    tix = (lambda b: nblk - 1 - b) if arev else (lambda b: b)
    has_acc = acc is not None
    NT = (((1,), (1,)), ((), ()))
    TN = (((0,), (0,)), ((), ()))

    def body(*refs):
        if has_acc:
            (u_ref, dy_ref, cin_ref, wb_ref, wc_ref, lt_ref, la_ref, acc_ref,
             du_ref, dwb_ref, dwc_ref, dlam_ref, s_s, g_s, car, acar) = refs
        else:
            (u_ref, dy_ref, cin_ref, wb_ref, wc_ref, lt_ref, la_ref,
             du_ref, dwb_ref, dwc_ref, dlam_ref, s_s, g_s, car, acar) = refs

        @pl.when(pl.program_id(0) == 0)
        def _():
            acar[...] = jnp.zeros_like(acar)
            dwb_ref[...] = jnp.zeros_like(dwb_ref)
            dwc_ref[...] = jnp.zeros_like(dwc_ref)
            dlam_ref[...] = jnp.zeros_like(dlam_ref)

        car[...] = cin_ref[...]
        for j in range(NB):
            s_s[:, j * 2 * BS:(j + 1) * 2 * BS] = jnp.dot(
                u_ref[:, j * LANE:(j + 1) * LANE].astype(BF16), wb_ref[j], preferred_element_type=F32)
            g_s[:, j * 2 * BS:(j + 1) * 2 * BS] = lax.dot_general(
                dy_ref[:, j * LANE:(j + 1) * LANE].astype(BF16), wc_ref[j], NT, preferred_element_type=F32)

        def fgrp(r, _):
            base = pl.multiple_of((ngr - 1 - r if rev else r) * 8, 8)
            for j in range(NB):
                c0 = j * 2 * BS
                xr, xi = _scan_group(s_s[pl.ds(base, 8), c0:c0 + BS], s_s[pl.ds(base, 8), c0 + BS:c0 + 2 * BS],
                                     lt_ref, j, car[:, c0:c0 + BS], car[:, c0 + BS:c0 + 2 * BS], rev)
                s_s[pl.ds(base, 8), c0:c0 + BS] = xr
                s_s[pl.ds(base, 8), c0 + BS:c0 + 2 * BS] = xi
                car[:, c0:c0 + BS] = _last_row(xr, rev)
                car[:, c0 + BS:c0 + 2 * BS] = _last_row(xi, rev)
            return 0

        lax.fori_loop(0, ngr, fgrp, 0)

        row = lax.broadcasted_iota(jnp.int32, (8, BS), 0)

        def agrp(r, _):
            gi = ngr - 1 - r if arev else r
            base = pl.multiple_of(gi * 8, 8)
            pgi = gi + 1 if rev else gi - 1
            inside = jnp.logical_and(pgi >= 0, pgi < ngr)
            pbase = pl.multiple_of(jnp.clip(pgi, 0, ngr - 1) * 8, 8)
            for j in range(NB):
                c0 = j * 2 * BS
                ar, ai = _scan_group(g_s[pl.ds(base, 8), c0:c0 + BS], g_s[pl.ds(base, 8), c0 + BS:c0 + 2 * BS],
                                     la_ref, j, acar[:, c0:c0 + BS], acar[:, c0 + BS:c0 + 2 * BS], arev)
                g_s[pl.ds(base, 8), c0:c0 + BS] = ar
                g_s[pl.ds(base, 8), c0 + BS:c0 + 2 * BS] = ai
                acar[:, c0:c0 + BS] = _last_row(ar, arev)
                acar[:, c0 + BS:c0 + 2 * BS] = _last_row(ai, arev)
                sr, si = s_s[pl.ds(base, 8), c0:c0 + BS], s_s[pl.ds(base, 8), c0 + BS:c0 + 2 * BS]
                edge_r = jnp.where(inside, _last_row(s_s[pl.ds(pbase, 8), c0:c0 + BS], rev), cin_ref[:, c0:c0 + BS])
                edge_i = jnp.where(inside, _last_row(s_s[pl.ds(pbase, 8), c0 + BS:c0 + 2 * BS], rev),
                                   cin_ref[:, c0 + BS:c0 + 2 * BS])
                sh = 7 if rev else 1
                first = 7 if rev else 0
                pr = jnp.where(row == first, edge_r, pltpu.roll(sr, sh, 0))
                pi = jnp.where(row == first, edge_i, pltpu.roll(si, sh, 0))
                dlam_ref[:, c0:c0 + BS] += ar * pr + ai * pi
                dlam_ref[:, c0 + BS:c0 + 2 * BS] += ai * pr - ar * pi
            return 0

        lax.fori_loop(0, ngr, agrp, 0)
        for j in range(NB):
            a_j = g_s[:, j * 2 * BS:(j + 1) * 2 * BS].astype(BF16)
            u_j = u_ref[:, j * LANE:(j + 1) * LANE].astype(BF16)
            du = lax.dot_general(a_j, wb_ref[j], NT, preferred_element_type=F32)
            if has_acc:
                du = du + acc_ref[:, j * LANE:(j + 1) * LANE]
            du_ref[:, j * LANE:(j + 1) * LANE] = du
            dwb_ref[j] += lax.dot_general(u_j, a_j, TN, preferred_element_type=F32)
            dwc_ref[j] += lax.dot_general(s_s[:, j * 2 * BS:(j + 1) * 2 * BS].astype(BF16),
                                          dy_ref[:, j * LANE:(j + 1) * LANE].astype(BF16), TN,
                                          preferred_element_type=F32)

    W2 = NB * 2 * BS
    in_specs = [pl.BlockSpec((tb, SW), lambda b: (tix(b), ucb)), pl.BlockSpec((tb, SW), lambda b: (tix(b), 0)),
                pl.BlockSpec((8, W2), lambda b: (tix(b), 0)),
                pl.BlockSpec(wb.shape, lambda b: (0, 0, 0)), pl.BlockSpec(wc.shape, lambda b: (0, 0, 0)),
                pl.BlockSpec(lt.shape, lambda b: (0, 0, 0, 0)), pl.BlockSpec(lt_adj.shape, lambda b: (0, 0, 0, 0))]
    args = [hsrc, dy, cin, wb, wc, lt, lt_adj]
    if has_acc:
        in_specs.append(pl.BlockSpec((tb, SW), lambda b: (tix(b), 0)))
        args.append(acc)
    return pl.pallas_call(
        body, name=name, grid=(nblk,), in_specs=in_specs,
        out_specs=[pl.BlockSpec((tb, SW), lambda b: (tix(b), 0)),
                   pl.BlockSpec(wb.shape, lambda b: (0, 0, 0)), pl.BlockSpec(wc.shape, lambda b: (0, 0, 0)),
                   pl.BlockSpec((8, W2), lambda b: (0, 0))],
        out_shape=[jax.ShapeDtypeStruct((L, SW), F32), jax.ShapeDtypeStruct(wb.shape, F32),
                   jax.ShapeDtypeStruct(wc.shape, F32), jax.ShapeDtypeStruct((8, W2), F32)],
        scratch_shapes=[pltpu.VMEM((tb, W2), F32), pltpu.VMEM((tb, W2), F32),
                        pltpu.VMEM((8, W2), F32), pltpu.VMEM((8, W2), F32)],
        compiler_params=_cparams(("arbitrary",)),
    )(*args)


_NN = (((1,), (0,)), ((), ()))
_NT = (((1,), (1,)), ((), ()))
_TN = (((0,), (0,)), ((), ()))


def _dotb(a, b, dn=_NN):
    return lax.dot_general(a.astype(BF16), b.astype(BF16), dn, preferred_element_type=F32)


def _split(x):
    hi = x.astype(BF16)
    return hi, (x - hi.astype(F32)).astype(BF16)


def _dot3(a, b, dn=_NN):
    ah, al = _split(a)
    bh, bl = _split(b)
    f = lambda x, y: lax.dot_general(x, y, dn, preferred_element_type=F32)
    return f(ah, bh) + (f(ah, bl) + f(al, bh))


@jax.custom_vjp
def _dot3_nn(a, b):
    return _dot3(a, b, _NN)


_dot3_nn.defvjp(lambda a, b: (_dot3(a, b, _NN), (a, b)),
                lambda res, g: (_dotb(g, res[1], _NT), _dotb(res[0], g, _TN)))


@jax.custom_vjp
def _dot3_nt(a, b):
    return _dot3(a, b, _NT)


_dot3_nt.defvjp(lambda a, b: (_dot3(a, b, _NT), (a, b)),
                lambda res, g: (_dotb(g, res[1], _NN), _dotb(g, res[0], _TN)))


def _delta_chunk(rev, one_pass_grads, *flat):
    heads = [flat[i:i + 7] for i in range(0, len(flat), 7)]
    q, k, v, beta, gc, gt, s_in = [list(t) for t in zip(*heads)]
    c, hd = q[0].shape
    each = lambda f, *ls: [f(*t) for t in zip(*ls)]
    mm_nn = _dot3_nn if one_pass_grads else _dot3
    mm_nt = _dot3_nt if one_pass_grads else (lambda x, y: _dot3(x, y, _NT))
    ii = lax.broadcasted_iota(jnp.int32, (c, c), 0)
    jj = lax.broadcasted_iota(jnp.int32, (c, c), 1)
    incl = (ii <= jj) if rev else (ii >= jj)
    strict = (ii < jj) if rev else (ii > jj)
    eye = (ii == jj).astype(F32)
    decay = each(lambda g: jnp.where(incl, jnp.exp(jnp.where(incl, g[:, :c] - jnp.transpose(g)[:c, :], 0.0)), 0.0), gc)
    kb = each(lambda a, b: a * b, k, beta)
    a = each(lambda x, y, dc: jnp.where(strict, mm_nt(x, y) * dc, 0.0), kb, k, decay)
    tinv = each(lambda x: eye - x, a)
    p = a
    n = 2
    while n < c:
        p = each(lambda x: mm_nn(x, x), p)
        tinv = each(lambda t, x: mm_nn(t, eye + x), tinv, p)
        n *= 2
    eg = each(jnp.exp, gc)
    u = each(lambda t, x, b: mm_nn(t, x * b), tinv, v, beta)
    w = each(lambda t, x, e: mm_nn(t, x * e), tinv, kb, eg)
    intra = each(lambda x, y, dc: _dotb(x, y, _NT) * dc, q, k, decay)
    v_new = each(lambda x, y, s: x - _dotb(y, s), u, w, s_in)
    o = each(lambda x, e, s, m, vn: _dotb(x * e, s) + _dotb(m, vn), q, eg, s_in, intra, v_new)
    s_out = each(lambda s, t, x, g, vn: s * jnp.exp(jnp.broadcast_to(t[0:1, :], (hd, hd)))
                 + _dotb(x * jnp.exp(t - g), vn, _TN), s_in, gt, k, gc, v_new)
    return tuple(x for pair in zip(o, s_out) for x in pair)


def _delta_fwd(q, k, v, gates, *, vcb, rev, acc, name, d):
    L, H, hd, C = q.shape[0], d['DNH'], d['DNK'], d['CHUNK']
    nc = L // C
    cix = (lambda i: nc - 1 - i) if rev else (lambda i: i)
    has_acc = acc is not None

    def body(*refs):
        if has_acc:
            q_ref, k_ref, v_ref, b_ref, gc_ref, gt_ref, acc_ref, o_ref, ss_ref, st = refs
        else:
            q_ref, k_ref, v_ref, b_ref, gc_ref, gt_ref, o_ref, ss_ref, st = refs

        @pl.when(pl.program_id(0) == 0)
        def _():
            st[...] = jnp.zeros_like(st)

        sls = [slice(h * hd, (h + 1) * hd) for h in range(H)]
        ins = [(q_ref[:, sl], k_ref[:, sl], v_ref[:, sl], b_ref[:, sl], gc_ref[:, sl], gt_ref[:, sl], st[h])
               for h, sl in enumerate(sls)]
        accv = [acc_ref[:, sl] for sl in sls] if has_acc else None
        res = _delta_chunk(rev, False, *[t for head in ins for t in head])
        for h, sl in enumerate(sls):
            o, s_out = res[2 * h], res[2 * h + 1]
            ss_ref[0, h] = ins[h][6]
            o_ref[:, sl] = o + accv[h] if has_acc else o
            st[h] = s_out

    blk = pl.BlockSpec((C, H * hd), lambda i: (cix(i), 0))
    in_specs = [blk, blk, pl.BlockSpec((C, H * hd), lambda i: (cix(i), vcb)), blk, blk, blk]
    args = [q, k, v, *gates]
    if has_acc:
        in_specs.append(blk)
        args.append(acc)
    return pl.pallas_call(
        body, name=name, grid=(nc,), in_specs=in_specs,
        out_specs=[blk, pl.BlockSpec((1, H, hd, hd), lambda i: (cix(i), 0, 0, 0))],
        out_shape=[jax.ShapeDtypeStruct((L, H * hd), F32), jax.ShapeDtypeStruct((nc, H, hd, hd), F32)],
        scratch_shapes=[pltpu.VMEM((H, hd, hd), F32)],
        compiler_params=_cparams(("arbitrary",)),
    )(*args)


def _delta_bwd(q, k, v, gates, ssave, do, *, vcb, rev, accs, name, d):
    L, H, hd, C = q.shape[0], d['DNH'], d['DNK'], d['CHUNK']
    nc = L // C
    cix = (lambda i: i) if rev else (lambda i: nc - 1 - i)
    has_acc = accs is not None

    def body(*refs):
        if has_acc:
            (q_ref, k_ref, v_ref, b_ref, gc_ref, gt_ref, ss_ref, do_ref, aq_ref, ak_ref, av_ref,
             dq_ref, dk_ref, dv_ref, db_ref, dgc_ref, dgt_ref, dst) = refs
        else:
            (q_ref, k_ref, v_ref, b_ref, gc_ref, gt_ref, ss_ref, do_ref,
             dq_ref, dk_ref, dv_ref, db_ref, dgc_ref, dgt_ref, dst) = refs

        @pl.when(pl.program_id(0) == 0)
        def _():
            dst[...] = jnp.zeros_like(dst)

        sls = [slice(h * hd, (h + 1) * hd) for h in range(H)]
        ins = [(q_ref[:, sl], k_ref[:, sl], v_ref[:, sl], b_ref[:, sl], gc_ref[:, sl], gt_ref[:, sl], ss_ref[0, h])
               for h, sl in enumerate(sls)]
        cts = tuple(t for h, sl in enumerate(sls) for t in (do_ref[:, sl], dst[h]))
        accv = [(aq_ref[:, sl], ak_ref[:, sl], av_ref[:, sl]) for sl in sls] if has_acc else None
        _, vjp = jax.vjp(functools.partial(_delta_chunk, rev, True), *[t for head in ins for t in head])
        res = vjp(cts)
        for h, sl in enumerate(sls):
            dq, dk, dv, db, dgc, dgt, ds = res[7 * h:7 * h + 7]
            dst[h] = ds
            if has_acc:
                dq, dk, dv = dq + accv[h][0], dk + accv[h][1], dv + accv[h][2]
            dq_ref[:, sl] = dq
            dk_ref[:, sl] = dk
            dv_ref[:, sl] = dv
            db_ref[:, sl] = db
            dgc_ref[:, sl] = dgc
            dgt_ref[:, sl] = dgt

    blk = pl.BlockSpec((C, H * hd), lambda i: (cix(i), 0))
    in_specs = [blk, blk, pl.BlockSpec((C, H * hd), lambda i: (cix(i), vcb)), blk, blk, blk,
                pl.BlockSpec((1, H, hd, hd), lambda i: (cix(i), 0, 0, 0)), blk]
    args = [q, k, v, *gates, ssave, do]
    if has_acc:
        in_specs += [blk, blk, blk]
        args += list(accs)
    return pl.pallas_call(
        body, name=name, grid=(nc,), in_specs=in_specs, out_specs=[blk] * 6,
        out_shape=[jax.ShapeDtypeStruct((L, H * hd), F32)] * 6,
        scratch_shapes=[pltpu.VMEM((H, hd, hd), F32)],
        compiler_params=_cparams(("arbitrary",)),
    )(*args)


def _conv_specs(tm, w, cb0, nrb, L):
    hb = tm // 8
    last8 = L // 8 - 1
    cur = pl.BlockSpec((tm, w), lambda s, i: (i, cb0 + s))
    prev = pl.BlockSpec((8, w), lambda s, i: (jnp.maximum(i * hb - 1, 0), cb0 + s))
    nxt = pl.BlockSpec((8, w), lambda s, i: (jnp.minimum((i + 1) * hb, last8), cb0 + s))
    return [prev, cur, nxt]


def _fill_halo(dst, prev_ref, cur_ref, next_ref, i, nrb, tm):
    dst[pl.ds(0, 8), :] = jnp.where(i > 0, prev_ref[...], 0.0)
    dst[pl.ds(8, tm), :] = cur_ref[...]
    dst[pl.ds(8 + tm, 8), :] = jnp.where(i < nrb - 1, next_ref[...], 0.0)


def _conv_fwd(hsrc, cb0, wt, *, tm, name, d):
    L, w, K = hsrc.shape[0], d['DW'], d['CONV']
    tm = _pick(tm, L)
    nrb = L // tm

    def body(prev_ref, cur_ref, next_ref, w_ref, o_ref, xs):
        i = pl.program_id(1)
        _fill_halo(xs, prev_ref, cur_ref, next_ref, i, nrb, tm)
        y = jnp.zeros((tm, w), F32)
        for kk in range(K):
            y = y + w_ref[0, pl.ds(kk, 1), :] * xs[pl.ds(8 - K // 2 + kk, tm), :]
        o_ref[...] = _silu(y)

    return pl.pallas_call(
        body, name=name, grid=(3, nrb),
        in_specs=_conv_specs(tm, w, cb0, nrb, L) + [pl.BlockSpec((1, 8, w), lambda s, i: (s, 0, 0))],
        out_specs=pl.BlockSpec((tm, w), lambda s, i: (i, s)),
        out_shape=jax.ShapeDtypeStruct((L, 3 * w), F32),
        scratch_shapes=[pltpu.VMEM((tm + 16, w), F32)],
        compiler_params=_cparams(("parallel", "parallel")),
    )(hsrc, hsrc, hsrc, wt)


def _conv_bwd(hsrc, cb0, wt, dact, *, tm, name, d):
    L, w, K = hsrc.shape[0], d['DW'], d['CONV']
    tm = _pick(tm, L)
    nrb = L // tm
    half = K // 2

    def body(xp_ref, xc_ref, xn_ref, gp_ref, gc_ref, gn_ref, w_ref, dx_ref, dw_ref, xs, gs, dys):
        i = pl.program_id(1)
        _fill_halo(xs, xp_ref, xc_ref, xn_ref, i, nrb, tm)
        _fill_halo(gs, gp_ref, gc_ref, gn_ref, i, nrb, tm)
        y = jnp.zeros((tm + 8, w), F32)
        for kk in range(K):
            y = y + w_ref[0, pl.ds(kk, 1), :] * xs[pl.ds(4 - half + kk, tm + 8), :]
        sg = jax.nn.sigmoid(y)
        dys[...] = gs[pl.ds(4, tm + 8), :] * (sg * (1.0 + y * (1.0 - sg)))
        dx = jnp.zeros((tm, w), F32)
        for kk in range(K):
            dx = dx + w_ref[0, pl.ds(kk, 1), :] * dys[pl.ds(4 + half - kk, tm), :]
        dx_ref[...] = dx

        @pl.when(i == 0)
        def _():
            dw_ref[...] = jnp.zeros_like(dw_ref)

        dy = dys[pl.ds(4, tm), :]
        for kk in range(K):
            dw_ref[0, pl.ds(kk, 1), :] += jnp.sum(dy * xs[pl.ds(8 - half + kk, tm), :], axis=0, keepdims=True)

    gspecs = _conv_specs(tm, w, 0, nrb, L)
    return pl.pallas_call(
        body, name=name, grid=(3, nrb),
        in_specs=_conv_specs(tm, w, cb0, nrb, L) + gspecs + [pl.BlockSpec((1, 8, w), lambda s, i: (s, 0, 0))],
        out_specs=[pl.BlockSpec((tm, w), lambda s, i: (i, s)), pl.BlockSpec((1, 8, w), lambda s, i: (s, 0, 0))],
        out_shape=[jax.ShapeDtypeStruct((L, 3 * w), F32), jax.ShapeDtypeStruct((3, 8, w), F32)],
        scratch_shapes=[pltpu.VMEM((tm + 16, w), F32), pltpu.VMEM((tm + 16, w), F32), pltpu.VMEM((tm + 8, w), F32)],
        compiler_params=_cparams(("parallel", "arbitrary")),
    )(hsrc, hsrc, hsrc, dact, dact, dact, wt)


def _wide(v, n):
    return v if n == LANE else jnp.tile(v, (1, n // LANE))


def _attn_fwd(qh, kh, vh, *, tq, tk, name, d):
    L, H, KVH, hd = qh.shape[0], d['AH'], d['AKV'], d['AD']
    grp = H // KVH
    tq, tk = _pick(tq, L), _pick(tk, L)
    nk = L // tk

    def body(q_ref, k_ref, v_ref, o_ref, lse_ref, m_s, l_s, acc):
        j = pl.program_id(2)

        @pl.when(j == 0)
        def _():
            m_s[...] = jnp.full_like(m_s, -1e30)
            l_s[...] = jnp.zeros_like(l_s)
            acc[...] = jnp.zeros_like(acc)

        s = lax.dot_general(q_ref[...], k_ref[...], _NT, preferred_element_type=F32)
        m_old = m_s[...]
        m_new = jnp.maximum(m_old, jnp.max(s, axis=-1, keepdims=True))
        alpha = jnp.exp(m_old - m_new)
        p = jnp.exp(s - _wide(m_new, tk))
        l_s[...] = alpha * l_s[...] + jnp.sum(p, axis=-1, keepdims=True)
        acc[...] = alpha * acc[...] + jnp.dot(p.astype(BF16), v_ref[...], preferred_element_type=F32)
        m_s[...] = m_new

        @pl.when(j == nk - 1)
        def _():
            o_ref[...] = acc[...] / l_s[...]
            lse_ref[...] = m_s[...] + jnp.log(l_s[...])

    qspec = pl.BlockSpec((tq, hd), lambda h, i, j: (i, h))
    kspec = pl.BlockSpec((tk, hd), lambda h, i, j: (j, h // grp))
    return pl.pallas_call(
        body, name=name, grid=(H, L // tq, nk), in_specs=[qspec, kspec, kspec], out_specs=[qspec, qspec],
        out_shape=[jax.ShapeDtypeStruct((L, H * hd), F32), jax.ShapeDtypeStruct((L, H * hd), F32)],
        scratch_shapes=[pltpu.VMEM((tq, hd), F32), pltpu.VMEM((tq, hd), F32), pltpu.VMEM((tq, hd), F32)],
        compiler_params=_cparams(("parallel", "parallel", "arbitrary")),
    )(qh, kh, vh)


def _attn_bwd(qh, kh, vh, do, lse, delta, *, tq, tk, name, d):
    L, H, KVH, hd = qh.shape[0], d['AH'], d['AKV'], d['AD']
    grp = H // KVH
    tq, tk = _pick(tq, L), _pick(tk, L)
    nk = L // tk

    def body(q_ref, k_ref, v_ref, do_ref, lse_ref, dl_ref, dq_ref, dk_ref, dv_ref, dq_s):
        g, i, j = pl.program_id(1), pl.program_id(2), pl.program_id(3)

        @pl.when(jnp.logical_and(jnp.logical_and(g == 0, i == 0), j == 0))
        def _():
            dk_ref[...] = jnp.zeros_like(dk_ref)
            dv_ref[...] = jnp.zeros_like(dv_ref)

        @pl.when(j == 0)
        def _():
            dq_s[...] = jnp.zeros_like(dq_s)

        q, k, do_ = q_ref[...], k_ref[...], do_ref[...].astype(BF16)
        s = lax.dot_general(q, k, _NT, preferred_element_type=F32)
        p = jnp.exp(s - _wide(lse_ref[...], tk))
        dp = lax.dot_general(do_, v_ref[...], _NT, preferred_element_type=F32)
        ds = (p * (dp - _wide(dl_ref[...], tk))).astype(BF16)
        dq_s[...] += jnp.dot(ds, k, preferred_element_type=F32)
        rows = pl.ds(pl.multiple_of(j * tk, tk), tk)
        dv_ref[rows, :] += lax.dot_general(p.astype(BF16), do_, _TN, preferred_element_type=F32)
        dk_ref[rows, :] += lax.dot_general(ds, q, _TN, preferred_element_type=F32)

        @pl.when(j == nk - 1)
        def _():
            dq_ref[...] = dq_s[...]

    qspec = pl.BlockSpec((tq, hd), lambda kv, g, i, j: (i, kv * grp + g))
    kspec = pl.BlockSpec((tk, hd), lambda kv, g, i, j: (j, kv))
    colspec = pl.BlockSpec((L, hd), lambda kv, g, i, j: (0, kv))
    return pl.pallas_call(
        body, name=name, grid=(KVH, grp, L // tq, nk),
        in_specs=[qspec, kspec, kspec, qspec, qspec, qspec], out_specs=[qspec, colspec, colspec],
        out_shape=[jax.ShapeDtypeStruct((L, H * hd), F32)] + [jax.ShapeDtypeStruct((L, KVH * hd), F32)] * 2,
        scratch_shapes=[pltpu.VMEM((tq, hd), F32)],
        compiler_params=_cparams(("parallel", "arbitrary", "arbitrary", "arbitrary")),
    )(qh, kh, vh, do, lse, delta)


def _loss_grad(x, g, tgt, *, tm, name):
    L, D = x.shape
    tm = _pick(tm, L)

    def body(x_ref, g_ref, t_ref, loss_ref, dx_ref, dg_ref):
        def f(xv, gv):
            err = _rms(xv, gv) - t_ref[...]
            return 0.5 * jnp.sum(jnp.mean(err * err, axis=-1, keepdims=True))

        val, vjp = jax.vjp(f, x_ref[...], g_ref[...])
        dx, dg = vjp(jnp.ones((), F32))
        dx_ref[...] = dx

        @pl.when(pl.program_id(0) == 0)
        def _():
            loss_ref[...] = jnp.zeros_like(loss_ref)
            dg_ref[...] = jnp.zeros_like(dg_ref)

        loss_ref[...] += val
        dg_ref[...] += dg

    return pl.pallas_call(
        body, name=name, grid=(L // tm,),
        in_specs=[pl.BlockSpec((tm, D), lambda i: (i, 0)), pl.BlockSpec((1, D), lambda i: (0, 0)),
                  pl.BlockSpec((tm, D), lambda i: (i, 0))],
        out_specs=[pl.BlockSpec((8, LANE), lambda i: (0, 0)), pl.BlockSpec((tm, D), lambda i: (i, 0)),
                   pl.BlockSpec((1, D), lambda i: (0, 0))],
        out_shape=[jax.ShapeDtypeStruct((8, LANE), F32), jax.ShapeDtypeStruct((L, D), F32),
                   jax.ShapeDtypeStruct((1, D), F32)],
        compiler_params=_cparams(("arbitrary",)),
    )(x, g, tgt)


def _sum_slots(recv, *, tr, name):
    n, R, W = recv.shape
    tr = _pick(tr, R)

    def body(r_ref, o_ref):
        s = r_ref[0].astype(F32)
        for i in range(1, n):
            s = s + r_ref[i].astype(F32)
        o_ref[...] = s

    return pl.pallas_call(
        body, name=name, grid=(R // tr,),
        in_specs=[pl.BlockSpec((n, tr, W), lambda i: (0, i, 0))], out_specs=pl.BlockSpec((tr, W), lambda i: (i, 0)),
        out_shape=jax.ShapeDtypeStruct((R, W), F32), compiler_params=_cparams(("parallel",)),
    )(recv)


def _adamw(w, g, m, v, *, tr, name):
    R, W = w.shape
    tr = _pick(tr, R)
    c1 = 1.0 - ADAM_B1 ** ADAM_STEP
    c2 = 1.0 - ADAM_B2 ** ADAM_STEP

    def body(w_ref, g_ref, m_ref, v_ref, d_ref, nm_ref, nv_ref):
        gv = g_ref[...]
        nm = ADAM_B1 * m_ref[...] + (1.0 - ADAM_B1) * gv
        nv = ADAM_B2 * v_ref[...] + (1.0 - ADAM_B2) * (gv * gv)
        d_ref[...] = -ADAM_LR * ((nm / c1) / (jnp.sqrt(nv / c2) + ADAM_EPS) + ADAM_WD * w_ref[...])
        nm_ref[...] = nm
        nv_ref[...] = nv

    spec = pl.BlockSpec((tr, W), lambda i: (i, 0))
    return pl.pallas_call(
        body, name=name, grid=(R // tr,), in_specs=[spec] * 4, out_specs=[spec] * 3,
        out_shape=[jax.ShapeDtypeStruct((R, W), F32)] * 3, compiler_params=_cparams(("parallel",)),
    )(w, g, m, v)


def _sum_adamw(recv, w, m, v, *, tr, name):
    n, R, W = recv.shape
    tr = _pick(tr, R)
    c1 = 1.0 - ADAM_B1 ** ADAM_STEP
    c2 = 1.0 - ADAM_B2 ** ADAM_STEP

    def body(r_ref, w_ref, m_ref, v_ref, g_ref, d_ref, nm_ref, nv_ref):
        gv = r_ref[0].astype(F32)
        for i in range(1, n):
            gv = gv + r_ref[i].astype(F32)
        nm = ADAM_B1 * m_ref[...] + (1.0 - ADAM_B1) * gv
        nv = ADAM_B2 * v_ref[...] + (1.0 - ADAM_B2) * (gv * gv)
        g_ref[...] = gv
        d_ref[...] = -ADAM_LR * ((nm / c1) / (jnp.sqrt(nv / c2) + ADAM_EPS) + ADAM_WD * w_ref[...])
        nm_ref[...] = nm
        nv_ref[...] = nv

    spec = pl.BlockSpec((tr, W), lambda i: (i, 0))
    return pl.pallas_call(
        body, name=name, grid=(R // tr,),
        in_specs=[pl.BlockSpec((n, tr, W), lambda i: (0, i, 0))] + [spec] * 3, out_specs=[spec] * 4,
        out_shape=[jax.ShapeDtypeStruct((R, W), F32)] * 4, compiler_params=_cparams(("parallel",)),
    )(recv, w, m, v)


_MESH = pl.DeviceIdType.MESH


def _all_gather(xs, *, name):
    na = len(xs)

    def body(*refs):
        x_refs, out_refs = refs[:na], refs[na:2 * na]
        send_sems, recv_sems, local_sems = refs[2 * na:]
        x, y, c = lax.axis_index("x"), lax.axis_index("y"), lax.axis_index("c")
        me, sibling = (x, y, c), (x, y, 1 - c)
        chips = [(1 - x, y), (x, 1 - y), (1 - x, 1 - y)]

        def slot(a, px, py, pc):
            return out_refs[a].at[4 * px + 2 * py + pc]

        def copy(a, k, block, to, src=None):
            return pltpu.make_async_remote_copy(
                src_ref=slot(a, *block) if src is None else src, dst_ref=slot(a, *block),
                send_sem=send_sems.at[7 * a + k], recv_sem=recv_sems.at[7 * a + k], device_id=to,
                device_id_type=_MESH)

        mine = [pltpu.make_async_copy(x_refs[a], slot(a, *me), local_sems.at[a]) for a in range(na)]
        for cp in mine:
            cp.start()
        first = []
        for a in range(na):
            first.append(copy(a, 0, me, sibling, src=x_refs[a]))
            first += [copy(a, 1 + j, me, (*chip, c), src=x_refs[a]) for j, chip in enumerate(chips)]
        for cp in first:
            cp.start()
        passed = []
        for a in range(na):
            for j, chip in enumerate(chips):
                copy(a, 1 + j, (*chip, c), me).wait_recv()
                passed.append(copy(a, 4 + j, (*chip, c), sibling))
                passed[-1].start()
        for a in range(na):
            copy(a, 0, sibling, me).wait_recv()
            for j, chip in enumerate(chips):
                copy(a, 4 + j, (*chip, 1 - c), me).wait_recv()
        for cp in first + passed:
            cp.wait_send()
        for cp in mine:
            cp.wait()

    return pl.pallas_call(
        body, name=name,
        out_shape=[jax.ShapeDtypeStruct((N_DEV,) + t.shape, t.dtype) for t in xs],
        in_specs=[pl.BlockSpec(memory_space=pl.ANY)] * na, out_specs=[pl.BlockSpec(memory_space=pl.ANY)] * na,
        scratch_shapes=[pltpu.SemaphoreType.DMA((7 * na,)), pltpu.SemaphoreType.DMA((7 * na,)),
                        pltpu.SemaphoreType.DMA((na,))],
    )(*xs)


N_CHIP = N_DEV // 2


def _pair_exchange(gs, *, name):
    na = len(gs)

    def body(*refs):
        g_refs, out_refs = refs[:na], refs[na:2 * na]
        send_sems, recv_sems = refs[2 * na:]
        x, y, c = lax.axis_index("x"), lax.axis_index("y"), lax.axis_index("c")

        def copy(a, chip, core):
            return pltpu.make_async_remote_copy(
                src_ref=g_refs[a].at[2 * chip + core], dst_ref=out_refs[a].at[chip],
                send_sem=send_sems.at[N_CHIP * a + chip], recv_sem=recv_sems.at[N_CHIP * a + chip],
                device_id=(x, y, 1 - c), device_id_type=_MESH)

        sends = [copy(a, chip, 1 - c) for a in range(na) for chip in range(N_CHIP)]
        for cp in sends:
            cp.start()
        for cp in sends:
            cp.wait_recv()
        for cp in sends:
            cp.wait_send()

    return pl.pallas_call(
        body, name=name,
        out_shape=[jax.ShapeDtypeStruct((N_CHIP,) + t.shape[1:], t.dtype) for t in gs],
        in_specs=[pl.BlockSpec(memory_space=pl.ANY)] * na, out_specs=[pl.BlockSpec(memory_space=pl.ANY)] * na,
        scratch_shapes=[pltpu.SemaphoreType.DMA((N_CHIP * na,)), pltpu.SemaphoreType.DMA((N_CHIP * na,))],
    )(*gs)


def _pair_sum(g, r, core, *, out_dtype, tr, name):
    _, R, W = g.shape
    tr = _pick(tr, R)

    def body(core_ref, g_ref, r_ref, o_ref):
        o_ref[...] = (g_ref[...] + r_ref[...]).astype(o_ref.dtype)

    return pl.pallas_call(
        body, name=name,
        grid_spec=pltpu.PrefetchScalarGridSpec(
            num_scalar_prefetch=1, grid=(N_CHIP, R // tr),
            in_specs=[pl.BlockSpec((1, tr, W), lambda ch, i, core_ref: (2 * ch + core_ref[0], i, 0)),
                      pl.BlockSpec((1, tr, W), lambda ch, i, core_ref: (ch, i, 0))],
            out_specs=pl.BlockSpec((1, tr, W), lambda ch, i, core_ref: (ch, i, 0))),
        out_shape=jax.ShapeDtypeStruct((N_CHIP, R, W), out_dtype),
        compiler_params=_cparams(("parallel", "parallel")),
    )(core, g, r)


def _chip_exchange(hs, *, name):
    na = len(hs)

    def body(*refs):
        h_refs, out_refs = refs[:na], refs[na:2 * na]
        send_sems, recv_sems, local_sems = refs[2 * na:]
        x, y, c = lax.axis_index("x"), lax.axis_index("y"), lax.axis_index("c")
        me = 2 * x + y

        def copy(a, mask, started):
            px, py = x ^ (mask >> 1), y ^ (mask & 1)
            mine_, theirs = me, 2 * px + py
            return pltpu.make_async_remote_copy(
                src_ref=h_refs[a].at[theirs if started else mine_],
                dst_ref=out_refs[a].at[mine_ if started else theirs],
                send_sem=send_sems.at[3 * a + mask - 1], recv_sem=recv_sems.at[3 * a + mask - 1],
                device_id=(px, py, c), device_id_type=_MESH)

        mine = [pltpu.make_async_copy(h_refs[a].at[me], out_refs[a].at[me], local_sems.at[a]) for a in range(na)]
        for cp in mine:
            cp.start()
        sends = [copy(a, mask, True) for a in range(na) for mask in range(1, N_CHIP)]
        for cp in sends:
            cp.start()
        for a in range(na):
            for mask in range(1, N_CHIP):
                copy(a, mask, False).wait_recv()
        for cp in sends:
            cp.wait_send()
        for cp in mine:
            cp.wait()

    return pl.pallas_call(
        body, name=name,
        out_shape=[jax.ShapeDtypeStruct(t.shape, t.dtype) for t in hs],
        in_specs=[pl.BlockSpec(memory_space=pl.ANY)] * na, out_specs=[pl.BlockSpec(memory_space=pl.ANY)] * na,
        scratch_shapes=[pltpu.SemaphoreType.DMA((3 * na,)), pltpu.SemaphoreType.DMA((3 * na,)),
                        pltpu.SemaphoreType.DMA((na,))],
    )(*hs)


def _rows_of(shape):
    return -(-int(np.prod(shape)) // PACK_W)


def _pack(arrs, dtype, lead=0, total_rows=None):
    pieces = []
    for a in arrs:
        f = a.reshape(a.shape[:lead] + (-1,)).astype(dtype)
        pad = (-f.shape[-1]) % PACK_W
        if pad:
            f = jnp.pad(f, [(0, 0)] * lead + [(0, pad)])
        pieces.append(f.reshape(a.shape[:lead] + (-1, PACK_W)))
    buf = jnp.concatenate(pieces, axis=lead)
    if total_rows is not None and buf.shape[lead] < total_rows:
        buf = jnp.pad(buf, [(0, 0)] * lead + [(0, total_rows - buf.shape[lead]), (0, 0)])
    return buf


def _unpack(buf, shapes, lead=0):
    out, r = [], 0
    for shp in shapes:
        n, rows = int(np.prod(shp)), _rows_of(shp)
        piece = buf[(slice(None),) * lead + (slice(r, r + rows),)]
        piece = piece.reshape(buf.shape[:lead] + (-1,))[..., :n]
        out.append(piece.reshape(buf.shape[:lead] + tuple(shp)))
        r += rows
    return out


def _pack_flat(arrs, total_rows):
    flat = jnp.concatenate([t.reshape(-1).astype(F32) for t in arrs])
    return jnp.pad(flat, (0, total_rows * PACK_W - flat.shape[0])).reshape(total_rows, PACK_W)


def _unpack_flat(buf, shapes):
    flat, out, off = buf.reshape(-1), [], 0
    for shp in shapes:
        n = int(np.prod(shp))
        out.append(flat[off:off + n].reshape(shp))
        off += n
    return out


def _to_full(parts):
    dep, r = parts.shape[1:3]
    return jnp.transpose(parts, (1, 0) + tuple(range(2, parts.ndim))).reshape((dep, N_DEV * r) + parts.shape[3:])


def _to_slabs(full):
    dep, r = full.shape[:2]
    t = full.reshape((dep, N_DEV, r // N_DEV) + full.shape[2:])
    return jnp.transpose(t, (1, 0) + tuple(range(2, t.ndim)))


def _ref_cols(parts, ro, wd):
    w, out = parts.shape[2], []
    for dev in range(N_DEV):
        lo, hi = max(ro, dev * w), min(ro + wd, (dev + 1) * w)
        if lo < hi:
            out.append(parts[dev][:, lo - dev * w:hi - dev * w])
    return out


def _w_in_to_layout(parts, seg, rseg, nh2):
    D = parts.shape[1]
    cols, off = [], 0
    names = sorted([k for k in seg if not k.startswith('_')], key=lambda k: seg[k][0])
    for nm in names:
        o, wd = seg[nm]
        if o > off:
            cols.append(jnp.zeros((D, o - off), parts.dtype))
        if nm == 'dadb':
            cols += _ref_cols(parts, rseg['da'][0], nh2) + _ref_cols(parts, rseg['db'][0], nh2)
            cols.append(jnp.zeros((D, wd - 2 * nh2), parts.dtype))
        else:
            cols += _ref_cols(parts, rseg[nm][0], wd)
        off = o + wd
    if seg['_total'] > off:
        cols.append(jnp.zeros((D, seg['_total'] - off), parts.dtype))
    return jnp.concatenate(cols, axis=1)


def _w_in_slabs(dw, seg, rseg, nh2):
    w = rseg['_total'] // N_DEV
    ref = []
    for nm in sorted([k for k in rseg if not k.startswith('_')], key=lambda k: rseg[k][0]):
        lo = {'da': seg['dadb'][0], 'db': seg['dadb'][0] + nh2}.get(nm)
        ref.append((rseg[nm][0], rseg[nm][1], seg[nm][0] if lo is None else lo))
    slabs = []
    for dev in range(N_DEV):
        cols = []
        for ro, wd, lo in ref:
            a, b = max(ro, dev * w), min(ro + wd, (dev + 1) * w)
            if a < b:
                cols.append(dw[:, lo + a - ro:lo + b - ro])
        slabs.append(jnp.concatenate(cols, axis=1))
    return jnp.stack(slabs, axis=0)


def _assemble_dh(pieces, seg, L):
    cols, off = [], 0
    for nm in sorted(pieces, key=lambda k: seg[k][0]):
        o = seg[nm][0]
        if o > off:
            cols.append(jnp.zeros((L, o - off), F32))
        cols.append(pieces[nm])
        off = o + pieces[nm].shape[1]
    if seg['_total'] > off:
        cols.append(jnp.zeros((L, seg['_total'] - off), F32))
    return jnp.concatenate(cols, axis=1)


def _lane_pad(v):
    v = v.reshape(1, -1)
    return jnp.pad(v, ((0, 0), (0, LANE - v.shape[1])))


def _rope_tables(L, c):
    rows = L // c['GRID_W']
    row = jnp.repeat(jnp.arange(rows), c['GRID_W']).astype(F32)
    col = jnp.tile(jnp.arange(c['GRID_W']), rows).astype(F32)
    axis_dim = c['AD'] // 2
    freqs = c['ROPE_THETA'] ** (-jnp.arange(0, axis_dim, 2, dtype=F32) / axis_dim)
    ang = jnp.concatenate([row[:, None] * freqs, col[:, None] * freqs], axis=-1)
    cosf = jnp.repeat(jnp.cos(ang), 2, axis=1)
    sn = jnp.sin(ang)
    sins = jnp.stack([-sn, sn], axis=-1).reshape(L, c['AD'])
    idx = np.arange(c['AD'])
    perm = np.zeros((c['AD'], c['AD']), np.float32)
    perm[idx, idx ^ 1] = 1.0
    return cosf, sins, jnp.asarray(perm)


def _s5_dir_params(a, l, dr):
    return (a['ssm_a_re'][l, dr], a['ssm_a_im'][l, dr], a['ssm_log_step'][l, dr], a['ssm_b_re'][l, dr],
            a['ssm_b_im'][l, dr], a['ssm_c_re'][l, dr], a['ssm_c_im'][l, dr])


def _layer_fwd(x, mem, l, wt, a, rope, c, d, seg):
    L, D = x.shape
    SW, DW, AW, AKW, MW, H = d['SW'], d['DW'], d['AW'], d['AKW'], d['MW'], d['DNH']
    cb = lambda nm: seg[nm][0] // seg[nm][1]
    sv = {'x': x}
    p = f"l{l}_"
    sv['g_norm'] = a['norm_g'][l][None, :]
    xn, = _rowwise(_f_norm, [(x, D, 0)], [sv['g_norm']], [(D, BF16)], tm=256, name=p + "norm")
    h = _mm(xn, wt['wp'], name=p + "in_proj", tm=1024, tn=1536, tk=2048)
    sv['xn'], sv['h'] = xn, h

    ysum, sv['s5'] = None, []
    for dr in range(2):
        wb, wc, lr, li = _s5_prep(*_s5_dir_params(a, l, dr), d)
        wb16, wc16 = wb.astype(BF16), wc.astype(BF16)
        lt = _s5_tables(lr, li, bool(dr), False)
        ysum, cin = _s5_fwd(h, cb('u_a'), wb16, wc16, lt, rev=bool(dr), acc=ysum, tb=TILES['s5_t'],
                            name=p + f"s5_fwd{dr}", d=d)
        sv['s5'].append((wb16, wc16, lt, _s5_tables(lr, li, not bool(dr), True), cin))
    sv['ysum'] = ysum
    sv['s5_par'] = [a['ssm_d'][l][None, :], wt['w_glu'], a['ssm_b_glu'][l][None, :]]
    sv['s5_rows'] = [(ysum, SW, 0), (h, SW, cb('u_a')), (h, SW, cb('z_a'))]
    y_a, = _rowwise(_f_s5tail, sv['s5_rows'], sv['s5_par'], [(SW, F32)], tm=256, name=p + "s5_tail")

    act = _conv_fwd(h, cb('dq'), wt['conv'], tm=256, name=p + "dn_conv", d=d)
    sv['act'] = act
    sv['dn_par'] = [_lane_pad(a['dn_a_log'][l]), _lane_pad(a['dn_dt_bias'][l])]
    sv['dn_rows'] = [(act, DW, 0), (act, DW, 1), (h, LANE, seg['dadb'][0] // LANE)]
    dn_out = _rowwise(_make_f_dnpre(H, d['DNK'], c['CHUNK']), sv['dn_rows'], sv['dn_par'], [(DW, F32)] * 8,
                      tm=256, name=p + "dn_pre")
    qn, kn = dn_out[:2]
    sv['qn'], sv['kn'], sv['gates'] = qn, kn, [dn_out[2:5], dn_out[5:8]]
    o_dn, sv['dn_state'] = None, []
    for dr in range(2):
        o_dn, ss = _delta_fwd(qn, kn, act, sv['gates'][dr], vcb=2, rev=bool(dr), acc=o_dn,
                              name=p + f"dn_fwd{dr}", d=d)
        sv['dn_state'].append(ss)
    sv['dnpost_rows'] = [(o_dn, DW, 0), (h, DW, cb('z_b'))]
    sv['dnpost_par'] = [a['dn_norm_g'][l][None, :]]
    y_b, = _rowwise(_make_f_dnpost(d['DNK']), sv['dnpost_rows'], sv['dnpost_par'], [(DW, F32)], tm=256,
                    name=p + "dn_post")

    cosf, sins, perm = rope
    sv['att_par'] = [perm, a['attn_q_norm'][l][None, :], a['attn_k_norm'][l][None, :]]
    qh, kh, vh = _rowwise(_make_f_attpre(d['AD'], True),
                          [(h, AW, cb('aq')), (h, AKW, cb('ak')), (h, AKW, cb('av')), (cosf, d['AD'], 0),
                           (sins, d['AD'], 0)], sv['att_par'], [(AW, BF16), (AKW, BF16), (AKW, BF16)],
                          tm=256, name=p + "att_pre")
    o_att, lse = _attn_fwd(qh, kh, vh, tq=TILES['att_q'], tk=TILES['att_k'], name=p + "att_fwd", d=d)
    sv['qh'], sv['kh'], sv['vh'], sv['o_att'], sv['lse'] = qh, kh, vh, o_att, lse
    y_c, = _rowwise(_f_gate, [(o_att, AW, 0), (h, AW, cb('z_c'))], [], [(AW, F32)], tm=256, name=p + "att_post")

    sv['g_mem'] = a['mem_norm_g'][l][None, :]
    memn, = _rowwise(_f_norm, [(mem, D, 0)], [sv['g_mem']], [(D, BF16)], tm=256, name=p + "mem_norm")
    kv = _mm(memn, wt['w_mem_kv'], name=p + "mem_kv")
    sv['memn'], sv['kv'] = memn, kv
    y_m, = _rowwise(_make_f_mem(d['MH'], d['MD']), [(h, MW, cb('mq')), (h, MW, cb('z_m'))], [kv], [(MW, F32)],
                    tm=256, name=p + "mem_attn")

    ys = [y_a, y_b, y_c, y_m]
    ps = [_mm(y, wb_, name=p + f"branch_proj{i}", out_dtype=BF16)
          for i, (y, wb_) in enumerate(zip(ys, wt['w_branch']))]
    gcb = seg['gates'][0] // D
    sv['merge_rows'] = [(pp, D, 0) for pp in ps] + [(h, D, gcb + i) for i in range(4)]
    merged, = _rowwise(_f_merge, sv['merge_rows'], [], [(D, BF16)], tm=128, name=p + "merge")
    sv['ys'], sv['merged'] = ys, merged
    return _mm(merged, wt['w_out'], add=x, name=p + "out_proj"), sv


def _layer_bwd(dx, mem, l, wt, a, rope, sv, c, d, seg):
    L, D = dx.shape
    SW, DW, AW, AKW, MW, H = d['SW'], d['DW'], d['AW'], d['AKW'], d['MW'], d['DNH']
    cb = lambda nm: seg[nm][0] // seg[nm][1]
    p = f"l{l}_"
    h = sv['h']
    gr = {}
    dmerged = _mm(dx, wt['w_out'], tb=True, name=p + "d_merged")
    gr['w_out'] = _mm(sv['merged'], dx, ta=True, name=p + "dw_out")
    dmr, _ = _rowwise_bwd(_f_merge, sv['merge_rows'], [], [[(dmerged, D, 0)]], [True] * 8, [], tm=128,
                          name=p + "merge_bwd", row_grad_dtype=BF16)
    dps, dgates = dmr[:4], dmr[4:]
    dys = [_mm(dp, wb_, tb=True, name=p + f"d_branch{i}") for i, (dp, wb_) in enumerate(zip(dps, wt['w_branch']))]
    gr['w_branch'] = jnp.concatenate(
        [_mm(y, dp, ta=True, name=p + f"dw_branch{i}") for i, (y, dp) in enumerate(zip(sv['ys'], dps))], axis=0)

    (dmq, dzm), (dkv,) = _rowwise_bwd(_make_f_mem(d['MH'], d['MD']), [(h, MW, cb('mq')), (h, MW, cb('z_m'))],
                                      [sv['kv']], [[(dys[3], MW, 0)]], [True, True], [True], tm=256,
                                      name=p + "mem_attn_bwd")
    gr['w_mem_kv'] = _mm(sv['memn'], dkv, ta=True, name=p + "dw_mem_kv")
    dmemn = _mm(dkv, wt['w_mem_kv'], tb=True, name=p + "d_memn")
    _, (dg_mem,) = _rowwise_bwd(_f_norm, [(mem, D, 0)], [sv['g_mem']], [[(dmemn, D, 0)]], [False], [True], tm=256,
                                name=p + "mem_norm_bwd")
    gr['mem_norm_g'] = dg_mem[0]

    (do_att, dzc), _ = _rowwise_bwd(_f_gate, [(sv['o_att'], AW, 0), (h, AW, cb('z_c'))], [], [[(dys[2], AW, 0)]],
                                    [True, True], [], tm=256, name=p + "att_post_bwd")
    delta, = _rowwise(_make_f_delta(d['AD']), [(do_att, AW, 0), (sv['o_att'], AW, 0)], [], [(AW, F32)], tm=256,
                      name=p + "att_delta")
    att_in = (sv['qh'], sv['kh'], sv['vh'], do_att, sv['lse'], delta)
    dqh, dkh, dvh = _attn_bwd(*att_in, tq=TILES['att_q'], tk=TILES['att_k'], name=p + "att_bwd", d=d)
    cosf, sins, _ = rope
    (daq, dak), (dqg, dkg) = _rowwise_bwd(
        _make_f_attpre(d['AD'], False),
        [(h, AW, cb('aq')), (h, AKW, cb('ak')), (cosf, d['AD'], 0), (sins, d['AD'], 0)], sv['att_par'],
        [[(dqh, AW, 0)], [(dkh, AKW, 0)]], [True, True, False, False], [False, True, True], tm=256,
        name=p + "att_pre_bwd")
    gr['attn_q_norm'], gr['attn_k_norm'] = dqg[0], dkg[0]

    (do_dn, dzb), (dng,) = _rowwise_bwd(_make_f_dnpost(d['DNK']), sv['dnpost_rows'], sv['dnpost_par'],
                                        [[(dys[1], DW, 0)]], [True, True], [True], tm=256, name=p + "dn_post_bwd")
    gr['dn_norm_g'] = dng[0]
    accs, dn_dgates = None, []
    for dr in range(2):
        res = _delta_bwd(sv['qn'], sv['kn'], sv['act'], sv['gates'][dr], sv['dn_state'][dr], do_dn, vcb=2,
                         rev=bool(dr), accs=accs, name=p + f"dn_bwd{dr}", d=d)
        accs = res[:3]
        dn_dgates += res[3:]
    dqn, dkn, dvc = accs
    (dqc, dkc, ddadb), (dalog, ddtb) = _rowwise_bwd(
        _make_f_dnpre(H, d['DNK'], c['CHUNK']), sv['dn_rows'], sv['dn_par'],
        [[(t, DW, 0)] for t in [dqn, dkn] + dn_dgates], [True] * 3, [True, True], tm=256, name=p + "dn_pre_bwd")
    gr['dn_a_log'] = dalog[0, :2 * H].reshape(2, H)
    gr['dn_dt_bias'] = ddtb[0, :2 * H].reshape(2, H)
    dconv_x, dconv_w = _conv_bwd(h, cb('dq'), wt['conv'], jnp.concatenate([dqc, dkc, dvc], axis=1), tm=256,
                                 name=p + "dn_conv_bwd", d=d)
    gr['dn_conv'] = jnp.transpose(dconv_w[:, :c['CONV'], :], (0, 2, 1)).reshape(3 * DW, c['CONV'])

    (dysum, du, dza), (dd, dwglu, dbglu) = _rowwise_bwd(_f_s5tail, sv['s5_rows'], sv['s5_par'], [[(dys[0], SW, 0)]],
                                                        [True] * 3, [True] * 3, tm=256, name=p + "s5_tail_bwd")
    gr['ssm_d'], gr['ssm_w_glu'], gr['ssm_b_glu'] = dd[0], dwglu, dbglu[0]
    s5g = []
    for dr in range(2):
        wb16, wc16, lt, lt_adj, cin = sv['s5'][dr]
        du, dwb, dwc, dlam = _s5_bwd(h, cb('u_a'), dysum, cin, wb16, wc16, lt, lt_adj, rev=bool(dr), acc=du,
                                     tb=TILES['s5_t'],
                                     name=p + f"s5_bwd{dr}", d=d)
        dl = jnp.sum(dlam, axis=0).reshape(d['NB'], 2, d['BS'])
        _, prep_vjp = jax.vjp(lambda *pp: _s5_prep(*pp, d), *_s5_dir_params(a, l, dr))
        s5g.append(prep_vjp((dwb, dwc, dl[:, 0], dl[:, 1])))
    for i, nm in enumerate(['ssm_a_re', 'ssm_a_im', 'ssm_log_step', 'ssm_b_re', 'ssm_b_im', 'ssm_c_re', 'ssm_c_im']):
        gr[nm] = jnp.stack([s5g[0][i], s5g[1][i]], axis=0)

    dh = _assemble_dh({'u_a': du, 'z_a': dza, 'dq': dconv_x, 'z_b': dzb, 'ak': dak, 'av': dvh, 'aq': daq,
                       'z_c': dzc, 'mq': dmq, 'z_m': dzm, 'gates': jnp.concatenate(dgates, axis=1),
                       'dadb': ddadb}, seg, L).astype(BF16)
    gr['wp'] = _mm(sv['xn'], dh, ta=True, name=p + "dw_in", tm=1024, tn=1536, tk=2048)
    dxn = _mm(dh, wt['wp'], tb=True, name=p + "d_xn", tm=1024, tn=1024, tk=1536)
    (dx_in,), (dg_norm,) = _rowwise_bwd(_f_norm, [(sv['x'], D, 0)], [sv['g_norm']], [[(dxn, D, 0)]], [True], [True],
                                        tm=256, name=p + "norm_bwd", accs={0: (dx, D, 0)})
    gr['norm_g'] = dg_norm[0]
    return dx_in, gr


_ARG_NAMES = (['x', 'mem'] + WEIGHTS + ['loss_target'] + ['m_' + w for w in WEIGHTS] + ['v_' + w for w in WEIGHTS])


def kernel(x, mem, norm_g, w_in, ssm_a_re, ssm_a_im, ssm_log_step, ssm_b_re, ssm_b_im, ssm_c_re, ssm_c_im,
           ssm_d, ssm_w_glu, ssm_b_glu, dn_conv, dn_a_log, dn_dt_bias, dn_norm_g, attn_q_norm, attn_k_norm,
           mem_norm_g, w_mem_kv, w_branch, w_out, final_norm_g, loss_target, m_norm_g, m_w_in, m_ssm_a_re,
           m_ssm_a_im, m_ssm_log_step, m_ssm_b_re, m_ssm_b_im, m_ssm_c_re, m_ssm_c_im, m_ssm_d, m_ssm_w_glu,
           m_ssm_b_glu, m_dn_conv, m_dn_a_log, m_dn_dt_bias, m_dn_norm_g, m_attn_q_norm, m_attn_k_norm,
           m_mem_norm_g, m_w_mem_kv, m_w_branch, m_w_out, m_final_norm_g, v_norm_g, v_w_in, v_ssm_a_re,
           v_ssm_a_im, v_ssm_log_step, v_ssm_b_re, v_ssm_b_im, v_ssm_c_re, v_ssm_c_im, v_ssm_d, v_ssm_w_glu,
           v_ssm_b_glu, v_dn_conv, v_dn_a_log, v_dn_dt_bias, v_dn_norm_g, v_attn_q_norm, v_attn_k_norm,
           v_mem_norm_g, v_w_mem_kv, v_w_branch, v_w_out, v_final_norm_g):
    given = locals()
    return _train_step({n: given[n] for n in _ARG_NAMES})


def _train_step(a):
    c = CFG
    d = _dims(c)
    seg, rseg = _layout(c)
    depth, nh2 = c['DEPTH'], 2 * c['DNH']
    x, mem, tgt = a['x'][0], a['mem'][0], a['loss_target'][0]
    L, D = x.shape

    packed = [n for n in SHARDED if n != 'w_in']
    shard_shapes = [a[n].shape for n in packed]
    rw = _round_up(sum(_rows_of(s) for s in shard_shapes), LANE)
    win_shape = a['w_in'].shape
    wcols = win_shape[2]
    g_win, gathered = _all_gather([a['w_in'].astype(BF16).reshape(depth * D, wcols),
                                   _pack([a[n] for n in packed], BF16, total_rows=rw)], name="weights_all_gather")
    full = {n: _to_full(p_) for n, p_ in zip(packed, _unpack(gathered, shard_shapes, lead=1))}
    offs = np.cumsum([0, d['SW'], d['DW'], d['AW'], d['MW']])
    wts = []
    for l in range(depth):
        conv = jnp.transpose(full['dn_conv'][l].astype(F32).reshape(3, d['DW'], c['CONV']), (0, 2, 1))
        wts.append(dict(
            wp=_w_in_to_layout(g_win[:, l * D:(l + 1) * D], seg, rseg, nh2),
            w_branch=[full['w_branch'][l, offs[i]:offs[i + 1]] for i in range(4)],
            w_out=full['w_out'][l], w_mem_kv=full['w_mem_kv'][l], w_glu=full['ssm_w_glu'][l].astype(F32),
            conv=jnp.pad(conv, ((0, 0), (0, 8 - c['CONV']), (0, 0)))))
    rope = _rope_tables(L, c)

    saved = []
    for l in range(depth):
        x, sv = _layer_fwd(x, mem, l, wts[l], a, rope, c, d, seg)
        saved.append(sv)
    loss_part, dx, dg_final = _loss_grad(x, a['final_norm_g'][None, :], tgt, tm=256, name="final_norm_loss")
    grads = [None] * depth
    for l in reversed(range(depth)):
        dx, grads[l] = _layer_bwd(dx, mem, l, wts[l], a, rope, saved[l], c, d, seg)

    gfull = {n: jnp.stack([grads[l][n] for l in range(depth)], axis=0) for n in WEIGHTS
             if n not in ('w_in', 'final_norm_g')}
    gfull['final_norm_g'] = dg_final[0]

    win_slabs = jnp.concatenate([_w_in_slabs(grads[l]['wp'], seg, rseg, nh2) for l in range(depth)], axis=1)
    small_shapes = [a[n].shape for n in SMALL] + [(1,)]
    rs = _round_up(_rows_of((sum(int(np.prod(s)) for s in small_shapes),)), LANE)
    g_shard = _pack([_to_slabs(gfull[n]) for n in packed], F32, lead=1, total_rows=rw)
    g_small = _pack_flat([gfull[n] for n in SMALL] + [loss_part[0, :1]], rs)
    slabs = [win_slabs, g_shard, jnp.broadcast_to(g_small[None], (N_DEV,) + g_small.shape)]
    core = lax.axis_index("c").astype(jnp.int32).reshape(1)
    from_sibling = _pair_exchange(slabs, name="grads_pair_exchange")
    pair_sums = [_pair_sum(g, r, core, out_dtype=dt, tr=256, name=f"grads_pair_sum{i}")
                 for i, (g, r, dt) in enumerate(zip(slabs, from_sibling, (BF16, BF16, F32)))]
    recv = _chip_exchange(pair_sums, name="grads_chip_exchange")
    flat = lambda t: t.reshape(depth * D, wcols)
    win_out = [t.reshape(win_shape) for t in _sum_adamw(recv[0], flat(a['w_in']), flat(a['m_w_in']),
                                                        flat(a['v_w_in']), tr=256, name="w_in_sum_adamw")]

    zero = jnp.zeros((1,), F32)
    shard_pack = lambda prefix: _pack([a[prefix + n] for n in packed], F32, total_rows=rw)
    small_pack = lambda prefix: _pack_flat([a[prefix + n] for n in SMALL] + [zero], rs)
    shard_out = _sum_adamw(recv[1], shard_pack(''), shard_pack('m_'), shard_pack('v_'), tr=256,
                           name="shard_sum_adamw")
    small_out = _sum_adamw(recv[2], small_pack(''), small_pack('m_'), small_pack('v_'), tr=256,
                           name="small_sum_adamw")

    def split(i):
        vals = dict(zip(packed, _unpack(shard_out[i], shard_shapes)))
        small = _unpack_flat(small_out[i], small_shapes)
        vals.update(zip(SMALL, small[:-1]))
        return vals, small[-1]

    _, loss = split(0)
    outs = [loss.reshape(()), dx[None]]
    for i in range(4):
        vals, _ = split(i)
        vals['w_in'] = win_out[i]
        outs += [vals[n] for n in WEIGHTS]
    return tuple(outs)
```
